```python
import math
import jax, jax.numpy as jnp
from jax import lax
import numpy as np

D_MODEL = 1024
BATCH = 8
SEQ = 2048
DEPTH = 4

RET_HEADS = 4
RET_DK = 128
RET_DV = 256
RET_CHUNK = 128
RET_THETA = 10000.0
DIL_HEADS = 8
DIL_HD = 64
DIL_PAIRS = ((128, 1), (512, 4), (2048, 16))
DIL_BLOCK = 128
ROPE_THETA = 500000.0
ROPE_DIMS = DIL_HD // 4
GDN_HEADS = 8
GDN_DK = 128
GDN_DV = 128
GDN_CHUNK = 64
GDN_CONV = 4
D_FF = 2816
FFN_CONV = 3
DN_ALPHA = (2.0 * DEPTH) ** 0.25
DN_BETA = (8.0 * DEPTH) ** -0.25
EPS = 1e-5
N_EVEN = (DEPTH + 1) // 2
N_ODD = DEPTH // 2

RET_QK_W = RET_HEADS * RET_DK
RET_V_W = RET_HEADS * RET_DV
DIL_W = DIL_HEADS * DIL_HD
EV_IN = 2 * RET_QK_W + 2 * RET_V_W + 3 * DIL_W
EV_MIX = RET_V_W + DIL_W
GDN_W = GDN_HEADS * GDN_DK
OD_IN = 4 * GDN_W + 2 * GDN_HEADS

kernel_name = "hybrid_retention_dilated_gdn_deepnorm"

F32 = jnp.float32


def _layernorm(x, g, b):
    xf = x.astype(F32)
    mu = jnp.mean(xf, -1, keepdims=True)
    var = jnp.mean(jnp.square(xf - mu), -1, keepdims=True)
    return ((xf - mu) * lax.rsqrt(var + EPS) * g.astype(F32) + b.astype(F32)).astype(x.dtype)


def _rms(xf):
    return xf * lax.rsqrt(jnp.mean(xf * xf, -1, keepdims=True) + EPS)


def _l2norm(xf):
    return xf * lax.rsqrt(jnp.sum(xf * xf, -1, keepdims=True) + 1e-6)


def _heads(t, n_heads):
    b, s, w = t.shape
    return t.reshape(b, s, n_heads, w // n_heads).transpose(0, 2, 1, 3)


def _merge(t):
    b, h, s, d = t.shape
    return t.transpose(0, 2, 1, 3).reshape(b, s, h * d)


def _rotary(x, positions, n_rot, theta):
    half = n_rot // 2
    inv = jnp.power(theta, -jnp.arange(half, dtype=F32) * 2.0 / n_rot)
    ang = positions.astype(F32)[:, None, :, None] * inv
    cos, sin = jnp.cos(ang), jnp.sin(ang)
    xf = x.astype(F32)
    x1, x2 = xf[..., :half], xf[..., half:n_rot]
    out = jnp.concatenate([x1 * cos - x2 * sin, x2 * cos + x1 * sin, xf[..., n_rot:]], -1)
    return out.astype(x.dtype)


def _causal_dwconv(x, w):
    k_w = w.shape[0]
    s = x.shape[1]
    xp = jnp.pad(x, ((0, 0), (k_w - 1, 0), (0, 0)))
    return sum(xp[:, j:j + s] * w[j] for j in range(k_w))


def _retention(q, k, v):
    b, h, s, dk = q.shape
    dv = v.shape[-1]
    c = RET_CHUNK
    n = s // c
    lg = jnp.log1p(-jnp.power(2.0, -5.0 - jnp.arange(h, dtype=F32)))
    idx = jnp.arange(c, dtype=F32)
    rel = idx[:, None] - idx[None, :]
    dmat = jnp.where(rel >= 0, jnp.exp(jnp.maximum(rel, 0.0) * lg[:, None, None]), 0.0)
    zeta = jnp.exp((c - 1 - idx) * lg[:, None])
    xi = jnp.exp((idx + 1) * lg[:, None])
    g_chunk = jnp.exp(c * lg)
    qc = q.reshape(b, h, n, c, dk)
    kc = k.reshape(b, h, n, c, dk)
    vc = v.reshape(b, h, n, c, dv)
    scores = jnp.einsum('bhnid,bhnjd->bhnij', qc, kc) * dmat[None, :, None]
    o_intra = jnp.einsum('bhnij,bhnje->bhnie', scores, vc)
    kv = jnp.einsum('bhnjd,bhnje->bhnde', kc * zeta[None, :, None, :, None], vc)

    def step(r_state, kv_n):
        return r_state * g_chunk[None, :, None, None] + kv_n, r_state

    _, r_prev = lax.scan(step, jnp.zeros((b, h, dk, dv), F32), jnp.moveaxis(kv, 2, 0))
    r_prev = jnp.moveaxis(r_prev, 0, 2)
    o_inter = jnp.einsum('bhnid,bhnde->bhnie', qc, r_prev) * xi[None, :, None, :, None]
    return (o_intra + o_inter).reshape(b, h, s, dv)


def _dilated_branch(q, k, v, window, dilation):
    b, h, s, hd = q.shape
    L = s // dilation
    wd = window // dilation
    qb_sz = DIL_BLOCK
    nb = -(-L // qb_sz)
    lp = nb * qb_sz

    def sub(t):
        t = t.reshape(b, h, L, dilation, hd).transpose(0, 1, 3, 2, 4)
        return jnp.pad(t, ((0, 0), (0, 0), (0, 0), (0, lp - L), (0, 0)))

    def band(t):
        tb = t.reshape(b, h, dilation, nb, qb_sz, hd)
        prev = jnp.pad(tb, ((0, 0), (0, 0), (0, 0), (1, 0), (0, 0), (0, 0)))[:, :, :, :nb]
        return jnp.concatenate([prev, tb], axis=4)

    qs = sub(q).reshape(b, h, dilation, nb, qb_sz, hd)
    kb = band(sub(k))
    vb = band(sub(v))
    qpos = jnp.arange(nb)[:, None] * qb_sz + jnp.arange(qb_sz)[None, :]
    kpos = jnp.arange(nb)[:, None] * qb_sz - qb_sz + jnp.arange(2 * qb_sz)[None, :]
    dist = qpos[:, :, None] - kpos[:, None, :]
    mask = (dist >= 0) & (dist <= wd) & (kpos[:, None, :] >= 0)
    sc = jnp.einsum('bhrnqd,bhrnkd->bhrnqk', qs, kb).astype(F32) * (hd ** -0.5)
    sc = jnp.where(mask, sc, -jnp.inf)
    m = jnp.max(sc, -1)
    p = jnp.exp(sc - m[..., None])
    l = jnp.sum(p, -1)
    o = jnp.einsum('bhrnqk,bhrnkd->bhrnqd', p, vb.astype(F32)) / l[..., None]

    def unsub(t):
        tail = t.shape[5:]
        t = t.reshape(b, h, dilation, lp, *tail)[:, :, :, :L]
        t = jnp.moveaxis(t, 2, 3)
        return t.reshape(b, h, s, *tail)

    return unsub(o), unsub(m), unsub(l)


def _dilated_attention(q, k, v):
    outs = [_dilated_branch(q, k, v, w, r) for (w, r) in DIL_PAIRS]
    m_all = jnp.stack([m for (_, m, _) in outs])
    m_max = jnp.max(m_all, 0)
    wts = jnp.stack([l * jnp.exp(m - m_max) for (_, m, l) in outs])
    o_all = jnp.stack([o for (o, _, _) in outs])
    return jnp.sum(wts[..., None] * o_all, 0) / jnp.sum(wts, 0)[..., None]


def _even_mixer(x, positions, w_in, w_out):
    hproj = x @ w_in
    sizes = (RET_QK_W, RET_QK_W, RET_V_W, RET_V_W, DIL_W, DIL_W, DIL_W)
    offs = [0]
    for sz in sizes:
        offs.append(offs[-1] + sz)
    qa, ka, va, ga, qb, kb, vb = [hproj[..., offs[i]:offs[i + 1]] for i in range(len(sizes))]
    qa = _rotary(_heads(qa, RET_HEADS), positions, RET_DK, RET_THETA).astype(F32)
    ka = _rotary(_heads(ka, RET_HEADS), positions, RET_DK, RET_THETA).astype(F32) * (RET_DK ** -0.5)
    ya = _rms(_retention(qa, ka, _heads(va, RET_HEADS).astype(F32)))
    ya = (_merge(ya) * jax.nn.silu(ga.astype(F32))).astype(x.dtype)
    qb = _rotary(_heads(qb, DIL_HEADS), positions, ROPE_DIMS, ROPE_THETA)
    kb = _rotary(_heads(kb, DIL_HEADS), positions, ROPE_DIMS, ROPE_THETA)
    yb = _merge(_dilated_attention(qb, kb, _heads(vb, DIL_HEADS))).astype(x.dtype)
    return jnp.concatenate([ya, yb], -1) @ w_out


def _gated_delta_rule(q, k, v, beta, g):
    b, h, s, dk = q.shape
    dv = v.shape[-1]
    c = GDN_CHUNK
    n = s // c
    ch = lambda t: t.reshape(b, h, n, c, *t.shape[3:])
    q, k, v, beta, g = ch(q), ch(k), ch(v), ch(beta), ch(g)
    gc = jnp.cumsum(g, -1)
    tri = jnp.tril(jnp.ones((c, c), bool))
    strict = jnp.tril(jnp.ones((c, c), bool), -1)
    diff = gc[..., :, None] - gc[..., None, :]
    decay = jnp.where(tri, jnp.exp(jnp.where(tri, diff, 0.0)), 0.0)
    kb = k * beta[..., None]
    a_mat = jnp.eye(c, dtype=F32) + jnp.where(strict, jnp.einsum('bhnid,bhnjd->bhnij', kb, k) * decay, 0.0)
    rhs = jnp.concatenate([v * beta[..., None], kb * jnp.exp(gc)[..., None]], -1)
    sol = lax.linalg.triangular_solve(a_mat, rhs, left_side=True, lower=True, unit_diagonal=True)
    u, w = sol[..., :dv], sol[..., dv:]
    attn = jnp.where(tri, jnp.einsum('bhnid,bhnjd->bhnij', q, k) * decay, 0.0)
    glast = gc[..., -1]
    k_dec = k * jnp.exp(glast[..., None] - gc)[..., None]
    q_dec = q * jnp.exp(gc)[..., None]

    def step(state, inp):
        qd_n, u_n, w_n, attn_n, kd_n, gl_n = inp
        v_new = u_n - jnp.einsum('bhcd,bhde->bhce', w_n, state)
        o_n = jnp.einsum('bhcd,bhde->bhce', qd_n, state) + jnp.einsum('bhij,bhje->bhie', attn_n, v_new)
        state = state * jnp.exp(gl_n)[..., None, None] + jnp.einsum('bhcd,bhce->bhde', kd_n, v_new)
        return state, o_n

    xs = tuple(jnp.moveaxis(t, 2, 0) for t in (q_dec, u, w, attn, k_dec, glast))
    _, o = lax.scan(step, jnp.zeros((b, h, dk, dv), F32), xs)
    return jnp.moveaxis(o, 0, 2).reshape(b, h, s, dv)


def _odd_mixer(x, w_in, conv_w, a_log, dt_bias, norm_w, w_out):
    hproj = x @ w_in
    qkv = jax.nn.silu(_causal_dwconv(hproj[..., :3 * GDN_W], conv_w))
    gate = hproj[..., 3 * GDN_W:4 * GDN_W]
    b_raw = hproj[..., 4 * GDN_W:4 * GDN_W + GDN_HEADS].astype(F32)
    a_raw = hproj[..., 4 * GDN_W + GDN_HEADS:].astype(F32)
    q = _l2norm(_heads(qkv[..., :GDN_W], GDN_HEADS).astype(F32)) * (GDN_DK ** -0.5)
    k = _l2norm(_heads(qkv[..., GDN_W:2 * GDN_W], GDN_HEADS).astype(F32))
    v = _heads(qkv[..., 2 * GDN_W:], GDN_HEADS).astype(F32)
    beta = jax.nn.sigmoid(b_raw).transpose(0, 2, 1)
    g = (-jnp.exp(a_log.astype(F32)) * jax.nn.softplus(a_raw + dt_bias.astype(F32))).transpose(0, 2, 1)
    o = _rms(_gated_delta_rule(q, k, v, beta, g)) * norm_w.astype(F32)
    o = (_merge(o) * jax.nn.silu(gate.astype(F32))).astype(x.dtype)
    return o @ w_out


def _conv_ffn(x, w_up, conv_w, conv_b, w_down):
    hproj = _causal_dwconv(x @ w_up, conv_w) + conv_b
    gate, val = hproj[..., :D_FF], hproj[..., D_FF:]
    return (jax.nn.silu(gate) * val) @ w_down


def _fwd_setup_inputs(seed: int = 0) -> dict:
    key = jax.random.key(seed)
    ks = jax.random.split(key, 20)
    nrm = lambda kk, shape, scale: jax.random.normal(kk, shape, F32) * scale
    x = nrm(ks[0], (BATCH, SEQ, D_MODEL), 1.0)
    start = jax.random.randint(ks[1], (BATCH, 1), 0, 4096, dtype=jnp.int32)
    positions = start + jnp.arange(SEQ, dtype=jnp.int32)[None, :]
    ev_w_in = nrm(ks[2], (N_EVEN, D_MODEL, EV_IN), D_MODEL ** -0.5)
    ev_w_out = nrm(ks[3], (N_EVEN, EV_MIX, D_MODEL), EV_MIX ** -0.5 * DN_BETA)
    od_w_in = nrm(ks[4], (N_ODD, D_MODEL, OD_IN), D_MODEL ** -0.5)
    od_conv_w = nrm(ks[5], (N_ODD, GDN_CONV, 3 * GDN_W), GDN_CONV ** -0.5)
    od_a_log = jnp.log(jax.random.uniform(ks[6], (N_ODD, GDN_HEADS), F32, 1.0, 16.0))
    dt = jnp.exp(jax.random.uniform(ks[7], (N_ODD, GDN_HEADS), F32, math.log(1e-3), math.log(1e-1)))
    od_dt_bias = dt + jnp.log(-jnp.expm1(-dt))
    od_norm_w = 1.0 + nrm(ks[8], (N_ODD, GDN_DV), 0.02)
    od_w_out = nrm(ks[9], (N_ODD, GDN_W, D_MODEL), GDN_W ** -0.5 * DN_BETA)
    ffn_w_up = nrm(ks[10], (DEPTH, D_MODEL, 2 * D_FF), D_MODEL ** -0.5)
    ffn_conv_w = nrm(ks[11], (DEPTH, FFN_CONV, 2 * D_FF), FFN_CONV ** -0.5)
    ffn_conv_b = nrm(ks[12], (DEPTH, 2 * D_FF), 0.01)
    ffn_w_down = nrm(ks[13], (DEPTH, D_FF, D_MODEL), D_FF ** -0.5 * DN_BETA)
    ln1_g = 1.0 + nrm(ks[14], (DEPTH, D_MODEL), 0.02)
    ln1_b = nrm(ks[15], (DEPTH, D_MODEL), 0.01)
    ln2_g = 1.0 + nrm(ks[16], (DEPTH, D_MODEL), 0.02)
    ln2_b = nrm(ks[17], (DEPTH, D_MODEL), 0.01)
    return {"x": x, "positions": positions, "ev_w_in": ev_w_in, "ev_w_out": ev_w_out,
            "od_w_in": od_w_in, "od_conv_w": od_conv_w, "od_a_log": od_a_log,
            "od_dt_bias": od_dt_bias, "od_norm_w": od_norm_w, "od_w_out": od_w_out,
            "ffn_w_up": ffn_w_up, "ffn_conv_w": ffn_conv_w, "ffn_conv_b": ffn_conv_b,
            "ffn_w_down": ffn_w_down, "ln1_g": ln1_g, "ln1_b": ln1_b,
            "ln2_g": ln2_g, "ln2_b": ln2_b}


def _fwd_reference(x, positions, ev_w_in, ev_w_out, od_w_in, od_conv_w, od_a_log, od_dt_bias,
              od_norm_w, od_w_out, ffn_w_up, ffn_conv_w, ffn_conv_b, ffn_w_down,
              ln1_g, ln1_b, ln2_g, ln2_b):
    for layer in range(DEPTH):
        j = layer // 2
        if layer % 2 == 0:
            mix = _even_mixer(x, positions, ev_w_in[j], ev_w_out[j])
        else:
            mix = _odd_mixer(x, od_w_in[j], od_conv_w[j], od_a_log[j], od_dt_bias[j],
                             od_norm_w[j], od_w_out[j])
        x = _layernorm(DN_ALPHA * x + mix, ln1_g[layer], ln1_b[layer])
        ffn = _conv_ffn(x, ffn_w_up[layer], ffn_conv_w[layer], ffn_conv_b[layer], ffn_w_down[layer])
        x = _layernorm(DN_ALPHA * x + ffn, ln2_g[layer], ln2_b[layer])
    return x


import jax as _jax
import jax.numpy as _jnp

TWIN_FORMAT = 'train_step'
FWD_PARAMS = ['x', 'positions', 'ev_w_in', 'ev_w_out', 'od_w_in', 'od_conv_w', 'od_a_log', 'od_dt_bias', 'od_norm_w', 'od_w_out', 'ffn_w_up', 'ffn_conv_w', 'ffn_conv_b', 'ffn_w_down', 'ln1_g', 'ln1_b', 'ln2_g', 'ln2_b']
TWIN_WEIGHTS = ['ev_w_in', 'ev_w_out', 'od_w_in', 'od_conv_w', 'od_a_log', 'od_dt_bias', 'od_norm_w', 'od_w_out', 'ffn_w_up', 'ffn_conv_w', 'ffn_conv_b', 'ffn_w_down', 'ln1_g', 'ln1_b', 'ln2_g', 'ln2_b']
TWIN_DIFF_INPUT = 'x'
TWIN_INPUTS = ['x', 'positions', 'ev_w_in', 'ev_w_out', 'od_w_in', 'od_conv_w', 'od_a_log', 'od_dt_bias', 'od_norm_w', 'od_w_out', 'ffn_w_up', 'ffn_conv_w', 'ffn_conv_b', 'ffn_w_down', 'ln1_g', 'ln1_b', 'ln2_g', 'ln2_b', 'loss_target', 'm_ev_w_in', 'm_ev_w_out', 'm_od_w_in', 'm_od_conv_w', 'm_od_a_log', 'm_od_dt_bias', 'm_od_norm_w', 'm_od_w_out', 'm_ffn_w_up', 'm_ffn_conv_w', 'm_ffn_conv_b', 'm_ffn_w_down', 'm_ln1_g', 'm_ln1_b', 'm_ln2_g', 'm_ln2_b', 'v_ev_w_in', 'v_ev_w_out', 'v_od_w_in', 'v_od_conv_w', 'v_od_a_log', 'v_od_dt_bias', 'v_od_norm_w', 'v_od_w_out', 'v_ffn_w_up', 'v_ffn_conv_w', 'v_ffn_conv_b', 'v_ffn_w_down', 'v_ln1_g', 'v_ln1_b', 'v_ln2_g', 'v_ln2_b']
TWIN_OUTPUTS = ['loss', 'grad_x', 'grad_ev_w_in', 'grad_ev_w_out', 'grad_od_w_in', 'grad_od_conv_w', 'grad_od_a_log', 'grad_od_dt_bias', 'grad_od_norm_w', 'grad_od_w_out', 'grad_ffn_w_up', 'grad_ffn_conv_w', 'grad_ffn_conv_b', 'grad_ffn_w_down', 'grad_ln1_g', 'grad_ln1_b', 'grad_ln2_g', 'grad_ln2_b', 'delta_ev_w_in', 'delta_ev_w_out', 'delta_od_w_in', 'delta_od_conv_w', 'delta_od_a_log', 'delta_od_dt_bias', 'delta_od_norm_w', 'delta_od_w_out', 'delta_ffn_w_up', 'delta_ffn_conv_w', 'delta_ffn_conv_b', 'delta_ffn_w_down', 'delta_ln1_g', 'delta_ln1_b', 'delta_ln2_g', 'delta_ln2_b', 'new_m_ev_w_in', 'new_m_ev_w_out', 'new_m_od_w_in', 'new_m_od_conv_w', 'new_m_od_a_log', 'new_m_od_dt_bias', 'new_m_od_norm_w', 'new_m_od_w_out', 'new_m_ffn_w_up', 'new_m_ffn_conv_w', 'new_m_ffn_conv_b', 'new_m_ffn_w_down', 'new_m_ln1_g', 'new_m_ln1_b', 'new_m_ln2_g', 'new_m_ln2_b', 'new_v_ev_w_in', 'new_v_ev_w_out', 'new_v_od_w_in', 'new_v_od_conv_w', 'new_v_od_a_log', 'new_v_od_dt_bias', 'new_v_od_norm_w', 'new_v_od_w_out', 'new_v_ffn_w_up', 'new_v_ffn_conv_w', 'new_v_ffn_conv_b', 'new_v_ffn_w_down', 'new_v_ln1_g', 'new_v_ln1_b', 'new_v_ln2_g', 'new_v_ln2_b']
TWIN_LEAF_KINDS = {'loss': 'loss', 'grad_x': 'grad_x', 'grad_ev_w_in': 'grad_w', 'grad_ev_w_out': 'grad_w', 'grad_od_w_in': 'grad_w', 'grad_od_conv_w': 'grad_w', 'grad_od_a_log': 'grad_w', 'grad_od_dt_bias': 'grad_w', 'grad_od_norm_w': 'grad_w', 'grad_od_w_out': 'grad_w', 'grad_ffn_w_up': 'grad_w', 'grad_ffn_conv_w': 'grad_w', 'grad_ffn_conv_b': 'grad_w', 'grad_ffn_w_down': 'grad_w', 'grad_ln1_g': 'grad_w', 'grad_ln1_b': 'grad_w', 'grad_ln2_g': 'grad_w', 'grad_ln2_b': 'grad_w', 'delta_ev_w_in': 'delta_w', 'delta_ev_w_out': 'delta_w', 'delta_od_w_in': 'delta_w', 'delta_od_conv_w': 'delta_w', 'delta_od_a_log': 'delta_w', 'delta_od_dt_bias': 'delta_w', 'delta_od_norm_w': 'delta_w', 'delta_od_w_out': 'delta_w', 'delta_ffn_w_up': 'delta_w', 'delta_ffn_conv_w': 'delta_w', 'delta_ffn_conv_b': 'delta_w', 'delta_ffn_w_down': 'delta_w', 'delta_ln1_g': 'delta_w', 'delta_ln1_b': 'delta_w', 'delta_ln2_g': 'delta_w', 'delta_ln2_b': 'delta_w', 'new_m_ev_w_in': 'new_m', 'new_m_ev_w_out': 'new_m', 'new_m_od_w_in': 'new_m', 'new_m_od_conv_w': 'new_m', 'new_m_od_a_log': 'new_m', 'new_m_od_dt_bias': 'new_m', 'new_m_od_norm_w': 'new_m', 'new_m_od_w_out': 'new_m', 'new_m_ffn_w_up': 'new_m', 'new_m_ffn_conv_w': 'new_m', 'new_m_ffn_conv_b': 'new_m', 'new_m_ffn_w_down': 'new_m', 'new_m_ln1_g': 'new_m', 'new_m_ln1_b': 'new_m', 'new_m_ln2_g': 'new_m', 'new_m_ln2_b': 'new_m', 'new_v_ev_w_in': 'new_v', 'new_v_ev_w_out': 'new_v', 'new_v_od_w_in': 'new_v', 'new_v_od_conv_w': 'new_v', 'new_v_od_a_log': 'new_v', 'new_v_od_dt_bias': 'new_v', 'new_v_od_norm_w': 'new_v', 'new_v_od_w_out': 'new_v', 'new_v_ffn_w_up': 'new_v', 'new_v_ffn_conv_w': 'new_v', 'new_v_ffn_conv_b': 'new_v', 'new_v_ffn_w_down': 'new_v', 'new_v_ln1_g': 'new_v', 'new_v_ln1_b': 'new_v', 'new_v_ln2_g': 'new_v', 'new_v_ln2_b': 'new_v'}


def _forward(args):
    return _fwd_reference(*[args[k] for k in FWD_PARAMS])


def _output_shape():
    out = _jax.eval_shape(lambda: _forward(_fwd_setup_inputs(0)))
    return out.shape, out.dtype

N_MICROBATCH = 1
ADAM_LR = 0.001
ADAM_B1 = 0.9
ADAM_B2 = 0.999
ADAM_EPS = 1e-08
ADAM_WD = 0.01
ADAM_STEP = 10
PER_EXAMPLE_BATCH_AXIS = {'x': 0, 'positions': 0, 'loss_target': 0}
SHARED_INPUTS = []
_WEIGHT_DTYPES = {'ev_w_in': _jnp.float32, 'ev_w_out': _jnp.float32, 'od_w_in': _jnp.float32, 'od_conv_w': _jnp.float32, 'od_a_log': _jnp.float32, 'od_dt_bias': _jnp.float32, 'od_norm_w': _jnp.float32, 'od_w_out': _jnp.float32, 'ffn_w_up': _jnp.float32, 'ffn_conv_w': _jnp.float32, 'ffn_conv_b': _jnp.float32, 'ffn_w_down': _jnp.float32, 'ln1_g': _jnp.float32, 'ln1_b': _jnp.float32, 'ln2_g': _jnp.float32, 'ln2_b': _jnp.float32}
MOMENT_SCALE = {'ev_w_in': 1.587778e-02, 'ev_w_out': 3.993311e-02, 'od_w_in': 1.451615e-02, 'od_conv_w': 1.308185e-02, 'od_a_log': 5.970151e-02, 'od_dt_bias': 5.901884e-02, 'od_norm_w': 4.983689e-02, 'od_w_out': 4.225014e-02, 'ffn_w_up': 1.208828e-02, 'ffn_conv_w': 1.215587e-02, 'ffn_conv_b': 1.209028e-02, 'ffn_w_down': 4.693822e-02, 'ln1_g': 5.393324e-01, 'ln1_b': 1.684166e-01, 'ln2_g': 8.040222e+00, 'ln2_b': 3.722412e-01}


def _to_microbatches(a, axis):
    t = _jnp.moveaxis(a, axis, 0)
    t = t.reshape((N_MICROBATCH, t.shape[0] // N_MICROBATCH) + t.shape[1:])
    return _jnp.moveaxis(t, 1, axis + 1)


def setup_inputs(seed: int = 0) -> dict:
    inp = _fwd_setup_inputs(seed)
    key = _jax.random.fold_in(_jax.random.key(seed), 7919)
    shape, _ = _output_shape()
    out = dict(inp)
    out["loss_target"] = _jax.random.normal(_jax.random.fold_in(key, 0), shape, _jnp.float32)
    for i, name in enumerate(TWIN_WEIGHTS):
        w = inp[name].astype(_jnp.float32)
        if MOMENT_SCALE is None:
            s = _jnp.sqrt(_jnp.mean(_jnp.square(w)) + 1e-30)
        else:
            s = MOMENT_SCALE[name]
        km, kv = _jax.random.split(_jax.random.fold_in(key, i + 1))
        out[name] = w
        out["m_" + name] = s * _jax.random.normal(km, w.shape, _jnp.float32)
        out["v_" + name] = (s * s) * _jax.random.uniform(kv, w.shape, _jnp.float32, 0.5, 1.5)
    if N_MICROBATCH > 1:
        for name, axis in PER_EXAMPLE_BATCH_AXIS.items():
            out[name] = _to_microbatches(out[name], axis)
    return {'x': out['x'], 'positions': out['positions'], 'ev_w_in': out['ev_w_in'], 'ev_w_out': out['ev_w_out'], 'od_w_in': out['od_w_in'], 'od_conv_w': out['od_conv_w'], 'od_a_log': out['od_a_log'], 'od_dt_bias': out['od_dt_bias'], 'od_norm_w': out['od_norm_w'], 'od_w_out': out['od_w_out'], 'ffn_w_up': out['ffn_w_up'], 'ffn_conv_w': out['ffn_conv_w'], 'ffn_conv_b': out['ffn_conv_b'], 'ffn_w_down': out['ffn_w_down'], 'ln1_g': out['ln1_g'], 'ln1_b': out['ln1_b'], 'ln2_g': out['ln2_g'], 'ln2_b': out['ln2_b'], 'loss_target': out['loss_target'], 'm_ev_w_in': out['m_ev_w_in'], 'm_ev_w_out': out['m_ev_w_out'], 'm_od_w_in': out['m_od_w_in'], 'm_od_conv_w': out['m_od_conv_w'], 'm_od_a_log': out['m_od_a_log'], 'm_od_dt_bias': out['m_od_dt_bias'], 'm_od_norm_w': out['m_od_norm_w'], 'm_od_w_out': out['m_od_w_out'], 'm_ffn_w_up': out['m_ffn_w_up'], 'm_ffn_conv_w': out['m_ffn_conv_w'], 'm_ffn_conv_b': out['m_ffn_conv_b'], 'm_ffn_w_down': out['m_ffn_w_down'], 'm_ln1_g': out['m_ln1_g'], 'm_ln1_b': out['m_ln1_b'], 'm_ln2_g': out['m_ln2_g'], 'm_ln2_b': out['m_ln2_b'], 'v_ev_w_in': out['v_ev_w_in'], 'v_ev_w_out': out['v_ev_w_out'], 'v_od_w_in': out['v_od_w_in'], 'v_od_conv_w': out['v_od_conv_w'], 'v_od_a_log': out['v_od_a_log'], 'v_od_dt_bias': out['v_od_dt_bias'], 'v_od_norm_w': out['v_od_norm_w'], 'v_od_w_out': out['v_od_w_out'], 'v_ffn_w_up': out['v_ffn_w_up'], 'v_ffn_conv_w': out['v_ffn_conv_w'], 'v_ffn_conv_b': out['v_ffn_conv_b'], 'v_ffn_w_down': out['v_ffn_w_down'], 'v_ln1_g': out['v_ln1_g'], 'v_ln1_b': out['v_ln1_b'], 'v_ln2_g': out['v_ln2_g'], 'v_ln2_b': out['v_ln2_b']}


def _loss(weights, diff, rest, loss_target):
    with _jax.named_scope("forward"):
        args = {**rest, TWIN_DIFF_INPUT: diff, **{k: w.astype(_WEIGHT_DTYPES[k]) for k, w in weights.items()}}
        y = _forward(args)
    with _jax.named_scope("loss_head"):
        err = _jnp.square(y.astype(_jnp.float32) - loss_target)
        return 0.5 * _jnp.sum(_jnp.mean(err, axis=-1)) if err.ndim else 0.5 * err


def _adamw(w, g, m, v):
    m = ADAM_B1 * m + (1.0 - ADAM_B1) * g
    v = ADAM_B2 * v + (1.0 - ADAM_B2) * _jnp.square(g)
    m_hat = m / (1.0 - ADAM_B1 ** ADAM_STEP)
    v_hat = v / (1.0 - ADAM_B2 ** ADAM_STEP)
    delta = -ADAM_LR * (m_hat / (_jnp.sqrt(v_hat) + ADAM_EPS) + ADAM_WD * w)
    return delta, m, v


def reference(x, positions, ev_w_in, ev_w_out, od_w_in, od_conv_w, od_a_log, od_dt_bias, od_norm_w, od_w_out, ffn_w_up, ffn_conv_w, ffn_conv_b, ffn_w_down, ln1_g, ln1_b, ln2_g, ln2_b, loss_target, m_ev_w_in, m_ev_w_out, m_od_w_in, m_od_conv_w, m_od_a_log, m_od_dt_bias, m_od_norm_w, m_od_w_out, m_ffn_w_up, m_ffn_conv_w, m_ffn_conv_b, m_ffn_w_down, m_ln1_g, m_ln1_b, m_ln2_g, m_ln2_b, v_ev_w_in, v_ev_w_out, v_od_w_in, v_od_conv_w, v_od_a_log, v_od_dt_bias, v_od_norm_w, v_od_w_out, v_ffn_w_up, v_ffn_conv_w, v_ffn_conv_b, v_ffn_w_down, v_ln1_g, v_ln1_b, v_ln2_g, v_ln2_b):
    given = dict(x=x, positions=positions, ev_w_in=ev_w_in, ev_w_out=ev_w_out, od_w_in=od_w_in, od_conv_w=od_conv_w, od_a_log=od_a_log, od_dt_bias=od_dt_bias, od_norm_w=od_norm_w, od_w_out=od_w_out, ffn_w_up=ffn_w_up, ffn_conv_w=ffn_conv_w, ffn_conv_b=ffn_conv_b, ffn_w_down=ffn_w_down, ln1_g=ln1_g, ln1_b=ln1_b, ln2_g=ln2_g, ln2_b=ln2_b, loss_target=loss_target, m_ev_w_in=m_ev_w_in, m_ev_w_out=m_ev_w_out, m_od_w_in=m_od_w_in, m_od_conv_w=m_od_conv_w, m_od_a_log=m_od_a_log, m_od_dt_bias=m_od_dt_bias, m_od_norm_w=m_od_norm_w, m_od_w_out=m_od_w_out, m_ffn_w_up=m_ffn_w_up, m_ffn_conv_w=m_ffn_conv_w, m_ffn_conv_b=m_ffn_conv_b, m_ffn_w_down=m_ffn_w_down, m_ln1_g=m_ln1_g, m_ln1_b=m_ln1_b, m_ln2_g=m_ln2_g, m_ln2_b=m_ln2_b, v_ev_w_in=v_ev_w_in, v_ev_w_out=v_ev_w_out, v_od_w_in=v_od_w_in, v_od_conv_w=v_od_conv_w, v_od_a_log=v_od_a_log, v_od_dt_bias=v_od_dt_bias, v_od_norm_w=v_od_norm_w, v_od_w_out=v_od_w_out, v_ffn_w_up=v_ffn_w_up, v_ffn_conv_w=v_ffn_conv_w, v_ffn_conv_b=v_ffn_conv_b, v_ffn_w_down=v_ffn_w_down, v_ln1_g=v_ln1_g, v_ln1_b=v_ln1_b, v_ln2_g=v_ln2_g, v_ln2_b=v_ln2_b)
    weights = {n: given[n] for n in TWIN_WEIGHTS}
    shared = {n: given[n] for n in SHARED_INPUTS}
    per_example = {n: given[n] for n in ['x', 'positions']}
    grad_fn = _jax.value_and_grad(_loss, argnums=(0, 1))

    def one_microbatch(ex, loss_target):
        ex = dict(ex)
        diff = ex.pop(TWIN_DIFF_INPUT)
        return grad_fn(weights, diff, {**shared, **ex}, loss_target)

    if N_MICROBATCH == 1:
        loss, (grad_w, grad_x) = one_microbatch(per_example, given["loss_target"])
    else:
        def body(carry, xs):
            loss_sum, grad_sum = carry
            l_k, (gw_k, gx_k) = one_microbatch(xs[0], xs[1])
            with _jax.named_scope("update"):
                return (loss_sum + l_k, _jax.tree.map(_jnp.add, grad_sum, gw_k)), gx_k

        init = (_jnp.zeros((), _jnp.float32), _jax.tree.map(_jnp.zeros_like, weights))
        (loss, grad_w), grad_x = _jax.lax.scan(body, init, (per_example, given["loss_target"]))
    with _jax.named_scope("update"):
        delta_w, new_m, new_v = {}, {}, {}
        for n in TWIN_WEIGHTS:
            delta_w[n], new_m[n], new_v[n] = _adamw(weights[n], grad_w[n], given["m_" + n], given["v_" + n])
    return (loss, grad_x, *[grad_w[n] for n in TWIN_WEIGHTS], *[delta_w[n] for n in TWIN_WEIGHTS],
            *[new_m[n] for n in TWIN_WEIGHTS], *[new_v[n] for n in TWIN_WEIGHTS])
```

```python
import functools
import math

import numpy as np
import jax
import jax.numpy as jnp
from jax import lax
from jax.experimental import pallas as pl
from jax.experimental.pallas import tpu as pltpu

F32 = jnp.float32
BF16 = jnp.bfloat16
MESH = pl.DeviceIdType.MESH

D_MODEL = 1024
SEQ = 2048
DEPTH = 4
N_DEV = 8
RET_HEADS, RET_DK, RET_DV = 4, 128, 256
RET_THETA = 10000.0
DIL_HEADS, DIL_HD = 8, 64
DIL_PAIRS = ((128, 1), (512, 4), (2048, 16))
ROPE_THETA = 500000.0
ROPE_DIMS = DIL_HD // 4
GDN_HEADS, GDN_DK, GDN_DV, GDN_CHUNK, GDN_CONV = 8, 128, 128, 64, 4
D_FF = 2816
FFN_CONV = 3
ALPHA = (2.0 * DEPTH) ** 0.25
EPS = 1e-5
RET_QK_W = RET_HEADS * RET_DK
RET_V_W = RET_HEADS * RET_DV
DIL_W = DIL_HEADS * DIL_HD
EV_IN = 2 * RET_QK_W + 2 * RET_V_W + 3 * DIL_W
EV_MIX = RET_V_W + DIL_W
GDN_W = GDN_HEADS * GDN_DK
OD_IN = 4 * GDN_W + 2 * GDN_HEADS
OD_IN_PAD = 4 * GDN_W + 128
ADAM_LR, ADAM_B1, ADAM_B2, ADAM_EPS, ADAM_WD, ADAM_STEP = 0.001, 0.9, 0.999, 1e-08, 0.01, 10

LANES = 128
VMEM_LIMIT = 56 * 1024 * 1024
ATT_BLK = 256
NEG = -1e30


def _cp(**kw):
    return pltpu.CompilerParams(vmem_limit_bytes=VMEM_LIMIT, **kw)


def _tile(n, cap):
    if n <= cap:
        return n
    best = None
    for t in range(LANES, cap + 1, LANES):
        if n % t == 0:
            best = t
    assert best is not None, (n, cap)
    return best


def _mm(a, b, *, ta=False, tb=False, name, out_dtype=F32):
    m = a.shape[1] if ta else a.shape[0]
    k = a.shape[0] if ta else a.shape[1]
    n = b.shape[0] if tb else b.shape[1]
    assert (b.shape[1] if tb else b.shape[0]) == k
    assert a.dtype == BF16 and b.dtype == BF16
    tm = _tile(m, 1024 if k <= 2048 else 512)
    tn = _tile(n, 512)
    dims = (((0 if ta else 1,), (1 if tb else 0,)), ((), ()))

    def body(a_ref, b_ref, o_ref):
        o_ref[...] = lax.dot_general(a_ref[...], b_ref[...], dims,
                                     preferred_element_type=F32).astype(o_ref.dtype)

    a_spec = pl.BlockSpec((k, tm), lambda i, j: (0, i)) if ta else pl.BlockSpec((tm, k), lambda i, j: (i, 0))
    b_spec = pl.BlockSpec((tn, k), lambda i, j: (j, 0)) if tb else pl.BlockSpec((k, tn), lambda i, j: (0, j))
    return pl.pallas_call(
        body, grid=(m // tm, n // tn), in_specs=[a_spec, b_spec],
        out_specs=pl.BlockSpec((tm, tn), lambda i, j: (i, j)),
        out_shape=jax.ShapeDtypeStruct((m, n), out_dtype), name=name, compiler_params=_cp())(a, b)


LN_ROWS = 256


def _ln_fwd(x, m, g, b, *, name):
    t, d = x.shape

    def body(x_ref, m_ref, g_ref, b_ref, z_ref, y_ref, yb_ref):
        z = ALPHA * x_ref[...] + m_ref[...]
        mu = jnp.mean(z, -1, keepdims=True)
        zc = z - mu
        var = jnp.mean(zc * zc, -1, keepdims=True)
        y = zc * lax.rsqrt(var + EPS) * g_ref[...] + b_ref[...]
        z_ref[...] = z
        y_ref[...] = y
        yb_ref[...] = y.astype(BF16)

    row = pl.BlockSpec((LN_ROWS, d), lambda i: (i, 0))
    vec = pl.BlockSpec((1, d), lambda i: (0, 0))
    return pl.pallas_call(
        body, grid=(t // LN_ROWS,), in_specs=[row, row, vec, vec], out_specs=[row, row, row],
        out_shape=[jax.ShapeDtypeStruct((t, d), F32), jax.ShapeDtypeStruct((t, d), F32),
                   jax.ShapeDtypeStruct((t, d), BF16)],
        name=name, compiler_params=_cp())(x, m, g, b)


def _ln_bwd(z, g, dya, dyb, *, name):
    t, d = z.shape
    two = dyb is not None

    def body(*refs):
        if two:
            z_ref, g_ref, dya_ref, dyb_ref, dz_ref, dzb_ref, dg_ref, db_ref = refs
            dy = dya_ref[...] + ALPHA * dyb_ref[...]
        else:
            z_ref, g_ref, dya_ref, dz_ref, dzb_ref, dg_ref, db_ref = refs
            dy = dya_ref[...]
        zz = z_ref[...]
        mu = jnp.mean(zz, -1, keepdims=True)
        zc = zz - mu
        var = jnp.mean(zc * zc, -1, keepdims=True)
        r = lax.rsqrt(var + EPS)
        xh = zc * r
        dxh = dy * g_ref[...]
        dz = r * (dxh - jnp.mean(dxh, -1, keepdims=True) - xh * jnp.mean(dxh * xh, -1, keepdims=True))
        dz_ref[...] = dz
        dzb_ref[...] = dz.astype(BF16)

        @pl.when(pl.program_id(0) == 0)
        def _():
            dg_ref[...] = jnp.zeros_like(dg_ref)
            db_ref[...] = jnp.zeros_like(db_ref)

        dg_ref[...] += jnp.sum(dy * xh, 0, keepdims=True)
        db_ref[...] += jnp.sum(dy, 0, keepdims=True)

    row = pl.BlockSpec((LN_ROWS, d), lambda i: (i, 0))
    vec = pl.BlockSpec((1, d), lambda i: (0, 0))
    ins = [z, g, dya] + ([dyb] if two else [])
    return pl.pallas_call(
        body, grid=(t // LN_ROWS,), in_specs=[row, vec, row] + ([row] if two else []),
        out_specs=[row, row, vec, vec],
        out_shape=[jax.ShapeDtypeStruct((t, d), F32), jax.ShapeDtypeStruct((t, d), BF16),
                   jax.ShapeDtypeStruct((1, d), F32), jax.ShapeDtypeStruct((1, d), F32)],
        name=name, compiler_params=_cp())(*ins)


def _axpy(a, b, *, name):
    t, d = a.shape

    def body(a_ref, b_ref, o_ref):
        o_ref[...] = a_ref[...] + ALPHA * b_ref[...]

    row = pl.BlockSpec((LN_ROWS, d), lambda i: (i, 0))
    return pl.pallas_call(body, grid=(t // LN_ROWS,), in_specs=[row, row], out_specs=row,
                          out_shape=jax.ShapeDtypeStruct((t, d), F32), name=name, compiler_params=_cp())(a, b)


def _loss_head(y, target, *, name):
    t, d = y.shape

    def body(y_ref, t_ref, dy_ref, l_ref):
        e = y_ref[...] - t_ref[...]
        dy_ref[...] = e * (1.0 / d)

        @pl.when(pl.program_id(0) == 0)
        def _():
            l_ref[...] = jnp.zeros_like(l_ref)

        l_ref[...] += jnp.zeros_like(l_ref) + 0.5 * jnp.sum(jnp.mean(e * e, -1, keepdims=True), 0, keepdims=True)

    row = pl.BlockSpec((LN_ROWS, d), lambda i: (i, 0))
    return pl.pallas_call(
        body, grid=(t // LN_ROWS,), in_specs=[row, row],
        out_specs=[row, pl.BlockSpec((1, LANES), lambda i: (0, 0))],
        out_shape=[jax.ShapeDtypeStruct((t, d), F32), jax.ShapeDtypeStruct((1, LANES), F32)],
        name=name, compiler_params=_cp())(y, target)


def _sig(x):
    return 1.0 / (1.0 + jnp.exp(-x))


def _silu(x):
    return x * _sig(x)


def _dsilu(x):
    s = _sig(x)
    return s * (1.0 + x * (1.0 - s))


def _shift_down(u, k, row):
    if k == 0:
        return u
    return jnp.where(row >= k, pltpu.roll(u, k, 0), 0.0)


def _shift_up(u, k, row):
    if k == 0:
        return u
    t = u.shape[0]
    return jnp.where(row < t - k, pltpu.roll(u, t - k, 0), 0.0)


def _dwconv(u, w_ref, row):
    kk = w_ref.shape[0]
    acc = None
    for j in range(kk):
        term = w_ref[j:j + 1, :] * _shift_down(u, kk - 1 - j, row)
        acc = term if acc is None else acc + term
    return acc


def _dwconv_bwd(u, w_ref, dc, row, dw_ref):
    kk = w_ref.shape[0]
    du = None
    for j in range(kk):
        term = w_ref[j:j + 1, :] * _shift_up(dc, kk - 1 - j, row)
        du = term if du is None else du + term
        dw_ref[j:j + 1, :] = jnp.sum(dc * _shift_down(u, kk - 1 - j, row), 0, keepdims=True)
    return du


FFN_TC = 256


def _ffn_mid_fwd(u, cw, cb, *, name):
    t = u.shape[0]
    nb = D_FF // FFN_TC

    def body(ug_ref, uv_ref, wg_ref, wv_ref, bg_ref, bv_ref, a_ref):
        row = lax.broadcasted_iota(jnp.int32, (t, FFN_TC), 0)
        cg = _dwconv(ug_ref[...], wg_ref, row) + bg_ref[...]
        cv = _dwconv(uv_ref[...], wv_ref, row) + bv_ref[...]
        a_ref[...] = (_silu(cg) * cv).astype(BF16)

    col = lambda off: pl.BlockSpec((t, FFN_TC), lambda j: (0, j + off))
    wsp = lambda off: pl.BlockSpec((FFN_CONV, FFN_TC), lambda j: (0, j + off))
    bsp = lambda off: pl.BlockSpec((1, FFN_TC), lambda j: (0, j + off))
    return pl.pallas_call(
        body, grid=(nb,), in_specs=[col(0), col(nb), wsp(0), wsp(nb), bsp(0), bsp(nb)],
        out_specs=pl.BlockSpec((t, FFN_TC), lambda j: (0, j)),
        out_shape=jax.ShapeDtypeStruct((t, D_FF), BF16), name=name, compiler_params=_cp())(u, u, cw, cw, cb, cb)


def _ffn_mid_bwd(u, cw, cb, da, *, name):
    t = u.shape[0]
    nb = D_FF // FFN_TC

    def body(ug_ref, uv_ref, wg_ref, wv_ref, bg_ref, bv_ref, da_ref,
             dug_ref, duv_ref, dwg_ref, dwv_ref, dbg_ref, dbv_ref):
        row = lax.broadcasted_iota(jnp.int32, (t, FFN_TC), 0)
        ug, uv = ug_ref[...], uv_ref[...]
        cg = _dwconv(ug, wg_ref, row) + bg_ref[...]
        cv = _dwconv(uv, wv_ref, row) + bv_ref[...]
        da_ = da_ref[...]
        dcv = da_ * _silu(cg)
        dcg = da_ * cv * _dsilu(cg)
        dug_ref[...] = _dwconv_bwd(ug, wg_ref, dcg, row, dwg_ref).astype(BF16)
        duv_ref[...] = _dwconv_bwd(uv, wv_ref, dcv, row, dwv_ref).astype(BF16)
        dbg_ref[...] = jnp.sum(dcg, 0, keepdims=True)
        dbv_ref[...] = jnp.sum(dcv, 0, keepdims=True)

    col = lambda off: pl.BlockSpec((t, FFN_TC), lambda j: (0, j + off))
    wsp = lambda off: pl.BlockSpec((FFN_CONV, FFN_TC), lambda j: (0, j + off))
    bsp = lambda off: pl.BlockSpec((1, FFN_TC), lambda j: (0, j + off))
    outs = pl.pallas_call(
        body, grid=(nb,), in_specs=[col(0), col(nb), wsp(0), wsp(nb), bsp(0), bsp(nb), col(0)],
        out_specs=[col(0), col(0), wsp(0), wsp(0), bsp(0), bsp(0)],
        out_shape=[jax.ShapeDtypeStruct((t, D_FF), BF16), jax.ShapeDtypeStruct((t, D_FF), BF16),
                   jax.ShapeDtypeStruct((FFN_CONV, D_FF), F32), jax.ShapeDtypeStruct((FFN_CONV, D_FF), F32),
                   jax.ShapeDtypeStruct((1, D_FF), F32), jax.ShapeDtypeStruct((1, D_FF), F32)],
        name=name, compiler_params=_cp())(u, u, cw, cw, cb, cb, da)
    dug, duv, dwg, dwv, dbg, dbv = outs
    return (jnp.concatenate([dug, duv], 1), jnp.concatenate([dwg, dwv], 1), jnp.concatenate([dbg, dbv], 1))


def _rot_a(x, c2, s2):
    return x * c2 + pltpu.roll(x, RET_DK // 2, 1) * s2


def _rot_a_t(dy, c2, s2):
    return dy * c2 + pltpu.roll(dy * s2, RET_DK // 2, 1)


def _decay_tile(lg, blk_diff):
    r = lax.broadcasted_iota(jnp.int32, (ATT_BLK, ATT_BLK), 0)
    c = lax.broadcasted_iota(jnp.int32, (ATT_BLK, ATT_BLK), 1)
    rel = r - c + blk_diff * ATT_BLK
    return jnp.where(rel >= 0, jnp.exp(jnp.maximum(rel, 0).astype(F32) * lg), 0.0)


def _nt(a, b):
    return lax.dot_general(a, b, (((1,), (1,)), ((), ())), preferred_element_type=F32)


def _nn(a, b):
    return lax.dot_general(a, b, (((1,), (0,)), ((), ())), preferred_element_type=F32)


def _tn(a, b):
    return lax.dot_general(a, b, (((0,), (0,)), ((), ())), preferred_element_type=F32)


def _ret_specs(t):
    q = pl.BlockSpec((t, RET_DK), lambda h: (0, h))
    k = pl.BlockSpec((t, RET_DK), lambda h: (0, RET_HEADS + h))
    v = pl.BlockSpec((t, RET_DV), lambda h: (0, RET_HEADS + h))
    g = pl.BlockSpec((t, RET_DV), lambda h: (0, 2 * RET_HEADS + h))
    tab = pl.BlockSpec((t, RET_DK), lambda h: (0, 0))
    lg = pl.BlockSpec((1, 1, LANES), lambda h: (h, 0, 0))
    return q, k, v, g, tab, lg


def _ret_fwd(h, c2, s2, lgt, *, name):
    t = h.shape[0]
    nblk = t // ATT_BLK
    scale = RET_DK ** -0.5

    def body(q_ref, k_ref, v_ref, g_ref, c_ref, s_ref, lg_ref, o_ref, ya_ref, qs, ks, vs):
        c2_, s2_ = c_ref[...], s_ref[...]
        qs[...] = _rot_a(q_ref[...], c2_, s2_).astype(BF16)
        ks[...] = (_rot_a(k_ref[...], c2_, s2_) * scale).astype(BF16)
        vs[...] = v_ref[...].astype(BF16)
        lg = lg_ref[0, :, 0:1]
        for i in range(nblk):
            qi = qs[pl.ds(i * ATT_BLK, ATT_BLK), :]
            acc = jnp.zeros((ATT_BLK, RET_DV), F32)
            for j in range(i + 1):
                sl = pl.ds(j * ATT_BLK, ATT_BLK)
                s = _nt(qi, ks[sl, :]) * _decay_tile(lg, i - j)
                acc = acc + _nn(s.astype(BF16), vs[sl, :])
            rows = pl.ds(i * ATT_BLK, ATT_BLK)
            o_ref[rows, :] = acc
            r = lax.rsqrt(jnp.mean(acc * acc, -1, keepdims=True) + EPS)
            ya_ref[rows, :] = (acc * r * _silu(g_ref[rows, :])).astype(BF16)

    q, k, v, g, tab, lg = _ret_specs(t)
    out = pl.BlockSpec((t, RET_DV), lambda hh: (0, hh))
    return pl.pallas_call(
        body, grid=(RET_HEADS,), in_specs=[q, k, v, g, tab, tab, lg], out_specs=[out, out],
        out_shape=[jax.ShapeDtypeStruct((t, RET_V_W), F32), jax.ShapeDtypeStruct((t, RET_V_W), BF16)],
        scratch_shapes=[pltpu.VMEM((t, RET_DK), BF16), pltpu.VMEM((t, RET_DK), BF16), pltpu.VMEM((t, RET_DV), BF16)],
        name=name, compiler_params=_cp())(h, h, h, h, c2, s2, lgt)


def _ret_bwd(h, c2, s2, lgt, o, dy, *, name):
    t = h.shape[0]
    nblk = t // ATT_BLK
    scale = RET_DK ** -0.5

    def body(q_ref, k_ref, v_ref, g_ref, c_ref, s_ref, lg_ref, o_ref, dy_ref,
             dq_ref, dk_ref, dv_ref, dg_ref, qs, ks, vs, dos, dka, dva):
        c2_, s2_ = c_ref[...], s_ref[...]
        qs[...] = _rot_a(q_ref[...], c2_, s2_).astype(BF16)
        ks[...] = (_rot_a(k_ref[...], c2_, s2_) * scale).astype(BF16)
        vs[...] = v_ref[...].astype(BF16)
        lg = lg_ref[0, :, 0:1]
        oo = o_ref[...]
        gg = g_ref[...]
        dya = dy_ref[...]
        r = lax.rsqrt(jnp.mean(oo * oo, -1, keepdims=True) + EPS)
        rn = oo * r
        dg_ref[...] = (dya * rn * _dsilu(gg)).astype(BF16)
        drn = dya * _silu(gg)
        dos[...] = (r * (drn - rn * jnp.mean(drn * rn, -1, keepdims=True))).astype(BF16)
        dka[...] = jnp.zeros_like(dka)
        dva[...] = jnp.zeros_like(dva)
        for i in range(nblk):
            rows = pl.ds(i * ATT_BLK, ATT_BLK)
            qi = qs[rows, :]
            doi = dos[rows, :]
            dqa = jnp.zeros((ATT_BLK, RET_DK), F32)
            for j in range(i + 1):
                sl = pl.ds(j * ATT_BLK, ATT_BLK)
                dt_ = _decay_tile(lg, i - j)
                kj = ks[sl, :]
                s = (_nt(qi, kj) * dt_).astype(BF16)
                ds = (_nt(doi, vs[sl, :]) * dt_).astype(BF16)
                dqa = dqa + _nn(ds, kj)
                dka[sl, :] += _tn(ds, qi)
                dva[sl, :] += _tn(s, doi)
            dq_ref[rows, :] = _rot_a_t(dqa, c_ref[rows, :], s_ref[rows, :]).astype(BF16)
        dk_ref[...] = (_rot_a_t(dka[...], c2_, s2_) * scale).astype(BF16)
        dv_ref[...] = dva[...].astype(BF16)

    q, k, v, g, tab, lg = _ret_specs(t)
    blk_v = pl.BlockSpec((t, RET_DV), lambda hh: (0, hh))
    blk_k = pl.BlockSpec((t, RET_DK), lambda hh: (0, hh))
    return pl.pallas_call(
        body, grid=(RET_HEADS,), in_specs=[q, k, v, g, tab, tab, lg, blk_v, blk_v],
        out_specs=[blk_k, blk_k, blk_v, blk_v],
        out_shape=[jax.ShapeDtypeStruct((t, RET_QK_W), BF16), jax.ShapeDtypeStruct((t, RET_QK_W), BF16),
                   jax.ShapeDtypeStruct((t, RET_V_W), BF16), jax.ShapeDtypeStruct((t, RET_V_W), BF16)],
        scratch_shapes=[pltpu.VMEM((t, RET_DK), BF16), pltpu.VMEM((t, RET_DK), BF16), pltpu.VMEM((t, RET_DV), BF16),
                        pltpu.VMEM((t, RET_DV), BF16), pltpu.VMEM((t, RET_DK), F32), pltpu.VMEM((t, RET_DV), F32)],
        name=name, compiler_params=_cp())(h, h, h, h, c2, s2, lgt, o, dy)


def _rot_b(x, cb, shi, slo):
    return x * cb + pltpu.roll(x, ROPE_DIMS // 2, 1) * shi + pltpu.roll(x, LANES - ROPE_DIMS // 2, 1) * slo


def _rot_b_t(dy, cb, shi, slo):
    return dy * cb + pltpu.roll(dy * shi, LANES - ROPE_DIMS // 2, 1) + pltpu.roll(dy * slo, ROPE_DIMS // 2, 1)


def _dil_specs(t):
    base = (2 * RET_QK_W + 2 * RET_V_W) // LANES
    npair = DIL_W // LANES
    q = pl.BlockSpec((t, LANES), lambda p: (0, base + p))
    k = pl.BlockSpec((t, LANES), lambda p: (0, base + npair + p))
    v = pl.BlockSpec((t, LANES), lambda p: (0, base + 2 * npair + p))
    tab = pl.BlockSpec((t, LANES), lambda p: (0, 0))
    strip = pl.BlockSpec((ATT_BLK, t), lambda p: (0, 0))
    pair = pl.BlockSpec((t, LANES), lambda p: (0, p))
    return q, k, v, tab, strip, pair


def _dil_fwd(h, cb, shi, slo, strip, *, name):
    t = h.shape[0]
    nblk = t // ATT_BLK
    scale = DIL_HD ** -0.5

    def body(q_ref, k_ref, v_ref, cb_ref, shi_ref, slo_ref, st_ref, o_ref, yb_ref, lse_ref, qs, ks, vs):
        cb_, shi_, slo_ = cb_ref[...], shi_ref[...], slo_ref[...]
        lane = lax.broadcasted_iota(jnp.int32, (t, LANES), 1)
        qr = _rot_b(q_ref[...], cb_, shi_, slo_) * scale
        qs[0] = jnp.where(lane < DIL_HD, qr, 0.0).astype(BF16)
        qs[1] = jnp.where(lane >= DIL_HD, qr, 0.0).astype(BF16)
        ks[...] = _rot_b(k_ref[...], cb_, shi_, slo_).astype(BF16)
        vs[...] = v_ref[...].astype(BF16)
        lane_b = lax.broadcasted_iota(jnp.int32, (ATT_BLK, LANES), 1)
        for i in range(nblk):
            w = (i + 1) * ATT_BLK
            rows = pl.ds(i * ATT_BLK, ATT_BLK)
            logc = st_ref[:, t - w:t]
            outs, lses = [], []
            for hd in range(2):
                s = _nt(qs[hd, rows, :], ks[0:w, :]) + logc
                m = jnp.max(s, -1, keepdims=True)
                p = jnp.exp(s - m)
                l = jnp.sum(p, -1, keepdims=True)
                outs.append(_nn(p.astype(BF16), vs[0:w, :]) / l)
                lses.append(m + jnp.log(l))
            o = jnp.where(lane_b < DIL_HD, outs[0], outs[1])
            o_ref[rows, :] = o
            yb_ref[rows, :] = o.astype(BF16)
            lse_ref[rows, :] = jnp.where(lane_b < DIL_HD, lses[0], lses[1])

    q, k, v, tab, strip_spec, pair = _dil_specs(t)
    return pl.pallas_call(
        body, grid=(DIL_W // LANES,), in_specs=[q, k, v, tab, tab, tab, strip_spec], out_specs=[pair, pair, pair],
        out_shape=[jax.ShapeDtypeStruct((t, DIL_W), F32), jax.ShapeDtypeStruct((t, DIL_W), BF16),
                   jax.ShapeDtypeStruct((t, DIL_W), F32)],
        scratch_shapes=[pltpu.VMEM((2, t, LANES), BF16), pltpu.VMEM((t, LANES), BF16), pltpu.VMEM((t, LANES), BF16)],
        name=name, compiler_params=_cp())(h, h, h, cb, shi, slo, strip)


def _dil_bwd(h, cb, shi, slo, strip, o, lse, dy, *, name):
    t = h.shape[0]
    nblk = t // ATT_BLK
    scale = DIL_HD ** -0.5

    def body(q_ref, k_ref, v_ref, cb_ref, shi_ref, slo_ref, st_ref, o_ref, lse_ref, dy_ref,
             dq_ref, dk_ref, dv_ref, qs, ks, vs, dos, dls, dka, dva):
        cb_, shi_, slo_ = cb_ref[...], shi_ref[...], slo_ref[...]
        lane = lax.broadcasted_iota(jnp.int32, (t, LANES), 1)
        qr = _rot_b(q_ref[...], cb_, shi_, slo_) * scale
        qs[0] = jnp.where(lane < DIL_HD, qr, 0.0).astype(BF16)
        qs[1] = jnp.where(lane >= DIL_HD, qr, 0.0).astype(BF16)
        ks[...] = _rot_b(k_ref[...], cb_, shi_, slo_).astype(BF16)
        vs[...] = v_ref[...].astype(BF16)
        do = dy_ref[...]
        prod = do * o_ref[...]
        d0 = jnp.sum(jnp.where(lane < DIL_HD, prod, 0.0), -1, keepdims=True)
        d1 = jnp.sum(jnp.where(lane >= DIL_HD, prod, 0.0), -1, keepdims=True)
        dls[...] = jnp.where(lane < DIL_HD, d0, d1)
        dos[0] = jnp.where(lane < DIL_HD, do, 0.0).astype(BF16)
        dos[1] = jnp.where(lane >= DIL_HD, do, 0.0).astype(BF16)
        dka[...] = jnp.zeros_like(dka)
        dva[...] = jnp.zeros_like(dva)
        lane_b = lax.broadcasted_iota(jnp.int32, (ATT_BLK, LANES), 1)
        for i in range(nblk):
            w = (i + 1) * ATT_BLK
            rows = pl.ds(i * ATT_BLK, ATT_BLK)
            logc = st_ref[:, t - w:t]
            dqs = []
            for hd in range(2):
                col = hd * DIL_HD
                qh = qs[hd, rows, :]
                doh = dos[hd, rows, :]
                lse_h = lse_ref[rows, col:col + 1]
                dl_h = dls[rows, col:col + 1]
                p = jnp.exp(_nt(qh, ks[0:w, :]) + logc - lse_h)
                dp = _nt(doh, vs[0:w, :])
                ds = (p * (dp - dl_h)).astype(BF16)
                dqs.append(_nn(ds, ks[0:w, :]))
                dka[0:w, :] += _tn(ds, qh)
                dva[0:w, :] += _tn(p.astype(BF16), doh)
            dq = jnp.where(lane_b < DIL_HD, dqs[0], dqs[1]) * scale
            dq_ref[rows, :] = _rot_b_t(dq, cb_ref[rows, :], shi_ref[rows, :], slo_ref[rows, :]).astype(BF16)
        dk_ref[...] = _rot_b_t(dka[...], cb_, shi_, slo_).astype(BF16)
        dv_ref[...] = dva[...].astype(BF16)

    q, k, v, tab, strip_spec, pair = _dil_specs(t)
    dy_spec = pl.BlockSpec((t, LANES), lambda p: (0, RET_V_W // LANES + p))
    return pl.pallas_call(
        body, grid=(DIL_W // LANES,), in_specs=[q, k, v, tab, tab, tab, strip_spec, pair, pair, dy_spec],
        out_specs=[pair, pair, pair],
        out_shape=[jax.ShapeDtypeStruct((t, DIL_W), BF16)] * 3,
        scratch_shapes=[pltpu.VMEM((2, t, LANES), BF16), pltpu.VMEM((t, LANES), BF16), pltpu.VMEM((t, LANES), BF16),
                        pltpu.VMEM((2, t, LANES), BF16), pltpu.VMEM((t, LANES), F32),
                        pltpu.VMEM((t, LANES), F32), pltpu.VMEM((t, LANES), F32)],
        name=name, compiler_params=_cp())(h, h, h, cb, shi, slo, strip, o, lse, dy)


def _gdn_prep_fwd(h, cw, *, name):
    t = h.shape[0]
    qscale = GDN_DK ** -0.5

    def body(hq_ref, hk_ref, hv_ref, wq_ref, wk_ref, wv_ref, q_ref, k_ref, v_ref):
        row = lax.broadcasted_iota(jnp.int32, (t, GDN_DK), 0)
        sq = _silu(_dwconv(hq_ref[...], wq_ref, row))
        sk = _silu(_dwconv(hk_ref[...], wk_ref, row))
        q_ref[0] = sq * lax.rsqrt(jnp.sum(sq * sq, -1, keepdims=True) + 1e-6) * qscale
        k_ref[0] = sk * lax.rsqrt(jnp.sum(sk * sk, -1, keepdims=True) + 1e-6)
        v_ref[0] = _silu(_dwconv(hv_ref[...], wv_ref, row))

    hs = lambda off: pl.BlockSpec((t, GDN_DK), lambda i: (0, i + off))
    ws = lambda off: pl.BlockSpec((GDN_CONV, GDN_DK), lambda i: (0, i + off))
    out = pl.BlockSpec((1, t, GDN_DK), lambda i: (i, 0, 0))
    return pl.pallas_call(
        body, grid=(GDN_HEADS,), in_specs=[hs(0), hs(8), hs(16), ws(0), ws(8), ws(16)], out_specs=[out, out, out],
        out_shape=[jax.ShapeDtypeStruct((GDN_HEADS, t, GDN_DK), F32)] * 3,
        name=name, compiler_params=_cp())(h, h, h, cw, cw, cw)


def _gdn_prep_bwd(h, cw, dq, dk, dv, *, name):
    t = h.shape[0]
    qscale = GDN_DK ** -0.5

    def body(hq_ref, hk_ref, hv_ref, wq_ref, wk_ref, wv_ref, dq_ref, dk_ref, dv_ref,
             dhq_ref, dhk_ref, dhv_ref, dwq_ref, dwk_ref, dwv_ref):
        row = lax.broadcasted_iota(jnp.int32, (t, GDN_DK), 0)

        def one(h_ref, w_ref, d_ref, dh_ref, dw_ref, norm, sc):
            u = h_ref[...]
            c = _dwconv(u, w_ref, row)
            d = d_ref[0]
            if norm:
                s = _silu(c)
                r = lax.rsqrt(jnp.sum(s * s, -1, keepdims=True) + 1e-6)
                n = s * r
                d = d * sc
                d = r * (d - n * jnp.sum(d * n, -1, keepdims=True))
            dc = d * _dsilu(c)
            dh_ref[...] = _dwconv_bwd(u, w_ref, dc, row, dw_ref).astype(BF16)

        one(hq_ref, wq_ref, dq_ref, dhq_ref, dwq_ref, True, qscale)
        one(hk_ref, wk_ref, dk_ref, dhk_ref, dwk_ref, True, 1.0)
        one(hv_ref, wv_ref, dv_ref, dhv_ref, dwv_ref, False, 1.0)

    hs = lambda off: pl.BlockSpec((t, GDN_DK), lambda i: (0, i + off))
    ws = lambda off: pl.BlockSpec((GDN_CONV, GDN_DK), lambda i: (0, i + off))
    hd = pl.BlockSpec((1, t, GDN_DK), lambda i: (i, 0, 0))
    return pl.pallas_call(
        body, grid=(GDN_HEADS,), in_specs=[hs(0), hs(8), hs(16), ws(0), ws(8), ws(16), hd, hd, hd],
        out_specs=[hs(0), hs(0), hs(0), ws(0), ws(0), ws(0)],
        out_shape=[jax.ShapeDtypeStruct((t, GDN_W), BF16)] * 3 + [jax.ShapeDtypeStruct((GDN_CONV, GDN_W), F32)] * 3,
        name=name, compiler_params=_cp())(h, h, h, cw, cw, cw, dq, dk, dv)


def _make_mm2(hi):
    def prep(x):
        return x if hi else x.astype(BF16)
    prec = lax.Precision.HIGHEST if hi else None

    def raw(a, b, dims):
        return lax.dot_general(prep(a), prep(b), (dims, ((), ())), precision=prec, preferred_element_type=F32)

    @jax.custom_vjp
    def nn(a, b):
        return raw(a, b, ((1,), (0,)))

    @jax.custom_vjp
    def nt(a, b):
        return raw(a, b, ((1,), (1,)))

    @jax.custom_vjp
    def tn(a, b):
        return raw(a, b, ((0,), (0,)))

    nn.defvjp(lambda a, b: (nn(a, b), (a, b)), lambda r, g: (nt(g, r[1]), tn(r[0], g)))
    nt.defvjp(lambda a, b: (nt(a, b), (a, b)), lambda r, g: (nn(g, r[1]), tn(g, r[0])))
    tn.defvjp(lambda a, b: (tn(a, b), (a, b)), lambda r, g: (nt(r[1], g), nn(r[0], g)))
    return nn, nt, tn


_NN, _NT, _TN = _make_mm2(False)
_NNH, _NTH, _TNH = _make_mm2(True)


@jax.custom_vjp
def _inv_unit_lower(l):
    c = l.shape[0]
    eye = (lax.broadcasted_iota(jnp.int32, (c, c), 0) == lax.broadcasted_iota(jnp.int32, (c, c), 1)).astype(F32)
    p = -l
    t = eye + p
    for _ in range(int(math.log2(c)) - 1):
        p = _NNH(p, p)
        t = t + _NNH(t, p)
    return t


def _inv_fwd(l):
    t = _inv_unit_lower(l)
    return t, t


def _inv_bwd(t, dt):
    return (-_NTH(_TNH(t, dt), t),)


_inv_unit_lower.defvjp(_inv_fwd, _inv_bwd)


def _softplus(x):
    return jnp.maximum(x, 0.0) + jnp.log1p(jnp.exp(-jnp.abs(x)))


def _gdn_chunk(q, k, v, braw, araw, alog, dtb, state):
    c = q.shape[0]
    ri = lax.broadcasted_iota(jnp.int32, (c, c), 0)
    ci = lax.broadcasted_iota(jnp.int32, (c, c), 1)
    tri = ri >= ci
    strict = ri > ci
    eye = (ri == ci).astype(F32)
    beta = _sig(braw)
    g = -jnp.exp(alog) * _softplus(araw + dtb)
    gcm = _NNH(tri.astype(F32), g * jnp.ones((c, c), F32))
    gct = _NTH(eye, gcm)
    decay = jnp.where(tri, jnp.exp(jnp.where(tri, gcm - gct, 0.0)), 0.0)
    gc = jnp.sum(gcm, 1, keepdims=True) * (1.0 / c)
    glast = jnp.sum(g, 0, keepdims=True)
    egc = jnp.exp(gc)
    kb = k * beta
    tm = _inv_unit_lower(jnp.where(strict, _NT(kb, k) * decay, 0.0))
    u = _NNH(tm, v * beta)
    w = _NNH(tm, kb * egc)
    attn = jnp.where(tri, _NT(q, k) * decay, 0.0)
    k_dec = k * jnp.exp(glast - gc)
    q_dec = q * egc
    v_new = u - _NN(w, state)
    o = _NN(q_dec, state) + _NN(attn, v_new)
    new_state = state * jnp.exp(glast) + _TN(k_dec, v_new)
    return o, new_state


def _gdn_specs(t, rev):
    nch = t // GDN_CHUNK
    cm = (lambda n: nch - 1 - n) if rev else (lambda n: n)
    tok = pl.BlockSpec((1, GDN_CHUNK, GDN_DK), lambda h, n: (h, cm(n), 0))
    par = pl.BlockSpec((1, 1, LANES), lambda h, n: (h, 0, 0))
    st = pl.BlockSpec((1, 1, GDN_DK, GDN_DV), lambda h, n: (h, cm(n), 0, 0))
    return tok, par, st


def _gdn_core_fwd(q, k, v, bb, ab, alog, dtb, *, name):
    t = q.shape[1]
    nch = t // GDN_CHUNK

    def body(q_ref, k_ref, v_ref, bb_ref, ab_ref, al_ref, dt_ref, o_ref, st_ref, state):
        @pl.when(pl.program_id(1) == 0)
        def _():
            state[...] = jnp.zeros_like(state)

        s0 = state[...]
        st_ref[0, 0] = s0
        o, s1 = _gdn_chunk(q_ref[0], k_ref[0], v_ref[0], bb_ref[0, :, 0:1], ab_ref[0, :, 0:1],
                           al_ref[0, :, 0:1], dt_ref[0, :, 0:1], s0)
        o_ref[0] = o
        state[...] = s1

    tok, par, st = _gdn_specs(t, False)
    return pl.pallas_call(
        body, grid=(GDN_HEADS, nch), in_specs=[tok, tok, tok, tok, tok, par, par], out_specs=[tok, st],
        out_shape=[jax.ShapeDtypeStruct((GDN_HEADS, t, GDN_DV), F32),
                   jax.ShapeDtypeStruct((GDN_HEADS, nch, GDN_DK, GDN_DV), F32)],
        scratch_shapes=[pltpu.VMEM((GDN_DK, GDN_DV), F32)],
        name=name, compiler_params=_cp())(q, k, v, bb, ab, alog, dtb)


def _gdn_core_bwd(q, k, v, bb, ab, alog, dtb, states, do, *, name):
    t = q.shape[1]
    nch = t // GDN_CHUNK

    def body(q_ref, k_ref, v_ref, bb_ref, ab_ref, al_ref, dt_ref, st_ref, do_ref,
             dq_ref, dk_ref, dv_ref, dbb_ref, dab_ref, dal_ref, ddt_ref, dstate):
        @pl.when(pl.program_id(1) == 0)
        def _():
            dstate[...] = jnp.zeros_like(dstate)
            dal_ref[...] = jnp.zeros_like(dal_ref)
            ddt_ref[...] = jnp.zeros_like(ddt_ref)

        args = (q_ref[0], k_ref[0], v_ref[0], bb_ref[0, :, 0:1], ab_ref[0, :, 0:1],
                al_ref[0, :, 0:1], dt_ref[0, :, 0:1], st_ref[0, 0])
        _, pull = jax.vjp(_gdn_chunk, *args)
        dq, dk, dv, dbr, dar, dal, ddt, ds = pull((do_ref[0], dstate[...]))
        dq_ref[0] = dq
        dk_ref[0] = dk
        dv_ref[0] = dv
        dbb_ref[0] = dbr + jnp.zeros((GDN_CHUNK, LANES), F32)
        dab_ref[0] = dar + jnp.zeros((GDN_CHUNK, LANES), F32)
        dal_ref[0] += dal + jnp.zeros((1, LANES), F32)
        ddt_ref[0] += ddt + jnp.zeros((1, LANES), F32)
        dstate[...] = ds

    tok, par, st = _gdn_specs(t, True)
    tokshape = jax.ShapeDtypeStruct((GDN_HEADS, t, GDN_DK), F32)
    parshape = jax.ShapeDtypeStruct((GDN_HEADS, 1, LANES), F32)
    return pl.pallas_call(
        body, grid=(GDN_HEADS, nch), in_specs=[tok, tok, tok, tok, tok, par, par, st, tok],
        out_specs=[tok, tok, tok, tok, tok, par, par],
        out_shape=[tokshape] * 5 + [parshape] * 2,
        scratch_shapes=[pltpu.VMEM((GDN_DK, GDN_DV), F32)],
        name=name, compiler_params=_cp())(q, k, v, bb, ab, alog, dtb, states, do)


GDN_ROWS = 512


def _gdn_post_fwd(o, h, nw, *, name):
    t = o.shape[1]

    def body(o_ref, g_ref, nw_ref, y_ref):
        oo = o_ref[0]
        r = lax.rsqrt(jnp.mean(oo * oo, -1, keepdims=True) + EPS)
        y_ref[...] = (oo * r * nw_ref[...] * _silu(g_ref[...])).astype(BF16)

    return pl.pallas_call(
        body, grid=(GDN_HEADS, t // GDN_ROWS),
        in_specs=[pl.BlockSpec((1, GDN_ROWS, GDN_DV), lambda hh, i: (hh, i, 0)),
                  pl.BlockSpec((GDN_ROWS, GDN_DV), lambda hh, i: (i, 3 * GDN_HEADS + hh)),
                  pl.BlockSpec((1, GDN_DV), lambda hh, i: (0, 0))],
        out_specs=pl.BlockSpec((GDN_ROWS, GDN_DV), lambda hh, i: (i, hh)),
        out_shape=jax.ShapeDtypeStruct((t, GDN_W), BF16), name=name, compiler_params=_cp())(o, h, nw)


def _gdn_post_bwd(o, h, nw, dy, *, name):
    t = o.shape[1]

    def body(o_ref, g_ref, nw_ref, dy_ref, do_ref, dg_ref, dnw_ref):
        oo, gg, nw_, dy_ = o_ref[0], g_ref[...], nw_ref[...], dy_ref[...]
        r = lax.rsqrt(jnp.mean(oo * oo, -1, keepdims=True) + EPS)
        n = oo * r
        sg = _silu(gg)
        dg_ref[...] = (dy_ * n * nw_ * _dsilu(gg)).astype(BF16)
        dn = dy_ * sg * nw_
        do_ref[0] = r * (dn - n * jnp.mean(dn * n, -1, keepdims=True))

        @pl.when((pl.program_id(0) == 0) & (pl.program_id(1) == 0))
        def _():
            dnw_ref[...] = jnp.zeros_like(dnw_ref)

        dnw_ref[...] += jnp.sum(dy_ * sg * n, 0, keepdims=True)

    return pl.pallas_call(
        body, grid=(GDN_HEADS, t // GDN_ROWS),
        in_specs=[pl.BlockSpec((1, GDN_ROWS, GDN_DV), lambda hh, i: (hh, i, 0)),
                  pl.BlockSpec((GDN_ROWS, GDN_DV), lambda hh, i: (i, 3 * GDN_HEADS + hh)),
                  pl.BlockSpec((1, GDN_DV), lambda hh, i: (0, 0)),
                  pl.BlockSpec((GDN_ROWS, GDN_DV), lambda hh, i: (i, hh))],
        out_specs=[pl.BlockSpec((1, GDN_ROWS, GDN_DV), lambda hh, i: (hh, i, 0)),
                   pl.BlockSpec((GDN_ROWS, GDN_DV), lambda hh, i: (i, hh)),
                   pl.BlockSpec((1, GDN_DV), lambda hh, i: (0, 0))],
        out_shape=[jax.ShapeDtypeStruct((GDN_HEADS, t, GDN_DV), F32), jax.ShapeDtypeStruct((t, GDN_W), BF16),
                   jax.ShapeDtypeStruct((1, GDN_DV), F32)],
        name=name, compiler_params=_cp())(o, h, nw, dy)


def _tables(positions):
    pos = positions.astype(F32)[:, None]
    half = RET_DK // 2
    inv = jnp.power(RET_THETA, -jnp.arange(half, dtype=F32) * 2.0 / RET_DK)
    ang = pos * inv
    cos, sin = jnp.cos(ang), jnp.sin(ang)
    c2a = jnp.concatenate([cos, cos], 1)
    s2a = jnp.concatenate([-sin, sin], 1)
    hb = ROPE_DIMS // 2
    invb = jnp.power(ROPE_THETA, -jnp.arange(hb, dtype=F32) * 2.0 / ROPE_DIMS)
    angb = pos * invb
    cosb, sinb = jnp.cos(angb), jnp.sin(angb)
    t = pos.shape[0]
    ones = jnp.ones((t, DIL_HD - ROPE_DIMS), F32)
    zeros = jnp.zeros((t, DIL_HD - ROPE_DIMS), F32)
    z8 = jnp.zeros((t, hb), F32)
    cb = jnp.concatenate([cosb, cosb, ones] * 2, 1)
    shi = jnp.concatenate([z8, sinb, zeros] * 2, 1)
    slo = jnp.concatenate([-sinb, z8, zeros] * 2, 1)
    lg = jnp.log1p(-jnp.power(2.0, -5.0 - jnp.arange(RET_HEADS, dtype=F32)))
    lgt = jnp.broadcast_to(lg[:, None, None], (RET_HEADS, 1, LANES))
    delta = jnp.arange(ATT_BLK, dtype=jnp.int32)[:, None] + (SEQ - ATT_BLK) - jnp.arange(SEQ, dtype=jnp.int32)[None, :]
    cnt = jnp.zeros(delta.shape, F32)
    for (w, d) in DIL_PAIRS:
        cnt = cnt + ((delta >= 0) & (delta <= w) & (delta % d == 0)).astype(F32)
    strip = jnp.where(cnt > 0, jnp.log(jnp.maximum(cnt, 1.0)), NEG)
    return c2a, s2a, cb, shi, slo, lgt, strip


def _local_step(x, positions, target, W, small):
    c2a, s2a, cb, shi, slo, lgt, strip = _tables(positions)
    t = x.shape[0]
    saved = []
    xf = x
    xb = x.astype(BF16)
    for layer in range(DEPTH):
        j = layer // 2
        L = f"L{layer}_"
        rec = {"x": xf, "xb": xb}
        if layer % 2 == 0:
            h = _mm(xb, W["ev_w_in"][j], name=L + "ev_in")
            ro, ya = _ret_fwd(h, c2a, s2a, lgt, name=L + "ret_fwd")
            do_, yb, lse = _dil_fwd(h, cb, shi, slo, strip, name=L + "dil_fwd")
            y = jnp.concatenate([ya, yb], 1)
            mix = _mm(y, W["ev_w_out"][j], name=L + "ev_out")
            rec.update(h=h, ro=ro, dil_o=do_, lse=lse, y=y)
        else:
            h = _mm(xb, W["od_w_in"][j], name=L + "od_in")
            cw = W["od_conv_w"][j]
            q, k, v = _gdn_prep_fwd(h, cw, name=L + "gdn_prep")
            hs = h[:, 4 * GDN_W:4 * GDN_W + 2 * GDN_HEADS]
            bb = jnp.broadcast_to(hs[:, :GDN_HEADS].T[:, :, None], (GDN_HEADS, t, LANES))
            ab = jnp.broadcast_to(hs[:, GDN_HEADS:].T[:, :, None], (GDN_HEADS, t, LANES))
            alog = jnp.broadcast_to(small["od_a_log"][j][:, None, None], (GDN_HEADS, 1, LANES))
            dtb = jnp.broadcast_to(small["od_dt_bias"][j][:, None, None], (GDN_HEADS, 1, LANES))
            o, states = _gdn_core_fwd(q, k, v, bb, ab, alog, dtb, name=L + "gdn_fwd")
            nw = small["od_norm_w"][j][None, :]
            y = _gdn_post_fwd(o, h, nw, name=L + "gdn_post")
            mix = _mm(y, W["od_w_out"][j], name=L + "od_out")
            rec.update(h=h, q=q, k=k, v=v, bb=bb, ab=ab, alog=alog, dtb=dtb, states=states, o=o, y=y, nw=nw, cw=cw)
        z1, x1, x1b = _ln_fwd(xf, mix, small["ln1_g"][layer][None], small["ln1_b"][layer][None], name=L + "ln1")
        u = _mm(x1b, W["ffn_w_up"][layer], name=L + "ffn_up")
        fcw = W["ffn_conv_w"][layer]
        fcb = small["ffn_conv_b"][layer][None]
        a = _ffn_mid_fwd(u, fcw, fcb, name=L + "ffn_mid")
        f = _mm(a, W["ffn_w_down"][layer], name=L + "ffn_down")
        z2, x2, x2b = _ln_fwd(x1, f, small["ln2_g"][layer][None], small["ln2_b"][layer][None], name=L + "ln2")
        rec.update(z1=z1, x1b=x1b, u=u, a=a, z2=z2, fcw=fcw, fcb=fcb)
        saved.append(rec)
        xf, xb = x2, x2b

    dy, lossv = _loss_head(xf, target, name="loss_head")
    loss = lossv[0, 0]

    gW = {n: [None] * W[n].shape[0] for n in W}
    gS = {n: [None] * small[n].shape[0] for n in small}
    dres, dmm = dy, None
    for layer in reversed(range(DEPTH)):
        j = layer // 2
        L = f"L{layer}_"
        rec = saved[layer]
        if dmm is None:
            dz2, dz2b, dg2, db2 = _ln_bwd(rec["z2"], small["ln2_g"][layer][None], dres, None, name=L + "ln2_bwd")
        else:
            dz2, dz2b, dg2, db2 = _ln_bwd(rec["z2"], small["ln2_g"][layer][None], dmm, dres, name=L + "ln2_bwd")
        gS["ln2_g"][layer], gS["ln2_b"][layer] = dg2[0], db2[0]
        gW["ffn_w_down"][layer] = _mm(rec["a"], dz2b, ta=True, name=L + "ffn_down_dw")
        da = _mm(dz2b, W["ffn_w_down"][layer], tb=True, name=L + "ffn_down_dx")
        du, dcw, dcb = _ffn_mid_bwd(rec["u"], rec["fcw"], rec["fcb"], da, name=L + "ffn_mid_bwd")
        gW["ffn_conv_w"][layer] = dcw
        gS["ffn_conv_b"][layer] = dcb[0]
        gW["ffn_w_up"][layer] = _mm(rec["x1b"], du, ta=True, name=L + "ffn_up_dw")
        dx1 = _mm(du, W["ffn_w_up"][layer], tb=True, name=L + "ffn_up_dx")
        dz1, dz1b, dg1, db1 = _ln_bwd(rec["z1"], small["ln1_g"][layer][None], dx1, dz2, name=L + "ln1_bwd")
        gS["ln1_g"][layer], gS["ln1_b"][layer] = dg1[0], db1[0]
        if layer % 2 == 0:
            gW["ev_w_out"][j] = _mm(rec["y"], dz1b, ta=True, name=L + "ev_out_dw")
            dyy = _mm(dz1b, W["ev_w_out"][j], tb=True, name=L + "ev_out_dx")
            dqa, dka, dva, dga = _ret_bwd(rec["h"], c2a, s2a, lgt, rec["ro"], dyy, name=L + "ret_bwd")
            dqb, dkb, dvb = _dil_bwd(rec["h"], cb, shi, slo, strip, rec["dil_o"], rec["lse"], dyy, name=L + "dil_bwd")
            dh = jnp.concatenate([dqa, dka, dva, dga, dqb, dkb, dvb], 1)
            gW["ev_w_in"][j] = _mm(rec["xb"], dh, ta=True, name=L + "ev_in_dw")
            dxin = _mm(dh, W["ev_w_in"][j], tb=True, name=L + "ev_in_dx")
        else:
            gW["od_w_out"][j] = _mm(rec["y"], dz1b, ta=True, name=L + "od_out_dw")
            dyy = _mm(dz1b, W["od_w_out"][j], tb=True, name=L + "od_out_dx")
            do, dgate, dnw = _gdn_post_bwd(rec["o"], rec["h"], rec["nw"], dyy, name=L + "gdn_post_bwd")
            gS["od_norm_w"][j] = dnw[0]
            dq, dk, dv, dbb, dab, dal, ddt = _gdn_core_bwd(
                rec["q"], rec["k"], rec["v"], rec["bb"], rec["ab"], rec["alog"], rec["dtb"], rec["states"], do,
                name=L + "gdn_bwd")
            gS["od_a_log"][j] = dal[:, 0, 0]
            gS["od_dt_bias"][j] = ddt[:, 0, 0]
            dhq, dhk, dhv, dwq, dwk, dwv = _gdn_prep_bwd(rec["h"], rec["cw"], dq, dk, dv, name=L + "gdn_prep_bwd")
            gW["od_conv_w"][j] = jnp.concatenate([dwq, dwk, dwv], 1)
            dsm = jnp.concatenate([dbb[:, :, 0].T, dab[:, :, 0].T,
                                   jnp.zeros((t, LANES - 2 * GDN_HEADS), F32)], 1).astype(BF16)
            dh = jnp.concatenate([dhq, dhk, dhv, dgate, dsm], 1)
            gW["od_w_in"][j] = _mm(rec["xb"], dh, ta=True, name=L + "od_in_dw")
            dxin = _mm(dh, W["od_w_in"][j], tb=True, name=L + "od_in_dx")
        dres, dmm = dz1, dxin
    grad_x = _axpy(dmm, dres, name="grad_x")
    gW = {n: jnp.stack(v) for n, v in gW.items()}
    gS = {n: jnp.stack(v) for n, v in gS.items()}
    return loss, grad_x, gW, gS


HBM = pl.BlockSpec(memory_space=pltpu.HBM)


def _me():
    return lax.axis_index("x"), lax.axis_index("y"), lax.axis_index("c")


def _all_gather(shards, *, name):
    n = len(shards)

    def body(*refs):
        ins, outs = refs[:n], refs[n:2 * n]
        send_sems, recv_sems, local_sems = refs[2 * n:]
        x, y, c = _me()
        me, sibling = (x, y, c), (x, y, 1 - c)
        chips = [(1 - x, y), (x, 1 - y), (1 - x, 1 - y)]

        def slot(out, px, py, pc):
            return out.at[4 * px + 2 * py + pc]

        def copy(a, kk, block, to, src=None):
            return pltpu.make_async_remote_copy(
                src_ref=slot(outs[a], *block) if src is None else src, dst_ref=slot(outs[a], *block),
                send_sem=send_sems.at[a, kk], recv_sem=recv_sems.at[a, kk], device_id=to, device_id_type=MESH)

        mine = [pltpu.make_async_copy(ins[a], slot(outs[a], *me), local_sems.at[a]) for a in range(n)]
        for cp in mine:
            cp.start()
        first = []
        for a in range(n):
            first.append(copy(a, 0, me, sibling, src=ins[a]))
            first += [copy(a, 1 + jj, me, (*chip, c), src=ins[a]) for jj, chip in enumerate(chips)]
        for cp in first:
            cp.start()
        passed = []
        for jj, chip in enumerate(chips):
            for a in range(n):
                copy(a, 1 + jj, (*chip, c), me).wait_recv()
                cp = copy(a, 4 + jj, (*chip, c), sibling)
                cp.start()
                passed.append(cp)
        for a in range(n):
            copy(a, 0, sibling, me).wait_recv()
            for jj, chip in enumerate(chips):
                copy(a, 4 + jj, (*chip, 1 - c), me).wait_recv()
        for cp in first + passed:
            cp.wait_send()
        for cp in mine:
            cp.wait()

    return pl.pallas_call(
        body, in_specs=[HBM] * n, out_specs=[HBM] * n,
        out_shape=[jax.ShapeDtypeStruct((N_DEV, *s.shape), s.dtype) for s in shards],
        scratch_shapes=[pltpu.SemaphoreType.DMA((n, 7)), pltpu.SemaphoreType.DMA((n, 7)), pltpu.SemaphoreType.DMA((n,))],
        name=name, compiler_params=pltpu.CompilerParams(has_side_effects=True))(*shards)


def _sibling_exchange(gs, *, name):
    n = len(gs)

    def body(*refs):
        ins, outs = refs[:n], refs[n:2 * n]
        send_sems, recv_sems = refs[2 * n:]
        x, y, c = _me()
        cps = [pltpu.make_async_remote_copy(
            src_ref=ins[a].at[:, 1 - c], dst_ref=outs[a], send_sem=send_sems.at[a], recv_sem=recv_sems.at[a],
            device_id=(x, y, 1 - c), device_id_type=MESH) for a in range(n)]
        for cp in cps:
            cp.start()
        for cp in cps:
            cp.wait()

    return pl.pallas_call(
        body, in_specs=[HBM] * n, out_specs=[HBM] * n,
        out_shape=[jax.ShapeDtypeStruct((4, *g.shape[2:]), g.dtype) for g in gs],
        scratch_shapes=[pltpu.SemaphoreType.DMA((n,)), pltpu.SemaphoreType.DMA((n,))],
        name=name, compiler_params=pltpu.CompilerParams(has_side_effects=True))(*gs)


def _pair_add(g, r, cidx, *, name):
    _, _, rr, cc = g.shape
    tr = rr
    for cand in (512, 384, 256, 192, 176, 128, 64, 32, 16, 8):
        if rr % cand == 0:
            tr = cand
            break
    if rr < 8:
        tr = rr

    def body(c_ref, g_ref, r_ref, ob_ref, of_ref):
        s = g_ref[0, 0] + r_ref[0]
        ob_ref[0] = s.astype(BF16)
        of_ref[0] = s

    grid_spec = pltpu.PrefetchScalarGridSpec(
        num_scalar_prefetch=1, grid=(4, rr // tr),
        in_specs=[pl.BlockSpec((1, 1, tr, cc), lambda kk, i, c_ref: (kk, c_ref[0], i, 0)),
                  pl.BlockSpec((1, tr, cc), lambda kk, i, c_ref: (kk, i, 0))],
        out_specs=[pl.BlockSpec((1, tr, cc), lambda kk, i, c_ref: (kk, i, 0)),
                   pl.BlockSpec((1, tr, cc), lambda kk, i, c_ref: (kk, i, 0))])
    return pl.pallas_call(
        body, grid_spec=grid_spec,
        out_shape=[jax.ShapeDtypeStruct((4, rr, cc), BF16), jax.ShapeDtypeStruct((4, rr, cc), F32)],
        name=name, compiler_params=_cp())(cidx, g, r)


def _chip_exchange(ps, *, name):
    n = len(ps)

    def body(*refs):
        ins, outs = refs[:n], refs[n:2 * n]
        send_sems, recv_sems = refs[2 * n:]
        x, y, c = _me()
        chips = [(1 - x, y), (x, 1 - y), (1 - x, 1 - y)]
        cps = []
        for a in range(n):
            for jj, (px, py) in enumerate(chips):
                cps.append(pltpu.make_async_remote_copy(
                    src_ref=ins[a].at[2 * px + py], dst_ref=outs[a].at[jj],
                    send_sem=send_sems.at[a, jj], recv_sem=recv_sems.at[a, jj],
                    device_id=(px, py, c), device_id_type=MESH))
        for cp in cps:
            cp.start()
        for cp in cps:
            cp.wait()

    return pl.pallas_call(
        body, in_specs=[HBM] * n, out_specs=[HBM] * n,
        out_shape=[jax.ShapeDtypeStruct((3, *p.shape[1:]), p.dtype) for p in ps],
        scratch_shapes=[pltpu.SemaphoreType.DMA((n, 3)), pltpu.SemaphoreType.DMA((n, 3))],
        name=name, compiler_params=pltpu.CompilerParams(has_side_effects=True))(*ps)


def _small_exchange(vec, *, name):
    rr = vec.shape[0]

    def body(v_ref, o_ref, send_sems, recv_sems):
        x, y, c = _me()
        myid = 4 * x + 2 * y + c
        o_ref[myid] = v_ref[...]
        cps = []
        for kk in range(1, N_DEV):
            px, py, pc = x ^ (kk >> 2), y ^ ((kk >> 1) & 1), c ^ (kk & 1)
            cps.append(pltpu.make_async_remote_copy(
                src_ref=v_ref, dst_ref=o_ref.at[myid], send_sem=send_sems.at[kk], recv_sem=recv_sems.at[kk],
                device_id=(px, py, pc), device_id_type=MESH))
        for cp in cps:
            cp.start()
        for kk in range(1, N_DEV):
            px, py, pc = x ^ (kk >> 2), y ^ ((kk >> 1) & 1), c ^ (kk & 1)
            pltpu.make_async_remote_copy(
                src_ref=v_ref, dst_ref=o_ref.at[4 * px + 2 * py + pc], send_sem=send_sems.at[kk],
                recv_sem=recv_sems.at[kk], device_id=(px, py, pc), device_id_type=MESH).wait_recv()
        for cp in cps:
            cp.wait_send()

    return pl.pallas_call(
        body, in_specs=[pl.BlockSpec(memory_space=pltpu.VMEM)], out_specs=pl.BlockSpec(memory_space=pltpu.VMEM),
        out_shape=jax.ShapeDtypeStruct((N_DEV, rr, LANES), F32),
        scratch_shapes=[pltpu.SemaphoreType.DMA((N_DEV,)), pltpu.SemaphoreType.DMA((N_DEV,))],
        name=name, compiler_params=pltpu.CompilerParams(has_side_effects=True))(vec)


def _adam_math(w, g, m, v):
    m = ADAM_B1 * m + (1.0 - ADAM_B1) * g
    v = ADAM_B2 * v + (1.0 - ADAM_B2) * (g * g)
    m_hat = m / (1.0 - ADAM_B1 ** ADAM_STEP)
    v_hat = v / (1.0 - ADAM_B2 ** ADAM_STEP)
    delta = -ADAM_LR * (m_hat / (jnp.sqrt(v_hat) + ADAM_EPS) + ADAM_WD * w)
    return delta, m, v


def _adamw_sharded(w, m, v, own, recv, *, name):
    rr, cc = w.shape
    tr = rr
    for cand in (512, 384, 256, 192, 176, 128, 64, 32, 16, 8):
        if rr % cand == 0:
            tr = cand
            break

    def body(w_ref, m_ref, v_ref, own_ref, r_ref, g_ref, d_ref, nm_ref, nv_ref):
        g = ((own_ref[...] + r_ref[0].astype(F32)) + r_ref[1].astype(F32)) + r_ref[2].astype(F32)
        d, nm, nv = _adam_math(w_ref[...], g, m_ref[...], v_ref[...])
        g_ref[...] = g
        d_ref[...] = d
        nm_ref[...] = nm
        nv_ref[...] = nv

    blk = pl.BlockSpec((tr, cc), lambda i: (i, 0))
    sh = jax.ShapeDtypeStruct((rr, cc), F32)
    return pl.pallas_call(
        body, grid=(rr // tr,), in_specs=[blk, blk, blk, blk, pl.BlockSpec((3, tr, cc), lambda i: (0, i, 0))],
        out_specs=[blk] * 4, out_shape=[sh] * 4, name=name, compiler_params=_cp())(w, m, v, own, recv)


def _adamw_small(w, m, v, gall, *, name):
    rr = w.shape[0]

    def body(w_ref, m_ref, v_ref, g_ref, go_ref, d_ref, nm_ref, nv_ref):
        g = g_ref[0]
        for kk in range(1, N_DEV):
            g = g + g_ref[kk]
        d, nm, nv = _adam_math(w_ref[...], g, m_ref[...], v_ref[...])
        go_ref[...] = g
        d_ref[...] = d
        nm_ref[...] = nm
        nv_ref[...] = nv

    sh = jax.ShapeDtypeStruct((rr, LANES), F32)
    return pl.pallas_call(body, out_shape=[sh] * 4, name=name, compiler_params=_cp())(w, m, v, gall)


COL_SHARDED = ("ev_w_in", "od_w_in", "od_conv_w", "ffn_w_up", "ffn_conv_w")
ROW_SHARDED = ("ev_w_out", "od_w_out", "ffn_w_down")
BIG = ("ev_w_in", "ev_w_out", "od_w_in", "od_w_out", "ffn_w_up", "ffn_w_down")
SHARDED = ("ev_w_in", "ev_w_out", "od_w_in", "od_conv_w", "od_w_out", "ffn_w_up", "ffn_conv_w", "ffn_w_down")
SMALL = ("od_a_log", "od_dt_bias", "od_norm_w", "ffn_conv_b", "ln1_g", "ln1_b", "ln2_g", "ln2_b")
ALL_W = ("ev_w_in", "ev_w_out", "od_w_in", "od_conv_w", "od_a_log", "od_dt_bias", "od_norm_w", "od_w_out",
         "ffn_w_up", "ffn_conv_w", "ffn_conv_b", "ffn_w_down", "ln1_g", "ln1_b", "ln2_g", "ln2_b")


def _unshard(name, g):
    if name in COL_SHARDED:
        _, l, r, c = g.shape
        return jnp.transpose(g, (1, 2, 0, 3)).reshape(l, r, N_DEV * c)
    _, l, r, c = g.shape
    return jnp.transpose(g, (1, 0, 2, 3)).reshape(l, N_DEV * r, c)


def _reshard(name, full):
    l = full.shape[0]
    if name in COL_SHARDED:
        r, c8 = full.shape[1:]
        return jnp.transpose(full.reshape(l, r, N_DEV, c8 // N_DEV), (2, 0, 1, 3))
    r8, c = full.shape[1:]
    return jnp.transpose(full.reshape(l, N_DEV, r8 // N_DEV, c), (1, 0, 2, 3))


def _pack_small(d):
    flat = jnp.concatenate([d[n].reshape(-1) for n in SMALL])
    pad = (-flat.shape[0]) % (8 * LANES)
    return jnp.pad(flat, (0, pad)).reshape(-1, LANES)


def _unpack_small(packed, like):
    flat = packed.reshape(-1)
    out, off = {}, 0
    for n in SMALL:
        sz = int(np.prod(like[n].shape))
        out[n] = flat[off:off + sz].reshape(like[n].shape)
        off += sz
    return out


def kernel(x, positions, ev_w_in, ev_w_out, od_w_in, od_conv_w, od_a_log, od_dt_bias, od_norm_w, od_w_out, ffn_w_up, ffn_conv_w, ffn_conv_b, ffn_w_down, ln1_g, ln1_b, ln2_g, ln2_b, loss_target, m_ev_w_in, m_ev_w_out, m_od_w_in, m_od_conv_w, m_od_a_log, m_od_dt_bias, m_od_norm_w, m_od_w_out, m_ffn_w_up, m_ffn_conv_w, m_ffn_conv_b, m_ffn_w_down, m_ln1_g, m_ln1_b, m_ln2_g, m_ln2_b, v_ev_w_in, v_ev_w_out, v_od_w_in, v_od_conv_w, v_od_a_log, v_od_dt_bias, v_od_norm_w, v_od_w_out, v_ffn_w_up, v_ffn_conv_w, v_ffn_conv_b, v_ffn_w_down, v_ln1_g, v_ln1_b, v_ln2_g, v_ln2_b):
    w = dict(ev_w_in=ev_w_in, ev_w_out=ev_w_out, od_w_in=od_w_in, od_conv_w=od_conv_w, od_a_log=od_a_log,
             od_dt_bias=od_dt_bias, od_norm_w=od_norm_w, od_w_out=od_w_out, ffn_w_up=ffn_w_up, ffn_conv_w=ffn_conv_w,
             ffn_conv_b=ffn_conv_b, ffn_w_down=ffn_w_down, ln1_g=ln1_g, ln1_b=ln1_b, ln2_g=ln2_g, ln2_b=ln2_b)
    mom = dict(ev_w_in=m_ev_w_in, ev_w_out=m_ev_w_out, od_w_in=m_od_w_in, od_conv_w=m_od_conv_w, od_a_log=m_od_a_log,
               od_dt_bias=m_od_dt_bias, od_norm_w=m_od_norm_w, od_w_out=m_od_w_out, ffn_w_up=m_ffn_w_up,
               ffn_conv_w=m_ffn_conv_w, ffn_conv_b=m_ffn_conv_b, ffn_w_down=m_ffn_w_down, ln1_g=m_ln1_g,
               ln1_b=m_ln1_b, ln2_g=m_ln2_g, ln2_b=m_ln2_b)
    var = dict(ev_w_in=v_ev_w_in, ev_w_out=v_ev_w_out, od_w_in=v_od_w_in, od_conv_w=v_od_conv_w, od_a_log=v_od_a_log,
               od_dt_bias=v_od_dt_bias, od_norm_w=v_od_norm_w, od_w_out=v_od_w_out, ffn_w_up=v_ffn_w_up,
               ffn_conv_w=v_ffn_conv_w, ffn_conv_b=v_ffn_conv_b, ffn_w_down=v_ffn_w_down, ln1_g=v_ln1_g,
               ln1_b=v_ln1_b, ln2_g=v_ln2_g, ln2_b=v_ln2_b)

    shards = [w[n].astype(BF16) if n in BIG else w[n] for n in SHARDED]
    gathered = _all_gather(shards, name="weights_all_gather")
    full = {n: _unshard(n, g) for n, g in zip(SHARDED, gathered)}
    full["od_w_in"] = jnp.pad(full["od_w_in"], ((0, 0), (0, 0), (0, OD_IN_PAD - OD_IN)))
    small = {n: w[n] for n in SMALL}

    loss, grad_x, gW, gS = _local_step(x[0], positions[0], loss_target[0], full, small)
    gW["od_w_in"] = gW["od_w_in"][:, :, :OD_IN]
    loss = lax.psum(loss, ("x", "y", "c"))

    cidx = lax.axis_index("c").astype(jnp.int32).reshape(1)
    by_owner = [_reshard(n, gW[n]) for n in SHARDED]
    by_owner = [g.reshape(4, 2, -1, g.shape[-1]) for g in by_owner]
    from_sibling = _sibling_exchange(by_owner, name="grads_sibling_exchange")
    sums = [_pair_add(g, r, cidx, name=f"grads_pair_add_{n}") for n, g, r in zip(SHARDED, by_owner, from_sibling)]
    from_chips = _chip_exchange([s[0] for s in sums], name="grads_chip_exchange")
    mychip = 2 * lax.axis_index("x") + lax.axis_index("y")

    outs_g, outs_d, outs_m, outs_v = {}, {}, {}, {}
    for n, s, r in zip(SHARDED, sums, from_chips):
        own = lax.dynamic_index_in_dim(s[1], mychip, 0, keepdims=False)
        shp = w[n].shape
        two = lambda a: a.reshape(-1, shp[-1])
        g, d, nm, nv = _adamw_sharded(two(w[n]), two(mom[n]), two(var[n]), own, r, name=f"adamw_{n}")
        outs_g[n], outs_d[n], outs_m[n], outs_v[n] = g.reshape(shp), d.reshape(shp), nm.reshape(shp), nv.reshape(shp)

    gall = _small_exchange(_pack_small(gS), name="small_grads_exchange")
    g, d, nm, nv = _adamw_small(_pack_small({n: w[n] for n in SMALL}), _pack_small({n: mom[n] for n in SMALL}),
                                _pack_small({n: var[n] for n in SMALL}), gall, name="adamw_small")
    for dst, packed in ((outs_g, g), (outs_d, d), (outs_m, nm), (outs_v, nv)):
        dst.update(_unpack_small(packed, {n: w[n] for n in SMALL}))

    return (loss, grad_x[None], *[outs_g[n] for n in ALL_W], *[outs_d[n] for n in ALL_W],
            *[outs_m[n] for n in ALL_W], *[outs_v[n] for n in ALL_W])
```

```python
import functools
import math

import numpy as np
import jax
import jax.numpy as jnp
from jax import lax
from jax.experimental import pallas as pl
from jax.experimental.pallas import tpu as pltpu

F32 = jnp.float32
BF16 = jnp.bfloat16
MESH = pl.DeviceIdType.MESH

D_MODEL = 1024
SEQ = 2048
DEPTH = 4
N_DEV = 8
RET_HEADS, RET_DK, RET_DV = 4, 128, 256
RET_THETA = 10000.0
DIL_HEADS, DIL_HD = 8, 64
DIL_PAIRS = ((128, 1), (512, 4), (2048, 16))
ROPE_THETA = 500000.0
ROPE_DIMS = DIL_HD // 4
GDN_HEADS, GDN_DK, GDN_DV, GDN_CHUNK, GDN_CONV = 8, 128, 128, 64, 4
D_FF = 2816
FFN_CONV = 3
ALPHA = (2.0 * DEPTH) ** 0.25
EPS = 1e-5
RET_QK_W = RET_HEADS * RET_DK
RET_V_W = RET_HEADS * RET_DV
DIL_W = DIL_HEADS * DIL_HD
EV_IN = 2 * RET_QK_W + 2 * RET_V_W + 3 * DIL_W
EV_MIX = RET_V_W + DIL_W
GDN_W = GDN_HEADS * GDN_DK
OD_IN = 4 * GDN_W + 2 * GDN_HEADS
OD_IN_PAD = 4 * GDN_W + 128
ADAM_LR, ADAM_B1, ADAM_B2, ADAM_EPS, ADAM_WD, ADAM_STEP = 0.001, 0.9, 0.999, 1e-08, 0.01, 10

LANES = 128
VMEM_LIMIT = 56 * 1024 * 1024
ATT_BLK = 256
NEG = -1e30


def _cp(**kw):
    return pltpu.CompilerParams(vmem_limit_bytes=VMEM_LIMIT, **kw)


def _tile(n, cap):
    if n <= cap:
        return n
    best = None
    for t in range(LANES, cap + 1, LANES):
        if n % t == 0:
            best = t
    assert best is not None, (n, cap)
    return best


def _mm(a, b, *, ta=False, tb=False, name, out_dtype=F32):
    m = a.shape[1] if ta else a.shape[0]
    k = a.shape[0] if ta else a.shape[1]
    n = b.shape[0] if tb else b.shape[1]
    assert (b.shape[1] if tb else b.shape[0]) == k
    assert a.dtype == BF16 and b.dtype == BF16
    tm = _tile(m, 1024 if k <= 2048 else 512)
    tn = _tile(n, 512)
    dims = (((0 if ta else 1,), (1 if tb else 0,)), ((), ()))

    def body(a_ref, b_ref, o_ref):
        o_ref[...] = lax.dot_general(a_ref[...], b_ref[...], dims,
                                     preferred_element_type=F32).astype(o_ref.dtype)

    a_spec = pl.BlockSpec((k, tm), lambda i, j: (0, i)) if ta else pl.BlockSpec((tm, k), lambda i, j: (i, 0))
    b_spec = pl.BlockSpec((tn, k), lambda i, j: (j, 0)) if tb else pl.BlockSpec((k, tn), lambda i, j: (0, j))
    return pl.pallas_call(
        body, grid=(m // tm, n // tn), in_specs=[a_spec, b_spec],
        out_specs=pl.BlockSpec((tm, tn), lambda i, j: (i, j)),
        out_shape=jax.ShapeDtypeStruct((m, n), out_dtype), name=name, compiler_params=_cp())(a, b)


LN_ROWS = 256


def _ln_fwd(x, m, g, b, *, name):
    t, d = x.shape

    def body(x_ref, m_ref, g_ref, b_ref, z_ref, y_ref, yb_ref):
        z = ALPHA * x_ref[...] + m_ref[...]
        mu = jnp.mean(z, -1, keepdims=True)
        zc = z - mu
        var = jnp.mean(zc * zc, -1, keepdims=True)
        y = zc * lax.rsqrt(var + EPS) * g_ref[...] + b_ref[...]
        z_ref[...] = z
        y_ref[...] = y
        yb_ref[...] = y.astype(BF16)

    row = pl.BlockSpec((LN_ROWS, d), lambda i: (i, 0))
    vec = pl.BlockSpec((1, d), lambda i: (0, 0))
    return pl.pallas_call(
        body, grid=(t // LN_ROWS,), in_specs=[row, row, vec, vec], out_specs=[row, row, row],
        out_shape=[jax.ShapeDtypeStruct((t, d), F32), jax.ShapeDtypeStruct((t, d), F32),
                   jax.ShapeDtypeStruct((t, d), BF16)],
        name=name, compiler_params=_cp())(x, m, g, b)


def _ln_bwd(z, g, dya, dyb, *, name):
    t, d = z.shape
    two = dyb is not None

    def body(*refs):
        if two:
            z_ref, g_ref, dya_ref, dyb_ref, dz_ref, dzb_ref, dg_ref, db_ref = refs
            dy = dya_ref[...] + ALPHA * dyb_ref[...]
        else:
            z_ref, g_ref, dya_ref, dz_ref, dzb_ref, dg_ref, db_ref = refs
            dy = dya_ref[...]
        zz = z_ref[...]
        mu = jnp.mean(zz, -1, keepdims=True)
        zc = zz - mu
        var = jnp.mean(zc * zc, -1, keepdims=True)
        r = lax.rsqrt(var + EPS)
        xh = zc * r
        dxh = dy * g_ref[...]
        dz = r * (dxh - jnp.mean(dxh, -1, keepdims=True) - xh * jnp.mean(dxh * xh, -1, keepdims=True))
        dz_ref[...] = dz
        dzb_ref[...] = dz.astype(BF16)

        @pl.when(pl.program_id(0) == 0)
        def _():
            dg_ref[...] = jnp.zeros_like(dg_ref)
            db_ref[...] = jnp.zeros_like(db_ref)

        dg_ref[...] += jnp.sum(dy * xh, 0, keepdims=True)
        db_ref[...] += jnp.sum(dy, 0, keepdims=True)

    row = pl.BlockSpec((LN_ROWS, d), lambda i: (i, 0))
    vec = pl.BlockSpec((1, d), lambda i: (0, 0))
    ins = [z, g, dya] + ([dyb] if two else [])
    return pl.pallas_call(
        body, grid=(t // LN_ROWS,), in_specs=[row, vec, row] + ([row] if two else []),
        out_specs=[row, row, vec, vec],
        out_shape=[jax.ShapeDtypeStruct((t, d), F32), jax.ShapeDtypeStruct((t, d), BF16),
                   jax.ShapeDtypeStruct((1, d), F32), jax.ShapeDtypeStruct((1, d), F32)],
        name=name, compiler_params=_cp())(*ins)


def _axpy(a, b, *, name):
    t, d = a.shape

    def body(a_ref, b_ref, o_ref):
        o_ref[...] = a_ref[...] + ALPHA * b_ref[...]

    row = pl.BlockSpec((LN_ROWS, d), lambda i: (i, 0))
    return pl.pallas_call(body, grid=(t // LN_ROWS,), in_specs=[row, row], out_specs=row,
                          out_shape=jax.ShapeDtypeStruct((t, d), F32), name=name, compiler_params=_cp())(a, b)


def _loss_head(y, target, *, name):
    t, d = y.shape

    def body(y_ref, t_ref, dy_ref, l_ref):
        e = y_ref[...] - t_ref[...]
        dy_ref[...] = e * (1.0 / d)

        @pl.when(pl.program_id(0) == 0)
        def _():
            l_ref[...] = jnp.zeros_like(l_ref)

        l_ref[...] += jnp.zeros_like(l_ref) + 0.5 * jnp.sum(jnp.mean(e * e, -1, keepdims=True), 0, keepdims=True)

    row = pl.BlockSpec((LN_ROWS, d), lambda i: (i, 0))
    return pl.pallas_call(
        body, grid=(t // LN_ROWS,), in_specs=[row, row],
        out_specs=[row, pl.BlockSpec((1, LANES), lambda i: (0, 0))],
        out_shape=[jax.ShapeDtypeStruct((t, d), F32), jax.ShapeDtypeStruct((1, LANES), F32)],
        name=name, compiler_params=_cp())(y, target)


def _sig(x):
    return 1.0 / (1.0 + jnp.exp(-x))


def _silu(x):
    return x * _sig(x)


def _dsilu(x):
    s = _sig(x)
    return s * (1.0 + x * (1.0 - s))


def _shift_down(u, k, row):
    if k == 0:
        return u
    return jnp.where(row >= k, pltpu.roll(u, k, 0), 0.0)


def _shift_up(u, k, row):
    if k == 0:
        return u
    t = u.shape[0]
    return jnp.where(row < t - k, pltpu.roll(u, t - k, 0), 0.0)


def _dwconv(u, w_ref, row):
    kk = w_ref.shape[0]
    acc = None
    for j in range(kk):
        term = w_ref[j:j + 1, :] * _shift_down(u, kk - 1 - j, row)
        acc = term if acc is None else acc + term
    return acc


def _dwconv_bwd(u, w_ref, dc, row, dw_ref):
    kk = w_ref.shape[0]
    du = None
    for j in range(kk):
        term = w_ref[j:j + 1, :] * _shift_up(dc, kk - 1 - j, row)
        du = term if du is None else du + term
        dw_ref[j:j + 1, :] = jnp.sum(dc * _shift_down(u, kk - 1 - j, row), 0, keepdims=True)
    return du


FFN_TC = 256


def _ffn_mid_fwd(u, cw, cb, *, name):
    t = u.shape[0]
    nb = D_FF // FFN_TC

    def body(ug_ref, uv_ref, wg_ref, wv_ref, bg_ref, bv_ref, a_ref):
        row = lax.broadcasted_iota(jnp.int32, (t, FFN_TC), 0)
        cg = _dwconv(ug_ref[...], wg_ref, row) + bg_ref[...]
        cv = _dwconv(uv_ref[...], wv_ref, row) + bv_ref[...]
        a_ref[...] = (_silu(cg) * cv).astype(BF16)

    col = lambda off: pl.BlockSpec((t, FFN_TC), lambda j: (0, j + off))
    wsp = lambda off: pl.BlockSpec((FFN_CONV, FFN_TC), lambda j: (0, j + off))
    bsp = lambda off: pl.BlockSpec((1, FFN_TC), lambda j: (0, j + off))
    return pl.pallas_call(
        body, grid=(nb,), in_specs=[col(0), col(nb), wsp(0), wsp(nb), bsp(0), bsp(nb)],
        out_specs=pl.BlockSpec((t, FFN_TC), lambda j: (0, j)),
        out_shape=jax.ShapeDtypeStruct((t, D_FF), BF16), name=name, compiler_params=_cp())(u, u, cw, cw, cb, cb)


def _ffn_mid_bwd(u, cw, cb, da, *, name):
    t = u.shape[0]
    nb = D_FF // FFN_TC

    def body(ug_ref, uv_ref, wg_ref, wv_ref, bg_ref, bv_ref, da_ref,
             dug_ref, duv_ref, dwg_ref, dwv_ref, dbg_ref, dbv_ref):
        row = lax.broadcasted_iota(jnp.int32, (t, FFN_TC), 0)
        ug, uv = ug_ref[...], uv_ref[...]
        cg = _dwconv(ug, wg_ref, row) + bg_ref[...]
        cv = _dwconv(uv, wv_ref, row) + bv_ref[...]
        da_ = da_ref[...]
        dcv = da_ * _silu(cg)
        dcg = da_ * cv * _dsilu(cg)
        dug_ref[...] = _dwconv_bwd(ug, wg_ref, dcg, row, dwg_ref).astype(BF16)
        duv_ref[...] = _dwconv_bwd(uv, wv_ref, dcv, row, dwv_ref).astype(BF16)
        dbg_ref[...] = jnp.sum(dcg, 0, keepdims=True)
        dbv_ref[...] = jnp.sum(dcv, 0, keepdims=True)

    col = lambda off: pl.BlockSpec((t, FFN_TC), lambda j: (0, j + off))
    wsp = lambda off: pl.BlockSpec((FFN_CONV, FFN_TC), lambda j: (0, j + off))
    bsp = lambda off: pl.BlockSpec((1, FFN_TC), lambda j: (0, j + off))
    outs = pl.pallas_call(
        body, grid=(nb,), in_specs=[col(0), col(nb), wsp(0), wsp(nb), bsp(0), bsp(nb), col(0)],
        out_specs=[col(0), col(0), wsp(0), wsp(0), bsp(0), bsp(0)],
        out_shape=[jax.ShapeDtypeStruct((t, D_FF), BF16), jax.ShapeDtypeStruct((t, D_FF), BF16),
                   jax.ShapeDtypeStruct((FFN_CONV, D_FF), F32), jax.ShapeDtypeStruct((FFN_CONV, D_FF), F32),
                   jax.ShapeDtypeStruct((1, D_FF), F32), jax.ShapeDtypeStruct((1, D_FF), F32)],
        name=name, compiler_params=_cp())(u, u, cw, cw, cb, cb, da)
    dug, duv, dwg, dwv, dbg, dbv = outs
    return (jnp.concatenate([dug, duv], 1), jnp.concatenate([dwg, dwv], 1), jnp.concatenate([dbg, dbv], 1))


def _rot_a(x, c2, s2):
    return x * c2 + pltpu.roll(x, RET_DK // 2, 1) * s2


def _rot_a_t(dy, c2, s2):
    return dy * c2 + pltpu.roll(dy * s2, RET_DK // 2, 1)


def _decay_tile(lg, blk_diff):
    r = lax.broadcasted_iota(jnp.int32, (ATT_BLK, ATT_BLK), 0)
    c = lax.broadcasted_iota(jnp.int32, (ATT_BLK, ATT_BLK), 1)
    rel = r - c + blk_diff * ATT_BLK
    return jnp.where(rel >= 0, jnp.exp(jnp.maximum(rel, 0).astype(F32) * lg), 0.0)


def _nt(a, b):
    return lax.dot_general(a, b, (((1,), (1,)), ((), ())), preferred_element_type=F32)


def _nn(a, b):
    return lax.dot_general(a, b, (((1,), (0,)), ((), ())), preferred_element_type=F32)


def _tn(a, b):
    return lax.dot_general(a, b, (((0,), (0,)), ((), ())), preferred_element_type=F32)


def _ret_specs(t):
    q = pl.BlockSpec((t, RET_DK), lambda h: (0, h))
    k = pl.BlockSpec((t, RET_DK), lambda h: (0, RET_HEADS + h))
    v = pl.BlockSpec((t, RET_DV), lambda h: (0, RET_HEADS + h))
    g = pl.BlockSpec((t, RET_DV), lambda h: (0, 2 * RET_HEADS + h))
    tab = pl.BlockSpec((t, RET_DK), lambda h: (0, 0))
    lg = pl.BlockSpec((1, 1, LANES), lambda h: (h, 0, 0))
    return q, k, v, g, tab, lg


def _ret_fwd(h, c2, s2, lgt, *, name):
    t = h.shape[0]
    nblk = t // ATT_BLK
    scale = RET_DK ** -0.5

    def body(q_ref, k_ref, v_ref, g_ref, c_ref, s_ref, lg_ref, o_ref, ya_ref, qs, ks, vs):
        c2_, s2_ = c_ref[...], s_ref[...]
        qs[...] = _rot_a(q_ref[...], c2_, s2_).astype(BF16)
        ks[...] = (_rot_a(k_ref[...], c2_, s2_) * scale).astype(BF16)
        vs[...] = v_ref[...].astype(BF16)
        lg = lg_ref[0, :, 0:1]
        for i in range(nblk):
            qi = qs[pl.ds(i * ATT_BLK, ATT_BLK), :]
            acc = jnp.zeros((ATT_BLK, RET_DV), F32)
            for j in range(i + 1):
                sl = pl.ds(j * ATT_BLK, ATT_BLK)
                s = _nt(qi, ks[sl, :]) * _decay_tile(lg, i - j)
                acc = acc + _nn(s.astype(BF16), vs[sl, :])
            rows = pl.ds(i * ATT_BLK, ATT_BLK)
            o_ref[rows, :] = acc
            r = lax.rsqrt(jnp.mean(acc * acc, -1, keepdims=True) + EPS)
            ya_ref[rows, :] = (acc * r * _silu(g_ref[rows, :])).astype(BF16)

    q, k, v, g, tab, lg = _ret_specs(t)
    out = pl.BlockSpec((t, RET_DV), lambda hh: (0, hh))
    return pl.pallas_call(
        body, grid=(RET_HEADS,), in_specs=[q, k, v, g, tab, tab, lg], out_specs=[out, out],
        out_shape=[jax.ShapeDtypeStruct((t, RET_V_W), F32), jax.ShapeDtypeStruct((t, RET_V_W), BF16)],
        scratch_shapes=[pltpu.VMEM((t, RET_DK), BF16), pltpu.VMEM((t, RET_DK), BF16), pltpu.VMEM((t, RET_DV), BF16)],
        name=name, compiler_params=_cp())(h, h, h, h, c2, s2, lgt)


def _ret_bwd(h, c2, s2, lgt, o, dy, *, name):
    t = h.shape[0]
    nblk = t // ATT_BLK
    scale = RET_DK ** -0.5

    def body(q_ref, k_ref, v_ref, g_ref, c_ref, s_ref, lg_ref, o_ref, dy_ref,
             dq_ref, dk_ref, dv_ref, dg_ref, qs, ks, vs, dos, dka, dva):
        c2_, s2_ = c_ref[...], s_ref[...]
        qs[...] = _rot_a(q_ref[...], c2_, s2_).astype(BF16)
        ks[...] = (_rot_a(k_ref[...], c2_, s2_) * scale).astype(BF16)
        vs[...] = v_ref[...].astype(BF16)
        lg = lg_ref[0, :, 0:1]
        oo = o_ref[...]
        gg = g_ref[...]
        dya = dy_ref[...]
        r = lax.rsqrt(jnp.mean(oo * oo, -1, keepdims=True) + EPS)
        rn = oo * r
        dg_ref[...] = (dya * rn * _dsilu(gg)).astype(BF16)
        drn = dya * _silu(gg)
        dos[...] = (r * (drn - rn * jnp.mean(drn * rn, -1, keepdims=True))).astype(BF16)
        dka[...] = jnp.zeros_like(dka)
        dva[...] = jnp.zeros_like(dva)
        for i in range(nblk):
            rows = pl.ds(i * ATT_BLK, ATT_BLK)
            qi = qs[rows, :]
            doi = dos[rows, :]
            dqa = jnp.zeros((ATT_BLK, RET_DK), F32)
            for j in range(i + 1):
                sl = pl.ds(j * ATT_BLK, ATT_BLK)
                dt_ = _decay_tile(lg, i - j)
                kj = ks[sl, :]
                s = (_nt(qi, kj) * dt_).astype(BF16)
                ds = (_nt(doi, vs[sl, :]) * dt_).astype(BF16)
                dqa = dqa + _nn(ds, kj)
                dka[sl, :] += _tn(ds, qi)
                dva[sl, :] += _tn(s, doi)
            dq_ref[rows, :] = _rot_a_t(dqa, c_ref[rows, :], s_ref[rows, :]).astype(BF16)
        dk_ref[...] = (_rot_a_t(dka[...], c2_, s2_) * scale).astype(BF16)
        dv_ref[...] = dva[...].astype(BF16)

    q, k, v, g, tab, lg = _ret_specs(t)
    blk_v = pl.BlockSpec((t, RET_DV), lambda hh: (0, hh))
    blk_k = pl.BlockSpec((t, RET_DK), lambda hh: (0, hh))
    return pl.pallas_call(
        body, grid=(RET_HEADS,), in_specs=[q, k, v, g, tab, tab, lg, blk_v, blk_v],
        out_specs=[blk_k, blk_k, blk_v, blk_v],
        out_shape=[jax.ShapeDtypeStruct((t, RET_QK_W), BF16), jax.ShapeDtypeStruct((t, RET_QK_W), BF16),
                   jax.ShapeDtypeStruct((t, RET_V_W), BF16), jax.ShapeDtypeStruct((t, RET_V_W), BF16)],
        scratch_shapes=[pltpu.VMEM((t, RET_DK), BF16), pltpu.VMEM((t, RET_DK), BF16), pltpu.VMEM((t, RET_DV), BF16),
                        pltpu.VMEM((t, RET_DV), BF16), pltpu.VMEM((t, RET_DK), F32), pltpu.VMEM((t, RET_DV), F32)],
        name=name, compiler_params=_cp())(h, h, h, h, c2, s2, lgt, o, dy)


def _rot_b(x, cb, shi, slo):
    return x * cb + pltpu.roll(x, ROPE_DIMS // 2, 1) * shi + pltpu.roll(x, LANES - ROPE_DIMS // 2, 1) * slo


def _rot_b_t(dy, cb, shi, slo):
    return dy * cb + pltpu.roll(dy * shi, LANES - ROPE_DIMS // 2, 1) + pltpu.roll(dy * slo, ROPE_DIMS // 2, 1)


def _dil_specs(t):
    base = (2 * RET_QK_W + 2 * RET_V_W) // LANES
    npair = DIL_W // LANES
    q = pl.BlockSpec((t, LANES), lambda p: (0, base + p))
    k = pl.BlockSpec((t, LANES), lambda p: (0, base + npair + p))
    v = pl.BlockSpec((t, LANES), lambda p: (0, base + 2 * npair + p))
    tab = pl.BlockSpec((t, LANES), lambda p: (0, 0))
    strip = pl.BlockSpec((ATT_BLK, t), lambda p: (0, 0))
    pair = pl.BlockSpec((t, LANES), lambda p: (0, p))
    return q, k, v, tab, strip, pair


def _dil_fwd(h, cb, shi, slo, strip, *, name):
    t = h.shape[0]
    nblk = t // ATT_BLK
    scale = DIL_HD ** -0.5

    def body(q_ref, k_ref, v_ref, cb_ref, shi_ref, slo_ref, st_ref, o_ref, yb_ref, lse_ref, qs, ks, vs):
        cb_, shi_, slo_ = cb_ref[...], shi_ref[...], slo_ref[...]
        lane = lax.broadcasted_iota(jnp.int32, (t, LANES), 1)
        qr = _rot_b(q_ref[...], cb_, shi_, slo_) * scale
        qs[0] = jnp.where(lane < DIL_HD, qr, 0.0).astype(BF16)
        qs[1] = jnp.where(lane >= DIL_HD, qr, 0.0).astype(BF16)
        ks[...] = _rot_b(k_ref[...], cb_, shi_, slo_).astype(BF16)
        vs[...] = v_ref[...].astype(BF16)
        lane_b = lax.broadcasted_iota(jnp.int32, (ATT_BLK, LANES), 1)
        for i in range(nblk):
            w = (i + 1) * ATT_BLK
            rows = pl.ds(i * ATT_BLK, ATT_BLK)
            logc = st_ref[:, t - w:t]
            outs, lses = [], []
            for hd in range(2):
                s = _nt(qs[hd, rows, :], ks[0:w, :]) + logc
                m = jnp.max(s, -1, keepdims=True)
                p = jnp.exp(s - m)
                l = jnp.sum(p, -1, keepdims=True)
                outs.append(_nn(p.astype(BF16), vs[0:w, :]) / l)
                lses.append(m + jnp.log(l))
            o = jnp.where(lane_b < DIL_HD, outs[0], outs[1])
            o_ref[rows, :] = o
            yb_ref[rows, :] = o.astype(BF16)
            lse_ref[rows, :] = jnp.where(lane_b < DIL_HD, lses[0], lses[1])

    q, k, v, tab, strip_spec, pair = _dil_specs(t)
    return pl.pallas_call(
        body, grid=(DIL_W // LANES,), in_specs=[q, k, v, tab, tab, tab, strip_spec], out_specs=[pair, pair, pair],
        out_shape=[jax.ShapeDtypeStruct((t, DIL_W), F32), jax.ShapeDtypeStruct((t, DIL_W), BF16),
                   jax.ShapeDtypeStruct((t, DIL_W), F32)],
        scratch_shapes=[pltpu.VMEM((2, t, LANES), BF16), pltpu.VMEM((t, LANES), BF16), pltpu.VMEM((t, LANES), BF16)],
        name=name, compiler_params=_cp())(h, h, h, cb, shi, slo, strip)


def _dil_bwd(h, cb, shi, slo, strip, o, lse, dy, *, name):
    t = h.shape[0]
    nblk = t // ATT_BLK
    scale = DIL_HD ** -0.5

    def body(q_ref, k_ref, v_ref, cb_ref, shi_ref, slo_ref, st_ref, o_ref, lse_ref, dy_ref,
             dq_ref, dk_ref, dv_ref, qs, ks, vs, dos, dls, dka, dva):
        cb_, shi_, slo_ = cb_ref[...], shi_ref[...], slo_ref[...]
        lane = lax.broadcasted_iota(jnp.int32, (t, LANES), 1)
        qr = _rot_b(q_ref[...], cb_, shi_, slo_) * scale
        qs[0] = jnp.where(lane < DIL_HD, qr, 0.0).astype(BF16)
        qs[1] = jnp.where(lane >= DIL_HD, qr, 0.0).astype(BF16)
        ks[...] = _rot_b(k_ref[...], cb_, shi_, slo_).astype(BF16)
        vs[...] = v_ref[...].astype(BF16)
        do = dy_ref[...]
        prod = do * o_ref[...]
        d0 = jnp.sum(jnp.where(lane < DIL_HD, prod, 0.0), -1, keepdims=True)
        d1 = jnp.sum(jnp.where(lane >= DIL_HD, prod, 0.0), -1, keepdims=True)
        dls[...] = jnp.where(lane < DIL_HD, d0, d1)
        dos[0] = jnp.where(lane < DIL_HD, do, 0.0).astype(BF16)
        dos[1] = jnp.where(lane >= DIL_HD, do, 0.0).astype(BF16)
        dka[...] = jnp.zeros_like(dka)
        dva[...] = jnp.zeros_like(dva)
        lane_b = lax.broadcasted_iota(jnp.int32, (ATT_BLK, LANES), 1)
        for i in range(nblk):
            w = (i + 1) * ATT_BLK
            rows = pl.ds(i * ATT_BLK, ATT_BLK)
            logc = st_ref[:, t - w:t]
            dqs = []
            for hd in range(2):
                col = hd * DIL_HD
                qh = qs[hd, rows, :]
                doh = dos[hd, rows, :]
                lse_h = lse_ref[rows, col:col + 1]
                dl_h = dls[rows, col:col + 1]
                p = jnp.exp(_nt(qh, ks[0:w, :]) + logc - lse_h)
                dp = _nt(doh, vs[0:w, :])
                ds = (p * (dp - dl_h)).astype(BF16)
                dqs.append(_nn(ds, ks[0:w, :]))
                dka[0:w, :] += _tn(ds, qh)
                dva[0:w, :] += _tn(p.astype(BF16), doh)
            dq = jnp.where(lane_b < DIL_HD, dqs[0], dqs[1]) * scale
            dq_ref[rows, :] = _rot_b_t(dq, cb_ref[rows, :], shi_ref[rows, :], slo_ref[rows, :]).astype(BF16)
        dk_ref[...] = _rot_b_t(dka[...], cb_, shi_, slo_).astype(BF16)
        dv_ref[...] = dva[...].astype(BF16)

    q, k, v, tab, strip_spec, pair = _dil_specs(t)
    dy_spec = pl.BlockSpec((t, LANES), lambda p: (0, RET_V_W // LANES + p))
    return pl.pallas_call(
        body, grid=(DIL_W // LANES,), in_specs=[q, k, v, tab, tab, tab, strip_spec, pair, pair, dy_spec],
        out_specs=[pair, pair, pair],
        out_shape=[jax.ShapeDtypeStruct((t, DIL_W), BF16)] * 3,
        scratch_shapes=[pltpu.VMEM((2, t, LANES), BF16), pltpu.VMEM((t, LANES), BF16), pltpu.VMEM((t, LANES), BF16),
                        pltpu.VMEM((2, t, LANES), BF16), pltpu.VMEM((t, LANES), F32),
                        pltpu.VMEM((t, LANES), F32), pltpu.VMEM((t, LANES), F32)],
        name=name, compiler_params=_cp())(h, h, h, cb, shi, slo, strip, o, lse, dy)


def _gdn_prep_fwd(h, cw, *, name):
    t = h.shape[0]
    qscale = GDN_DK ** -0.5

    def body(hq_ref, hk_ref, hv_ref, wq_ref, wk_ref, wv_ref, q_ref, k_ref, v_ref):
        row = lax.broadcasted_iota(jnp.int32, (t, GDN_DK), 0)
        sq = _silu(_dwconv(hq_ref[...], wq_ref, row))
        sk = _silu(_dwconv(hk_ref[...], wk_ref, row))
        q_ref[0] = sq * lax.rsqrt(jnp.sum(sq * sq, -1, keepdims=True) + 1e-6) * qscale
        k_ref[0] = sk * lax.rsqrt(jnp.sum(sk * sk, -1, keepdims=True) + 1e-6)
        v_ref[0] = _silu(_dwconv(hv_ref[...], wv_ref, row))

    hs = lambda off: pl.BlockSpec((t, GDN_DK), lambda i: (0, i + off))
    ws = lambda off: pl.BlockSpec((GDN_CONV, GDN_DK), lambda i: (0, i + off))
    out = pl.BlockSpec((1, t, GDN_DK), lambda i: (i, 0, 0))
    return pl.pallas_call(
        body, grid=(GDN_HEADS,), in_specs=[hs(0), hs(8), hs(16), ws(0), ws(8), ws(16)], out_specs=[out, out, out],
        out_shape=[jax.ShapeDtypeStruct((GDN_HEADS, t, GDN_DK), F32)] * 3,
        name=name, compiler_params=_cp())(h, h, h, cw, cw, cw)


def _gdn_prep_bwd(h, cw, dq, dk, dv, *, name):
    t = h.shape[0]
    qscale = GDN_DK ** -0.5

    def body(hq_ref, hk_ref, hv_ref, wq_ref, wk_ref, wv_ref, dq_ref, dk_ref, dv_ref,
             dhq_ref, dhk_ref, dhv_ref, dwq_ref, dwk_ref, dwv_ref):
        row = lax.broadcasted_iota(jnp.int32, (t, GDN_DK), 0)

        def one(h_ref, w_ref, d_ref, dh_ref, dw_ref, norm, sc):
            u = h_ref[...]
            c = _dwconv(u, w_ref, row)
            d = d_ref[0]
            if norm:
                s = _silu(c)
                r = lax.rsqrt(jnp.sum(s * s, -1, keepdims=True) + 1e-6)
                n = s * r
                d = d * sc
                d = r * (d - n * jnp.sum(d * n, -1, keepdims=True))
            dc = d * _dsilu(c)
            dh_ref[...] = _dwconv_bwd(u, w_ref, dc, row, dw_ref).astype(BF16)

        one(hq_ref, wq_ref, dq_ref, dhq_ref, dwq_ref, True, qscale)
        one(hk_ref, wk_ref, dk_ref, dhk_ref, dwk_ref, True, 1.0)
        one(hv_ref, wv_ref, dv_ref, dhv_ref, dwv_ref, False, 1.0)

    hs = lambda off: pl.BlockSpec((t, GDN_DK), lambda i: (0, i + off))
    ws = lambda off: pl.BlockSpec((GDN_CONV, GDN_DK), lambda i: (0, i + off))
    hd = pl.BlockSpec((1, t, GDN_DK), lambda i: (i, 0, 0))
    return pl.pallas_call(
        body, grid=(GDN_HEADS,), in_specs=[hs(0), hs(8), hs(16), ws(0), ws(8), ws(16), hd, hd, hd],
        out_specs=[hs(0), hs(0), hs(0), ws(0), ws(0), ws(0)],
        out_shape=[jax.ShapeDtypeStruct((t, GDN_W), BF16)] * 3 + [jax.ShapeDtypeStruct((GDN_CONV, GDN_W), F32)] * 3,
        name=name, compiler_params=_cp())(h, h, h, cw, cw, cw, dq, dk, dv)


def _make_mm2(hi):
    def prep(x):
        return x if hi else x.astype(BF16)
    prec = lax.Precision.HIGHEST if hi else None

    def raw(a, b, dims):
        return lax.dot_general(prep(a), prep(b), (dims, ((), ())), precision=prec, preferred_element_type=F32)

    @jax.custom_vjp
    def nn(a, b):
        return raw(a, b, ((1,), (0,)))

    @jax.custom_vjp
    def nt(a, b):
        return raw(a, b, ((1,), (1,)))

    @jax.custom_vjp
    def tn(a, b):
        return raw(a, b, ((0,), (0,)))

    nn.defvjp(lambda a, b: (nn(a, b), (a, b)), lambda r, g: (nt(g, r[1]), tn(r[0], g)))
    nt.defvjp(lambda a, b: (nt(a, b), (a, b)), lambda r, g: (nn(g, r[1]), tn(g, r[0])))
    tn.defvjp(lambda a, b: (tn(a, b), (a, b)), lambda r, g: (nt(r[1], g), nn(r[0], g)))
    return nn, nt, tn


_NN, _NT, _TN = _make_mm2(False)
_NNH, _NTH, _TNH = _make_mm2(True)


@jax.custom_vjp
def _inv_unit_lower(l):
    c = l.shape[0]
    eye = (lax.broadcasted_iota(jnp.int32, (c, c), 0) == lax.broadcasted_iota(jnp.int32, (c, c), 1)).astype(F32)
    p = -l
    t = eye + p
    for _ in range(int(math.log2(c)) - 1):
        p = _NNH(p, p)
        t = t + _NNH(t, p)
    return t


def _inv_fwd(l):
    t = _inv_unit_lower(l)
    return t, t


def _inv_bwd(t, dt):
    return (-_NTH(_TNH(t, dt), t),)


_inv_unit_lower.defvjp(_inv_fwd, _inv_bwd)


def _softplus(x):
    return jnp.maximum(x, 0.0) + jnp.log1p(jnp.exp(-jnp.abs(x)))


def _gdn_chunk(q, k, v, braw, araw, alog, dtb, state):
    c = q.shape[0]
    ri = lax.broadcasted_iota(jnp.int32, (c, c), 0)
    ci = lax.broadcasted_iota(jnp.int32, (c, c), 1)
    tri = ri >= ci
    strict = ri > ci
    eye = (ri == ci).astype(F32)
    beta = _sig(braw)
    g = -jnp.exp(alog) * _softplus(araw + dtb)
    gcm = _NNH(tri.astype(F32), g * jnp.ones((c, c), F32))
    gct = _NTH(eye, gcm)
    decay = jnp.where(tri, jnp.exp(jnp.where(tri, gcm - gct, 0.0)), 0.0)
    gc = jnp.sum(gcm, 1, keepdims=True) * (1.0 / c)
    glast = jnp.sum(g, 0, keepdims=True)
    egc = jnp.exp(gc)
    kb = k * beta
    tm = _inv_unit_lower(jnp.where(strict, _NT(kb, k) * decay, 0.0))
    u = _NNH(tm, v * beta)
    w = _NNH(tm, kb * egc)
    attn = jnp.where(tri, _NT(q, k) * decay, 0.0)
    k_dec = k * jnp.exp(glast - gc)
    q_dec = q * egc
    v_new = u - _NN(w, state)
    o = _NN(q_dec, state) + _NN(attn, v_new)
    new_state = state * jnp.exp(glast) + _TN(k_dec, v_new)
    return o, new_state


def _gdn_specs(t, rev):
    nch = t // GDN_CHUNK
    cm = (lambda n: nch - 1 - n) if rev else (lambda n: n)
    tok = pl.BlockSpec((GDN_HEADS, GDN_CHUNK, GDN_DK), lambda n: (0, cm(n), 0))
    par = pl.BlockSpec((GDN_HEADS, 1, LANES), lambda n: (0, 0, 0))
    st = pl.BlockSpec((GDN_HEADS, 1, GDN_DK, GDN_DV), lambda n: (0, cm(n), 0, 0))
    return tok, par, st


def _gdn_core_fwd(q, k, v, bb, ab, alog, dtb, *, name):
    t = q.shape[1]
    nch = t // GDN_CHUNK

    def body(q_ref, k_ref, v_ref, bb_ref, ab_ref, al_ref, dt_ref, o_ref, st_ref, state):
        @pl.when(pl.program_id(0) == 0)
        def _():
            state[...] = jnp.zeros_like(state)

        s0 = state[...]
        st_ref[:, 0] = s0
        o, s1 = jax.vmap(_gdn_chunk)(q_ref[...], k_ref[...], v_ref[...], bb_ref[:, :, 0:1], ab_ref[:, :, 0:1],
                                     al_ref[:, :, 0:1], dt_ref[:, :, 0:1], s0)
        o_ref[...] = o
        state[...] = s1

    tok, par, st = _gdn_specs(t, False)
    return pl.pallas_call(
        body, grid=(nch,), in_specs=[tok, tok, tok, tok, tok, par, par], out_specs=[tok, st],
        out_shape=[jax.ShapeDtypeStruct((GDN_HEADS, t, GDN_DV), F32),
                   jax.ShapeDtypeStruct((GDN_HEADS, nch, GDN_DK, GDN_DV), F32)],
        scratch_shapes=[pltpu.VMEM((GDN_HEADS, GDN_DK, GDN_DV), F32)],
        name=name, compiler_params=_cp())(q, k, v, bb, ab, alog, dtb)


def _gdn_core_bwd(q, k, v, bb, ab, alog, dtb, states, do, *, name):
    t = q.shape[1]
    nch = t // GDN_CHUNK

    def body(q_ref, k_ref, v_ref, bb_ref, ab_ref, al_ref, dt_ref, st_ref, do_ref,
             dq_ref, dk_ref, dv_ref, dbb_ref, dab_ref, dal_ref, ddt_ref, dstate):
        @pl.when(pl.program_id(0) == 0)
        def _():
            dstate[...] = jnp.zeros_like(dstate)
            dal_ref[...] = jnp.zeros_like(dal_ref)
            ddt_ref[...] = jnp.zeros_like(ddt_ref)

        args = (q_ref[...], k_ref[...], v_ref[...], bb_ref[:, :, 0:1], ab_ref[:, :, 0:1],
                al_ref[:, :, 0:1], dt_ref[:, :, 0:1], st_ref[:, 0])
        _, pull = jax.vjp(jax.vmap(_gdn_chunk), *args)
        dq, dk, dv, dbr, dar, dal, ddt, ds = pull((do_ref[...], dstate[...]))
        dq_ref[...] = dq
        dk_ref[...] = dk
        dv_ref[...] = dv
        dbb_ref[...] = dbr + jnp.zeros((GDN_HEADS, GDN_CHUNK, LANES), F32)
        dab_ref[...] = dar + jnp.zeros((GDN_HEADS, GDN_CHUNK, LANES), F32)
        dal_ref[...] += dal + jnp.zeros((GDN_HEADS, 1, LANES), F32)
        ddt_ref[...] += ddt + jnp.zeros((GDN_HEADS, 1, LANES), F32)
        dstate[...] = ds

    tok, par, st = _gdn_specs(t, True)
    tokshape = jax.ShapeDtypeStruct((GDN_HEADS, t, GDN_DK), F32)
    parshape = jax.ShapeDtypeStruct((GDN_HEADS, 1, LANES), F32)
    return pl.pallas_call(
        body, grid=(nch,), in_specs=[tok, tok, tok, tok, tok, par, par, st, tok],
        out_specs=[tok, tok, tok, tok, tok, par, par],
        out_shape=[tokshape] * 5 + [parshape] * 2,
        scratch_shapes=[pltpu.VMEM((GDN_HEADS, GDN_DK, GDN_DV), F32)],
        name=name, compiler_params=_cp())(q, k, v, bb, ab, alog, dtb, states, do)


GDN_ROWS = 512


def _gdn_post_fwd(o, h, nw, *, name):
    t = o.shape[1]

    def body(o_ref, g_ref, nw_ref, y_ref):
        oo = o_ref[0]
        r = lax.rsqrt(jnp.mean(oo * oo, -1, keepdims=True) + EPS)
        y_ref[...] = (oo * r * nw_ref[...] * _silu(g_ref[...])).astype(BF16)

    return pl.pallas_call(
        body, grid=(GDN_HEADS, t // GDN_ROWS),
        in_specs=[pl.BlockSpec((1, GDN_ROWS, GDN_DV), lambda hh, i: (hh, i, 0)),
                  pl.BlockSpec((GDN_ROWS, GDN_DV), lambda hh, i: (i, 3 * GDN_HEADS + hh)),
                  pl.BlockSpec((1, GDN_DV), lambda hh, i: (0, 0))],
        out_specs=pl.BlockSpec((GDN_ROWS, GDN_DV), lambda hh, i: (i, hh)),
        out_shape=jax.ShapeDtypeStruct((t, GDN_W), BF16), name=name, compiler_params=_cp())(o, h, nw)


def _gdn_post_bwd(o, h, nw, dy, *, name):
    t = o.shape[1]

    def body(o_ref, g_ref, nw_ref, dy_ref, do_ref, dg_ref, dnw_ref):
        oo, gg, nw_, dy_ = o_ref[0], g_ref[...], nw_ref[...], dy_ref[...]
        r = lax.rsqrt(jnp.mean(oo * oo, -1, keepdims=True) + EPS)
        n = oo * r
        sg = _silu(gg)
        dg_ref[...] = (dy_ * n * nw_ * _dsilu(gg)).astype(BF16)
        dn = dy_ * sg * nw_
        do_ref[0] = r * (dn - n * jnp.mean(dn * n, -1, keepdims=True))

        @pl.when((pl.program_id(0) == 0) & (pl.program_id(1) == 0))
        def _():
            dnw_ref[...] = jnp.zeros_like(dnw_ref)

        dnw_ref[...] += jnp.sum(dy_ * sg * n, 0, keepdims=True)

    return pl.pallas_call(
        body, grid=(GDN_HEADS, t // GDN_ROWS),
        in_specs=[pl.BlockSpec((1, GDN_ROWS, GDN_DV), lambda hh, i: (hh, i, 0)),
                  pl.BlockSpec((GDN_ROWS, GDN_DV), lambda hh, i: (i, 3 * GDN_HEADS + hh)),
                  pl.BlockSpec((1, GDN_DV), lambda hh, i: (0, 0)),
                  pl.BlockSpec((GDN_ROWS, GDN_DV), lambda hh, i: (i, hh))],
        out_specs=[pl.BlockSpec((1, GDN_ROWS, GDN_DV), lambda hh, i: (hh, i, 0)),
                   pl.BlockSpec((GDN_ROWS, GDN_DV), lambda hh, i: (i, hh)),
                   pl.BlockSpec((1, GDN_DV), lambda hh, i: (0, 0))],
        out_shape=[jax.ShapeDtypeStruct((GDN_HEADS, t, GDN_DV), F32), jax.ShapeDtypeStruct((t, GDN_W), BF16),
                   jax.ShapeDtypeStruct((1, GDN_DV), F32)],
        name=name, compiler_params=_cp())(o, h, nw, dy)


def _tables(positions):
    pos = positions.astype(F32)[:, None]
    half = RET_DK // 2
    inv = jnp.power(RET_THETA, -jnp.arange(half, dtype=F32) * 2.0 / RET_DK)
    ang = pos * inv
    cos, sin = jnp.cos(ang), jnp.sin(ang)
    c2a = jnp.concatenate([cos, cos], 1)
    s2a = jnp.concatenate([-sin, sin], 1)
    hb = ROPE_DIMS // 2
    invb = jnp.power(ROPE_THETA, -jnp.arange(hb, dtype=F32) * 2.0 / ROPE_DIMS)
    angb = pos * invb
    cosb, sinb = jnp.cos(angb), jnp.sin(angb)
    t = pos.shape[0]
    ones = jnp.ones((t, DIL_HD - ROPE_DIMS), F32)
    zeros = jnp.zeros((t, DIL_HD - ROPE_DIMS), F32)
    z8 = jnp.zeros((t, hb), F32)
    cb = jnp.concatenate([cosb, cosb, ones] * 2, 1)
    shi = jnp.concatenate([z8, sinb, zeros] * 2, 1)
    slo = jnp.concatenate([-sinb, z8, zeros] * 2, 1)
    lg = jnp.log1p(-jnp.power(2.0, -5.0 - jnp.arange(RET_HEADS, dtype=F32)))
    lgt = jnp.broadcast_to(lg[:, None, None], (RET_HEADS, 1, LANES))
    delta = jnp.arange(ATT_BLK, dtype=jnp.int32)[:, None] + (SEQ - ATT_BLK) - jnp.arange(SEQ, dtype=jnp.int32)[None, :]
    cnt = jnp.zeros(delta.shape, F32)
    for (w, d) in DIL_PAIRS:
        cnt = cnt + ((delta >= 0) & (delta <= w) & (delta % d == 0)).astype(F32)
    strip = jnp.where(cnt > 0, jnp.log(jnp.maximum(cnt, 1.0)), NEG)
    return c2a, s2a, cb, shi, slo, lgt, strip


def _local_step(x, positions, target, W, small):
    c2a, s2a, cb, shi, slo, lgt, strip = _tables(positions)
    t = x.shape[0]
    saved = []
    xf = x
    xb = x.astype(BF16)
    for layer in range(DEPTH):
        j = layer // 2
        L = f"L{layer}_"
        rec = {"x": xf, "xb": xb}
        if layer % 2 == 0:
            h = _mm(xb, W["ev_w_in"][j], name=L + "ev_in")
            ro, ya = _ret_fwd(h, c2a, s2a, lgt, name=L + "ret_fwd")
            do_, yb, lse = _dil_fwd(h, cb, shi, slo, strip, name=L + "dil_fwd")
            y = jnp.concatenate([ya, yb], 1)
            mix = _mm(y, W["ev_w_out"][j], name=L + "ev_out")
            rec.update(h=h, ro=ro, dil_o=do_, lse=lse, y=y)
        else:
            h = _mm(xb, W["od_w_in"][j], name=L + "od_in")
            cw = W["od_conv_w"][j]
            q, k, v = _gdn_prep_fwd(h, cw, name=L + "gdn_prep")
            hs = h[:, 4 * GDN_W:4 * GDN_W + 2 * GDN_HEADS]
            bb = jnp.broadcast_to(hs[:, :GDN_HEADS].T[:, :, None], (GDN_HEADS, t, LANES))
            ab = jnp.broadcast_to(hs[:, GDN_HEADS:].T[:, :, None], (GDN_HEADS, t, LANES))
            alog = jnp.broadcast_to(small["od_a_log"][j][:, None, None], (GDN_HEADS, 1, LANES))
            dtb = jnp.broadcast_to(small["od_dt_bias"][j][:, None, None], (GDN_HEADS, 1, LANES))
            o, states = _gdn_core_fwd(q, k, v, bb, ab, alog, dtb, name=L + "gdn_fwd")
            nw = small["od_norm_w"][j][None, :]
            y = _gdn_post_fwd(o, h, nw, name=L + "gdn_post")
            mix = _mm(y, W["od_w_out"][j], name=L + "od_out")
            rec.update(h=h, q=q, k=k, v=v, bb=bb, ab=ab, alog=alog, dtb=dtb, states=states, o=o, y=y, nw=nw, cw=cw)
        z1, x1, x1b = _ln_fwd(xf, mix, small["ln1_g"][layer][None], small["ln1_b"][layer][None], name=L + "ln1")
        u = _mm(x1b, W["ffn_w_up"][layer], name=L + "ffn_up")
        fcw = W["ffn_conv_w"][layer]
        fcb = small["ffn_conv_b"][layer][None]
        a = _ffn_mid_fwd(u, fcw, fcb, name=L + "ffn_mid")
        f = _mm(a, W["ffn_w_down"][layer], name=L + "ffn_down")
        z2, x2, x2b = _ln_fwd(x1, f, small["ln2_g"][layer][None], small["ln2_b"][layer][None], name=L + "ln2")
        rec.update(z1=z1, x1b=x1b, u=u, a=a, z2=z2, fcw=fcw, fcb=fcb)
        saved.append(rec)
        xf, xb = x2, x2b

    dy, lossv = _loss_head(xf, target, name="loss_head")
    loss = lossv[0, 0]

    gW = {n: [None] * W[n].shape[0] for n in W}
    gS = {n: [None] * small[n].shape[0] for n in small}
    dres, dmm = dy, None
    for layer in reversed(range(DEPTH)):
        j = layer // 2
        L = f"L{layer}_"
        rec = saved[layer]
        if dmm is None:
            dz2, dz2b, dg2, db2 = _ln_bwd(rec["z2"], small["ln2_g"][layer][None], dres, None, name=L + "ln2_bwd")
        else:
            dz2, dz2b, dg2, db2 = _ln_bwd(rec["z2"], small["ln2_g"][layer][None], dmm, dres, name=L + "ln2_bwd")
        gS["ln2_g"][layer], gS["ln2_b"][layer] = dg2[0], db2[0]
        gW["ffn_w_down"][layer] = _mm(rec["a"], dz2b, ta=True, name=L + "ffn_down_dw")
        da = _mm(dz2b, W["ffn_w_down"][layer], tb=True, name=L + "ffn_down_dx")
        du, dcw, dcb = _ffn_mid_bwd(rec["u"], rec["fcw"], rec["fcb"], da, name=L + "ffn_mid_bwd")
        gW["ffn_conv_w"][layer] = dcw
        gS["ffn_conv_b"][layer] = dcb[0]
        gW["ffn_w_up"][layer] = _mm(rec["x1b"], du, ta=True, name=L + "ffn_up_dw")
        dx1 = _mm(du, W["ffn_w_up"][layer], tb=True, name=L + "ffn_up_dx")
        dz1, dz1b, dg1, db1 = _ln_bwd(rec["z1"], small["ln1_g"][layer][None], dx1, dz2, name=L + "ln1_bwd")
        gS["ln1_g"][layer], gS["ln1_b"][layer] = dg1[0], db1[0]
        if layer % 2 == 0:
            gW["ev_w_out"][j] = _mm(rec["y"], dz1b, ta=True, name=L + "ev_out_dw")
            dyy = _mm(dz1b, W["ev_w_out"][j], tb=True, name=L + "ev_out_dx")
            dqa, dka, dva, dga = _ret_bwd(rec["h"], c2a, s2a, lgt, rec["ro"], dyy, name=L + "ret_bwd")
            dqb, dkb, dvb = _dil_bwd(rec["h"], cb, shi, slo, strip, rec["dil_o"], rec["lse"], dyy, name=L + "dil_bwd")
            dh = jnp.concatenate([dqa, dka, dva, dga, dqb, dkb, dvb], 1)
            gW["ev_w_in"][j] = _mm(rec["xb"], dh, ta=True, name=L + "ev_in_dw")
            dxin = _mm(dh, W["ev_w_in"][j], tb=True, name=L + "ev_in_dx")
        else:
            gW["od_w_out"][j] = _mm(rec["y"], dz1b, ta=True, name=L + "od_out_dw")
            dyy = _mm(dz1b, W["od_w_out"][j], tb=True, name=L + "od_out_dx")
            do, dgate, dnw = _gdn_post_bwd(rec["o"], rec["h"], rec["nw"], dyy, name=L + "gdn_post_bwd")
            gS["od_norm_w"][j] = dnw[0]
            dq, dk, dv, dbb, dab, dal, ddt = _gdn_core_bwd(
                rec["q"], rec["k"], rec["v"], rec["bb"], rec["ab"], rec["alog"], rec["dtb"], rec["states"], do,
                name=L + "gdn_bwd")
            gS["od_a_log"][j] = dal[:, 0, 0]
            gS["od_dt_bias"][j] = ddt[:, 0, 0]
            dhq, dhk, dhv, dwq, dwk, dwv = _gdn_prep_bwd(rec["h"], rec["cw"], dq, dk, dv, name=L + "gdn_prep_bwd")
            gW["od_conv_w"][j] = jnp.concatenate([dwq, dwk, dwv], 1)
            dsm = jnp.concatenate([dbb[:, :, 0].T, dab[:, :, 0].T,
                                   jnp.zeros((t, LANES - 2 * GDN_HEADS), F32)], 1).astype(BF16)
            dh = jnp.concatenate([dhq, dhk, dhv, dgate, dsm], 1)
            gW["od_w_in"][j] = _mm(rec["xb"], dh, ta=True, name=L + "od_in_dw")
            dxin = _mm(dh, W["od_w_in"][j], tb=True, name=L + "od_in_dx")
        dres, dmm = dz1, dxin
    grad_x = _axpy(dmm, dres, name="grad_x")
    gW = {n: jnp.stack(v) for n, v in gW.items()}
    gS = {n: jnp.stack(v) for n, v in gS.items()}
    return loss, grad_x, gW, gS


HBM = pl.BlockSpec(memory_space=pltpu.HBM)


def _me():
    return lax.axis_index("x"), lax.axis_index("y"), lax.axis_index("c")


def _all_gather(shards, *, name):
    n = len(shards)

    def body(*refs):
        ins, outs = refs[:n], refs[n:2 * n]
        send_sems, recv_sems, local_sems = refs[2 * n:]
        x, y, c = _me()
        me, sibling = (x, y, c), (x, y, 1 - c)
        chips = [(1 - x, y), (x, 1 - y), (1 - x, 1 - y)]

        def slot(out, px, py, pc):
            return out.at[4 * px + 2 * py + pc]

        def copy(a, kk, block, to, src=None):
            return pltpu.make_async_remote_copy(
                src_ref=slot(outs[a], *block) if src is None else src, dst_ref=slot(outs[a], *block),
                send_sem=send_sems.at[a, kk], recv_sem=recv_sems.at[a, kk], device_id=to, device_id_type=MESH)

        mine = [pltpu.make_async_copy(ins[a], slot(outs[a], *me), local_sems.at[a]) for a in range(n)]
        for cp in mine:
            cp.start()
        first = []
        for a in range(n):
            first.append(copy(a, 0, me, sibling, src=ins[a]))
            first += [copy(a, 1 + jj, me, (*chip, c), src=ins[a]) for jj, chip in enumerate(chips)]
        for cp in first:
            cp.start()
        passed = []
        for jj, chip in enumerate(chips):
            for a in range(n):
                copy(a, 1 + jj, (*chip, c), me).wait_recv()
                cp = copy(a, 4 + jj, (*chip, c), sibling)
                cp.start()
                passed.append(cp)
        for a in range(n):
            copy(a, 0, sibling, me).wait_recv()
            for jj, chip in enumerate(chips):
                copy(a, 4 + jj, (*chip, 1 - c), me).wait_recv()
        for cp in first + passed:
            cp.wait_send()
        for cp in mine:
            cp.wait()

    return pl.pallas_call(
        body, in_specs=[HBM] * n, out_specs=[HBM] * n,
        out_shape=[jax.ShapeDtypeStruct((N_DEV, *s.shape), s.dtype) for s in shards],
        scratch_shapes=[pltpu.SemaphoreType.DMA((n, 7)), pltpu.SemaphoreType.DMA((n, 7)), pltpu.SemaphoreType.DMA((n,))],
        name=name, compiler_params=pltpu.CompilerParams(has_side_effects=True))(*shards)


def _sibling_exchange(gs, *, name):
    n = len(gs)

    def body(*refs):
        ins, outs = refs[:n], refs[n:2 * n]
        send_sems, recv_sems = refs[2 * n:]
        x, y, c = _me()
        cps = [pltpu.make_async_remote_copy(
            src_ref=ins[a].at[:, 1 - c], dst_ref=outs[a], send_sem=send_sems.at[a], recv_sem=recv_sems.at[a],
            device_id=(x, y, 1 - c), device_id_type=MESH) for a in range(n)]
        for cp in cps:
            cp.start()
        for cp in cps:
            cp.wait()

    return pl.pallas_call(
        body, in_specs=[HBM] * n, out_specs=[HBM] * n,
        out_shape=[jax.ShapeDtypeStruct((4, *g.shape[2:]), g.dtype) for g in gs],
        scratch_shapes=[pltpu.SemaphoreType.DMA((n,)), pltpu.SemaphoreType.DMA((n,))],
        name=name, compiler_params=pltpu.CompilerParams(has_side_effects=True))(*gs)


def _pair_add(g, r, cidx, *, name):
    _, _, rr, cc = g.shape
    tr = rr
    for cand in (512, 384, 256, 192, 176, 128, 64, 32, 16, 8):
        if rr % cand == 0:
            tr = cand
            break
    if rr < 8:
        tr = rr

    def body(c_ref, g_ref, r_ref, ob_ref, of_ref):
        s = g_ref[0, 0] + r_ref[0]
        ob_ref[0] = s.astype(BF16)
        of_ref[0] = s

    grid_spec = pltpu.PrefetchScalarGridSpec(
        num_scalar_prefetch=1, grid=(4, rr // tr),
        in_specs=[pl.BlockSpec((1, 1, tr, cc), lambda kk, i, c_ref: (kk, c_ref[0], i, 0)),
                  pl.BlockSpec((1, tr, cc), lambda kk, i, c_ref: (kk, i, 0))],
        out_specs=[pl.BlockSpec((1, tr, cc), lambda kk, i, c_ref: (kk, i, 0)),
                   pl.BlockSpec((1, tr, cc), lambda kk, i, c_ref: (kk, i, 0))])
    return pl.pallas_call(
        body, grid_spec=grid_spec,
        out_shape=[jax.ShapeDtypeStruct((4, rr, cc), BF16), jax.ShapeDtypeStruct((4, rr, cc), F32)],
        name=name, compiler_params=_cp())(cidx, g, r)


def _chip_exchange(ps, *, name):
    n = len(ps)

    def body(*refs):
        ins, outs = refs[:n], refs[n:2 * n]
        send_sems, recv_sems = refs[2 * n:]
        x, y, c = _me()
        chips = [(1 - x, y), (x, 1 - y), (1 - x, 1 - y)]
        cps = []
        for a in range(n):
            for jj, (px, py) in enumerate(chips):
                cps.append(pltpu.make_async_remote_copy(
                    src_ref=ins[a].at[2 * px + py], dst_ref=outs[a].at[jj],
                    send_sem=send_sems.at[a, jj], recv_sem=recv_sems.at[a, jj],
                    device_id=(px, py, c), device_id_type=MESH))
        for cp in cps:
            cp.start()
        for cp in cps:
            cp.wait()

    return pl.pallas_call(
        body, in_specs=[HBM] * n, out_specs=[HBM] * n,
        out_shape=[jax.ShapeDtypeStruct((3, *p.shape[1:]), p.dtype) for p in ps],
        scratch_shapes=[pltpu.SemaphoreType.DMA((n, 3)), pltpu.SemaphoreType.DMA((n, 3))],
        name=name, compiler_params=pltpu.CompilerParams(has_side_effects=True))(*ps)


def _small_exchange(vec, *, name):
    rr = vec.shape[0]

    def body(v_ref, o_ref, send_sems, recv_sems):
        x, y, c = _me()
        myid = 4 * x + 2 * y + c
        o_ref[myid] = v_ref[...]
        cps = []
        for kk in range(1, N_DEV):
            px, py, pc = x ^ (kk >> 2), y ^ ((kk >> 1) & 1), c ^ (kk & 1)
            cps.append(pltpu.make_async_remote_copy(
                src_ref=v_ref, dst_ref=o_ref.at[myid], send_sem=send_sems.at[kk], recv_sem=recv_sems.at[kk],
                device_id=(px, py, pc), device_id_type=MESH))
        for cp in cps:
            cp.start()
        for kk in range(1, N_DEV):
            px, py, pc = x ^ (kk >> 2), y ^ ((kk >> 1) & 1), c ^ (kk & 1)
            pltpu.make_async_remote_copy(
                src_ref=v_ref, dst_ref=o_ref.at[4 * px + 2 * py + pc], send_sem=send_sems.at[kk],
                recv_sem=recv_sems.at[kk], device_id=(px, py, pc), device_id_type=MESH).wait_recv()
        for cp in cps:
            cp.wait_send()

    return pl.pallas_call(
        body, in_specs=[pl.BlockSpec(memory_space=pltpu.VMEM)], out_specs=pl.BlockSpec(memory_space=pltpu.VMEM),
        out_shape=jax.ShapeDtypeStruct((N_DEV, rr, LANES), F32),
        scratch_shapes=[pltpu.SemaphoreType.DMA((N_DEV,)), pltpu.SemaphoreType.DMA((N_DEV,))],
        name=name, compiler_params=pltpu.CompilerParams(has_side_effects=True))(vec)


def _adam_math(w, g, m, v):
    m = ADAM_B1 * m + (1.0 - ADAM_B1) * g
    v = ADAM_B2 * v + (1.0 - ADAM_B2) * (g * g)
    m_hat = m / (1.0 - ADAM_B1 ** ADAM_STEP)
    v_hat = v / (1.0 - ADAM_B2 ** ADAM_STEP)
    delta = -ADAM_LR * (m_hat / (jnp.sqrt(v_hat) + ADAM_EPS) + ADAM_WD * w)
    return delta, m, v


def _adamw_sharded(w, m, v, own, recv, *, name):
    rr, cc = w.shape
    tr = rr
    for cand in (512, 384, 256, 192, 176, 128, 64, 32, 16, 8):
        if rr % cand == 0:
            tr = cand
            break

    def body(w_ref, m_ref, v_ref, own_ref, r_ref, g_ref, d_ref, nm_ref, nv_ref):
        g = ((own_ref[...] + r_ref[0].astype(F32)) + r_ref[1].astype(F32)) + r_ref[2].astype(F32)
        d, nm, nv = _adam_math(w_ref[...], g, m_ref[...], v_ref[...])
        g_ref[...] = g
        d_ref[...] = d
        nm_ref[...] = nm
        nv_ref[...] = nv

    blk = pl.BlockSpec((tr, cc), lambda i: (i, 0))
    sh = jax.ShapeDtypeStruct((rr, cc), F32)
    return pl.pallas_call(
        body, grid=(rr // tr,), in_specs=[blk, blk, blk, blk, pl.BlockSpec((3, tr, cc), lambda i: (0, i, 0))],
        out_specs=[blk] * 4, out_shape=[sh] * 4, name=name, compiler_params=_cp())(w, m, v, own, recv)


def _adamw_small(w, m, v, gall, *, name):
    rr = w.shape[0]

    def body(w_ref, m_ref, v_ref, g_ref, go_ref, d_ref, nm_ref, nv_ref):
        g = g_ref[0]
        for kk in range(1, N_DEV):
            g = g + g_ref[kk]
        d, nm, nv = _adam_math(w_ref[...], g, m_ref[...], v_ref[...])
        go_ref[...] = g
        d_ref[...] = d
        nm_ref[...] = nm
        nv_ref[...] = nv

    sh = jax.ShapeDtypeStruct((rr, LANES), F32)
    return pl.pallas_call(body, out_shape=[sh] * 4, name=name, compiler_params=_cp())(w, m, v, gall)


COL_SHARDED = ("ev_w_in", "od_w_in", "od_conv_w", "ffn_w_up", "ffn_conv_w")
ROW_SHARDED = ("ev_w_out", "od_w_out", "ffn_w_down")
BIG = ("ev_w_in", "ev_w_out", "od_w_in", "od_w_out", "ffn_w_up", "ffn_w_down")
SHARDED = ("ev_w_in", "ev_w_out", "od_w_in", "od_conv_w", "od_w_out", "ffn_w_up", "ffn_conv_w", "ffn_w_down")
SMALL = ("od_a_log", "od_dt_bias", "od_norm_w", "ffn_conv_b", "ln1_g", "ln1_b", "ln2_g", "ln2_b")
ALL_W = ("ev_w_in", "ev_w_out", "od_w_in", "od_conv_w", "od_a_log", "od_dt_bias", "od_norm_w", "od_w_out",
         "ffn_w_up", "ffn_conv_w", "ffn_conv_b", "ffn_w_down", "ln1_g", "ln1_b", "ln2_g", "ln2_b")


def _unshard(name, g):
    if name in COL_SHARDED:
        _, l, r, c = g.shape
        return jnp.transpose(g, (1, 2, 0, 3)).reshape(l, r, N_DEV * c)
    _, l, r, c = g.shape
    return jnp.transpose(g, (1, 0, 2, 3)).reshape(l, N_DEV * r, c)


def _reshard(name, full):
    l = full.shape[0]
    if name in COL_SHARDED:
        r, c8 = full.shape[1:]
        return jnp.transpose(full.reshape(l, r, N_DEV, c8 // N_DEV), (2, 0, 1, 3))
    r8, c = full.shape[1:]
    return jnp.transpose(full.reshape(l, N_DEV, r8 // N_DEV, c), (1, 0, 2, 3))


def _pack_small(d):
    flat = jnp.concatenate([d[n].reshape(-1) for n in SMALL])
    pad = (-flat.shape[0]) % (8 * LANES)
    return jnp.pad(flat, (0, pad)).reshape(-1, LANES)


def _unpack_small(packed, like):
    flat = packed.reshape(-1)
    out, off = {}, 0
    for n in SMALL:
        sz = int(np.prod(like[n].shape))
        out[n] = flat[off:off + sz].reshape(like[n].shape)
        off += sz
    return out


def kernel(x, positions, ev_w_in, ev_w_out, od_w_in, od_conv_w, od_a_log, od_dt_bias, od_norm_w, od_w_out, ffn_w_up, ffn_conv_w, ffn_conv_b, ffn_w_down, ln1_g, ln1_b, ln2_g, ln2_b, loss_target, m_ev_w_in, m_ev_w_out, m_od_w_in, m_od_conv_w, m_od_a_log, m_od_dt_bias, m_od_norm_w, m_od_w_out, m_ffn_w_up, m_ffn_conv_w, m_ffn_conv_b, m_ffn_w_down, m_ln1_g, m_ln1_b, m_ln2_g, m_ln2_b, v_ev_w_in, v_ev_w_out, v_od_w_in, v_od_conv_w, v_od_a_log, v_od_dt_bias, v_od_norm_w, v_od_w_out, v_ffn_w_up, v_ffn_conv_w, v_ffn_conv_b, v_ffn_w_down, v_ln1_g, v_ln1_b, v_ln2_g, v_ln2_b):
    w = dict(ev_w_in=ev_w_in, ev_w_out=ev_w_out, od_w_in=od_w_in, od_conv_w=od_conv_w, od_a_log=od_a_log,
             od_dt_bias=od_dt_bias, od_norm_w=od_norm_w, od_w_out=od_w_out, ffn_w_up=ffn_w_up, ffn_conv_w=ffn_conv_w,
             ffn_conv_b=ffn_conv_b, ffn_w_down=ffn_w_down, ln1_g=ln1_g, ln1_b=ln1_b, ln2_g=ln2_g, ln2_b=ln2_b)
    mom = dict(ev_w_in=m_ev_w_in, ev_w_out=m_ev_w_out, od_w_in=m_od_w_in, od_conv_w=m_od_conv_w, od_a_log=m_od_a_log,
               od_dt_bias=m_od_dt_bias, od_norm_w=m_od_norm_w, od_w_out=m_od_w_out, ffn_w_up=m_ffn_w_up,
               ffn_conv_w=m_ffn_conv_w, ffn_conv_b=m_ffn_conv_b, ffn_w_down=m_ffn_w_down, ln1_g=m_ln1_g,
               ln1_b=m_ln1_b, ln2_g=m_ln2_g, ln2_b=m_ln2_b)
    var = dict(ev_w_in=v_ev_w_in, ev_w_out=v_ev_w_out, od_w_in=v_od_w_in, od_conv_w=v_od_conv_w, od_a_log=v_od_a_log,
               od_dt_bias=v_od_dt_bias, od_norm_w=v_od_norm_w, od_w_out=v_od_w_out, ffn_w_up=v_ffn_w_up,
               ffn_conv_w=v_ffn_conv_w, ffn_conv_b=v_ffn_conv_b, ffn_w_down=v_ffn_w_down, ln1_g=v_ln1_g,
               ln1_b=v_ln1_b, ln2_g=v_ln2_g, ln2_b=v_ln2_b)

    shards = [w[n].astype(BF16) if n in BIG else w[n] for n in SHARDED]
    gathered = _all_gather(shards, name="weights_all_gather")
    full = {n: _unshard(n, g) for n, g in zip(SHARDED, gathered)}
    full["od_w_in"] = jnp.pad(full["od_w_in"], ((0, 0), (0, 0), (0, OD_IN_PAD - OD_IN)))
    small = {n: w[n] for n in SMALL}

    loss, grad_x, gW, gS = _local_step(x[0], positions[0], loss_target[0], full, small)
    gW["od_w_in"] = gW["od_w_in"][:, :, :OD_IN]
    loss = lax.psum(loss, ("x", "y", "c"))

    cidx = lax.axis_index("c").astype(jnp.int32).reshape(1)
    by_owner = [_reshard(n, gW[n]) for n in SHARDED]
    by_owner = [g.reshape(4, 2, -1, g.shape[-1]) for g in by_owner]
    from_sibling = _sibling_exchange(by_owner, name="grads_sibling_exchange")
    sums = [_pair_add(g, r, cidx, name=f"grads_pair_add_{n}") for n, g, r in zip(SHARDED, by_owner, from_sibling)]
    from_chips = _chip_exchange([s[0] for s in sums], name="grads_chip_exchange")
    mychip = 2 * lax.axis_index("x") + lax.axis_index("y")

    outs_g, outs_d, outs_m, outs_v = {}, {}, {}, {}
    for n, s, r in zip(SHARDED, sums, from_chips):
        own = lax.dynamic_index_in_dim(s[1], mychip, 0, keepdims=False)
        shp = w[n].shape
        two = lambda a: a.reshape(-1, shp[-1])
        g, d, nm, nv = _adamw_sharded(two(w[n]), two(mom[n]), two(var[n]), own, r, name=f"adamw_{n}")
        outs_g[n], outs_d[n], outs_m[n], outs_v[n] = g.reshape(shp), d.reshape(shp), nm.reshape(shp), nv.reshape(shp)

    gall = _small_exchange(_pack_small(gS), name="small_grads_exchange")
    g, d, nm, nv = _adamw_small(_pack_small({n: w[n] for n in SMALL}), _pack_small({n: mom[n] for n in SMALL}),
                                _pack_small({n: var[n] for n in SMALL}), gall, name="adamw_small")
    for dst, packed in ((outs_g, g), (outs_d, d), (outs_m, nm), (outs_v, nv)):
        dst.update(_unpack_small(packed, {n: w[n] for n in SMALL}))

    return (loss, grad_x[None], *[outs_g[n] for n in ALL_W], *[outs_d[n] for n in ALL_W],
            *[outs_m[n] for n in ALL_W], *[outs_v[n] for n in ALL_W])
```

```python
import functools
import math

import numpy as np
import jax
import jax.numpy as jnp
from jax import lax
from jax.experimental import pallas as pl
from jax.experimental.pallas import tpu as pltpu

F32 = jnp.float32
BF16 = jnp.bfloat16
MESH = pl.DeviceIdType.MESH

D_MODEL = 1024
SEQ = 2048
DEPTH = 4
N_DEV = 8
RET_HEADS, RET_DK, RET_DV = 4, 128, 256
RET_THETA = 10000.0
DIL_HEADS, DIL_HD = 8, 64
DIL_PAIRS = ((128, 1), (512, 4), (2048, 16))
ROPE_THETA = 500000.0
ROPE_DIMS = DIL_HD // 4
GDN_HEADS, GDN_DK, GDN_DV, GDN_CHUNK, GDN_CONV = 8, 128, 128, 64, 4
D_FF = 2816
FFN_CONV = 3
ALPHA = (2.0 * DEPTH) ** 0.25
EPS = 1e-5
RET_QK_W = RET_HEADS * RET_DK
RET_V_W = RET_HEADS * RET_DV
DIL_W = DIL_HEADS * DIL_HD
EV_IN = 2 * RET_QK_W + 2 * RET_V_W + 3 * DIL_W
EV_MIX = RET_V_W + DIL_W
GDN_W = GDN_HEADS * GDN_DK
OD_IN = 4 * GDN_W + 2 * GDN_HEADS
OD_IN_PAD = 4 * GDN_W + 128
ADAM_LR, ADAM_B1, ADAM_B2, ADAM_EPS, ADAM_WD, ADAM_STEP = 0.001, 0.9, 0.999, 1e-08, 0.01, 10

LANES = 128
VMEM_LIMIT = 56 * 1024 * 1024
ATT_BLK = 256
NEG = -1e30


def _cp(**kw):
    return pltpu.CompilerParams(vmem_limit_bytes=VMEM_LIMIT, **kw)


def _tile(n, cap):
    if n <= cap:
        return n
    best = None
    for t in range(LANES, cap + 1, LANES):
        if n % t == 0:
            best = t
    assert best is not None, (n, cap)
    return best


def _mm(a, b, *, ta=False, tb=False, name, out_dtype=F32):
    m = a.shape[1] if ta else a.shape[0]
    k = a.shape[0] if ta else a.shape[1]
    n = b.shape[0] if tb else b.shape[1]
    assert (b.shape[1] if tb else b.shape[0]) == k
    assert a.dtype == BF16 and b.dtype == BF16
    tm = _tile(m, 1024 if k <= 2048 else 512)
    tn = _tile(n, 512)
    dims = (((0 if ta else 1,), (1 if tb else 0,)), ((), ()))

    def body(a_ref, b_ref, o_ref):
        o_ref[...] = lax.dot_general(a_ref[...], b_ref[...], dims,
                                     preferred_element_type=F32).astype(o_ref.dtype)

    a_spec = pl.BlockSpec((k, tm), lambda i, j: (0, i)) if ta else pl.BlockSpec((tm, k), lambda i, j: (i, 0))
    b_spec = pl.BlockSpec((tn, k), lambda i, j: (j, 0)) if tb else pl.BlockSpec((k, tn), lambda i, j: (0, j))
    return pl.pallas_call(
        body, grid=(m // tm, n // tn), in_specs=[a_spec, b_spec],
        out_specs=pl.BlockSpec((tm, tn), lambda i, j: (i, j)),
        out_shape=jax.ShapeDtypeStruct((m, n), out_dtype), name=name, compiler_params=_cp())(a, b)


LN_ROWS = 256


def _ln_fwd(x, m, g, b, *, name):
    t, d = x.shape

    def body(x_ref, m_ref, g_ref, b_ref, z_ref, y_ref, yb_ref):
        z = ALPHA * x_ref[...] + m_ref[...]
        mu = jnp.mean(z, -1, keepdims=True)
        zc = z - mu
        var = jnp.mean(zc * zc, -1, keepdims=True)
        y = zc * lax.rsqrt(var + EPS) * g_ref[...] + b_ref[...]
        z_ref[...] = z
        y_ref[...] = y
        yb_ref[...] = y.astype(BF16)

    row = pl.BlockSpec((LN_ROWS, d), lambda i: (i, 0))
    vec = pl.BlockSpec((1, d), lambda i: (0, 0))
    return pl.pallas_call(
        body, grid=(t // LN_ROWS,), in_specs=[row, row, vec, vec], out_specs=[row, row, row],
        out_shape=[jax.ShapeDtypeStruct((t, d), F32), jax.ShapeDtypeStruct((t, d), F32),
                   jax.ShapeDtypeStruct((t, d), BF16)],
        name=name, compiler_params=_cp())(x, m, g, b)


def _ln_bwd(z, g, dya, dyb, *, name):
    t, d = z.shape
    two = dyb is not None

    def body(*refs):
        if two:
            z_ref, g_ref, dya_ref, dyb_ref, dz_ref, dzb_ref, dg_ref, db_ref = refs
            dy = dya_ref[...] + ALPHA * dyb_ref[...]
        else:
            z_ref, g_ref, dya_ref, dz_ref, dzb_ref, dg_ref, db_ref = refs
            dy = dya_ref[...]
        zz = z_ref[...]
        mu = jnp.mean(zz, -1, keepdims=True)
        zc = zz - mu
        var = jnp.mean(zc * zc, -1, keepdims=True)
        r = lax.rsqrt(var + EPS)
        xh = zc * r
        dxh = dy * g_ref[...]
        dz = r * (dxh - jnp.mean(dxh, -1, keepdims=True) - xh * jnp.mean(dxh * xh, -1, keepdims=True))
        dz_ref[...] = dz
        dzb_ref[...] = dz.astype(BF16)

        @pl.when(pl.program_id(0) == 0)
        def _():
            dg_ref[...] = jnp.zeros_like(dg_ref)
            db_ref[...] = jnp.zeros_like(db_ref)

        dg_ref[...] += jnp.sum(dy * xh, 0, keepdims=True)
        db_ref[...] += jnp.sum(dy, 0, keepdims=True)

    row = pl.BlockSpec((LN_ROWS, d), lambda i: (i, 0))
    vec = pl.BlockSpec((1, d), lambda i: (0, 0))
    ins = [z, g, dya] + ([dyb] if two else [])
    return pl.pallas_call(
        body, grid=(t // LN_ROWS,), in_specs=[row, vec, row] + ([row] if two else []),
        out_specs=[row, row, vec, vec],
        out_shape=[jax.ShapeDtypeStruct((t, d), F32), jax.ShapeDtypeStruct((t, d), BF16),
                   jax.ShapeDtypeStruct((1, d), F32), jax.ShapeDtypeStruct((1, d), F32)],
        name=name, compiler_params=_cp())(*ins)


def _axpy(a, b, *, name):
    t, d = a.shape

    def body(a_ref, b_ref, o_ref):
        o_ref[...] = a_ref[...] + ALPHA * b_ref[...]

    row = pl.BlockSpec((LN_ROWS, d), lambda i: (i, 0))
    return pl.pallas_call(body, grid=(t // LN_ROWS,), in_specs=[row, row], out_specs=row,
                          out_shape=jax.ShapeDtypeStruct((t, d), F32), name=name, compiler_params=_cp())(a, b)


def _loss_head(y, target, *, name):
    t, d = y.shape

    def body(y_ref, t_ref, dy_ref, l_ref):
        e = y_ref[...] - t_ref[...]
        dy_ref[...] = e * (1.0 / d)

        @pl.when(pl.program_id(0) == 0)
        def _():
            l_ref[...] = jnp.zeros_like(l_ref)

        l_ref[...] += jnp.zeros_like(l_ref) + 0.5 * jnp.sum(jnp.mean(e * e, -1, keepdims=True), 0, keepdims=True)

    row = pl.BlockSpec((LN_ROWS, d), lambda i: (i, 0))
    return pl.pallas_call(
        body, grid=(t // LN_ROWS,), in_specs=[row, row],
        out_specs=[row, pl.BlockSpec((1, LANES), lambda i: (0, 0))],
        out_shape=[jax.ShapeDtypeStruct((t, d), F32), jax.ShapeDtypeStruct((1, LANES), F32)],
        name=name, compiler_params=_cp())(y, target)


def _sig(x):
    return 1.0 / (1.0 + jnp.exp(-x))


def _silu(x):
    return x * _sig(x)


def _dsilu(x):
    s = _sig(x)
    return s * (1.0 + x * (1.0 - s))


def _shift_down(u, k, row):
    if k == 0:
        return u
    return jnp.where(row >= k, pltpu.roll(u, k, 0), 0.0)


def _shift_up(u, k, row):
    if k == 0:
        return u
    t = u.shape[0]
    return jnp.where(row < t - k, pltpu.roll(u, t - k, 0), 0.0)


def _dwconv(u, w_ref, row):
    kk = w_ref.shape[0]
    acc = None
    for j in range(kk):
        term = w_ref[j:j + 1, :] * _shift_down(u, kk - 1 - j, row)
        acc = term if acc is None else acc + term
    return acc


def _dwconv_bwd(u, w_ref, dc, row, dw_ref):
    kk = w_ref.shape[0]
    du = None
    for j in range(kk):
        term = w_ref[j:j + 1, :] * _shift_up(dc, kk - 1 - j, row)
        du = term if du is None else du + term
        dw_ref[j:j + 1, :] = jnp.sum(dc * _shift_down(u, kk - 1 - j, row), 0, keepdims=True)
    return du


FFN_TC = 256


def _ffn_mid_fwd(u, cw, cb, *, name):
    t = u.shape[0]
    nb = D_FF // FFN_TC

    def body(ug_ref, uv_ref, wg_ref, wv_ref, bg_ref, bv_ref, a_ref):
        row = lax.broadcasted_iota(jnp.int32, (t, FFN_TC), 0)
        cg = _dwconv(ug_ref[...], wg_ref, row) + bg_ref[...]
        cv = _dwconv(uv_ref[...], wv_ref, row) + bv_ref[...]
        a_ref[...] = (_silu(cg) * cv).astype(BF16)

    col = lambda off: pl.BlockSpec((t, FFN_TC), lambda j: (0, j + off))
    wsp = lambda off: pl.BlockSpec((FFN_CONV, FFN_TC), lambda j: (0, j + off))
    bsp = lambda off: pl.BlockSpec((1, FFN_TC), lambda j: (0, j + off))
    return pl.pallas_call(
        body, grid=(nb,), in_specs=[col(0), col(nb), wsp(0), wsp(nb), bsp(0), bsp(nb)],
        out_specs=pl.BlockSpec((t, FFN_TC), lambda j: (0, j)),
        out_shape=jax.ShapeDtypeStruct((t, D_FF), BF16), name=name, compiler_params=_cp())(u, u, cw, cw, cb, cb)


def _ffn_mid_bwd(u, cw, cb, da, *, name):
    t = u.shape[0]
    nb = D_FF // FFN_TC

    def body(ug_ref, uv_ref, wg_ref, wv_ref, bg_ref, bv_ref, da_ref,
             dug_ref, duv_ref, dwg_ref, dwv_ref, dbg_ref, dbv_ref):
        row = lax.broadcasted_iota(jnp.int32, (t, FFN_TC), 0)
        ug, uv = ug_ref[...], uv_ref[...]
        cg = _dwconv(ug, wg_ref, row) + bg_ref[...]
        cv = _dwconv(uv, wv_ref, row) + bv_ref[...]
        da_ = da_ref[...]
        dcv = da_ * _silu(cg)
        dcg = da_ * cv * _dsilu(cg)
        dug_ref[...] = _dwconv_bwd(ug, wg_ref, dcg, row, dwg_ref).astype(BF16)
        duv_ref[...] = _dwconv_bwd(uv, wv_ref, dcv, row, dwv_ref).astype(BF16)
        dbg_ref[...] = jnp.sum(dcg, 0, keepdims=True)
        dbv_ref[...] = jnp.sum(dcv, 0, keepdims=True)

    col = lambda off: pl.BlockSpec((t, FFN_TC), lambda j: (0, j + off))
    wsp = lambda off: pl.BlockSpec((FFN_CONV, FFN_TC), lambda j: (0, j + off))
    bsp = lambda off: pl.BlockSpec((1, FFN_TC), lambda j: (0, j + off))
    outs = pl.pallas_call(
        body, grid=(nb,), in_specs=[col(0), col(nb), wsp(0), wsp(nb), bsp(0), bsp(nb), col(0)],
        out_specs=[col(0), col(0), wsp(0), wsp(0), bsp(0), bsp(0)],
        out_shape=[jax.ShapeDtypeStruct((t, D_FF), BF16), jax.ShapeDtypeStruct((t, D_FF), BF16),
                   jax.ShapeDtypeStruct((FFN_CONV, D_FF), F32), jax.ShapeDtypeStruct((FFN_CONV, D_FF), F32),
                   jax.ShapeDtypeStruct((1, D_FF), F32), jax.ShapeDtypeStruct((1, D_FF), F32)],
        name=name, compiler_params=_cp())(u, u, cw, cw, cb, cb, da)
    dug, duv, dwg, dwv, dbg, dbv = outs
    return (jnp.concatenate([dug, duv], 1), jnp.concatenate([dwg, dwv], 1), jnp.concatenate([dbg, dbv], 1))


def _rot_a(x, c2, s2):
    return x * c2 + pltpu.roll(x, RET_DK // 2, 1) * s2


def _rot_a_t(dy, c2, s2):
    return dy * c2 + pltpu.roll(dy * s2, RET_DK // 2, 1)


def _decay_tile(lg, blk_diff):
    r = lax.broadcasted_iota(jnp.int32, (ATT_BLK, ATT_BLK), 0)
    c = lax.broadcasted_iota(jnp.int32, (ATT_BLK, ATT_BLK), 1)
    rel = r - c + blk_diff * ATT_BLK
    return jnp.where(rel >= 0, jnp.exp(jnp.maximum(rel, 0).astype(F32) * lg), 0.0)


def _nt(a, b):
    return lax.dot_general(a, b, (((1,), (1,)), ((), ())), preferred_element_type=F32)


def _nn(a, b):
    return lax.dot_general(a, b, (((1,), (0,)), ((), ())), preferred_element_type=F32)


def _tn(a, b):
    return lax.dot_general(a, b, (((0,), (0,)), ((), ())), preferred_element_type=F32)


def _ret_specs(t):
    q = pl.BlockSpec((t, RET_DK), lambda h: (0, h))
    k = pl.BlockSpec((t, RET_DK), lambda h: (0, RET_HEADS + h))
    v = pl.BlockSpec((t, RET_DV), lambda h: (0, RET_HEADS + h))
    g = pl.BlockSpec((t, RET_DV), lambda h: (0, 2 * RET_HEADS + h))
    tab = pl.BlockSpec((t, RET_DK), lambda h: (0, 0))
    lg = pl.BlockSpec((1, 1, LANES), lambda h: (h, 0, 0))
    return q, k, v, g, tab, lg


def _ret_fwd(h, c2, s2, lgt, *, name):
    t = h.shape[0]
    nblk = t // ATT_BLK
    scale = RET_DK ** -0.5

    def body(q_ref, k_ref, v_ref, g_ref, c_ref, s_ref, lg_ref, o_ref, ya_ref, qs, ks, vs):
        c2_, s2_ = c_ref[...], s_ref[...]
        qs[...] = _rot_a(q_ref[...], c2_, s2_).astype(BF16)
        ks[...] = (_rot_a(k_ref[...], c2_, s2_) * scale).astype(BF16)
        vs[...] = v_ref[...].astype(BF16)
        lg = lg_ref[0, :, 0:1]
        for i in range(nblk):
            qi = qs[pl.ds(i * ATT_BLK, ATT_BLK), :]
            acc = jnp.zeros((ATT_BLK, RET_DV), F32)
            for j in range(i + 1):
                sl = pl.ds(j * ATT_BLK, ATT_BLK)
                s = _nt(qi, ks[sl, :]) * _decay_tile(lg, i - j)
                acc = acc + _nn(s.astype(BF16), vs[sl, :])
            rows = pl.ds(i * ATT_BLK, ATT_BLK)
            o_ref[rows, :] = acc
            r = lax.rsqrt(jnp.mean(acc * acc, -1, keepdims=True) + EPS)
            ya_ref[rows, :] = (acc * r * _silu(g_ref[rows, :])).astype(BF16)

    q, k, v, g, tab, lg = _ret_specs(t)
    out = pl.BlockSpec((t, RET_DV), lambda hh: (0, hh))
    return pl.pallas_call(
        body, grid=(RET_HEADS,), in_specs=[q, k, v, g, tab, tab, lg], out_specs=[out, out],
        out_shape=[jax.ShapeDtypeStruct((t, RET_V_W), F32), jax.ShapeDtypeStruct((t, RET_V_W), BF16)],
        scratch_shapes=[pltpu.VMEM((t, RET_DK), BF16), pltpu.VMEM((t, RET_DK), BF16), pltpu.VMEM((t, RET_DV), BF16)],
        name=name, compiler_params=_cp())(h, h, h, h, c2, s2, lgt)


def _ret_bwd(h, c2, s2, lgt, o, dy, *, name):
    t = h.shape[0]
    nblk = t // ATT_BLK
    scale = RET_DK ** -0.5

    def body(q_ref, k_ref, v_ref, g_ref, c_ref, s_ref, lg_ref, o_ref, dy_ref,
             dq_ref, dk_ref, dv_ref, dg_ref, qs, ks, vs, dos, dka, dva):
        c2_, s2_ = c_ref[...], s_ref[...]
        qs[...] = _rot_a(q_ref[...], c2_, s2_).astype(BF16)
        ks[...] = (_rot_a(k_ref[...], c2_, s2_) * scale).astype(BF16)
        vs[...] = v_ref[...].astype(BF16)
        lg = lg_ref[0, :, 0:1]
        oo = o_ref[...]
        gg = g_ref[...]
        dya = dy_ref[...]
        r = lax.rsqrt(jnp.mean(oo * oo, -1, keepdims=True) + EPS)
        rn = oo * r
        dg_ref[...] = (dya * rn * _dsilu(gg)).astype(BF16)
        drn = dya * _silu(gg)
        dos[...] = (r * (drn - rn * jnp.mean(drn * rn, -1, keepdims=True))).astype(BF16)
        dka[...] = jnp.zeros_like(dka)
        dva[...] = jnp.zeros_like(dva)
        for i in range(nblk):
            rows = pl.ds(i * ATT_BLK, ATT_BLK)
            qi = qs[rows, :]
            doi = dos[rows, :]
            dqa = jnp.zeros((ATT_BLK, RET_DK), F32)
            for j in range(i + 1):
                sl = pl.ds(j * ATT_BLK, ATT_BLK)
                dt_ = _decay_tile(lg, i - j)
                kj = ks[sl, :]
                s = (_nt(qi, kj) * dt_).astype(BF16)
                ds = (_nt(doi, vs[sl, :]) * dt_).astype(BF16)
                dqa = dqa + _nn(ds, kj)
                dka[sl, :] += _tn(ds, qi)
                dva[sl, :] += _tn(s, doi)
            dq_ref[rows, :] = _rot_a_t(dqa, c_ref[rows, :], s_ref[rows, :]).astype(BF16)
        dk_ref[...] = (_rot_a_t(dka[...], c2_, s2_) * scale).astype(BF16)
        dv_ref[...] = dva[...].astype(BF16)

    q, k, v, g, tab, lg = _ret_specs(t)
    blk_v = pl.BlockSpec((t, RET_DV), lambda hh: (0, hh))
    blk_k = pl.BlockSpec((t, RET_DK), lambda hh: (0, hh))
    return pl.pallas_call(
        body, grid=(RET_HEADS,), in_specs=[q, k, v, g, tab, tab, lg, blk_v, blk_v],
        out_specs=[blk_k, blk_k, blk_v, blk_v],
        out_shape=[jax.ShapeDtypeStruct((t, RET_QK_W), BF16), jax.ShapeDtypeStruct((t, RET_QK_W), BF16),
                   jax.ShapeDtypeStruct((t, RET_V_W), BF16), jax.ShapeDtypeStruct((t, RET_V_W), BF16)],
        scratch_shapes=[pltpu.VMEM((t, RET_DK), BF16), pltpu.VMEM((t, RET_DK), BF16), pltpu.VMEM((t, RET_DV), BF16),
                        pltpu.VMEM((t, RET_DV), BF16), pltpu.VMEM((t, RET_DK), F32), pltpu.VMEM((t, RET_DV), F32)],
        name=name, compiler_params=_cp())(h, h, h, h, c2, s2, lgt, o, dy)


def _rot_b(x, cb, shi, slo):
    return x * cb + pltpu.roll(x, ROPE_DIMS // 2, 1) * shi + pltpu.roll(x, LANES - ROPE_DIMS // 2, 1) * slo


def _rot_b_t(dy, cb, shi, slo):
    return dy * cb + pltpu.roll(dy * shi, LANES - ROPE_DIMS // 2, 1) + pltpu.roll(dy * slo, ROPE_DIMS // 2, 1)


def _dil_specs(t):
    base = (2 * RET_QK_W + 2 * RET_V_W) // LANES
    npair = DIL_W // LANES
    q = pl.BlockSpec((t, LANES), lambda p: (0, base + p))
    k = pl.BlockSpec((t, LANES), lambda p: (0, base + npair + p))
    v = pl.BlockSpec((t, LANES), lambda p: (0, base + 2 * npair + p))
    tab = pl.BlockSpec((t, LANES), lambda p: (0, 0))
    strip = pl.BlockSpec((ATT_BLK, t), lambda p: (0, 0))
    pair = pl.BlockSpec((t, LANES), lambda p: (0, p))
    return q, k, v, tab, strip, pair


def _dil_fwd(h, cb, shi, slo, strip, *, name):
    t = h.shape[0]
    nblk = t // ATT_BLK
    scale = DIL_HD ** -0.5

    def body(q_ref, k_ref, v_ref, cb_ref, shi_ref, slo_ref, st_ref, o_ref, yb_ref, lse_ref, qs, ks, vs):
        cb_, shi_, slo_ = cb_ref[...], shi_ref[...], slo_ref[...]
        lane = lax.broadcasted_iota(jnp.int32, (t, LANES), 1)
        qr = _rot_b(q_ref[...], cb_, shi_, slo_) * scale
        qs[0] = jnp.where(lane < DIL_HD, qr, 0.0).astype(BF16)
        qs[1] = jnp.where(lane >= DIL_HD, qr, 0.0).astype(BF16)
        ks[...] = _rot_b(k_ref[...], cb_, shi_, slo_).astype(BF16)
        vs[...] = v_ref[...].astype(BF16)
        lane_b = lax.broadcasted_iota(jnp.int32, (ATT_BLK, LANES), 1)
        for i in range(nblk):
            w = (i + 1) * ATT_BLK
            rows = pl.ds(i * ATT_BLK, ATT_BLK)
            logc = st_ref[:, t - w:t]
            outs, lses = [], []
            for hd in range(2):
                s = _nt(qs[hd, rows, :], ks[0:w, :]) + logc
                m = jnp.max(s, -1, keepdims=True)
                p = jnp.exp(s - m)
                l = jnp.sum(p, -1, keepdims=True)
                outs.append(_nn(p.astype(BF16), vs[0:w, :]) / l)
                lses.append(m + jnp.log(l))
            o = jnp.where(lane_b < DIL_HD, outs[0], outs[1])
            o_ref[rows, :] = o
            yb_ref[rows, :] = o.astype(BF16)
            lse_ref[rows, :] = jnp.where(lane_b < DIL_HD, lses[0], lses[1])

    q, k, v, tab, strip_spec, pair = _dil_specs(t)
    return pl.pallas_call(
        body, grid=(DIL_W // LANES,), in_specs=[q, k, v, tab, tab, tab, strip_spec], out_specs=[pair, pair, pair],
        out_shape=[jax.ShapeDtypeStruct((t, DIL_W), F32), jax.ShapeDtypeStruct((t, DIL_W), BF16),
                   jax.ShapeDtypeStruct((t, DIL_W), F32)],
        scratch_shapes=[pltpu.VMEM((2, t, LANES), BF16), pltpu.VMEM((t, LANES), BF16), pltpu.VMEM((t, LANES), BF16)],
        name=name, compiler_params=_cp())(h, h, h, cb, shi, slo, strip)


def _dil_bwd(h, cb, shi, slo, strip, o, lse, dy, *, name):
    t = h.shape[0]
    nblk = t // ATT_BLK
    scale = DIL_HD ** -0.5

    def body(q_ref, k_ref, v_ref, cb_ref, shi_ref, slo_ref, st_ref, o_ref, lse_ref, dy_ref,
             dq_ref, dk_ref, dv_ref, qs, ks, vs, dos, dls, dka, dva):
        cb_, shi_, slo_ = cb_ref[...], shi_ref[...], slo_ref[...]
        lane = lax.broadcasted_iota(jnp.int32, (t, LANES), 1)
        qr = _rot_b(q_ref[...], cb_, shi_, slo_) * scale
        qs[0] = jnp.where(lane < DIL_HD, qr, 0.0).astype(BF16)
        qs[1] = jnp.where(lane >= DIL_HD, qr, 0.0).astype(BF16)
        ks[...] = _rot_b(k_ref[...], cb_, shi_, slo_).astype(BF16)
        vs[...] = v_ref[...].astype(BF16)
        do = dy_ref[...]
        prod = do * o_ref[...]
        d0 = jnp.sum(jnp.where(lane < DIL_HD, prod, 0.0), -1, keepdims=True)
        d1 = jnp.sum(jnp.where(lane >= DIL_HD, prod, 0.0), -1, keepdims=True)
        dls[...] = jnp.where(lane < DIL_HD, d0, d1)
        dos[0] = jnp.where(lane < DIL_HD, do, 0.0).astype(BF16)
        dos[1] = jnp.where(lane >= DIL_HD, do, 0.0).astype(BF16)
        dka[...] = jnp.zeros_like(dka)
        dva[...] = jnp.zeros_like(dva)
        lane_b = lax.broadcasted_iota(jnp.int32, (ATT_BLK, LANES), 1)
        for i in range(nblk):
            w = (i + 1) * ATT_BLK
            rows = pl.ds(i * ATT_BLK, ATT_BLK)
            logc = st_ref[:, t - w:t]
            dqs = []
            for hd in range(2):
                col = hd * DIL_HD
                qh = qs[hd, rows, :]
                doh = dos[hd, rows, :]
                lse_h = lse_ref[rows, col:col + 1]
                dl_h = dls[rows, col:col + 1]
                p = jnp.exp(_nt(qh, ks[0:w, :]) + logc - lse_h)
                dp = _nt(doh, vs[0:w, :])
                ds = (p * (dp - dl_h)).astype(BF16)
                dqs.append(_nn(ds, ks[0:w, :]))
                dka[0:w, :] += _tn(ds, qh)
                dva[0:w, :] += _tn(p.astype(BF16), doh)
            dq = jnp.where(lane_b < DIL_HD, dqs[0], dqs[1]) * scale
            dq_ref[rows, :] = _rot_b_t(dq, cb_ref[rows, :], shi_ref[rows, :], slo_ref[rows, :]).astype(BF16)
        dk_ref[...] = _rot_b_t(dka[...], cb_, shi_, slo_).astype(BF16)
        dv_ref[...] = dva[...].astype(BF16)

    q, k, v, tab, strip_spec, pair = _dil_specs(t)
    dy_spec = pl.BlockSpec((t, LANES), lambda p: (0, RET_V_W // LANES + p))
    return pl.pallas_call(
        body, grid=(DIL_W // LANES,), in_specs=[q, k, v, tab, tab, tab, strip_spec, pair, pair, dy_spec],
        out_specs=[pair, pair, pair],
        out_shape=[jax.ShapeDtypeStruct((t, DIL_W), BF16)] * 3,
        scratch_shapes=[pltpu.VMEM((2, t, LANES), BF16), pltpu.VMEM((t, LANES), BF16), pltpu.VMEM((t, LANES), BF16),
                        pltpu.VMEM((2, t, LANES), BF16), pltpu.VMEM((t, LANES), F32),
                        pltpu.VMEM((t, LANES), F32), pltpu.VMEM((t, LANES), F32)],
        name=name, compiler_params=_cp())(h, h, h, cb, shi, slo, strip, o, lse, dy)


def _gdn_prep_fwd(h, cw, *, name):
    t = h.shape[0]
    qscale = GDN_DK ** -0.5

    def body(hq_ref, hk_ref, hv_ref, wq_ref, wk_ref, wv_ref, q_ref, k_ref, v_ref):
        row = lax.broadcasted_iota(jnp.int32, (t, GDN_DK), 0)
        sq = _silu(_dwconv(hq_ref[...], wq_ref, row))
        sk = _silu(_dwconv(hk_ref[...], wk_ref, row))
        q_ref[0] = sq * lax.rsqrt(jnp.sum(sq * sq, -1, keepdims=True) + 1e-6) * qscale
        k_ref[0] = sk * lax.rsqrt(jnp.sum(sk * sk, -1, keepdims=True) + 1e-6)
        v_ref[0] = _silu(_dwconv(hv_ref[...], wv_ref, row))

    hs = lambda off: pl.BlockSpec((t, GDN_DK), lambda i: (0, i + off))
    ws = lambda off: pl.BlockSpec((GDN_CONV, GDN_DK), lambda i: (0, i + off))
    out = pl.BlockSpec((1, t, GDN_DK), lambda i: (i, 0, 0))
    return pl.pallas_call(
        body, grid=(GDN_HEADS,), in_specs=[hs(0), hs(8), hs(16), ws(0), ws(8), ws(16)], out_specs=[out, out, out],
        out_shape=[jax.ShapeDtypeStruct((GDN_HEADS, t, GDN_DK), F32)] * 3,
        name=name, compiler_params=_cp())(h, h, h, cw, cw, cw)


def _gdn_prep_bwd(h, cw, dq, dk, dv, *, name):
    t = h.shape[0]
    qscale = GDN_DK ** -0.5

    def body(hq_ref, hk_ref, hv_ref, wq_ref, wk_ref, wv_ref, dq_ref, dk_ref, dv_ref,
             dhq_ref, dhk_ref, dhv_ref, dwq_ref, dwk_ref, dwv_ref):
        row = lax.broadcasted_iota(jnp.int32, (t, GDN_DK), 0)

        def one(h_ref, w_ref, d_ref, dh_ref, dw_ref, norm, sc):
            u = h_ref[...]
            c = _dwconv(u, w_ref, row)
            d = d_ref[0]
            if norm:
                s = _silu(c)
                r = lax.rsqrt(jnp.sum(s * s, -1, keepdims=True) + 1e-6)
                n = s * r
                d = d * sc
                d = r * (d - n * jnp.sum(d * n, -1, keepdims=True))
            dc = d * _dsilu(c)
            dh_ref[...] = _dwconv_bwd(u, w_ref, dc, row, dw_ref).astype(BF16)

        one(hq_ref, wq_ref, dq_ref, dhq_ref, dwq_ref, True, qscale)
        one(hk_ref, wk_ref, dk_ref, dhk_ref, dwk_ref, True, 1.0)
        one(hv_ref, wv_ref, dv_ref, dhv_ref, dwv_ref, False, 1.0)

    hs = lambda off: pl.BlockSpec((t, GDN_DK), lambda i: (0, i + off))
    ws = lambda off: pl.BlockSpec((GDN_CONV, GDN_DK), lambda i: (0, i + off))
    hd = pl.BlockSpec((1, t, GDN_DK), lambda i: (i, 0, 0))
    return pl.pallas_call(
        body, grid=(GDN_HEADS,), in_specs=[hs(0), hs(8), hs(16), ws(0), ws(8), ws(16), hd, hd, hd],
        out_specs=[hs(0), hs(0), hs(0), ws(0), ws(0), ws(0)],
        out_shape=[jax.ShapeDtypeStruct((t, GDN_W), BF16)] * 3 + [jax.ShapeDtypeStruct((GDN_CONV, GDN_W), F32)] * 3,
        name=name, compiler_params=_cp())(h, h, h, cw, cw, cw, dq, dk, dv)


def _make_mm2(hi):
    def prep(x):
        return x if hi else x.astype(BF16)
    prec = lax.Precision.HIGHEST if hi else None

    def raw(a, b, dims):
        return lax.dot_general(prep(a), prep(b), (dims, ((), ())), precision=prec, preferred_element_type=F32)

    @jax.custom_vjp
    def nn(a, b):
        return raw(a, b, ((1,), (0,)))

    @jax.custom_vjp
    def nt(a, b):
        return raw(a, b, ((1,), (1,)))

    @jax.custom_vjp
    def tn(a, b):
        return raw(a, b, ((0,), (0,)))

    nn.defvjp(lambda a, b: (nn(a, b), (a, b)), lambda r, g: (nt(g, r[1]), tn(r[0], g)))
    nt.defvjp(lambda a, b: (nt(a, b), (a, b)), lambda r, g: (nn(g, r[1]), tn(g, r[0])))
    tn.defvjp(lambda a, b: (tn(a, b), (a, b)), lambda r, g: (nt(r[1], g), nn(r[0], g)))
    return nn, nt, tn


_NN, _NT, _TN = _make_mm2(False)
_NNH, _NTH, _TNH = _make_mm2(True)


@jax.custom_vjp
def _inv_unit_lower(l):
    c = l.shape[0]
    eye = (lax.broadcasted_iota(jnp.int32, (c, c), 0) == lax.broadcasted_iota(jnp.int32, (c, c), 1)).astype(F32)
    p = -l
    t = eye + p
    for _ in range(int(math.log2(c)) - 1):
        p = _NNH(p, p)
        t = t + _NNH(t, p)
    return t


def _inv_fwd(l):
    t = _inv_unit_lower(l)
    return t, t


def _inv_bwd(t, dt):
    return (-_NTH(_TNH(t, dt), t),)


_inv_unit_lower.defvjp(_inv_fwd, _inv_bwd)


def _softplus(x):
    return jnp.maximum(x, 0.0) + jnp.log1p(jnp.exp(-jnp.abs(x)))


def _gdn_chunk(q, k, v, braw, araw, alog, dtb, state):
    c = q.shape[0]
    ri = lax.broadcasted_iota(jnp.int32, (c, c), 0)
    ci = lax.broadcasted_iota(jnp.int32, (c, c), 1)
    tri = ri >= ci
    strict = ri > ci
    eye = (ri == ci).astype(F32)
    beta = _sig(braw)
    g = -jnp.exp(alog) * _softplus(araw + dtb)
    gcm = _NNH(tri.astype(F32), g * jnp.ones((c, c), F32))
    gct = _NTH(eye, gcm)
    decay = jnp.where(tri, jnp.exp(jnp.where(tri, gcm - gct, 0.0)), 0.0)
    gc = jnp.sum(gcm, 1, keepdims=True) * (1.0 / c)
    glast = jnp.sum(g, 0, keepdims=True)
    egc = jnp.exp(gc)
    kb = k * beta
    tm = _inv_unit_lower(jnp.where(strict, _NT(kb, k) * decay, 0.0))
    u = _NNH(tm, v * beta)
    w = _NNH(tm, kb * egc)
    attn = jnp.where(tri, _NT(q, k) * decay, 0.0)
    k_dec = k * jnp.exp(glast - gc)
    q_dec = q * egc
    v_new = u - _NN(w, state)
    o = _NN(q_dec, state) + _NN(attn, v_new)
    new_state = state * jnp.exp(glast) + _TN(k_dec, v_new)
    return o, new_state


def _gdn_specs(t, rev):
    nch = t // GDN_CHUNK
    cm = (lambda n: nch - 1 - n) if rev else (lambda n: n)
    tok = pl.BlockSpec((GDN_HEADS, GDN_CHUNK, GDN_DK), lambda n: (0, cm(n), 0))
    par = pl.BlockSpec((GDN_HEADS, 1, LANES), lambda n: (0, 0, 0))
    st = pl.BlockSpec((GDN_HEADS, 1, GDN_DK, GDN_DV), lambda n: (0, cm(n), 0, 0))
    return tok, par, st


def _gdn_core_fwd(q, k, v, bb, ab, alog, dtb, *, name):
    t = q.shape[1]
    nch = t // GDN_CHUNK

    def body(q_ref, k_ref, v_ref, bb_ref, ab_ref, al_ref, dt_ref, o_ref, st_ref, state):
        @pl.when(pl.program_id(0) == 0)
        def _():
            state[...] = jnp.zeros_like(state)

        s0 = state[...]
        st_ref[:, 0] = s0
        o, s1 = jax.vmap(_gdn_chunk)(q_ref[...], k_ref[...], v_ref[...], bb_ref[:, :, 0:1], ab_ref[:, :, 0:1],
                                     al_ref[:, :, 0:1], dt_ref[:, :, 0:1], s0)
        o_ref[...] = o
        state[...] = s1

    tok, par, st = _gdn_specs(t, False)
    return pl.pallas_call(
        body, grid=(nch,), in_specs=[tok, tok, tok, tok, tok, par, par], out_specs=[tok, st],
        out_shape=[jax.ShapeDtypeStruct((GDN_HEADS, t, GDN_DV), F32),
                   jax.ShapeDtypeStruct((GDN_HEADS, nch, GDN_DK, GDN_DV), F32)],
        scratch_shapes=[pltpu.VMEM((GDN_HEADS, GDN_DK, GDN_DV), F32)],
        name=name, compiler_params=_cp())(q, k, v, bb, ab, alog, dtb)


def _gdn_core_bwd(q, k, v, bb, ab, alog, dtb, states, do, *, name):
    t = q.shape[1]
    nch = t // GDN_CHUNK

    def body(q_ref, k_ref, v_ref, bb_ref, ab_ref, al_ref, dt_ref, st_ref, do_ref,
             dq_ref, dk_ref, dv_ref, dbb_ref, dab_ref, dal_ref, ddt_ref, dstate):
        @pl.when(pl.program_id(0) == 0)
        def _():
            dstate[...] = jnp.zeros_like(dstate)
            dal_ref[...] = jnp.zeros_like(dal_ref)
            ddt_ref[...] = jnp.zeros_like(ddt_ref)

        args = (q_ref[...], k_ref[...], v_ref[...], bb_ref[:, :, 0:1], ab_ref[:, :, 0:1],
                al_ref[:, :, 0:1], dt_ref[:, :, 0:1], st_ref[:, 0])
        _, pull = jax.vjp(jax.vmap(_gdn_chunk), *args)
        dq, dk, dv, dbr, dar, dal, ddt, ds = pull((do_ref[...], dstate[...]))
        dq_ref[...] = dq
        dk_ref[...] = dk
        dv_ref[...] = dv
        dbb_ref[...] = dbr + jnp.zeros((GDN_HEADS, GDN_CHUNK, LANES), F32)
        dab_ref[...] = dar + jnp.zeros((GDN_HEADS, GDN_CHUNK, LANES), F32)
        dal_ref[...] += dal + jnp.zeros((GDN_HEADS, 1, LANES), F32)
        ddt_ref[...] += ddt + jnp.zeros((GDN_HEADS, 1, LANES), F32)
        dstate[...] = ds

    tok, par, st = _gdn_specs(t, True)
    tokshape = jax.ShapeDtypeStruct((GDN_HEADS, t, GDN_DK), F32)
    parshape = jax.ShapeDtypeStruct((GDN_HEADS, 1, LANES), F32)
    return pl.pallas_call(
        body, grid=(nch,), in_specs=[tok, tok, tok, tok, tok, par, par, st, tok],
        out_specs=[tok, tok, tok, tok, tok, par, par],
        out_shape=[tokshape] * 5 + [parshape] * 2,
        scratch_shapes=[pltpu.VMEM((GDN_HEADS, GDN_DK, GDN_DV), F32)],
        name=name, compiler_params=_cp())(q, k, v, bb, ab, alog, dtb, states, do)


GDN_ROWS = 512


def _gdn_post_fwd(o, h, nw, *, name):
    t = o.shape[1]

    def body(o_ref, g_ref, nw_ref, y_ref):
        oo = o_ref[0]
        r = lax.rsqrt(jnp.mean(oo * oo, -1, keepdims=True) + EPS)
        y_ref[...] = (oo * r * nw_ref[...] * _silu(g_ref[...])).astype(BF16)

    return pl.pallas_call(
        body, grid=(GDN_HEADS, t // GDN_ROWS),
        in_specs=[pl.BlockSpec((1, GDN_ROWS, GDN_DV), lambda hh, i: (hh, i, 0)),
                  pl.BlockSpec((GDN_ROWS, GDN_DV), lambda hh, i: (i, 3 * GDN_HEADS + hh)),
                  pl.BlockSpec((1, GDN_DV), lambda hh, i: (0, 0))],
        out_specs=pl.BlockSpec((GDN_ROWS, GDN_DV), lambda hh, i: (i, hh)),
        out_shape=jax.ShapeDtypeStruct((t, GDN_W), BF16), name=name, compiler_params=_cp())(o, h, nw)


def _gdn_post_bwd(o, h, nw, dy, *, name):
    t = o.shape[1]

    def body(o_ref, g_ref, nw_ref, dy_ref, do_ref, dg_ref, dnw_ref):
        oo, gg, nw_, dy_ = o_ref[0], g_ref[...], nw_ref[...], dy_ref[...]
        r = lax.rsqrt(jnp.mean(oo * oo, -1, keepdims=True) + EPS)
        n = oo * r
        sg = _silu(gg)
        dg_ref[...] = (dy_ * n * nw_ * _dsilu(gg)).astype(BF16)
        dn = dy_ * sg * nw_
        do_ref[0] = r * (dn - n * jnp.mean(dn * n, -1, keepdims=True))

        @pl.when((pl.program_id(0) == 0) & (pl.program_id(1) == 0))
        def _():
            dnw_ref[...] = jnp.zeros_like(dnw_ref)

        dnw_ref[...] += jnp.sum(dy_ * sg * n, 0, keepdims=True)

    return pl.pallas_call(
        body, grid=(GDN_HEADS, t // GDN_ROWS),
        in_specs=[pl.BlockSpec((1, GDN_ROWS, GDN_DV), lambda hh, i: (hh, i, 0)),
                  pl.BlockSpec((GDN_ROWS, GDN_DV), lambda hh, i: (i, 3 * GDN_HEADS + hh)),
                  pl.BlockSpec((1, GDN_DV), lambda hh, i: (0, 0)),
                  pl.BlockSpec((GDN_ROWS, GDN_DV), lambda hh, i: (i, hh))],
        out_specs=[pl.BlockSpec((1, GDN_ROWS, GDN_DV), lambda hh, i: (hh, i, 0)),
                   pl.BlockSpec((GDN_ROWS, GDN_DV), lambda hh, i: (i, hh)),
                   pl.BlockSpec((1, GDN_DV), lambda hh, i: (0, 0))],
        out_shape=[jax.ShapeDtypeStruct((GDN_HEADS, t, GDN_DV), F32), jax.ShapeDtypeStruct((t, GDN_W), BF16),
                   jax.ShapeDtypeStruct((1, GDN_DV), F32)],
        name=name, compiler_params=_cp())(o, h, nw, dy)


def _tables(positions):
    pos = positions.astype(F32)[:, None]
    half = RET_DK // 2
    inv = jnp.power(RET_THETA, -jnp.arange(half, dtype=F32) * 2.0 / RET_DK)
    ang = pos * inv
    cos, sin = jnp.cos(ang), jnp.sin(ang)
    c2a = jnp.concatenate([cos, cos], 1)
    s2a = jnp.concatenate([-sin, sin], 1)
    hb = ROPE_DIMS // 2
    invb = jnp.power(ROPE_THETA, -jnp.arange(hb, dtype=F32) * 2.0 / ROPE_DIMS)
    angb = pos * invb
    cosb, sinb = jnp.cos(angb), jnp.sin(angb)
    t = pos.shape[0]
    ones = jnp.ones((t, DIL_HD - ROPE_DIMS), F32)
    zeros = jnp.zeros((t, DIL_HD - ROPE_DIMS), F32)
    z8 = jnp.zeros((t, hb), F32)
    cb = jnp.concatenate([cosb, cosb, ones] * 2, 1)
    shi = jnp.concatenate([z8, sinb, zeros] * 2, 1)
    slo = jnp.concatenate([-sinb, z8, zeros] * 2, 1)
    lg = jnp.log1p(-jnp.power(2.0, -5.0 - jnp.arange(RET_HEADS, dtype=F32)))
    lgt = jnp.broadcast_to(lg[:, None, None], (RET_HEADS, 1, LANES))
    delta = jnp.arange(ATT_BLK, dtype=jnp.int32)[:, None] + (SEQ - ATT_BLK) - jnp.arange(SEQ, dtype=jnp.int32)[None, :]
    cnt = jnp.zeros(delta.shape, F32)
    for (w, d) in DIL_PAIRS:
        cnt = cnt + ((delta >= 0) & (delta <= w) & (delta % d == 0)).astype(F32)
    strip = jnp.where(cnt > 0, jnp.log(jnp.maximum(cnt, 1.0)), NEG)
    return c2a, s2a, cb, shi, slo, lgt, strip


def _local_step(x, positions, target, get_w, put_g, small):
    c2a, s2a, cb, shi, slo, lgt, strip = _tables(positions)
    t = x.shape[0]
    saved = []
    xf = x
    xb = x.astype(BF16)
    for layer in range(DEPTH):
        j = layer // 2
        L = f"L{layer}_"
        W = get_w(layer)
        rec = {"x": xf, "xb": xb, "W": W}
        if layer % 2 == 0:
            h = _mm(xb, W["in_t"], tb=True, name=L + "ev_in")
            ro, ya = _ret_fwd(h, c2a, s2a, lgt, name=L + "ret_fwd")
            do_, yb, lse = _dil_fwd(h, cb, shi, slo, strip, name=L + "dil_fwd")
            y = jnp.concatenate([ya, yb], 1)
            mix = _mm(y, W["out"], name=L + "ev_out")
            rec.update(h=h, ro=ro, dil_o=do_, lse=lse, y=y)
        else:
            h = _mm(xb, W["in_t"], tb=True, name=L + "od_in")
            cw = W["conv"]
            q, k, v = _gdn_prep_fwd(h, cw, name=L + "gdn_prep")
            hs = h[:, 4 * GDN_W:4 * GDN_W + 2 * GDN_HEADS]
            bb = jnp.broadcast_to(hs[:, :GDN_HEADS].T[:, :, None], (GDN_HEADS, t, LANES))
            ab = jnp.broadcast_to(hs[:, GDN_HEADS:].T[:, :, None], (GDN_HEADS, t, LANES))
            alog = jnp.broadcast_to(small["od_a_log"][j][:, None, None], (GDN_HEADS, 1, LANES))
            dtb = jnp.broadcast_to(small["od_dt_bias"][j][:, None, None], (GDN_HEADS, 1, LANES))
            o, states = _gdn_core_fwd(q, k, v, bb, ab, alog, dtb, name=L + "gdn_fwd")
            nw = small["od_norm_w"][j][None, :]
            y = _gdn_post_fwd(o, h, nw, name=L + "gdn_post")
            mix = _mm(y, W["out"], name=L + "od_out")
            rec.update(h=h, q=q, k=k, v=v, bb=bb, ab=ab, alog=alog, dtb=dtb, states=states, o=o, y=y, nw=nw, cw=cw)
        z1, x1, x1b = _ln_fwd(xf, mix, small["ln1_g"][layer][None], small["ln1_b"][layer][None], name=L + "ln1")
        u = _mm(x1b, W["up_t"], tb=True, name=L + "ffn_up")
        fcw = W["fconv"]
        fcb = small["ffn_conv_b"][layer][None]
        a = _ffn_mid_fwd(u, fcw, fcb, name=L + "ffn_mid")
        f = _mm(a, W["down"], name=L + "ffn_down")
        z2, x2, x2b = _ln_fwd(x1, f, small["ln2_g"][layer][None], small["ln2_b"][layer][None], name=L + "ln2")
        rec.update(z1=z1, x1b=x1b, u=u, a=a, z2=z2, fcw=fcw, fcb=fcb)
        saved.append(rec)
        xf, xb = x2, x2b

    dy, lossv = _loss_head(xf, target, name="loss_head")
    loss = lossv[0, 0]

    gS = {n: [None] * small[n].shape[0] for n in small}
    dres, dmm = dy, None
    for layer in reversed(range(DEPTH)):
        j = layer // 2
        L = f"L{layer}_"
        rec = saved[layer]
        W = rec["W"]
        g = {}
        if dmm is None:
            dz2, dz2b, dg2, db2 = _ln_bwd(rec["z2"], small["ln2_g"][layer][None], dres, None, name=L + "ln2_bwd")
        else:
            dz2, dz2b, dg2, db2 = _ln_bwd(rec["z2"], small["ln2_g"][layer][None], dmm, dres, name=L + "ln2_bwd")
        gS["ln2_g"][layer], gS["ln2_b"][layer] = dg2[0], db2[0]
        g["down"] = _mm(rec["a"], dz2b, ta=True, name=L + "ffn_down_dw")
        da = _mm(dz2b, W["down"], tb=True, name=L + "ffn_down_dx")
        du, dcw, dcb = _ffn_mid_bwd(rec["u"], rec["fcw"], rec["fcb"], da, name=L + "ffn_mid_bwd")
        g["fconv"] = dcw
        gS["ffn_conv_b"][layer] = dcb[0]
        g["up_t"] = _mm(du, rec["x1b"], ta=True, name=L + "ffn_up_dw")
        dx1 = _mm(du, W["up_t"], name=L + "ffn_up_dx")
        dz1, dz1b, dg1, db1 = _ln_bwd(rec["z1"], small["ln1_g"][layer][None], dx1, dz2, name=L + "ln1_bwd")
        gS["ln1_g"][layer], gS["ln1_b"][layer] = dg1[0], db1[0]
        if layer % 2 == 0:
            g["out"] = _mm(rec["y"], dz1b, ta=True, name=L + "ev_out_dw")
            dyy = _mm(dz1b, W["out"], tb=True, name=L + "ev_out_dx")
            dqa, dka, dva, dga = _ret_bwd(rec["h"], c2a, s2a, lgt, rec["ro"], dyy, name=L + "ret_bwd")
            dqb, dkb, dvb = _dil_bwd(rec["h"], cb, shi, slo, strip, rec["dil_o"], rec["lse"], dyy, name=L + "dil_bwd")
            dh = jnp.concatenate([dqa, dka, dva, dga, dqb, dkb, dvb], 1)
            g["in_t"] = _mm(dh, rec["xb"], ta=True, name=L + "ev_in_dw")
            dxin = _mm(dh, W["in_t"], name=L + "ev_in_dx")
        else:
            g["out"] = _mm(rec["y"], dz1b, ta=True, name=L + "od_out_dw")
            dyy = _mm(dz1b, W["out"], tb=True, name=L + "od_out_dx")
            do, dgate, dnw = _gdn_post_bwd(rec["o"], rec["h"], rec["nw"], dyy, name=L + "gdn_post_bwd")
            gS["od_norm_w"][j] = dnw[0]
            dq, dk, dv, dbb, dab, dal, ddt = _gdn_core_bwd(
                rec["q"], rec["k"], rec["v"], rec["bb"], rec["ab"], rec["alog"], rec["dtb"], rec["states"], do,
                name=L + "gdn_bwd")
            gS["od_a_log"][j] = dal[:, 0, 0]
            gS["od_dt_bias"][j] = ddt[:, 0, 0]
            dhq, dhk, dhv, dwq, dwk, dwv = _gdn_prep_bwd(rec["h"], rec["cw"], dq, dk, dv, name=L + "gdn_prep_bwd")
            g["conv"] = jnp.concatenate([dwq, dwk, dwv], 1)
            dsm = jnp.concatenate([dbb[:, :, 0].T, dab[:, :, 0].T,
                                   jnp.zeros((t, LANES - 2 * GDN_HEADS), F32)], 1).astype(BF16)
            dh = jnp.concatenate([dhq, dhk, dhv, dgate, dsm], 1)
            g["in_t"] = _mm(dh, rec["xb"], ta=True, name=L + "od_in_dw")
            dxin = _mm(dh, W["in_t"], name=L + "od_in_dx")
        put_g(layer, g)
        dres, dmm = dz1, dxin
    grad_x = _axpy(dmm, dres, name="grad_x")
    gS = {n: jnp.stack(v) for n, v in gS.items()}
    return loss, grad_x, gS


HBM = pl.BlockSpec(memory_space=pltpu.HBM)


def _me():
    return lax.axis_index("x"), lax.axis_index("y"), lax.axis_index("c")


def _all_gather(shards, *, name):
    n = len(shards)

    def body(*refs):
        ins, outs = refs[:n], refs[n:2 * n]
        send_sems, recv_sems, local_sems = refs[2 * n:]
        x, y, c = _me()
        me, sibling = (x, y, c), (x, y, 1 - c)
        chips = [(1 - x, y), (x, 1 - y), (1 - x, 1 - y)]

        def slot(out, px, py, pc):
            return out.at[4 * px + 2 * py + pc]

        def copy(a, kk, block, to, src=None):
            return pltpu.make_async_remote_copy(
                src_ref=slot(outs[a], *block) if src is None else src, dst_ref=slot(outs[a], *block),
                send_sem=send_sems.at[a, kk], recv_sem=recv_sems.at[a, kk], device_id=to, device_id_type=MESH)

        mine = [pltpu.make_async_copy(ins[a], slot(outs[a], *me), local_sems.at[a]) for a in range(n)]
        for cp in mine:
            cp.start()
        first = []
        for a in range(n):
            first.append(copy(a, 0, me, sibling, src=ins[a]))
            first += [copy(a, 1 + jj, me, (*chip, c), src=ins[a]) for jj, chip in enumerate(chips)]
        for cp in first:
            cp.start()
        passed = []
        for jj, chip in enumerate(chips):
            for a in range(n):
                copy(a, 1 + jj, (*chip, c), me).wait_recv()
                cp = copy(a, 4 + jj, (*chip, c), sibling)
                cp.start()
                passed.append(cp)
        for a in range(n):
            copy(a, 0, sibling, me).wait_recv()
            for jj, chip in enumerate(chips):
                copy(a, 4 + jj, (*chip, 1 - c), me).wait_recv()
        for cp in first + passed:
            cp.wait_send()
        for cp in mine:
            cp.wait()

    return pl.pallas_call(
        body, in_specs=[HBM] * n, out_specs=[HBM] * n,
        out_shape=[jax.ShapeDtypeStruct((N_DEV, *s.shape), s.dtype) for s in shards],
        scratch_shapes=[pltpu.SemaphoreType.DMA((n, 7)), pltpu.SemaphoreType.DMA((n, 7)), pltpu.SemaphoreType.DMA((n,))],
        name=name, compiler_params=pltpu.CompilerParams(has_side_effects=True))(*shards)


def _sibling_exchange(gs, *, name):
    n = len(gs)

    def body(*refs):
        ins, outs = refs[:n], refs[n:2 * n]
        send_sems, recv_sems = refs[2 * n:]
        x, y, c = _me()
        cps = [pltpu.make_async_remote_copy(
            src_ref=ins[a].at[:, 1 - c], dst_ref=outs[a], send_sem=send_sems.at[a], recv_sem=recv_sems.at[a],
            device_id=(x, y, 1 - c), device_id_type=MESH) for a in range(n)]
        for cp in cps:
            cp.start()
        for cp in cps:
            cp.wait()

    return pl.pallas_call(
        body, in_specs=[HBM] * n, out_specs=[HBM] * n,
        out_shape=[jax.ShapeDtypeStruct((4, *g.shape[2:]), g.dtype) for g in gs],
        scratch_shapes=[pltpu.SemaphoreType.DMA((n,)), pltpu.SemaphoreType.DMA((n,))],
        name=name, compiler_params=pltpu.CompilerParams(has_side_effects=True))(*gs)


def _pair_add(g, r, cidx, *, name):
    _, _, rr, cc = g.shape
    tr = rr
    for cand in (512, 384, 256, 192, 176, 128, 64, 32, 16, 8):
        if rr % cand == 0:
            tr = cand
            break
    if rr < 8:
        tr = rr

    def body(c_ref, g_ref, r_ref, ob_ref, of_ref):
        s = g_ref[0, 0] + r_ref[0]
        ob_ref[0] = s.astype(BF16)
        of_ref[0] = s

    grid_spec = pltpu.PrefetchScalarGridSpec(
        num_scalar_prefetch=1, grid=(4, rr // tr),
        in_specs=[pl.BlockSpec((1, 1, tr, cc), lambda kk, i, c_ref: (kk, c_ref[0], i, 0)),
                  pl.BlockSpec((1, tr, cc), lambda kk, i, c_ref: (kk, i, 0))],
        out_specs=[pl.BlockSpec((1, tr, cc), lambda kk, i, c_ref: (kk, i, 0)),
                   pl.BlockSpec((1, tr, cc), lambda kk, i, c_ref: (kk, i, 0))])
    return pl.pallas_call(
        body, grid_spec=grid_spec,
        out_shape=[jax.ShapeDtypeStruct((4, rr, cc), BF16), jax.ShapeDtypeStruct((4, rr, cc), F32)],
        name=name, compiler_params=_cp())(cidx, g, r)


def _chip_exchange(ps, *, name):
    n = len(ps)

    def body(*refs):
        ins, outs = refs[:n], refs[n:2 * n]
        send_sems, recv_sems = refs[2 * n:]
        x, y, c = _me()
        chips = [(1 - x, y), (x, 1 - y), (1 - x, 1 - y)]
        cps = []
        for a in range(n):
            for jj, (px, py) in enumerate(chips):
                cps.append(pltpu.make_async_remote_copy(
                    src_ref=ins[a].at[2 * px + py], dst_ref=outs[a].at[jj],
                    send_sem=send_sems.at[a, jj], recv_sem=recv_sems.at[a, jj],
                    device_id=(px, py, c), device_id_type=MESH))
        for cp in cps:
            cp.start()
        for cp in cps:
            cp.wait()

    return pl.pallas_call(
        body, in_specs=[HBM] * n, out_specs=[HBM] * n,
        out_shape=[jax.ShapeDtypeStruct((3, *p.shape[1:]), p.dtype) for p in ps],
        scratch_shapes=[pltpu.SemaphoreType.DMA((n, 3)), pltpu.SemaphoreType.DMA((n, 3))],
        name=name, compiler_params=pltpu.CompilerParams(has_side_effects=True))(*ps)


def _row_tile(rr):
    for cand in (512, 384, 256, 192, 176, 128, 64, 32, 16, 8):
        if rr % cand == 0:
            return cand
    return rr


def _sum4(pf, recv, chip, *, name):
    _, rr, cc = pf.shape
    tr = _row_tile(rr)

    def body(c_ref, p_ref, r_ref, o_ref):
        o_ref[...] = ((p_ref[0] + r_ref[0].astype(F32)) + r_ref[1].astype(F32)) + r_ref[2].astype(F32)

    grid_spec = pltpu.PrefetchScalarGridSpec(
        num_scalar_prefetch=1, grid=(rr // tr,),
        in_specs=[pl.BlockSpec((1, tr, cc), lambda i, c_ref: (c_ref[0], i, 0)),
                  pl.BlockSpec((3, tr, cc), lambda i, c_ref: (0, i, 0))],
        out_specs=pl.BlockSpec((tr, cc), lambda i, c_ref: (i, 0)))
    return pl.pallas_call(body, grid_spec=grid_spec, out_shape=jax.ShapeDtypeStruct((rr, cc), F32),
                          name=name, compiler_params=_cp())(chip, pf, recv)


def _small_exchange(vec, *, name):
    rr = vec.shape[0]

    def body(v_ref, o_ref, send_sems, recv_sems):
        x, y, c = _me()
        myid = 4 * x + 2 * y + c
        o_ref[myid] = v_ref[...]
        cps = []
        for kk in range(1, N_DEV):
            px, py, pc = x ^ (kk >> 2), y ^ ((kk >> 1) & 1), c ^ (kk & 1)
            cps.append(pltpu.make_async_remote_copy(
                src_ref=v_ref, dst_ref=o_ref.at[myid], send_sem=send_sems.at[kk], recv_sem=recv_sems.at[kk],
                device_id=(px, py, pc), device_id_type=MESH))
        for cp in cps:
            cp.start()
        for kk in range(1, N_DEV):
            px, py, pc = x ^ (kk >> 2), y ^ ((kk >> 1) & 1), c ^ (kk & 1)
            pltpu.make_async_remote_copy(
                src_ref=v_ref, dst_ref=o_ref.at[4 * px + 2 * py + pc], send_sem=send_sems.at[kk],
                recv_sem=recv_sems.at[kk], device_id=(px, py, pc), device_id_type=MESH).wait_recv()
        for cp in cps:
            cp.wait_send()

    return pl.pallas_call(
        body, in_specs=[pl.BlockSpec(memory_space=pltpu.VMEM)], out_specs=pl.BlockSpec(memory_space=pltpu.VMEM),
        out_shape=jax.ShapeDtypeStruct((N_DEV, rr, LANES), F32),
        scratch_shapes=[pltpu.SemaphoreType.DMA((N_DEV,)), pltpu.SemaphoreType.DMA((N_DEV,))],
        name=name, compiler_params=pltpu.CompilerParams(has_side_effects=True))(vec)


def _adam_math(w, g, m, v):
    m = ADAM_B1 * m + (1.0 - ADAM_B1) * g
    v = ADAM_B2 * v + (1.0 - ADAM_B2) * (g * g)
    m_hat = m / (1.0 - ADAM_B1 ** ADAM_STEP)
    v_hat = v / (1.0 - ADAM_B2 ** ADAM_STEP)
    delta = -ADAM_LR * (m_hat / (jnp.sqrt(v_hat) + ADAM_EPS) + ADAM_WD * w)
    return delta, m, v


def _adamw_sharded(w, m, v, g, *, name):
    ll, rr, cc = w.shape
    tr = _row_tile(rr)

    def body(w_ref, m_ref, v_ref, g_ref, d_ref, nm_ref, nv_ref):
        d, nm, nv = _adam_math(w_ref[...], g_ref[...], m_ref[...], v_ref[...])
        d_ref[...] = d
        nm_ref[...] = nm
        nv_ref[...] = nv

    blk = pl.BlockSpec((1, tr, cc), lambda l, i: (l, i, 0))
    sh = jax.ShapeDtypeStruct((ll, rr, cc), F32)
    return pl.pallas_call(
        body, grid=(ll, rr // tr), in_specs=[blk] * 4, out_specs=[blk] * 3, out_shape=[sh] * 3,
        name=name, compiler_params=_cp())(w, m, v, g)


def _adamw_small(w, m, v, gall, *, name):
    rr = w.shape[0]

    def body(w_ref, m_ref, v_ref, g_ref, go_ref, d_ref, nm_ref, nv_ref):
        g = g_ref[0]
        for kk in range(1, N_DEV):
            g = g + g_ref[kk]
        d, nm, nv = _adam_math(w_ref[...], g, m_ref[...], v_ref[...])
        go_ref[...] = g
        d_ref[...] = d
        nm_ref[...] = nm
        nv_ref[...] = nv

    sh = jax.ShapeDtypeStruct((rr, LANES), F32)
    return pl.pallas_call(body, out_shape=[sh] * 4, name=name, compiler_params=_cp())(w, m, v, gall)


SHARDED = ("ev_w_in", "ev_w_out", "od_w_in", "od_conv_w", "od_w_out", "ffn_w_up", "ffn_conv_w", "ffn_w_down")
SMALL = ("od_a_log", "od_dt_bias", "od_norm_w", "ffn_conv_b", "ln1_g", "ln1_b", "ln2_g", "ln2_b")
ALL_W = ("ev_w_in", "ev_w_out", "od_w_in", "od_conv_w", "od_a_log", "od_dt_bias", "od_norm_w", "od_w_out",
         "ffn_w_up", "ffn_conv_w", "ffn_conv_b", "ffn_w_down", "ln1_g", "ln1_b", "ln2_g", "ln2_b")


def _layer_items(layer):
    j = layer // 2
    if layer % 2 == 0:
        mixer = [("in_t", "ev_w_in", j, "colT"), ("out", "ev_w_out", j, "row")]
    else:
        mixer = [("in_t", "od_w_in", j, "colT"), ("conv", "od_conv_w", j, "colsmall"), ("out", "od_w_out", j, "row")]
    return mixer + [("up_t", "ffn_w_up", layer, "colT"), ("fconv", "ffn_conv_w", layer, "colsmall"),
                    ("down", "ffn_w_down", layer, "row")]


def _to_send(kind, shard):
    if kind == "colT":
        return shard.T.astype(BF16)
    return shard.astype(BF16) if kind == "row" else shard


def _from_gather(kind, name, g):
    if kind == "colsmall":
        return jnp.transpose(g, (1, 0, 2)).reshape(g.shape[1], -1)
    full = g.reshape(-1, g.shape[-1])
    if name == "od_w_in":
        full = jnp.pad(full, ((0, OD_IN_PAD - OD_IN), (0, 0)))
    return full


def _by_owner(kind, name, gfull):
    if kind == "colsmall":
        kk, c8 = gfull.shape
        return jnp.transpose(gfull.reshape(kk, N_DEV, c8 // N_DEV), (1, 0, 2))
    if name == "od_w_in":
        gfull = gfull[:OD_IN]
    return gfull.reshape(N_DEV, gfull.shape[0] // N_DEV, gfull.shape[1])


def _pack_small(d):
    flat = jnp.concatenate([d[n].reshape(-1) for n in SMALL])
    pad = (-flat.shape[0]) % (8 * LANES)
    return jnp.pad(flat, (0, pad)).reshape(-1, LANES)


def _unpack_small(packed, like):
    flat = packed.reshape(-1)
    out, off = {}, 0
    for n in SMALL:
        sz = int(np.prod(like[n].shape))
        out[n] = flat[off:off + sz].reshape(like[n].shape)
        off += sz
    return out


def kernel(x, positions, ev_w_in, ev_w_out, od_w_in, od_conv_w, od_a_log, od_dt_bias, od_norm_w, od_w_out, ffn_w_up, ffn_conv_w, ffn_conv_b, ffn_w_down, ln1_g, ln1_b, ln2_g, ln2_b, loss_target, m_ev_w_in, m_ev_w_out, m_od_w_in, m_od_conv_w, m_od_a_log, m_od_dt_bias, m_od_norm_w, m_od_w_out, m_ffn_w_up, m_ffn_conv_w, m_ffn_conv_b, m_ffn_w_down, m_ln1_g, m_ln1_b, m_ln2_g, m_ln2_b, v_ev_w_in, v_ev_w_out, v_od_w_in, v_od_conv_w, v_od_a_log, v_od_dt_bias, v_od_norm_w, v_od_w_out, v_ffn_w_up, v_ffn_conv_w, v_ffn_conv_b, v_ffn_w_down, v_ln1_g, v_ln1_b, v_ln2_g, v_ln2_b):
    w = dict(ev_w_in=ev_w_in, ev_w_out=ev_w_out, od_w_in=od_w_in, od_conv_w=od_conv_w, od_a_log=od_a_log,
             od_dt_bias=od_dt_bias, od_norm_w=od_norm_w, od_w_out=od_w_out, ffn_w_up=ffn_w_up, ffn_conv_w=ffn_conv_w,
             ffn_conv_b=ffn_conv_b, ffn_w_down=ffn_w_down, ln1_g=ln1_g, ln1_b=ln1_b, ln2_g=ln2_g, ln2_b=ln2_b)
    mom = dict(ev_w_in=m_ev_w_in, ev_w_out=m_ev_w_out, od_w_in=m_od_w_in, od_conv_w=m_od_conv_w, od_a_log=m_od_a_log,
               od_dt_bias=m_od_dt_bias, od_norm_w=m_od_norm_w, od_w_out=m_od_w_out, ffn_w_up=m_ffn_w_up,
               ffn_conv_w=m_ffn_conv_w, ffn_conv_b=m_ffn_conv_b, ffn_w_down=m_ffn_w_down, ln1_g=m_ln1_g,
               ln1_b=m_ln1_b, ln2_g=m_ln2_g, ln2_b=m_ln2_b)
    var = dict(ev_w_in=v_ev_w_in, ev_w_out=v_ev_w_out, od_w_in=v_od_w_in, od_conv_w=v_od_conv_w, od_a_log=v_od_a_log,
               od_dt_bias=v_od_dt_bias, od_norm_w=v_od_norm_w, od_w_out=v_od_w_out, ffn_w_up=v_ffn_w_up,
               ffn_conv_w=v_ffn_conv_w, ffn_conv_b=v_ffn_conv_b, ffn_w_down=v_ffn_w_down, ln1_g=v_ln1_g,
               ln1_b=v_ln1_b, ln2_g=v_ln2_g, ln2_b=v_ln2_b)

    items = [(layer, *it) for layer in range(DEPTH) for it in _layer_items(layer)]
    shards = [_to_send(kind, w[n][j]) for (_, _, n, j, kind) in items]
    gathered = _all_gather(shards, name="weights_all_gather")
    full = [dict() for _ in range(DEPTH)]
    for (layer, key, n, j, kind), g in zip(items, gathered):
        full[layer][key] = _from_gather(kind, n, g)
    small = {n: w[n] for n in SMALL}

    grads = [None] * DEPTH

    def put_g(layer, g):
        grads[layer] = g

    loss, grad_x, gS = _local_step(x[0], positions[0], loss_target[0], lambda layer: full[layer], put_g, small)
    loss = lax.psum(loss, ("x", "y", "c"))

    cidx = lax.axis_index("c").astype(jnp.int32).reshape(1)
    chip = (2 * lax.axis_index("x") + lax.axis_index("y")).astype(jnp.int32).reshape(1)
    by_owner = [_by_owner(kind, n, grads[layer][key]) for (layer, key, n, j, kind) in items]
    by_owner = [g.reshape(4, 2, *g.shape[1:]) for g in by_owner]
    from_sibling = _sibling_exchange(by_owner, name="grads_sibling_exchange")
    sums = [_pair_add(g, r, cidx, name=f"L{it[0]}_{it[1]}_pair_add") for it, g, r in zip(items, by_owner, from_sibling)]
    from_chips = _chip_exchange([s[0] for s in sums], name="grads_chip_exchange")
    per_name = {n: [None] * w[n].shape[0] for n in SHARDED}
    for it, s, r in zip(items, sums, from_chips):
        layer, key, n, j, kind = it
        gsh = _sum4(s[1], r, chip, name=f"L{layer}_{key}_sum")
        per_name[n][j] = gsh.T if kind == "colT" else gsh

    outs_g, outs_d, outs_m, outs_v = {}, {}, {}, {}
    for n in SHARDED:
        g = jnp.stack(per_name[n])
        outs_g[n] = g
        outs_d[n], outs_m[n], outs_v[n] = _adamw_sharded(w[n], mom[n], var[n], g, name=f"adamw_{n}")

    gall = _small_exchange(_pack_small(gS), name="small_grads_exchange")
    g, d, nm, nv = _adamw_small(_pack_small({n: w[n] for n in SMALL}), _pack_small({n: mom[n] for n in SMALL}),
                                _pack_small({n: var[n] for n in SMALL}), gall, name="adamw_small")
    for dst, packed in ((outs_g, g), (outs_d, d), (outs_m, nm), (outs_v, nv)):
        dst.update(_unpack_small(packed, {n: w[n] for n in SMALL}))

    return (loss, grad_x[None], *[outs_g[n] for n in ALL_W], *[outs_d[n] for n in ALL_W],
            *[outs_m[n] for n in ALL_W], *[outs_v[n] for n in ALL_W])
```

```python
import functools
import math

import numpy as np
import jax
import jax.numpy as jnp
from jax import lax
from jax.experimental import pallas as pl
from jax.experimental.pallas import tpu as pltpu

F32 = jnp.float32
BF16 = jnp.bfloat16
MESH = pl.DeviceIdType.MESH

D_MODEL = 1024
SEQ = 2048
DEPTH = 4
N_DEV = 8
RET_HEADS, RET_DK, RET_DV = 4, 128, 256
RET_THETA = 10000.0
DIL_HEADS, DIL_HD = 8, 64
DIL_PAIRS = ((128, 1), (512, 4), (2048, 16))
ROPE_THETA = 500000.0
ROPE_DIMS = DIL_HD // 4
GDN_HEADS, GDN_DK, GDN_DV, GDN_CHUNK, GDN_CONV = 8, 128, 128, 64, 4
D_FF = 2816
FFN_CONV = 3
ALPHA = (2.0 * DEPTH) ** 0.25
EPS = 1e-5
RET_QK_W = RET_HEADS * RET_DK
RET_V_W = RET_HEADS * RET_DV
DIL_W = DIL_HEADS * DIL_HD
EV_IN = 2 * RET_QK_W + 2 * RET_V_W + 3 * DIL_W
EV_MIX = RET_V_W + DIL_W
GDN_W = GDN_HEADS * GDN_DK
OD_IN = 4 * GDN_W + 2 * GDN_HEADS
OD_IN_PAD = 4 * GDN_W + 128
ADAM_LR, ADAM_B1, ADAM_B2, ADAM_EPS, ADAM_WD, ADAM_STEP = 0.001, 0.9, 0.999, 1e-08, 0.01, 10

LANES = 128
VMEM_LIMIT = 56 * 1024 * 1024
ATT_BLK = 256
NEG = -1e30


def _cp(**kw):
    return pltpu.CompilerParams(vmem_limit_bytes=VMEM_LIMIT, **kw)


def _tile(n, cap):
    if n <= cap:
        return n
    best = None
    for t in range(LANES, cap + 1, LANES):
        if n % t == 0:
            best = t
    assert best is not None, (n, cap)
    return best


def _mm(a, b, *, ta=False, tb=False, name, out_dtype=F32, dep=None):
    m = a.shape[1] if ta else a.shape[0]
    k = a.shape[0] if ta else a.shape[1]
    n = b.shape[0] if tb else b.shape[1]
    assert (b.shape[1] if tb else b.shape[0]) == k
    assert a.dtype == BF16 and b.dtype == BF16
    tm = _tile(m, 1024 if k <= 2048 else 512)
    tn = _tile(n, 512)
    dims = (((0 if ta else 1,), (1 if tb else 0,)), ((), ()))

    def body(a_ref, b_ref, *rest):
        o_ref = rest[-1]
        o_ref[...] = lax.dot_general(a_ref[...], b_ref[...], dims,
                                     preferred_element_type=F32).astype(o_ref.dtype)

    a_spec = pl.BlockSpec((k, tm), lambda i, j: (0, i)) if ta else pl.BlockSpec((tm, k), lambda i, j: (i, 0))
    b_spec = pl.BlockSpec((tn, k), lambda i, j: (j, 0)) if tb else pl.BlockSpec((k, tn), lambda i, j: (0, j))
    extra = [] if dep is None else [dep]
    return pl.pallas_call(
        body, grid=(m // tm, n // tn), in_specs=[a_spec, b_spec] + [pl.BlockSpec(memory_space=pl.ANY)] * len(extra),
        out_specs=pl.BlockSpec((tm, tn), lambda i, j: (i, j)),
        out_shape=jax.ShapeDtypeStruct((m, n), out_dtype), name=name, compiler_params=_cp())(a, b, *extra)


LN_ROWS = 256


def _ln_fwd(x, m, g, b, *, name):
    t, d = x.shape

    def body(x_ref, m_ref, g_ref, b_ref, z_ref, y_ref, yb_ref):
        z = ALPHA * x_ref[...] + m_ref[...]
        mu = jnp.mean(z, -1, keepdims=True)
        zc = z - mu
        var = jnp.mean(zc * zc, -1, keepdims=True)
        y = zc * lax.rsqrt(var + EPS) * g_ref[...] + b_ref[...]
        z_ref[...] = z
        y_ref[...] = y
        yb_ref[...] = y.astype(BF16)

    row = pl.BlockSpec((LN_ROWS, d), lambda i: (i, 0))
    vec = pl.BlockSpec((1, d), lambda i: (0, 0))
    return pl.pallas_call(
        body, grid=(t // LN_ROWS,), in_specs=[row, row, vec, vec], out_specs=[row, row, row],
        out_shape=[jax.ShapeDtypeStruct((t, d), F32), jax.ShapeDtypeStruct((t, d), F32),
                   jax.ShapeDtypeStruct((t, d), BF16)],
        name=name, compiler_params=_cp())(x, m, g, b)


def _ln_bwd(z, g, dya, dyb, *, name):
    t, d = z.shape
    two = dyb is not None

    def body(*refs):
        if two:
            z_ref, g_ref, dya_ref, dyb_ref, dz_ref, dzb_ref, dg_ref, db_ref = refs
            dy = dya_ref[...] + ALPHA * dyb_ref[...]
        else:
            z_ref, g_ref, dya_ref, dz_ref, dzb_ref, dg_ref, db_ref = refs
            dy = dya_ref[...]
        zz = z_ref[...]
        mu = jnp.mean(zz, -1, keepdims=True)
        zc = zz - mu
        var = jnp.mean(zc * zc, -1, keepdims=True)
        r = lax.rsqrt(var + EPS)
        xh = zc * r
        dxh = dy * g_ref[...]
        dz = r * (dxh - jnp.mean(dxh, -1, keepdims=True) - xh * jnp.mean(dxh * xh, -1, keepdims=True))
        dz_ref[...] = dz
        dzb_ref[...] = dz.astype(BF16)

        @pl.when(pl.program_id(0) == 0)
        def _():
            dg_ref[...] = jnp.zeros_like(dg_ref)
            db_ref[...] = jnp.zeros_like(db_ref)

        dg_ref[...] += jnp.sum(dy * xh, 0, keepdims=True)
        db_ref[...] += jnp.sum(dy, 0, keepdims=True)

    row = pl.BlockSpec((LN_ROWS, d), lambda i: (i, 0))
    vec = pl.BlockSpec((1, d), lambda i: (0, 0))
    ins = [z, g, dya] + ([dyb] if two else [])
    return pl.pallas_call(
        body, grid=(t // LN_ROWS,), in_specs=[row, vec, row] + ([row] if two else []),
        out_specs=[row, row, vec, vec],
        out_shape=[jax.ShapeDtypeStruct((t, d), F32), jax.ShapeDtypeStruct((t, d), BF16),
                   jax.ShapeDtypeStruct((1, d), F32), jax.ShapeDtypeStruct((1, d), F32)],
        name=name, compiler_params=_cp())(*ins)


def _axpy(a, b, *, name):
    t, d = a.shape

    def body(a_ref, b_ref, o_ref):
        o_ref[...] = a_ref[...] + ALPHA * b_ref[...]

    row = pl.BlockSpec((LN_ROWS, d), lambda i: (i, 0))
    return pl.pallas_call(body, grid=(t // LN_ROWS,), in_specs=[row, row], out_specs=row,
                          out_shape=jax.ShapeDtypeStruct((t, d), F32), name=name, compiler_params=_cp())(a, b)


def _loss_head(y, target, *, name):
    t, d = y.shape

    def body(y_ref, t_ref, dy_ref, l_ref):
        e = y_ref[...] - t_ref[...]
        dy_ref[...] = e * (1.0 / d)

        @pl.when(pl.program_id(0) == 0)
        def _():
            l_ref[...] = jnp.zeros_like(l_ref)

        l_ref[...] += jnp.zeros_like(l_ref) + 0.5 * jnp.sum(jnp.mean(e * e, -1, keepdims=True), 0, keepdims=True)

    row = pl.BlockSpec((LN_ROWS, d), lambda i: (i, 0))
    return pl.pallas_call(
        body, grid=(t // LN_ROWS,), in_specs=[row, row],
        out_specs=[row, pl.BlockSpec((1, LANES), lambda i: (0, 0))],
        out_shape=[jax.ShapeDtypeStruct((t, d), F32), jax.ShapeDtypeStruct((1, LANES), F32)],
        name=name, compiler_params=_cp())(y, target)


def _sig(x):
    return 1.0 / (1.0 + jnp.exp(-x))


def _silu(x):
    return x * _sig(x)


def _dsilu(x):
    s = _sig(x)
    return s * (1.0 + x * (1.0 - s))


def _shift_down(u, k, row):
    if k == 0:
        return u
    return jnp.where(row >= k, pltpu.roll(u, k, 0), 0.0)


def _shift_up(u, k, row):
    if k == 0:
        return u
    t = u.shape[0]
    return jnp.where(row < t - k, pltpu.roll(u, t - k, 0), 0.0)


def _dwconv(u, w_ref, row):
    kk = w_ref.shape[0]
    acc = None
    for j in range(kk):
        term = w_ref[j:j + 1, :] * _shift_down(u, kk - 1 - j, row)
        acc = term if acc is None else acc + term
    return acc


def _dwconv_bwd(u, w_ref, dc, row, dw_ref):
    kk = w_ref.shape[0]
    du = None
    for j in range(kk):
        term = w_ref[j:j + 1, :] * _shift_up(dc, kk - 1 - j, row)
        du = term if du is None else du + term
        dw_ref[j:j + 1, :] = jnp.sum(dc * _shift_down(u, kk - 1 - j, row), 0, keepdims=True)
    return du


FFN_TC = 256


def _ffn_mid_fwd(u, cw, cb, *, name):
    t = u.shape[0]
    nb = D_FF // FFN_TC

    def body(ug_ref, uv_ref, wg_ref, wv_ref, bg_ref, bv_ref, a_ref):
        row = lax.broadcasted_iota(jnp.int32, (t, FFN_TC), 0)
        cg = _dwconv(ug_ref[...], wg_ref, row) + bg_ref[...]
        cv = _dwconv(uv_ref[...], wv_ref, row) + bv_ref[...]
        a_ref[...] = (_silu(cg) * cv).astype(BF16)

    col = lambda off: pl.BlockSpec((t, FFN_TC), lambda j: (0, j + off))
    wsp = lambda off: pl.BlockSpec((FFN_CONV, FFN_TC), lambda j: (0, j + off))
    bsp = lambda off: pl.BlockSpec((1, FFN_TC), lambda j: (0, j + off))
    return pl.pallas_call(
        body, grid=(nb,), in_specs=[col(0), col(nb), wsp(0), wsp(nb), bsp(0), bsp(nb)],
        out_specs=pl.BlockSpec((t, FFN_TC), lambda j: (0, j)),
        out_shape=jax.ShapeDtypeStruct((t, D_FF), BF16), name=name, compiler_params=_cp())(u, u, cw, cw, cb, cb)


def _ffn_mid_bwd(u, cw, cb, da, *, name):
    t = u.shape[0]
    nb = D_FF // FFN_TC

    def body(ug_ref, uv_ref, wg_ref, wv_ref, bg_ref, bv_ref, da_ref,
             dug_ref, duv_ref, dwg_ref, dwv_ref, dbg_ref, dbv_ref):
        row = lax.broadcasted_iota(jnp.int32, (t, FFN_TC), 0)
        ug, uv = ug_ref[...], uv_ref[...]
        cg = _dwconv(ug, wg_ref, row) + bg_ref[...]
        cv = _dwconv(uv, wv_ref, row) + bv_ref[...]
        da_ = da_ref[...]
        dcv = da_ * _silu(cg)
        dcg = da_ * cv * _dsilu(cg)
        dug_ref[...] = _dwconv_bwd(ug, wg_ref, dcg, row, dwg_ref).astype(BF16)
        duv_ref[...] = _dwconv_bwd(uv, wv_ref, dcv, row, dwv_ref).astype(BF16)
        dbg_ref[...] = jnp.sum(dcg, 0, keepdims=True)
        dbv_ref[...] = jnp.sum(dcv, 0, keepdims=True)

    col = lambda off: pl.BlockSpec((t, FFN_TC), lambda j: (0, j + off))
    wsp = lambda off: pl.BlockSpec((FFN_CONV, FFN_TC), lambda j: (0, j + off))
    bsp = lambda off: pl.BlockSpec((1, FFN_TC), lambda j: (0, j + off))
    outs = pl.pallas_call(
        body, grid=(nb,), in_specs=[col(0), col(nb), wsp(0), wsp(nb), bsp(0), bsp(nb), col(0)],
        out_specs=[col(0), col(0), wsp(0), wsp(0), bsp(0), bsp(0)],
        out_shape=[jax.ShapeDtypeStruct((t, D_FF), BF16), jax.ShapeDtypeStruct((t, D_FF), BF16),
                   jax.ShapeDtypeStruct((FFN_CONV, D_FF), F32), jax.ShapeDtypeStruct((FFN_CONV, D_FF), F32),
                   jax.ShapeDtypeStruct((1, D_FF), F32), jax.ShapeDtypeStruct((1, D_FF), F32)],
        name=name, compiler_params=_cp())(u, u, cw, cw, cb, cb, da)
    dug, duv, dwg, dwv, dbg, dbv = outs
    return (jnp.concatenate([dug, duv], 1), jnp.concatenate([dwg, dwv], 1), jnp.concatenate([dbg, dbv], 1))


def _rot_a(x, c2, s2):
    return x * c2 + pltpu.roll(x, RET_DK // 2, 1) * s2


def _rot_a_t(dy, c2, s2):
    return dy * c2 + pltpu.roll(dy * s2, RET_DK // 2, 1)


def _decay_tile(lg, blk_diff):
    r = lax.broadcasted_iota(jnp.int32, (ATT_BLK, ATT_BLK), 0)
    c = lax.broadcasted_iota(jnp.int32, (ATT_BLK, ATT_BLK), 1)
    rel = r - c + blk_diff * ATT_BLK
    return jnp.where(rel >= 0, jnp.exp(jnp.maximum(rel, 0).astype(F32) * lg), 0.0)


def _nt(a, b):
    return lax.dot_general(a, b, (((1,), (1,)), ((), ())), preferred_element_type=F32)


def _nn(a, b):
    return lax.dot_general(a, b, (((1,), (0,)), ((), ())), preferred_element_type=F32)


def _tn(a, b):
    return lax.dot_general(a, b, (((0,), (0,)), ((), ())), preferred_element_type=F32)


def _ret_specs(t):
    q = pl.BlockSpec((t, RET_DK), lambda h: (0, h))
    k = pl.BlockSpec((t, RET_DK), lambda h: (0, RET_HEADS + h))
    v = pl.BlockSpec((t, RET_DV), lambda h: (0, RET_HEADS + h))
    g = pl.BlockSpec((t, RET_DV), lambda h: (0, 2 * RET_HEADS + h))
    tab = pl.BlockSpec((t, RET_DK), lambda h: (0, 0))
    lg = pl.BlockSpec((1, 1, LANES), lambda h: (h, 0, 0))
    return q, k, v, g, tab, lg


def _ret_fwd(h, c2, s2, lgt, *, name):
    t = h.shape[0]
    nblk = t // ATT_BLK
    scale = RET_DK ** -0.5

    def body(q_ref, k_ref, v_ref, g_ref, c_ref, s_ref, lg_ref, o_ref, ya_ref, qs, ks, vs):
        c2_, s2_ = c_ref[...], s_ref[...]
        qs[...] = _rot_a(q_ref[...], c2_, s2_).astype(BF16)
        ks[...] = (_rot_a(k_ref[...], c2_, s2_) * scale).astype(BF16)
        vs[...] = v_ref[...].astype(BF16)
        lg = lg_ref[0, :, 0:1]
        for i in range(nblk):
            qi = qs[pl.ds(i * ATT_BLK, ATT_BLK), :]
            acc = jnp.zeros((ATT_BLK, RET_DV), F32)
            for j in range(i + 1):
                sl = pl.ds(j * ATT_BLK, ATT_BLK)
                s = _nt(qi, ks[sl, :]) * _decay_tile(lg, i - j)
                acc = acc + _nn(s.astype(BF16), vs[sl, :])
            rows = pl.ds(i * ATT_BLK, ATT_BLK)
            o_ref[rows, :] = acc
            r = lax.rsqrt(jnp.mean(acc * acc, -1, keepdims=True) + EPS)
            ya_ref[rows, :] = (acc * r * _silu(g_ref[rows, :])).astype(BF16)

    q, k, v, g, tab, lg = _ret_specs(t)
    out = pl.BlockSpec((t, RET_DV), lambda hh: (0, hh))
    return pl.pallas_call(
        body, grid=(RET_HEADS,), in_specs=[q, k, v, g, tab, tab, lg], out_specs=[out, out],
        out_shape=[jax.ShapeDtypeStruct((t, RET_V_W), F32), jax.ShapeDtypeStruct((t, RET_V_W), BF16)],
        scratch_shapes=[pltpu.VMEM((t, RET_DK), BF16), pltpu.VMEM((t, RET_DK), BF16), pltpu.VMEM((t, RET_DV), BF16)],
        name=name, compiler_params=_cp())(h, h, h, h, c2, s2, lgt)


def _ret_bwd(h, c2, s2, lgt, o, dy, *, name):
    t = h.shape[0]
    nblk = t // ATT_BLK
    scale = RET_DK ** -0.5

    def body(q_ref, k_ref, v_ref, g_ref, c_ref, s_ref, lg_ref, o_ref, dy_ref,
             dq_ref, dk_ref, dv_ref, dg_ref, qs, ks, vs, dos, dka, dva):
        c2_, s2_ = c_ref[...], s_ref[...]
        qs[...] = _rot_a(q_ref[...], c2_, s2_).astype(BF16)
        ks[...] = (_rot_a(k_ref[...], c2_, s2_) * scale).astype(BF16)
        vs[...] = v_ref[...].astype(BF16)
        lg = lg_ref[0, :, 0:1]
        oo = o_ref[...]
        gg = g_ref[...]
        dya = dy_ref[...]
        r = lax.rsqrt(jnp.mean(oo * oo, -1, keepdims=True) + EPS)
        rn = oo * r
        dg_ref[...] = (dya * rn * _dsilu(gg)).astype(BF16)
        drn = dya * _silu(gg)
        dos[...] = (r * (drn - rn * jnp.mean(drn * rn, -1, keepdims=True))).astype(BF16)
        dka[...] = jnp.zeros_like(dka)
        dva[...] = jnp.zeros_like(dva)
        for i in range(nblk):
            rows = pl.ds(i * ATT_BLK, ATT_BLK)
            qi = qs[rows, :]
            doi = dos[rows, :]
            dqa = jnp.zeros((ATT_BLK, RET_DK), F32)
            for j in range(i + 1):
                sl = pl.ds(j * ATT_BLK, ATT_BLK)
                dt_ = _decay_tile(lg, i - j)
                kj = ks[sl, :]
                s = (_nt(qi, kj) * dt_).astype(BF16)
                ds = (_nt(doi, vs[sl, :]) * dt_).astype(BF16)
                dqa = dqa + _nn(ds, kj)
                dka[sl, :] += _tn(ds, qi)
                dva[sl, :] += _tn(s, doi)
            dq_ref[rows, :] = _rot_a_t(dqa, c_ref[rows, :], s_ref[rows, :]).astype(BF16)
        dk_ref[...] = (_rot_a_t(dka[...], c2_, s2_) * scale).astype(BF16)
        dv_ref[...] = dva[...].astype(BF16)

    q, k, v, g, tab, lg = _ret_specs(t)
    blk_v = pl.BlockSpec((t, RET_DV), lambda hh: (0, hh))
    blk_k = pl.BlockSpec((t, RET_DK), lambda hh: (0, hh))
    return pl.pallas_call(
        body, grid=(RET_HEADS,), in_specs=[q, k, v, g, tab, tab, lg, blk_v, blk_v],
        out_specs=[blk_k, blk_k, blk_v, blk_v],
        out_shape=[jax.ShapeDtypeStruct((t, RET_QK_W), BF16), jax.ShapeDtypeStruct((t, RET_QK_W), BF16),
                   jax.ShapeDtypeStruct((t, RET_V_W), BF16), jax.ShapeDtypeStruct((t, RET_V_W), BF16)],
        scratch_shapes=[pltpu.VMEM((t, RET_DK), BF16), pltpu.VMEM((t, RET_DK), BF16), pltpu.VMEM((t, RET_DV), BF16),
                        pltpu.VMEM((t, RET_DV), BF16), pltpu.VMEM((t, RET_DK), F32), pltpu.VMEM((t, RET_DV), F32)],
        name=name, compiler_params=_cp())(h, h, h, h, c2, s2, lgt, o, dy)


def _rot_b(x, cb, shi, slo):
    return x * cb + pltpu.roll(x, ROPE_DIMS // 2, 1) * shi + pltpu.roll(x, LANES - ROPE_DIMS // 2, 1) * slo


def _rot_b_t(dy, cb, shi, slo):
    return dy * cb + pltpu.roll(dy * shi, LANES - ROPE_DIMS // 2, 1) + pltpu.roll(dy * slo, ROPE_DIMS // 2, 1)


def _dil_specs(t):
    base = (2 * RET_QK_W + 2 * RET_V_W) // LANES
    npair = DIL_W // LANES
    q = pl.BlockSpec((t, LANES), lambda p: (0, base + p))
    k = pl.BlockSpec((t, LANES), lambda p: (0, base + npair + p))
    v = pl.BlockSpec((t, LANES), lambda p: (0, base + 2 * npair + p))
    tab = pl.BlockSpec((t, LANES), lambda p: (0, 0))
    strip = pl.BlockSpec((ATT_BLK, t), lambda p: (0, 0))
    pair = pl.BlockSpec((t, LANES), lambda p: (0, p))
    return q, k, v, tab, strip, pair


def _dil_fwd(h, cb, shi, slo, strip, *, name):
    t = h.shape[0]
    nblk = t // ATT_BLK
    scale = DIL_HD ** -0.5

    def body(q_ref, k_ref, v_ref, cb_ref, shi_ref, slo_ref, st_ref, o_ref, yb_ref, lse_ref, qs, ks, vs):
        cb_, shi_, slo_ = cb_ref[...], shi_ref[...], slo_ref[...]
        lane = lax.broadcasted_iota(jnp.int32, (t, LANES), 1)
        qr = _rot_b(q_ref[...], cb_, shi_, slo_) * scale
        qs[0] = jnp.where(lane < DIL_HD, qr, 0.0).astype(BF16)
        qs[1] = jnp.where(lane >= DIL_HD, qr, 0.0).astype(BF16)
        ks[...] = _rot_b(k_ref[...], cb_, shi_, slo_).astype(BF16)
        vs[...] = v_ref[...].astype(BF16)
        lane_b = lax.broadcasted_iota(jnp.int32, (ATT_BLK, LANES), 1)
        for i in range(nblk):
            w = (i + 1) * ATT_BLK
            rows = pl.ds(i * ATT_BLK, ATT_BLK)
            logc = st_ref[:, t - w:t]
            outs, lses = [], []
            for hd in range(2):
                s = _nt(qs[hd, rows, :], ks[0:w, :]) + logc
                m = jnp.max(s, -1, keepdims=True)
                p = jnp.exp(s - m)
                l = jnp.sum(p, -1, keepdims=True)
                outs.append(_nn(p.astype(BF16), vs[0:w, :]) / l)
                lses.append(m + jnp.log(l))
            o = jnp.where(lane_b < DIL_HD, outs[0], outs[1])
            o_ref[rows, :] = o
            yb_ref[rows, :] = o.astype(BF16)
            lse_ref[rows, :] = jnp.where(lane_b < DIL_HD, lses[0], lses[1])

    q, k, v, tab, strip_spec, pair = _dil_specs(t)
    return pl.pallas_call(
        body, grid=(DIL_W // LANES,), in_specs=[q, k, v, tab, tab, tab, strip_spec], out_specs=[pair, pair, pair],
        out_shape=[jax.ShapeDtypeStruct((t, DIL_W), F32), jax.ShapeDtypeStruct((t, DIL_W), BF16),
                   jax.ShapeDtypeStruct((t, DIL_W), F32)],
        scratch_shapes=[pltpu.VMEM((2, t, LANES), BF16), pltpu.VMEM((t, LANES), BF16), pltpu.VMEM((t, LANES), BF16)],
        name=name, compiler_params=_cp())(h, h, h, cb, shi, slo, strip)


def _dil_bwd(h, cb, shi, slo, strip, o, lse, dy, *, name):
    t = h.shape[0]
    nblk = t // ATT_BLK
    scale = DIL_HD ** -0.5

    def body(q_ref, k_ref, v_ref, cb_ref, shi_ref, slo_ref, st_ref, o_ref, lse_ref, dy_ref,
             dq_ref, dk_ref, dv_ref, qs, ks, vs, dos, dls, dka, dva):
        cb_, shi_, slo_ = cb_ref[...], shi_ref[...], slo_ref[...]
        lane = lax.broadcasted_iota(jnp.int32, (t, LANES), 1)
        qr = _rot_b(q_ref[...], cb_, shi_, slo_) * scale
        qs[0] = jnp.where(lane < DIL_HD, qr, 0.0).astype(BF16)
        qs[1] = jnp.where(lane >= DIL_HD, qr, 0.0).astype(BF16)
        ks[...] = _rot_b(k_ref[...], cb_, shi_, slo_).astype(BF16)
        vs[...] = v_ref[...].astype(BF16)
        do = dy_ref[...]
        prod = do * o_ref[...]
        d0 = jnp.sum(jnp.where(lane < DIL_HD, prod, 0.0), -1, keepdims=True)
        d1 = jnp.sum(jnp.where(lane >= DIL_HD, prod, 0.0), -1, keepdims=True)
        dls[...] = jnp.where(lane < DIL_HD, d0, d1)
        dos[0] = jnp.where(lane < DIL_HD, do, 0.0).astype(BF16)
        dos[1] = jnp.where(lane >= DIL_HD, do, 0.0).astype(BF16)
        dka[...] = jnp.zeros_like(dka)
        dva[...] = jnp.zeros_like(dva)
        lane_b = lax.broadcasted_iota(jnp.int32, (ATT_BLK, LANES), 1)
        for i in range(nblk):
            w = (i + 1) * ATT_BLK
            rows = pl.ds(i * ATT_BLK, ATT_BLK)
            logc = st_ref[:, t - w:t]
            dqs = []
            for hd in range(2):
                col = hd * DIL_HD
                qh = qs[hd, rows, :]
                doh = dos[hd, rows, :]
                lse_h = lse_ref[rows, col:col + 1]
                dl_h = dls[rows, col:col + 1]
                p = jnp.exp(_nt(qh, ks[0:w, :]) + logc - lse_h)
                dp = _nt(doh, vs[0:w, :])
                ds = (p * (dp - dl_h)).astype(BF16)
                dqs.append(_nn(ds, ks[0:w, :]))
                dka[0:w, :] += _tn(ds, qh)
                dva[0:w, :] += _tn(p.astype(BF16), doh)
            dq = jnp.where(lane_b < DIL_HD, dqs[0], dqs[1]) * scale
            dq_ref[rows, :] = _rot_b_t(dq, cb_ref[rows, :], shi_ref[rows, :], slo_ref[rows, :]).astype(BF16)
        dk_ref[...] = _rot_b_t(dka[...], cb_, shi_, slo_).astype(BF16)
        dv_ref[...] = dva[...].astype(BF16)

    q, k, v, tab, strip_spec, pair = _dil_specs(t)
    dy_spec = pl.BlockSpec((t, LANES), lambda p: (0, RET_V_W // LANES + p))
    return pl.pallas_call(
        body, grid=(DIL_W // LANES,), in_specs=[q, k, v, tab, tab, tab, strip_spec, pair, pair, dy_spec],
        out_specs=[pair, pair, pair],
        out_shape=[jax.ShapeDtypeStruct((t, DIL_W), BF16)] * 3,
        scratch_shapes=[pltpu.VMEM((2, t, LANES), BF16), pltpu.VMEM((t, LANES), BF16), pltpu.VMEM((t, LANES), BF16),
                        pltpu.VMEM((2, t, LANES), BF16), pltpu.VMEM((t, LANES), F32),
                        pltpu.VMEM((t, LANES), F32), pltpu.VMEM((t, LANES), F32)],
        name=name, compiler_params=_cp())(h, h, h, cb, shi, slo, strip, o, lse, dy)


def _gdn_prep_fwd(h, cw, *, name):
    t = h.shape[0]
    qscale = GDN_DK ** -0.5

    def body(hq_ref, hk_ref, hv_ref, wq_ref, wk_ref, wv_ref, q_ref, k_ref, v_ref):
        row = lax.broadcasted_iota(jnp.int32, (t, GDN_DK), 0)
        sq = _silu(_dwconv(hq_ref[...], wq_ref, row))
        sk = _silu(_dwconv(hk_ref[...], wk_ref, row))
        q_ref[0] = sq * lax.rsqrt(jnp.sum(sq * sq, -1, keepdims=True) + 1e-6) * qscale
        k_ref[0] = sk * lax.rsqrt(jnp.sum(sk * sk, -1, keepdims=True) + 1e-6)
        v_ref[0] = _silu(_dwconv(hv_ref[...], wv_ref, row))

    hs = lambda off: pl.BlockSpec((t, GDN_DK), lambda i: (0, i + off))
    ws = lambda off: pl.BlockSpec((GDN_CONV, GDN_DK), lambda i: (0, i + off))
    out = pl.BlockSpec((1, t, GDN_DK), lambda i: (i, 0, 0))
    return pl.pallas_call(
        body, grid=(GDN_HEADS,), in_specs=[hs(0), hs(8), hs(16), ws(0), ws(8), ws(16)], out_specs=[out, out, out],
        out_shape=[jax.ShapeDtypeStruct((GDN_HEADS, t, GDN_DK), F32)] * 3,
        name=name, compiler_params=_cp())(h, h, h, cw, cw, cw)


def _gdn_prep_bwd(h, cw, dq, dk, dv, *, name):
    t = h.shape[0]
    qscale = GDN_DK ** -0.5

    def body(hq_ref, hk_ref, hv_ref, wq_ref, wk_ref, wv_ref, dq_ref, dk_ref, dv_ref,
             dhq_ref, dhk_ref, dhv_ref, dwq_ref, dwk_ref, dwv_ref):
        row = lax.broadcasted_iota(jnp.int32, (t, GDN_DK), 0)

        def one(h_ref, w_ref, d_ref, dh_ref, dw_ref, norm, sc):
            u = h_ref[...]
            c = _dwconv(u, w_ref, row)
            d = d_ref[0]
            if norm:
                s = _silu(c)
                r = lax.rsqrt(jnp.sum(s * s, -1, keepdims=True) + 1e-6)
                n = s * r
                d = d * sc
                d = r * (d - n * jnp.sum(d * n, -1, keepdims=True))
            dc = d * _dsilu(c)
            dh_ref[...] = _dwconv_bwd(u, w_ref, dc, row, dw_ref).astype(BF16)

        one(hq_ref, wq_ref, dq_ref, dhq_ref, dwq_ref, True, qscale)
        one(hk_ref, wk_ref, dk_ref, dhk_ref, dwk_ref, True, 1.0)
        one(hv_ref, wv_ref, dv_ref, dhv_ref, dwv_ref, False, 1.0)

    hs = lambda off: pl.BlockSpec((t, GDN_DK), lambda i: (0, i + off))
    ws = lambda off: pl.BlockSpec((GDN_CONV, GDN_DK), lambda i: (0, i + off))
    hd = pl.BlockSpec((1, t, GDN_DK), lambda i: (i, 0, 0))
    return pl.pallas_call(
        body, grid=(GDN_HEADS,), in_specs=[hs(0), hs(8), hs(16), ws(0), ws(8), ws(16), hd, hd, hd],
        out_specs=[hs(0), hs(0), hs(0), ws(0), ws(0), ws(0)],
        out_shape=[jax.ShapeDtypeStruct((t, GDN_W), BF16)] * 3 + [jax.ShapeDtypeStruct((GDN_CONV, GDN_W), F32)] * 3,
        name=name, compiler_params=_cp())(h, h, h, cw, cw, cw, dq, dk, dv)


def _make_mm2(hi):
    def prep(x):
        return x if hi else x.astype(BF16)
    prec = lax.Precision.HIGHEST if hi else None

    def raw(a, b, dims):
        return lax.dot_general(prep(a), prep(b), (dims, ((), ())), precision=prec, preferred_element_type=F32)

    @jax.custom_vjp
    def nn(a, b):
        return raw(a, b, ((1,), (0,)))

    @jax.custom_vjp
    def nt(a, b):
        return raw(a, b, ((1,), (1,)))

    @jax.custom_vjp
    def tn(a, b):
        return raw(a, b, ((0,), (0,)))

    nn.defvjp(lambda a, b: (nn(a, b), (a, b)), lambda r, g: (nt(g, r[1]), tn(r[0], g)))
    nt.defvjp(lambda a, b: (nt(a, b), (a, b)), lambda r, g: (nn(g, r[1]), tn(g, r[0])))
    tn.defvjp(lambda a, b: (tn(a, b), (a, b)), lambda r, g: (nt(r[1], g), nn(r[0], g)))
    return nn, nt, tn


_NN, _NT, _TN = _make_mm2(False)
_NNH, _NTH, _TNH = _make_mm2(True)


@jax.custom_vjp
def _inv_unit_lower(l):
    c = l.shape[0]
    eye = (lax.broadcasted_iota(jnp.int32, (c, c), 0) == lax.broadcasted_iota(jnp.int32, (c, c), 1)).astype(F32)
    p = -l
    t = eye + p
    for _ in range(int(math.log2(c)) - 1):
        p = _NNH(p, p)
        t = t + _NNH(t, p)
    return t


def _inv_fwd(l):
    t = _inv_unit_lower(l)
    return t, t


def _inv_bwd(t, dt):
    return (-_NTH(_TNH(t, dt), t),)


_inv_unit_lower.defvjp(_inv_fwd, _inv_bwd)


def _softplus(x):
    return jnp.maximum(x, 0.0) + jnp.log1p(jnp.exp(-jnp.abs(x)))


def _gdn_chunk(q, k, v, braw, araw, alog, dtb, state):
    c = q.shape[0]
    ri = lax.broadcasted_iota(jnp.int32, (c, c), 0)
    ci = lax.broadcasted_iota(jnp.int32, (c, c), 1)
    tri = ri >= ci
    strict = ri > ci
    eye = (ri == ci).astype(F32)
    beta = _sig(braw)
    g = -jnp.exp(alog) * _softplus(araw + dtb)
    gcm = _NNH(tri.astype(F32), g * jnp.ones((c, c), F32))
    gct = _NTH(eye, gcm)
    decay = jnp.where(tri, jnp.exp(jnp.where(tri, gcm - gct, 0.0)), 0.0)
    gc = jnp.sum(gcm, 1, keepdims=True) * (1.0 / c)
    glast = jnp.sum(g, 0, keepdims=True)
    egc = jnp.exp(gc)
    kb = k * beta
    tm = _inv_unit_lower(jnp.where(strict, _NT(kb, k) * decay, 0.0))
    u = _NNH(tm, v * beta)
    w = _NNH(tm, kb * egc)
    attn = jnp.where(tri, _NT(q, k) * decay, 0.0)
    k_dec = k * jnp.exp(glast - gc)
    q_dec = q * egc
    v_new = u - _NN(w, state)
    o = _NN(q_dec, state) + _NN(attn, v_new)
    new_state = state * jnp.exp(glast) + _TN(k_dec, v_new)
    return o, new_state


def _gdn_specs(t, rev):
    nch = t // GDN_CHUNK
    cm = (lambda n: nch - 1 - n) if rev else (lambda n: n)
    tok = pl.BlockSpec((GDN_HEADS, GDN_CHUNK, GDN_DK), lambda n: (0, cm(n), 0))
    par = pl.BlockSpec((GDN_HEADS, 1, LANES), lambda n: (0, 0, 0))
    st = pl.BlockSpec((GDN_HEADS, 1, GDN_DK, GDN_DV), lambda n: (0, cm(n), 0, 0))
    return tok, par, st


def _gdn_core_fwd(q, k, v, bb, ab, alog, dtb, *, name):
    t = q.shape[1]
    nch = t // GDN_CHUNK

    def body(q_ref, k_ref, v_ref, bb_ref, ab_ref, al_ref, dt_ref, o_ref, st_ref, state):
        @pl.when(pl.program_id(0) == 0)
        def _():
            state[...] = jnp.zeros_like(state)

        s0 = state[...]
        st_ref[:, 0] = s0
        o, s1 = jax.vmap(_gdn_chunk)(q_ref[...], k_ref[...], v_ref[...], bb_ref[:, :, 0:1], ab_ref[:, :, 0:1],
                                     al_ref[:, :, 0:1], dt_ref[:, :, 0:1], s0)
        o_ref[...] = o
        state[...] = s1

    tok, par, st = _gdn_specs(t, False)
    return pl.pallas_call(
        body, grid=(nch,), in_specs=[tok, tok, tok, tok, tok, par, par], out_specs=[tok, st],
        out_shape=[jax.ShapeDtypeStruct((GDN_HEADS, t, GDN_DV), F32),
                   jax.ShapeDtypeStruct((GDN_HEADS, nch, GDN_DK, GDN_DV), F32)],
        scratch_shapes=[pltpu.VMEM((GDN_HEADS, GDN_DK, GDN_DV), F32)],
        name=name, compiler_params=_cp())(q, k, v, bb, ab, alog, dtb)


def _gdn_core_bwd(q, k, v, bb, ab, alog, dtb, states, do, *, name):
    t = q.shape[1]
    nch = t // GDN_CHUNK

    def body(q_ref, k_ref, v_ref, bb_ref, ab_ref, al_ref, dt_ref, st_ref, do_ref,
             dq_ref, dk_ref, dv_ref, dbb_ref, dab_ref, dal_ref, ddt_ref, dstate):
        @pl.when(pl.program_id(0) == 0)
        def _():
            dstate[...] = jnp.zeros_like(dstate)
            dal_ref[...] = jnp.zeros_like(dal_ref)
            ddt_ref[...] = jnp.zeros_like(ddt_ref)

        args = (q_ref[...], k_ref[...], v_ref[...], bb_ref[:, :, 0:1], ab_ref[:, :, 0:1],
                al_ref[:, :, 0:1], dt_ref[:, :, 0:1], st_ref[:, 0])
        _, pull = jax.vjp(jax.vmap(_gdn_chunk), *args)
        dq, dk, dv, dbr, dar, dal, ddt, ds = pull((do_ref[...], dstate[...]))
        dq_ref[...] = dq
        dk_ref[...] = dk
        dv_ref[...] = dv
        dbb_ref[...] = dbr + jnp.zeros((GDN_HEADS, GDN_CHUNK, LANES), F32)
        dab_ref[...] = dar + jnp.zeros((GDN_HEADS, GDN_CHUNK, LANES), F32)
        dal_ref[...] += dal + jnp.zeros((GDN_HEADS, 1, LANES), F32)
        ddt_ref[...] += ddt + jnp.zeros((GDN_HEADS, 1, LANES), F32)
        dstate[...] = ds

    tok, par, st = _gdn_specs(t, True)
    tokshape = jax.ShapeDtypeStruct((GDN_HEADS, t, GDN_DK), F32)
    parshape = jax.ShapeDtypeStruct((GDN_HEADS, 1, LANES), F32)
    return pl.pallas_call(
        body, grid=(nch,), in_specs=[tok, tok, tok, tok, tok, par, par, st, tok],
        out_specs=[tok, tok, tok, tok, tok, par, par],
        out_shape=[tokshape] * 5 + [parshape] * 2,
        scratch_shapes=[pltpu.VMEM((GDN_HEADS, GDN_DK, GDN_DV), F32)],
        name=name, compiler_params=_cp())(q, k, v, bb, ab, alog, dtb, states, do)


GDN_ROWS = 512


def _gdn_post_fwd(o, h, nw, *, name):
    t = o.shape[1]

    def body(o_ref, g_ref, nw_ref, y_ref):
        oo = o_ref[0]
        r = lax.rsqrt(jnp.mean(oo * oo, -1, keepdims=True) + EPS)
        y_ref[...] = (oo * r * nw_ref[...] * _silu(g_ref[...])).astype(BF16)

    return pl.pallas_call(
        body, grid=(GDN_HEADS, t // GDN_ROWS),
        in_specs=[pl.BlockSpec((1, GDN_ROWS, GDN_DV), lambda hh, i: (hh, i, 0)),
                  pl.BlockSpec((GDN_ROWS, GDN_DV), lambda hh, i: (i, 3 * GDN_HEADS + hh)),
                  pl.BlockSpec((1, GDN_DV), lambda hh, i: (0, 0))],
        out_specs=pl.BlockSpec((GDN_ROWS, GDN_DV), lambda hh, i: (i, hh)),
        out_shape=jax.ShapeDtypeStruct((t, GDN_W), BF16), name=name, compiler_params=_cp())(o, h, nw)


def _gdn_post_bwd(o, h, nw, dy, *, name):
    t = o.shape[1]

    def body(o_ref, g_ref, nw_ref, dy_ref, do_ref, dg_ref, dnw_ref):
        oo, gg, nw_, dy_ = o_ref[0], g_ref[...], nw_ref[...], dy_ref[...]
        r = lax.rsqrt(jnp.mean(oo * oo, -1, keepdims=True) + EPS)
        n = oo * r
        sg = _silu(gg)
        dg_ref[...] = (dy_ * n * nw_ * _dsilu(gg)).astype(BF16)
        dn = dy_ * sg * nw_
        do_ref[0] = r * (dn - n * jnp.mean(dn * n, -1, keepdims=True))

        @pl.when((pl.program_id(0) == 0) & (pl.program_id(1) == 0))
        def _():
            dnw_ref[...] = jnp.zeros_like(dnw_ref)

        dnw_ref[...] += jnp.sum(dy_ * sg * n, 0, keepdims=True)

    return pl.pallas_call(
        body, grid=(GDN_HEADS, t // GDN_ROWS),
        in_specs=[pl.BlockSpec((1, GDN_ROWS, GDN_DV), lambda hh, i: (hh, i, 0)),
                  pl.BlockSpec((GDN_ROWS, GDN_DV), lambda hh, i: (i, 3 * GDN_HEADS + hh)),
                  pl.BlockSpec((1, GDN_DV), lambda hh, i: (0, 0)),
                  pl.BlockSpec((GDN_ROWS, GDN_DV), lambda hh, i: (i, hh))],
        out_specs=[pl.BlockSpec((1, GDN_ROWS, GDN_DV), lambda hh, i: (hh, i, 0)),
                   pl.BlockSpec((GDN_ROWS, GDN_DV), lambda hh, i: (i, hh)),
                   pl.BlockSpec((1, GDN_DV), lambda hh, i: (0, 0))],
        out_shape=[jax.ShapeDtypeStruct((GDN_HEADS, t, GDN_DV), F32), jax.ShapeDtypeStruct((t, GDN_W), BF16),
                   jax.ShapeDtypeStruct((1, GDN_DV), F32)],
        name=name, compiler_params=_cp())(o, h, nw, dy)


def _tables(positions):
    pos = positions.astype(F32)[:, None]
    half = RET_DK // 2
    inv = jnp.power(RET_THETA, -jnp.arange(half, dtype=F32) * 2.0 / RET_DK)
    ang = pos * inv
    cos, sin = jnp.cos(ang), jnp.sin(ang)
    c2a = jnp.concatenate([cos, cos], 1)
    s2a = jnp.concatenate([-sin, sin], 1)
    hb = ROPE_DIMS // 2
    invb = jnp.power(ROPE_THETA, -jnp.arange(hb, dtype=F32) * 2.0 / ROPE_DIMS)
    angb = pos * invb
    cosb, sinb = jnp.cos(angb), jnp.sin(angb)
    t = pos.shape[0]
    ones = jnp.ones((t, DIL_HD - ROPE_DIMS), F32)
    zeros = jnp.zeros((t, DIL_HD - ROPE_DIMS), F32)
    z8 = jnp.zeros((t, hb), F32)
    cb = jnp.concatenate([cosb, cosb, ones] * 2, 1)
    shi = jnp.concatenate([z8, sinb, zeros] * 2, 1)
    slo = jnp.concatenate([-sinb, z8, zeros] * 2, 1)
    lg = jnp.log1p(-jnp.power(2.0, -5.0 - jnp.arange(RET_HEADS, dtype=F32)))
    lgt = jnp.broadcast_to(lg[:, None, None], (RET_HEADS, 1, LANES))
    delta = jnp.arange(ATT_BLK, dtype=jnp.int32)[:, None] + (SEQ - ATT_BLK) - jnp.arange(SEQ, dtype=jnp.int32)[None, :]
    cnt = jnp.zeros(delta.shape, F32)
    for (w, d) in DIL_PAIRS:
        cnt = cnt + ((delta >= 0) & (delta <= w) & (delta % d == 0)).astype(F32)
    strip = jnp.where(cnt > 0, jnp.log(jnp.maximum(cnt, 1.0)), NEG)
    return c2a, s2a, cb, shi, slo, lgt, strip


def _local_step(x, positions, target, get_w, put_g, small):
    c2a, s2a, cb, shi, slo, lgt, strip = _tables(positions)
    t = x.shape[0]
    saved = []
    xf = x
    xb = x.astype(BF16)
    for layer in range(DEPTH):
        j = layer // 2
        L = f"L{layer}_"
        W, dep = get_w(layer, "mixer", xb)
        rec = {"x": xf, "xb": xb}
        if layer % 2 == 0:
            h = _mm(xb, W["in_t"], tb=True, name=L + "ev_in", dep=dep)
            ro, ya = _ret_fwd(h, c2a, s2a, lgt, name=L + "ret_fwd")
            do_, yb, lse = _dil_fwd(h, cb, shi, slo, strip, name=L + "dil_fwd")
            y = jnp.concatenate([ya, yb], 1)
            mix = _mm(y, W["out"], name=L + "ev_out")
            rec.update(h=h, ro=ro, dil_o=do_, lse=lse, y=y)
        else:
            h = _mm(xb, W["in_t"], tb=True, name=L + "od_in", dep=dep)
            cw = W["conv"]
            q, k, v = _gdn_prep_fwd(h, cw, name=L + "gdn_prep")
            hs = h[:, 4 * GDN_W:4 * GDN_W + 2 * GDN_HEADS]
            bb = jnp.broadcast_to(hs[:, :GDN_HEADS].T[:, :, None], (GDN_HEADS, t, LANES))
            ab = jnp.broadcast_to(hs[:, GDN_HEADS:].T[:, :, None], (GDN_HEADS, t, LANES))
            alog = jnp.broadcast_to(small["od_a_log"][j][:, None, None], (GDN_HEADS, 1, LANES))
            dtb = jnp.broadcast_to(small["od_dt_bias"][j][:, None, None], (GDN_HEADS, 1, LANES))
            o, states = _gdn_core_fwd(q, k, v, bb, ab, alog, dtb, name=L + "gdn_fwd")
            nw = small["od_norm_w"][j][None, :]
            y = _gdn_post_fwd(o, h, nw, name=L + "gdn_post")
            mix = _mm(y, W["out"], name=L + "od_out")
            rec.update(h=h, q=q, k=k, v=v, bb=bb, ab=ab, alog=alog, dtb=dtb, states=states, o=o, y=y, nw=nw, cw=cw)
        z1, x1, x1b = _ln_fwd(xf, mix, small["ln1_g"][layer][None], small["ln1_b"][layer][None], name=L + "ln1")
        rec["Wm"] = W
        W, dep = get_w(layer, "ffn", x1b)
        rec["Wf"] = W
        u = _mm(x1b, W["up_t"], tb=True, name=L + "ffn_up", dep=dep)
        fcw = W["fconv"]
        fcb = small["ffn_conv_b"][layer][None]
        a = _ffn_mid_fwd(u, fcw, fcb, name=L + "ffn_mid")
        f = _mm(a, W["down"], name=L + "ffn_down")
        z2, x2, x2b = _ln_fwd(x1, f, small["ln2_g"][layer][None], small["ln2_b"][layer][None], name=L + "ln2")
        rec.update(z1=z1, x1b=x1b, u=u, a=a, z2=z2, fcw=fcw, fcb=fcb)
        saved.append(rec)
        xf, xb = x2, x2b

    dy, lossv = _loss_head(xf, target, name="loss_head")
    loss = lossv[0, 0]

    gS = {n: [None] * small[n].shape[0] for n in small}
    dres, dmm = dy, None
    for layer in reversed(range(DEPTH)):
        j = layer // 2
        L = f"L{layer}_"
        rec = saved[layer]
        Wm, Wf = rec["Wm"], rec["Wf"]
        g = {}
        if dmm is None:
            dz2, dz2b, dg2, db2 = _ln_bwd(rec["z2"], small["ln2_g"][layer][None], dres, None, name=L + "ln2_bwd")
        else:
            dz2, dz2b, dg2, db2 = _ln_bwd(rec["z2"], small["ln2_g"][layer][None], dmm, dres, name=L + "ln2_bwd")
        gS["ln2_g"][layer], gS["ln2_b"][layer] = dg2[0], db2[0]
        g["down"] = _mm(rec["a"], dz2b, ta=True, name=L + "ffn_down_dw", out_dtype=BF16)
        da = _mm(dz2b, Wf["down"], tb=True, name=L + "ffn_down_dx")
        du, dcw, dcb = _ffn_mid_bwd(rec["u"], rec["fcw"], rec["fcb"], da, name=L + "ffn_mid_bwd")
        g["fconv"] = dcw.astype(BF16)
        gS["ffn_conv_b"][layer] = dcb[0]
        g["up_t"] = _mm(du, rec["x1b"], ta=True, name=L + "ffn_up_dw", out_dtype=BF16)
        dep = put_g(layer, "ffn", g)
        dx1 = _mm(du, Wf["up_t"], name=L + "ffn_up_dx", dep=dep)
        dz1, dz1b, dg1, db1 = _ln_bwd(rec["z1"], small["ln1_g"][layer][None], dx1, dz2, name=L + "ln1_bwd")
        gS["ln1_g"][layer], gS["ln1_b"][layer] = dg1[0], db1[0]
        g = {}
        if layer % 2 == 0:
            g["out"] = _mm(rec["y"], dz1b, ta=True, name=L + "ev_out_dw", out_dtype=BF16)
            dyy = _mm(dz1b, Wm["out"], tb=True, name=L + "ev_out_dx")
            dqa, dka, dva, dga = _ret_bwd(rec["h"], c2a, s2a, lgt, rec["ro"], dyy, name=L + "ret_bwd")
            dqb, dkb, dvb = _dil_bwd(rec["h"], cb, shi, slo, strip, rec["dil_o"], rec["lse"], dyy, name=L + "dil_bwd")
            dh = jnp.concatenate([dqa, dka, dva, dga, dqb, dkb, dvb], 1)
            g["in_t"] = _mm(dh, rec["xb"], ta=True, name=L + "ev_in_dw", out_dtype=BF16)
            dep = put_g(layer, "mixer", g)
            dxin = _mm(dh, Wm["in_t"], name=L + "ev_in_dx", dep=dep)
        else:
            g["out"] = _mm(rec["y"], dz1b, ta=True, name=L + "od_out_dw", out_dtype=BF16)
            dyy = _mm(dz1b, Wm["out"], tb=True, name=L + "od_out_dx")
            do, dgate, dnw = _gdn_post_bwd(rec["o"], rec["h"], rec["nw"], dyy, name=L + "gdn_post_bwd")
            gS["od_norm_w"][j] = dnw[0]
            dq, dk, dv, dbb, dab, dal, ddt = _gdn_core_bwd(
                rec["q"], rec["k"], rec["v"], rec["bb"], rec["ab"], rec["alog"], rec["dtb"], rec["states"], do,
                name=L + "gdn_bwd")
            gS["od_a_log"][j] = dal[:, 0, 0]
            gS["od_dt_bias"][j] = ddt[:, 0, 0]
            dhq, dhk, dhv, dwq, dwk, dwv = _gdn_prep_bwd(rec["h"], rec["cw"], dq, dk, dv, name=L + "gdn_prep_bwd")
            g["conv"] = jnp.concatenate([dwq, dwk, dwv], 1).astype(BF16)
            dsm = jnp.concatenate([dbb[:, :, 0].T, dab[:, :, 0].T,
                                   jnp.zeros((t, LANES - 2 * GDN_HEADS), F32)], 1).astype(BF16)
            dh = jnp.concatenate([dhq, dhk, dhv, dgate, dsm], 1)
            g["in_t"] = _mm(dh, rec["xb"], ta=True, name=L + "od_in_dw", out_dtype=BF16)
            dep = put_g(layer, "mixer", g)
            dxin = _mm(dh, Wm["in_t"], name=L + "od_in_dx", dep=dep)
        dres, dmm = dz1, dxin
    grad_x = _axpy(dmm, dres, name="grad_x")
    gS = {n: jnp.stack(v) for n, v in gS.items()}
    return loss, grad_x, gS


HBM = pl.BlockSpec(memory_space=pltpu.HBM)


def _me():
    return lax.axis_index("x"), lax.axis_index("y"), lax.axis_index("c")


def _my_index():
    x, y, c = _me()
    return 4 * x + 2 * y + c


SEM = pl.BlockSpec(memory_space=pltpu.SEMAPHORE)
ANY = pl.BlockSpec(memory_space=pl.ANY)
N_PEERS = N_DEV - 1


def _peer(kk):
    x, y, c = _me()
    return x ^ (kk >> 2), y ^ ((kk >> 1) & 1), c ^ (kk & 1)


def _exchange_copies(mode, srcs, lands, send_sems, recv_sems, incoming):
    myid = _my_index()
    out = []
    for a in range(len(srcs)):
        for kk in range(1, N_DEV):
            px, py, pc = _peer(kk)
            pid = 4 * px + 2 * py + pc
            src = srcs[a] if mode == "gather" else srcs[a].at[pid]
            out.append(pltpu.make_async_remote_copy(
                src_ref=src, dst_ref=lands[a].at[pid if incoming else myid],
                send_sem=send_sems.at[a * N_PEERS + kk - 1], recv_sem=recv_sems.at[a * N_PEERS + kk - 1],
                device_id=(px, py, pc), device_id_type=MESH))
    return out


def _exchange_start(srcs, mode, after, *, name):
    k = len(srcs)
    land_shapes = [((N_DEV, *s.shape) if mode == "gather" else s.shape) for s in srcs]

    def body(*refs):
        src, land = refs[:k], refs[k:2 * k]
        o = 2 * k + (0 if after is None else 1)
        send_sems, recv_sems = refs[o], refs[o + 1]
        token = refs[o + 2 + 2 * k]
        for cp in _exchange_copies(mode, src, land, send_sems, recv_sems, False):
            cp.start()
        token[...] = jnp.zeros_like(token)

    ins = [pltpu.with_memory_space_constraint(s, pltpu.HBM) for s in srcs]
    ins += [pltpu.with_memory_space_constraint(lax.empty(shp, s.dtype), pltpu.HBM) for shp, s in zip(land_shapes, srcs)]
    outs = pl.pallas_call(
        body, name=name,
        out_shape=(pltpu.SemaphoreType.DMA((k * N_PEERS,)), pltpu.SemaphoreType.DMA((k * N_PEERS,)),
                   *[pltpu.HBM(s.shape, s.dtype) for s in srcs],
                   *[pltpu.HBM(shp, s.dtype) for shp, s in zip(land_shapes, srcs)],
                   jax.ShapeDtypeStruct((8, LANES), F32)),
        in_specs=[HBM] * (2 * k) + ([] if after is None else [ANY]),
        out_specs=(SEM, SEM, *[HBM] * (2 * k), pl.BlockSpec(memory_space=pltpu.VMEM)),
        input_output_aliases={i: 2 + i for i in range(2 * k)},
        compiler_params=pltpu.CompilerParams(has_side_effects=pltpu.SideEffectType.DATAFLOW_SIDE_EFFECTING),
    )(*ins, *([] if after is None else [after]))
    return outs[0], outs[1], list(outs[2:2 + k]), list(outs[2 + k:2 + 2 * k]), outs[2 + 2 * k]


def _exchange_wait(started, mode, after, *, name):
    send_sems, recv_sems, srcs, lands, _ = started
    k = len(srcs)

    def body(*refs):
        src, land = refs[:k], refs[k:2 * k]
        s_sems, r_sems = refs[2 * k], refs[2 * k + 1]
        for cp in _exchange_copies(mode, src, land, s_sems, r_sems, True):
            cp.wait_send()
            cp.wait_recv()

    outs = pl.pallas_call(
        body, name=name,
        out_shape=(*[pltpu.HBM(s.shape, s.dtype) for s in srcs], *[pltpu.HBM(l.shape, l.dtype) for l in lands]),
        in_specs=[HBM] * (2 * k) + [SEM, SEM, ANY], out_specs=tuple([HBM] * (2 * k)),
        input_output_aliases={i: i for i in range(2 * k)},
        compiler_params=pltpu.CompilerParams(has_side_effects=pltpu.SideEffectType.DATAFLOW_SIDE_EFFECTING),
    )(*srcs, *lands, send_sems, recv_sems, after)
    return list(outs[:k]), list(outs[k:])


def _sum8(land, *, name):
    _, rr, cc = land.shape
    tr = _row_tile(rr)

    def body(l_ref, o_ref):
        acc = l_ref[0].astype(F32)
        for d in range(1, N_DEV):
            acc = acc + l_ref[d].astype(F32)
        o_ref[...] = acc

    return pl.pallas_call(
        body, grid=(rr // tr,), in_specs=[pl.BlockSpec((N_DEV, tr, cc), lambda i: (0, i, 0))],
        out_specs=pl.BlockSpec((tr, cc), lambda i: (i, 0)), out_shape=jax.ShapeDtypeStruct((rr, cc), F32),
        name=name, compiler_params=_cp())(land)


def _all_gather(shards, *, name):
    n = len(shards)

    def body(*refs):
        ins, outs = refs[:n], refs[n:2 * n]
        send_sems, recv_sems, local_sems = refs[2 * n:]
        x, y, c = _me()
        me, sibling = (x, y, c), (x, y, 1 - c)
        chips = [(1 - x, y), (x, 1 - y), (1 - x, 1 - y)]

        def slot(out, px, py, pc):
            return out.at[4 * px + 2 * py + pc]

        def copy(a, kk, block, to, src=None):
            return pltpu.make_async_remote_copy(
                src_ref=slot(outs[a], *block) if src is None else src, dst_ref=slot(outs[a], *block),
                send_sem=send_sems.at[a, kk], recv_sem=recv_sems.at[a, kk], device_id=to, device_id_type=MESH)

        mine = [pltpu.make_async_copy(ins[a], slot(outs[a], *me), local_sems.at[a]) for a in range(n)]
        for cp in mine:
            cp.start()
        first = []
        for a in range(n):
            first.append(copy(a, 0, me, sibling, src=ins[a]))
            first += [copy(a, 1 + jj, me, (*chip, c), src=ins[a]) for jj, chip in enumerate(chips)]
        for cp in first:
            cp.start()
        passed = []
        for jj, chip in enumerate(chips):
            for a in range(n):
                copy(a, 1 + jj, (*chip, c), me).wait_recv()
                cp = copy(a, 4 + jj, (*chip, c), sibling)
                cp.start()
                passed.append(cp)
        for a in range(n):
            copy(a, 0, sibling, me).wait_recv()
            for jj, chip in enumerate(chips):
                copy(a, 4 + jj, (*chip, 1 - c), me).wait_recv()
        for cp in first + passed:
            cp.wait_send()
        for cp in mine:
            cp.wait()

    return pl.pallas_call(
        body, in_specs=[HBM] * n, out_specs=[HBM] * n,
        out_shape=[jax.ShapeDtypeStruct((N_DEV, *s.shape), s.dtype) for s in shards],
        scratch_shapes=[pltpu.SemaphoreType.DMA((n, 7)), pltpu.SemaphoreType.DMA((n, 7)), pltpu.SemaphoreType.DMA((n,))],
        name=name, compiler_params=pltpu.CompilerParams(has_side_effects=True))(*shards)


def _sibling_exchange(gs, *, name):
    n = len(gs)

    def body(*refs):
        ins, outs = refs[:n], refs[n:2 * n]
        send_sems, recv_sems = refs[2 * n:]
        x, y, c = _me()
        cps = [pltpu.make_async_remote_copy(
            src_ref=ins[a].at[:, 1 - c], dst_ref=outs[a], send_sem=send_sems.at[a], recv_sem=recv_sems.at[a],
            device_id=(x, y, 1 - c), device_id_type=MESH) for a in range(n)]
        for cp in cps:
            cp.start()
        for cp in cps:
            cp.wait()

    return pl.pallas_call(
        body, in_specs=[HBM] * n, out_specs=[HBM] * n,
        out_shape=[jax.ShapeDtypeStruct((4, *g.shape[2:]), g.dtype) for g in gs],
        scratch_shapes=[pltpu.SemaphoreType.DMA((n,)), pltpu.SemaphoreType.DMA((n,))],
        name=name, compiler_params=pltpu.CompilerParams(has_side_effects=True))(*gs)


def _pair_add(g, r, cidx, *, name):
    _, _, rr, cc = g.shape
    tr = rr
    for cand in (512, 384, 256, 192, 176, 128, 64, 32, 16, 8):
        if rr % cand == 0:
            tr = cand
            break
    if rr < 8:
        tr = rr

    def body(c_ref, g_ref, r_ref, ob_ref, of_ref):
        s = g_ref[0, 0] + r_ref[0]
        ob_ref[0] = s.astype(BF16)
        of_ref[0] = s

    grid_spec = pltpu.PrefetchScalarGridSpec(
        num_scalar_prefetch=1, grid=(4, rr // tr),
        in_specs=[pl.BlockSpec((1, 1, tr, cc), lambda kk, i, c_ref: (kk, c_ref[0], i, 0)),
                  pl.BlockSpec((1, tr, cc), lambda kk, i, c_ref: (kk, i, 0))],
        out_specs=[pl.BlockSpec((1, tr, cc), lambda kk, i, c_ref: (kk, i, 0)),
                   pl.BlockSpec((1, tr, cc), lambda kk, i, c_ref: (kk, i, 0))])
    return pl.pallas_call(
        body, grid_spec=grid_spec,
        out_shape=[jax.ShapeDtypeStruct((4, rr, cc), BF16), jax.ShapeDtypeStruct((4, rr, cc), F32)],
        name=name, compiler_params=_cp())(cidx, g, r)


def _chip_exchange(ps, *, name):
    n = len(ps)

    def body(*refs):
        ins, outs = refs[:n], refs[n:2 * n]
        send_sems, recv_sems = refs[2 * n:]
        x, y, c = _me()
        chips = [(1 - x, y), (x, 1 - y), (1 - x, 1 - y)]
        cps = []
        for a in range(n):
            for jj, (px, py) in enumerate(chips):
                cps.append(pltpu.make_async_remote_copy(
                    src_ref=ins[a].at[2 * px + py], dst_ref=outs[a].at[jj],
                    send_sem=send_sems.at[a, jj], recv_sem=recv_sems.at[a, jj],
                    device_id=(px, py, c), device_id_type=MESH))
        for cp in cps:
            cp.start()
        for cp in cps:
            cp.wait()

    return pl.pallas_call(
        body, in_specs=[HBM] * n, out_specs=[HBM] * n,
        out_shape=[jax.ShapeDtypeStruct((3, *p.shape[1:]), p.dtype) for p in ps],
        scratch_shapes=[pltpu.SemaphoreType.DMA((n, 3)), pltpu.SemaphoreType.DMA((n, 3))],
        name=name, compiler_params=pltpu.CompilerParams(has_side_effects=True))(*ps)


def _row_tile(rr):
    for cand in (512, 384, 256, 192, 176, 128, 64, 32, 16, 8):
        if rr % cand == 0:
            return cand
    return rr


def _sum4(pf, recv, chip, *, name):
    _, rr, cc = pf.shape
    tr = _row_tile(rr)

    def body(c_ref, p_ref, r_ref, o_ref):
        o_ref[...] = ((p_ref[0] + r_ref[0].astype(F32)) + r_ref[1].astype(F32)) + r_ref[2].astype(F32)

    grid_spec = pltpu.PrefetchScalarGridSpec(
        num_scalar_prefetch=1, grid=(rr // tr,),
        in_specs=[pl.BlockSpec((1, tr, cc), lambda i, c_ref: (c_ref[0], i, 0)),
                  pl.BlockSpec((3, tr, cc), lambda i, c_ref: (0, i, 0))],
        out_specs=pl.BlockSpec((tr, cc), lambda i, c_ref: (i, 0)))
    return pl.pallas_call(body, grid_spec=grid_spec, out_shape=jax.ShapeDtypeStruct((rr, cc), F32),
                          name=name, compiler_params=_cp())(chip, pf, recv)


def _small_exchange(vec, *, name):
    rr = vec.shape[0]

    def body(v_ref, o_ref, send_sems, recv_sems):
        x, y, c = _me()
        myid = 4 * x + 2 * y + c
        o_ref[myid] = v_ref[...]
        cps = []
        for kk in range(1, N_DEV):
            px, py, pc = x ^ (kk >> 2), y ^ ((kk >> 1) & 1), c ^ (kk & 1)
            cps.append(pltpu.make_async_remote_copy(
                src_ref=v_ref, dst_ref=o_ref.at[myid], send_sem=send_sems.at[kk], recv_sem=recv_sems.at[kk],
                device_id=(px, py, pc), device_id_type=MESH))
        for cp in cps:
            cp.start()
        for kk in range(1, N_DEV):
            px, py, pc = x ^ (kk >> 2), y ^ ((kk >> 1) & 1), c ^ (kk & 1)
            pltpu.make_async_remote_copy(
                src_ref=v_ref, dst_ref=o_ref.at[4 * px + 2 * py + pc], send_sem=send_sems.at[kk],
                recv_sem=recv_sems.at[kk], device_id=(px, py, pc), device_id_type=MESH).wait_recv()
        for cp in cps:
            cp.wait_send()

    return pl.pallas_call(
        body, in_specs=[pl.BlockSpec(memory_space=pltpu.VMEM)], out_specs=pl.BlockSpec(memory_space=pltpu.VMEM),
        out_shape=jax.ShapeDtypeStruct((N_DEV, rr, LANES), F32),
        scratch_shapes=[pltpu.SemaphoreType.DMA((N_DEV,)), pltpu.SemaphoreType.DMA((N_DEV,))],
        name=name, compiler_params=pltpu.CompilerParams(has_side_effects=True))(vec)


def _adam_math(w, g, m, v):
    m = ADAM_B1 * m + (1.0 - ADAM_B1) * g
    v = ADAM_B2 * v + (1.0 - ADAM_B2) * (g * g)
    m_hat = m / (1.0 - ADAM_B1 ** ADAM_STEP)
    v_hat = v / (1.0 - ADAM_B2 ** ADAM_STEP)
    delta = -ADAM_LR * (m_hat / (jnp.sqrt(v_hat) + ADAM_EPS) + ADAM_WD * w)
    return delta, m, v


def _adamw_sharded(w, m, v, g, *, name):
    ll, rr, cc = w.shape
    tr = _row_tile(rr)

    def body(w_ref, m_ref, v_ref, g_ref, d_ref, nm_ref, nv_ref):
        d, nm, nv = _adam_math(w_ref[...], g_ref[...], m_ref[...], v_ref[...])
        d_ref[...] = d
        nm_ref[...] = nm
        nv_ref[...] = nv

    blk = pl.BlockSpec((1, tr, cc), lambda l, i: (l, i, 0))
    sh = jax.ShapeDtypeStruct((ll, rr, cc), F32)
    return pl.pallas_call(
        body, grid=(ll, rr // tr), in_specs=[blk] * 4, out_specs=[blk] * 3, out_shape=[sh] * 3,
        name=name, compiler_params=_cp())(w, m, v, g)


def _adamw_small(w, m, v, gall, *, name):
    rr = w.shape[0]

    def body(w_ref, m_ref, v_ref, g_ref, go_ref, d_ref, nm_ref, nv_ref):
        g = g_ref[0]
        for kk in range(1, N_DEV):
            g = g + g_ref[kk]
        d, nm, nv = _adam_math(w_ref[...], g, m_ref[...], v_ref[...])
        go_ref[...] = g
        d_ref[...] = d
        nm_ref[...] = nm
        nv_ref[...] = nv

    sh = jax.ShapeDtypeStruct((rr, LANES), F32)
    return pl.pallas_call(body, out_shape=[sh] * 4, name=name, compiler_params=_cp())(w, m, v, gall)


SHARDED = ("ev_w_in", "ev_w_out", "od_w_in", "od_conv_w", "od_w_out", "ffn_w_up", "ffn_conv_w", "ffn_w_down")
SMALL = ("od_a_log", "od_dt_bias", "od_norm_w", "ffn_conv_b", "ln1_g", "ln1_b", "ln2_g", "ln2_b")
ALL_W = ("ev_w_in", "ev_w_out", "od_w_in", "od_conv_w", "od_a_log", "od_dt_bias", "od_norm_w", "od_w_out",
         "ffn_w_up", "ffn_conv_w", "ffn_conv_b", "ffn_w_down", "ln1_g", "ln1_b", "ln2_g", "ln2_b")


def _layer_items(layer):
    j = layer // 2
    if layer % 2 == 0:
        mixer = [("in_t", "ev_w_in", j, "colT"), ("out", "ev_w_out", j, "row")]
    else:
        mixer = [("in_t", "od_w_in", j, "colT"), ("conv", "od_conv_w", j, "colsmall"), ("out", "od_w_out", j, "row")]
    return mixer + [("up_t", "ffn_w_up", layer, "colT"), ("fconv", "ffn_conv_w", layer, "colsmall"),
                    ("down", "ffn_w_down", layer, "row")]


def _to_send(kind, shard):
    if kind == "colT":
        return shard.T.astype(BF16)
    return shard.astype(BF16) if kind == "row" else shard


def _from_gather(kind, name, g):
    if kind == "colsmall":
        return jnp.transpose(g, (1, 0, 2)).reshape(g.shape[1], -1)
    full = g.reshape(-1, g.shape[-1])
    if name == "od_w_in":
        full = jnp.pad(full, ((0, OD_IN_PAD - OD_IN), (0, 0)))
    return full


def _by_owner(kind, name, gfull):
    if kind == "colsmall":
        kk, c8 = gfull.shape
        return jnp.transpose(gfull.reshape(kk, N_DEV, c8 // N_DEV), (1, 0, 2))
    if name == "od_w_in":
        gfull = gfull[:OD_IN]
    return gfull.reshape(N_DEV, gfull.shape[0] // N_DEV, gfull.shape[1])


def _pack_small(d):
    flat = jnp.concatenate([d[n].reshape(-1) for n in SMALL])
    pad = (-flat.shape[0]) % (8 * LANES)
    return jnp.pad(flat, (0, pad)).reshape(-1, LANES)


def _unpack_small(packed, like):
    flat = packed.reshape(-1)
    out, off = {}, 0
    for n in SMALL:
        sz = int(np.prod(like[n].shape))
        out[n] = flat[off:off + sz].reshape(like[n].shape)
        off += sz
    return out


def kernel(x, positions, ev_w_in, ev_w_out, od_w_in, od_conv_w, od_a_log, od_dt_bias, od_norm_w, od_w_out, ffn_w_up, ffn_conv_w, ffn_conv_b, ffn_w_down, ln1_g, ln1_b, ln2_g, ln2_b, loss_target, m_ev_w_in, m_ev_w_out, m_od_w_in, m_od_conv_w, m_od_a_log, m_od_dt_bias, m_od_norm_w, m_od_w_out, m_ffn_w_up, m_ffn_conv_w, m_ffn_conv_b, m_ffn_w_down, m_ln1_g, m_ln1_b, m_ln2_g, m_ln2_b, v_ev_w_in, v_ev_w_out, v_od_w_in, v_od_conv_w, v_od_a_log, v_od_dt_bias, v_od_norm_w, v_od_w_out, v_ffn_w_up, v_ffn_conv_w, v_ffn_conv_b, v_ffn_w_down, v_ln1_g, v_ln1_b, v_ln2_g, v_ln2_b):
    w = dict(ev_w_in=ev_w_in, ev_w_out=ev_w_out, od_w_in=od_w_in, od_conv_w=od_conv_w, od_a_log=od_a_log,
             od_dt_bias=od_dt_bias, od_norm_w=od_norm_w, od_w_out=od_w_out, ffn_w_up=ffn_w_up, ffn_conv_w=ffn_conv_w,
             ffn_conv_b=ffn_conv_b, ffn_w_down=ffn_w_down, ln1_g=ln1_g, ln1_b=ln1_b, ln2_g=ln2_g, ln2_b=ln2_b)
    mom = dict(ev_w_in=m_ev_w_in, ev_w_out=m_ev_w_out, od_w_in=m_od_w_in, od_conv_w=m_od_conv_w, od_a_log=m_od_a_log,
               od_dt_bias=m_od_dt_bias, od_norm_w=m_od_norm_w, od_w_out=m_od_w_out, ffn_w_up=m_ffn_w_up,
               ffn_conv_w=m_ffn_conv_w, ffn_conv_b=m_ffn_conv_b, ffn_w_down=m_ffn_w_down, ln1_g=m_ln1_g,
               ln1_b=m_ln1_b, ln2_g=m_ln2_g, ln2_b=m_ln2_b)
    var = dict(ev_w_in=v_ev_w_in, ev_w_out=v_ev_w_out, od_w_in=v_od_w_in, od_conv_w=v_od_conv_w, od_a_log=v_od_a_log,
               od_dt_bias=v_od_dt_bias, od_norm_w=v_od_norm_w, od_w_out=v_od_w_out, ffn_w_up=v_ffn_w_up,
               ffn_conv_w=v_ffn_conv_w, ffn_conv_b=v_ffn_conv_b, ffn_w_down=v_ffn_w_down, ln1_g=v_ln1_g,
               ln1_b=v_ln1_b, ln2_g=v_ln2_g, ln2_b=v_ln2_b)

    myid = _my_index()
    small = {n: w[n] for n in SMALL}
    groups = [(layer, part) for layer in range(DEPTH) for part in ("mixer", "ffn")]

    def group_items(gi):
        layer, part = groups[gi]
        its = _layer_items(layer)
        return its[:-3] if part == "mixer" else its[-3:]

    fetch = {}

    def gather_start(gi, after):
        srcs = [_to_send(kind, w[n][j]) for (_, n, j, kind) in group_items(gi)]
        fetch[gi] = _exchange_start(srcs, "gather", after, name=f"gather{gi}_start")

    def get_w(layer, part, after):
        gi = groups.index((layer, part))
        srcs, lands = _exchange_wait(fetch.pop(gi), "gather", after, name=f"gather{gi}_wait")
        lands = [lax.dynamic_update_index_in_dim(l, s, myid, 0) for l, s in zip(lands, srcs)]
        dep = None
        if gi + 1 < len(groups):
            gather_start(gi + 1, lands[0])
            dep = fetch[gi + 1][4]
        return {key: _from_gather(kind, n, l) for (key, n, _, kind), l in zip(group_items(gi), lands)}, dep

    landed = {}
    pending = []

    def scatter_finish(after):
        started, gi = pending.pop()
        srcs, lands = _exchange_wait(started, "scatter", after, name=f"scatter{gi}_wait")
        for (key, _, _, _), l, s in zip(group_items(gi), lands, srcs):
            own = lax.dynamic_index_in_dim(s, myid, 0, keepdims=False)
            landed[(groups[gi][0], key)] = lax.dynamic_update_index_in_dim(l, own, myid, 0)

    def put_g(layer, part, g):
        gi = groups.index((layer, part))
        srcs = [_by_owner(kind, n, g[key]) for (key, n, _, kind) in group_items(gi)]
        started = _exchange_start(srcs, "scatter", None, name=f"scatter{gi}_start")
        if pending:
            scatter_finish(started[4])
        pending.append((started, gi))
        return started[4]

    gather_start(0, None)
    loss, grad_x, gS = _local_step(x[0], positions[0], loss_target[0], get_w, put_g, small)
    loss = lax.psum(loss, ("x", "y", "c"))

    outs_g, outs_d, outs_m, outs_v = {}, {}, {}, {}
    where = {n: [None] * w[n].shape[0] for n in SHARDED}
    for layer in range(DEPTH):
        for (key, n, j, kind) in _layer_items(layer):
            where[n][j] = (layer, key, kind)

    def update(n):
        parts = []
        for (layer, key, kind) in where[n]:
            gsh = _sum8(landed[(layer, key)], name=f"L{layer}_{key}_sum")
            parts.append(gsh.T if kind == "colT" else gsh)
        outs_g[n] = jnp.stack(parts)
        outs_d[n], outs_m[n], outs_v[n] = _adamw_sharded(w[n], mom[n], var[n], outs_g[n], name=f"adamw_{n}")

    last = {n for (_, n, _, _) in group_items(pending[0][1])}
    for n in SHARDED:
        if n not in last:
            update(n)
    scatter_finish(outs_d[[n for n in SHARDED if n not in last][-1]])
    for n in SHARDED:
        if n in last:
            update(n)

    gall = _small_exchange(_pack_small(gS), name="small_grads_exchange")
    g, d, nm, nv = _adamw_small(_pack_small({n: w[n] for n in SMALL}), _pack_small({n: mom[n] for n in SMALL}),
                                _pack_small({n: var[n] for n in SMALL}), gall, name="adamw_small")
    for dst, packed in ((outs_g, g), (outs_d, d), (outs_m, nm), (outs_v, nv)):
        dst.update(_unpack_small(packed, {n: w[n] for n in SMALL}))

    return (loss, grad_x[None], *[outs_g[n] for n in ALL_W], *[outs_d[n] for n in ALL_W],
            *[outs_m[n] for n in ALL_W], *[outs_v[n] for n in ALL_W])
```

```python
import functools
import math

import numpy as np
import jax
import jax.numpy as jnp
from jax import lax
from jax.experimental import pallas as pl
from jax.experimental.pallas import tpu as pltpu

F32 = jnp.float32
BF16 = jnp.bfloat16
MESH = pl.DeviceIdType.MESH

D_MODEL = 1024
SEQ = 2048
DEPTH = 4
N_DEV = 8
RET_HEADS, RET_DK, RET_DV = 4, 128, 256
RET_THETA = 10000.0
DIL_HEADS, DIL_HD = 8, 64
DIL_PAIRS = ((128, 1), (512, 4), (2048, 16))
ROPE_THETA = 500000.0
ROPE_DIMS = DIL_HD // 4
GDN_HEADS, GDN_DK, GDN_DV, GDN_CHUNK, GDN_CONV = 8, 128, 128, 64, 4
D_FF = 2816
FFN_CONV = 3
ALPHA = (2.0 * DEPTH) ** 0.25
EPS = 1e-5
RET_QK_W = RET_HEADS * RET_DK
RET_V_W = RET_HEADS * RET_DV
DIL_W = DIL_HEADS * DIL_HD
EV_IN = 2 * RET_QK_W + 2 * RET_V_W + 3 * DIL_W
EV_MIX = RET_V_W + DIL_W
GDN_W = GDN_HEADS * GDN_DK
OD_IN = 4 * GDN_W + 2 * GDN_HEADS
OD_IN_PAD = 4 * GDN_W + 128
ADAM_LR, ADAM_B1, ADAM_B2, ADAM_EPS, ADAM_WD, ADAM_STEP = 0.001, 0.9, 0.999, 1e-08, 0.01, 10

LANES = 128
VMEM_LIMIT = 56 * 1024 * 1024
ATT_BLK = 256
NEG = -1e30


def _cp(**kw):
    return pltpu.CompilerParams(vmem_limit_bytes=VMEM_LIMIT, **kw)


def _tile(n, cap):
    if n <= cap:
        return n
    best = None
    for t in range(LANES, cap + 1, LANES):
        if n % t == 0:
            best = t
    assert best is not None, (n, cap)
    return best


def _mm(a, b, *, ta=False, tb=False, name, out_dtype=F32, dep=None, tm=None, tn=None):
    m = a.shape[1] if ta else a.shape[0]
    k = a.shape[0] if ta else a.shape[1]
    n = b.shape[0] if tb else b.shape[1]
    assert (b.shape[1] if tb else b.shape[0]) == k
    assert a.dtype == BF16 and b.dtype == BF16
    if tn is None:
        tn = n if n <= 1024 else _tile(n, 512)
    if tm is None:
        tm = m if (tn < n and k <= 1024 and m <= 2048) else _tile(m, 512)
    dims = (((0 if ta else 1,), (1 if tb else 0,)), ((), ()))

    def body(a_ref, b_ref, *rest):
        o_ref = rest[-1]
        o_ref[...] = lax.dot_general(a_ref[...], b_ref[...], dims,
                                     preferred_element_type=F32).astype(o_ref.dtype)

    a_spec = pl.BlockSpec((k, tm), lambda i, j: (0, i)) if ta else pl.BlockSpec((tm, k), lambda i, j: (i, 0))
    b_spec = pl.BlockSpec((tn, k), lambda i, j: (j, 0)) if tb else pl.BlockSpec((k, tn), lambda i, j: (0, j))
    extra = [] if dep is None else [dep]
    return pl.pallas_call(
        body, grid=(m // tm, n // tn), in_specs=[a_spec, b_spec] + [pl.BlockSpec(memory_space=pl.ANY)] * len(extra),
        out_specs=pl.BlockSpec((tm, tn), lambda i, j: (i, j)),
        out_shape=jax.ShapeDtypeStruct((m, n), out_dtype), name=name, compiler_params=_cp())(a, b, *extra)


LN_ROWS = 256


def _ln_fwd(x, m, g, b, *, name):
    t, d = x.shape

    def body(x_ref, m_ref, g_ref, b_ref, z_ref, y_ref, yb_ref):
        z = ALPHA * x_ref[...] + m_ref[...]
        mu = jnp.mean(z, -1, keepdims=True)
        zc = z - mu
        var = jnp.mean(zc * zc, -1, keepdims=True)
        y = zc * lax.rsqrt(var + EPS) * g_ref[...] + b_ref[...]
        z_ref[...] = z
        y_ref[...] = y
        yb_ref[...] = y.astype(BF16)

    row = pl.BlockSpec((LN_ROWS, d), lambda i: (i, 0))
    vec = pl.BlockSpec((1, d), lambda i: (0, 0))
    return pl.pallas_call(
        body, grid=(t // LN_ROWS,), in_specs=[row, row, vec, vec], out_specs=[row, row, row],
        out_shape=[jax.ShapeDtypeStruct((t, d), F32), jax.ShapeDtypeStruct((t, d), F32),
                   jax.ShapeDtypeStruct((t, d), BF16)],
        name=name, compiler_params=_cp())(x, m, g, b)


def _ln_bwd(z, g, dya, dyb, *, name):
    t, d = z.shape
    two = dyb is not None

    def body(*refs):
        if two:
            z_ref, g_ref, dya_ref, dyb_ref, dz_ref, dzb_ref, dg_ref, db_ref = refs
            dy = dya_ref[...] + ALPHA * dyb_ref[...]
        else:
            z_ref, g_ref, dya_ref, dz_ref, dzb_ref, dg_ref, db_ref = refs
            dy = dya_ref[...]
        zz = z_ref[...]
        mu = jnp.mean(zz, -1, keepdims=True)
        zc = zz - mu
        var = jnp.mean(zc * zc, -1, keepdims=True)
        r = lax.rsqrt(var + EPS)
        xh = zc * r
        dxh = dy * g_ref[...]
        dz = r * (dxh - jnp.mean(dxh, -1, keepdims=True) - xh * jnp.mean(dxh * xh, -1, keepdims=True))
        dz_ref[...] = dz
        dzb_ref[...] = dz.astype(BF16)

        @pl.when(pl.program_id(0) == 0)
        def _():
            dg_ref[...] = jnp.zeros_like(dg_ref)
            db_ref[...] = jnp.zeros_like(db_ref)

        dg_ref[...] += jnp.sum(dy * xh, 0, keepdims=True)
        db_ref[...] += jnp.sum(dy, 0, keepdims=True)

    row = pl.BlockSpec((LN_ROWS, d), lambda i: (i, 0))
    vec = pl.BlockSpec((1, d), lambda i: (0, 0))
    ins = [z, g, dya] + ([dyb] if two else [])
    return pl.pallas_call(
        body, grid=(t // LN_ROWS,), in_specs=[row, vec, row] + ([row] if two else []),
        out_specs=[row, row, vec, vec],
        out_shape=[jax.ShapeDtypeStruct((t, d), F32), jax.ShapeDtypeStruct((t, d), BF16),
                   jax.ShapeDtypeStruct((1, d), F32), jax.ShapeDtypeStruct((1, d), F32)],
        name=name, compiler_params=_cp())(*ins)


def _axpy(a, b, *, name):
    t, d = a.shape

    def body(a_ref, b_ref, o_ref):
        o_ref[...] = a_ref[...] + ALPHA * b_ref[...]

    row = pl.BlockSpec((LN_ROWS, d), lambda i: (i, 0))
    return pl.pallas_call(body, grid=(t // LN_ROWS,), in_specs=[row, row], out_specs=row,
                          out_shape=jax.ShapeDtypeStruct((t, d), F32), name=name, compiler_params=_cp())(a, b)


def _loss_head(y, target, *, name):
    t, d = y.shape

    def body(y_ref, t_ref, dy_ref, l_ref):
        e = y_ref[...] - t_ref[...]
        dy_ref[...] = e * (1.0 / d)

        @pl.when(pl.program_id(0) == 0)
        def _():
            l_ref[...] = jnp.zeros_like(l_ref)

        l_ref[...] += jnp.zeros_like(l_ref) + 0.5 * jnp.sum(jnp.mean(e * e, -1, keepdims=True), 0, keepdims=True)

    row = pl.BlockSpec((LN_ROWS, d), lambda i: (i, 0))
    return pl.pallas_call(
        body, grid=(t // LN_ROWS,), in_specs=[row, row],
        out_specs=[row, pl.BlockSpec((1, LANES), lambda i: (0, 0))],
        out_shape=[jax.ShapeDtypeStruct((t, d), F32), jax.ShapeDtypeStruct((1, LANES), F32)],
        name=name, compiler_params=_cp())(y, target)


def _sig(x):
    return 1.0 / (1.0 + jnp.exp(-x))


def _silu(x):
    return x * _sig(x)


def _dsilu(x):
    s = _sig(x)
    return s * (1.0 + x * (1.0 - s))


def _shift_down(u, k, row):
    if k == 0:
        return u
    return jnp.where(row >= k, pltpu.roll(u, k, 0), 0.0)


def _shift_up(u, k, row):
    if k == 0:
        return u
    t = u.shape[0]
    return jnp.where(row < t - k, pltpu.roll(u, t - k, 0), 0.0)


def _dwconv(u, w_ref, row):
    kk = w_ref.shape[0]
    acc = None
    for j in range(kk):
        term = w_ref[j:j + 1, :] * _shift_down(u, kk - 1 - j, row)
        acc = term if acc is None else acc + term
    return acc


def _dwconv_bwd(u, w_ref, dc, row, dw_ref):
    kk = w_ref.shape[0]
    du = None
    for j in range(kk):
        term = w_ref[j:j + 1, :] * _shift_up(dc, kk - 1 - j, row)
        du = term if du is None else du + term
        dw_ref[j:j + 1, :] = jnp.sum(dc * _shift_down(u, kk - 1 - j, row), 0, keepdims=True)
    return du


CONV_ROWS = 64


def _rows(b):
    return pl.ds(pl.multiple_of(b * CONV_ROWS, CONV_ROWS), CONV_ROWS)


def _shifted_down(ref, b, k, row):
    cur = ref[_rows(b), :]
    if k == 0:
        return cur
    prev = jnp.where(b > 0, ref[_rows(jnp.maximum(b - 1, 0)), :], 0.0)
    return jnp.where(row >= k, pltpu.roll(cur, k, 0), pltpu.roll(prev, k, 0))


def _shifted_up(ref, b, k, row, nblk):
    cur = ref[_rows(b), :]
    if k == 0:
        return cur
    nxt = jnp.where(b < nblk - 1, ref[_rows(jnp.minimum(b + 1, nblk - 1)), :], 0.0)
    return jnp.where(row < CONV_ROWS - k, pltpu.roll(cur, CONV_ROWS - k, 0), pltpu.roll(nxt, CONV_ROWS - k, 0))


def _dwconv_blk(u_ref, w_ref, b, row):
    kk = w_ref.shape[0]
    views = [_shifted_down(u_ref, b, kk - 1 - j, row) for j in range(kk)]
    acc = None
    for j in range(kk):
        term = w_ref[j:j + 1, :] * views[j]
        acc = term if acc is None else acc + term
    return acc, views


def _dwconv_du_blk(dc_ref, w_ref, b, row, nblk):
    kk = w_ref.shape[0]
    du = None
    for j in range(kk):
        term = w_ref[j:j + 1, :] * _shifted_up(dc_ref, b, kk - 1 - j, row, nblk)
        du = term if du is None else du + term
    return du


FFN_TC = 256


def _ffn_mid_fwd(u, cw, cb, *, name):
    t = u.shape[0]
    nb = D_FF // FFN_TC

    def body(ug_ref, uv_ref, wg_ref, wv_ref, bg_ref, bv_ref, a_ref):
        row = lax.broadcasted_iota(jnp.int32, (t, FFN_TC), 0)
        cg = _dwconv(ug_ref[...], wg_ref, row) + bg_ref[...]
        cv = _dwconv(uv_ref[...], wv_ref, row) + bv_ref[...]
        a_ref[...] = (_silu(cg) * cv).astype(BF16)

    col = lambda off: pl.BlockSpec((t, FFN_TC), lambda j: (0, j + off))
    wsp = lambda off: pl.BlockSpec((FFN_CONV, FFN_TC), lambda j: (0, j + off))
    bsp = lambda off: pl.BlockSpec((1, FFN_TC), lambda j: (0, j + off))
    return pl.pallas_call(
        body, grid=(nb,), in_specs=[col(0), col(nb), wsp(0), wsp(nb), bsp(0), bsp(nb)],
        out_specs=pl.BlockSpec((t, FFN_TC), lambda j: (0, j)),
        out_shape=jax.ShapeDtypeStruct((t, D_FF), BF16), name=name, compiler_params=_cp())(u, u, cw, cw, cb, cb)


def _ffn_mid_bwd(u, cw, cb, da, *, name):
    t = u.shape[0]
    nb = D_FF // FFN_TC

    nblk = t // CONV_ROWS

    def body(ug_ref, uv_ref, wg_ref, wv_ref, bg_ref, bv_ref, da_ref,
             dug_ref, duv_ref, dwg_ref, dwv_ref, dbg_ref, dbv_ref, dcg_s, dcv_s):
        row = lax.broadcasted_iota(jnp.int32, (CONV_ROWS, FFN_TC), 0)
        zero = jnp.zeros((1, FFN_TC), F32)

        def first(b, acc):
            cg, ugs = _dwconv_blk(ug_ref, wg_ref, b, row)
            cv, uvs = _dwconv_blk(uv_ref, wv_ref, b, row)
            cg = cg + bg_ref[...]
            cv = cv + bv_ref[...]
            da_ = da_ref[_rows(b), :]
            dcv = da_ * _silu(cg)
            dcg = da_ * cv * _dsilu(cg)
            dcg_s[_rows(b), :] = dcg
            dcv_s[_rows(b), :] = dcv
            red = [jnp.sum(dcg * s, 0, keepdims=True) for s in ugs] + [jnp.sum(dcg, 0, keepdims=True)]
            red += [jnp.sum(dcv * s, 0, keepdims=True) for s in uvs] + [jnp.sum(dcv, 0, keepdims=True)]
            return tuple(a + r for a, r in zip(acc, red))

        acc = lax.fori_loop(0, nblk, first, (zero,) * (2 * FFN_CONV + 2))
        for j in range(FFN_CONV):
            dwg_ref[j:j + 1, :] = acc[j]
            dwv_ref[j:j + 1, :] = acc[FFN_CONV + 1 + j]
        dbg_ref[...] = acc[FFN_CONV]
        dbv_ref[...] = acc[2 * FFN_CONV + 1]

        def second(b, carry):
            dug_ref[_rows(b), :] = _dwconv_du_blk(dcg_s, wg_ref, b, row, nblk).astype(BF16)
            duv_ref[_rows(b), :] = _dwconv_du_blk(dcv_s, wv_ref, b, row, nblk).astype(BF16)
            return carry

        lax.fori_loop(0, nblk, second, 0)

    col = lambda off: pl.BlockSpec((t, FFN_TC), lambda j: (0, j + off))
    wsp = lambda off: pl.BlockSpec((FFN_CONV, FFN_TC), lambda j: (0, j + off))
    bsp = lambda off: pl.BlockSpec((1, FFN_TC), lambda j: (0, j + off))
    outs = pl.pallas_call(
        body, grid=(nb,), in_specs=[col(0), col(nb), wsp(0), wsp(nb), bsp(0), bsp(nb), col(0)],
        out_specs=[col(0), col(0), wsp(0), wsp(0), bsp(0), bsp(0)],
        out_shape=[jax.ShapeDtypeStruct((t, D_FF), BF16), jax.ShapeDtypeStruct((t, D_FF), BF16),
                   jax.ShapeDtypeStruct((FFN_CONV, D_FF), F32), jax.ShapeDtypeStruct((FFN_CONV, D_FF), F32),
                   jax.ShapeDtypeStruct((1, D_FF), F32), jax.ShapeDtypeStruct((1, D_FF), F32)],
        scratch_shapes=[pltpu.VMEM((t, FFN_TC), F32), pltpu.VMEM((t, FFN_TC), F32)],
        name=name, compiler_params=_cp())(u, u, cw, cw, cb, cb, da)
    dug, duv, dwg, dwv, dbg, dbv = outs
    return (jnp.concatenate([dug, duv], 1), jnp.concatenate([dwg, dwv], 1), jnp.concatenate([dbg, dbv], 1))


def _rot_a(x, c2, s2):
    return x * c2 + pltpu.roll(x, RET_DK // 2, 1) * s2


def _rot_a_t(dy, c2, s2):
    return dy * c2 + pltpu.roll(dy * s2, RET_DK // 2, 1)


def _decay_tile(lg, blk_diff):
    r = lax.broadcasted_iota(jnp.int32, (ATT_BLK, ATT_BLK), 0)
    c = lax.broadcasted_iota(jnp.int32, (ATT_BLK, ATT_BLK), 1)
    rel = r - c + blk_diff * ATT_BLK
    return jnp.where(rel >= 0, jnp.exp(jnp.maximum(rel, 0).astype(F32) * lg), 0.0)


def _nt(a, b):
    return lax.dot_general(a, b, (((1,), (1,)), ((), ())), preferred_element_type=F32)


def _nn(a, b):
    return lax.dot_general(a, b, (((1,), (0,)), ((), ())), preferred_element_type=F32)


def _tn(a, b):
    return lax.dot_general(a, b, (((0,), (0,)), ((), ())), preferred_element_type=F32)


def _ret_specs(t):
    q = pl.BlockSpec((t, RET_DK), lambda h: (0, h))
    k = pl.BlockSpec((t, RET_DK), lambda h: (0, RET_HEADS + h))
    v = pl.BlockSpec((t, RET_DV), lambda h: (0, RET_HEADS + h))
    g = pl.BlockSpec((t, RET_DV), lambda h: (0, 2 * RET_HEADS + h))
    tab = pl.BlockSpec((t, RET_DK), lambda h: (0, 0))
    lg = pl.BlockSpec((1, 1, LANES), lambda h: (h, 0, 0))
    return q, k, v, g, tab, lg


def _ret_fwd(h, c2, s2, lgt, *, name):
    t = h.shape[0]
    nblk = t // ATT_BLK
    scale = RET_DK ** -0.5

    def body(q_ref, k_ref, v_ref, g_ref, c_ref, s_ref, lg_ref, o_ref, ya_ref, qs, ks, vs):
        c2_, s2_ = c_ref[...], s_ref[...]
        qs[...] = _rot_a(q_ref[...], c2_, s2_).astype(BF16)
        ks[...] = (_rot_a(k_ref[...], c2_, s2_) * scale).astype(BF16)
        vs[...] = v_ref[...].astype(BF16)
        lg = lg_ref[0, :, 0:1]
        for i in range(nblk):
            qi = qs[pl.ds(i * ATT_BLK, ATT_BLK), :]
            acc = jnp.zeros((ATT_BLK, RET_DV), F32)
            for j in range(i + 1):
                sl = pl.ds(j * ATT_BLK, ATT_BLK)
                s = _nt(qi, ks[sl, :]) * _decay_tile(lg, i - j)
                acc = acc + _nn(s.astype(BF16), vs[sl, :])
            rows = pl.ds(i * ATT_BLK, ATT_BLK)
            o_ref[rows, :] = acc
            r = lax.rsqrt(jnp.mean(acc * acc, -1, keepdims=True) + EPS)
            ya_ref[rows, :] = (acc * r * _silu(g_ref[rows, :])).astype(BF16)

    q, k, v, g, tab, lg = _ret_specs(t)
    out = pl.BlockSpec((t, RET_DV), lambda hh: (0, hh))
    return pl.pallas_call(
        body, grid=(RET_HEADS,), in_specs=[q, k, v, g, tab, tab, lg], out_specs=[out, out],
        out_shape=[jax.ShapeDtypeStruct((t, RET_V_W), F32), jax.ShapeDtypeStruct((t, RET_V_W), BF16)],
        scratch_shapes=[pltpu.VMEM((t, RET_DK), BF16), pltpu.VMEM((t, RET_DK), BF16), pltpu.VMEM((t, RET_DV), BF16)],
        name=name, compiler_params=_cp())(h, h, h, h, c2, s2, lgt)


def _ret_bwd(h, c2, s2, lgt, o, dy, *, name):
    t = h.shape[0]
    nblk = t // ATT_BLK
    scale = RET_DK ** -0.5

    def body(q_ref, k_ref, v_ref, g_ref, c_ref, s_ref, lg_ref, o_ref, dy_ref,
             dq_ref, dk_ref, dv_ref, dg_ref, qs, ks, vs, dos, dka, dva):
        c2_, s2_ = c_ref[...], s_ref[...]
        qs[...] = _rot_a(q_ref[...], c2_, s2_).astype(BF16)
        ks[...] = (_rot_a(k_ref[...], c2_, s2_) * scale).astype(BF16)
        vs[...] = v_ref[...].astype(BF16)
        lg = lg_ref[0, :, 0:1]
        oo = o_ref[...]
        gg = g_ref[...]
        dya = dy_ref[...]
        r = lax.rsqrt(jnp.mean(oo * oo, -1, keepdims=True) + EPS)
        rn = oo * r
        dg_ref[...] = (dya * rn * _dsilu(gg)).astype(BF16)
        drn = dya * _silu(gg)
        dos[...] = (r * (drn - rn * jnp.mean(drn * rn, -1, keepdims=True))).astype(BF16)
        dka[...] = jnp.zeros_like(dka)
        dva[...] = jnp.zeros_like(dva)
        for i in range(nblk):
            rows = pl.ds(i * ATT_BLK, ATT_BLK)
            qi = qs[rows, :]
            doi = dos[rows, :]
            dqa = jnp.zeros((ATT_BLK, RET_DK), F32)
            for j in range(i + 1):
                sl = pl.ds(j * ATT_BLK, ATT_BLK)
                dt_ = _decay_tile(lg, i - j)
                kj = ks[sl, :]
                s = (_nt(qi, kj) * dt_).astype(BF16)
                ds = (_nt(doi, vs[sl, :]) * dt_).astype(BF16)
                dqa = dqa + _nn(ds, kj)
                dka[sl, :] += _tn(ds, qi)
                dva[sl, :] += _tn(s, doi)
            dq_ref[rows, :] = _rot_a_t(dqa, c_ref[rows, :], s_ref[rows, :]).astype(BF16)
        dk_ref[...] = (_rot_a_t(dka[...], c2_, s2_) * scale).astype(BF16)
        dv_ref[...] = dva[...].astype(BF16)

    q, k, v, g, tab, lg = _ret_specs(t)
    blk_v = pl.BlockSpec((t, RET_DV), lambda hh: (0, hh))
    blk_k = pl.BlockSpec((t, RET_DK), lambda hh: (0, hh))
    return pl.pallas_call(
        body, grid=(RET_HEADS,), in_specs=[q, k, v, g, tab, tab, lg, blk_v, blk_v],
        out_specs=[blk_k, blk_k, blk_v, blk_v],
        out_shape=[jax.ShapeDtypeStruct((t, RET_QK_W), BF16), jax.ShapeDtypeStruct((t, RET_QK_W), BF16),
                   jax.ShapeDtypeStruct((t, RET_V_W), BF16), jax.ShapeDtypeStruct((t, RET_V_W), BF16)],
        scratch_shapes=[pltpu.VMEM((t, RET_DK), BF16), pltpu.VMEM((t, RET_DK), BF16), pltpu.VMEM((t, RET_DV), BF16),
                        pltpu.VMEM((t, RET_DV), BF16), pltpu.VMEM((t, RET_DK), F32), pltpu.VMEM((t, RET_DV), F32)],
        name=name, compiler_params=_cp())(h, h, h, h, c2, s2, lgt, o, dy)


def _rot_b(x, cb, shi, slo):
    return x * cb + pltpu.roll(x, ROPE_DIMS // 2, 1) * shi + pltpu.roll(x, LANES - ROPE_DIMS // 2, 1) * slo


def _rot_b_t(dy, cb, shi, slo):
    return dy * cb + pltpu.roll(dy * shi, LANES - ROPE_DIMS // 2, 1) + pltpu.roll(dy * slo, ROPE_DIMS // 2, 1)


def _dil_specs(t):
    base = (2 * RET_QK_W + 2 * RET_V_W) // LANES
    npair = DIL_W // LANES
    q = pl.BlockSpec((t, LANES), lambda p: (0, base + p))
    k = pl.BlockSpec((t, LANES), lambda p: (0, base + npair + p))
    v = pl.BlockSpec((t, LANES), lambda p: (0, base + 2 * npair + p))
    tab = pl.BlockSpec((t, LANES), lambda p: (0, 0))
    strip = pl.BlockSpec((ATT_BLK, t), lambda p: (0, 0))
    pair = pl.BlockSpec((t, LANES), lambda p: (0, p))
    return q, k, v, tab, strip, pair


def _dil_fwd(h, cb, shi, slo, strip, *, name):
    t = h.shape[0]
    nblk = t // ATT_BLK
    scale = DIL_HD ** -0.5

    def body(q_ref, k_ref, v_ref, cb_ref, shi_ref, slo_ref, st_ref, o_ref, yb_ref, lse_ref, qs, ks, vs):
        cb_, shi_, slo_ = cb_ref[...], shi_ref[...], slo_ref[...]
        lane = lax.broadcasted_iota(jnp.int32, (t, LANES), 1)
        qr = _rot_b(q_ref[...], cb_, shi_, slo_) * scale
        qs[0] = jnp.where(lane < DIL_HD, qr, 0.0).astype(BF16)
        qs[1] = jnp.where(lane >= DIL_HD, qr, 0.0).astype(BF16)
        ks[...] = _rot_b(k_ref[...], cb_, shi_, slo_).astype(BF16)
        vs[...] = v_ref[...].astype(BF16)
        lane_b = lax.broadcasted_iota(jnp.int32, (ATT_BLK, LANES), 1)
        for i in range(nblk):
            w = (i + 1) * ATT_BLK
            rows = pl.ds(i * ATT_BLK, ATT_BLK)
            logc = st_ref[:, t - w:t]
            outs, lses = [], []
            for hd in range(2):
                s = _nt(qs[hd, rows, :], ks[0:w, :]) + logc
                m = jnp.max(s, -1, keepdims=True)
                p = jnp.exp(s - m)
                l = jnp.sum(p, -1, keepdims=True)
                outs.append(_nn(p.astype(BF16), vs[0:w, :]) / l)
                lses.append(m + jnp.log(l))
            o = jnp.where(lane_b < DIL_HD, outs[0], outs[1])
            o_ref[rows, :] = o
            yb_ref[rows, :] = o.astype(BF16)
            lse_ref[rows, :] = jnp.where(lane_b < DIL_HD, lses[0], lses[1])

    q, k, v, tab, strip_spec, pair = _dil_specs(t)
    return pl.pallas_call(
        body, grid=(DIL_W // LANES,), in_specs=[q, k, v, tab, tab, tab, strip_spec], out_specs=[pair, pair, pair],
        out_shape=[jax.ShapeDtypeStruct((t, DIL_W), F32), jax.ShapeDtypeStruct((t, DIL_W), BF16),
                   jax.ShapeDtypeStruct((t, DIL_W), F32)],
        scratch_shapes=[pltpu.VMEM((2, t, LANES), BF16), pltpu.VMEM((t, LANES), BF16), pltpu.VMEM((t, LANES), BF16)],
        name=name, compiler_params=_cp())(h, h, h, cb, shi, slo, strip)


def _dil_bwd(h, cb, shi, slo, strip, o, lse, dy, *, name):
    t = h.shape[0]
    nblk = t // ATT_BLK
    scale = DIL_HD ** -0.5

    def body(q_ref, k_ref, v_ref, cb_ref, shi_ref, slo_ref, st_ref, o_ref, lse_ref, dy_ref,
             dq_ref, dk_ref, dv_ref, qs, ks, vs, dos, dls, dka, dva):
        cb_, shi_, slo_ = cb_ref[...], shi_ref[...], slo_ref[...]
        lane = lax.broadcasted_iota(jnp.int32, (t, LANES), 1)
        qr = _rot_b(q_ref[...], cb_, shi_, slo_) * scale
        qs[0] = jnp.where(lane < DIL_HD, qr, 0.0).astype(BF16)
        qs[1] = jnp.where(lane >= DIL_HD, qr, 0.0).astype(BF16)
        ks[...] = _rot_b(k_ref[...], cb_, shi_, slo_).astype(BF16)
        vs[...] = v_ref[...].astype(BF16)
        do = dy_ref[...]
        prod = do * o_ref[...]
        d0 = jnp.sum(jnp.where(lane < DIL_HD, prod, 0.0), -1, keepdims=True)
        d1 = jnp.sum(jnp.where(lane >= DIL_HD, prod, 0.0), -1, keepdims=True)
        dls[...] = jnp.where(lane < DIL_HD, d0, d1)
        dos[0] = jnp.where(lane < DIL_HD, do, 0.0).astype(BF16)
        dos[1] = jnp.where(lane >= DIL_HD, do, 0.0).astype(BF16)
        dka[...] = jnp.zeros_like(dka)
        dva[...] = jnp.zeros_like(dva)
        lane_b = lax.broadcasted_iota(jnp.int32, (ATT_BLK, LANES), 1)
        for i in range(nblk):
            w = (i + 1) * ATT_BLK
            rows = pl.ds(i * ATT_BLK, ATT_BLK)
            logc = st_ref[:, t - w:t]
            dqs = []
            for hd in range(2):
                col = hd * DIL_HD
                qh = qs[hd, rows, :]
                doh = dos[hd, rows, :]
                lse_h = lse_ref[rows, col:col + 1]
                dl_h = dls[rows, col:col + 1]
                p = jnp.exp(_nt(qh, ks[0:w, :]) + logc - lse_h)
                dp = _nt(doh, vs[0:w, :])
                ds = (p * (dp - dl_h)).astype(BF16)
                dqs.append(_nn(ds, ks[0:w, :]))
                dka[0:w, :] += _tn(ds, qh)
                dva[0:w, :] += _tn(p.astype(BF16), doh)
            dq = jnp.where(lane_b < DIL_HD, dqs[0], dqs[1]) * scale
            dq_ref[rows, :] = _rot_b_t(dq, cb_ref[rows, :], shi_ref[rows, :], slo_ref[rows, :]).astype(BF16)
        dk_ref[...] = _rot_b_t(dka[...], cb_, shi_, slo_).astype(BF16)
        dv_ref[...] = dva[...].astype(BF16)

    q, k, v, tab, strip_spec, pair = _dil_specs(t)
    dy_spec = pl.BlockSpec((t, LANES), lambda p: (0, RET_V_W // LANES + p))
    return pl.pallas_call(
        body, grid=(DIL_W // LANES,), in_specs=[q, k, v, tab, tab, tab, strip_spec, pair, pair, dy_spec],
        out_specs=[pair, pair, pair],
        out_shape=[jax.ShapeDtypeStruct((t, DIL_W), BF16)] * 3,
        scratch_shapes=[pltpu.VMEM((2, t, LANES), BF16), pltpu.VMEM((t, LANES), BF16), pltpu.VMEM((t, LANES), BF16),
                        pltpu.VMEM((2, t, LANES), BF16), pltpu.VMEM((t, LANES), F32),
                        pltpu.VMEM((t, LANES), F32), pltpu.VMEM((t, LANES), F32)],
        name=name, compiler_params=_cp())(h, h, h, cb, shi, slo, strip, o, lse, dy)


def _gdn_prep_fwd(h, cw, *, name):
    t = h.shape[0]
    qscale = GDN_DK ** -0.5

    def body(hq_ref, hk_ref, hv_ref, wq_ref, wk_ref, wv_ref, q_ref, k_ref, v_ref):
        row = lax.broadcasted_iota(jnp.int32, (t, GDN_DK), 0)
        sq = _silu(_dwconv(hq_ref[...], wq_ref, row))
        sk = _silu(_dwconv(hk_ref[...], wk_ref, row))
        q_ref[0] = sq * lax.rsqrt(jnp.sum(sq * sq, -1, keepdims=True) + 1e-6) * qscale
        k_ref[0] = sk * lax.rsqrt(jnp.sum(sk * sk, -1, keepdims=True) + 1e-6)
        v_ref[0] = _silu(_dwconv(hv_ref[...], wv_ref, row))

    hs = lambda off: pl.BlockSpec((t, GDN_DK), lambda i: (0, i + off))
    ws = lambda off: pl.BlockSpec((GDN_CONV, GDN_DK), lambda i: (0, i + off))
    out = pl.BlockSpec((1, t, GDN_DK), lambda i: (i, 0, 0))
    return pl.pallas_call(
        body, grid=(GDN_HEADS,), in_specs=[hs(0), hs(8), hs(16), ws(0), ws(8), ws(16)], out_specs=[out, out, out],
        out_shape=[jax.ShapeDtypeStruct((GDN_HEADS, t, GDN_DK), F32)] * 3,
        name=name, compiler_params=_cp())(h, h, h, cw, cw, cw)


def _gdn_prep_bwd(h, cw, dq, dk, dv, *, name):
    t = h.shape[0]
    qscale = GDN_DK ** -0.5

    def body(hq_ref, hk_ref, hv_ref, wq_ref, wk_ref, wv_ref, dq_ref, dk_ref, dv_ref,
             dhq_ref, dhk_ref, dhv_ref, dwq_ref, dwk_ref, dwv_ref):
        row = lax.broadcasted_iota(jnp.int32, (t, GDN_DK), 0)

        def one(h_ref, w_ref, d_ref, dh_ref, dw_ref, norm, sc):
            u = h_ref[...]
            c = _dwconv(u, w_ref, row)
            d = d_ref[0]
            if norm:
                s = _silu(c)
                r = lax.rsqrt(jnp.sum(s * s, -1, keepdims=True) + 1e-6)
                n = s * r
                d = d * sc
                d = r * (d - n * jnp.sum(d * n, -1, keepdims=True))
            dc = d * _dsilu(c)
            dh_ref[...] = _dwconv_bwd(u, w_ref, dc, row, dw_ref).astype(BF16)

        one(hq_ref, wq_ref, dq_ref, dhq_ref, dwq_ref, True, qscale)
        one(hk_ref, wk_ref, dk_ref, dhk_ref, dwk_ref, True, 1.0)
        one(hv_ref, wv_ref, dv_ref, dhv_ref, dwv_ref, False, 1.0)

    hs = lambda off: pl.BlockSpec((t, GDN_DK), lambda i: (0, i + off))
    ws = lambda off: pl.BlockSpec((GDN_CONV, GDN_DK), lambda i: (0, i + off))
    hd = pl.BlockSpec((1, t, GDN_DK), lambda i: (i, 0, 0))
    return pl.pallas_call(
        body, grid=(GDN_HEADS,), in_specs=[hs(0), hs(8), hs(16), ws(0), ws(8), ws(16), hd, hd, hd],
        out_specs=[hs(0), hs(0), hs(0), ws(0), ws(0), ws(0)],
        out_shape=[jax.ShapeDtypeStruct((t, GDN_W), BF16)] * 3 + [jax.ShapeDtypeStruct((GDN_CONV, GDN_W), F32)] * 3,
        name=name, compiler_params=_cp())(h, h, h, cw, cw, cw, dq, dk, dv)


def _make_mm2(wide):
    def raw(a, b, dims):
        if wide:
            return lax.dot_general(a, b, (dims, ((), ())), precision=lax.Precision.HIGHEST, preferred_element_type=F32)
        return lax.dot_general(a.astype(BF16), b.astype(BF16), (dims, ((), ())), preferred_element_type=F32)

    @jax.custom_vjp
    def nn(a, b):
        return raw(a, b, ((1,), (0,)))

    @jax.custom_vjp
    def nt(a, b):
        return raw(a, b, ((1,), (1,)))

    @jax.custom_vjp
    def tn(a, b):
        return raw(a, b, ((0,), (0,)))

    nn.defvjp(lambda a, b: (nn(a, b), (a, b)), lambda r, g: (nt(g, r[1]), tn(r[0], g)))
    nt.defvjp(lambda a, b: (nt(a, b), (a, b)), lambda r, g: (nn(g, r[1]), tn(g, r[0])))
    tn.defvjp(lambda a, b: (tn(a, b), (a, b)), lambda r, g: (nt(r[1], g), nn(r[0], g)))
    return nn, nt, tn


_NN, _NT, _TN = _make_mm2(False)
_NNW, _NTW, _TNW = _make_mm2(True)


def _square_masks(c):
    ri = lax.broadcasted_iota(jnp.int32, (c, c), 0)
    ci = lax.broadcasted_iota(jnp.int32, (c, c), 1)
    return ri >= ci, ri > ci, ri == ci


def _cumsum_rows(m):
    tri, _, _ = _square_masks(m.shape[0])
    return _NNW(tri.astype(F32), m)


def _transpose_sq(m):
    _, _, eye = _square_masks(m.shape[0])
    return _NTW(eye.astype(F32), m)


@jax.custom_vjp
def _inv_unit_lower(l):
    c = l.shape[0]
    _, _, eye = _square_masks(c)
    p = -l
    t = eye.astype(F32) + p
    for _ in range(int(math.log2(c)) - 1):
        p = _NNW(p, p)
        t = t + _NNW(t, p)
    return t


def _inv_fwd(l):
    t = _inv_unit_lower(l)
    return t, t


def _inv_bwd(t, dt):
    return (-_NTW(_TNW(t, dt), t),)


_inv_unit_lower.defvjp(_inv_fwd, _inv_bwd)


def _softplus(x):
    return jnp.maximum(x, 0.0) + jnp.log1p(jnp.exp(-jnp.abs(x)))


def _gdn_chunk(q, k, v, braw, araw, alog, dtb, state):
    c = q.shape[0]
    dv = v.shape[1]
    tri, strict, _ = _square_masks(c)
    beta = _sig(braw)
    g = -jnp.exp(alog) * _softplus(araw + dtb)
    gcm = _cumsum_rows(g * jnp.ones((c, c), F32))
    gct = _transpose_sq(gcm)
    decay = jnp.where(tri, jnp.exp(jnp.where(tri, gcm - gct, 0.0)), 0.0)
    gc = jnp.sum(gcm, 1, keepdims=True) * (1.0 / c)
    glast = jnp.sum(g, 0, keepdims=True)
    egc = jnp.exp(gc)
    kb = k * beta
    tm = _inv_unit_lower(jnp.where(strict, _NT(kb, k) * decay, 0.0))
    sol = _NNW(tm, jnp.concatenate([v * beta, kb * egc], 1))
    u, w = sol[:, :dv], sol[:, dv:]
    attn = jnp.where(tri, _NT(q, k) * decay, 0.0)
    k_dec = k * jnp.exp(glast - gc)
    q_dec = q * egc
    v_new = u - _NN(w, state)
    o = _NN(q_dec, state) + _NN(attn, v_new)
    new_state = state * jnp.exp(glast) + _TN(k_dec, v_new)
    return o, new_state


def _gdn_specs(t, rev):
    nch = t // GDN_CHUNK
    cm = (lambda n: nch - 1 - n) if rev else (lambda n: n)
    tok = pl.BlockSpec((GDN_HEADS, GDN_CHUNK, GDN_DK), lambda n: (0, cm(n), 0))
    par = pl.BlockSpec((GDN_HEADS, 1, LANES), lambda n: (0, 0, 0))
    st = pl.BlockSpec((GDN_HEADS, 1, GDN_DK, GDN_DV), lambda n: (0, cm(n), 0, 0))
    return tok, par, st


def _gdn_core_fwd(q, k, v, bb, ab, alog, dtb, *, name):
    t = q.shape[1]
    nch = t // GDN_CHUNK

    def body(q_ref, k_ref, v_ref, bb_ref, ab_ref, al_ref, dt_ref, o_ref, st_ref, state):
        @pl.when(pl.program_id(0) == 0)
        def _():
            state[...] = jnp.zeros_like(state)

        s0 = state[...]
        st_ref[:, 0] = s0
        o, s1 = jax.vmap(_gdn_chunk)(q_ref[...], k_ref[...], v_ref[...], bb_ref[:, :, 0:1], ab_ref[:, :, 0:1],
                                     al_ref[:, :, 0:1], dt_ref[:, :, 0:1], s0)
        o_ref[...] = o
        state[...] = s1

    tok, par, st = _gdn_specs(t, False)
    return pl.pallas_call(
        body, grid=(nch,), in_specs=[tok, tok, tok, tok, tok, par, par], out_specs=[tok, st],
        out_shape=[jax.ShapeDtypeStruct((GDN_HEADS, t, GDN_DV), F32),
                   jax.ShapeDtypeStruct((GDN_HEADS, nch, GDN_DK, GDN_DV), F32)],
        scratch_shapes=[pltpu.VMEM((GDN_HEADS, GDN_DK, GDN_DV), F32)],
        name=name, compiler_params=_cp())(q, k, v, bb, ab, alog, dtb)


def _gdn_core_bwd(q, k, v, bb, ab, alog, dtb, states, do, *, name):
    t = q.shape[1]
    nch = t // GDN_CHUNK

    def body(q_ref, k_ref, v_ref, bb_ref, ab_ref, al_ref, dt_ref, st_ref, do_ref,
             dq_ref, dk_ref, dv_ref, dbb_ref, dab_ref, dal_ref, ddt_ref, dstate):
        @pl.when(pl.program_id(0) == 0)
        def _():
            dstate[...] = jnp.zeros_like(dstate)
            dal_ref[...] = jnp.zeros_like(dal_ref)
            ddt_ref[...] = jnp.zeros_like(ddt_ref)

        args = (q_ref[...], k_ref[...], v_ref[...], bb_ref[:, :, 0:1], ab_ref[:, :, 0:1],
                al_ref[:, :, 0:1], dt_ref[:, :, 0:1], st_ref[:, 0])
        _, pull = jax.vjp(jax.vmap(_gdn_chunk), *args)
        dq, dk, dv, dbr, dar, dal, ddt, ds = pull((do_ref[...], dstate[...]))
        dq_ref[...] = dq
        dk_ref[...] = dk
        dv_ref[...] = dv
        dbb_ref[...] = dbr + jnp.zeros((GDN_HEADS, GDN_CHUNK, LANES), F32)
        dab_ref[...] = dar + jnp.zeros((GDN_HEADS, GDN_CHUNK, LANES), F32)
        dal_ref[...] += dal + jnp.zeros((GDN_HEADS, 1, LANES), F32)
        ddt_ref[...] += ddt + jnp.zeros((GDN_HEADS, 1, LANES), F32)
        dstate[...] = ds

    tok, par, st = _gdn_specs(t, True)
    tokshape = jax.ShapeDtypeStruct((GDN_HEADS, t, GDN_DK), F32)
    parshape = jax.ShapeDtypeStruct((GDN_HEADS, 1, LANES), F32)
    return pl.pallas_call(
        body, grid=(nch,), in_specs=[tok, tok, tok, tok, tok, par, par, st, tok],
        out_specs=[tok, tok, tok, tok, tok, par, par],
        out_shape=[tokshape] * 5 + [parshape] * 2,
        scratch_shapes=[pltpu.VMEM((GDN_HEADS, GDN_DK, GDN_DV), F32)],
        name=name, compiler_params=_cp())(q, k, v, bb, ab, alog, dtb, states, do)


GDN_ROWS = 512


def _gdn_post_fwd(o, h, nw, *, name):
    t = o.shape[1]

    def body(o_ref, g_ref, nw_ref, y_ref):
        oo = o_ref[0]
        r = lax.rsqrt(jnp.mean(oo * oo, -1, keepdims=True) + EPS)
        y_ref[...] = (oo * r * nw_ref[...] * _silu(g_ref[...])).astype(BF16)

    return pl.pallas_call(
        body, grid=(GDN_HEADS, t // GDN_ROWS),
        in_specs=[pl.BlockSpec((1, GDN_ROWS, GDN_DV), lambda hh, i: (hh, i, 0)),
                  pl.BlockSpec((GDN_ROWS, GDN_DV), lambda hh, i: (i, 3 * GDN_HEADS + hh)),
                  pl.BlockSpec((1, GDN_DV), lambda hh, i: (0, 0))],
        out_specs=pl.BlockSpec((GDN_ROWS, GDN_DV), lambda hh, i: (i, hh)),
        out_shape=jax.ShapeDtypeStruct((t, GDN_W), BF16), name=name, compiler_params=_cp())(o, h, nw)


def _gdn_post_bwd(o, h, nw, dy, *, name):
    t = o.shape[1]

    def body(o_ref, g_ref, nw_ref, dy_ref, do_ref, dg_ref, dnw_ref):
        oo, gg, nw_, dy_ = o_ref[0], g_ref[...], nw_ref[...], dy_ref[...]
        r = lax.rsqrt(jnp.mean(oo * oo, -1, keepdims=True) + EPS)
        n = oo * r
        sg = _silu(gg)
        dg_ref[...] = (dy_ * n * nw_ * _dsilu(gg)).astype(BF16)
        dn = dy_ * sg * nw_
        do_ref[0] = r * (dn - n * jnp.mean(dn * n, -1, keepdims=True))

        @pl.when((pl.program_id(0) == 0) & (pl.program_id(1) == 0))
        def _():
            dnw_ref[...] = jnp.zeros_like(dnw_ref)

        dnw_ref[...] += jnp.sum(dy_ * sg * n, 0, keepdims=True)

    return pl.pallas_call(
        body, grid=(GDN_HEADS, t // GDN_ROWS),
        in_specs=[pl.BlockSpec((1, GDN_ROWS, GDN_DV), lambda hh, i: (hh, i, 0)),
                  pl.BlockSpec((GDN_ROWS, GDN_DV), lambda hh, i: (i, 3 * GDN_HEADS + hh)),
                  pl.BlockSpec((1, GDN_DV), lambda hh, i: (0, 0)),
                  pl.BlockSpec((GDN_ROWS, GDN_DV), lambda hh, i: (i, hh))],
        out_specs=[pl.BlockSpec((1, GDN_ROWS, GDN_DV), lambda hh, i: (hh, i, 0)),
                   pl.BlockSpec((GDN_ROWS, GDN_DV), lambda hh, i: (i, hh)),
                   pl.BlockSpec((1, GDN_DV), lambda hh, i: (0, 0))],
        out_shape=[jax.ShapeDtypeStruct((GDN_HEADS, t, GDN_DV), F32), jax.ShapeDtypeStruct((t, GDN_W), BF16),
                   jax.ShapeDtypeStruct((1, GDN_DV), F32)],
        name=name, compiler_params=_cp())(o, h, nw, dy)


def _tables(positions):
    pos = positions.astype(F32)[:, None]
    half = RET_DK // 2
    inv = jnp.power(RET_THETA, -jnp.arange(half, dtype=F32) * 2.0 / RET_DK)
    ang = pos * inv
    cos, sin = jnp.cos(ang), jnp.sin(ang)
    c2a = jnp.concatenate([cos, cos], 1)
    s2a = jnp.concatenate([-sin, sin], 1)
    hb = ROPE_DIMS // 2
    invb = jnp.power(ROPE_THETA, -jnp.arange(hb, dtype=F32) * 2.0 / ROPE_DIMS)
    angb = pos * invb
    cosb, sinb = jnp.cos(angb), jnp.sin(angb)
    t = pos.shape[0]
    ones = jnp.ones((t, DIL_HD - ROPE_DIMS), F32)
    zeros = jnp.zeros((t, DIL_HD - ROPE_DIMS), F32)
    z8 = jnp.zeros((t, hb), F32)
    cb = jnp.concatenate([cosb, cosb, ones] * 2, 1)
    shi = jnp.concatenate([z8, sinb, zeros] * 2, 1)
    slo = jnp.concatenate([-sinb, z8, zeros] * 2, 1)
    lg = jnp.log1p(-jnp.power(2.0, -5.0 - jnp.arange(RET_HEADS, dtype=F32)))
    lgt = jnp.broadcast_to(lg[:, None, None], (RET_HEADS, 1, LANES))
    delta = jnp.arange(ATT_BLK, dtype=jnp.int32)[:, None] + (SEQ - ATT_BLK) - jnp.arange(SEQ, dtype=jnp.int32)[None, :]
    cnt = jnp.zeros(delta.shape, F32)
    for (w, d) in DIL_PAIRS:
        cnt = cnt + ((delta >= 0) & (delta <= w) & (delta % d == 0)).astype(F32)
    strip = jnp.where(cnt > 0, jnp.log(jnp.maximum(cnt, 1.0)), NEG)
    return c2a, s2a, cb, shi, slo, lgt, strip


def _local_step(x, positions, target, get_w, put_g, small):
    c2a, s2a, cb, shi, slo, lgt, strip = _tables(positions)
    t = x.shape[0]
    saved = []
    xf = x
    xb = x.astype(BF16)
    for layer in range(DEPTH):
        j = layer // 2
        L = f"L{layer}_"
        W, dep = get_w(layer, "mixer", xb)
        rec = {"x": xf, "xb": xb}
        if layer % 2 == 0:
            h = _mm(xb, W["in_t"], tb=True, name=L + "ev_in", dep=dep)
            ro, ya = _ret_fwd(h, c2a, s2a, lgt, name=L + "ret_fwd")
            do_, yb, lse = _dil_fwd(h, cb, shi, slo, strip, name=L + "dil_fwd")
            y = jnp.concatenate([ya, yb], 1)
            mix = _mm(y, W["out"], name=L + "ev_out")
            rec.update(h=h, ro=ro, dil_o=do_, lse=lse, y=y)
        else:
            h = _mm(xb, W["in_t"], tb=True, name=L + "od_in", dep=dep)
            cw = W["conv"]
            q, k, v = _gdn_prep_fwd(h, cw, name=L + "gdn_prep")
            hs = h[:, 4 * GDN_W:4 * GDN_W + 2 * GDN_HEADS]
            bb = jnp.broadcast_to(hs[:, :GDN_HEADS].T[:, :, None], (GDN_HEADS, t, LANES))
            ab = jnp.broadcast_to(hs[:, GDN_HEADS:].T[:, :, None], (GDN_HEADS, t, LANES))
            alog = jnp.broadcast_to(small["od_a_log"][j][:, None, None], (GDN_HEADS, 1, LANES))
            dtb = jnp.broadcast_to(small["od_dt_bias"][j][:, None, None], (GDN_HEADS, 1, LANES))
            o, states = _gdn_core_fwd(q, k, v, bb, ab, alog, dtb, name=L + "gdn_fwd")
            nw = small["od_norm_w"][j][None, :]
            y = _gdn_post_fwd(o, h, nw, name=L + "gdn_post")
            mix = _mm(y, W["out"], name=L + "od_out")
            rec.update(h=h, q=q, k=k, v=v, bb=bb, ab=ab, alog=alog, dtb=dtb, states=states, o=o, y=y, nw=nw, cw=cw)
        z1, x1, x1b = _ln_fwd(xf, mix, small["ln1_g"][layer][None], small["ln1_b"][layer][None], name=L + "ln1")
        rec["Wm"] = W
        W, dep = get_w(layer, "ffn", x1b)
        rec["Wf"] = W
        u = _mm(x1b, W["up_t"], tb=True, name=L + "ffn_up", dep=dep)
        fcw = W["fconv"]
        fcb = small["ffn_conv_b"][layer][None]
        a = _ffn_mid_fwd(u, fcw, fcb, name=L + "ffn_mid")
        f = _mm(a, W["down"], name=L + "ffn_down")
        z2, x2, x2b = _ln_fwd(x1, f, small["ln2_g"][layer][None], small["ln2_b"][layer][None], name=L + "ln2")
        rec.update(z1=z1, x1b=x1b, u=u, a=a, z2=z2, fcw=fcw, fcb=fcb)
        saved.append(rec)
        xf, xb = x2, x2b

    dy, lossv = _loss_head(xf, target, name="loss_head")
    loss = lossv[0, 0]

    gS = {n: [None] * small[n].shape[0] for n in small}
    dres, dmm = dy, None
    for layer in reversed(range(DEPTH)):
        j = layer // 2
        L = f"L{layer}_"
        rec = saved[layer]
        Wm, Wf = rec["Wm"], rec["Wf"]
        g = {}
        if dmm is None:
            dz2, dz2b, dg2, db2 = _ln_bwd(rec["z2"], small["ln2_g"][layer][None], dres, None, name=L + "ln2_bwd")
        else:
            dz2, dz2b, dg2, db2 = _ln_bwd(rec["z2"], small["ln2_g"][layer][None], dmm, dres, name=L + "ln2_bwd")
        gS["ln2_g"][layer], gS["ln2_b"][layer] = dg2[0], db2[0]
        g["down"] = _mm(rec["a"], dz2b, ta=True, name=L + "ffn_down_dw", out_dtype=BF16)
        da = _mm(dz2b, Wf["down"], tb=True, name=L + "ffn_down_dx")
        du, dcw, dcb = _ffn_mid_bwd(rec["u"], rec["fcw"], rec["fcb"], da, name=L + "ffn_mid_bwd")
        g["fconv"] = dcw.astype(BF16)
        gS["ffn_conv_b"][layer] = dcb[0]
        g["up_t"] = _mm(du, rec["x1b"], ta=True, name=L + "ffn_up_dw", out_dtype=BF16)
        dep = put_g(layer, "ffn", g)
        dx1 = _mm(du, Wf["up_t"], name=L + "ffn_up_dx", dep=dep)
        dz1, dz1b, dg1, db1 = _ln_bwd(rec["z1"], small["ln1_g"][layer][None], dx1, dz2, name=L + "ln1_bwd")
        gS["ln1_g"][layer], gS["ln1_b"][layer] = dg1[0], db1[0]
        g = {}
        if layer % 2 == 0:
            g["out"] = _mm(rec["y"], dz1b, ta=True, name=L + "ev_out_dw", out_dtype=BF16)
            dyy = _mm(dz1b, Wm["out"], tb=True, name=L + "ev_out_dx")
            dqa, dka, dva, dga = _ret_bwd(rec["h"], c2a, s2a, lgt, rec["ro"], dyy, name=L + "ret_bwd")
            dqb, dkb, dvb = _dil_bwd(rec["h"], cb, shi, slo, strip, rec["dil_o"], rec["lse"], dyy, name=L + "dil_bwd")
            dh = jnp.concatenate([dqa, dka, dva, dga, dqb, dkb, dvb], 1)
            g["in_t"] = _mm(dh, rec["xb"], ta=True, name=L + "ev_in_dw", out_dtype=BF16)
            dep = put_g(layer, "mixer", g)
            dxin = _mm(dh, Wm["in_t"], name=L + "ev_in_dx", dep=dep)
        else:
            g["out"] = _mm(rec["y"], dz1b, ta=True, name=L + "od_out_dw", out_dtype=BF16)
            dyy = _mm(dz1b, Wm["out"], tb=True, name=L + "od_out_dx")
            do, dgate, dnw = _gdn_post_bwd(rec["o"], rec["h"], rec["nw"], dyy, name=L + "gdn_post_bwd")
            gS["od_norm_w"][j] = dnw[0]
            dq, dk, dv, dbb, dab, dal, ddt = _gdn_core_bwd(
                rec["q"], rec["k"], rec["v"], rec["bb"], rec["ab"], rec["alog"], rec["dtb"], rec["states"], do,
                name=L + "gdn_bwd")
            gS["od_a_log"][j] = dal[:, 0, 0]
            gS["od_dt_bias"][j] = ddt[:, 0, 0]
            dhq, dhk, dhv, dwq, dwk, dwv = _gdn_prep_bwd(rec["h"], rec["cw"], dq, dk, dv, name=L + "gdn_prep_bwd")
            g["conv"] = jnp.concatenate([dwq, dwk, dwv], 1).astype(BF16)
            dsm = jnp.concatenate([dbb[:, :, 0].T, dab[:, :, 0].T,
                                   jnp.zeros((t, LANES - 2 * GDN_HEADS), F32)], 1).astype(BF16)
            dh = jnp.concatenate([dhq, dhk, dhv, dgate, dsm], 1)
            g["in_t"] = _mm(dh, rec["xb"], ta=True, name=L + "od_in_dw", out_dtype=BF16)
            dep = put_g(layer, "mixer", g)
            dxin = _mm(dh, Wm["in_t"], name=L + "od_in_dx", dep=dep)
        dres, dmm = dz1, dxin
    grad_x = _axpy(dmm, dres, name="grad_x")
    gS = {n: jnp.stack(v) for n, v in gS.items()}
    return loss, grad_x, gS


HBM = pl.BlockSpec(memory_space=pltpu.HBM)


def _me():
    return lax.axis_index("x"), lax.axis_index("y"), lax.axis_index("c")


def _my_index():
    x, y, c = _me()
    return 4 * x + 2 * y + c


SEM = pl.BlockSpec(memory_space=pltpu.SEMAPHORE)
ANY = pl.BlockSpec(memory_space=pl.ANY)
N_PEERS = N_DEV - 1


def _peer(kk):
    x, y, c = _me()
    return x ^ (kk >> 2), y ^ ((kk >> 1) & 1), c ^ (kk & 1)


def _exchange_copies(mode, srcs, lands, send_sems, recv_sems, incoming):
    myid = _my_index()
    out = []
    for a in range(len(srcs)):
        for kk in range(1, N_DEV):
            px, py, pc = _peer(kk)
            pid = 4 * px + 2 * py + pc
            src = srcs[a] if mode == "gather" else srcs[a].at[pid]
            out.append(pltpu.make_async_remote_copy(
                src_ref=src, dst_ref=lands[a].at[pid if incoming else myid],
                send_sem=send_sems.at[a * N_PEERS + kk - 1], recv_sem=recv_sems.at[a * N_PEERS + kk - 1],
                device_id=(px, py, pc), device_id_type=MESH))
    return out


def _exchange_start(srcs, mode, after, *, name):
    k = len(srcs)
    land_shapes = [((N_DEV, *s.shape) if mode == "gather" else s.shape) for s in srcs]

    def body(*refs):
        src, land = refs[:k], refs[k:2 * k]
        o = 2 * k + (0 if after is None else 1)
        send_sems, recv_sems = refs[o], refs[o + 1]
        token = refs[o + 2 + 2 * k]
        for cp in _exchange_copies(mode, src, land, send_sems, recv_sems, False):
            cp.start()
        token[...] = jnp.zeros_like(token)

    ins = [pltpu.with_memory_space_constraint(s, pltpu.HBM) for s in srcs]
    ins += [pltpu.with_memory_space_constraint(lax.empty(shp, s.dtype), pltpu.HBM) for shp, s in zip(land_shapes, srcs)]
    outs = pl.pallas_call(
        body, name=name,
        out_shape=(pltpu.SemaphoreType.DMA((k * N_PEERS,)), pltpu.SemaphoreType.DMA((k * N_PEERS,)),
                   *[pltpu.HBM(s.shape, s.dtype) for s in srcs],
                   *[pltpu.HBM(shp, s.dtype) for shp, s in zip(land_shapes, srcs)],
                   jax.ShapeDtypeStruct((8, LANES), F32)),
        in_specs=[HBM] * (2 * k) + ([] if after is None else [ANY]),
        out_specs=(SEM, SEM, *[HBM] * (2 * k), pl.BlockSpec(memory_space=pltpu.VMEM)),
        input_output_aliases={i: 2 + i for i in range(2 * k)},
        compiler_params=pltpu.CompilerParams(has_side_effects=pltpu.SideEffectType.DATAFLOW_SIDE_EFFECTING),
    )(*ins, *([] if after is None else [after]))
    return outs[0], outs[1], list(outs[2:2 + k]), list(outs[2 + k:2 + 2 * k]), outs[2 + 2 * k]


def _exchange_wait(started, mode, after, *, name):
    send_sems, recv_sems, srcs, lands, _ = started
    k = len(srcs)

    def body(*refs):
        src, land = refs[:k], refs[k:2 * k]
        s_sems, r_sems = refs[2 * k], refs[2 * k + 1]
        for cp in _exchange_copies(mode, src, land, s_sems, r_sems, True):
            cp.wait_send()
            cp.wait_recv()

    outs = pl.pallas_call(
        body, name=name,
        out_shape=(*[pltpu.HBM(s.shape, s.dtype) for s in srcs], *[pltpu.HBM(l.shape, l.dtype) for l in lands]),
        in_specs=[HBM] * (2 * k) + [SEM, SEM, ANY], out_specs=tuple([HBM] * (2 * k)),
        input_output_aliases={i: i for i in range(2 * k)},
        compiler_params=pltpu.CompilerParams(has_side_effects=pltpu.SideEffectType.DATAFLOW_SIDE_EFFECTING),
    )(*srcs, *lands, send_sems, recv_sems, after)
    return list(outs[:k]), list(outs[k:])


def _sum8(land, *, name):
    _, rr, cc = land.shape
    tr = _row_tile(rr)

    def body(l_ref, o_ref):
        acc = l_ref[0].astype(F32)
        for d in range(1, N_DEV):
            acc = acc + l_ref[d].astype(F32)
        o_ref[...] = acc

    return pl.pallas_call(
        body, grid=(rr // tr,), in_specs=[pl.BlockSpec((N_DEV, tr, cc), lambda i: (0, i, 0))],
        out_specs=pl.BlockSpec((tr, cc), lambda i: (i, 0)), out_shape=jax.ShapeDtypeStruct((rr, cc), F32),
        name=name, compiler_params=_cp())(land)


def _all_gather(shards, *, name):
    n = len(shards)

    def body(*refs):
        ins, outs = refs[:n], refs[n:2 * n]
        send_sems, recv_sems, local_sems = refs[2 * n:]
        x, y, c = _me()
        me, sibling = (x, y, c), (x, y, 1 - c)
        chips = [(1 - x, y), (x, 1 - y), (1 - x, 1 - y)]

        def slot(out, px, py, pc):
            return out.at[4 * px + 2 * py + pc]

        def copy(a, kk, block, to, src=None):
            return pltpu.make_async_remote_copy(
                src_ref=slot(outs[a], *block) if src is None else src, dst_ref=slot(outs[a], *block),
                send_sem=send_sems.at[a, kk], recv_sem=recv_sems.at[a, kk], device_id=to, device_id_type=MESH)

        mine = [pltpu.make_async_copy(ins[a], slot(outs[a], *me), local_sems.at[a]) for a in range(n)]
        for cp in mine:
            cp.start()
        first = []
        for a in range(n):
            first.append(copy(a, 0, me, sibling, src=ins[a]))
            first += [copy(a, 1 + jj, me, (*chip, c), src=ins[a]) for jj, chip in enumerate(chips)]
        for cp in first:
            cp.start()
        passed = []
        for jj, chip in enumerate(chips):
            for a in range(n):
                copy(a, 1 + jj, (*chip, c), me).wait_recv()
                cp = copy(a, 4 + jj, (*chip, c), sibling)
                cp.start()
                passed.append(cp)
        for a in range(n):
            copy(a, 0, sibling, me).wait_recv()
            for jj, chip in enumerate(chips):
                copy(a, 4 + jj, (*chip, 1 - c), me).wait_recv()
        for cp in first + passed:
            cp.wait_send()
        for cp in mine:
            cp.wait()

    return pl.pallas_call(
        body, in_specs=[HBM] * n, out_specs=[HBM] * n,
        out_shape=[jax.ShapeDtypeStruct((N_DEV, *s.shape), s.dtype) for s in shards],
        scratch_shapes=[pltpu.SemaphoreType.DMA((n, 7)), pltpu.SemaphoreType.DMA((n, 7)), pltpu.SemaphoreType.DMA((n,))],
        name=name, compiler_params=pltpu.CompilerParams(has_side_effects=True))(*shards)


def _sibling_exchange(gs, *, name):
    n = len(gs)

    def body(*refs):
        ins, outs = refs[:n], refs[n:2 * n]
        send_sems, recv_sems = refs[2 * n:]
        x, y, c = _me()
        cps = [pltpu.make_async_remote_copy(
            src_ref=ins[a].at[:, 1 - c], dst_ref=outs[a], send_sem=send_sems.at[a], recv_sem=recv_sems.at[a],
            device_id=(x, y, 1 - c), device_id_type=MESH) for a in range(n)]
        for cp in cps:
            cp.start()
        for cp in cps:
            cp.wait()

    return pl.pallas_call(
        body, in_specs=[HBM] * n, out_specs=[HBM] * n,
        out_shape=[jax.ShapeDtypeStruct((4, *g.shape[2:]), g.dtype) for g in gs],
        scratch_shapes=[pltpu.SemaphoreType.DMA((n,)), pltpu.SemaphoreType.DMA((n,))],
        name=name, compiler_params=pltpu.CompilerParams(has_side_effects=True))(*gs)


def _pair_add(g, r, cidx, *, name):
    _, _, rr, cc = g.shape
    tr = rr
    for cand in (512, 384, 256, 192, 176, 128, 64, 32, 16, 8):
        if rr % cand == 0:
            tr = cand
            break
    if rr < 8:
        tr = rr

    def body(c_ref, g_ref, r_ref, ob_ref, of_ref):
        s = g_ref[0, 0] + r_ref[0]
        ob_ref[0] = s.astype(BF16)
        of_ref[0] = s

    grid_spec = pltpu.PrefetchScalarGridSpec(
        num_scalar_prefetch=1, grid=(4, rr // tr),
        in_specs=[pl.BlockSpec((1, 1, tr, cc), lambda kk, i, c_ref: (kk, c_ref[0], i, 0)),
                  pl.BlockSpec((1, tr, cc), lambda kk, i, c_ref: (kk, i, 0))],
        out_specs=[pl.BlockSpec((1, tr, cc), lambda kk, i, c_ref: (kk, i, 0)),
                   pl.BlockSpec((1, tr, cc), lambda kk, i, c_ref: (kk, i, 0))])
    return pl.pallas_call(
        body, grid_spec=grid_spec,
        out_shape=[jax.ShapeDtypeStruct((4, rr, cc), BF16), jax.ShapeDtypeStruct((4, rr, cc), F32)],
        name=name, compiler_params=_cp())(cidx, g, r)


def _chip_exchange(ps, *, name):
    n = len(ps)

    def body(*refs):
        ins, outs = refs[:n], refs[n:2 * n]
        send_sems, recv_sems = refs[2 * n:]
        x, y, c = _me()
        chips = [(1 - x, y), (x, 1 - y), (1 - x, 1 - y)]
        cps = []
        for a in range(n):
            for jj, (px, py) in enumerate(chips):
                cps.append(pltpu.make_async_remote_copy(
                    src_ref=ins[a].at[2 * px + py], dst_ref=outs[a].at[jj],
                    send_sem=send_sems.at[a, jj], recv_sem=recv_sems.at[a, jj],
                    device_id=(px, py, c), device_id_type=MESH))
        for cp in cps:
            cp.start()
        for cp in cps:
            cp.wait()

    return pl.pallas_call(
        body, in_specs=[HBM] * n, out_specs=[HBM] * n,
        out_shape=[jax.ShapeDtypeStruct((3, *p.shape[1:]), p.dtype) for p in ps],
        scratch_shapes=[pltpu.SemaphoreType.DMA((n, 3)), pltpu.SemaphoreType.DMA((n, 3))],
        name=name, compiler_params=pltpu.CompilerParams(has_side_effects=True))(*ps)


def _row_tile(rr):
    for cand in (512, 384, 256, 192, 176, 128, 64, 32, 16, 8):
        if rr % cand == 0:
            return cand
    return rr


def _sum4(pf, recv, chip, *, name):
    _, rr, cc = pf.shape
    tr = _row_tile(rr)

    def body(c_ref, p_ref, r_ref, o_ref):
        o_ref[...] = ((p_ref[0] + r_ref[0].astype(F32)) + r_ref[1].astype(F32)) + r_ref[2].astype(F32)

    grid_spec = pltpu.PrefetchScalarGridSpec(
        num_scalar_prefetch=1, grid=(rr // tr,),
        in_specs=[pl.BlockSpec((1, tr, cc), lambda i, c_ref: (c_ref[0], i, 0)),
                  pl.BlockSpec((3, tr, cc), lambda i, c_ref: (0, i, 0))],
        out_specs=pl.BlockSpec((tr, cc), lambda i, c_ref: (i, 0)))
    return pl.pallas_call(body, grid_spec=grid_spec, out_shape=jax.ShapeDtypeStruct((rr, cc), F32),
                          name=name, compiler_params=_cp())(chip, pf, recv)


def _small_exchange(vec, *, name):
    rr = vec.shape[0]

    def body(v_ref, o_ref, send_sems, recv_sems):
        x, y, c = _me()
        myid = 4 * x + 2 * y + c
        o_ref[myid] = v_ref[...]
        cps = []
        for kk in range(1, N_DEV):
            px, py, pc = x ^ (kk >> 2), y ^ ((kk >> 1) & 1), c ^ (kk & 1)
            cps.append(pltpu.make_async_remote_copy(
                src_ref=v_ref, dst_ref=o_ref.at[myid], send_sem=send_sems.at[kk], recv_sem=recv_sems.at[kk],
                device_id=(px, py, pc), device_id_type=MESH))
        for cp in cps:
            cp.start()
        for kk in range(1, N_DEV):
            px, py, pc = x ^ (kk >> 2), y ^ ((kk >> 1) & 1), c ^ (kk & 1)
            pltpu.make_async_remote_copy(
                src_ref=v_ref, dst_ref=o_ref.at[4 * px + 2 * py + pc], send_sem=send_sems.at[kk],
                recv_sem=recv_sems.at[kk], device_id=(px, py, pc), device_id_type=MESH).wait_recv()
        for cp in cps:
            cp.wait_send()

    return pl.pallas_call(
        body, in_specs=[pl.BlockSpec(memory_space=pltpu.VMEM)], out_specs=pl.BlockSpec(memory_space=pltpu.VMEM),
        out_shape=jax.ShapeDtypeStruct((N_DEV, rr, LANES), F32),
        scratch_shapes=[pltpu.SemaphoreType.DMA((N_DEV,)), pltpu.SemaphoreType.DMA((N_DEV,))],
        name=name, compiler_params=pltpu.CompilerParams(has_side_effects=True))(vec)


def _adam_math(w, g, m, v):
    m = ADAM_B1 * m + (1.0 - ADAM_B1) * g
    v = ADAM_B2 * v + (1.0 - ADAM_B2) * (g * g)
    m_hat = m / (1.0 - ADAM_B1 ** ADAM_STEP)
    v_hat = v / (1.0 - ADAM_B2 ** ADAM_STEP)
    delta = -ADAM_LR * (m_hat / (jnp.sqrt(v_hat) + ADAM_EPS) + ADAM_WD * w)
    return delta, m, v


def _adamw_sharded(w, m, v, g, *, name):
    ll, rr, cc = w.shape
    tr = _row_tile(rr)

    def body(w_ref, m_ref, v_ref, g_ref, d_ref, nm_ref, nv_ref):
        d, nm, nv = _adam_math(w_ref[...], g_ref[...], m_ref[...], v_ref[...])
        d_ref[...] = d
        nm_ref[...] = nm
        nv_ref[...] = nv

    blk = pl.BlockSpec((1, tr, cc), lambda l, i: (l, i, 0))
    sh = jax.ShapeDtypeStruct((ll, rr, cc), F32)
    return pl.pallas_call(
        body, grid=(ll, rr // tr), in_specs=[blk] * 4, out_specs=[blk] * 3, out_shape=[sh] * 3,
        name=name, compiler_params=_cp())(w, m, v, g)


def _adamw_small(w, m, v, gall, *, name):
    rr = w.shape[0]

    def body(w_ref, m_ref, v_ref, g_ref, go_ref, d_ref, nm_ref, nv_ref):
        g = g_ref[0]
        for kk in range(1, N_DEV):
            g = g + g_ref[kk]
        d, nm, nv = _adam_math(w_ref[...], g, m_ref[...], v_ref[...])
        go_ref[...] = g
        d_ref[...] = d
        nm_ref[...] = nm
        nv_ref[...] = nv

    sh = jax.ShapeDtypeStruct((rr, LANES), F32)
    return pl.pallas_call(body, out_shape=[sh] * 4, name=name, compiler_params=_cp())(w, m, v, gall)


SHARDED = ("ev_w_in", "ev_w_out", "od_w_in", "od_conv_w", "od_w_out", "ffn_w_up", "ffn_conv_w", "ffn_w_down")
SMALL = ("od_a_log", "od_dt_bias", "od_norm_w", "ffn_conv_b", "ln1_g", "ln1_b", "ln2_g", "ln2_b")
ALL_W = ("ev_w_in", "ev_w_out", "od_w_in", "od_conv_w", "od_a_log", "od_dt_bias", "od_norm_w", "od_w_out",
         "ffn_w_up", "ffn_conv_w", "ffn_conv_b", "ffn_w_down", "ln1_g", "ln1_b", "ln2_g", "ln2_b")


def _layer_items(layer):
    j = layer // 2
    if layer % 2 == 0:
        mixer = [("in_t", "ev_w_in", j, "colT"), ("out", "ev_w_out", j, "row")]
    else:
        mixer = [("in_t", "od_w_in", j, "colT"), ("conv", "od_conv_w", j, "colsmall"), ("out", "od_w_out", j, "row")]
    return mixer + [("up_t", "ffn_w_up", layer, "colT"), ("fconv", "ffn_conv_w", layer, "colsmall"),
                    ("down", "ffn_w_down", layer, "row")]


def _to_send(kind, shard):
    if kind == "colT":
        return shard.T.astype(BF16)
    return shard.astype(BF16) if kind == "row" else shard


def _from_gather(kind, name, g):
    if kind == "colsmall":
        return jnp.transpose(g, (1, 0, 2)).reshape(g.shape[1], -1)
    full = g.reshape(-1, g.shape[-1])
    if name == "od_w_in":
        full = jnp.pad(full, ((0, OD_IN_PAD - OD_IN), (0, 0)))
    return full


def _by_owner(kind, name, gfull):
    if kind == "colsmall":
        kk, c8 = gfull.shape
        return jnp.transpose(gfull.reshape(kk, N_DEV, c8 // N_DEV), (1, 0, 2))
    if name == "od_w_in":
        gfull = gfull[:OD_IN]
    return gfull.reshape(N_DEV, gfull.shape[0] // N_DEV, gfull.shape[1])


def _pack_small(d):
    flat = jnp.concatenate([d[n].reshape(-1) for n in SMALL])
    pad = (-flat.shape[0]) % (8 * LANES)
    return jnp.pad(flat, (0, pad)).reshape(-1, LANES)


def _unpack_small(packed, like):
    flat = packed.reshape(-1)
    out, off = {}, 0
    for n in SMALL:
        sz = int(np.prod(like[n].shape))
        out[n] = flat[off:off + sz].reshape(like[n].shape)
        off += sz
    return out


def kernel(x, positions, ev_w_in, ev_w_out, od_w_in, od_conv_w, od_a_log, od_dt_bias, od_norm_w, od_w_out, ffn_w_up, ffn_conv_w, ffn_conv_b, ffn_w_down, ln1_g, ln1_b, ln2_g, ln2_b, loss_target, m_ev_w_in, m_ev_w_out, m_od_w_in, m_od_conv_w, m_od_a_log, m_od_dt_bias, m_od_norm_w, m_od_w_out, m_ffn_w_up, m_ffn_conv_w, m_ffn_conv_b, m_ffn_w_down, m_ln1_g, m_ln1_b, m_ln2_g, m_ln2_b, v_ev_w_in, v_ev_w_out, v_od_w_in, v_od_conv_w, v_od_a_log, v_od_dt_bias, v_od_norm_w, v_od_w_out, v_ffn_w_up, v_ffn_conv_w, v_ffn_conv_b, v_ffn_w_down, v_ln1_g, v_ln1_b, v_ln2_g, v_ln2_b):
    w = dict(ev_w_in=ev_w_in, ev_w_out=ev_w_out, od_w_in=od_w_in, od_conv_w=od_conv_w, od_a_log=od_a_log,
             od_dt_bias=od_dt_bias, od_norm_w=od_norm_w, od_w_out=od_w_out, ffn_w_up=ffn_w_up, ffn_conv_w=ffn_conv_w,
             ffn_conv_b=ffn_conv_b, ffn_w_down=ffn_w_down, ln1_g=ln1_g, ln1_b=ln1_b, ln2_g=ln2_g, ln2_b=ln2_b)
    mom = dict(ev_w_in=m_ev_w_in, ev_w_out=m_ev_w_out, od_w_in=m_od_w_in, od_conv_w=m_od_conv_w, od_a_log=m_od_a_log,
               od_dt_bias=m_od_dt_bias, od_norm_w=m_od_norm_w, od_w_out=m_od_w_out, ffn_w_up=m_ffn_w_up,
               ffn_conv_w=m_ffn_conv_w, ffn_conv_b=m_ffn_conv_b, ffn_w_down=m_ffn_w_down, ln1_g=m_ln1_g,
               ln1_b=m_ln1_b, ln2_g=m_ln2_g, ln2_b=m_ln2_b)
    var = dict(ev_w_in=v_ev_w_in, ev_w_out=v_ev_w_out, od_w_in=v_od_w_in, od_conv_w=v_od_conv_w, od_a_log=v_od_a_log,
               od_dt_bias=v_od_dt_bias, od_norm_w=v_od_norm_w, od_w_out=v_od_w_out, ffn_w_up=v_ffn_w_up,
               ffn_conv_w=v_ffn_conv_w, ffn_conv_b=v_ffn_conv_b, ffn_w_down=v_ffn_w_down, ln1_g=v_ln1_g,
               ln1_b=v_ln1_b, ln2_g=v_ln2_g, ln2_b=v_ln2_b)

    myid = _my_index()
    small = {n: w[n] for n in SMALL}
    groups = [(layer, part) for layer in range(DEPTH) for part in ("mixer", "ffn")]

    def group_items(gi):
        layer, part = groups[gi]
        its = _layer_items(layer)
        return its[:-3] if part == "mixer" else its[-3:]

    fetch = {}

    def gather_start(gi, after):
        srcs = [_to_send(kind, w[n][j]) for (_, n, j, kind) in group_items(gi)]
        fetch[gi] = _exchange_start(srcs, "gather", after, name=f"gather{gi}_start")

    def get_w(layer, part, after):
        gi = groups.index((layer, part))
        srcs, lands = _exchange_wait(fetch.pop(gi), "gather", after, name=f"gather{gi}_wait")
        lands = [lax.dynamic_update_index_in_dim(l, s, myid, 0) for l, s in zip(lands, srcs)]
        dep = None
        if gi + 1 < len(groups):
            gather_start(gi + 1, lands[0])
            dep = fetch[gi + 1][4]
        return {key: _from_gather(kind, n, l) for (key, n, _, kind), l in zip(group_items(gi), lands)}, dep

    landed = {}
    pending = []

    def scatter_finish(after):
        started, gi = pending.pop()
        srcs, lands = _exchange_wait(started, "scatter", after, name=f"scatter{gi}_wait")
        for (key, _, _, _), l, s in zip(group_items(gi), lands, srcs):
            own = lax.dynamic_index_in_dim(s, myid, 0, keepdims=False)
            landed[(groups[gi][0], key)] = lax.dynamic_update_index_in_dim(l, own, myid, 0)

    def put_g(layer, part, g):
        gi = groups.index((layer, part))
        srcs = [_by_owner(kind, n, g[key]) for (key, n, _, kind) in group_items(gi)]
        started = _exchange_start(srcs, "scatter", None, name=f"scatter{gi}_start")
        if pending:
            scatter_finish(started[4])
        pending.append((started, gi))
        return started[4]

    gather_start(0, None)
    loss, grad_x, gS = _local_step(x[0], positions[0], loss_target[0], get_w, put_g, small)
    loss = lax.psum(loss, ("x", "y", "c"))

    outs_g, outs_d, outs_m, outs_v = {}, {}, {}, {}
    where = {n: [None] * w[n].shape[0] for n in SHARDED}
    for layer in range(DEPTH):
        for (key, n, j, kind) in _layer_items(layer):
            where[n][j] = (layer, key, kind)

    def update(n):
        parts = []
        for (layer, key, kind) in where[n]:
            gsh = _sum8(landed[(layer, key)], name=f"L{layer}_{key}_sum")
            parts.append(gsh.T if kind == "colT" else gsh)
        outs_g[n] = jnp.stack(parts)
        outs_d[n], outs_m[n], outs_v[n] = _adamw_sharded(w[n], mom[n], var[n], outs_g[n], name=f"adamw_{n}")

    last = {n for (_, n, _, _) in group_items(pending[0][1])}
    for n in SHARDED:
        if n not in last:
            update(n)
    scatter_finish(outs_d[[n for n in SHARDED if n not in last][-1]])
    for n in SHARDED:
        if n in last:
            update(n)

    gall = _small_exchange(_pack_small(gS), name="small_grads_exchange")
    g, d, nm, nv = _adamw_small(_pack_small({n: w[n] for n in SMALL}), _pack_small({n: mom[n] for n in SMALL}),
                                _pack_small({n: var[n] for n in SMALL}), gall, name="adamw_small")
    for dst, packed in ((outs_g, g), (outs_d, d), (outs_m, nm), (outs_v, nv)):
        dst.update(_unpack_small(packed, {n: w[n] for n in SMALL}))

    return (loss, grad_x[None], *[outs_g[n] for n in ALL_W], *[outs_d[n] for n in ALL_W],
            *[outs_m[n] for n in ALL_W], *[outs_v[n] for n in ALL_W])
```

```python
import functools
import math

import numpy as np
import jax
import jax.numpy as jnp
from jax import lax
from jax.experimental import pallas as pl
from jax.experimental.pallas import tpu as pltpu

F32 = jnp.float32
BF16 = jnp.bfloat16
MESH = pl.DeviceIdType.MESH

D_MODEL = 1024
SEQ = 2048
DEPTH = 4
N_DEV = 8
RET_HEADS, RET_DK, RET_DV = 4, 128, 256
RET_THETA = 10000.0
DIL_HEADS, DIL_HD = 8, 64
DIL_PAIRS = ((128, 1), (512, 4), (2048, 16))
ROPE_THETA = 500000.0
ROPE_DIMS = DIL_HD // 4
GDN_HEADS, GDN_DK, GDN_DV, GDN_CHUNK, GDN_CONV = 8, 128, 128, 64, 4
D_FF = 2816
FFN_CONV = 3
ALPHA = (2.0 * DEPTH) ** 0.25
EPS = 1e-5
RET_QK_W = RET_HEADS * RET_DK
RET_V_W = RET_HEADS * RET_DV
DIL_W = DIL_HEADS * DIL_HD
EV_IN = 2 * RET_QK_W + 2 * RET_V_W + 3 * DIL_W
EV_MIX = RET_V_W + DIL_W
GDN_W = GDN_HEADS * GDN_DK
OD_IN = 4 * GDN_W + 2 * GDN_HEADS
OD_IN_PAD = 4 * GDN_W + 128
ADAM_LR, ADAM_B1, ADAM_B2, ADAM_EPS, ADAM_WD, ADAM_STEP = 0.001, 0.9, 0.999, 1e-08, 0.01, 10

LANES = 128
VMEM_LIMIT = 56 * 1024 * 1024
ATT_BLK = 256
NEG = -1e30


def _cp(**kw):
    return pltpu.CompilerParams(vmem_limit_bytes=VMEM_LIMIT, **kw)


def _tile(n, cap):
    if n <= cap:
        return n
    best = None
    for t in range(LANES, cap + 1, LANES):
        if n % t == 0:
            best = t
    assert best is not None, (n, cap)
    return best


def _mm(a, b, *, ta=False, tb=False, name, out_dtype=F32, dep=None, tm=None, tn=None):
    m = a.shape[1] if ta else a.shape[0]
    k = a.shape[0] if ta else a.shape[1]
    n = b.shape[0] if tb else b.shape[1]
    assert (b.shape[1] if tb else b.shape[0]) == k
    assert a.dtype == BF16 and b.dtype == BF16
    if tn is None:
        tn = n if n <= 1024 else _tile(n, 512)
    if tm is None:
        tm = m if (tn < n and k <= 1024 and m <= 2048) else _tile(m, 512)
    dims = (((0 if ta else 1,), (1 if tb else 0,)), ((), ()))

    def body(a_ref, b_ref, *rest):
        o_ref = rest[-1]
        o_ref[...] = lax.dot_general(a_ref[...], b_ref[...], dims,
                                     preferred_element_type=F32).astype(o_ref.dtype)

    a_spec = pl.BlockSpec((k, tm), lambda i, j: (0, i)) if ta else pl.BlockSpec((tm, k), lambda i, j: (i, 0))
    b_spec = pl.BlockSpec((tn, k), lambda i, j: (j, 0)) if tb else pl.BlockSpec((k, tn), lambda i, j: (0, j))
    extra = [] if dep is None else [dep]
    return pl.pallas_call(
        body, grid=(m // tm, n // tn), in_specs=[a_spec, b_spec] + [pl.BlockSpec(memory_space=pl.ANY)] * len(extra),
        out_specs=pl.BlockSpec((tm, tn), lambda i, j: (i, j)),
        out_shape=jax.ShapeDtypeStruct((m, n), out_dtype), name=name, compiler_params=_cp())(a, b, *extra)


LN_ROWS = 256


def _ln_fwd(x, m, g, b, *, name):
    t, d = x.shape

    def body(x_ref, m_ref, g_ref, b_ref, z_ref, y_ref, yb_ref):
        z = ALPHA * x_ref[...] + m_ref[...]
        mu = jnp.mean(z, -1, keepdims=True)
        zc = z - mu
        var = jnp.mean(zc * zc, -1, keepdims=True)
        y = zc * lax.rsqrt(var + EPS) * g_ref[...] + b_ref[...]
        z_ref[...] = z
        y_ref[...] = y
        yb_ref[...] = y.astype(BF16)

    row = pl.BlockSpec((LN_ROWS, d), lambda i: (i, 0))
    vec = pl.BlockSpec((1, d), lambda i: (0, 0))
    return pl.pallas_call(
        body, grid=(t // LN_ROWS,), in_specs=[row, row, vec, vec], out_specs=[row, row, row],
        out_shape=[jax.ShapeDtypeStruct((t, d), F32), jax.ShapeDtypeStruct((t, d), F32),
                   jax.ShapeDtypeStruct((t, d), BF16)],
        name=name, compiler_params=_cp())(x, m, g, b)


def _ln_bwd(z, g, dya, dyb, *, name):
    t, d = z.shape
    two = dyb is not None

    def body(*refs):
        if two:
            z_ref, g_ref, dya_ref, dyb_ref, dz_ref, dzb_ref, dg_ref, db_ref = refs
            dy = dya_ref[...] + ALPHA * dyb_ref[...]
        else:
            z_ref, g_ref, dya_ref, dz_ref, dzb_ref, dg_ref, db_ref = refs
            dy = dya_ref[...]
        zz = z_ref[...]
        mu = jnp.mean(zz, -1, keepdims=True)
        zc = zz - mu
        var = jnp.mean(zc * zc, -1, keepdims=True)
        r = lax.rsqrt(var + EPS)
        xh = zc * r
        dxh = dy * g_ref[...]
        dz = r * (dxh - jnp.mean(dxh, -1, keepdims=True) - xh * jnp.mean(dxh * xh, -1, keepdims=True))
        dz_ref[...] = dz
        dzb_ref[...] = dz.astype(BF16)

        @pl.when(pl.program_id(0) == 0)
        def _():
            dg_ref[...] = jnp.zeros_like(dg_ref)
            db_ref[...] = jnp.zeros_like(db_ref)

        dg_ref[...] += jnp.sum(dy * xh, 0, keepdims=True)
        db_ref[...] += jnp.sum(dy, 0, keepdims=True)

    row = pl.BlockSpec((LN_ROWS, d), lambda i: (i, 0))
    vec = pl.BlockSpec((1, d), lambda i: (0, 0))
    ins = [z, g, dya] + ([dyb] if two else [])
    return pl.pallas_call(
        body, grid=(t // LN_ROWS,), in_specs=[row, vec, row] + ([row] if two else []),
        out_specs=[row, row, vec, vec],
        out_shape=[jax.ShapeDtypeStruct((t, d), F32), jax.ShapeDtypeStruct((t, d), BF16),
                   jax.ShapeDtypeStruct((1, d), F32), jax.ShapeDtypeStruct((1, d), F32)],
        name=name, compiler_params=_cp())(*ins)


def _axpy(a, b, *, name):
    t, d = a.shape

    def body(a_ref, b_ref, o_ref):
        o_ref[...] = a_ref[...] + ALPHA * b_ref[...]

    row = pl.BlockSpec((LN_ROWS, d), lambda i: (i, 0))
    return pl.pallas_call(body, grid=(t // LN_ROWS,), in_specs=[row, row], out_specs=row,
                          out_shape=jax.ShapeDtypeStruct((t, d), F32), name=name, compiler_params=_cp())(a, b)


def _loss_head(y, target, *, name):
    t, d = y.shape

    def body(y_ref, t_ref, dy_ref, l_ref):
        e = y_ref[...] - t_ref[...]
        dy_ref[...] = e * (1.0 / d)

        @pl.when(pl.program_id(0) == 0)
        def _():
            l_ref[...] = jnp.zeros_like(l_ref)

        l_ref[...] += jnp.zeros_like(l_ref) + 0.5 * jnp.sum(jnp.mean(e * e, -1, keepdims=True), 0, keepdims=True)

    row = pl.BlockSpec((LN_ROWS, d), lambda i: (i, 0))
    return pl.pallas_call(
        body, grid=(t // LN_ROWS,), in_specs=[row, row],
        out_specs=[row, pl.BlockSpec((1, LANES), lambda i: (0, 0))],
        out_shape=[jax.ShapeDtypeStruct((t, d), F32), jax.ShapeDtypeStruct((1, LANES), F32)],
        name=name, compiler_params=_cp())(y, target)


def _sig(x):
    return 1.0 / (1.0 + jnp.exp(-x))


def _silu(x):
    return x * _sig(x)


def _dsilu(x):
    s = _sig(x)
    return s * (1.0 + x * (1.0 - s))


def _shift_down(u, k, row):
    if k == 0:
        return u
    return jnp.where(row >= k, pltpu.roll(u, k, 0), 0.0)


def _shift_up(u, k, row):
    if k == 0:
        return u
    t = u.shape[0]
    return jnp.where(row < t - k, pltpu.roll(u, t - k, 0), 0.0)


def _dwconv(u, w_ref, row):
    kk = w_ref.shape[0]
    acc = None
    for j in range(kk):
        term = w_ref[j:j + 1, :] * _shift_down(u, kk - 1 - j, row)
        acc = term if acc is None else acc + term
    return acc


def _dwconv_bwd(u, w_ref, dc, row, dw_ref):
    kk = w_ref.shape[0]
    du = None
    for j in range(kk):
        term = w_ref[j:j + 1, :] * _shift_up(dc, kk - 1 - j, row)
        du = term if du is None else du + term
        dw_ref[j:j + 1, :] = jnp.sum(dc * _shift_down(u, kk - 1 - j, row), 0, keepdims=True)
    return du


CONV_ROWS = 64


def _rows(b):
    return pl.ds(pl.multiple_of(b * CONV_ROWS, CONV_ROWS), CONV_ROWS)


def _shifted_down(ref, b, k, row):
    cur = ref[_rows(b), :]
    if k == 0:
        return cur
    prev = jnp.where(b > 0, ref[_rows(jnp.maximum(b - 1, 0)), :], 0.0)
    return jnp.where(row >= k, pltpu.roll(cur, k, 0), pltpu.roll(prev, k, 0))


def _shifted_up(ref, b, k, row, nblk):
    cur = ref[_rows(b), :]
    if k == 0:
        return cur
    nxt = jnp.where(b < nblk - 1, ref[_rows(jnp.minimum(b + 1, nblk - 1)), :], 0.0)
    return jnp.where(row < CONV_ROWS - k, pltpu.roll(cur, CONV_ROWS - k, 0), pltpu.roll(nxt, CONV_ROWS - k, 0))


def _dwconv_blk(u_ref, w_ref, b, row):
    kk = w_ref.shape[0]
    views = [_shifted_down(u_ref, b, kk - 1 - j, row) for j in range(kk)]
    acc = None
    for j in range(kk):
        term = w_ref[j:j + 1, :] * views[j]
        acc = term if acc is None else acc + term
    return acc, views


def _dwconv_du_blk(dc_ref, w_ref, b, row, nblk):
    kk = w_ref.shape[0]
    du = None
    for j in range(kk):
        term = w_ref[j:j + 1, :] * _shifted_up(dc_ref, b, kk - 1 - j, row, nblk)
        du = term if du is None else du + term
    return du


FFN_TC = 256


def _ffn_mid_fwd(u, cw, cb, *, name):
    t = u.shape[0]
    nb = D_FF // FFN_TC

    def body(ug_ref, uv_ref, wg_ref, wv_ref, bg_ref, bv_ref, a_ref):
        row = lax.broadcasted_iota(jnp.int32, (t, FFN_TC), 0)
        cg = _dwconv(ug_ref[...], wg_ref, row) + bg_ref[...]
        cv = _dwconv(uv_ref[...], wv_ref, row) + bv_ref[...]
        a_ref[...] = (_silu(cg) * cv).astype(BF16)

    col = lambda off: pl.BlockSpec((t, FFN_TC), lambda j: (0, j + off))
    wsp = lambda off: pl.BlockSpec((FFN_CONV, FFN_TC), lambda j: (0, j + off))
    bsp = lambda off: pl.BlockSpec((1, FFN_TC), lambda j: (0, j + off))
    return pl.pallas_call(
        body, grid=(nb,), in_specs=[col(0), col(nb), wsp(0), wsp(nb), bsp(0), bsp(nb)],
        out_specs=pl.BlockSpec((t, FFN_TC), lambda j: (0, j)),
        out_shape=jax.ShapeDtypeStruct((t, D_FF), BF16), name=name, compiler_params=_cp())(u, u, cw, cw, cb, cb)


def _ffn_mid_bwd(u, cw, cb, da, *, name):
    t = u.shape[0]
    nb = D_FF // FFN_TC

    nblk = t // CONV_ROWS

    def body(ug_ref, uv_ref, wg_ref, wv_ref, bg_ref, bv_ref, da_ref,
             dug_ref, duv_ref, dwg_ref, dwv_ref, dbg_ref, dbv_ref, dcg_s, dcv_s):
        row = lax.broadcasted_iota(jnp.int32, (CONV_ROWS, FFN_TC), 0)
        zero = jnp.zeros((1, FFN_TC), F32)

        def first(b, acc):
            cg, ugs = _dwconv_blk(ug_ref, wg_ref, b, row)
            cv, uvs = _dwconv_blk(uv_ref, wv_ref, b, row)
            cg = cg + bg_ref[...]
            cv = cv + bv_ref[...]
            da_ = da_ref[_rows(b), :]
            dcv = da_ * _silu(cg)
            dcg = da_ * cv * _dsilu(cg)
            dcg_s[_rows(b), :] = dcg
            dcv_s[_rows(b), :] = dcv
            red = [jnp.sum(dcg * s, 0, keepdims=True) for s in ugs] + [jnp.sum(dcg, 0, keepdims=True)]
            red += [jnp.sum(dcv * s, 0, keepdims=True) for s in uvs] + [jnp.sum(dcv, 0, keepdims=True)]
            return tuple(a + r for a, r in zip(acc, red))

        acc = lax.fori_loop(0, nblk, first, (zero,) * (2 * FFN_CONV + 2))
        for j in range(FFN_CONV):
            dwg_ref[j:j + 1, :] = acc[j]
            dwv_ref[j:j + 1, :] = acc[FFN_CONV + 1 + j]
        dbg_ref[...] = acc[FFN_CONV]
        dbv_ref[...] = acc[2 * FFN_CONV + 1]

        def second(b, carry):
            dug_ref[_rows(b), :] = _dwconv_du_blk(dcg_s, wg_ref, b, row, nblk).astype(BF16)
            duv_ref[_rows(b), :] = _dwconv_du_blk(dcv_s, wv_ref, b, row, nblk).astype(BF16)
            return carry

        lax.fori_loop(0, nblk, second, 0)

    col = lambda off: pl.BlockSpec((t, FFN_TC), lambda j: (0, j + off))
    wsp = lambda off: pl.BlockSpec((FFN_CONV, FFN_TC), lambda j: (0, j + off))
    bsp = lambda off: pl.BlockSpec((1, FFN_TC), lambda j: (0, j + off))
    outs = pl.pallas_call(
        body, grid=(nb,), in_specs=[col(0), col(nb), wsp(0), wsp(nb), bsp(0), bsp(nb), col(0)],
        out_specs=[col(0), col(0), wsp(0), wsp(0), bsp(0), bsp(0)],
        out_shape=[jax.ShapeDtypeStruct((t, D_FF), BF16), jax.ShapeDtypeStruct((t, D_FF), BF16),
                   jax.ShapeDtypeStruct((FFN_CONV, D_FF), F32), jax.ShapeDtypeStruct((FFN_CONV, D_FF), F32),
                   jax.ShapeDtypeStruct((1, D_FF), F32), jax.ShapeDtypeStruct((1, D_FF), F32)],
        scratch_shapes=[pltpu.VMEM((t, FFN_TC), F32), pltpu.VMEM((t, FFN_TC), F32)],
        name=name, compiler_params=_cp())(u, u, cw, cw, cb, cb, da)
    dug, duv, dwg, dwv, dbg, dbv = outs
    return (jnp.concatenate([dug, duv], 1), jnp.concatenate([dwg, dwv], 1), jnp.concatenate([dbg, dbv], 1))


def _rot_a(x, c2, s2):
    return x * c2 + pltpu.roll(x, RET_DK // 2, 1) * s2


def _rot_a_t(dy, c2, s2):
    return dy * c2 + pltpu.roll(dy * s2, RET_DK // 2, 1)


def _decay_tile(lg, blk_diff):
    r = lax.broadcasted_iota(jnp.int32, (ATT_BLK, ATT_BLK), 0)
    c = lax.broadcasted_iota(jnp.int32, (ATT_BLK, ATT_BLK), 1)
    rel = r - c + blk_diff * ATT_BLK
    return jnp.where(rel >= 0, jnp.exp(jnp.maximum(rel, 0).astype(F32) * lg), 0.0)


def _nt(a, b):
    return lax.dot_general(a, b, (((1,), (1,)), ((), ())), preferred_element_type=F32)


def _nn(a, b):
    return lax.dot_general(a, b, (((1,), (0,)), ((), ())), preferred_element_type=F32)


def _tn(a, b):
    return lax.dot_general(a, b, (((0,), (0,)), ((), ())), preferred_element_type=F32)


def _ret_specs(t):
    q = pl.BlockSpec((t, RET_DK), lambda h: (0, h))
    k = pl.BlockSpec((t, RET_DK), lambda h: (0, RET_HEADS + h))
    v = pl.BlockSpec((t, RET_DV), lambda h: (0, RET_HEADS + h))
    g = pl.BlockSpec((t, RET_DV), lambda h: (0, 2 * RET_HEADS + h))
    tab = pl.BlockSpec((t, RET_DK), lambda h: (0, 0))
    lg = pl.BlockSpec((1, 1, LANES), lambda h: (h, 0, 0))
    return q, k, v, g, tab, lg


def _ret_fwd(h, c2, s2, lgt, *, name):
    t = h.shape[0]
    nblk = t // ATT_BLK
    scale = RET_DK ** -0.5

    def body(q_ref, k_ref, v_ref, g_ref, c_ref, s_ref, lg_ref, o_ref, ya_ref, qs, ks, vs):
        c2_, s2_ = c_ref[...], s_ref[...]
        qs[...] = _rot_a(q_ref[...], c2_, s2_).astype(BF16)
        ks[...] = (_rot_a(k_ref[...], c2_, s2_) * scale).astype(BF16)
        vs[...] = v_ref[...].astype(BF16)
        lg = lg_ref[0, :, 0:1]
        for i in range(nblk):
            qi = qs[pl.ds(i * ATT_BLK, ATT_BLK), :]
            acc = jnp.zeros((ATT_BLK, RET_DV), F32)
            for j in range(i + 1):
                sl = pl.ds(j * ATT_BLK, ATT_BLK)
                s = _nt(qi, ks[sl, :]) * _decay_tile(lg, i - j)
                acc = acc + _nn(s.astype(BF16), vs[sl, :])
            rows = pl.ds(i * ATT_BLK, ATT_BLK)
            o_ref[rows, :] = acc
            r = lax.rsqrt(jnp.mean(acc * acc, -1, keepdims=True) + EPS)
            ya_ref[rows, :] = (acc * r * _silu(g_ref[rows, :])).astype(BF16)

    q, k, v, g, tab, lg = _ret_specs(t)
    out = pl.BlockSpec((t, RET_DV), lambda hh: (0, hh))
    return pl.pallas_call(
        body, grid=(RET_HEADS,), in_specs=[q, k, v, g, tab, tab, lg], out_specs=[out, out],
        out_shape=[jax.ShapeDtypeStruct((t, RET_V_W), F32), jax.ShapeDtypeStruct((t, RET_V_W), BF16)],
        scratch_shapes=[pltpu.VMEM((t, RET_DK), BF16), pltpu.VMEM((t, RET_DK), BF16), pltpu.VMEM((t, RET_DV), BF16)],
        name=name, compiler_params=_cp())(h, h, h, h, c2, s2, lgt)


def _ret_bwd(h, c2, s2, lgt, o, dy, *, name):
    t = h.shape[0]
    nblk = t // ATT_BLK
    scale = RET_DK ** -0.5

    def body(q_ref, k_ref, v_ref, g_ref, c_ref, s_ref, lg_ref, o_ref, dy_ref,
             dq_ref, dk_ref, dv_ref, dg_ref, qs, ks, vs, dos, dka, dva):
        c2_, s2_ = c_ref[...], s_ref[...]
        qs[...] = _rot_a(q_ref[...], c2_, s2_).astype(BF16)
        ks[...] = (_rot_a(k_ref[...], c2_, s2_) * scale).astype(BF16)
        vs[...] = v_ref[...].astype(BF16)
        lg = lg_ref[0, :, 0:1]
        oo = o_ref[...]
        gg = g_ref[...]
        dya = dy_ref[...]
        r = lax.rsqrt(jnp.mean(oo * oo, -1, keepdims=True) + EPS)
        rn = oo * r
        dg_ref[...] = (dya * rn * _dsilu(gg)).astype(BF16)
        drn = dya * _silu(gg)
        dos[...] = (r * (drn - rn * jnp.mean(drn * rn, -1, keepdims=True))).astype(BF16)
        dka[...] = jnp.zeros_like(dka)
        dva[...] = jnp.zeros_like(dva)
        for i in range(nblk):
            rows = pl.ds(i * ATT_BLK, ATT_BLK)
            qi = qs[rows, :]
            doi = dos[rows, :]
            dqa = jnp.zeros((ATT_BLK, RET_DK), F32)
            for j in range(i + 1):
                sl = pl.ds(j * ATT_BLK, ATT_BLK)
                dt_ = _decay_tile(lg, i - j)
                kj = ks[sl, :]
                s = (_nt(qi, kj) * dt_).astype(BF16)
                ds = (_nt(doi, vs[sl, :]) * dt_).astype(BF16)
                dqa = dqa + _nn(ds, kj)
                dka[sl, :] += _tn(ds, qi)
                dva[sl, :] += _tn(s, doi)
            dq_ref[rows, :] = _rot_a_t(dqa, c_ref[rows, :], s_ref[rows, :]).astype(BF16)
        dk_ref[...] = (_rot_a_t(dka[...], c2_, s2_) * scale).astype(BF16)
        dv_ref[...] = dva[...].astype(BF16)

    q, k, v, g, tab, lg = _ret_specs(t)
    blk_v = pl.BlockSpec((t, RET_DV), lambda hh: (0, hh))
    blk_k = pl.BlockSpec((t, RET_DK), lambda hh: (0, hh))
    return pl.pallas_call(
        body, grid=(RET_HEADS,), in_specs=[q, k, v, g, tab, tab, lg, blk_v, blk_v],
        out_specs=[blk_k, blk_k, blk_v, blk_v],
        out_shape=[jax.ShapeDtypeStruct((t, RET_QK_W), BF16), jax.ShapeDtypeStruct((t, RET_QK_W), BF16),
                   jax.ShapeDtypeStruct((t, RET_V_W), BF16), jax.ShapeDtypeStruct((t, RET_V_W), BF16)],
        scratch_shapes=[pltpu.VMEM((t, RET_DK), BF16), pltpu.VMEM((t, RET_DK), BF16), pltpu.VMEM((t, RET_DV), BF16),
                        pltpu.VMEM((t, RET_DV), BF16), pltpu.VMEM((t, RET_DK), F32), pltpu.VMEM((t, RET_DV), F32)],
        name=name, compiler_params=_cp())(h, h, h, h, c2, s2, lgt, o, dy)


def _rot_b(x, cb, shi, slo):
    return x * cb + pltpu.roll(x, ROPE_DIMS // 2, 1) * shi + pltpu.roll(x, LANES - ROPE_DIMS // 2, 1) * slo


def _rot_b_t(dy, cb, shi, slo):
    return dy * cb + pltpu.roll(dy * shi, LANES - ROPE_DIMS // 2, 1) + pltpu.roll(dy * slo, ROPE_DIMS // 2, 1)


def _dil_specs(t):
    base = (2 * RET_QK_W + 2 * RET_V_W) // LANES
    npair = DIL_W // LANES
    q = pl.BlockSpec((t, LANES), lambda p: (0, base + p))
    k = pl.BlockSpec((t, LANES), lambda p: (0, base + npair + p))
    v = pl.BlockSpec((t, LANES), lambda p: (0, base + 2 * npair + p))
    tab = pl.BlockSpec((t, LANES), lambda p: (0, 0))
    strip = pl.BlockSpec((ATT_BLK, t), lambda p: (0, 0))
    pair = pl.BlockSpec((t, LANES), lambda p: (0, p))
    return q, k, v, tab, strip, pair


def _dil_fwd(h, cb, shi, slo, strip, *, name):
    t = h.shape[0]
    nblk = t // ATT_BLK
    scale = DIL_HD ** -0.5

    def body(q_ref, k_ref, v_ref, cb_ref, shi_ref, slo_ref, st_ref, o_ref, yb_ref, lse_ref, qs, ks, vs):
        cb_, shi_, slo_ = cb_ref[...], shi_ref[...], slo_ref[...]
        lane = lax.broadcasted_iota(jnp.int32, (t, LANES), 1)
        qr = _rot_b(q_ref[...], cb_, shi_, slo_) * scale
        qs[0] = jnp.where(lane < DIL_HD, qr, 0.0).astype(BF16)
        qs[1] = jnp.where(lane >= DIL_HD, qr, 0.0).astype(BF16)
        ks[...] = _rot_b(k_ref[...], cb_, shi_, slo_).astype(BF16)
        vs[...] = v_ref[...].astype(BF16)
        lane_b = lax.broadcasted_iota(jnp.int32, (ATT_BLK, LANES), 1)
        for i in range(nblk):
            w = (i + 1) * ATT_BLK
            rows = pl.ds(i * ATT_BLK, ATT_BLK)
            logc = st_ref[:, t - w:t]
            outs, lses = [], []
            for hd in range(2):
                s = _nt(qs[hd, rows, :], ks[0:w, :]) + logc
                m = jnp.max(s, -1, keepdims=True)
                p = jnp.exp(s - m)
                l = jnp.sum(p, -1, keepdims=True)
                outs.append(_nn(p.astype(BF16), vs[0:w, :]) / l)
                lses.append(m + jnp.log(l))
            o = jnp.where(lane_b < DIL_HD, outs[0], outs[1])
            o_ref[rows, :] = o
            yb_ref[rows, :] = o.astype(BF16)
            lse_ref[rows, :] = jnp.where(lane_b < DIL_HD, lses[0], lses[1])

    q, k, v, tab, strip_spec, pair = _dil_specs(t)
    return pl.pallas_call(
        body, grid=(DIL_W // LANES,), in_specs=[q, k, v, tab, tab, tab, strip_spec], out_specs=[pair, pair, pair],
        out_shape=[jax.ShapeDtypeStruct((t, DIL_W), F32), jax.ShapeDtypeStruct((t, DIL_W), BF16),
                   jax.ShapeDtypeStruct((t, DIL_W), F32)],
        scratch_shapes=[pltpu.VMEM((2, t, LANES), BF16), pltpu.VMEM((t, LANES), BF16), pltpu.VMEM((t, LANES), BF16)],
        name=name, compiler_params=_cp())(h, h, h, cb, shi, slo, strip)


def _dil_bwd(h, cb, shi, slo, strip, o, lse, dy, *, name):
    t = h.shape[0]
    nblk = t // ATT_BLK
    scale = DIL_HD ** -0.5

    def body(q_ref, k_ref, v_ref, cb_ref, shi_ref, slo_ref, st_ref, o_ref, lse_ref, dy_ref,
             dq_ref, dk_ref, dv_ref, qs, ks, vs, dos, dls, dka, dva):
        cb_, shi_, slo_ = cb_ref[...], shi_ref[...], slo_ref[...]
        lane = lax.broadcasted_iota(jnp.int32, (t, LANES), 1)
        qr = _rot_b(q_ref[...], cb_, shi_, slo_) * scale
        qs[0] = jnp.where(lane < DIL_HD, qr, 0.0).astype(BF16)
        qs[1] = jnp.where(lane >= DIL_HD, qr, 0.0).astype(BF16)
        ks[...] = _rot_b(k_ref[...], cb_, shi_, slo_).astype(BF16)
        vs[...] = v_ref[...].astype(BF16)
        do = dy_ref[...]
        prod = do * o_ref[...]
        d0 = jnp.sum(jnp.where(lane < DIL_HD, prod, 0.0), -1, keepdims=True)
        d1 = jnp.sum(jnp.where(lane >= DIL_HD, prod, 0.0), -1, keepdims=True)
        dls[...] = jnp.where(lane < DIL_HD, d0, d1)
        dos[0] = jnp.where(lane < DIL_HD, do, 0.0).astype(BF16)
        dos[1] = jnp.where(lane >= DIL_HD, do, 0.0).astype(BF16)
        dka[...] = jnp.zeros_like(dka)
        dva[...] = jnp.zeros_like(dva)
        lane_b = lax.broadcasted_iota(jnp.int32, (ATT_BLK, LANES), 1)
        for i in range(nblk):
            w = (i + 1) * ATT_BLK
            rows = pl.ds(i * ATT_BLK, ATT_BLK)
            logc = st_ref[:, t - w:t]
            dqs = []
            for hd in range(2):
                col = hd * DIL_HD
                qh = qs[hd, rows, :]
                doh = dos[hd, rows, :]
                lse_h = lse_ref[rows, col:col + 1]
                dl_h = dls[rows, col:col + 1]
                p = jnp.exp(_nt(qh, ks[0:w, :]) + logc - lse_h)
                dp = _nt(doh, vs[0:w, :])
                ds = (p * (dp - dl_h)).astype(BF16)
                dqs.append(_nn(ds, ks[0:w, :]))
                dka[0:w, :] += _tn(ds, qh)
                dva[0:w, :] += _tn(p.astype(BF16), doh)
            dq = jnp.where(lane_b < DIL_HD, dqs[0], dqs[1]) * scale
            dq_ref[rows, :] = _rot_b_t(dq, cb_ref[rows, :], shi_ref[rows, :], slo_ref[rows, :]).astype(BF16)
        dk_ref[...] = _rot_b_t(dka[...], cb_, shi_, slo_).astype(BF16)
        dv_ref[...] = dva[...].astype(BF16)

    q, k, v, tab, strip_spec, pair = _dil_specs(t)
    dy_spec = pl.BlockSpec((t, LANES), lambda p: (0, RET_V_W // LANES + p))
    return pl.pallas_call(
        body, grid=(DIL_W // LANES,), in_specs=[q, k, v, tab, tab, tab, strip_spec, pair, pair, dy_spec],
        out_specs=[pair, pair, pair],
        out_shape=[jax.ShapeDtypeStruct((t, DIL_W), BF16)] * 3,
        scratch_shapes=[pltpu.VMEM((2, t, LANES), BF16), pltpu.VMEM((t, LANES), BF16), pltpu.VMEM((t, LANES), BF16),
                        pltpu.VMEM((2, t, LANES), BF16), pltpu.VMEM((t, LANES), F32),
                        pltpu.VMEM((t, LANES), F32), pltpu.VMEM((t, LANES), F32)],
        name=name, compiler_params=_cp())(h, h, h, cb, shi, slo, strip, o, lse, dy)


def _gdn_prep_fwd(h, cw, *, name):
    t = h.shape[0]
    qscale = GDN_DK ** -0.5

    def body(hq_ref, hk_ref, hv_ref, wq_ref, wk_ref, wv_ref, q_ref, k_ref, v_ref):
        row = lax.broadcasted_iota(jnp.int32, (t, GDN_DK), 0)
        sq = _silu(_dwconv(hq_ref[...], wq_ref, row))
        sk = _silu(_dwconv(hk_ref[...], wk_ref, row))
        q_ref[0] = sq * lax.rsqrt(jnp.sum(sq * sq, -1, keepdims=True) + 1e-6) * qscale
        k_ref[0] = sk * lax.rsqrt(jnp.sum(sk * sk, -1, keepdims=True) + 1e-6)
        v_ref[0] = _silu(_dwconv(hv_ref[...], wv_ref, row))

    hs = lambda off: pl.BlockSpec((t, GDN_DK), lambda i: (0, i + off))
    ws = lambda off: pl.BlockSpec((GDN_CONV, GDN_DK), lambda i: (0, i + off))
    out = pl.BlockSpec((1, t, GDN_DK), lambda i: (i, 0, 0))
    return pl.pallas_call(
        body, grid=(GDN_HEADS,), in_specs=[hs(0), hs(8), hs(16), ws(0), ws(8), ws(16)], out_specs=[out, out, out],
        out_shape=[jax.ShapeDtypeStruct((GDN_HEADS, t, GDN_DK), F32)] * 3,
        name=name, compiler_params=_cp())(h, h, h, cw, cw, cw)


def _gdn_prep_bwd(h, cw, dq, dk, dv, *, name):
    t = h.shape[0]
    qscale = GDN_DK ** -0.5

    def body(hq_ref, hk_ref, hv_ref, wq_ref, wk_ref, wv_ref, dq_ref, dk_ref, dv_ref,
             dhq_ref, dhk_ref, dhv_ref, dwq_ref, dwk_ref, dwv_ref):
        row = lax.broadcasted_iota(jnp.int32, (t, GDN_DK), 0)

        def one(h_ref, w_ref, d_ref, dh_ref, dw_ref, norm, sc):
            u = h_ref[...]
            c = _dwconv(u, w_ref, row)
            d = d_ref[0]
            if norm:
                s = _silu(c)
                r = lax.rsqrt(jnp.sum(s * s, -1, keepdims=True) + 1e-6)
                n = s * r
                d = d * sc
                d = r * (d - n * jnp.sum(d * n, -1, keepdims=True))
            dc = d * _dsilu(c)
            dh_ref[...] = _dwconv_bwd(u, w_ref, dc, row, dw_ref).astype(BF16)

        one(hq_ref, wq_ref, dq_ref, dhq_ref, dwq_ref, True, qscale)
        one(hk_ref, wk_ref, dk_ref, dhk_ref, dwk_ref, True, 1.0)
        one(hv_ref, wv_ref, dv_ref, dhv_ref, dwv_ref, False, 1.0)

    hs = lambda off: pl.BlockSpec((t, GDN_DK), lambda i: (0, i + off))
    ws = lambda off: pl.BlockSpec((GDN_CONV, GDN_DK), lambda i: (0, i + off))
    hd = pl.BlockSpec((1, t, GDN_DK), lambda i: (i, 0, 0))
    return pl.pallas_call(
        body, grid=(GDN_HEADS,), in_specs=[hs(0), hs(8), hs(16), ws(0), ws(8), ws(16), hd, hd, hd],
        out_specs=[hs(0), hs(0), hs(0), ws(0), ws(0), ws(0)],
        out_shape=[jax.ShapeDtypeStruct((t, GDN_W), BF16)] * 3 + [jax.ShapeDtypeStruct((GDN_CONV, GDN_W), F32)] * 3,
        name=name, compiler_params=_cp())(h, h, h, cw, cw, cw, dq, dk, dv)


def _make_mm2(wide):
    def raw(a, b, dims):
        if wide:
            return lax.dot_general(a, b, (dims, ((), ())), precision=lax.Precision.HIGHEST, preferred_element_type=F32)
        return lax.dot_general(a.astype(BF16), b.astype(BF16), (dims, ((), ())), preferred_element_type=F32)

    @jax.custom_vjp
    def nn(a, b):
        return raw(a, b, ((1,), (0,)))

    @jax.custom_vjp
    def nt(a, b):
        return raw(a, b, ((1,), (1,)))

    @jax.custom_vjp
    def tn(a, b):
        return raw(a, b, ((0,), (0,)))

    nn.defvjp(lambda a, b: (nn(a, b), (a, b)), lambda r, g: (nt(g, r[1]), tn(r[0], g)))
    nt.defvjp(lambda a, b: (nt(a, b), (a, b)), lambda r, g: (nn(g, r[1]), tn(g, r[0])))
    tn.defvjp(lambda a, b: (tn(a, b), (a, b)), lambda r, g: (nt(r[1], g), nn(r[0], g)))
    return nn, nt, tn


_NN, _NT, _TN = _make_mm2(False)
_NNW, _NTW, _TNW = _make_mm2(True)


def _square_masks(c):
    ri = lax.broadcasted_iota(jnp.int32, (c, c), 0)
    ci = lax.broadcasted_iota(jnp.int32, (c, c), 1)
    return ri >= ci, ri > ci, ri == ci


def _cumsum_rows(m):
    tri, _, _ = _square_masks(m.shape[0])
    return _NNW(tri.astype(F32), m)


def _transpose_sq(m):
    _, _, eye = _square_masks(m.shape[0])
    return _NTW(eye.astype(F32), m)


@jax.custom_vjp
def _inv_unit_lower(l):
    c = l.shape[0]
    _, _, eye = _square_masks(c)
    p = -l
    t = eye.astype(F32) + p
    for _ in range(int(math.log2(c)) - 1):
        p = _NNW(p, p)
        t = t + _NNW(t, p)
    return t


def _inv_fwd(l):
    t = _inv_unit_lower(l)
    return t, t


def _inv_bwd(t, dt):
    return (-_NTW(_TNW(t, dt), t),)


_inv_unit_lower.defvjp(_inv_fwd, _inv_bwd)


def _softplus(x):
    return jnp.maximum(x, 0.0) + jnp.log1p(jnp.exp(-jnp.abs(x)))


def _gdn_chunk(q, k, v, braw, araw, alog, dtb, state):
    c = q.shape[0]
    dv = v.shape[1]
    tri, strict, _ = _square_masks(c)
    beta = _sig(braw)
    g = -jnp.exp(alog) * _softplus(araw + dtb)
    gcm = _cumsum_rows(g * jnp.ones((c, c), F32))
    gct = _transpose_sq(gcm)
    decay = jnp.where(tri, jnp.exp(jnp.where(tri, gcm - gct, 0.0)), 0.0)
    gc = jnp.sum(gcm, 1, keepdims=True) * (1.0 / c)
    glast = jnp.sum(g, 0, keepdims=True)
    egc = jnp.exp(gc)
    kb = k * beta
    tm = _inv_unit_lower(jnp.where(strict, _NT(kb, k) * decay, 0.0))
    sol = _NNW(tm, jnp.concatenate([v * beta, kb * egc], 1))
    u, w = sol[:, :dv], sol[:, dv:]
    attn = jnp.where(tri, _NT(q, k) * decay, 0.0)
    k_dec = k * jnp.exp(glast - gc)
    q_dec = q * egc
    v_new = u - _NN(w, state)
    o = _NN(q_dec, state) + _NN(attn, v_new)
    new_state = state * jnp.exp(glast) + _TN(k_dec, v_new)
    return o, new_state


def _gdn_specs(t, rev):
    nch = t // GDN_CHUNK
    cm = (lambda n: nch - 1 - n) if rev else (lambda n: n)
    tok = pl.BlockSpec((GDN_HEADS, GDN_CHUNK, GDN_DK), lambda n: (0, cm(n), 0))
    par = pl.BlockSpec((GDN_HEADS, 1, LANES), lambda n: (0, 0, 0))
    st = pl.BlockSpec((GDN_HEADS, 1, GDN_DK, GDN_DV), lambda n: (0, cm(n), 0, 0))
    return tok, par, st


def _gdn_core_fwd(q, k, v, bb, ab, alog, dtb, *, name):
    t = q.shape[1]
    nch = t // GDN_CHUNK

    def body(q_ref, k_ref, v_ref, bb_ref, ab_ref, al_ref, dt_ref, o_ref, st_ref, state):
        @pl.when(pl.program_id(0) == 0)
        def _():
            state[...] = jnp.zeros_like(state)

        s0 = state[...]
        st_ref[:, 0] = s0
        o, s1 = jax.vmap(_gdn_chunk)(q_ref[...], k_ref[...], v_ref[...], bb_ref[:, :, 0:1], ab_ref[:, :, 0:1],
                                     al_ref[:, :, 0:1], dt_ref[:, :, 0:1], s0)
        o_ref[...] = o
        state[...] = s1

    tok, par, st = _gdn_specs(t, False)
    return pl.pallas_call(
        body, grid=(nch,), in_specs=[tok, tok, tok, tok, tok, par, par], out_specs=[tok, st],
        out_shape=[jax.ShapeDtypeStruct((GDN_HEADS, t, GDN_DV), F32),
                   jax.ShapeDtypeStruct((GDN_HEADS, nch, GDN_DK, GDN_DV), F32)],
        scratch_shapes=[pltpu.VMEM((GDN_HEADS, GDN_DK, GDN_DV), F32)],
        name=name, compiler_params=_cp())(q, k, v, bb, ab, alog, dtb)


def _gdn_core_bwd(q, k, v, bb, ab, alog, dtb, states, do, *, name):
    t = q.shape[1]
    nch = t // GDN_CHUNK

    def body(q_ref, k_ref, v_ref, bb_ref, ab_ref, al_ref, dt_ref, st_ref, do_ref,
             dq_ref, dk_ref, dv_ref, dbb_ref, dab_ref, dal_ref, ddt_ref, dstate):
        @pl.when(pl.program_id(0) == 0)
        def _():
            dstate[...] = jnp.zeros_like(dstate)
            dal_ref[...] = jnp.zeros_like(dal_ref)
            ddt_ref[...] = jnp.zeros_like(ddt_ref)

        args = (q_ref[...], k_ref[...], v_ref[...], bb_ref[:, :, 0:1], ab_ref[:, :, 0:1],
                al_ref[:, :, 0:1], dt_ref[:, :, 0:1], st_ref[:, 0])
        _, pull = jax.vjp(jax.vmap(_gdn_chunk), *args)
        dq, dk, dv, dbr, dar, dal, ddt, ds = pull((do_ref[...], dstate[...]))
        dq_ref[...] = dq
        dk_ref[...] = dk
        dv_ref[...] = dv
        dbb_ref[...] = dbr + jnp.zeros((GDN_HEADS, GDN_CHUNK, LANES), F32)
        dab_ref[...] = dar + jnp.zeros((GDN_HEADS, GDN_CHUNK, LANES), F32)
        dal_ref[...] += dal + jnp.zeros((GDN_HEADS, 1, LANES), F32)
        ddt_ref[...] += ddt + jnp.zeros((GDN_HEADS, 1, LANES), F32)
        dstate[...] = ds

    tok, par, st = _gdn_specs(t, True)
    tokshape = jax.ShapeDtypeStruct((GDN_HEADS, t, GDN_DK), F32)
    parshape = jax.ShapeDtypeStruct((GDN_HEADS, 1, LANES), F32)
    return pl.pallas_call(
        body, grid=(nch,), in_specs=[tok, tok, tok, tok, tok, par, par, st, tok],
        out_specs=[tok, tok, tok, tok, tok, par, par],
        out_shape=[tokshape] * 5 + [parshape] * 2,
        scratch_shapes=[pltpu.VMEM((GDN_HEADS, GDN_DK, GDN_DV), F32)],
        name=name, compiler_params=_cp())(q, k, v, bb, ab, alog, dtb, states, do)


GDN_ROWS = 512


def _gdn_post_fwd(o, h, nw, *, name):
    t = o.shape[1]

    def body(o_ref, g_ref, nw_ref, y_ref):
        oo = o_ref[0]
        r = lax.rsqrt(jnp.mean(oo * oo, -1, keepdims=True) + EPS)
        y_ref[...] = (oo * r * nw_ref[...] * _silu(g_ref[...])).astype(BF16)

    return pl.pallas_call(
        body, grid=(GDN_HEADS, t // GDN_ROWS),
        in_specs=[pl.BlockSpec((1, GDN_ROWS, GDN_DV), lambda hh, i: (hh, i, 0)),
                  pl.BlockSpec((GDN_ROWS, GDN_DV), lambda hh, i: (i, 3 * GDN_HEADS + hh)),
                  pl.BlockSpec((1, GDN_DV), lambda hh, i: (0, 0))],
        out_specs=pl.BlockSpec((GDN_ROWS, GDN_DV), lambda hh, i: (i, hh)),
        out_shape=jax.ShapeDtypeStruct((t, GDN_W), BF16), name=name, compiler_params=_cp())(o, h, nw)


def _gdn_post_bwd(o, h, nw, dy, *, name):
    t = o.shape[1]

    def body(o_ref, g_ref, nw_ref, dy_ref, do_ref, dg_ref, dnw_ref):
        oo, gg, nw_, dy_ = o_ref[0], g_ref[...], nw_ref[...], dy_ref[...]
        r = lax.rsqrt(jnp.mean(oo * oo, -1, keepdims=True) + EPS)
        n = oo * r
        sg = _silu(gg)
        dg_ref[...] = (dy_ * n * nw_ * _dsilu(gg)).astype(BF16)
        dn = dy_ * sg * nw_
        do_ref[0] = r * (dn - n * jnp.mean(dn * n, -1, keepdims=True))

        @pl.when((pl.program_id(0) == 0) & (pl.program_id(1) == 0))
        def _():
            dnw_ref[...] = jnp.zeros_like(dnw_ref)

        dnw_ref[...] += jnp.sum(dy_ * sg * n, 0, keepdims=True)

    return pl.pallas_call(
        body, grid=(GDN_HEADS, t // GDN_ROWS),
        in_specs=[pl.BlockSpec((1, GDN_ROWS, GDN_DV), lambda hh, i: (hh, i, 0)),
                  pl.BlockSpec((GDN_ROWS, GDN_DV), lambda hh, i: (i, 3 * GDN_HEADS + hh)),
                  pl.BlockSpec((1, GDN_DV), lambda hh, i: (0, 0)),
                  pl.BlockSpec((GDN_ROWS, GDN_DV), lambda hh, i: (i, hh))],
        out_specs=[pl.BlockSpec((1, GDN_ROWS, GDN_DV), lambda hh, i: (hh, i, 0)),
                   pl.BlockSpec((GDN_ROWS, GDN_DV), lambda hh, i: (i, hh)),
                   pl.BlockSpec((1, GDN_DV), lambda hh, i: (0, 0))],
        out_shape=[jax.ShapeDtypeStruct((GDN_HEADS, t, GDN_DV), F32), jax.ShapeDtypeStruct((t, GDN_W), BF16),
                   jax.ShapeDtypeStruct((1, GDN_DV), F32)],
        name=name, compiler_params=_cp())(o, h, nw, dy)


def _tables(positions):
    pos = positions.astype(F32)[:, None]
    half = RET_DK // 2
    inv = jnp.power(RET_THETA, -jnp.arange(half, dtype=F32) * 2.0 / RET_DK)
    ang = pos * inv
    cos, sin = jnp.cos(ang), jnp.sin(ang)
    c2a = jnp.concatenate([cos, cos], 1)
    s2a = jnp.concatenate([-sin, sin], 1)
    hb = ROPE_DIMS // 2
    invb = jnp.power(ROPE_THETA, -jnp.arange(hb, dtype=F32) * 2.0 / ROPE_DIMS)
    angb = pos * invb
    cosb, sinb = jnp.cos(angb), jnp.sin(angb)
    t = pos.shape[0]
    ones = jnp.ones((t, DIL_HD - ROPE_DIMS), F32)
    zeros = jnp.zeros((t, DIL_HD - ROPE_DIMS), F32)
    z8 = jnp.zeros((t, hb), F32)
    cb = jnp.concatenate([cosb, cosb, ones] * 2, 1)
    shi = jnp.concatenate([z8, sinb, zeros] * 2, 1)
    slo = jnp.concatenate([-sinb, z8, zeros] * 2, 1)
    lg = jnp.log1p(-jnp.power(2.0, -5.0 - jnp.arange(RET_HEADS, dtype=F32)))
    lgt = jnp.broadcast_to(lg[:, None, None], (RET_HEADS, 1, LANES))
    delta = jnp.arange(ATT_BLK, dtype=jnp.int32)[:, None] + (SEQ - ATT_BLK) - jnp.arange(SEQ, dtype=jnp.int32)[None, :]
    cnt = jnp.zeros(delta.shape, F32)
    for (w, d) in DIL_PAIRS:
        cnt = cnt + ((delta >= 0) & (delta <= w) & (delta % d == 0)).astype(F32)
    strip = jnp.where(cnt > 0, jnp.log(jnp.maximum(cnt, 1.0)), NEG)
    return c2a, s2a, cb, shi, slo, lgt, strip


def _local_step(x, positions, target, get_w, mid, put_g, small):
    c2a, s2a, cb, shi, slo, lgt, strip = _tables(positions)
    t = x.shape[0]
    saved = []
    xf = x
    xb = x.astype(BF16)
    for layer in range(DEPTH):
        j = layer // 2
        L = f"L{layer}_"
        W, dep = get_w(layer, "mixer", xb)
        rec = {"x": xf, "xb": xb}
        if layer % 2 == 0:
            h = _mm(xb, W["in_t"], tb=True, name=L + "ev_in", dep=dep)
            ro, ya = _ret_fwd(h, c2a, s2a, lgt, name=L + "ret_fwd")
            do_, yb, lse = _dil_fwd(h, cb, shi, slo, strip, name=L + "dil_fwd")
            y = jnp.concatenate([ya, yb], 1)
            mix = _mm(y, W["out"], name=L + "ev_out", dep=mid(layer, "mixer", y))
            rec.update(h=h, ro=ro, dil_o=do_, lse=lse, y=y)
        else:
            h = _mm(xb, W["in_t"], tb=True, name=L + "od_in", dep=dep)
            cw = W["conv"]
            q, k, v = _gdn_prep_fwd(h, cw, name=L + "gdn_prep")
            hs = h[:, 4 * GDN_W:4 * GDN_W + 2 * GDN_HEADS]
            bb = jnp.broadcast_to(hs[:, :GDN_HEADS].T[:, :, None], (GDN_HEADS, t, LANES))
            ab = jnp.broadcast_to(hs[:, GDN_HEADS:].T[:, :, None], (GDN_HEADS, t, LANES))
            alog = jnp.broadcast_to(small["od_a_log"][j][:, None, None], (GDN_HEADS, 1, LANES))
            dtb = jnp.broadcast_to(small["od_dt_bias"][j][:, None, None], (GDN_HEADS, 1, LANES))
            o, states = _gdn_core_fwd(q, k, v, bb, ab, alog, dtb, name=L + "gdn_fwd")
            nw = small["od_norm_w"][j][None, :]
            y = _gdn_post_fwd(o, h, nw, name=L + "gdn_post")
            mix = _mm(y, W["out"], name=L + "od_out", dep=mid(layer, "mixer", y))
            rec.update(h=h, q=q, k=k, v=v, bb=bb, ab=ab, alog=alog, dtb=dtb, states=states, o=o, y=y, nw=nw, cw=cw)
        z1, x1, x1b = _ln_fwd(xf, mix, small["ln1_g"][layer][None], small["ln1_b"][layer][None], name=L + "ln1")
        rec["Wm"] = W
        W, dep = get_w(layer, "ffn", x1b)
        rec["Wf"] = W
        u = _mm(x1b, W["up_t"], tb=True, name=L + "ffn_up", dep=dep)
        fcw = W["fconv"]
        fcb = small["ffn_conv_b"][layer][None]
        a = _ffn_mid_fwd(u, fcw, fcb, name=L + "ffn_mid")
        f = _mm(a, W["down"], name=L + "ffn_down", dep=mid(layer, "ffn", a))
        z2, x2, x2b = _ln_fwd(x1, f, small["ln2_g"][layer][None], small["ln2_b"][layer][None], name=L + "ln2")
        rec.update(z1=z1, x1b=x1b, u=u, a=a, z2=z2, fcw=fcw, fcb=fcb)
        saved.append(rec)
        xf, xb = x2, x2b

    dy, lossv = _loss_head(xf, target, name="loss_head")
    loss = lossv[0, 0]

    gS = {n: [None] * small[n].shape[0] for n in small}
    dres, dmm = dy, None
    for layer in reversed(range(DEPTH)):
        j = layer // 2
        L = f"L{layer}_"
        rec = saved[layer]
        Wm, Wf = rec["Wm"], rec["Wf"]
        g = {}
        if dmm is None:
            dz2, dz2b, dg2, db2 = _ln_bwd(rec["z2"], small["ln2_g"][layer][None], dres, None, name=L + "ln2_bwd")
        else:
            dz2, dz2b, dg2, db2 = _ln_bwd(rec["z2"], small["ln2_g"][layer][None], dmm, dres, name=L + "ln2_bwd")
        gS["ln2_g"][layer], gS["ln2_b"][layer] = dg2[0], db2[0]
        g["down"] = _mm(rec["a"], dz2b, ta=True, name=L + "ffn_down_dw", out_dtype=BF16)
        da = _mm(dz2b, Wf["down"], tb=True, name=L + "ffn_down_dx")
        du, dcw, dcb = _ffn_mid_bwd(rec["u"], rec["fcw"], rec["fcb"], da, name=L + "ffn_mid_bwd")
        g["fconv"] = dcw.astype(BF16)
        gS["ffn_conv_b"][layer] = dcb[0]
        g["up_t"] = _mm(du, rec["x1b"], ta=True, name=L + "ffn_up_dw", out_dtype=BF16)
        dep = put_g(layer, "ffn", g)
        dx1 = _mm(du, Wf["up_t"], name=L + "ffn_up_dx", dep=dep)
        dz1, dz1b, dg1, db1 = _ln_bwd(rec["z1"], small["ln1_g"][layer][None], dx1, dz2, name=L + "ln1_bwd")
        gS["ln1_g"][layer], gS["ln1_b"][layer] = dg1[0], db1[0]
        g = {}
        if layer % 2 == 0:
            g["out"] = _mm(rec["y"], dz1b, ta=True, name=L + "ev_out_dw", out_dtype=BF16)
            dyy = _mm(dz1b, Wm["out"], tb=True, name=L + "ev_out_dx")
            dqa, dka, dva, dga = _ret_bwd(rec["h"], c2a, s2a, lgt, rec["ro"], dyy, name=L + "ret_bwd")
            dqb, dkb, dvb = _dil_bwd(rec["h"], cb, shi, slo, strip, rec["dil_o"], rec["lse"], dyy, name=L + "dil_bwd")
            dh = jnp.concatenate([dqa, dka, dva, dga, dqb, dkb, dvb], 1)
            g["in_t"] = _mm(dh, rec["xb"], ta=True, name=L + "ev_in_dw", out_dtype=BF16)
            dep = put_g(layer, "mixer", g)
            dxin = _mm(dh, Wm["in_t"], name=L + "ev_in_dx", dep=dep)
        else:
            g["out"] = _mm(rec["y"], dz1b, ta=True, name=L + "od_out_dw", out_dtype=BF16)
            dyy = _mm(dz1b, Wm["out"], tb=True, name=L + "od_out_dx")
            do, dgate, dnw = _gdn_post_bwd(rec["o"], rec["h"], rec["nw"], dyy, name=L + "gdn_post_bwd")
            gS["od_norm_w"][j] = dnw[0]
            dq, dk, dv, dbb, dab, dal, ddt = _gdn_core_bwd(
                rec["q"], rec["k"], rec["v"], rec["bb"], rec["ab"], rec["alog"], rec["dtb"], rec["states"], do,
                name=L + "gdn_bwd")
            gS["od_a_log"][j] = dal[:, 0, 0]
            gS["od_dt_bias"][j] = ddt[:, 0, 0]
            dhq, dhk, dhv, dwq, dwk, dwv = _gdn_prep_bwd(rec["h"], rec["cw"], dq, dk, dv, name=L + "gdn_prep_bwd")
            g["conv"] = jnp.concatenate([dwq, dwk, dwv], 1).astype(BF16)
            dsm = jnp.concatenate([dbb[:, :, 0].T, dab[:, :, 0].T,
                                   jnp.zeros((t, LANES - 2 * GDN_HEADS), F32)], 1).astype(BF16)
            dh = jnp.concatenate([dhq, dhk, dhv, dgate, dsm], 1)
            g["in_t"] = _mm(dh, rec["xb"], ta=True, name=L + "od_in_dw", out_dtype=BF16)
            dep = put_g(layer, "mixer", g)
            dxin = _mm(dh, Wm["in_t"], name=L + "od_in_dx", dep=dep)
        dres, dmm = dz1, dxin
    grad_x = _axpy(dmm, dres, name="grad_x")
    gS = {n: jnp.stack(v) for n, v in gS.items()}
    return loss, grad_x, gS


HBM = pl.BlockSpec(memory_space=pltpu.HBM)


def _me():
    return lax.axis_index("x"), lax.axis_index("y"), lax.axis_index("c")


def _my_index():
    x, y, c = _me()
    return 4 * x + 2 * y + c


SEM = pl.BlockSpec(memory_space=pltpu.SEMAPHORE)
ANY = pl.BlockSpec(memory_space=pl.ANY)
PLANS = {"scatter": (1, 2, 3, 4, 5, 6, 7), "spread": (1, 2, 4, 6), "relay": (2, 4, 6)}
SIBLING = 1


def _peer(kk):
    x, y, c = _me()
    return x ^ (kk >> 2), y ^ ((kk >> 1) & 1), c ^ (kk & 1)


def _peer_index(kk):
    px, py, pc = _peer(kk)
    return 4 * px + 2 * py + pc


def _job_copies(mode, srcs, lands, send_sems, recv_sems, incoming):
    myid = _my_index()
    plan = PLANS[mode]
    out = []
    for a in range(len(lands)):
        for idx, kk in enumerate(plan):
            if mode == "relay":
                to, src = _peer(SIBLING), lands[a].at[_peer_index(kk)]
                slot_there, slot_here = _peer_index(kk), _peer_index(kk ^ SIBLING)
            else:
                to, src = _peer(kk), (srcs[a] if mode == "spread" else srcs[a].at[_peer_index(kk)])
                slot_there, slot_here = myid, _peer_index(kk)
            sem = a * len(plan) + idx
            out.append(pltpu.make_async_remote_copy(
                src_ref=src, dst_ref=lands[a].at[slot_here if incoming else slot_there],
                send_sem=send_sems.at[sem], recv_sem=recv_sems.at[sem], device_id=to, device_id_type=MESH))
    return out


def _split_jobs(jobs, arrays):
    out, o = [], 0
    for (_, srcs, lands) in jobs:
        out.append((arrays[o:o + len(srcs)], arrays[o + len(srcs):o + len(srcs) + len(lands)]))
        o += len(srcs) + len(lands)
    return out


def _exchange_start(jobs, after, *, name):
    jobs = [(mode, list(srcs), [lax.empty((N_DEV, *s.shape) if mode == "spread" else s.shape, s.dtype) for s in srcs]
             if lands is None else list(lands)) for (mode, srcs, lands) in jobs]
    flat = [a for (_, srcs, lands) in jobs for a in (*srcs, *lands)]
    n, nj = len(flat), len(jobs)
    nsem = [len(PLANS[mode]) * len(lands) for (mode, _, lands) in jobs]

    def body(*refs):
        o = n + (0 if after is None else 1)
        sems, token = refs[o:o + 2 * nj], refs[o + 2 * nj + n]
        for ji, ((mode, _, _), (src, land)) in enumerate(zip(jobs, _split_jobs(jobs, refs[:n]))):
            for cp in _job_copies(mode, src, land, sems[2 * ji], sems[2 * ji + 1], False):
                cp.start()
        token[...] = jnp.zeros_like(token)

    outs = pl.pallas_call(
        body, name=name,
        out_shape=(*[pltpu.SemaphoreType.DMA((ns,)) for ns in nsem for _ in range(2)],
                   *[pltpu.HBM(a.shape, a.dtype) for a in flat], jax.ShapeDtypeStruct((8, LANES), F32)),
        in_specs=[HBM] * n + ([] if after is None else [ANY]),
        out_specs=(*[SEM] * (2 * nj), *[HBM] * n, pl.BlockSpec(memory_space=pltpu.VMEM)),
        input_output_aliases={i: 2 * nj + i for i in range(n)},
        compiler_params=pltpu.CompilerParams(has_side_effects=pltpu.SideEffectType.DATAFLOW_SIDE_EFFECTING),
    )(*[pltpu.with_memory_space_constraint(a, pltpu.HBM) for a in flat], *([] if after is None else [after]))
    thru = _split_jobs(jobs, list(outs[2 * nj:2 * nj + n]))
    started = [(mode, outs[2 * ji], outs[2 * ji + 1], src, land) for ji, ((mode, _, _), (src, land)) in enumerate(zip(jobs, thru))]
    return started, outs[2 * nj + n]


def _exchange_wait(started, after, *, name):
    jobs = [(mode, srcs, lands) for (mode, _, _, srcs, lands) in started]
    flat = [a for (_, srcs, lands) in jobs for a in (*srcs, *lands)]
    n, nj = len(flat), len(jobs)

    def body(*refs):
        sems = refs[n:n + 2 * nj]
        for ji, ((mode, _, _), (src, land)) in enumerate(zip(jobs, _split_jobs(jobs, refs[:n]))):
            for cp in _job_copies(mode, src, land, sems[2 * ji], sems[2 * ji + 1], True):
                cp.wait_send()
                cp.wait_recv()

    outs = pl.pallas_call(
        body, name=name, out_shape=tuple(pltpu.HBM(a.shape, a.dtype) for a in flat),
        in_specs=[HBM] * n + [SEM] * (2 * nj) + [ANY], out_specs=tuple([HBM] * n),
        input_output_aliases={i: i for i in range(n)},
        compiler_params=pltpu.CompilerParams(has_side_effects=pltpu.SideEffectType.DATAFLOW_SIDE_EFFECTING),
    )(*flat, *[s for (_, ss, rs, _, _) in started for s in (ss, rs)], after)
    return _split_jobs(jobs, list(outs))


def _sum8(land, *, name):
    _, rr, cc = land.shape
    tr = _row_tile(rr)

    def body(l_ref, o_ref):
        acc = l_ref[0].astype(F32)
        for d in range(1, N_DEV):
            acc = acc + l_ref[d].astype(F32)
        o_ref[...] = acc

    return pl.pallas_call(
        body, grid=(rr // tr,), in_specs=[pl.BlockSpec((N_DEV, tr, cc), lambda i: (0, i, 0))],
        out_specs=pl.BlockSpec((tr, cc), lambda i: (i, 0)), out_shape=jax.ShapeDtypeStruct((rr, cc), F32),
        name=name, compiler_params=_cp())(land)


def _row_tile(rr):
    for cand in (512, 384, 256, 192, 176, 128, 64, 32, 16, 8):
        if rr % cand == 0:
            return cand
    return rr


def _small_exchange(vec, *, name):
    rr = vec.shape[0]

    def body(v_ref, o_ref, send_sems, recv_sems):
        x, y, c = _me()
        myid = 4 * x + 2 * y + c
        o_ref[myid] = v_ref[...]
        cps = []
        for kk in range(1, N_DEV):
            px, py, pc = x ^ (kk >> 2), y ^ ((kk >> 1) & 1), c ^ (kk & 1)
            cps.append(pltpu.make_async_remote_copy(
                src_ref=v_ref, dst_ref=o_ref.at[myid], send_sem=send_sems.at[kk], recv_sem=recv_sems.at[kk],
                device_id=(px, py, pc), device_id_type=MESH))
        for cp in cps:
            cp.start()
        for kk in range(1, N_DEV):
            px, py, pc = x ^ (kk >> 2), y ^ ((kk >> 1) & 1), c ^ (kk & 1)
            pltpu.make_async_remote_copy(
                src_ref=v_ref, dst_ref=o_ref.at[4 * px + 2 * py + pc], send_sem=send_sems.at[kk],
                recv_sem=recv_sems.at[kk], device_id=(px, py, pc), device_id_type=MESH).wait_recv()
        for cp in cps:
            cp.wait_send()

    return pl.pallas_call(
        body, in_specs=[pl.BlockSpec(memory_space=pltpu.VMEM)], out_specs=pl.BlockSpec(memory_space=pltpu.VMEM),
        out_shape=jax.ShapeDtypeStruct((N_DEV, rr, LANES), F32),
        scratch_shapes=[pltpu.SemaphoreType.DMA((N_DEV,)), pltpu.SemaphoreType.DMA((N_DEV,))],
        name=name, compiler_params=pltpu.CompilerParams(has_side_effects=True))(vec)


def _adam_math(w, g, m, v):
    m = ADAM_B1 * m + (1.0 - ADAM_B1) * g
    v = ADAM_B2 * v + (1.0 - ADAM_B2) * (g * g)
    m_hat = m / (1.0 - ADAM_B1 ** ADAM_STEP)
    v_hat = v / (1.0 - ADAM_B2 ** ADAM_STEP)
    delta = -ADAM_LR * (m_hat / (jnp.sqrt(v_hat) + ADAM_EPS) + ADAM_WD * w)
    return delta, m, v


def _adamw_sharded(w, m, v, g, *, name):
    ll, rr, cc = w.shape
    tr = _row_tile(rr)

    def body(w_ref, m_ref, v_ref, g_ref, d_ref, nm_ref, nv_ref):
        d, nm, nv = _adam_math(w_ref[...], g_ref[...], m_ref[...], v_ref[...])
        d_ref[...] = d
        nm_ref[...] = nm
        nv_ref[...] = nv

    blk = pl.BlockSpec((1, tr, cc), lambda l, i: (l, i, 0))
    sh = jax.ShapeDtypeStruct((ll, rr, cc), F32)
    return pl.pallas_call(
        body, grid=(ll, rr // tr), in_specs=[blk] * 4, out_specs=[blk] * 3, out_shape=[sh] * 3,
        name=name, compiler_params=_cp())(w, m, v, g)


def _adamw_small(w, m, v, gall, *, name):
    rr = w.shape[0]

    def body(w_ref, m_ref, v_ref, g_ref, go_ref, d_ref, nm_ref, nv_ref):
        g = g_ref[0]
        for kk in range(1, N_DEV):
            g = g + g_ref[kk]
        d, nm, nv = _adam_math(w_ref[...], g, m_ref[...], v_ref[...])
        go_ref[...] = g
        d_ref[...] = d
        nm_ref[...] = nm
        nv_ref[...] = nv

    sh = jax.ShapeDtypeStruct((rr, LANES), F32)
    return pl.pallas_call(body, out_shape=[sh] * 4, name=name, compiler_params=_cp())(w, m, v, gall)


SHARDED = ("ev_w_in", "ev_w_out", "od_w_in", "od_conv_w", "od_w_out", "ffn_w_up", "ffn_conv_w", "ffn_w_down")
SMALL = ("od_a_log", "od_dt_bias", "od_norm_w", "ffn_conv_b", "ln1_g", "ln1_b", "ln2_g", "ln2_b")
ALL_W = ("ev_w_in", "ev_w_out", "od_w_in", "od_conv_w", "od_a_log", "od_dt_bias", "od_norm_w", "od_w_out",
         "ffn_w_up", "ffn_conv_w", "ffn_conv_b", "ffn_w_down", "ln1_g", "ln1_b", "ln2_g", "ln2_b")


def _layer_items(layer):
    j = layer // 2
    if layer % 2 == 0:
        mixer = [("in_t", "ev_w_in", j, "colT"), ("out", "ev_w_out", j, "row")]
    else:
        mixer = [("in_t", "od_w_in", j, "colT"), ("conv", "od_conv_w", j, "colsmall"), ("out", "od_w_out", j, "row")]
    return mixer + [("up_t", "ffn_w_up", layer, "colT"), ("fconv", "ffn_conv_w", layer, "colsmall"),
                    ("down", "ffn_w_down", layer, "row")]


def _to_send(kind, shard):
    if kind == "colT":
        return shard.T.astype(BF16)
    return shard.astype(BF16) if kind == "row" else shard


def _from_gather(kind, name, g):
    if kind == "colsmall":
        return jnp.transpose(g, (1, 0, 2)).reshape(g.shape[1], -1)
    full = g.reshape(-1, g.shape[-1])
    if name == "od_w_in":
        full = jnp.pad(full, ((0, OD_IN_PAD - OD_IN), (0, 0)))
    return full


def _by_owner(kind, name, gfull):
    if kind == "colsmall":
        kk, c8 = gfull.shape
        return jnp.transpose(gfull.reshape(kk, N_DEV, c8 // N_DEV), (1, 0, 2))
    if name == "od_w_in":
        gfull = gfull[:OD_IN]
    return gfull.reshape(N_DEV, gfull.shape[0] // N_DEV, gfull.shape[1])


def _pack_small(d):
    flat = jnp.concatenate([d[n].reshape(-1) for n in SMALL])
    pad = (-flat.shape[0]) % (8 * LANES)
    return jnp.pad(flat, (0, pad)).reshape(-1, LANES)


def _unpack_small(packed, like):
    flat = packed.reshape(-1)
    out, off = {}, 0
    for n in SMALL:
        sz = int(np.prod(like[n].shape))
        out[n] = flat[off:off + sz].reshape(like[n].shape)
        off += sz
    return out


def kernel(x, positions, ev_w_in, ev_w_out, od_w_in, od_conv_w, od_a_log, od_dt_bias, od_norm_w, od_w_out, ffn_w_up, ffn_conv_w, ffn_conv_b, ffn_w_down, ln1_g, ln1_b, ln2_g, ln2_b, loss_target, m_ev_w_in, m_ev_w_out, m_od_w_in, m_od_conv_w, m_od_a_log, m_od_dt_bias, m_od_norm_w, m_od_w_out, m_ffn_w_up, m_ffn_conv_w, m_ffn_conv_b, m_ffn_w_down, m_ln1_g, m_ln1_b, m_ln2_g, m_ln2_b, v_ev_w_in, v_ev_w_out, v_od_w_in, v_od_conv_w, v_od_a_log, v_od_dt_bias, v_od_norm_w, v_od_w_out, v_ffn_w_up, v_ffn_conv_w, v_ffn_conv_b, v_ffn_w_down, v_ln1_g, v_ln1_b, v_ln2_g, v_ln2_b):
    w = dict(ev_w_in=ev_w_in, ev_w_out=ev_w_out, od_w_in=od_w_in, od_conv_w=od_conv_w, od_a_log=od_a_log,
             od_dt_bias=od_dt_bias, od_norm_w=od_norm_w, od_w_out=od_w_out, ffn_w_up=ffn_w_up, ffn_conv_w=ffn_conv_w,
             ffn_conv_b=ffn_conv_b, ffn_w_down=ffn_w_down, ln1_g=ln1_g, ln1_b=ln1_b, ln2_g=ln2_g, ln2_b=ln2_b)
    mom = dict(ev_w_in=m_ev_w_in, ev_w_out=m_ev_w_out, od_w_in=m_od_w_in, od_conv_w=m_od_conv_w, od_a_log=m_od_a_log,
               od_dt_bias=m_od_dt_bias, od_norm_w=m_od_norm_w, od_w_out=m_od_w_out, ffn_w_up=m_ffn_w_up,
               ffn_conv_w=m_ffn_conv_w, ffn_conv_b=m_ffn_conv_b, ffn_w_down=m_ffn_w_down, ln1_g=m_ln1_g,
               ln1_b=m_ln1_b, ln2_g=m_ln2_g, ln2_b=m_ln2_b)
    var = dict(ev_w_in=v_ev_w_in, ev_w_out=v_ev_w_out, od_w_in=v_od_w_in, od_conv_w=v_od_conv_w, od_a_log=v_od_a_log,
               od_dt_bias=v_od_dt_bias, od_norm_w=v_od_norm_w, od_w_out=v_od_w_out, ffn_w_up=v_ffn_w_up,
               ffn_conv_w=v_ffn_conv_w, ffn_conv_b=v_ffn_conv_b, ffn_w_down=v_ffn_w_down, ln1_g=v_ln1_g,
               ln1_b=v_ln1_b, ln2_g=v_ln2_g, ln2_b=v_ln2_b)

    myid = _my_index()
    small = {n: w[n] for n in SMALL}
    groups = [(layer, part) for layer in range(DEPTH) for part in ("mixer", "ffn")]

    def group_items(gi):
        layer, part = groups[gi]
        its = _layer_items(layer)
        return its[:-3] if part == "mixer" else its[-3:]

    level1, level2 = {}, {}

    def spread_job(gi):
        return ("spread", [_to_send(kind, w[n][j]) for (_, n, j, kind) in group_items(gi)], None)

    def relay(gi, after, name):
        (srcs, lands), = _exchange_wait([level1.pop(gi)], after, name=name + "_wait")
        more = [spread_job(gi + 1)] if gi + 1 < len(groups) else []
        started, token = _exchange_start([("relay", [], lands)] + more, None, name=name + "_start")
        level2[gi] = (started[0], srcs)
        if more:
            level1[gi + 1] = started[1]
        return token

    def get_w(layer, part, after):
        gi = groups.index((layer, part))
        started, srcs = level2.pop(gi)
        (_, lands), = _exchange_wait([started], after, name=f"gather{gi}_wait")
        lands = [lax.dynamic_update_index_in_dim(l, s, myid, 0) for l, s in zip(lands, srcs)]
        return {key: _from_gather(kind, n, l) for (key, n, _, kind), l in zip(group_items(gi), lands)}, None

    def mid(layer, part, after):
        gi = groups.index((layer, part)) + 1
        return relay(gi, after, f"gather{gi}_relay") if gi < len(groups) else None

    landed = {}
    pending = []

    def scatter_finish(after):
        started, gi = pending.pop()
        (srcs, lands), = _exchange_wait([started], after, name=f"scatter{gi}_wait")
        for (key, _, _, _), l, s in zip(group_items(gi), lands, srcs):
            own = lax.dynamic_index_in_dim(s, myid, 0, keepdims=False)
            landed[(groups[gi][0], key)] = lax.dynamic_update_index_in_dim(l, own, myid, 0)

    def put_g(layer, part, g):
        gi = groups.index((layer, part))
        srcs = [_by_owner(kind, n, g[key]) for (key, n, _, kind) in group_items(gi)]
        (started,), token = _exchange_start([("scatter", srcs, None)], None, name=f"scatter{gi}_start")
        if pending:
            scatter_finish(token)
        pending.append((started, gi))
        return token

    (level1[0],), token = _exchange_start([spread_job(0)], None, name="gather0_spread_start")
    relay(0, token, "gather0_relay")
    loss, grad_x, gS = _local_step(x[0], positions[0], loss_target[0], get_w, mid, put_g, small)
    loss = lax.psum(loss, ("x", "y", "c"))

    outs_g, outs_d, outs_m, outs_v = {}, {}, {}, {}
    where = {n: [None] * w[n].shape[0] for n in SHARDED}
    for layer in range(DEPTH):
        for (key, n, j, kind) in _layer_items(layer):
            where[n][j] = (layer, key, kind)

    def update(n):
        parts = []
        for (layer, key, kind) in where[n]:
            gsh = _sum8(landed[(layer, key)], name=f"L{layer}_{key}_sum")
            parts.append(gsh.T if kind == "colT" else gsh)
        outs_g[n] = jnp.stack(parts)
        outs_d[n], outs_m[n], outs_v[n] = _adamw_sharded(w[n], mom[n], var[n], outs_g[n], name=f"adamw_{n}")

    last = {n for (_, n, _, _) in group_items(pending[0][1])}
    for n in SHARDED:
        if n not in last:
            update(n)
    scatter_finish(outs_d[[n for n in SHARDED if n not in last][-1]])
    for n in SHARDED:
        if n in last:
            update(n)

    gall = _small_exchange(_pack_small(gS), name="small_grads_exchange")
    g, d, nm, nv = _adamw_small(_pack_small({n: w[n] for n in SMALL}), _pack_small({n: mom[n] for n in SMALL}),
                                _pack_small({n: var[n] for n in SMALL}), gall, name="adamw_small")
    for dst, packed in ((outs_g, g), (outs_d, d), (outs_m, nm), (outs_v, nv)):
        dst.update(_unpack_small(packed, {n: w[n] for n in SMALL}))

    return (loss, grad_x[None], *[outs_g[n] for n in ALL_W], *[outs_d[n] for n in ALL_W],
            *[outs_m[n] for n in ALL_W], *[outs_v[n] for n in ALL_W])
```

```python
import functools
import math

import numpy as np
import jax
import jax.numpy as jnp
from jax import lax
from jax.experimental import pallas as pl
from jax.experimental.pallas import tpu as pltpu

F32 = jnp.float32
BF16 = jnp.bfloat16
MESH = pl.DeviceIdType.MESH

D_MODEL = 1024
SEQ = 2048
DEPTH = 4
N_DEV = 8
RET_HEADS, RET_DK, RET_DV = 4, 128, 256
RET_THETA = 10000.0
DIL_HEADS, DIL_HD = 8, 64
DIL_PAIRS = ((128, 1), (512, 4), (2048, 16))
ROPE_THETA = 500000.0
ROPE_DIMS = DIL_HD // 4
GDN_HEADS, GDN_DK, GDN_DV, GDN_CHUNK, GDN_CONV = 8, 128, 128, 64, 4
D_FF = 2816
FFN_CONV = 3
ALPHA = (2.0 * DEPTH) ** 0.25
EPS = 1e-5
RET_QK_W = RET_HEADS * RET_DK
RET_V_W = RET_HEADS * RET_DV
DIL_W = DIL_HEADS * DIL_HD
EV_IN = 2 * RET_QK_W + 2 * RET_V_W + 3 * DIL_W
EV_MIX = RET_V_W + DIL_W
GDN_W = GDN_HEADS * GDN_DK
OD_IN = 4 * GDN_W + 2 * GDN_HEADS
OD_IN_PAD = 4 * GDN_W + 128
ADAM_LR, ADAM_B1, ADAM_B2, ADAM_EPS, ADAM_WD, ADAM_STEP = 0.001, 0.9, 0.999, 1e-08, 0.01, 10

LANES = 128
VMEM_LIMIT = 56 * 1024 * 1024
ATT_BLK = 256
NEG = -1e30


def _cp(**kw):
    return pltpu.CompilerParams(vmem_limit_bytes=VMEM_LIMIT, **kw)


def _tile(n, cap):
    if n <= cap:
        return n
    best = None
    for t in range(LANES, cap + 1, LANES):
        if n % t == 0:
            best = t
    assert best is not None, (n, cap)
    return best


def _mm(a, b, *, ta=False, tb=False, name, out_dtype=F32, dep=None, tm=None, tn=None):
    m = a.shape[1] if ta else a.shape[0]
    k = a.shape[0] if ta else a.shape[1]
    n = b.shape[0] if tb else b.shape[1]
    assert (b.shape[1] if tb else b.shape[0]) == k
    assert a.dtype == BF16 and b.dtype == BF16
    if tn is None:
        tn = n if n <= 1024 else _tile(n, 512)
    if tm is None:
        tm = m if (tn < n and k <= 1024 and m <= 2048) else _tile(m, 512)
    dims = (((0 if ta else 1,), (1 if tb else 0,)), ((), ()))

    def body(a_ref, b_ref, *rest):
        o_ref = rest[-1]
        o_ref[...] = lax.dot_general(a_ref[...], b_ref[...], dims,
                                     preferred_element_type=F32).astype(o_ref.dtype)

    a_spec = pl.BlockSpec((k, tm), lambda i, j: (0, i)) if ta else pl.BlockSpec((tm, k), lambda i, j: (i, 0))
    b_spec = pl.BlockSpec((tn, k), lambda i, j: (j, 0)) if tb else pl.BlockSpec((k, tn), lambda i, j: (0, j))
    extra = [] if dep is None else [dep]
    return pl.pallas_call(
        body, grid=(m // tm, n // tn), in_specs=[a_spec, b_spec] + [pl.BlockSpec(memory_space=pl.ANY)] * len(extra),
        out_specs=pl.BlockSpec((tm, tn), lambda i, j: (i, j)),
        out_shape=jax.ShapeDtypeStruct((m, n), out_dtype), name=name, compiler_params=_cp())(a, b, *extra)


LN_ROWS = 256


def _ln_fwd(x, m, g, b, *, name):
    t, d = x.shape

    def body(x_ref, m_ref, g_ref, b_ref, z_ref, y_ref, yb_ref):
        z = ALPHA * x_ref[...] + m_ref[...]
        mu = jnp.mean(z, -1, keepdims=True)
        zc = z - mu
        var = jnp.mean(zc * zc, -1, keepdims=True)
        y = zc * lax.rsqrt(var + EPS) * g_ref[...] + b_ref[...]
        z_ref[...] = z
        y_ref[...] = y
        yb_ref[...] = y.astype(BF16)

    row = pl.BlockSpec((LN_ROWS, d), lambda i: (i, 0))
    vec = pl.BlockSpec((1, d), lambda i: (0, 0))
    return pl.pallas_call(
        body, grid=(t // LN_ROWS,), in_specs=[row, row, vec, vec], out_specs=[row, row, row],
        out_shape=[jax.ShapeDtypeStruct((t, d), F32), jax.ShapeDtypeStruct((t, d), F32),
                   jax.ShapeDtypeStruct((t, d), BF16)],
        name=name, compiler_params=_cp())(x, m, g, b)


def _ln_bwd(z, g, dya, dyb, *, name):
    t, d = z.shape
    two = dyb is not None

    def body(*refs):
        if two:
            z_ref, g_ref, dya_ref, dyb_ref, dz_ref, dzb_ref, dg_ref, db_ref = refs
            dy = dya_ref[...] + ALPHA * dyb_ref[...]
        else:
            z_ref, g_ref, dya_ref, dz_ref, dzb_ref, dg_ref, db_ref = refs
            dy = dya_ref[...]
        zz = z_ref[...]
        mu = jnp.mean(zz, -1, keepdims=True)
        zc = zz - mu
        var = jnp.mean(zc * zc, -1, keepdims=True)
        r = lax.rsqrt(var + EPS)
        xh = zc * r
        dxh = dy * g_ref[...]
        dz = r * (dxh - jnp.mean(dxh, -1, keepdims=True) - xh * jnp.mean(dxh * xh, -1, keepdims=True))
        dz_ref[...] = dz
        dzb_ref[...] = dz.astype(BF16)

        @pl.when(pl.program_id(0) == 0)
        def _():
            dg_ref[...] = jnp.zeros_like(dg_ref)
            db_ref[...] = jnp.zeros_like(db_ref)

        dg_ref[...] += jnp.sum(dy * xh, 0, keepdims=True)
        db_ref[...] += jnp.sum(dy, 0, keepdims=True)

    row = pl.BlockSpec((LN_ROWS, d), lambda i: (i, 0))
    vec = pl.BlockSpec((1, d), lambda i: (0, 0))
    ins = [z, g, dya] + ([dyb] if two else [])
    return pl.pallas_call(
        body, grid=(t // LN_ROWS,), in_specs=[row, vec, row] + ([row] if two else []),
        out_specs=[row, row, vec, vec],
        out_shape=[jax.ShapeDtypeStruct((t, d), F32), jax.ShapeDtypeStruct((t, d), BF16),
                   jax.ShapeDtypeStruct((1, d), F32), jax.ShapeDtypeStruct((1, d), F32)],
        name=name, compiler_params=_cp())(*ins)


def _axpy(a, b, *, name):
    t, d = a.shape

    def body(a_ref, b_ref, o_ref):
        o_ref[...] = a_ref[...] + ALPHA * b_ref[...]

    row = pl.BlockSpec((LN_ROWS, d), lambda i: (i, 0))
    return pl.pallas_call(body, grid=(t // LN_ROWS,), in_specs=[row, row], out_specs=row,
                          out_shape=jax.ShapeDtypeStruct((t, d), F32), name=name, compiler_params=_cp())(a, b)


def _loss_head(y, target, *, name):
    t, d = y.shape

    def body(y_ref, t_ref, dy_ref, l_ref):
        e = y_ref[...] - t_ref[...]
        dy_ref[...] = e * (1.0 / d)

        @pl.when(pl.program_id(0) == 0)
        def _():
            l_ref[...] = jnp.zeros_like(l_ref)

        l_ref[...] += jnp.zeros_like(l_ref) + 0.5 * jnp.sum(jnp.mean(e * e, -1, keepdims=True), 0, keepdims=True)

    row = pl.BlockSpec((LN_ROWS, d), lambda i: (i, 0))
    return pl.pallas_call(
        body, grid=(t // LN_ROWS,), in_specs=[row, row],
        out_specs=[row, pl.BlockSpec((1, LANES), lambda i: (0, 0))],
        out_shape=[jax.ShapeDtypeStruct((t, d), F32), jax.ShapeDtypeStruct((1, LANES), F32)],
        name=name, compiler_params=_cp())(y, target)


def _sig(x):
    return 1.0 / (1.0 + jnp.exp(-x))


def _silu(x):
    return x * _sig(x)


def _dsilu(x):
    s = _sig(x)
    return s * (1.0 + x * (1.0 - s))


def _shift_down(u, k, row):
    if k == 0:
        return u
    return jnp.where(row >= k, pltpu.roll(u, k, 0), 0.0)


def _shift_up(u, k, row):
    if k == 0:
        return u
    t = u.shape[0]
    return jnp.where(row < t - k, pltpu.roll(u, t - k, 0), 0.0)


def _dwconv(u, w_ref, row):
    kk = w_ref.shape[0]
    acc = None
    for j in range(kk):
        term = w_ref[j:j + 1, :] * _shift_down(u, kk - 1 - j, row)
        acc = term if acc is None else acc + term
    return acc


def _dwconv_bwd(u, w_ref, dc, row, dw_ref):
    kk = w_ref.shape[0]
    du = None
    for j in range(kk):
        term = w_ref[j:j + 1, :] * _shift_up(dc, kk - 1 - j, row)
        du = term if du is None else du + term
        dw_ref[j:j + 1, :] = jnp.sum(dc * _shift_down(u, kk - 1 - j, row), 0, keepdims=True)
    return du


CONV_ROWS = 64


def _rows(b):
    return pl.ds(pl.multiple_of(b * CONV_ROWS, CONV_ROWS), CONV_ROWS)


def _shifted_down(ref, b, k, row):
    cur = ref[_rows(b), :]
    if k == 0:
        return cur
    prev = jnp.where(b > 0, ref[_rows(jnp.maximum(b - 1, 0)), :], 0.0)
    return jnp.where(row >= k, pltpu.roll(cur, k, 0), pltpu.roll(prev, k, 0))


def _shifted_up(ref, b, k, row, nblk):
    cur = ref[_rows(b), :]
    if k == 0:
        return cur
    nxt = jnp.where(b < nblk - 1, ref[_rows(jnp.minimum(b + 1, nblk - 1)), :], 0.0)
    return jnp.where(row < CONV_ROWS - k, pltpu.roll(cur, CONV_ROWS - k, 0), pltpu.roll(nxt, CONV_ROWS - k, 0))


def _dwconv_blk(u_ref, w_ref, b, row):
    kk = w_ref.shape[0]
    views = [_shifted_down(u_ref, b, kk - 1 - j, row) for j in range(kk)]
    acc = None
    for j in range(kk):
        term = w_ref[j:j + 1, :] * views[j]
        acc = term if acc is None else acc + term
    return acc, views


def _dwconv_du_blk(dc_ref, w_ref, b, row, nblk):
    kk = w_ref.shape[0]
    du = None
    for j in range(kk):
        term = w_ref[j:j + 1, :] * _shifted_up(dc_ref, b, kk - 1 - j, row, nblk)
        du = term if du is None else du + term
    return du


FFN_TC = 256


def _ffn_mid_fwd(u, cw, cb, *, name):
    t = u.shape[0]
    nb = D_FF // FFN_TC

    def body(ug_ref, uv_ref, wg_ref, wv_ref, bg_ref, bv_ref, a_ref):
        row = lax.broadcasted_iota(jnp.int32, (t, FFN_TC), 0)
        cg = _dwconv(ug_ref[...], wg_ref, row) + bg_ref[...]
        cv = _dwconv(uv_ref[...], wv_ref, row) + bv_ref[...]
        a_ref[...] = (_silu(cg) * cv).astype(BF16)

    col = lambda off: pl.BlockSpec((t, FFN_TC), lambda j: (0, j + off))
    wsp = lambda off: pl.BlockSpec((FFN_CONV, FFN_TC), lambda j: (0, j + off))
    bsp = lambda off: pl.BlockSpec((1, FFN_TC), lambda j: (0, j + off))
    return pl.pallas_call(
        body, grid=(nb,), in_specs=[col(0), col(nb), wsp(0), wsp(nb), bsp(0), bsp(nb)],
        out_specs=pl.BlockSpec((t, FFN_TC), lambda j: (0, j)),
        out_shape=jax.ShapeDtypeStruct((t, D_FF), BF16), name=name, compiler_params=_cp())(u, u, cw, cw, cb, cb)


def _ffn_mid_bwd(u, cw, cb, da, *, name):
    t = u.shape[0]
    nb = D_FF // FFN_TC

    nblk = t // CONV_ROWS

    def body(ug_ref, uv_ref, wg_ref, wv_ref, bg_ref, bv_ref, da_ref,
             dug_ref, duv_ref, dwg_ref, dwv_ref, dbg_ref, dbv_ref, dcg_s, dcv_s):
        row = lax.broadcasted_iota(jnp.int32, (CONV_ROWS, FFN_TC), 0)
        zero = jnp.zeros((1, FFN_TC), F32)

        def first(b, acc):
            cg, ugs = _dwconv_blk(ug_ref, wg_ref, b, row)
            cv, uvs = _dwconv_blk(uv_ref, wv_ref, b, row)
            cg = cg + bg_ref[...]
            cv = cv + bv_ref[...]
            da_ = da_ref[_rows(b), :]
            dcv = da_ * _silu(cg)
            dcg = da_ * cv * _dsilu(cg)
            dcg_s[_rows(b), :] = dcg
            dcv_s[_rows(b), :] = dcv
            red = [jnp.sum(dcg * s, 0, keepdims=True) for s in ugs] + [jnp.sum(dcg, 0, keepdims=True)]
            red += [jnp.sum(dcv * s, 0, keepdims=True) for s in uvs] + [jnp.sum(dcv, 0, keepdims=True)]
            return tuple(a + r for a, r in zip(acc, red))

        acc = lax.fori_loop(0, nblk, first, (zero,) * (2 * FFN_CONV + 2))
        for j in range(FFN_CONV):
            dwg_ref[j:j + 1, :] = acc[j]
            dwv_ref[j:j + 1, :] = acc[FFN_CONV + 1 + j]
        dbg_ref[...] = acc[FFN_CONV]
        dbv_ref[...] = acc[2 * FFN_CONV + 1]

        def second(b, carry):
            dug_ref[_rows(b), :] = _dwconv_du_blk(dcg_s, wg_ref, b, row, nblk).astype(BF16)
            duv_ref[_rows(b), :] = _dwconv_du_blk(dcv_s, wv_ref, b, row, nblk).astype(BF16)
            return carry

        lax.fori_loop(0, nblk, second, 0)

    col = lambda off: pl.BlockSpec((t, FFN_TC), lambda j: (0, j + off))
    wsp = lambda off: pl.BlockSpec((FFN_CONV, FFN_TC), lambda j: (0, j + off))
    bsp = lambda off: pl.BlockSpec((1, FFN_TC), lambda j: (0, j + off))
    outs = pl.pallas_call(
        body, grid=(nb,), in_specs=[col(0), col(nb), wsp(0), wsp(nb), bsp(0), bsp(nb), col(0)],
        out_specs=[col(0), col(0), wsp(0), wsp(0), bsp(0), bsp(0)],
        out_shape=[jax.ShapeDtypeStruct((t, D_FF), BF16), jax.ShapeDtypeStruct((t, D_FF), BF16),
                   jax.ShapeDtypeStruct((FFN_CONV, D_FF), F32), jax.ShapeDtypeStruct((FFN_CONV, D_FF), F32),
                   jax.ShapeDtypeStruct((1, D_FF), F32), jax.ShapeDtypeStruct((1, D_FF), F32)],
        scratch_shapes=[pltpu.VMEM((t, FFN_TC), F32), pltpu.VMEM((t, FFN_TC), F32)],
        name=name, compiler_params=_cp())(u, u, cw, cw, cb, cb, da)
    dug, duv, dwg, dwv, dbg, dbv = outs
    return (jnp.concatenate([dug, duv], 1), jnp.concatenate([dwg, dwv], 1), jnp.concatenate([dbg, dbv], 1))


def _rot_a(x, c2, s2):
    return x * c2 + pltpu.roll(x, RET_DK // 2, 1) * s2


def _rot_a_t(dy, c2, s2):
    return dy * c2 + pltpu.roll(dy * s2, RET_DK // 2, 1)


def _decay_tile(lg, blk_diff):
    r = lax.broadcasted_iota(jnp.int32, (ATT_BLK, ATT_BLK), 0)
    c = lax.broadcasted_iota(jnp.int32, (ATT_BLK, ATT_BLK), 1)
    rel = r - c + blk_diff * ATT_BLK
    return jnp.where(rel >= 0, jnp.exp(jnp.maximum(rel, 0).astype(F32) * lg), 0.0)


def _nt(a, b):
    return lax.dot_general(a, b, (((1,), (1,)), ((), ())), preferred_element_type=F32)


def _nn(a, b):
    return lax.dot_general(a, b, (((1,), (0,)), ((), ())), preferred_element_type=F32)


def _tn(a, b):
    return lax.dot_general(a, b, (((0,), (0,)), ((), ())), preferred_element_type=F32)


def _ret_specs(t):
    q = pl.BlockSpec((t, RET_DK), lambda h: (0, h))
    k = pl.BlockSpec((t, RET_DK), lambda h: (0, RET_HEADS + h))
    v = pl.BlockSpec((t, RET_DV), lambda h: (0, RET_HEADS + h))
    g = pl.BlockSpec((t, RET_DV), lambda h: (0, 2 * RET_HEADS + h))
    tab = pl.BlockSpec((t, RET_DK), lambda h: (0, 0))
    lg = pl.BlockSpec((1, 1, LANES), lambda h: (h, 0, 0))
    return q, k, v, g, tab, lg


def _ret_fwd(h, c2, s2, lgt, *, name):
    t = h.shape[0]
    nblk = t // ATT_BLK
    scale = RET_DK ** -0.5

    def body(q_ref, k_ref, v_ref, g_ref, c_ref, s_ref, lg_ref, o_ref, ya_ref, qs, ks, vs):
        c2_, s2_ = c_ref[...], s_ref[...]
        qs[...] = _rot_a(q_ref[...], c2_, s2_).astype(BF16)
        ks[...] = (_rot_a(k_ref[...], c2_, s2_) * scale).astype(BF16)
        vs[...] = v_ref[...].astype(BF16)
        lg = lg_ref[0, :, 0:1]
        for i in range(nblk):
            qi = qs[pl.ds(i * ATT_BLK, ATT_BLK), :]
            acc = jnp.zeros((ATT_BLK, RET_DV), F32)
            for j in range(i + 1):
                sl = pl.ds(j * ATT_BLK, ATT_BLK)
                s = _nt(qi, ks[sl, :]) * _decay_tile(lg, i - j)
                acc = acc + _nn(s.astype(BF16), vs[sl, :])
            rows = pl.ds(i * ATT_BLK, ATT_BLK)
            o_ref[rows, :] = acc
            r = lax.rsqrt(jnp.mean(acc * acc, -1, keepdims=True) + EPS)
            ya_ref[rows, :] = (acc * r * _silu(g_ref[rows, :])).astype(BF16)

    q, k, v, g, tab, lg = _ret_specs(t)
    out = pl.BlockSpec((t, RET_DV), lambda hh: (0, hh))
    return pl.pallas_call(
        body, grid=(RET_HEADS,), in_specs=[q, k, v, g, tab, tab, lg], out_specs=[out, out],
        out_shape=[jax.ShapeDtypeStruct((t, RET_V_W), F32), jax.ShapeDtypeStruct((t, RET_V_W), BF16)],
        scratch_shapes=[pltpu.VMEM((t, RET_DK), BF16), pltpu.VMEM((t, RET_DK), BF16), pltpu.VMEM((t, RET_DV), BF16)],
        name=name, compiler_params=_cp())(h, h, h, h, c2, s2, lgt)


def _ret_bwd(h, c2, s2, lgt, o, dy, *, name):
    t = h.shape[0]
    nblk = t // ATT_BLK
    scale = RET_DK ** -0.5

    def body(q_ref, k_ref, v_ref, g_ref, c_ref, s_ref, lg_ref, o_ref, dy_ref,
             dq_ref, dk_ref, dv_ref, dg_ref, qs, ks, vs, dos, dka, dva):
        c2_, s2_ = c_ref[...], s_ref[...]
        qs[...] = _rot_a(q_ref[...], c2_, s2_).astype(BF16)
        ks[...] = (_rot_a(k_ref[...], c2_, s2_) * scale).astype(BF16)
        vs[...] = v_ref[...].astype(BF16)
        lg = lg_ref[0, :, 0:1]
        oo = o_ref[...]
        gg = g_ref[...]
        dya = dy_ref[...]
        r = lax.rsqrt(jnp.mean(oo * oo, -1, keepdims=True) + EPS)
        rn = oo * r
        dg_ref[...] = (dya * rn * _dsilu(gg)).astype(BF16)
        drn = dya * _silu(gg)
        dos[...] = (r * (drn - rn * jnp.mean(drn * rn, -1, keepdims=True))).astype(BF16)
        dka[...] = jnp.zeros_like(dka)
        dva[...] = jnp.zeros_like(dva)
        for i in range(nblk):
            rows = pl.ds(i * ATT_BLK, ATT_BLK)
            qi = qs[rows, :]
            doi = dos[rows, :]
            dqa = jnp.zeros((ATT_BLK, RET_DK), F32)
            for j in range(i + 1):
                sl = pl.ds(j * ATT_BLK, ATT_BLK)
                dt_ = _decay_tile(lg, i - j)
                kj = ks[sl, :]
                s = (_nt(qi, kj) * dt_).astype(BF16)
                ds = (_nt(doi, vs[sl, :]) * dt_).astype(BF16)
                dqa = dqa + _nn(ds, kj)
                dka[sl, :] += _tn(ds, qi)
                dva[sl, :] += _tn(s, doi)
            dq_ref[rows, :] = _rot_a_t(dqa, c_ref[rows, :], s_ref[rows, :]).astype(BF16)
        dk_ref[...] = (_rot_a_t(dka[...], c2_, s2_) * scale).astype(BF16)
        dv_ref[...] = dva[...].astype(BF16)

    q, k, v, g, tab, lg = _ret_specs(t)
    blk_v = pl.BlockSpec((t, RET_DV), lambda hh: (0, hh))
    blk_k = pl.BlockSpec((t, RET_DK), lambda hh: (0, hh))
    return pl.pallas_call(
        body, grid=(RET_HEADS,), in_specs=[q, k, v, g, tab, tab, lg, blk_v, blk_v],
        out_specs=[blk_k, blk_k, blk_v, blk_v],
        out_shape=[jax.ShapeDtypeStruct((t, RET_QK_W), BF16), jax.ShapeDtypeStruct((t, RET_QK_W), BF16),
                   jax.ShapeDtypeStruct((t, RET_V_W), BF16), jax.ShapeDtypeStruct((t, RET_V_W), BF16)],
        scratch_shapes=[pltpu.VMEM((t, RET_DK), BF16), pltpu.VMEM((t, RET_DK), BF16), pltpu.VMEM((t, RET_DV), BF16),
                        pltpu.VMEM((t, RET_DV), BF16), pltpu.VMEM((t, RET_DK), F32), pltpu.VMEM((t, RET_DV), F32)],
        name=name, compiler_params=_cp())(h, h, h, h, c2, s2, lgt, o, dy)


def _rot_b(x, cb, shi, slo):
    return x * cb + pltpu.roll(x, ROPE_DIMS // 2, 1) * shi + pltpu.roll(x, LANES - ROPE_DIMS // 2, 1) * slo


def _rot_b_t(dy, cb, shi, slo):
    return dy * cb + pltpu.roll(dy * shi, LANES - ROPE_DIMS // 2, 1) + pltpu.roll(dy * slo, ROPE_DIMS // 2, 1)


def _dil_specs(t):
    base = (2 * RET_QK_W + 2 * RET_V_W) // LANES
    npair = DIL_W // LANES
    q = pl.BlockSpec((t, LANES), lambda p: (0, base + p))
    k = pl.BlockSpec((t, LANES), lambda p: (0, base + npair + p))
    v = pl.BlockSpec((t, LANES), lambda p: (0, base + 2 * npair + p))
    tab = pl.BlockSpec((t, LANES), lambda p: (0, 0))
    strip = pl.BlockSpec((ATT_BLK, t), lambda p: (0, 0))
    pair = pl.BlockSpec((t, LANES), lambda p: (0, p))
    return q, k, v, tab, strip, pair


def _dil_fwd(h, cb, shi, slo, strip, *, name):
    t = h.shape[0]
    nblk = t // ATT_BLK
    scale = DIL_HD ** -0.5

    def body(q_ref, k_ref, v_ref, cb_ref, shi_ref, slo_ref, st_ref, o_ref, yb_ref, lse_ref, qs, ks, vs):
        cb_, shi_, slo_ = cb_ref[...], shi_ref[...], slo_ref[...]
        lane = lax.broadcasted_iota(jnp.int32, (t, LANES), 1)
        qr = _rot_b(q_ref[...], cb_, shi_, slo_) * scale
        qs[0] = jnp.where(lane < DIL_HD, qr, 0.0).astype(BF16)
        qs[1] = jnp.where(lane >= DIL_HD, qr, 0.0).astype(BF16)
        ks[...] = _rot_b(k_ref[...], cb_, shi_, slo_).astype(BF16)
        vs[...] = v_ref[...].astype(BF16)
        lane_b = lax.broadcasted_iota(jnp.int32, (ATT_BLK, LANES), 1)
        for i in range(nblk):
            w = (i + 1) * ATT_BLK
            rows = pl.ds(i * ATT_BLK, ATT_BLK)
            logc = st_ref[:, t - w:t]
            outs, lses = [], []
            for hd in range(2):
                s = _nt(qs[hd, rows, :], ks[0:w, :]) + logc
                m = jnp.max(s, -1, keepdims=True)
                p = jnp.exp(s - m)
                l = jnp.sum(p, -1, keepdims=True)
                outs.append(_nn(p.astype(BF16), vs[0:w, :]) / l)
                lses.append(m + jnp.log(l))
            o = jnp.where(lane_b < DIL_HD, outs[0], outs[1])
            o_ref[rows, :] = o
            yb_ref[rows, :] = o.astype(BF16)
            lse_ref[rows, :] = jnp.where(lane_b < DIL_HD, lses[0], lses[1])

    q, k, v, tab, strip_spec, pair = _dil_specs(t)
    return pl.pallas_call(
        body, grid=(DIL_W // LANES,), in_specs=[q, k, v, tab, tab, tab, strip_spec], out_specs=[pair, pair, pair],
        out_shape=[jax.ShapeDtypeStruct((t, DIL_W), F32), jax.ShapeDtypeStruct((t, DIL_W), BF16),
                   jax.ShapeDtypeStruct((t, DIL_W), F32)],
        scratch_shapes=[pltpu.VMEM((2, t, LANES), BF16), pltpu.VMEM((t, LANES), BF16), pltpu.VMEM((t, LANES), BF16)],
        name=name, compiler_params=_cp())(h, h, h, cb, shi, slo, strip)


def _dil_bwd(h, cb, shi, slo, strip, o, lse, dy, *, name):
    t = h.shape[0]
    nblk = t // ATT_BLK
    scale = DIL_HD ** -0.5

    def body(q_ref, k_ref, v_ref, cb_ref, shi_ref, slo_ref, st_ref, o_ref, lse_ref, dy_ref,
             dq_ref, dk_ref, dv_ref, qs, ks, vs, dos, dls, dka, dva):
        cb_, shi_, slo_ = cb_ref[...], shi_ref[...], slo_ref[...]
        lane = lax.broadcasted_iota(jnp.int32, (t, LANES), 1)
        qr = _rot_b(q_ref[...], cb_, shi_, slo_) * scale
        qs[0] = jnp.where(lane < DIL_HD, qr, 0.0).astype(BF16)
        qs[1] = jnp.where(lane >= DIL_HD, qr, 0.0).astype(BF16)
        ks[...] = _rot_b(k_ref[...], cb_, shi_, slo_).astype(BF16)
        vs[...] = v_ref[...].astype(BF16)
        do = dy_ref[...]
        prod = do * o_ref[...]
        d0 = jnp.sum(jnp.where(lane < DIL_HD, prod, 0.0), -1, keepdims=True)
        d1 = jnp.sum(jnp.where(lane >= DIL_HD, prod, 0.0), -1, keepdims=True)
        dls[...] = jnp.where(lane < DIL_HD, d0, d1)
        dos[0] = jnp.where(lane < DIL_HD, do, 0.0).astype(BF16)
        dos[1] = jnp.where(lane >= DIL_HD, do, 0.0).astype(BF16)
        dka[...] = jnp.zeros_like(dka)
        dva[...] = jnp.zeros_like(dva)
        lane_b = lax.broadcasted_iota(jnp.int32, (ATT_BLK, LANES), 1)
        for i in range(nblk):
            w = (i + 1) * ATT_BLK
            rows = pl.ds(i * ATT_BLK, ATT_BLK)
            logc = st_ref[:, t - w:t]
            dqs = []
            for hd in range(2):
                col = hd * DIL_HD
                qh = qs[hd, rows, :]
                doh = dos[hd, rows, :]
                lse_h = lse_ref[rows, col:col + 1]
                dl_h = dls[rows, col:col + 1]
                p = jnp.exp(_nt(qh, ks[0:w, :]) + logc - lse_h)
                dp = _nt(doh, vs[0:w, :])
                ds = (p * (dp - dl_h)).astype(BF16)
                dqs.append(_nn(ds, ks[0:w, :]))
                dka[0:w, :] += _tn(ds, qh)
                dva[0:w, :] += _tn(p.astype(BF16), doh)
            dq = jnp.where(lane_b < DIL_HD, dqs[0], dqs[1]) * scale
            dq_ref[rows, :] = _rot_b_t(dq, cb_ref[rows, :], shi_ref[rows, :], slo_ref[rows, :]).astype(BF16)
        dk_ref[...] = _rot_b_t(dka[...], cb_, shi_, slo_).astype(BF16)
        dv_ref[...] = dva[...].astype(BF16)

    q, k, v, tab, strip_spec, pair = _dil_specs(t)
    dy_spec = pl.BlockSpec((t, LANES), lambda p: (0, RET_V_W // LANES + p))
    return pl.pallas_call(
        body, grid=(DIL_W // LANES,), in_specs=[q, k, v, tab, tab, tab, strip_spec, pair, pair, dy_spec],
        out_specs=[pair, pair, pair],
        out_shape=[jax.ShapeDtypeStruct((t, DIL_W), BF16)] * 3,
        scratch_shapes=[pltpu.VMEM((2, t, LANES), BF16), pltpu.VMEM((t, LANES), BF16), pltpu.VMEM((t, LANES), BF16),
                        pltpu.VMEM((2, t, LANES), BF16), pltpu.VMEM((t, LANES), F32),
                        pltpu.VMEM((t, LANES), F32), pltpu.VMEM((t, LANES), F32)],
        name=name, compiler_params=_cp())(h, h, h, cb, shi, slo, strip, o, lse, dy)


def _gdn_prep_fwd(h, cw, *, name):
    t = h.shape[0]
    qscale = GDN_DK ** -0.5

    def body(hq_ref, hk_ref, hv_ref, wq_ref, wk_ref, wv_ref, q_ref, k_ref, v_ref):
        row = lax.broadcasted_iota(jnp.int32, (t, GDN_DK), 0)
        sq = _silu(_dwconv(hq_ref[...], wq_ref, row))
        sk = _silu(_dwconv(hk_ref[...], wk_ref, row))
        q_ref[0] = sq * lax.rsqrt(jnp.sum(sq * sq, -1, keepdims=True) + 1e-6) * qscale
        k_ref[0] = sk * lax.rsqrt(jnp.sum(sk * sk, -1, keepdims=True) + 1e-6)
        v_ref[0] = _silu(_dwconv(hv_ref[...], wv_ref, row))

    hs = lambda off: pl.BlockSpec((t, GDN_DK), lambda i: (0, i + off))
    ws = lambda off: pl.BlockSpec((GDN_CONV, GDN_DK), lambda i: (0, i + off))
    out = pl.BlockSpec((1, t, GDN_DK), lambda i: (i, 0, 0))
    return pl.pallas_call(
        body, grid=(GDN_HEADS,), in_specs=[hs(0), hs(8), hs(16), ws(0), ws(8), ws(16)], out_specs=[out, out, out],
        out_shape=[jax.ShapeDtypeStruct((GDN_HEADS, t, GDN_DK), F32)] * 3,
        name=name, compiler_params=_cp())(h, h, h, cw, cw, cw)


def _gdn_prep_bwd(h, cw, dq, dk, dv, *, name):
    t = h.shape[0]
    qscale = GDN_DK ** -0.5

    def body(hq_ref, hk_ref, hv_ref, wq_ref, wk_ref, wv_ref, dq_ref, dk_ref, dv_ref,
             dhq_ref, dhk_ref, dhv_ref, dwq_ref, dwk_ref, dwv_ref):
        row = lax.broadcasted_iota(jnp.int32, (t, GDN_DK), 0)

        def one(h_ref, w_ref, d_ref, dh_ref, dw_ref, norm, sc):
            u = h_ref[...]
            c = _dwconv(u, w_ref, row)
            d = d_ref[0]
            if norm:
                s = _silu(c)
                r = lax.rsqrt(jnp.sum(s * s, -1, keepdims=True) + 1e-6)
                n = s * r
                d = d * sc
                d = r * (d - n * jnp.sum(d * n, -1, keepdims=True))
            dc = d * _dsilu(c)
            dh_ref[...] = _dwconv_bwd(u, w_ref, dc, row, dw_ref).astype(BF16)

        one(hq_ref, wq_ref, dq_ref, dhq_ref, dwq_ref, True, qscale)
        one(hk_ref, wk_ref, dk_ref, dhk_ref, dwk_ref, True, 1.0)
        one(hv_ref, wv_ref, dv_ref, dhv_ref, dwv_ref, False, 1.0)

    hs = lambda off: pl.BlockSpec((t, GDN_DK), lambda i: (0, i + off))
    ws = lambda off: pl.BlockSpec((GDN_CONV, GDN_DK), lambda i: (0, i + off))
    hd = pl.BlockSpec((1, t, GDN_DK), lambda i: (i, 0, 0))
    return pl.pallas_call(
        body, grid=(GDN_HEADS,), in_specs=[hs(0), hs(8), hs(16), ws(0), ws(8), ws(16), hd, hd, hd],
        out_specs=[hs(0), hs(0), hs(0), ws(0), ws(0), ws(0)],
        out_shape=[jax.ShapeDtypeStruct((t, GDN_W), BF16)] * 3 + [jax.ShapeDtypeStruct((GDN_CONV, GDN_W), F32)] * 3,
        name=name, compiler_params=_cp())(h, h, h, cw, cw, cw, dq, dk, dv)


def _make_mm2(wide):
    def raw(a, b, dims):
        if wide:
            return lax.dot_general(a, b, (dims, ((), ())), precision=lax.Precision.HIGHEST, preferred_element_type=F32)
        return lax.dot_general(a.astype(BF16), b.astype(BF16), (dims, ((), ())), preferred_element_type=F32)

    @jax.custom_vjp
    def nn(a, b):
        return raw(a, b, ((1,), (0,)))

    @jax.custom_vjp
    def nt(a, b):
        return raw(a, b, ((1,), (1,)))

    @jax.custom_vjp
    def tn(a, b):
        return raw(a, b, ((0,), (0,)))

    nn.defvjp(lambda a, b: (nn(a, b), (a, b)), lambda r, g: (nt(g, r[1]), tn(r[0], g)))
    nt.defvjp(lambda a, b: (nt(a, b), (a, b)), lambda r, g: (nn(g, r[1]), tn(g, r[0])))
    tn.defvjp(lambda a, b: (tn(a, b), (a, b)), lambda r, g: (nt(r[1], g), nn(r[0], g)))
    return nn, nt, tn


_NN, _NT, _TN = _make_mm2(False)
_NNW, _NTW, _TNW = _make_mm2(True)


def _square_masks(c):
    ri = lax.broadcasted_iota(jnp.int32, (c, c), 0)
    ci = lax.broadcasted_iota(jnp.int32, (c, c), 1)
    return ri >= ci, ri > ci, ri == ci


def _cumsum_rows(m):
    tri, _, _ = _square_masks(m.shape[0])
    return _NNW(tri.astype(F32), m)


def _transpose_sq(m):
    _, _, eye = _square_masks(m.shape[0])
    return _NTW(eye.astype(F32), m)


@jax.custom_vjp
def _inv_unit_lower(l):
    c = l.shape[0]
    _, _, eye = _square_masks(c)
    p = -l
    t = eye.astype(F32) + p
    for _ in range(int(math.log2(c)) - 1):
        p = _NNW(p, p)
        t = t + _NNW(t, p)
    return t


def _inv_fwd(l):
    t = _inv_unit_lower(l)
    return t, t


def _inv_bwd(t, dt):
    return (-_NTW(_TNW(t, dt), t),)


_inv_unit_lower.defvjp(_inv_fwd, _inv_bwd)


def _softplus(x):
    return jnp.maximum(x, 0.0) + jnp.log1p(jnp.exp(-jnp.abs(x)))


def _gdn_chunk(q, k, v, braw, araw, alog, dtb, state):
    c = q.shape[0]
    dv = v.shape[1]
    tri, strict, _ = _square_masks(c)
    beta = _sig(braw)
    g = -jnp.exp(alog) * _softplus(araw + dtb)
    gcm = _cumsum_rows(g * jnp.ones((c, c), F32))
    gct = _transpose_sq(gcm)
    decay = jnp.where(tri, jnp.exp(jnp.where(tri, gcm - gct, 0.0)), 0.0)
    gc = jnp.sum(gcm, 1, keepdims=True) * (1.0 / c)
    glast = jnp.sum(g, 0, keepdims=True)
    egc = jnp.exp(gc)
    kb = k * beta
    tm = _inv_unit_lower(jnp.where(strict, _NT(kb, k) * decay, 0.0))
    sol = _NNW(tm, jnp.concatenate([v * beta, kb * egc], 1))
    u, w = sol[:, :dv], sol[:, dv:]
    attn = jnp.where(tri, _NT(q, k) * decay, 0.0)
    k_dec = k * jnp.exp(glast - gc)
    q_dec = q * egc
    v_new = u - _NN(w, state)
    o = _NN(q_dec, state) + _NN(attn, v_new)
    new_state = state * jnp.exp(glast) + _TN(k_dec, v_new)
    return o, new_state


def _gdn_specs(t, rev):
    nch = t // GDN_CHUNK
    cm = (lambda n: nch - 1 - n) if rev else (lambda n: n)
    tok = pl.BlockSpec((GDN_HEADS, GDN_CHUNK, GDN_DK), lambda n: (0, cm(n), 0))
    par = pl.BlockSpec((GDN_HEADS, 1, LANES), lambda n: (0, 0, 0))
    st = pl.BlockSpec((GDN_HEADS, 1, GDN_DK, GDN_DV), lambda n: (0, cm(n), 0, 0))
    return tok, par, st


def _gdn_core_fwd(q, k, v, bb, ab, alog, dtb, *, name):
    t = q.shape[1]
    nch = t // GDN_CHUNK

    def body(q_ref, k_ref, v_ref, bb_ref, ab_ref, al_ref, dt_ref, o_ref, st_ref, state):
        @pl.when(pl.program_id(0) == 0)
        def _():
            state[...] = jnp.zeros_like(state)

        s0 = state[...]
        st_ref[:, 0] = s0
        o, s1 = jax.vmap(_gdn_chunk)(q_ref[...], k_ref[...], v_ref[...], bb_ref[:, :, 0:1], ab_ref[:, :, 0:1],
                                     al_ref[:, :, 0:1], dt_ref[:, :, 0:1], s0)
        o_ref[...] = o
        state[...] = s1

    tok, par, st = _gdn_specs(t, False)
    return pl.pallas_call(
        body, grid=(nch,), in_specs=[tok, tok, tok, tok, tok, par, par], out_specs=[tok, st],
        out_shape=[jax.ShapeDtypeStruct((GDN_HEADS, t, GDN_DV), F32),
                   jax.ShapeDtypeStruct((GDN_HEADS, nch, GDN_DK, GDN_DV), F32)],
        scratch_shapes=[pltpu.VMEM((GDN_HEADS, GDN_DK, GDN_DV), F32)],
        name=name, compiler_params=_cp())(q, k, v, bb, ab, alog, dtb)


def _gdn_core_bwd(q, k, v, bb, ab, alog, dtb, states, do, *, name):
    t = q.shape[1]
    nch = t // GDN_CHUNK

    def body(q_ref, k_ref, v_ref, bb_ref, ab_ref, al_ref, dt_ref, st_ref, do_ref,
             dq_ref, dk_ref, dv_ref, dbb_ref, dab_ref, dal_ref, ddt_ref, dstate):
        @pl.when(pl.program_id(0) == 0)
        def _():
            dstate[...] = jnp.zeros_like(dstate)
            dal_ref[...] = jnp.zeros_like(dal_ref)
            ddt_ref[...] = jnp.zeros_like(ddt_ref)

        args = (q_ref[...], k_ref[...], v_ref[...], bb_ref[:, :, 0:1], ab_ref[:, :, 0:1],
                al_ref[:, :, 0:1], dt_ref[:, :, 0:1], st_ref[:, 0])
        _, pull = jax.vjp(jax.vmap(_gdn_chunk), *args)
        dq, dk, dv, dbr, dar, dal, ddt, ds = pull((do_ref[...], dstate[...]))
        dq_ref[...] = dq
        dk_ref[...] = dk
        dv_ref[...] = dv
        dbb_ref[...] = dbr + jnp.zeros((GDN_HEADS, GDN_CHUNK, LANES), F32)
        dab_ref[...] = dar + jnp.zeros((GDN_HEADS, GDN_CHUNK, LANES), F32)
        dal_ref[...] += dal + jnp.zeros((GDN_HEADS, 1, LANES), F32)
        ddt_ref[...] += ddt + jnp.zeros((GDN_HEADS, 1, LANES), F32)
        dstate[...] = ds

    tok, par, st = _gdn_specs(t, True)
    tokshape = jax.ShapeDtypeStruct((GDN_HEADS, t, GDN_DK), F32)
    parshape = jax.ShapeDtypeStruct((GDN_HEADS, 1, LANES), F32)
    return pl.pallas_call(
        body, grid=(nch,), in_specs=[tok, tok, tok, tok, tok, par, par, st, tok],
        out_specs=[tok, tok, tok, tok, tok, par, par],
        out_shape=[tokshape] * 5 + [parshape] * 2,
        scratch_shapes=[pltpu.VMEM((GDN_HEADS, GDN_DK, GDN_DV), F32)],
        name=name, compiler_params=_cp())(q, k, v, bb, ab, alog, dtb, states, do)


GDN_ROWS = 512


def _gdn_post_fwd(o, h, nw, *, name):
    t = o.shape[1]

    def body(o_ref, g_ref, nw_ref, y_ref):
        oo = o_ref[0]
        r = lax.rsqrt(jnp.mean(oo * oo, -1, keepdims=True) + EPS)
        y_ref[...] = (oo * r * nw_ref[...] * _silu(g_ref[...])).astype(BF16)

    return pl.pallas_call(
        body, grid=(GDN_HEADS, t // GDN_ROWS),
        in_specs=[pl.BlockSpec((1, GDN_ROWS, GDN_DV), lambda hh, i: (hh, i, 0)),
                  pl.BlockSpec((GDN_ROWS, GDN_DV), lambda hh, i: (i, 3 * GDN_HEADS + hh)),
                  pl.BlockSpec((1, GDN_DV), lambda hh, i: (0, 0))],
        out_specs=pl.BlockSpec((GDN_ROWS, GDN_DV), lambda hh, i: (i, hh)),
        out_shape=jax.ShapeDtypeStruct((t, GDN_W), BF16), name=name, compiler_params=_cp())(o, h, nw)


def _gdn_post_bwd(o, h, nw, dy, *, name):
    t = o.shape[1]

    def body(o_ref, g_ref, nw_ref, dy_ref, do_ref, dg_ref, dnw_ref):
        oo, gg, nw_, dy_ = o_ref[0], g_ref[...], nw_ref[...], dy_ref[...]
        r = lax.rsqrt(jnp.mean(oo * oo, -1, keepdims=True) + EPS)
        n = oo * r
        sg = _silu(gg)
        dg_ref[...] = (dy_ * n * nw_ * _dsilu(gg)).astype(BF16)
        dn = dy_ * sg * nw_
        do_ref[0] = r * (dn - n * jnp.mean(dn * n, -1, keepdims=True))

        @pl.when((pl.program_id(0) == 0) & (pl.program_id(1) == 0))
        def _():
            dnw_ref[...] = jnp.zeros_like(dnw_ref)

        dnw_ref[...] += jnp.sum(dy_ * sg * n, 0, keepdims=True)

    return pl.pallas_call(
        body, grid=(GDN_HEADS, t // GDN_ROWS),
        in_specs=[pl.BlockSpec((1, GDN_ROWS, GDN_DV), lambda hh, i: (hh, i, 0)),
                  pl.BlockSpec((GDN_ROWS, GDN_DV), lambda hh, i: (i, 3 * GDN_HEADS + hh)),
                  pl.BlockSpec((1, GDN_DV), lambda hh, i: (0, 0)),
                  pl.BlockSpec((GDN_ROWS, GDN_DV), lambda hh, i: (i, hh))],
        out_specs=[pl.BlockSpec((1, GDN_ROWS, GDN_DV), lambda hh, i: (hh, i, 0)),
                   pl.BlockSpec((GDN_ROWS, GDN_DV), lambda hh, i: (i, hh)),
                   pl.BlockSpec((1, GDN_DV), lambda hh, i: (0, 0))],
        out_shape=[jax.ShapeDtypeStruct((GDN_HEADS, t, GDN_DV), F32), jax.ShapeDtypeStruct((t, GDN_W), BF16),
                   jax.ShapeDtypeStruct((1, GDN_DV), F32)],
        name=name, compiler_params=_cp())(o, h, nw, dy)


def _tables(positions):
    pos = positions.astype(F32)[:, None]
    half = RET_DK // 2
    inv = jnp.power(RET_THETA, -jnp.arange(half, dtype=F32) * 2.0 / RET_DK)
    ang = pos * inv
    cos, sin = jnp.cos(ang), jnp.sin(ang)
    c2a = jnp.concatenate([cos, cos], 1)
    s2a = jnp.concatenate([-sin, sin], 1)
    hb = ROPE_DIMS // 2
    invb = jnp.power(ROPE_THETA, -jnp.arange(hb, dtype=F32) * 2.0 / ROPE_DIMS)
    angb = pos * invb
    cosb, sinb = jnp.cos(angb), jnp.sin(angb)
    t = pos.shape[0]
    ones = jnp.ones((t, DIL_HD - ROPE_DIMS), F32)
    zeros = jnp.zeros((t, DIL_HD - ROPE_DIMS), F32)
    z8 = jnp.zeros((t, hb), F32)
    cb = jnp.concatenate([cosb, cosb, ones] * 2, 1)
    shi = jnp.concatenate([z8, sinb, zeros] * 2, 1)
    slo = jnp.concatenate([-sinb, z8, zeros] * 2, 1)
    lg = jnp.log1p(-jnp.power(2.0, -5.0 - jnp.arange(RET_HEADS, dtype=F32)))
    lgt = jnp.broadcast_to(lg[:, None, None], (RET_HEADS, 1, LANES))
    delta = jnp.arange(ATT_BLK, dtype=jnp.int32)[:, None] + (SEQ - ATT_BLK) - jnp.arange(SEQ, dtype=jnp.int32)[None, :]
    cnt = jnp.zeros(delta.shape, F32)
    for (w, d) in DIL_PAIRS:
        cnt = cnt + ((delta >= 0) & (delta <= w) & (delta % d == 0)).astype(F32)
    strip = jnp.where(cnt > 0, jnp.log(jnp.maximum(cnt, 1.0)), NEG)
    return c2a, s2a, cb, shi, slo, lgt, strip


def _local_step(x, positions, target, get_w, mid, put_g, small):
    c2a, s2a, cb, shi, slo, lgt, strip = _tables(positions)
    t = x.shape[0]
    saved = []
    xf = x
    xb = x.astype(BF16)
    for layer in range(DEPTH):
        j = layer // 2
        L = f"L{layer}_"
        W, dep = get_w(layer, "mixer", xb)
        rec = {"x": xf, "xb": xb}
        if layer % 2 == 0:
            h = _mm(xb, W["in_t"], tb=True, name=L + "ev_in", dep=dep)
            ro, ya = _ret_fwd(h, c2a, s2a, lgt, name=L + "ret_fwd")
            do_, yb, lse = _dil_fwd(h, cb, shi, slo, strip, name=L + "dil_fwd")
            y = jnp.concatenate([ya, yb], 1)
            mix = _mm(y, W["out"], name=L + "ev_out", dep=mid(layer, "mixer", y))
            rec.update(h=h, ro=ro, dil_o=do_, lse=lse, y=y)
        else:
            h = _mm(xb, W["in_t"], tb=True, name=L + "od_in", dep=dep)
            cw = W["conv"]
            q, k, v = _gdn_prep_fwd(h, cw, name=L + "gdn_prep")
            hs = h[:, 4 * GDN_W:4 * GDN_W + 2 * GDN_HEADS]
            bb = jnp.broadcast_to(hs[:, :GDN_HEADS].T[:, :, None], (GDN_HEADS, t, LANES))
            ab = jnp.broadcast_to(hs[:, GDN_HEADS:].T[:, :, None], (GDN_HEADS, t, LANES))
            alog = jnp.broadcast_to(small["od_a_log"][j][:, None, None], (GDN_HEADS, 1, LANES))
            dtb = jnp.broadcast_to(small["od_dt_bias"][j][:, None, None], (GDN_HEADS, 1, LANES))
            o, states = _gdn_core_fwd(q, k, v, bb, ab, alog, dtb, name=L + "gdn_fwd")
            nw = small["od_norm_w"][j][None, :]
            y = _gdn_post_fwd(o, h, nw, name=L + "gdn_post")
            mix = _mm(y, W["out"], name=L + "od_out", dep=mid(layer, "mixer", y))
            rec.update(h=h, q=q, k=k, v=v, bb=bb, ab=ab, alog=alog, dtb=dtb, states=states, o=o, y=y, nw=nw, cw=cw)
        z1, x1, x1b = _ln_fwd(xf, mix, small["ln1_g"][layer][None], small["ln1_b"][layer][None], name=L + "ln1")
        rec["Wm"] = W
        W, dep = get_w(layer, "ffn", x1b)
        rec["Wf"] = W
        u = _mm(x1b, W["up_t"], tb=True, name=L + "ffn_up", dep=dep)
        fcw = W["fconv"]
        fcb = small["ffn_conv_b"][layer][None]
        a = _ffn_mid_fwd(u, fcw, fcb, name=L + "ffn_mid")
        f = _mm(a, W["down"], name=L + "ffn_down", dep=mid(layer, "ffn", a))
        z2, x2, x2b = _ln_fwd(x1, f, small["ln2_g"][layer][None], small["ln2_b"][layer][None], name=L + "ln2")
        rec.update(z1=z1, x1b=x1b, u=u, a=a, z2=z2, fcw=fcw, fcb=fcb)
        saved.append(rec)
        xf, xb = x2, x2b

    dy, lossv = _loss_head(xf, target, name="loss_head")
    loss = lossv[0, 0]

    gS = {n: [None] * small[n].shape[0] for n in small}
    dres, dmm = dy, None
    for layer in reversed(range(DEPTH)):
        j = layer // 2
        L = f"L{layer}_"
        rec = saved[layer]
        Wm, Wf = rec["Wm"], rec["Wf"]
        g = {}
        if dmm is None:
            dz2, dz2b, dg2, db2 = _ln_bwd(rec["z2"], small["ln2_g"][layer][None], dres, None, name=L + "ln2_bwd")
        else:
            dz2, dz2b, dg2, db2 = _ln_bwd(rec["z2"], small["ln2_g"][layer][None], dmm, dres, name=L + "ln2_bwd")
        gS["ln2_g"][layer], gS["ln2_b"][layer] = dg2[0], db2[0]
        g["down"] = _mm(rec["a"], dz2b, ta=True, name=L + "ffn_down_dw", out_dtype=BF16)
        da = _mm(dz2b, Wf["down"], tb=True, name=L + "ffn_down_dx")
        du, dcw, dcb = _ffn_mid_bwd(rec["u"], rec["fcw"], rec["fcb"], da, name=L + "ffn_mid_bwd")
        g["fconv"] = dcw.astype(BF16)
        gS["ffn_conv_b"][layer] = dcb[0]
        g["up_t"] = _mm(du, rec["x1b"], ta=True, name=L + "ffn_up_dw", out_dtype=BF16)
        dep = put_g(layer, "ffn", g)
        dx1 = _mm(du, Wf["up_t"], name=L + "ffn_up_dx", dep=dep)
        dz1, dz1b, dg1, db1 = _ln_bwd(rec["z1"], small["ln1_g"][layer][None], dx1, dz2, name=L + "ln1_bwd")
        gS["ln1_g"][layer], gS["ln1_b"][layer] = dg1[0], db1[0]
        g = {}
        if layer % 2 == 0:
            g["out"] = _mm(rec["y"], dz1b, ta=True, name=L + "ev_out_dw", out_dtype=BF16)
            dyy = _mm(dz1b, Wm["out"], tb=True, name=L + "ev_out_dx")
            dqa, dka, dva, dga = _ret_bwd(rec["h"], c2a, s2a, lgt, rec["ro"], dyy, name=L + "ret_bwd")
            dqb, dkb, dvb = _dil_bwd(rec["h"], cb, shi, slo, strip, rec["dil_o"], rec["lse"], dyy, name=L + "dil_bwd")
            dh = jnp.concatenate([dqa, dka, dva, dga, dqb, dkb, dvb], 1)
            g["in_t"] = _mm(dh, rec["xb"], ta=True, name=L + "ev_in_dw", out_dtype=BF16)
            dep = put_g(layer, "mixer", g)
            dxin = _mm(dh, Wm["in_t"], name=L + "ev_in_dx", dep=dep)
        else:
            g["out"] = _mm(rec["y"], dz1b, ta=True, name=L + "od_out_dw", out_dtype=BF16)
            dyy = _mm(dz1b, Wm["out"], tb=True, name=L + "od_out_dx")
            do, dgate, dnw = _gdn_post_bwd(rec["o"], rec["h"], rec["nw"], dyy, name=L + "gdn_post_bwd")
            gS["od_norm_w"][j] = dnw[0]
            dq, dk, dv, dbb, dab, dal, ddt = _gdn_core_bwd(
                rec["q"], rec["k"], rec["v"], rec["bb"], rec["ab"], rec["alog"], rec["dtb"], rec["states"], do,
                name=L + "gdn_bwd")
            gS["od_a_log"][j] = dal[:, 0, 0]
            gS["od_dt_bias"][j] = ddt[:, 0, 0]
            dhq, dhk, dhv, dwq, dwk, dwv = _gdn_prep_bwd(rec["h"], rec["cw"], dq, dk, dv, name=L + "gdn_prep_bwd")
            g["conv"] = jnp.concatenate([dwq, dwk, dwv], 1).astype(BF16)
            dsm = jnp.concatenate([dbb[:, :, 0].T, dab[:, :, 0].T,
                                   jnp.zeros((t, LANES - 2 * GDN_HEADS), F32)], 1).astype(BF16)
            dh = jnp.concatenate([dhq, dhk, dhv, dgate, dsm], 1)
            g["in_t"] = _mm(dh, rec["xb"], ta=True, name=L + "od_in_dw", out_dtype=BF16)
            dep = put_g(layer, "mixer", g)
            dxin = _mm(dh, Wm["in_t"], name=L + "od_in_dx", dep=dep)
        dres, dmm = dz1, dxin
    grad_x = _axpy(dmm, dres, name="grad_x")
    gS = {n: jnp.stack(v) for n, v in gS.items()}
    return loss, grad_x, gS


HBM = pl.BlockSpec(memory_space=pltpu.HBM)


def _me():
    return lax.axis_index("x"), lax.axis_index("y"), lax.axis_index("c")


def _my_index():
    x, y, c = _me()
    return 4 * x + 2 * y + c


SEM = pl.BlockSpec(memory_space=pltpu.SEMAPHORE)
ANY = pl.BlockSpec(memory_space=pl.ANY)
PLANS = {"scatter": (1, 2, 3, 4, 5, 6, 7), "spread": (1, 2, 4, 6), "relay": (2, 4, 6)}
SIBLING = 1


def _peer(kk):
    x, y, c = _me()
    return x ^ (kk >> 2), y ^ ((kk >> 1) & 1), c ^ (kk & 1)


def _peer_index(kk):
    px, py, pc = _peer(kk)
    return 4 * px + 2 * py + pc


def _job_copies(mode, srcs, lands, send_sems, recv_sems, incoming):
    myid = _my_index()
    plan = PLANS[mode]
    out = []
    for a in range(len(lands)):
        for idx, kk in enumerate(plan):
            if mode == "relay":
                to, src = _peer(SIBLING), lands[a].at[_peer_index(kk)]
                slot_there, slot_here = _peer_index(kk), _peer_index(kk ^ SIBLING)
            else:
                to, src = _peer(kk), (srcs[a] if mode == "spread" else srcs[a].at[_peer_index(kk)])
                slot_there, slot_here = myid, _peer_index(kk)
            sem = a * len(plan) + idx
            out.append(pltpu.make_async_remote_copy(
                src_ref=src, dst_ref=lands[a].at[slot_here if incoming else slot_there],
                send_sem=send_sems.at[sem], recv_sem=recv_sems.at[sem], device_id=to, device_id_type=MESH))
    return out


def _split_jobs(jobs, arrays):
    out, o = [], 0
    for (_, srcs, lands) in jobs:
        out.append((arrays[o:o + len(srcs)], arrays[o + len(srcs):o + len(srcs) + len(lands)]))
        o += len(srcs) + len(lands)
    return out


def _exchange_start(jobs, after, *, name):
    jobs = [(mode, list(srcs), [lax.empty((N_DEV, *s.shape) if mode == "spread" else s.shape, s.dtype) for s in srcs]
             if lands is None else list(lands)) for (mode, srcs, lands) in jobs]
    flat = [a for (_, srcs, lands) in jobs for a in (*srcs, *lands)]
    n, nj = len(flat), len(jobs)
    nsem = [len(PLANS[mode]) * len(lands) for (mode, _, lands) in jobs]

    def body(*refs):
        o = n + (0 if after is None else 1)
        sems, token = refs[o:o + 2 * nj], refs[o + 2 * nj + n]
        for ji, ((mode, _, _), (src, land)) in enumerate(zip(jobs, _split_jobs(jobs, refs[:n]))):
            for cp in _job_copies(mode, src, land, sems[2 * ji], sems[2 * ji + 1], False):
                cp.start()
        token[...] = jnp.zeros_like(token)

    outs = pl.pallas_call(
        body, name=name,
        out_shape=(*[pltpu.SemaphoreType.DMA((ns,)) for ns in nsem for _ in range(2)],
                   *[pltpu.HBM(a.shape, a.dtype) for a in flat], jax.ShapeDtypeStruct((8, LANES), F32)),
        in_specs=[HBM] * n + ([] if after is None else [ANY]),
        out_specs=(*[SEM] * (2 * nj), *[HBM] * n, pl.BlockSpec(memory_space=pltpu.VMEM)),
        input_output_aliases={i: 2 * nj + i for i in range(n)},
        compiler_params=pltpu.CompilerParams(has_side_effects=pltpu.SideEffectType.DATAFLOW_SIDE_EFFECTING),
    )(*[pltpu.with_memory_space_constraint(a, pltpu.HBM) for a in flat], *([] if after is None else [after]))
    thru = _split_jobs(jobs, list(outs[2 * nj:2 * nj + n]))
    started = [(mode, outs[2 * ji], outs[2 * ji + 1], src, land) for ji, ((mode, _, _), (src, land)) in enumerate(zip(jobs, thru))]
    return started, outs[2 * nj + n]


def _exchange_wait(started, after, *, name):
    jobs = [(mode, srcs, lands) for (mode, _, _, srcs, lands) in started]
    flat = [a for (_, srcs, lands) in jobs for a in (*srcs, *lands)]
    n, nj = len(flat), len(jobs)

    def body(*refs):
        sems = refs[n:n + 2 * nj]
        for ji, ((mode, _, _), (src, land)) in enumerate(zip(jobs, _split_jobs(jobs, refs[:n]))):
            for cp in _job_copies(mode, src, land, sems[2 * ji], sems[2 * ji + 1], True):
                cp.wait_send()
                cp.wait_recv()

    outs = pl.pallas_call(
        body, name=name, out_shape=tuple(pltpu.HBM(a.shape, a.dtype) for a in flat),
        in_specs=[HBM] * n + [SEM] * (2 * nj) + [ANY], out_specs=tuple([HBM] * n),
        input_output_aliases={i: i for i in range(n)},
        compiler_params=pltpu.CompilerParams(has_side_effects=pltpu.SideEffectType.DATAFLOW_SIDE_EFFECTING),
    )(*flat, *[s for (_, ss, rs, _, _) in started for s in (ss, rs)], after)
    return _split_jobs(jobs, list(outs))


TRANSPOSE_COLS = 256


def _sum8(land, *, transpose=False, name):
    _, rr, cc = land.shape

    def body(l_ref, o_ref):
        acc = l_ref[0].astype(F32)
        for d in range(1, N_DEV):
            acc = acc + l_ref[d].astype(F32)
        o_ref[...] = acc.T if transpose else acc

    if transpose:
        tc = TRANSPOSE_COLS
        grid, in_spec = (cc // tc,), pl.BlockSpec((N_DEV, rr, tc), lambda i: (0, 0, i))
        out_spec, out_shape = pl.BlockSpec((tc, rr), lambda i: (i, 0)), jax.ShapeDtypeStruct((cc, rr), F32)
    else:
        tr = _row_tile(rr)
        grid, in_spec = (rr // tr,), pl.BlockSpec((N_DEV, tr, cc), lambda i: (0, i, 0))
        out_spec, out_shape = pl.BlockSpec((tr, cc), lambda i: (i, 0)), jax.ShapeDtypeStruct((rr, cc), F32)
    return pl.pallas_call(body, grid=grid, in_specs=[in_spec], out_specs=out_spec, out_shape=out_shape,
                          name=name, compiler_params=_cp())(land)


def _transpose_cast(w, j, *, name):
    _, rr, cc = w.shape
    tr = TRANSPOSE_COLS

    def body(w_ref, o_ref):
        o_ref[...] = w_ref[0].T.astype(BF16)

    return pl.pallas_call(
        body, grid=(rr // tr,), in_specs=[pl.BlockSpec((1, tr, cc), lambda i: (j, i, 0))],
        out_specs=pl.BlockSpec((cc, tr), lambda i: (0, i)), out_shape=jax.ShapeDtypeStruct((cc, rr), BF16),
        name=name, compiler_params=_cp())(w)


def _row_tile(rr):
    for cand in (512, 384, 256, 192, 176, 128, 64, 32, 16, 8):
        if rr % cand == 0:
            return cand
    return rr


def _small_exchange(vec, *, name):
    rr = vec.shape[0]

    def body(v_ref, o_ref, send_sems, recv_sems):
        x, y, c = _me()
        myid = 4 * x + 2 * y + c
        o_ref[myid] = v_ref[...]
        cps = []
        for kk in range(1, N_DEV):
            px, py, pc = x ^ (kk >> 2), y ^ ((kk >> 1) & 1), c ^ (kk & 1)
            cps.append(pltpu.make_async_remote_copy(
                src_ref=v_ref, dst_ref=o_ref.at[myid], send_sem=send_sems.at[kk], recv_sem=recv_sems.at[kk],
                device_id=(px, py, pc), device_id_type=MESH))
        for cp in cps:
            cp.start()
        for kk in range(1, N_DEV):
            px, py, pc = x ^ (kk >> 2), y ^ ((kk >> 1) & 1), c ^ (kk & 1)
            pltpu.make_async_remote_copy(
                src_ref=v_ref, dst_ref=o_ref.at[4 * px + 2 * py + pc], send_sem=send_sems.at[kk],
                recv_sem=recv_sems.at[kk], device_id=(px, py, pc), device_id_type=MESH).wait_recv()
        for cp in cps:
            cp.wait_send()

    return pl.pallas_call(
        body, in_specs=[pl.BlockSpec(memory_space=pltpu.VMEM)], out_specs=pl.BlockSpec(memory_space=pltpu.VMEM),
        out_shape=jax.ShapeDtypeStruct((N_DEV, rr, LANES), F32),
        scratch_shapes=[pltpu.SemaphoreType.DMA((N_DEV,)), pltpu.SemaphoreType.DMA((N_DEV,))],
        name=name, compiler_params=pltpu.CompilerParams(has_side_effects=True))(vec)


def _adam_math(w, g, m, v):
    m = ADAM_B1 * m + (1.0 - ADAM_B1) * g
    v = ADAM_B2 * v + (1.0 - ADAM_B2) * (g * g)
    m_hat = m / (1.0 - ADAM_B1 ** ADAM_STEP)
    v_hat = v / (1.0 - ADAM_B2 ** ADAM_STEP)
    delta = -ADAM_LR * (m_hat / (jnp.sqrt(v_hat) + ADAM_EPS) + ADAM_WD * w)
    return delta, m, v


def _adamw_sharded(w, m, v, g, *, name):
    ll, rr, cc = w.shape
    tr = _row_tile(rr)

    def body(w_ref, m_ref, v_ref, g_ref, d_ref, nm_ref, nv_ref):
        d, nm, nv = _adam_math(w_ref[...], g_ref[...], m_ref[...], v_ref[...])
        d_ref[...] = d
        nm_ref[...] = nm
        nv_ref[...] = nv

    blk = pl.BlockSpec((1, tr, cc), lambda l, i: (l, i, 0))
    sh = jax.ShapeDtypeStruct((ll, rr, cc), F32)
    return pl.pallas_call(
        body, grid=(ll, rr // tr), in_specs=[blk] * 4, out_specs=[blk] * 3, out_shape=[sh] * 3,
        name=name, compiler_params=_cp())(w, m, v, g)


def _adamw_small(w, m, v, gall, *, name):
    rr = w.shape[0]

    def body(w_ref, m_ref, v_ref, g_ref, go_ref, d_ref, nm_ref, nv_ref):
        g = g_ref[0]
        for kk in range(1, N_DEV):
            g = g + g_ref[kk]
        d, nm, nv = _adam_math(w_ref[...], g, m_ref[...], v_ref[...])
        go_ref[...] = g
        d_ref[...] = d
        nm_ref[...] = nm
        nv_ref[...] = nv

    sh = jax.ShapeDtypeStruct((rr, LANES), F32)
    return pl.pallas_call(body, out_shape=[sh] * 4, name=name, compiler_params=_cp())(w, m, v, gall)


SHARDED = ("ev_w_in", "ev_w_out", "od_w_in", "od_conv_w", "od_w_out", "ffn_w_up", "ffn_conv_w", "ffn_w_down")
SMALL = ("od_a_log", "od_dt_bias", "od_norm_w", "ffn_conv_b", "ln1_g", "ln1_b", "ln2_g", "ln2_b")
ALL_W = ("ev_w_in", "ev_w_out", "od_w_in", "od_conv_w", "od_a_log", "od_dt_bias", "od_norm_w", "od_w_out",
         "ffn_w_up", "ffn_conv_w", "ffn_conv_b", "ffn_w_down", "ln1_g", "ln1_b", "ln2_g", "ln2_b")


def _layer_items(layer):
    j = layer // 2
    if layer % 2 == 0:
        mixer = [("in_t", "ev_w_in", j, "colT"), ("out", "ev_w_out", j, "row")]
    else:
        mixer = [("in_t", "od_w_in", j, "colT"), ("conv", "od_conv_w", j, "colsmall"), ("out", "od_w_out", j, "row")]
    return mixer + [("up_t", "ffn_w_up", layer, "colT"), ("fconv", "ffn_conv_w", layer, "colsmall"),
                    ("down", "ffn_w_down", layer, "row")]


def _to_send(kind, w, j, name):
    if kind == "colT":
        return _transpose_cast(w, j, name=name)
    return w[j].astype(BF16) if kind == "row" else w[j]


def _from_gather(kind, name, g):
    if kind == "colsmall":
        return jnp.transpose(g, (1, 0, 2)).reshape(g.shape[1], -1)
    full = g.reshape(-1, g.shape[-1])
    if name == "od_w_in":
        full = jnp.pad(full, ((0, OD_IN_PAD - OD_IN), (0, 0)))
    return full


def _by_owner(kind, name, gfull):
    if kind == "colsmall":
        kk, c8 = gfull.shape
        return jnp.transpose(gfull.reshape(kk, N_DEV, c8 // N_DEV), (1, 0, 2))
    if name == "od_w_in":
        gfull = gfull[:OD_IN]
    return gfull.reshape(N_DEV, gfull.shape[0] // N_DEV, gfull.shape[1])


def _pack_small(d):
    flat = jnp.concatenate([d[n].reshape(-1) for n in SMALL])
    pad = (-flat.shape[0]) % (8 * LANES)
    return jnp.pad(flat, (0, pad)).reshape(-1, LANES)


def _unpack_small(packed, like):
    flat = packed.reshape(-1)
    out, off = {}, 0
    for n in SMALL:
        sz = int(np.prod(like[n].shape))
        out[n] = flat[off:off + sz].reshape(like[n].shape)
        off += sz
    return out


def kernel(x, positions, ev_w_in, ev_w_out, od_w_in, od_conv_w, od_a_log, od_dt_bias, od_norm_w, od_w_out, ffn_w_up, ffn_conv_w, ffn_conv_b, ffn_w_down, ln1_g, ln1_b, ln2_g, ln2_b, loss_target, m_ev_w_in, m_ev_w_out, m_od_w_in, m_od_conv_w, m_od_a_log, m_od_dt_bias, m_od_norm_w, m_od_w_out, m_ffn_w_up, m_ffn_conv_w, m_ffn_conv_b, m_ffn_w_down, m_ln1_g, m_ln1_b, m_ln2_g, m_ln2_b, v_ev_w_in, v_ev_w_out, v_od_w_in, v_od_conv_w, v_od_a_log, v_od_dt_bias, v_od_norm_w, v_od_w_out, v_ffn_w_up, v_ffn_conv_w, v_ffn_conv_b, v_ffn_w_down, v_ln1_g, v_ln1_b, v_ln2_g, v_ln2_b):
    w = dict(ev_w_in=ev_w_in, ev_w_out=ev_w_out, od_w_in=od_w_in, od_conv_w=od_conv_w, od_a_log=od_a_log,
             od_dt_bias=od_dt_bias, od_norm_w=od_norm_w, od_w_out=od_w_out, ffn_w_up=ffn_w_up, ffn_conv_w=ffn_conv_w,
             ffn_conv_b=ffn_conv_b, ffn_w_down=ffn_w_down, ln1_g=ln1_g, ln1_b=ln1_b, ln2_g=ln2_g, ln2_b=ln2_b)
    mom = dict(ev_w_in=m_ev_w_in, ev_w_out=m_ev_w_out, od_w_in=m_od_w_in, od_conv_w=m_od_conv_w, od_a_log=m_od_a_log,
               od_dt_bias=m_od_dt_bias, od_norm_w=m_od_norm_w, od_w_out=m_od_w_out, ffn_w_up=m_ffn_w_up,
               ffn_conv_w=m_ffn_conv_w, ffn_conv_b=m_ffn_conv_b, ffn_w_down=m_ffn_w_down, ln1_g=m_ln1_g,
               ln1_b=m_ln1_b, ln2_g=m_ln2_g, ln2_b=m_ln2_b)
    var = dict(ev_w_in=v_ev_w_in, ev_w_out=v_ev_w_out, od_w_in=v_od_w_in, od_conv_w=v_od_conv_w, od_a_log=v_od_a_log,
               od_dt_bias=v_od_dt_bias, od_norm_w=v_od_norm_w, od_w_out=v_od_w_out, ffn_w_up=v_ffn_w_up,
               ffn_conv_w=v_ffn_conv_w, ffn_conv_b=v_ffn_conv_b, ffn_w_down=v_ffn_w_down, ln1_g=v_ln1_g,
               ln1_b=v_ln1_b, ln2_g=v_ln2_g, ln2_b=v_ln2_b)

    myid = _my_index()
    small = {n: w[n] for n in SMALL}
    groups = [(layer, part) for layer in range(DEPTH) for part in ("mixer", "ffn")]

    def group_items(gi):
        layer, part = groups[gi]
        its = _layer_items(layer)
        return its[:-3] if part == "mixer" else its[-3:]

    level1, level2 = {}, {}

    def spread_job(gi):
        return ("spread", [_to_send(kind, w[n], j, f"{n}{j}_transpose") for (_, n, j, kind) in group_items(gi)], None)

    def relay(gi, after, name):
        (srcs, lands), = _exchange_wait([level1.pop(gi)], after, name=name + "_wait")
        more = [spread_job(gi + 1)] if gi + 1 < len(groups) else []
        started, token = _exchange_start([("relay", [], lands)] + more, None, name=name + "_start")
        level2[gi] = (started[0], srcs)
        if more:
            level1[gi + 1] = started[1]
        return token

    def get_w(layer, part, after):
        gi = groups.index((layer, part))
        started, srcs = level2.pop(gi)
        (_, lands), = _exchange_wait([started], after, name=f"gather{gi}_wait")
        lands = [lax.dynamic_update_index_in_dim(l, s, myid, 0) for l, s in zip(lands, srcs)]
        return {key: _from_gather(kind, n, l) for (key, n, _, kind), l in zip(group_items(gi), lands)}, None

    def mid(layer, part, after):
        gi = groups.index((layer, part)) + 1
        return relay(gi, after, f"gather{gi}_relay") if gi < len(groups) else None

    landed = {}
    pending = []

    def scatter_finish(after):
        started, gi = pending.pop()
        (srcs, lands), = _exchange_wait([started], after, name=f"scatter{gi}_wait")
        for (key, _, _, _), l, s in zip(group_items(gi), lands, srcs):
            own = lax.dynamic_index_in_dim(s, myid, 0, keepdims=False)
            landed[(groups[gi][0], key)] = lax.dynamic_update_index_in_dim(l, own, myid, 0)

    def put_g(layer, part, g):
        gi = groups.index((layer, part))
        srcs = [_by_owner(kind, n, g[key]) for (key, n, _, kind) in group_items(gi)]
        (started,), token = _exchange_start([("scatter", srcs, None)], None, name=f"scatter{gi}_start")
        if pending:
            scatter_finish(token)
        pending.append((started, gi))
        return token

    (level1[0],), token = _exchange_start([spread_job(0)], None, name="gather0_spread_start")
    relay(0, token, "gather0_relay")
    loss, grad_x, gS = _local_step(x[0], positions[0], loss_target[0], get_w, mid, put_g, small)
    loss = lax.psum(loss, ("x", "y", "c"))

    outs_g, outs_d, outs_m, outs_v = {}, {}, {}, {}
    where = {n: [None] * w[n].shape[0] for n in SHARDED}
    for layer in range(DEPTH):
        for (key, n, j, kind) in _layer_items(layer):
            where[n][j] = (layer, key, kind)

    def update(n):
        parts = []
        for (layer, key, kind) in where[n]:
            parts.append(_sum8(landed[(layer, key)], transpose=kind == "colT", name=f"L{layer}_{key}_sum"))
        outs_g[n] = jnp.stack(parts)
        outs_d[n], outs_m[n], outs_v[n] = _adamw_sharded(w[n], mom[n], var[n], outs_g[n], name=f"adamw_{n}")

    last = {n for (_, n, _, _) in group_items(pending[0][1])}
    for n in SHARDED:
        if n not in last:
            update(n)
    scatter_finish(outs_d[[n for n in SHARDED if n not in last][-1]])
    for n in SHARDED:
        if n in last:
            update(n)

    gall = _small_exchange(_pack_small(gS), name="small_grads_exchange")
    g, d, nm, nv = _adamw_small(_pack_small({n: w[n] for n in SMALL}), _pack_small({n: mom[n] for n in SMALL}),
                                _pack_small({n: var[n] for n in SMALL}), gall, name="adamw_small")
    for dst, packed in ((outs_g, g), (outs_d, d), (outs_m, nm), (outs_v, nv)):
        dst.update(_unpack_small(packed, {n: w[n] for n in SMALL}))

    return (loss, grad_x[None], *[outs_g[n] for n in ALL_W], *[outs_d[n] for n in ALL_W],
            *[outs_m[n] for n in ALL_W], *[outs_v[n] for n in ALL_W])
```

```python
import functools
import math

import numpy as np
import jax
import jax.numpy as jnp
from jax import lax
from jax.experimental import pallas as pl
from jax.experimental.pallas import tpu as pltpu

F32 = jnp.float32
BF16 = jnp.bfloat16
MESH = pl.DeviceIdType.MESH

D_MODEL = 1024
SEQ = 2048
DEPTH = 4
N_DEV = 8
RET_HEADS, RET_DK, RET_DV = 4, 128, 256
RET_THETA = 10000.0
DIL_HEADS, DIL_HD = 8, 64
DIL_PAIRS = ((128, 1), (512, 4), (2048, 16))
ROPE_THETA = 500000.0
ROPE_DIMS = DIL_HD // 4
GDN_HEADS, GDN_DK, GDN_DV, GDN_CHUNK, GDN_CONV = 8, 128, 128, 64, 4
D_FF = 2816
FFN_CONV = 3
ALPHA = (2.0 * DEPTH) ** 0.25
EPS = 1e-5
RET_QK_W = RET_HEADS * RET_DK
RET_V_W = RET_HEADS * RET_DV
DIL_W = DIL_HEADS * DIL_HD
EV_IN = 2 * RET_QK_W + 2 * RET_V_W + 3 * DIL_W
EV_MIX = RET_V_W + DIL_W
GDN_W = GDN_HEADS * GDN_DK
OD_IN = 4 * GDN_W + 2 * GDN_HEADS
OD_IN_PAD = 4 * GDN_W + 128
ADAM_LR, ADAM_B1, ADAM_B2, ADAM_EPS, ADAM_WD, ADAM_STEP = 0.001, 0.9, 0.999, 1e-08, 0.01, 10

LANES = 128
VMEM_LIMIT = 56 * 1024 * 1024
ATT_BLK = 256
NEG = -1e30


def _cp(**kw):
    return pltpu.CompilerParams(vmem_limit_bytes=VMEM_LIMIT, **kw)


def _tile(n, cap):
    if n <= cap:
        return n
    best = None
    for t in range(LANES, cap + 1, LANES):
        if n % t == 0:
            best = t
    assert best is not None, (n, cap)
    return best


def _mm(a, b, *, ta=False, tb=False, name, out_dtype=F32, dep=None, tm=None, tn=None):
    m = a.shape[1] if ta else a.shape[0]
    k = a.shape[0] if ta else a.shape[1]
    n = b.shape[0] if tb else b.shape[1]
    assert (b.shape[1] if tb else b.shape[0]) == k
    assert a.dtype == BF16 and b.dtype == BF16
    if tn is None:
        tn = n if n <= 1024 else _tile(n, 512)
    if tm is None:
        tm = m if (tn < n and k <= 1024 and m <= 2048) else _tile(m, 512)
    dims = (((0 if ta else 1,), (1 if tb else 0,)), ((), ()))

    def body(a_ref, b_ref, *rest):
        o_ref = rest[-1]
        o_ref[...] = lax.dot_general(a_ref[...], b_ref[...], dims,
                                     preferred_element_type=F32).astype(o_ref.dtype)

    a_spec = pl.BlockSpec((k, tm), lambda i, j: (0, i)) if ta else pl.BlockSpec((tm, k), lambda i, j: (i, 0))
    b_spec = pl.BlockSpec((tn, k), lambda i, j: (j, 0)) if tb else pl.BlockSpec((k, tn), lambda i, j: (0, j))
    extra = [] if dep is None else [dep]
    return pl.pallas_call(
        body, grid=(m // tm, n // tn), in_specs=[a_spec, b_spec] + [pl.BlockSpec(memory_space=pl.ANY)] * len(extra),
        out_specs=pl.BlockSpec((tm, tn), lambda i, j: (i, j)),
        out_shape=jax.ShapeDtypeStruct((m, n), out_dtype), name=name, compiler_params=_cp())(a, b, *extra)


LN_ROWS = 256


def _ln_fwd(x, m, g, b, *, name):
    t, d = x.shape

    def body(x_ref, m_ref, g_ref, b_ref, z_ref, y_ref, yb_ref):
        z = ALPHA * x_ref[...] + m_ref[...]
        mu = jnp.mean(z, -1, keepdims=True)
        zc = z - mu
        var = jnp.mean(zc * zc, -1, keepdims=True)
        y = zc * lax.rsqrt(var + EPS) * g_ref[...] + b_ref[...]
        z_ref[...] = z
        y_ref[...] = y
        yb_ref[...] = y.astype(BF16)

    row = pl.BlockSpec((LN_ROWS, d), lambda i: (i, 0))
    vec = pl.BlockSpec((1, d), lambda i: (0, 0))
    return pl.pallas_call(
        body, grid=(t // LN_ROWS,), in_specs=[row, row, vec, vec], out_specs=[row, row, row],
        out_shape=[jax.ShapeDtypeStruct((t, d), F32), jax.ShapeDtypeStruct((t, d), F32),
                   jax.ShapeDtypeStruct((t, d), BF16)],
        name=name, compiler_params=_cp())(x, m, g, b)


def _ln_bwd(z, g, dya, dyb, *, name):
    t, d = z.shape
    two = dyb is not None

    def body(*refs):
        if two:
            z_ref, g_ref, dya_ref, dyb_ref, dz_ref, dzb_ref, dg_ref, db_ref = refs
            dy = dya_ref[...] + ALPHA * dyb_ref[...]
        else:
            z_ref, g_ref, dya_ref, dz_ref, dzb_ref, dg_ref, db_ref = refs
            dy = dya_ref[...]
        zz = z_ref[...]
        mu = jnp.mean(zz, -1, keepdims=True)
        zc = zz - mu
        var = jnp.mean(zc * zc, -1, keepdims=True)
        r = lax.rsqrt(var + EPS)
        xh = zc * r
        dxh = dy * g_ref[...]
        dz = r * (dxh - jnp.mean(dxh, -1, keepdims=True) - xh * jnp.mean(dxh * xh, -1, keepdims=True))
        dz_ref[...] = dz
        dzb_ref[...] = dz.astype(BF16)

        @pl.when(pl.program_id(0) == 0)
        def _():
            dg_ref[...] = jnp.zeros_like(dg_ref)
            db_ref[...] = jnp.zeros_like(db_ref)

        dg_ref[...] += jnp.sum(dy * xh, 0, keepdims=True)
        db_ref[...] += jnp.sum(dy, 0, keepdims=True)

    row = pl.BlockSpec((LN_ROWS, d), lambda i: (i, 0))
    vec = pl.BlockSpec((1, d), lambda i: (0, 0))
    ins = [z, g, dya] + ([dyb] if two else [])
    return pl.pallas_call(
        body, grid=(t // LN_ROWS,), in_specs=[row, vec, row] + ([row] if two else []),
        out_specs=[row, row, vec, vec],
        out_shape=[jax.ShapeDtypeStruct((t, d), F32), jax.ShapeDtypeStruct((t, d), BF16),
                   jax.ShapeDtypeStruct((1, d), F32), jax.ShapeDtypeStruct((1, d), F32)],
        name=name, compiler_params=_cp())(*ins)


def _axpy(a, b, *, name):
    t, d = a.shape

    def body(a_ref, b_ref, o_ref):
        o_ref[...] = a_ref[...] + ALPHA * b_ref[...]

    row = pl.BlockSpec((LN_ROWS, d), lambda i: (i, 0))
    return pl.pallas_call(body, grid=(t // LN_ROWS,), in_specs=[row, row], out_specs=row,
                          out_shape=jax.ShapeDtypeStruct((t, d), F32), name=name, compiler_params=_cp())(a, b)


def _loss_head(y, target, *, name):
    t, d = y.shape

    def body(y_ref, t_ref, dy_ref, l_ref):
        e = y_ref[...] - t_ref[...]
        dy_ref[...] = e * (1.0 / d)

        @pl.when(pl.program_id(0) == 0)
        def _():
            l_ref[...] = jnp.zeros_like(l_ref)

        l_ref[...] += jnp.zeros_like(l_ref) + 0.5 * jnp.sum(jnp.mean(e * e, -1, keepdims=True), 0, keepdims=True)

    row = pl.BlockSpec((LN_ROWS, d), lambda i: (i, 0))
    return pl.pallas_call(
        body, grid=(t // LN_ROWS,), in_specs=[row, row],
        out_specs=[row, pl.BlockSpec((1, LANES), lambda i: (0, 0))],
        out_shape=[jax.ShapeDtypeStruct((t, d), F32), jax.ShapeDtypeStruct((1, LANES), F32)],
        name=name, compiler_params=_cp())(y, target)


def _sig(x):
    return 1.0 / (1.0 + jnp.exp(-x))


def _silu(x):
    return x * _sig(x)


def _dsilu(x):
    s = _sig(x)
    return s * (1.0 + x * (1.0 - s))


def _shift_down(u, k, row):
    if k == 0:
        return u
    return jnp.where(row >= k, pltpu.roll(u, k, 0), 0.0)


def _shift_up(u, k, row):
    if k == 0:
        return u
    t = u.shape[0]
    return jnp.where(row < t - k, pltpu.roll(u, t - k, 0), 0.0)


def _dwconv(u, w_ref, row):
    kk = w_ref.shape[0]
    acc = None
    for j in range(kk):
        term = w_ref[j:j + 1, :] * _shift_down(u, kk - 1 - j, row)
        acc = term if acc is None else acc + term
    return acc


def _dwconv_bwd(u, w_ref, dc, row, dw_ref):
    kk = w_ref.shape[0]
    du = None
    for j in range(kk):
        term = w_ref[j:j + 1, :] * _shift_up(dc, kk - 1 - j, row)
        du = term if du is None else du + term
        dw_ref[j:j + 1, :] = jnp.sum(dc * _shift_down(u, kk - 1 - j, row), 0, keepdims=True)
    return du


CONV_ROWS = 64


def _rows(b):
    return pl.ds(pl.multiple_of(b * CONV_ROWS, CONV_ROWS), CONV_ROWS)


def _shifted_down(ref, b, k, row):
    cur = ref[_rows(b), :]
    if k == 0:
        return cur
    prev = jnp.where(b > 0, ref[_rows(jnp.maximum(b - 1, 0)), :], 0.0)
    return jnp.where(row >= k, pltpu.roll(cur, k, 0), pltpu.roll(prev, k, 0))


def _shifted_up(ref, b, k, row, nblk):
    cur = ref[_rows(b), :]
    if k == 0:
        return cur
    nxt = jnp.where(b < nblk - 1, ref[_rows(jnp.minimum(b + 1, nblk - 1)), :], 0.0)
    return jnp.where(row < CONV_ROWS - k, pltpu.roll(cur, CONV_ROWS - k, 0), pltpu.roll(nxt, CONV_ROWS - k, 0))


def _dwconv_blk(u_ref, w_ref, b, row):
    kk = w_ref.shape[0]
    views = [_shifted_down(u_ref, b, kk - 1 - j, row) for j in range(kk)]
    acc = None
    for j in range(kk):
        term = w_ref[j:j + 1, :] * views[j]
        acc = term if acc is None else acc + term
    return acc, views


def _dwconv_du_blk(dc_ref, w_ref, b, row, nblk):
    kk = w_ref.shape[0]
    du = None
    for j in range(kk):
        term = w_ref[j:j + 1, :] * _shifted_up(dc_ref, b, kk - 1 - j, row, nblk)
        du = term if du is None else du + term
    return du


FFN_TC = 256


def _ffn_mid_fwd(u, cw, cb, *, name):
    t = u.shape[0]
    nb = D_FF // FFN_TC

    def body(ug_ref, uv_ref, wg_ref, wv_ref, bg_ref, bv_ref, a_ref):
        row = lax.broadcasted_iota(jnp.int32, (t, FFN_TC), 0)
        cg = _dwconv(ug_ref[...], wg_ref, row) + bg_ref[...]
        cv = _dwconv(uv_ref[...], wv_ref, row) + bv_ref[...]
        a_ref[...] = (_silu(cg) * cv).astype(BF16)

    col = lambda off: pl.BlockSpec((t, FFN_TC), lambda j: (0, j + off))
    wsp = lambda off: pl.BlockSpec((FFN_CONV, FFN_TC), lambda j: (0, j + off))
    bsp = lambda off: pl.BlockSpec((1, FFN_TC), lambda j: (0, j + off))
    return pl.pallas_call(
        body, grid=(nb,), in_specs=[col(0), col(nb), wsp(0), wsp(nb), bsp(0), bsp(nb)],
        out_specs=pl.BlockSpec((t, FFN_TC), lambda j: (0, j)),
        out_shape=jax.ShapeDtypeStruct((t, D_FF), BF16), name=name, compiler_params=_cp())(u, u, cw, cw, cb, cb)


def _ffn_mid_bwd(u, cw, cb, da, *, name):
    t = u.shape[0]
    nb = D_FF // FFN_TC

    nblk = t // CONV_ROWS

    def body(ug_ref, uv_ref, wg_ref, wv_ref, bg_ref, bv_ref, da_ref,
             dug_ref, duv_ref, dwg_ref, dwv_ref, dbg_ref, dbv_ref, dcg_s, dcv_s):
        row = lax.broadcasted_iota(jnp.int32, (CONV_ROWS, FFN_TC), 0)
        zero = jnp.zeros((1, FFN_TC), F32)

        def first(b, acc):
            cg, ugs = _dwconv_blk(ug_ref, wg_ref, b, row)
            cv, uvs = _dwconv_blk(uv_ref, wv_ref, b, row)
            cg = cg + bg_ref[...]
            cv = cv + bv_ref[...]
            da_ = da_ref[_rows(b), :]
            dcv = da_ * _silu(cg)
            dcg = da_ * cv * _dsilu(cg)
            dcg_s[_rows(b), :] = dcg
            dcv_s[_rows(b), :] = dcv
            red = [jnp.sum(dcg * s, 0, keepdims=True) for s in ugs] + [jnp.sum(dcg, 0, keepdims=True)]
            red += [jnp.sum(dcv * s, 0, keepdims=True) for s in uvs] + [jnp.sum(dcv, 0, keepdims=True)]
            return tuple(a + r for a, r in zip(acc, red))

        acc = lax.fori_loop(0, nblk, first, (zero,) * (2 * FFN_CONV + 2))
        for j in range(FFN_CONV):
            dwg_ref[j:j + 1, :] = acc[j]
            dwv_ref[j:j + 1, :] = acc[FFN_CONV + 1 + j]
        dbg_ref[...] = acc[FFN_CONV]
        dbv_ref[...] = acc[2 * FFN_CONV + 1]

        def second(b, carry):
            dug_ref[_rows(b), :] = _dwconv_du_blk(dcg_s, wg_ref, b, row, nblk).astype(BF16)
            duv_ref[_rows(b), :] = _dwconv_du_blk(dcv_s, wv_ref, b, row, nblk).astype(BF16)
            return carry

        lax.fori_loop(0, nblk, second, 0)

    col = lambda off: pl.BlockSpec((t, FFN_TC), lambda j: (0, j + off))
    wsp = lambda off: pl.BlockSpec((FFN_CONV, FFN_TC), lambda j: (0, j + off))
    bsp = lambda off: pl.BlockSpec((1, FFN_TC), lambda j: (0, j + off))
    outs = pl.pallas_call(
        body, grid=(nb,), in_specs=[col(0), col(nb), wsp(0), wsp(nb), bsp(0), bsp(nb), col(0)],
        out_specs=[col(0), col(0), wsp(0), wsp(0), bsp(0), bsp(0)],
        out_shape=[jax.ShapeDtypeStruct((t, D_FF), BF16), jax.ShapeDtypeStruct((t, D_FF), BF16),
                   jax.ShapeDtypeStruct((FFN_CONV, D_FF), F32), jax.ShapeDtypeStruct((FFN_CONV, D_FF), F32),
                   jax.ShapeDtypeStruct((1, D_FF), F32), jax.ShapeDtypeStruct((1, D_FF), F32)],
        scratch_shapes=[pltpu.VMEM((t, FFN_TC), F32), pltpu.VMEM((t, FFN_TC), F32)],
        name=name, compiler_params=_cp())(u, u, cw, cw, cb, cb, da)
    dug, duv, dwg, dwv, dbg, dbv = outs
    return (jnp.concatenate([dug, duv], 1), jnp.concatenate([dwg, dwv], 1), jnp.concatenate([dbg, dbv], 1))


def _rot_a(x, c2, s2):
    return x * c2 + pltpu.roll(x, RET_DK // 2, 1) * s2


def _rot_a_t(dy, c2, s2):
    return dy * c2 + pltpu.roll(dy * s2, RET_DK // 2, 1)


def _decay_tile(lg, blk_diff):
    r = lax.broadcasted_iota(jnp.int32, (ATT_BLK, ATT_BLK), 0)
    c = lax.broadcasted_iota(jnp.int32, (ATT_BLK, ATT_BLK), 1)
    rel = r - c + blk_diff * ATT_BLK
    return jnp.where(rel >= 0, jnp.exp(jnp.maximum(rel, 0).astype(F32) * lg), 0.0)


def _nt(a, b):
    return lax.dot_general(a, b, (((1,), (1,)), ((), ())), preferred_element_type=F32)


def _nn(a, b):
    return lax.dot_general(a, b, (((1,), (0,)), ((), ())), preferred_element_type=F32)


def _tn(a, b):
    return lax.dot_general(a, b, (((0,), (0,)), ((), ())), preferred_element_type=F32)


def _ret_specs(t):
    q = pl.BlockSpec((t, RET_DK), lambda h: (0, h))
    k = pl.BlockSpec((t, RET_DK), lambda h: (0, RET_HEADS + h))
    v = pl.BlockSpec((t, RET_DV), lambda h: (0, RET_HEADS + h))
    g = pl.BlockSpec((t, RET_DV), lambda h: (0, 2 * RET_HEADS + h))
    tab = pl.BlockSpec((t, RET_DK), lambda h: (0, 0))
    lg = pl.BlockSpec((1, 1, LANES), lambda h: (h, 0, 0))
    return q, k, v, g, tab, lg


def _ret_fwd(h, c2, s2, lgt, *, name):
    t = h.shape[0]
    nblk = t // ATT_BLK
    scale = RET_DK ** -0.5

    def body(q_ref, k_ref, v_ref, g_ref, c_ref, s_ref, lg_ref, o_ref, ya_ref, qs, ks, vs):
        c2_, s2_ = c_ref[...], s_ref[...]
        qs[...] = _rot_a(q_ref[...], c2_, s2_).astype(BF16)
        ks[...] = (_rot_a(k_ref[...], c2_, s2_) * scale).astype(BF16)
        vs[...] = v_ref[...].astype(BF16)
        lg = lg_ref[0, :, 0:1]
        for i in range(nblk):
            qi = qs[pl.ds(i * ATT_BLK, ATT_BLK), :]
            acc = jnp.zeros((ATT_BLK, RET_DV), F32)
            for j in range(i + 1):
                sl = pl.ds(j * ATT_BLK, ATT_BLK)
                s = _nt(qi, ks[sl, :]) * _decay_tile(lg, i - j)
                acc = acc + _nn(s.astype(BF16), vs[sl, :])
            rows = pl.ds(i * ATT_BLK, ATT_BLK)
            o_ref[rows, :] = acc
            r = lax.rsqrt(jnp.mean(acc * acc, -1, keepdims=True) + EPS)
            ya_ref[rows, :] = (acc * r * _silu(g_ref[rows, :])).astype(BF16)

    q, k, v, g, tab, lg = _ret_specs(t)
    out = pl.BlockSpec((t, RET_DV), lambda hh: (0, hh))
    return pl.pallas_call(
        body, grid=(RET_HEADS,), in_specs=[q, k, v, g, tab, tab, lg], out_specs=[out, out],
        out_shape=[jax.ShapeDtypeStruct((t, RET_V_W), F32), jax.ShapeDtypeStruct((t, RET_V_W), BF16)],
        scratch_shapes=[pltpu.VMEM((t, RET_DK), BF16), pltpu.VMEM((t, RET_DK), BF16), pltpu.VMEM((t, RET_DV), BF16)],
        name=name, compiler_params=_cp())(h, h, h, h, c2, s2, lgt)


def _ret_bwd(h, c2, s2, lgt, o, dy, *, name):
    t = h.shape[0]
    nblk = t // ATT_BLK
    scale = RET_DK ** -0.5

    def body(q_ref, k_ref, v_ref, g_ref, c_ref, s_ref, lg_ref, o_ref, dy_ref,
             dq_ref, dk_ref, dv_ref, dg_ref, qs, ks, vs, dos, dka, dva):
        c2_, s2_ = c_ref[...], s_ref[...]
        qs[...] = _rot_a(q_ref[...], c2_, s2_).astype(BF16)
        ks[...] = (_rot_a(k_ref[...], c2_, s2_) * scale).astype(BF16)
        vs[...] = v_ref[...].astype(BF16)
        lg = lg_ref[0, :, 0:1]
        oo = o_ref[...]
        gg = g_ref[...]
        dya = dy_ref[...]
        r = lax.rsqrt(jnp.mean(oo * oo, -1, keepdims=True) + EPS)
        rn = oo * r
        dg_ref[...] = (dya * rn * _dsilu(gg)).astype(BF16)
        drn = dya * _silu(gg)
        dos[...] = (r * (drn - rn * jnp.mean(drn * rn, -1, keepdims=True))).astype(BF16)
        dka[...] = jnp.zeros_like(dka)
        dva[...] = jnp.zeros_like(dva)
        for i in range(nblk):
            rows = pl.ds(i * ATT_BLK, ATT_BLK)
            qi = qs[rows, :]
            doi = dos[rows, :]
            dqa = jnp.zeros((ATT_BLK, RET_DK), F32)
            for j in range(i + 1):
                sl = pl.ds(j * ATT_BLK, ATT_BLK)
                dt_ = _decay_tile(lg, i - j)
                kj = ks[sl, :]
                s = (_nt(qi, kj) * dt_).astype(BF16)
                ds = (_nt(doi, vs[sl, :]) * dt_).astype(BF16)
                dqa = dqa + _nn(ds, kj)
                dka[sl, :] += _tn(ds, qi)
                dva[sl, :] += _tn(s, doi)
            dq_ref[rows, :] = _rot_a_t(dqa, c_ref[rows, :], s_ref[rows, :]).astype(BF16)
        dk_ref[...] = (_rot_a_t(dka[...], c2_, s2_) * scale).astype(BF16)
        dv_ref[...] = dva[...].astype(BF16)

    q, k, v, g, tab, lg = _ret_specs(t)
    blk_v = pl.BlockSpec((t, RET_DV), lambda hh: (0, hh))
    blk_k = pl.BlockSpec((t, RET_DK), lambda hh: (0, hh))
    return pl.pallas_call(
        body, grid=(RET_HEADS,), in_specs=[q, k, v, g, tab, tab, lg, blk_v, blk_v],
        out_specs=[blk_k, blk_k, blk_v, blk_v],
        out_shape=[jax.ShapeDtypeStruct((t, RET_QK_W), BF16), jax.ShapeDtypeStruct((t, RET_QK_W), BF16),
                   jax.ShapeDtypeStruct((t, RET_V_W), BF16), jax.ShapeDtypeStruct((t, RET_V_W), BF16)],
        scratch_shapes=[pltpu.VMEM((t, RET_DK), BF16), pltpu.VMEM((t, RET_DK), BF16), pltpu.VMEM((t, RET_DV), BF16),
                        pltpu.VMEM((t, RET_DV), BF16), pltpu.VMEM((t, RET_DK), F32), pltpu.VMEM((t, RET_DV), F32)],
        name=name, compiler_params=_cp())(h, h, h, h, c2, s2, lgt, o, dy)


def _rot_b(x, cb, shi, slo):
    return x * cb + pltpu.roll(x, ROPE_DIMS // 2, 1) * shi + pltpu.roll(x, LANES - ROPE_DIMS // 2, 1) * slo


def _rot_b_t(dy, cb, shi, slo):
    return dy * cb + pltpu.roll(dy * shi, LANES - ROPE_DIMS // 2, 1) + pltpu.roll(dy * slo, ROPE_DIMS // 2, 1)


def _dil_specs(t):
    base = (2 * RET_QK_W + 2 * RET_V_W) // LANES
    npair = DIL_W // LANES
    q = pl.BlockSpec((t, LANES), lambda p: (0, base + p))
    k = pl.BlockSpec((t, LANES), lambda p: (0, base + npair + p))
    v = pl.BlockSpec((t, LANES), lambda p: (0, base + 2 * npair + p))
    tab = pl.BlockSpec((t, LANES), lambda p: (0, 0))
    strip = pl.BlockSpec((ATT_BLK, t), lambda p: (0, 0))
    pair = pl.BlockSpec((t, LANES), lambda p: (0, p))
    return q, k, v, tab, strip, pair


def _dil_fwd(h, cb, shi, slo, strip, *, name):
    t = h.shape[0]
    nblk = t // ATT_BLK
    scale = DIL_HD ** -0.5

    def body(q_ref, k_ref, v_ref, cb_ref, shi_ref, slo_ref, st_ref, o_ref, yb_ref, lse_ref, qs, ks, vs):
        cb_, shi_, slo_ = cb_ref[...], shi_ref[...], slo_ref[...]
        lane = lax.broadcasted_iota(jnp.int32, (t, LANES), 1)
        qr = _rot_b(q_ref[...], cb_, shi_, slo_) * scale
        qs[0] = jnp.where(lane < DIL_HD, qr, 0.0).astype(BF16)
        qs[1] = jnp.where(lane >= DIL_HD, qr, 0.0).astype(BF16)
        ks[...] = _rot_b(k_ref[...], cb_, shi_, slo_).astype(BF16)
        vs[...] = v_ref[...].astype(BF16)
        lane_b = lax.broadcasted_iota(jnp.int32, (ATT_BLK, LANES), 1)
        for i in range(nblk):
            w = (i + 1) * ATT_BLK
            rows = pl.ds(i * ATT_BLK, ATT_BLK)
            logc = st_ref[:, t - w:t]
            outs, lses = [], []
            for hd in range(2):
                s = _nt(qs[hd, rows, :], ks[0:w, :]) + logc
                m = jnp.max(s, -1, keepdims=True)
                p = jnp.exp(s - m)
                l = jnp.sum(p, -1, keepdims=True)
                outs.append(_nn(p.astype(BF16), vs[0:w, :]) / l)
                lses.append(m + jnp.log(l))
            o = jnp.where(lane_b < DIL_HD, outs[0], outs[1])
            o_ref[rows, :] = o
            yb_ref[rows, :] = o.astype(BF16)
            lse_ref[rows, :] = jnp.where(lane_b < DIL_HD, lses[0], lses[1])

    q, k, v, tab, strip_spec, pair = _dil_specs(t)
    return pl.pallas_call(
        body, grid=(DIL_W // LANES,), in_specs=[q, k, v, tab, tab, tab, strip_spec], out_specs=[pair, pair, pair],
        out_shape=[jax.ShapeDtypeStruct((t, DIL_W), F32), jax.ShapeDtypeStruct((t, DIL_W), BF16),
                   jax.ShapeDtypeStruct((t, DIL_W), F32)],
        scratch_shapes=[pltpu.VMEM((2, t, LANES), BF16), pltpu.VMEM((t, LANES), BF16), pltpu.VMEM((t, LANES), BF16)],
        name=name, compiler_params=_cp())(h, h, h, cb, shi, slo, strip)


def _dil_bwd(h, cb, shi, slo, strip, o, lse, dy, *, name):
    t = h.shape[0]
    nblk = t // ATT_BLK
    scale = DIL_HD ** -0.5

    def body(q_ref, k_ref, v_ref, cb_ref, shi_ref, slo_ref, st_ref, o_ref, lse_ref, dy_ref,
             dq_ref, dk_ref, dv_ref, qs, ks, vs, dos, dls, dka, dva):
        cb_, shi_, slo_ = cb_ref[...], shi_ref[...], slo_ref[...]
        lane = lax.broadcasted_iota(jnp.int32, (t, LANES), 1)
        qr = _rot_b(q_ref[...], cb_, shi_, slo_) * scale
        qs[0] = jnp.where(lane < DIL_HD, qr, 0.0).astype(BF16)
        qs[1] = jnp.where(lane >= DIL_HD, qr, 0.0).astype(BF16)
        ks[...] = _rot_b(k_ref[...], cb_, shi_, slo_).astype(BF16)
        vs[...] = v_ref[...].astype(BF16)
        do = dy_ref[...]
        prod = do * o_ref[...]
        d0 = jnp.sum(jnp.where(lane < DIL_HD, prod, 0.0), -1, keepdims=True)
        d1 = jnp.sum(jnp.where(lane >= DIL_HD, prod, 0.0), -1, keepdims=True)
        dls[...] = jnp.where(lane < DIL_HD, d0, d1)
        dos[0] = jnp.where(lane < DIL_HD, do, 0.0).astype(BF16)
        dos[1] = jnp.where(lane >= DIL_HD, do, 0.0).astype(BF16)
        dka[...] = jnp.zeros_like(dka)
        dva[...] = jnp.zeros_like(dva)
        lane_b = lax.broadcasted_iota(jnp.int32, (ATT_BLK, LANES), 1)
        for i in range(nblk):
            w = (i + 1) * ATT_BLK
            rows = pl.ds(i * ATT_BLK, ATT_BLK)
            logc = st_ref[:, t - w:t]
            dqs = []
            for hd in range(2):
                col = hd * DIL_HD
                qh = qs[hd, rows, :]
                doh = dos[hd, rows, :]
                lse_h = lse_ref[rows, col:col + 1]
                dl_h = dls[rows, col:col + 1]
                p = jnp.exp(_nt(qh, ks[0:w, :]) + logc - lse_h)
                dp = _nt(doh, vs[0:w, :])
                ds = (p * (dp - dl_h)).astype(BF16)
                dqs.append(_nn(ds, ks[0:w, :]))
                dka[0:w, :] += _tn(ds, qh)
                dva[0:w, :] += _tn(p.astype(BF16), doh)
            dq = jnp.where(lane_b < DIL_HD, dqs[0], dqs[1]) * scale
            dq_ref[rows, :] = _rot_b_t(dq, cb_ref[rows, :], shi_ref[rows, :], slo_ref[rows, :]).astype(BF16)
        dk_ref[...] = _rot_b_t(dka[...], cb_, shi_, slo_).astype(BF16)
        dv_ref[...] = dva[...].astype(BF16)

    q, k, v, tab, strip_spec, pair = _dil_specs(t)
    dy_spec = pl.BlockSpec((t, LANES), lambda p: (0, RET_V_W // LANES + p))
    return pl.pallas_call(
        body, grid=(DIL_W // LANES,), in_specs=[q, k, v, tab, tab, tab, strip_spec, pair, pair, dy_spec],
        out_specs=[pair, pair, pair],
        out_shape=[jax.ShapeDtypeStruct((t, DIL_W), BF16)] * 3,
        scratch_shapes=[pltpu.VMEM((2, t, LANES), BF16), pltpu.VMEM((t, LANES), BF16), pltpu.VMEM((t, LANES), BF16),
                        pltpu.VMEM((2, t, LANES), BF16), pltpu.VMEM((t, LANES), F32),
                        pltpu.VMEM((t, LANES), F32), pltpu.VMEM((t, LANES), F32)],
        name=name, compiler_params=_cp())(h, h, h, cb, shi, slo, strip, o, lse, dy)


def _gdn_prep_fwd(h, cw, *, name):
    t = h.shape[0]
    qscale = GDN_DK ** -0.5

    def body(hq_ref, hk_ref, hv_ref, wq_ref, wk_ref, wv_ref, q_ref, k_ref, v_ref):
        row = lax.broadcasted_iota(jnp.int32, (t, GDN_DK), 0)
        sq = _silu(_dwconv(hq_ref[...], wq_ref, row))
        sk = _silu(_dwconv(hk_ref[...], wk_ref, row))
        q_ref[0] = sq * lax.rsqrt(jnp.sum(sq * sq, -1, keepdims=True) + 1e-6) * qscale
        k_ref[0] = sk * lax.rsqrt(jnp.sum(sk * sk, -1, keepdims=True) + 1e-6)
        v_ref[0] = _silu(_dwconv(hv_ref[...], wv_ref, row))

    hs = lambda off: pl.BlockSpec((t, GDN_DK), lambda i: (0, i + off))
    ws = lambda off: pl.BlockSpec((GDN_CONV, GDN_DK), lambda i: (0, i + off))
    out = pl.BlockSpec((1, t, GDN_DK), lambda i: (i, 0, 0))
    return pl.pallas_call(
        body, grid=(GDN_HEADS,), in_specs=[hs(0), hs(8), hs(16), ws(0), ws(8), ws(16)], out_specs=[out, out, out],
        out_shape=[jax.ShapeDtypeStruct((GDN_HEADS, t, GDN_DK), F32)] * 3,
        name=name, compiler_params=_cp())(h, h, h, cw, cw, cw)


def _gdn_prep_bwd(h, cw, dq, dk, dv, *, name):
    t = h.shape[0]
    qscale = GDN_DK ** -0.5

    def body(hq_ref, hk_ref, hv_ref, wq_ref, wk_ref, wv_ref, dq_ref, dk_ref, dv_ref,
             dhq_ref, dhk_ref, dhv_ref, dwq_ref, dwk_ref, dwv_ref):
        row = lax.broadcasted_iota(jnp.int32, (t, GDN_DK), 0)

        def one(h_ref, w_ref, d_ref, dh_ref, dw_ref, norm, sc):
            u = h_ref[...]
            c = _dwconv(u, w_ref, row)
            d = d_ref[0]
            if norm:
                s = _silu(c)
                r = lax.rsqrt(jnp.sum(s * s, -1, keepdims=True) + 1e-6)
                n = s * r
                d = d * sc
                d = r * (d - n * jnp.sum(d * n, -1, keepdims=True))
            dc = d * _dsilu(c)
            dh_ref[...] = _dwconv_bwd(u, w_ref, dc, row, dw_ref).astype(BF16)

        one(hq_ref, wq_ref, dq_ref, dhq_ref, dwq_ref, True, qscale)
        one(hk_ref, wk_ref, dk_ref, dhk_ref, dwk_ref, True, 1.0)
        one(hv_ref, wv_ref, dv_ref, dhv_ref, dwv_ref, False, 1.0)

    hs = lambda off: pl.BlockSpec((t, GDN_DK), lambda i: (0, i + off))
    ws = lambda off: pl.BlockSpec((GDN_CONV, GDN_DK), lambda i: (0, i + off))
    hd = pl.BlockSpec((1, t, GDN_DK), lambda i: (i, 0, 0))
    return pl.pallas_call(
        body, grid=(GDN_HEADS,), in_specs=[hs(0), hs(8), hs(16), ws(0), ws(8), ws(16), hd, hd, hd],
        out_specs=[hs(0), hs(0), hs(0), ws(0), ws(0), ws(0)],
        out_shape=[jax.ShapeDtypeStruct((t, GDN_W), BF16)] * 3 + [jax.ShapeDtypeStruct((GDN_CONV, GDN_W), F32)] * 3,
        name=name, compiler_params=_cp())(h, h, h, cw, cw, cw, dq, dk, dv)


def _make_mm2(wide):
    def raw(a, b, dims):
        if wide:
            return lax.dot_general(a, b, (dims, ((), ())), precision=lax.Precision.HIGHEST, preferred_element_type=F32)
        return lax.dot_general(a.astype(BF16), b.astype(BF16), (dims, ((), ())), preferred_element_type=F32)

    @jax.custom_vjp
    def nn(a, b):
        return raw(a, b, ((1,), (0,)))

    @jax.custom_vjp
    def nt(a, b):
        return raw(a, b, ((1,), (1,)))

    @jax.custom_vjp
    def tn(a, b):
        return raw(a, b, ((0,), (0,)))

    nn.defvjp(lambda a, b: (nn(a, b), (a, b)), lambda r, g: (nt(g, r[1]), tn(r[0], g)))
    nt.defvjp(lambda a, b: (nt(a, b), (a, b)), lambda r, g: (nn(g, r[1]), tn(g, r[0])))
    tn.defvjp(lambda a, b: (tn(a, b), (a, b)), lambda r, g: (nt(r[1], g), nn(r[0], g)))
    return nn, nt, tn


_NN, _NT, _TN = _make_mm2(False)
_NNW, _NTW, _TNW = _make_mm2(True)


def _square_masks(c):
    ri = lax.broadcasted_iota(jnp.int32, (c, c), 0)
    ci = lax.broadcasted_iota(jnp.int32, (c, c), 1)
    return ri >= ci, ri > ci, ri == ci


def _cumsum_rows(m):
    tri, _, _ = _square_masks(m.shape[0])
    return _NNW(tri.astype(F32), m)


def _transpose_sq(m):
    _, _, eye = _square_masks(m.shape[0])
    return _NTW(eye.astype(F32), m)


@jax.custom_vjp
def _inv_unit_lower(l):
    c = l.shape[0]
    _, _, eye = _square_masks(c)
    p = -l
    t = eye.astype(F32) + p
    for _ in range(int(math.log2(c)) - 1):
        p = _NNW(p, p)
        t = t + _NNW(t, p)
    return t


def _inv_fwd(l):
    t = _inv_unit_lower(l)
    return t, t


def _inv_bwd(t, dt):
    return (-_NTW(_TNW(t, dt), t),)


_inv_unit_lower.defvjp(_inv_fwd, _inv_bwd)


def _softplus(x):
    return jnp.maximum(x, 0.0) + jnp.log1p(jnp.exp(-jnp.abs(x)))


def _gdn_chunk(q, k, v, braw, araw, alog, dtb, state):
    c = q.shape[0]
    dv = v.shape[1]
    tri, strict, _ = _square_masks(c)
    beta = _sig(braw)
    g = -jnp.exp(alog) * _softplus(araw + dtb)
    gcm = _cumsum_rows(g * jnp.ones((c, c), F32))
    gct = _transpose_sq(gcm)
    decay = jnp.where(tri, jnp.exp(jnp.where(tri, gcm - gct, 0.0)), 0.0)
    gc = jnp.sum(gcm, 1, keepdims=True) * (1.0 / c)
    glast = jnp.sum(g, 0, keepdims=True)
    egc = jnp.exp(gc)
    kb = k * beta
    tm = _inv_unit_lower(jnp.where(strict, _NT(kb, k) * decay, 0.0))
    sol = _NNW(tm, jnp.concatenate([v * beta, kb * egc], 1))
    u, w = sol[:, :dv], sol[:, dv:]
    attn = jnp.where(tri, _NT(q, k) * decay, 0.0)
    k_dec = k * jnp.exp(glast - gc)
    q_dec = q * egc
    v_new = u - _NN(w, state)
    o = _NN(q_dec, state) + _NN(attn, v_new)
    new_state = state * jnp.exp(glast) + _TN(k_dec, v_new)
    return o, new_state


def _gdn_specs(t, rev):
    nch = t // GDN_CHUNK
    cm = (lambda n: nch - 1 - n) if rev else (lambda n: n)
    tok = pl.BlockSpec((GDN_HEADS, GDN_CHUNK, GDN_DK), lambda n: (0, cm(n), 0))
    par = pl.BlockSpec((GDN_HEADS, 1, LANES), lambda n: (0, 0, 0))
    st = pl.BlockSpec((GDN_HEADS, 1, GDN_DK, GDN_DV), lambda n: (0, cm(n), 0, 0))
    return tok, par, st


def _gdn_core_fwd(q, k, v, bb, ab, alog, dtb, *, name):
    t = q.shape[1]
    nch = t // GDN_CHUNK

    def body(q_ref, k_ref, v_ref, bb_ref, ab_ref, al_ref, dt_ref, o_ref, st_ref, state):
        @pl.when(pl.program_id(0) == 0)
        def _():
            state[...] = jnp.zeros_like(state)

        s0 = state[...]
        st_ref[:, 0] = s0
        o, s1 = jax.vmap(_gdn_chunk)(q_ref[...], k_ref[...], v_ref[...], bb_ref[:, :, 0:1], ab_ref[:, :, 0:1],
                                     al_ref[:, :, 0:1], dt_ref[:, :, 0:1], s0)
        o_ref[...] = o
        state[...] = s1

    tok, par, st = _gdn_specs(t, False)
    return pl.pallas_call(
        body, grid=(nch,), in_specs=[tok, tok, tok, tok, tok, par, par], out_specs=[tok, st],
        out_shape=[jax.ShapeDtypeStruct((GDN_HEADS, t, GDN_DV), F32),
                   jax.ShapeDtypeStruct((GDN_HEADS, nch, GDN_DK, GDN_DV), F32)],
        scratch_shapes=[pltpu.VMEM((GDN_HEADS, GDN_DK, GDN_DV), F32)],
        name=name, compiler_params=_cp())(q, k, v, bb, ab, alog, dtb)


def _gdn_core_bwd(q, k, v, bb, ab, alog, dtb, states, do, *, name):
    t = q.shape[1]
    nch = t // GDN_CHUNK

    def body(q_ref, k_ref, v_ref, bb_ref, ab_ref, al_ref, dt_ref, st_ref, do_ref,
             dq_ref, dk_ref, dv_ref, dbb_ref, dab_ref, dal_ref, ddt_ref, dstate):
        @pl.when(pl.program_id(0) == 0)
        def _():
            dstate[...] = jnp.zeros_like(dstate)
            dal_ref[...] = jnp.zeros_like(dal_ref)
            ddt_ref[...] = jnp.zeros_like(ddt_ref)

        args = (q_ref[...], k_ref[...], v_ref[...], bb_ref[:, :, 0:1], ab_ref[:, :, 0:1],
                al_ref[:, :, 0:1], dt_ref[:, :, 0:1], st_ref[:, 0])
        _, pull = jax.vjp(jax.vmap(_gdn_chunk), *args)
        dq, dk, dv, dbr, dar, dal, ddt, ds = pull((do_ref[...], dstate[...]))
        dq_ref[...] = dq
        dk_ref[...] = dk
        dv_ref[...] = dv
        dbb_ref[...] = dbr + jnp.zeros((GDN_HEADS, GDN_CHUNK, LANES), F32)
        dab_ref[...] = dar + jnp.zeros((GDN_HEADS, GDN_CHUNK, LANES), F32)
        dal_ref[...] += dal + jnp.zeros((GDN_HEADS, 1, LANES), F32)
        ddt_ref[...] += ddt + jnp.zeros((GDN_HEADS, 1, LANES), F32)
        dstate[...] = ds

    tok, par, st = _gdn_specs(t, True)
    tokshape = jax.ShapeDtypeStruct((GDN_HEADS, t, GDN_DK), F32)
    parshape = jax.ShapeDtypeStruct((GDN_HEADS, 1, LANES), F32)
    return pl.pallas_call(
        body, grid=(nch,), in_specs=[tok, tok, tok, tok, tok, par, par, st, tok],
        out_specs=[tok, tok, tok, tok, tok, par, par],
        out_shape=[tokshape] * 5 + [parshape] * 2,
        scratch_shapes=[pltpu.VMEM((GDN_HEADS, GDN_DK, GDN_DV), F32)],
        name=name, compiler_params=_cp())(q, k, v, bb, ab, alog, dtb, states, do)


GDN_ROWS = 512


def _gdn_post_fwd(o, h, nw, *, name):
    t = o.shape[1]

    def body(o_ref, g_ref, nw_ref, y_ref):
        oo = o_ref[0]
        r = lax.rsqrt(jnp.mean(oo * oo, -1, keepdims=True) + EPS)
        y_ref[...] = (oo * r * nw_ref[...] * _silu(g_ref[...])).astype(BF16)

    return pl.pallas_call(
        body, grid=(GDN_HEADS, t // GDN_ROWS),
        in_specs=[pl.BlockSpec((1, GDN_ROWS, GDN_DV), lambda hh, i: (hh, i, 0)),
                  pl.BlockSpec((GDN_ROWS, GDN_DV), lambda hh, i: (i, 3 * GDN_HEADS + hh)),
                  pl.BlockSpec((1, GDN_DV), lambda hh, i: (0, 0))],
        out_specs=pl.BlockSpec((GDN_ROWS, GDN_DV), lambda hh, i: (i, hh)),
        out_shape=jax.ShapeDtypeStruct((t, GDN_W), BF16), name=name, compiler_params=_cp())(o, h, nw)


def _gdn_post_bwd(o, h, nw, dy, *, name):
    t = o.shape[1]

    def body(o_ref, g_ref, nw_ref, dy_ref, do_ref, dg_ref, dnw_ref):
        oo, gg, nw_, dy_ = o_ref[0], g_ref[...], nw_ref[...], dy_ref[...]
        r = lax.rsqrt(jnp.mean(oo * oo, -1, keepdims=True) + EPS)
        n = oo * r
        sg = _silu(gg)
        dg_ref[...] = (dy_ * n * nw_ * _dsilu(gg)).astype(BF16)
        dn = dy_ * sg * nw_
        do_ref[0] = r * (dn - n * jnp.mean(dn * n, -1, keepdims=True))

        @pl.when((pl.program_id(0) == 0) & (pl.program_id(1) == 0))
        def _():
            dnw_ref[...] = jnp.zeros_like(dnw_ref)

        dnw_ref[...] += jnp.sum(dy_ * sg * n, 0, keepdims=True)

    return pl.pallas_call(
        body, grid=(GDN_HEADS, t // GDN_ROWS),
        in_specs=[pl.BlockSpec((1, GDN_ROWS, GDN_DV), lambda hh, i: (hh, i, 0)),
                  pl.BlockSpec((GDN_ROWS, GDN_DV), lambda hh, i: (i, 3 * GDN_HEADS + hh)),
                  pl.BlockSpec((1, GDN_DV), lambda hh, i: (0, 0)),
                  pl.BlockSpec((GDN_ROWS, GDN_DV), lambda hh, i: (i, hh))],
        out_specs=[pl.BlockSpec((1, GDN_ROWS, GDN_DV), lambda hh, i: (hh, i, 0)),
                   pl.BlockSpec((GDN_ROWS, GDN_DV), lambda hh, i: (i, hh)),
                   pl.BlockSpec((1, GDN_DV), lambda hh, i: (0, 0))],
        out_shape=[jax.ShapeDtypeStruct((GDN_HEADS, t, GDN_DV), F32), jax.ShapeDtypeStruct((t, GDN_W), BF16),
                   jax.ShapeDtypeStruct((1, GDN_DV), F32)],
        name=name, compiler_params=_cp())(o, h, nw, dy)


def _tables(positions):
    pos = positions.astype(F32)[:, None]
    half = RET_DK // 2
    inv = jnp.power(RET_THETA, -jnp.arange(half, dtype=F32) * 2.0 / RET_DK)
    ang = pos * inv
    cos, sin = jnp.cos(ang), jnp.sin(ang)
    c2a = jnp.concatenate([cos, cos], 1)
    s2a = jnp.concatenate([-sin, sin], 1)
    hb = ROPE_DIMS // 2
    invb = jnp.power(ROPE_THETA, -jnp.arange(hb, dtype=F32) * 2.0 / ROPE_DIMS)
    angb = pos * invb
    cosb, sinb = jnp.cos(angb), jnp.sin(angb)
    t = pos.shape[0]
    ones = jnp.ones((t, DIL_HD - ROPE_DIMS), F32)
    zeros = jnp.zeros((t, DIL_HD - ROPE_DIMS), F32)
    z8 = jnp.zeros((t, hb), F32)
    cb = jnp.concatenate([cosb, cosb, ones] * 2, 1)
    shi = jnp.concatenate([z8, sinb, zeros] * 2, 1)
    slo = jnp.concatenate([-sinb, z8, zeros] * 2, 1)
    lg = jnp.log1p(-jnp.power(2.0, -5.0 - jnp.arange(RET_HEADS, dtype=F32)))
    lgt = jnp.broadcast_to(lg[:, None, None], (RET_HEADS, 1, LANES))
    delta = jnp.arange(ATT_BLK, dtype=jnp.int32)[:, None] + (SEQ - ATT_BLK) - jnp.arange(SEQ, dtype=jnp.int32)[None, :]
    cnt = jnp.zeros(delta.shape, F32)
    for (w, d) in DIL_PAIRS:
        cnt = cnt + ((delta >= 0) & (delta <= w) & (delta % d == 0)).astype(F32)
    strip = jnp.where(cnt > 0, jnp.log(jnp.maximum(cnt, 1.0)), NEG)
    return c2a, s2a, cb, shi, slo, lgt, strip


def _local_step(x, positions, target, get_w, mid, put_g, small):
    c2a, s2a, cb, shi, slo, lgt, strip = _tables(positions)
    t = x.shape[0]
    saved = []
    xf = x
    xb = x.astype(BF16)
    for layer in range(DEPTH):
        j = layer // 2
        L = f"L{layer}_"
        W, dep = get_w(layer, "mixer", xb)
        rec = {"x": xf, "xb": xb}
        if layer % 2 == 0:
            h = _mm(xb, W["in_t"], tb=True, name=L + "ev_in", dep=dep)
            ro, ya = _ret_fwd(h, c2a, s2a, lgt, name=L + "ret_fwd")
            do_, yb, lse = _dil_fwd(h, cb, shi, slo, strip, name=L + "dil_fwd")
            y = jnp.concatenate([ya, yb], 1)
            mix = _mm(y, W["out"], name=L + "ev_out", dep=mid(layer, "mixer", y))
            rec.update(h=h, ro=ro, dil_o=do_, lse=lse, y=y)
        else:
            h = _mm(xb, W["in_t"], tb=True, name=L + "od_in", dep=dep)
            cw = W["conv"]
            q, k, v = _gdn_prep_fwd(h, cw, name=L + "gdn_prep")
            hs = h[:, 4 * GDN_W:4 * GDN_W + 2 * GDN_HEADS]
            bb = jnp.broadcast_to(hs[:, :GDN_HEADS].T[:, :, None], (GDN_HEADS, t, LANES))
            ab = jnp.broadcast_to(hs[:, GDN_HEADS:].T[:, :, None], (GDN_HEADS, t, LANES))
            alog = jnp.broadcast_to(small["od_a_log"][j][:, None, None], (GDN_HEADS, 1, LANES))
            dtb = jnp.broadcast_to(small["od_dt_bias"][j][:, None, None], (GDN_HEADS, 1, LANES))
            o, states = _gdn_core_fwd(q, k, v, bb, ab, alog, dtb, name=L + "gdn_fwd")
            nw = small["od_norm_w"][j][None, :]
            y = _gdn_post_fwd(o, h, nw, name=L + "gdn_post")
            mix = _mm(y, W["out"], name=L + "od_out", dep=mid(layer, "mixer", y))
            rec.update(h=h, q=q, k=k, v=v, bb=bb, ab=ab, alog=alog, dtb=dtb, states=states, o=o, y=y, nw=nw, cw=cw)
        z1, x1, x1b = _ln_fwd(xf, mix, small["ln1_g"][layer][None], small["ln1_b"][layer][None], name=L + "ln1")
        rec["Wm"] = W
        W, dep = get_w(layer, "ffn", x1b)
        rec["Wf"] = W
        u = _mm(x1b, W["up_t"], tb=True, name=L + "ffn_up", dep=dep)
        fcw = W["fconv"]
        fcb = small["ffn_conv_b"][layer][None]
        a = _ffn_mid_fwd(u, fcw, fcb, name=L + "ffn_mid")
        f = _mm(a, W["down"], name=L + "ffn_down", dep=mid(layer, "ffn", a))
        z2, x2, x2b = _ln_fwd(x1, f, small["ln2_g"][layer][None], small["ln2_b"][layer][None], name=L + "ln2")
        rec.update(z1=z1, x1b=x1b, u=u, a=a, z2=z2, fcw=fcw, fcb=fcb)
        saved.append(rec)
        xf, xb = x2, x2b

    dy, lossv = _loss_head(xf, target, name="loss_head")
    loss = lossv[0, 0]

    gS = {n: [None] * small[n].shape[0] for n in small}
    dres, dmm = dy, None
    for layer in reversed(range(DEPTH)):
        j = layer // 2
        L = f"L{layer}_"
        rec = saved[layer]
        Wm, Wf = rec["Wm"], rec["Wf"]
        g = {}
        if dmm is None:
            dz2, dz2b, dg2, db2 = _ln_bwd(rec["z2"], small["ln2_g"][layer][None], dres, None, name=L + "ln2_bwd")
        else:
            dz2, dz2b, dg2, db2 = _ln_bwd(rec["z2"], small["ln2_g"][layer][None], dmm, dres, name=L + "ln2_bwd")
        gS["ln2_g"][layer], gS["ln2_b"][layer] = dg2[0], db2[0]
        g["down"] = _mm(rec["a"], dz2b, ta=True, name=L + "ffn_down_dw", out_dtype=BF16)
        da = _mm(dz2b, Wf["down"], tb=True, name=L + "ffn_down_dx")
        du, dcw, dcb = _ffn_mid_bwd(rec["u"], rec["fcw"], rec["fcb"], da, name=L + "ffn_mid_bwd")
        g["fconv"] = dcw.astype(BF16)
        gS["ffn_conv_b"][layer] = dcb[0]
        g["up_t"] = _mm(du, rec["x1b"], ta=True, name=L + "ffn_up_dw", out_dtype=BF16)
        dep = put_g(layer, "ffn", g)
        dx1 = _mm(du, Wf["up_t"], name=L + "ffn_up_dx", dep=dep)
        dz1, dz1b, dg1, db1 = _ln_bwd(rec["z1"], small["ln1_g"][layer][None], dx1, dz2, name=L + "ln1_bwd")
        gS["ln1_g"][layer], gS["ln1_b"][layer] = dg1[0], db1[0]
        g = {}
        if layer % 2 == 0:
            g["out"] = _mm(rec["y"], dz1b, ta=True, name=L + "ev_out_dw", out_dtype=BF16)
            dyy = _mm(dz1b, Wm["out"], tb=True, name=L + "ev_out_dx")
            dqa, dka, dva, dga = _ret_bwd(rec["h"], c2a, s2a, lgt, rec["ro"], dyy, name=L + "ret_bwd")
            dqb, dkb, dvb = _dil_bwd(rec["h"], cb, shi, slo, strip, rec["dil_o"], rec["lse"], dyy, name=L + "dil_bwd")
            dh = jnp.concatenate([dqa, dka, dva, dga, dqb, dkb, dvb], 1)
            g["in_t"] = _mm(dh, rec["xb"], ta=True, name=L + "ev_in_dw", out_dtype=BF16)
            dep = put_g(layer, "mixer", g)
            dxin = _mm(dh, Wm["in_t"], name=L + "ev_in_dx", dep=dep)
        else:
            g["out"] = _mm(rec["y"], dz1b, ta=True, name=L + "od_out_dw", out_dtype=BF16)
            dyy = _mm(dz1b, Wm["out"], tb=True, name=L + "od_out_dx")
            do, dgate, dnw = _gdn_post_bwd(rec["o"], rec["h"], rec["nw"], dyy, name=L + "gdn_post_bwd")
            gS["od_norm_w"][j] = dnw[0]
            dq, dk, dv, dbb, dab, dal, ddt = _gdn_core_bwd(
                rec["q"], rec["k"], rec["v"], rec["bb"], rec["ab"], rec["alog"], rec["dtb"], rec["states"], do,
                name=L + "gdn_bwd")
            gS["od_a_log"][j] = dal[:, 0, 0]
            gS["od_dt_bias"][j] = ddt[:, 0, 0]
            dhq, dhk, dhv, dwq, dwk, dwv = _gdn_prep_bwd(rec["h"], rec["cw"], dq, dk, dv, name=L + "gdn_prep_bwd")
            g["conv"] = jnp.concatenate([dwq, dwk, dwv], 1).astype(BF16)
            dsm = jnp.concatenate([dbb[:, :, 0].T, dab[:, :, 0].T,
                                   jnp.zeros((t, LANES - 2 * GDN_HEADS), F32)], 1).astype(BF16)
            dh = jnp.concatenate([dhq, dhk, dhv, dgate, dsm], 1)
            g["in_t"] = _mm(dh, rec["xb"], ta=True, name=L + "od_in_dw", out_dtype=BF16)
            dep = put_g(layer, "mixer", g)
            dxin = _mm(dh, Wm["in_t"], name=L + "od_in_dx", dep=dep)
        dres, dmm = dz1, dxin
    grad_x = _axpy(dmm, dres, name="grad_x")
    gS = {n: jnp.stack(v) for n, v in gS.items()}
    return loss, grad_x, gS


HBM = pl.BlockSpec(memory_space=pltpu.HBM)


def _me():
    return lax.axis_index("x"), lax.axis_index("y"), lax.axis_index("c")


def _my_index():
    x, y, c = _me()
    return 4 * x + 2 * y + c


SEM = pl.BlockSpec(memory_space=pltpu.SEMAPHORE)
ANY = pl.BlockSpec(memory_space=pl.ANY)
PLANS = {"scatter": (1, 2, 3, 4, 5, 6, 7), "spread": (1, 2, 4, 6), "relay": (2, 4, 6)}
SIBLING = 1


def _peer(kk):
    x, y, c = _me()
    return x ^ (kk >> 2), y ^ ((kk >> 1) & 1), c ^ (kk & 1)


def _peer_index(kk):
    px, py, pc = _peer(kk)
    return 4 * px + 2 * py + pc


def _job_copies(mode, srcs, lands, send_sems, recv_sems, incoming):
    myid = _my_index()
    plan = PLANS[mode]
    out = []
    for a in range(len(lands)):
        for idx, kk in enumerate(plan):
            if mode == "relay":
                to, src = _peer(SIBLING), lands[a].at[_peer_index(kk)]
                slot_there, slot_here = _peer_index(kk), _peer_index(kk ^ SIBLING)
            else:
                to, src = _peer(kk), (srcs[a] if mode == "spread" else srcs[a].at[_peer_index(kk)])
                slot_there, slot_here = myid, _peer_index(kk)
            sem = a * len(plan) + idx
            out.append(pltpu.make_async_remote_copy(
                src_ref=src, dst_ref=lands[a].at[slot_here if incoming else slot_there],
                send_sem=send_sems.at[sem], recv_sem=recv_sems.at[sem], device_id=to, device_id_type=MESH))
    return out


def _split_jobs(jobs, arrays):
    out, o = [], 0
    for (_, srcs, lands) in jobs:
        out.append((arrays[o:o + len(srcs)], arrays[o + len(srcs):o + len(srcs) + len(lands)]))
        o += len(srcs) + len(lands)
    return out


def _exchange_start(jobs, after, *, name):
    jobs = [(mode, list(srcs), [lax.empty((N_DEV, *s.shape) if mode == "spread" else s.shape, s.dtype) for s in srcs]
             if lands is None else list(lands)) for (mode, srcs, lands) in jobs]
    flat = [a for (_, srcs, lands) in jobs for a in (*srcs, *lands)]
    n, nj = len(flat), len(jobs)
    nsem = [len(PLANS[mode]) * len(lands) for (mode, _, lands) in jobs]

    def body(*refs):
        o = n + (0 if after is None else 1)
        sems, token = refs[o:o + 2 * nj], refs[o + 2 * nj + n]
        for ji, ((mode, _, _), (src, land)) in enumerate(zip(jobs, _split_jobs(jobs, refs[:n]))):
            for cp in _job_copies(mode, src, land, sems[2 * ji], sems[2 * ji + 1], False):
                cp.start()
        token[...] = jnp.zeros_like(token)

    outs = pl.pallas_call(
        body, name=name,
        out_shape=(*[pltpu.SemaphoreType.DMA((ns,)) for ns in nsem for _ in range(2)],
                   *[pltpu.HBM(a.shape, a.dtype) for a in flat], jax.ShapeDtypeStruct((8, LANES), F32)),
        in_specs=[HBM] * n + ([] if after is None else [ANY]),
        out_specs=(*[SEM] * (2 * nj), *[HBM] * n, pl.BlockSpec(memory_space=pltpu.VMEM)),
        input_output_aliases={i: 2 * nj + i for i in range(n)},
        compiler_params=pltpu.CompilerParams(has_side_effects=pltpu.SideEffectType.DATAFLOW_SIDE_EFFECTING),
    )(*[pltpu.with_memory_space_constraint(a, pltpu.HBM) for a in flat], *([] if after is None else [after]))
    thru = _split_jobs(jobs, list(outs[2 * nj:2 * nj + n]))
    started = [(mode, outs[2 * ji], outs[2 * ji + 1], src, land) for ji, ((mode, _, _), (src, land)) in enumerate(zip(jobs, thru))]
    return started, outs[2 * nj + n]


def _exchange_wait(started, after, *, name):
    jobs = [(mode, srcs, lands) for (mode, _, _, srcs, lands) in started]
    flat = [a for (_, srcs, lands) in jobs for a in (*srcs, *lands)]
    n, nj = len(flat), len(jobs)

    def body(*refs):
        sems = refs[n:n + 2 * nj]
        for ji, ((mode, _, _), (src, land)) in enumerate(zip(jobs, _split_jobs(jobs, refs[:n]))):
            for cp in _job_copies(mode, src, land, sems[2 * ji], sems[2 * ji + 1], True):
                cp.wait_send()
                cp.wait_recv()

    outs = pl.pallas_call(
        body, name=name, out_shape=tuple(pltpu.HBM(a.shape, a.dtype) for a in flat),
        in_specs=[HBM] * n + [SEM] * (2 * nj) + [ANY], out_specs=tuple([HBM] * n),
        input_output_aliases={i: i for i in range(n)},
        compiler_params=pltpu.CompilerParams(has_side_effects=pltpu.SideEffectType.DATAFLOW_SIDE_EFFECTING),
    )(*flat, *[s for (_, ss, rs, _, _) in started for s in (ss, rs)], after)
    return _split_jobs(jobs, list(outs))


def _sum8(land, *, name):
    _, rr, cc = land.shape
    tr = _row_tile(rr)

    def body(l_ref, o_ref):
        acc = l_ref[0].astype(F32)
        for d in range(1, N_DEV):
            acc = acc + l_ref[d].astype(F32)
        o_ref[...] = acc

    return pl.pallas_call(
        body, grid=(rr // tr,), in_specs=[pl.BlockSpec((N_DEV, tr, cc), lambda i: (0, i, 0))],
        out_specs=pl.BlockSpec((tr, cc), lambda i: (i, 0)), out_shape=jax.ShapeDtypeStruct((rr, cc), F32),
        name=name, compiler_params=_cp())(land)


def _row_tile(rr):
    for cand in (512, 384, 256, 192, 176, 128, 64, 32, 16, 8):
        if rr % cand == 0:
            return cand
    return rr


def _small_exchange(vec, *, name):
    rr = vec.shape[0]

    def body(v_ref, o_ref, send_sems, recv_sems):
        x, y, c = _me()
        myid = 4 * x + 2 * y + c
        o_ref[myid] = v_ref[...]
        cps = []
        for kk in range(1, N_DEV):
            px, py, pc = x ^ (kk >> 2), y ^ ((kk >> 1) & 1), c ^ (kk & 1)
            cps.append(pltpu.make_async_remote_copy(
                src_ref=v_ref, dst_ref=o_ref.at[myid], send_sem=send_sems.at[kk], recv_sem=recv_sems.at[kk],
                device_id=(px, py, pc), device_id_type=MESH))
        for cp in cps:
            cp.start()
        for kk in range(1, N_DEV):
            px, py, pc = x ^ (kk >> 2), y ^ ((kk >> 1) & 1), c ^ (kk & 1)
            pltpu.make_async_remote_copy(
                src_ref=v_ref, dst_ref=o_ref.at[4 * px + 2 * py + pc], send_sem=send_sems.at[kk],
                recv_sem=recv_sems.at[kk], device_id=(px, py, pc), device_id_type=MESH).wait_recv()
        for cp in cps:
            cp.wait_send()

    return pl.pallas_call(
        body, in_specs=[pl.BlockSpec(memory_space=pltpu.VMEM)], out_specs=pl.BlockSpec(memory_space=pltpu.VMEM),
        out_shape=jax.ShapeDtypeStruct((N_DEV, rr, LANES), F32),
        scratch_shapes=[pltpu.SemaphoreType.DMA((N_DEV,)), pltpu.SemaphoreType.DMA((N_DEV,))],
        name=name, compiler_params=pltpu.CompilerParams(has_side_effects=True))(vec)


def _adam_math(w, g, m, v):
    m = ADAM_B1 * m + (1.0 - ADAM_B1) * g
    v = ADAM_B2 * v + (1.0 - ADAM_B2) * (g * g)
    m_hat = m / (1.0 - ADAM_B1 ** ADAM_STEP)
    v_hat = v / (1.0 - ADAM_B2 ** ADAM_STEP)
    delta = -ADAM_LR * (m_hat / (jnp.sqrt(v_hat) + ADAM_EPS) + ADAM_WD * w)
    return delta, m, v


def _adamw_sharded(w, m, v, g, *, name):
    ll, rr, cc = w.shape
    tr = _row_tile(rr)

    def body(w_ref, m_ref, v_ref, g_ref, d_ref, nm_ref, nv_ref):
        d, nm, nv = _adam_math(w_ref[...], g_ref[...], m_ref[...], v_ref[...])
        d_ref[...] = d
        nm_ref[...] = nm
        nv_ref[...] = nv

    blk = pl.BlockSpec((1, tr, cc), lambda l, i: (l, i, 0))
    sh = jax.ShapeDtypeStruct((ll, rr, cc), F32)
    return pl.pallas_call(
        body, grid=(ll, rr // tr), in_specs=[blk] * 4, out_specs=[blk] * 3, out_shape=[sh] * 3,
        name=name, compiler_params=_cp())(w, m, v, g)


def _adamw_small(w, m, v, gall, *, name):
    rr = w.shape[0]

    def body(w_ref, m_ref, v_ref, g_ref, go_ref, d_ref, nm_ref, nv_ref):
        g = g_ref[0]
        for kk in range(1, N_DEV):
            g = g + g_ref[kk]
        d, nm, nv = _adam_math(w_ref[...], g, m_ref[...], v_ref[...])
        go_ref[...] = g
        d_ref[...] = d
        nm_ref[...] = nm
        nv_ref[...] = nv

    sh = jax.ShapeDtypeStruct((rr, LANES), F32)
    return pl.pallas_call(body, out_shape=[sh] * 4, name=name, compiler_params=_cp())(w, m, v, gall)


SHARDED = ("ev_w_in", "ev_w_out", "od_w_in", "od_conv_w", "od_w_out", "ffn_w_up", "ffn_conv_w", "ffn_w_down")
SMALL = ("od_a_log", "od_dt_bias", "od_norm_w", "ffn_conv_b", "ln1_g", "ln1_b", "ln2_g", "ln2_b")
ALL_W = ("ev_w_in", "ev_w_out", "od_w_in", "od_conv_w", "od_a_log", "od_dt_bias", "od_norm_w", "od_w_out",
         "ffn_w_up", "ffn_conv_w", "ffn_conv_b", "ffn_w_down", "ln1_g", "ln1_b", "ln2_g", "ln2_b")


def _layer_items(layer):
    j = layer // 2
    if layer % 2 == 0:
        mixer = [("in_t", "ev_w_in", j, "colT"), ("out", "ev_w_out", j, "row")]
    else:
        mixer = [("in_t", "od_w_in", j, "colT"), ("conv", "od_conv_w", j, "colsmall"), ("out", "od_w_out", j, "row")]
    return mixer + [("up_t", "ffn_w_up", layer, "colT"), ("fconv", "ffn_conv_w", layer, "colsmall"),
                    ("down", "ffn_w_down", layer, "row")]


def _to_send(kind, w, j):
    if kind == "colT":
        return w[j].T.astype(BF16)
    return w[j].astype(BF16) if kind == "row" else w[j]


def _from_gather(kind, name, g):
    if kind == "colsmall":
        return jnp.transpose(g, (1, 0, 2)).reshape(g.shape[1], -1)
    full = g.reshape(-1, g.shape[-1])
    if name == "od_w_in":
        full = jnp.pad(full, ((0, OD_IN_PAD - OD_IN), (0, 0)))
    return full


def _by_owner(kind, name, gfull):
    if kind == "colsmall":
        kk, c8 = gfull.shape
        return jnp.transpose(gfull.reshape(kk, N_DEV, c8 // N_DEV), (1, 0, 2))
    if name == "od_w_in":
        gfull = gfull[:OD_IN]
    return gfull.reshape(N_DEV, gfull.shape[0] // N_DEV, gfull.shape[1])


def _pack_small(d):
    flat = jnp.concatenate([d[n].reshape(-1) for n in SMALL])
    pad = (-flat.shape[0]) % (8 * LANES)
    return jnp.pad(flat, (0, pad)).reshape(-1, LANES)


def _unpack_small(packed, like):
    flat = packed.reshape(-1)
    out, off = {}, 0
    for n in SMALL:
        sz = int(np.prod(like[n].shape))
        out[n] = flat[off:off + sz].reshape(like[n].shape)
        off += sz
    return out


def kernel(x, positions, ev_w_in, ev_w_out, od_w_in, od_conv_w, od_a_log, od_dt_bias, od_norm_w, od_w_out, ffn_w_up, ffn_conv_w, ffn_conv_b, ffn_w_down, ln1_g, ln1_b, ln2_g, ln2_b, loss_target, m_ev_w_in, m_ev_w_out, m_od_w_in, m_od_conv_w, m_od_a_log, m_od_dt_bias, m_od_norm_w, m_od_w_out, m_ffn_w_up, m_ffn_conv_w, m_ffn_conv_b, m_ffn_w_down, m_ln1_g, m_ln1_b, m_ln2_g, m_ln2_b, v_ev_w_in, v_ev_w_out, v_od_w_in, v_od_conv_w, v_od_a_log, v_od_dt_bias, v_od_norm_w, v_od_w_out, v_ffn_w_up, v_ffn_conv_w, v_ffn_conv_b, v_ffn_w_down, v_ln1_g, v_ln1_b, v_ln2_g, v_ln2_b):
    w = dict(ev_w_in=ev_w_in, ev_w_out=ev_w_out, od_w_in=od_w_in, od_conv_w=od_conv_w, od_a_log=od_a_log,
             od_dt_bias=od_dt_bias, od_norm_w=od_norm_w, od_w_out=od_w_out, ffn_w_up=ffn_w_up, ffn_conv_w=ffn_conv_w,
             ffn_conv_b=ffn_conv_b, ffn_w_down=ffn_w_down, ln1_g=ln1_g, ln1_b=ln1_b, ln2_g=ln2_g, ln2_b=ln2_b)
    mom = dict(ev_w_in=m_ev_w_in, ev_w_out=m_ev_w_out, od_w_in=m_od_w_in, od_conv_w=m_od_conv_w, od_a_log=m_od_a_log,
               od_dt_bias=m_od_dt_bias, od_norm_w=m_od_norm_w, od_w_out=m_od_w_out, ffn_w_up=m_ffn_w_up,
               ffn_conv_w=m_ffn_conv_w, ffn_conv_b=m_ffn_conv_b, ffn_w_down=m_ffn_w_down, ln1_g=m_ln1_g,
               ln1_b=m_ln1_b, ln2_g=m_ln2_g, ln2_b=m_ln2_b)
    var = dict(ev_w_in=v_ev_w_in, ev_w_out=v_ev_w_out, od_w_in=v_od_w_in, od_conv_w=v_od_conv_w, od_a_log=v_od_a_log,
               od_dt_bias=v_od_dt_bias, od_norm_w=v_od_norm_w, od_w_out=v_od_w_out, ffn_w_up=v_ffn_w_up,
               ffn_conv_w=v_ffn_conv_w, ffn_conv_b=v_ffn_conv_b, ffn_w_down=v_ffn_w_down, ln1_g=v_ln1_g,
               ln1_b=v_ln1_b, ln2_g=v_ln2_g, ln2_b=v_ln2_b)

    myid = _my_index()
    small = {n: w[n] for n in SMALL}
    groups = [(layer, part) for layer in range(DEPTH) for part in ("mixer", "ffn")]

    def group_items(gi):
        layer, part = groups[gi]
        its = _layer_items(layer)
        return its[:-3] if part == "mixer" else its[-3:]

    level1, level2 = {}, {}

    def spread_job(gi):
        return ("spread", [_to_send(kind, w[n], j) for (_, n, j, kind) in group_items(gi)], None)

    def relay(gi, after, name):
        (srcs, lands), = _exchange_wait([level1.pop(gi)], after, name=name + "_wait")
        more = [spread_job(gi + 1)] if gi + 1 < len(groups) else []
        started, token = _exchange_start([("relay", [], lands)] + more, None, name=name + "_start")
        level2[gi] = (started[0], srcs)
        if more:
            level1[gi + 1] = started[1]
        return token

    def get_w(layer, part, after):
        gi = groups.index((layer, part))
        started, srcs = level2.pop(gi)
        (_, lands), = _exchange_wait([started], after, name=f"gather{gi}_wait")
        lands = [lax.dynamic_update_index_in_dim(l, s, myid, 0) for l, s in zip(lands, srcs)]
        return {key: _from_gather(kind, n, l) for (key, n, _, kind), l in zip(group_items(gi), lands)}, None

    def mid(layer, part, after):
        gi = groups.index((layer, part)) + 1
        return relay(gi, after, f"gather{gi}_relay") if gi < len(groups) else None

    landed = {}
    pending = []

    def scatter_finish(after):
        started, gi = pending.pop()
        (srcs, lands), = _exchange_wait([started], after, name=f"scatter{gi}_wait")
        for (key, _, _, _), l, s in zip(group_items(gi), lands, srcs):
            own = lax.dynamic_index_in_dim(s, myid, 0, keepdims=False)
            landed[(groups[gi][0], key)] = lax.dynamic_update_index_in_dim(l, own, myid, 0)

    def put_g(layer, part, g):
        gi = groups.index((layer, part))
        srcs = [_by_owner(kind, n, g[key]) for (key, n, _, kind) in group_items(gi)]
        (started,), token = _exchange_start([("scatter", srcs, None)], None, name=f"scatter{gi}_start")
        if pending:
            scatter_finish(token)
        pending.append((started, gi))
        return token

    (level1[0],), token = _exchange_start([spread_job(0)], None, name="gather0_spread_start")
    relay(0, token, "gather0_relay")
    loss, grad_x, gS = _local_step(x[0], positions[0], loss_target[0], get_w, mid, put_g, small)
    loss = lax.psum(loss, ("x", "y", "c"))

    outs_g, outs_d, outs_m, outs_v = {}, {}, {}, {}
    where = {n: [None] * w[n].shape[0] for n in SHARDED}
    for layer in range(DEPTH):
        for (key, n, j, kind) in _layer_items(layer):
            where[n][j] = (layer, key, kind)

    def update(n):
        g = jnp.stack([_sum8(landed[(layer, key)], name=f"L{layer}_{key}_sum") for (layer, key, _) in where[n]])
        if where[n][0][2] == "colT":
            tr = lambda a: jnp.swapaxes(a, 1, 2)
            d, nm, nv = _adamw_sharded(tr(w[n]), tr(mom[n]), tr(var[n]), g, name=f"adamw_{n}")
            outs_g[n], outs_d[n], outs_m[n], outs_v[n] = tr(g), tr(d), tr(nm), tr(nv)
        else:
            outs_g[n] = g
            outs_d[n], outs_m[n], outs_v[n] = _adamw_sharded(w[n], mom[n], var[n], g, name=f"adamw_{n}")

    last = {n for (_, n, _, _) in group_items(pending[0][1])}
    for n in SHARDED:
        if n not in last:
            update(n)
    scatter_finish(outs_d[[n for n in SHARDED if n not in last][-1]])
    for n in SHARDED:
        if n in last:
            update(n)

    gall = _small_exchange(_pack_small(gS), name="small_grads_exchange")
    g, d, nm, nv = _adamw_small(_pack_small({n: w[n] for n in SMALL}), _pack_small({n: mom[n] for n in SMALL}),
                                _pack_small({n: var[n] for n in SMALL}), gall, name="adamw_small")
    for dst, packed in ((outs_g, g), (outs_d, d), (outs_m, nm), (outs_v, nv)):
        dst.update(_unpack_small(packed, {n: w[n] for n in SMALL}))

    return (loss, grad_x[None], *[outs_g[n] for n in ALL_W], *[outs_d[n] for n in ALL_W],
            *[outs_m[n] for n in ALL_W], *[outs_v[n] for n in ALL_W])
```

```python
import functools
import math

import numpy as np
import jax
import jax.numpy as jnp
from jax import lax
from jax.experimental import pallas as pl
from jax.experimental.pallas import tpu as pltpu

F32 = jnp.float32
BF16 = jnp.bfloat16
MESH = pl.DeviceIdType.MESH

D_MODEL = 1024
SEQ = 2048
DEPTH = 4
N_DEV = 8
RET_HEADS, RET_DK, RET_DV = 4, 128, 256
RET_THETA = 10000.0
DIL_HEADS, DIL_HD = 8, 64
DIL_PAIRS = ((128, 1), (512, 4), (2048, 16))
ROPE_THETA = 500000.0
ROPE_DIMS = DIL_HD // 4
GDN_HEADS, GDN_DK, GDN_DV, GDN_CHUNK, GDN_CONV = 8, 128, 128, 64, 4
D_FF = 2816
FFN_CONV = 3
ALPHA = (2.0 * DEPTH) ** 0.25
EPS = 1e-5
RET_QK_W = RET_HEADS * RET_DK
RET_V_W = RET_HEADS * RET_DV
DIL_W = DIL_HEADS * DIL_HD
EV_IN = 2 * RET_QK_W + 2 * RET_V_W + 3 * DIL_W
EV_MIX = RET_V_W + DIL_W
GDN_W = GDN_HEADS * GDN_DK
OD_IN = 4 * GDN_W + 2 * GDN_HEADS
OD_IN_PAD = 4 * GDN_W + 128
ADAM_LR, ADAM_B1, ADAM_B2, ADAM_EPS, ADAM_WD, ADAM_STEP = 0.001, 0.9, 0.999, 1e-08, 0.01, 10

LANES = 128
VMEM_LIMIT = 56 * 1024 * 1024
ATT_BLK = 256
NEG = -1e30


def _cp(**kw):
    return pltpu.CompilerParams(vmem_limit_bytes=VMEM_LIMIT, **kw)


def _tile(n, cap):
    if n <= cap:
        return n
    best = None
    for t in range(LANES, cap + 1, LANES):
        if n % t == 0:
            best = t
    assert best is not None, (n, cap)
    return best


def _mm(a, b, *, ta=False, tb=False, name, out_dtype=F32, dep=None, tm=None, tn=None):
    m = a.shape[1] if ta else a.shape[0]
    k = a.shape[0] if ta else a.shape[1]
    n = b.shape[0] if tb else b.shape[1]
    assert (b.shape[1] if tb else b.shape[0]) == k
    assert a.dtype == BF16 and b.dtype == BF16
    if tn is None:
        tn = n if n <= 1024 else _tile(n, 512)
    if tm is None:
        tm = m if (tn < n and k <= 1024 and m <= 2048) else _tile(m, 512)
    dims = (((0 if ta else 1,), (1 if tb else 0,)), ((), ()))

    def body(a_ref, b_ref, *rest):
        o_ref = rest[-1]
        o_ref[...] = lax.dot_general(a_ref[...], b_ref[...], dims,
                                     preferred_element_type=F32).astype(o_ref.dtype)

    a_spec = pl.BlockSpec((k, tm), lambda i, j: (0, i)) if ta else pl.BlockSpec((tm, k), lambda i, j: (i, 0))
    b_spec = pl.BlockSpec((tn, k), lambda i, j: (j, 0)) if tb else pl.BlockSpec((k, tn), lambda i, j: (0, j))
    extra = [] if dep is None else [dep]
    return pl.pallas_call(
        body, grid=(m // tm, n // tn), in_specs=[a_spec, b_spec] + [pl.BlockSpec(memory_space=pl.ANY)] * len(extra),
        out_specs=pl.BlockSpec((tm, tn), lambda i, j: (i, j)),
        out_shape=jax.ShapeDtypeStruct((m, n), out_dtype), name=name, compiler_params=_cp())(a, b, *extra)


LN_ROWS = 256


def _ln_bwd(z, g, dya, dyb, *, name):
    t, d = z.shape
    two = dyb is not None

    def body(*refs):
        if two:
            z_ref, g_ref, dya_ref, dyb_ref, dz_ref, dzb_ref, dg_ref, db_ref = refs
            dy = dya_ref[...] + ALPHA * dyb_ref[...]
        else:
            z_ref, g_ref, dya_ref, dz_ref, dzb_ref, dg_ref, db_ref = refs
            dy = dya_ref[...]
        zz = z_ref[...]
        mu = jnp.mean(zz, -1, keepdims=True)
        zc = zz - mu
        var = jnp.mean(zc * zc, -1, keepdims=True)
        r = lax.rsqrt(var + EPS)
        xh = zc * r
        dxh = dy * g_ref[...]
        dz = r * (dxh - jnp.mean(dxh, -1, keepdims=True) - xh * jnp.mean(dxh * xh, -1, keepdims=True))
        dz_ref[...] = dz
        dzb_ref[...] = dz.astype(BF16)

        @pl.when(pl.program_id(0) == 0)
        def _():
            dg_ref[...] = jnp.zeros_like(dg_ref)
            db_ref[...] = jnp.zeros_like(db_ref)

        dg_ref[...] += jnp.sum(dy * xh, 0, keepdims=True)
        db_ref[...] += jnp.sum(dy, 0, keepdims=True)

    row = pl.BlockSpec((LN_ROWS, d), lambda i: (i, 0))
    vec = pl.BlockSpec((1, d), lambda i: (0, 0))
    ins = [z, g, dya] + ([dyb] if two else [])
    return pl.pallas_call(
        body, grid=(t // LN_ROWS,), in_specs=[row, vec, row] + ([row] if two else []),
        out_specs=[row, row, vec, vec],
        out_shape=[jax.ShapeDtypeStruct((t, d), F32), jax.ShapeDtypeStruct((t, d), BF16),
                   jax.ShapeDtypeStruct((1, d), F32), jax.ShapeDtypeStruct((1, d), F32)],
        name=name, compiler_params=_cp())(*ins)


def _ln_rows(k):
    return 256 if k > 4096 else 512


def _mm_ln_fwd(a, w, x, g, b, *, name, dep=None):
    t, k = a.shape
    d = w.shape[1]
    tm = _ln_rows(k)

    def body(a_ref, w_ref, x_ref, g_ref, b_ref, *rest):
        z_ref, y_ref, yb_ref = rest[-3:]
        z = ALPHA * x_ref[...] + _nn(a_ref[...], w_ref[...])
        mu = jnp.mean(z, -1, keepdims=True)
        zc = z - mu
        var = jnp.mean(zc * zc, -1, keepdims=True)
        y = zc * lax.rsqrt(var + EPS) * g_ref[...] + b_ref[...]
        z_ref[...] = z
        y_ref[...] = y
        yb_ref[...] = y.astype(BF16)

    row = pl.BlockSpec((tm, d), lambda i: (i, 0))
    vec = pl.BlockSpec((1, d), lambda i: (0, 0))
    extra = [] if dep is None else [dep]
    return pl.pallas_call(
        body, grid=(t // tm,),
        in_specs=[pl.BlockSpec((tm, k), lambda i: (i, 0)), pl.BlockSpec((k, d), lambda i: (0, 0)), row, vec, vec]
        + [pl.BlockSpec(memory_space=pl.ANY)] * len(extra),
        out_specs=[row, row, row],
        out_shape=[jax.ShapeDtypeStruct((t, d), F32), jax.ShapeDtypeStruct((t, d), F32), jax.ShapeDtypeStruct((t, d), BF16)],
        name=name, compiler_params=_cp())(a, w, x, g, b, *extra)


def _mm_ln_bwd(a, w, z, g, dyb, *, name, dep=None):
    t, k = a.shape
    d = w.shape[1]
    tm = _ln_rows(k)

    def body(a_ref, w_ref, z_ref, g_ref, dyb_ref, *rest):
        dz_ref, dzb_ref, dg_ref, db_ref = rest[-4:]
        dy = _nn(a_ref[...], w_ref[...]) + ALPHA * dyb_ref[...]
        zz = z_ref[...]
        mu = jnp.mean(zz, -1, keepdims=True)
        zc = zz - mu
        var = jnp.mean(zc * zc, -1, keepdims=True)
        r = lax.rsqrt(var + EPS)
        xh = zc * r
        dxh = dy * g_ref[...]
        dz = r * (dxh - jnp.mean(dxh, -1, keepdims=True) - xh * jnp.mean(dxh * xh, -1, keepdims=True))
        dz_ref[...] = dz
        dzb_ref[...] = dz.astype(BF16)

        @pl.when(pl.program_id(0) == 0)
        def _():
            dg_ref[...] = jnp.zeros_like(dg_ref)
            db_ref[...] = jnp.zeros_like(db_ref)

        dg_ref[...] += jnp.sum(dy * xh, 0, keepdims=True)
        db_ref[...] += jnp.sum(dy, 0, keepdims=True)

    row = pl.BlockSpec((tm, d), lambda i: (i, 0))
    vec = pl.BlockSpec((1, d), lambda i: (0, 0))
    extra = [] if dep is None else [dep]
    return pl.pallas_call(
        body, grid=(t // tm,),
        in_specs=[pl.BlockSpec((tm, k), lambda i: (i, 0)), pl.BlockSpec((k, d), lambda i: (0, 0)), row, vec, row]
        + [pl.BlockSpec(memory_space=pl.ANY)] * len(extra),
        out_specs=[row, row, vec, vec],
        out_shape=[jax.ShapeDtypeStruct((t, d), F32), jax.ShapeDtypeStruct((t, d), BF16),
                   jax.ShapeDtypeStruct((1, d), F32), jax.ShapeDtypeStruct((1, d), F32)],
        name=name, compiler_params=_cp())(a, w, z, g, dyb, *extra)


def _axpy(a, b, *, name):
    t, d = a.shape

    def body(a_ref, b_ref, o_ref):
        o_ref[...] = a_ref[...] + ALPHA * b_ref[...]

    row = pl.BlockSpec((LN_ROWS, d), lambda i: (i, 0))
    return pl.pallas_call(body, grid=(t // LN_ROWS,), in_specs=[row, row], out_specs=row,
                          out_shape=jax.ShapeDtypeStruct((t, d), F32), name=name, compiler_params=_cp())(a, b)


def _loss_head(y, target, *, name):
    t, d = y.shape

    def body(y_ref, t_ref, dy_ref, l_ref):
        e = y_ref[...] - t_ref[...]
        dy_ref[...] = e * (1.0 / d)

        @pl.when(pl.program_id(0) == 0)
        def _():
            l_ref[...] = jnp.zeros_like(l_ref)

        l_ref[...] += jnp.zeros_like(l_ref) + 0.5 * jnp.sum(jnp.mean(e * e, -1, keepdims=True), 0, keepdims=True)

    row = pl.BlockSpec((LN_ROWS, d), lambda i: (i, 0))
    return pl.pallas_call(
        body, grid=(t // LN_ROWS,), in_specs=[row, row],
        out_specs=[row, pl.BlockSpec((1, LANES), lambda i: (0, 0))],
        out_shape=[jax.ShapeDtypeStruct((t, d), F32), jax.ShapeDtypeStruct((1, LANES), F32)],
        name=name, compiler_params=_cp())(y, target)


def _sig(x):
    return 1.0 / (1.0 + jnp.exp(-x))


def _silu(x):
    return x * _sig(x)


def _dsilu(x):
    s = _sig(x)
    return s * (1.0 + x * (1.0 - s))


def _shift_down(u, k, row):
    if k == 0:
        return u
    return jnp.where(row >= k, pltpu.roll(u, k, 0), 0.0)


def _shift_up(u, k, row):
    if k == 0:
        return u
    t = u.shape[0]
    return jnp.where(row < t - k, pltpu.roll(u, t - k, 0), 0.0)


def _dwconv(u, w_ref, row):
    kk = w_ref.shape[0]
    acc = None
    for j in range(kk):
        term = w_ref[j:j + 1, :] * _shift_down(u, kk - 1 - j, row)
        acc = term if acc is None else acc + term
    return acc


def _dwconv_bwd(u, w_ref, dc, row, dw_ref):
    kk = w_ref.shape[0]
    du = None
    for j in range(kk):
        term = w_ref[j:j + 1, :] * _shift_up(dc, kk - 1 - j, row)
        du = term if du is None else du + term
        dw_ref[j:j + 1, :] = jnp.sum(dc * _shift_down(u, kk - 1 - j, row), 0, keepdims=True)
    return du


CONV_ROWS = 64


def _rows(b):
    return pl.ds(pl.multiple_of(b * CONV_ROWS, CONV_ROWS), CONV_ROWS)


def _shifted_down(ref, b, k, row):
    cur = ref[_rows(b), :]
    if k == 0:
        return cur
    prev = jnp.where(b > 0, ref[_rows(jnp.maximum(b - 1, 0)), :], 0.0)
    return jnp.where(row >= k, pltpu.roll(cur, k, 0), pltpu.roll(prev, k, 0))


def _shifted_up(ref, b, k, row, nblk):
    cur = ref[_rows(b), :]
    if k == 0:
        return cur
    nxt = jnp.where(b < nblk - 1, ref[_rows(jnp.minimum(b + 1, nblk - 1)), :], 0.0)
    return jnp.where(row < CONV_ROWS - k, pltpu.roll(cur, CONV_ROWS - k, 0), pltpu.roll(nxt, CONV_ROWS - k, 0))


def _dwconv_blk(u_ref, w_ref, b, row):
    kk = w_ref.shape[0]
    views = [_shifted_down(u_ref, b, kk - 1 - j, row) for j in range(kk)]
    acc = None
    for j in range(kk):
        term = w_ref[j:j + 1, :] * views[j]
        acc = term if acc is None else acc + term
    return acc, views


def _dwconv_du_blk(dc_ref, w_ref, b, row, nblk):
    kk = w_ref.shape[0]
    du = None
    for j in range(kk):
        term = w_ref[j:j + 1, :] * _shifted_up(dc_ref, b, kk - 1 - j, row, nblk)
        du = term if du is None else du + term
    return du


FFN_TC = 256


def _ffn_up_mid(x, up_t, cw, cb, *, name, dep=None):
    t, d = x.shape
    nb = D_FF // FFN_TC

    def body(x_ref, ugt_ref, uvt_ref, wg_ref, wv_ref, bg_ref, bv_ref, *rest):
        ug_ref, uv_ref, a_ref = rest[-3:]
        xx = x_ref[...]
        row = lax.broadcasted_iota(jnp.int32, (t, FFN_TC), 0)
        ug = _nt(xx, ugt_ref[...])
        ug_ref[...] = ug
        uv = _nt(xx, uvt_ref[...])
        uv_ref[...] = uv
        cg = _dwconv(ug, wg_ref, row) + bg_ref[...]
        cv = _dwconv(uv, wv_ref, row) + bv_ref[...]
        a_ref[...] = (_silu(cg) * cv).astype(BF16)

    col = pl.BlockSpec((t, FFN_TC), lambda j: (0, j))
    wt = lambda off: pl.BlockSpec((FFN_TC, d), lambda j: (j + off, 0))
    wsp = lambda off: pl.BlockSpec((FFN_CONV, FFN_TC), lambda j: (0, j + off))
    bsp = lambda off: pl.BlockSpec((1, FFN_TC), lambda j: (0, j + off))
    extra = [] if dep is None else [dep]
    return pl.pallas_call(
        body, grid=(nb,),
        in_specs=[pl.BlockSpec((t, d), lambda j: (0, 0)), wt(0), wt(nb), wsp(0), wsp(nb), bsp(0), bsp(nb)]
        + [pl.BlockSpec(memory_space=pl.ANY)] * len(extra),
        out_specs=[col, col, col],
        out_shape=[jax.ShapeDtypeStruct((t, D_FF), F32), jax.ShapeDtypeStruct((t, D_FF), F32),
                   jax.ShapeDtypeStruct((t, D_FF), BF16)],
        name=name, compiler_params=_cp())(x, up_t, up_t, cw, cw, cb, cb, *extra)


def _ffn_mid_bwd(ug, uv, cw, cb, dz, down, *, name):
    t, d = dz.shape
    nb = D_FF // FFN_TC

    nblk = t // CONV_ROWS

    def body(ug_ref, uv_ref, wg_ref, wv_ref, bg_ref, bv_ref, dz_ref, dn_ref,
             dug_ref, duv_ref, dwg_ref, dwv_ref, dbg_ref, dbv_ref, da_ref, dcg_s, dcv_s):
        da_ref[...] = _nt(dz_ref[...], dn_ref[...])
        row = lax.broadcasted_iota(jnp.int32, (CONV_ROWS, FFN_TC), 0)
        zero = jnp.zeros((1, FFN_TC), F32)

        def first(b, acc):
            cg, ugs = _dwconv_blk(ug_ref, wg_ref, b, row)
            cv, uvs = _dwconv_blk(uv_ref, wv_ref, b, row)
            cg = cg + bg_ref[...]
            cv = cv + bv_ref[...]
            da_ = da_ref[_rows(b), :]
            dcv = da_ * _silu(cg)
            dcg = da_ * cv * _dsilu(cg)
            dcg_s[_rows(b), :] = dcg
            dcv_s[_rows(b), :] = dcv
            red = [jnp.sum(dcg * s, 0, keepdims=True) for s in ugs] + [jnp.sum(dcg, 0, keepdims=True)]
            red += [jnp.sum(dcv * s, 0, keepdims=True) for s in uvs] + [jnp.sum(dcv, 0, keepdims=True)]
            return tuple(a + r for a, r in zip(acc, red))

        acc = lax.fori_loop(0, nblk, first, (zero,) * (2 * FFN_CONV + 2))
        for j in range(FFN_CONV):
            dwg_ref[j:j + 1, :] = acc[j]
            dwv_ref[j:j + 1, :] = acc[FFN_CONV + 1 + j]
        dbg_ref[...] = acc[FFN_CONV]
        dbv_ref[...] = acc[2 * FFN_CONV + 1]

        def second(b, carry):
            dug_ref[_rows(b), :] = _dwconv_du_blk(dcg_s, wg_ref, b, row, nblk).astype(BF16)
            duv_ref[_rows(b), :] = _dwconv_du_blk(dcv_s, wv_ref, b, row, nblk).astype(BF16)
            return carry

        lax.fori_loop(0, nblk, second, 0)

    col = pl.BlockSpec((t, FFN_TC), lambda j: (0, j))
    wsp = lambda off: pl.BlockSpec((FFN_CONV, FFN_TC), lambda j: (0, j + off))
    bsp = lambda off: pl.BlockSpec((1, FFN_TC), lambda j: (0, j + off))
    outs = pl.pallas_call(
        body, grid=(nb,),
        in_specs=[col, col, wsp(0), wsp(nb), bsp(0), bsp(nb), pl.BlockSpec((t, d), lambda j: (0, 0)),
                  pl.BlockSpec((FFN_TC, d), lambda j: (j, 0))],
        out_specs=[col, col, wsp(0), wsp(0), bsp(0), bsp(0)],
        out_shape=[jax.ShapeDtypeStruct((t, D_FF), BF16), jax.ShapeDtypeStruct((t, D_FF), BF16),
                   jax.ShapeDtypeStruct((FFN_CONV, D_FF), F32), jax.ShapeDtypeStruct((FFN_CONV, D_FF), F32),
                   jax.ShapeDtypeStruct((1, D_FF), F32), jax.ShapeDtypeStruct((1, D_FF), F32)],
        scratch_shapes=[pltpu.VMEM((t, FFN_TC), F32), pltpu.VMEM((t, FFN_TC), F32), pltpu.VMEM((t, FFN_TC), F32)],
        name=name, compiler_params=_cp())(ug, uv, cw, cw, cb, cb, dz, down)
    dug, duv, dwg, dwv, dbg, dbv = outs
    return (jnp.concatenate([dug, duv], 1), jnp.concatenate([dwg, dwv], 1), jnp.concatenate([dbg, dbv], 1))


def _rot_a(x, c2, s2):
    return x * c2 + pltpu.roll(x, RET_DK // 2, 1) * s2


def _rot_a_t(dy, c2, s2):
    return dy * c2 + pltpu.roll(dy * s2, RET_DK // 2, 1)


def _decay_tile(lg, blk_diff):
    r = lax.broadcasted_iota(jnp.int32, (ATT_BLK, ATT_BLK), 0)
    c = lax.broadcasted_iota(jnp.int32, (ATT_BLK, ATT_BLK), 1)
    rel = r - c + blk_diff * ATT_BLK
    return jnp.where(rel >= 0, jnp.exp(jnp.maximum(rel, 0).astype(F32) * lg), 0.0)


def _nt(a, b):
    return lax.dot_general(a, b, (((1,), (1,)), ((), ())), preferred_element_type=F32)


def _nn(a, b):
    return lax.dot_general(a, b, (((1,), (0,)), ((), ())), preferred_element_type=F32)


def _tn(a, b):
    return lax.dot_general(a, b, (((0,), (0,)), ((), ())), preferred_element_type=F32)


def _ret_specs(t):
    q = pl.BlockSpec((t, RET_DK), lambda h: (0, h))
    k = pl.BlockSpec((t, RET_DK), lambda h: (0, RET_HEADS + h))
    v = pl.BlockSpec((t, RET_DV), lambda h: (0, RET_HEADS + h))
    g = pl.BlockSpec((t, RET_DV), lambda h: (0, 2 * RET_HEADS + h))
    tab = pl.BlockSpec((t, RET_DK), lambda h: (0, 0))
    lg = pl.BlockSpec((1, 1, LANES), lambda h: (h, 0, 0))
    return q, k, v, g, tab, lg


def _ret_fwd(h, c2, s2, lgt, *, name):
    t = h.shape[0]
    nblk = t // ATT_BLK
    scale = RET_DK ** -0.5

    def body(q_ref, k_ref, v_ref, g_ref, c_ref, s_ref, lg_ref, o_ref, ya_ref, qs, ks, vs):
        c2_, s2_ = c_ref[...], s_ref[...]
        qs[...] = _rot_a(q_ref[...], c2_, s2_).astype(BF16)
        ks[...] = (_rot_a(k_ref[...], c2_, s2_) * scale).astype(BF16)
        vs[...] = v_ref[...].astype(BF16)
        lg = lg_ref[0, :, 0:1]
        for i in range(nblk):
            qi = qs[pl.ds(i * ATT_BLK, ATT_BLK), :]
            acc = jnp.zeros((ATT_BLK, RET_DV), F32)
            for j in range(i + 1):
                sl = pl.ds(j * ATT_BLK, ATT_BLK)
                s = _nt(qi, ks[sl, :]) * _decay_tile(lg, i - j)
                acc = acc + _nn(s.astype(BF16), vs[sl, :])
            rows = pl.ds(i * ATT_BLK, ATT_BLK)
            o_ref[rows, :] = acc
            r = lax.rsqrt(jnp.mean(acc * acc, -1, keepdims=True) + EPS)
            ya_ref[rows, :] = (acc * r * _silu(g_ref[rows, :])).astype(BF16)

    q, k, v, g, tab, lg = _ret_specs(t)
    out = pl.BlockSpec((t, RET_DV), lambda hh: (0, hh))
    return pl.pallas_call(
        body, grid=(RET_HEADS,), in_specs=[q, k, v, g, tab, tab, lg], out_specs=[out, out],
        out_shape=[jax.ShapeDtypeStruct((t, RET_V_W), F32), jax.ShapeDtypeStruct((t, RET_V_W), BF16)],
        scratch_shapes=[pltpu.VMEM((t, RET_DK), BF16), pltpu.VMEM((t, RET_DK), BF16), pltpu.VMEM((t, RET_DV), BF16)],
        name=name, compiler_params=_cp())(h, h, h, h, c2, s2, lgt)


def _ret_bwd(h, c2, s2, lgt, o, dy, *, name):
    t = h.shape[0]
    nblk = t // ATT_BLK
    scale = RET_DK ** -0.5

    def body(q_ref, k_ref, v_ref, g_ref, c_ref, s_ref, lg_ref, o_ref, dy_ref,
             dq_ref, dk_ref, dv_ref, dg_ref, qs, ks, vs, dos, dka, dva):
        c2_, s2_ = c_ref[...], s_ref[...]
        qs[...] = _rot_a(q_ref[...], c2_, s2_).astype(BF16)
        ks[...] = (_rot_a(k_ref[...], c2_, s2_) * scale).astype(BF16)
        vs[...] = v_ref[...].astype(BF16)
        lg = lg_ref[0, :, 0:1]
        oo = o_ref[...]
        gg = g_ref[...]
        dya = dy_ref[...]
        r = lax.rsqrt(jnp.mean(oo * oo, -1, keepdims=True) + EPS)
        rn = oo * r
        dg_ref[...] = (dya * rn * _dsilu(gg)).astype(BF16)
        drn = dya * _silu(gg)
        dos[...] = (r * (drn - rn * jnp.mean(drn * rn, -1, keepdims=True))).astype(BF16)
        dka[...] = jnp.zeros_like(dka)
        dva[...] = jnp.zeros_like(dva)
        for i in range(nblk):
            rows = pl.ds(i * ATT_BLK, ATT_BLK)
            qi = qs[rows, :]
            doi = dos[rows, :]
            dqa = jnp.zeros((ATT_BLK, RET_DK), F32)
            for j in range(i + 1):
                sl = pl.ds(j * ATT_BLK, ATT_BLK)
                dt_ = _decay_tile(lg, i - j)
                kj = ks[sl, :]
                s = (_nt(qi, kj) * dt_).astype(BF16)
                ds = (_nt(doi, vs[sl, :]) * dt_).astype(BF16)
                dqa = dqa + _nn(ds, kj)
                dka[sl, :] += _tn(ds, qi)
                dva[sl, :] += _tn(s, doi)
            dq_ref[rows, :] = _rot_a_t(dqa, c_ref[rows, :], s_ref[rows, :]).astype(BF16)
        dk_ref[...] = (_rot_a_t(dka[...], c2_, s2_) * scale).astype(BF16)
        dv_ref[...] = dva[...].astype(BF16)

    q, k, v, g, tab, lg = _ret_specs(t)
    blk_v = pl.BlockSpec((t, RET_DV), lambda hh: (0, hh))
    blk_k = pl.BlockSpec((t, RET_DK), lambda hh: (0, hh))
    return pl.pallas_call(
        body, grid=(RET_HEADS,), in_specs=[q, k, v, g, tab, tab, lg, blk_v, blk_v],
        out_specs=[blk_k, blk_k, blk_v, blk_v],
        out_shape=[jax.ShapeDtypeStruct((t, RET_QK_W), BF16), jax.ShapeDtypeStruct((t, RET_QK_W), BF16),
                   jax.ShapeDtypeStruct((t, RET_V_W), BF16), jax.ShapeDtypeStruct((t, RET_V_W), BF16)],
        scratch_shapes=[pltpu.VMEM((t, RET_DK), BF16), pltpu.VMEM((t, RET_DK), BF16), pltpu.VMEM((t, RET_DV), BF16),
                        pltpu.VMEM((t, RET_DV), BF16), pltpu.VMEM((t, RET_DK), F32), pltpu.VMEM((t, RET_DV), F32)],
        name=name, compiler_params=_cp())(h, h, h, h, c2, s2, lgt, o, dy)


def _rot_b(x, cb, shi, slo):
    return x * cb + pltpu.roll(x, ROPE_DIMS // 2, 1) * shi + pltpu.roll(x, LANES - ROPE_DIMS // 2, 1) * slo


def _rot_b_t(dy, cb, shi, slo):
    return dy * cb + pltpu.roll(dy * shi, LANES - ROPE_DIMS // 2, 1) + pltpu.roll(dy * slo, ROPE_DIMS // 2, 1)


def _dil_specs(t):
    base = (2 * RET_QK_W + 2 * RET_V_W) // LANES
    npair = DIL_W // LANES
    q = pl.BlockSpec((t, LANES), lambda p: (0, base + p))
    k = pl.BlockSpec((t, LANES), lambda p: (0, base + npair + p))
    v = pl.BlockSpec((t, LANES), lambda p: (0, base + 2 * npair + p))
    tab = pl.BlockSpec((t, LANES), lambda p: (0, 0))
    strip = pl.BlockSpec((ATT_BLK, t), lambda p: (0, 0))
    pair = pl.BlockSpec((t, LANES), lambda p: (0, p))
    return q, k, v, tab, strip, pair


def _dil_fwd(h, cb, shi, slo, strip, *, name):
    t = h.shape[0]
    nblk = t // ATT_BLK
    scale = DIL_HD ** -0.5

    def body(q_ref, k_ref, v_ref, cb_ref, shi_ref, slo_ref, st_ref, o_ref, yb_ref, lse_ref, qs, ks, vs):
        cb_, shi_, slo_ = cb_ref[...], shi_ref[...], slo_ref[...]
        lane = lax.broadcasted_iota(jnp.int32, (t, LANES), 1)
        qr = _rot_b(q_ref[...], cb_, shi_, slo_) * scale
        qs[0] = jnp.where(lane < DIL_HD, qr, 0.0).astype(BF16)
        qs[1] = jnp.where(lane >= DIL_HD, qr, 0.0).astype(BF16)
        ks[...] = _rot_b(k_ref[...], cb_, shi_, slo_).astype(BF16)
        vs[...] = v_ref[...].astype(BF16)
        lane_b = lax.broadcasted_iota(jnp.int32, (ATT_BLK, LANES), 1)
        for i in range(nblk):
            w = (i + 1) * ATT_BLK
            rows = pl.ds(i * ATT_BLK, ATT_BLK)
            logc = st_ref[:, t - w:t]
            outs, lses = [], []
            for hd in range(2):
                s = _nt(qs[hd, rows, :], ks[0:w, :]) + logc
                m = jnp.max(s, -1, keepdims=True)
                p = jnp.exp(s - m)
                l = jnp.sum(p, -1, keepdims=True)
                outs.append(_nn(p.astype(BF16), vs[0:w, :]) / l)
                lses.append(m + jnp.log(l))
            o = jnp.where(lane_b < DIL_HD, outs[0], outs[1])
            o_ref[rows, :] = o
            yb_ref[rows, :] = o.astype(BF16)
            lse_ref[rows, :] = jnp.where(lane_b < DIL_HD, lses[0], lses[1])

    q, k, v, tab, strip_spec, pair = _dil_specs(t)
    return pl.pallas_call(
        body, grid=(DIL_W // LANES,), in_specs=[q, k, v, tab, tab, tab, strip_spec], out_specs=[pair, pair, pair],
        out_shape=[jax.ShapeDtypeStruct((t, DIL_W), F32), jax.ShapeDtypeStruct((t, DIL_W), BF16),
                   jax.ShapeDtypeStruct((t, DIL_W), F32)],
        scratch_shapes=[pltpu.VMEM((2, t, LANES), BF16), pltpu.VMEM((t, LANES), BF16), pltpu.VMEM((t, LANES), BF16)],
        name=name, compiler_params=_cp())(h, h, h, cb, shi, slo, strip)


def _dil_bwd(h, cb, shi, slo, strip, o, lse, dy, *, name):
    t = h.shape[0]
    nblk = t // ATT_BLK
    scale = DIL_HD ** -0.5

    def body(q_ref, k_ref, v_ref, cb_ref, shi_ref, slo_ref, st_ref, o_ref, lse_ref, dy_ref,
             dq_ref, dk_ref, dv_ref, qs, ks, vs, dos, dls, dka, dva):
        cb_, shi_, slo_ = cb_ref[...], shi_ref[...], slo_ref[...]
        lane = lax.broadcasted_iota(jnp.int32, (t, LANES), 1)
        qr = _rot_b(q_ref[...], cb_, shi_, slo_) * scale
        qs[0] = jnp.where(lane < DIL_HD, qr, 0.0).astype(BF16)
        qs[1] = jnp.where(lane >= DIL_HD, qr, 0.0).astype(BF16)
        ks[...] = _rot_b(k_ref[...], cb_, shi_, slo_).astype(BF16)
        vs[...] = v_ref[...].astype(BF16)
        do = dy_ref[...]
        prod = do * o_ref[...]
        d0 = jnp.sum(jnp.where(lane < DIL_HD, prod, 0.0), -1, keepdims=True)
        d1 = jnp.sum(jnp.where(lane >= DIL_HD, prod, 0.0), -1, keepdims=True)
        dls[...] = jnp.where(lane < DIL_HD, d0, d1)
        dos[0] = jnp.where(lane < DIL_HD, do, 0.0).astype(BF16)
        dos[1] = jnp.where(lane >= DIL_HD, do, 0.0).astype(BF16)
        dka[...] = jnp.zeros_like(dka)
        dva[...] = jnp.zeros_like(dva)
        lane_b = lax.broadcasted_iota(jnp.int32, (ATT_BLK, LANES), 1)
        for i in range(nblk):
            w = (i + 1) * ATT_BLK
            rows = pl.ds(i * ATT_BLK, ATT_BLK)
            logc = st_ref[:, t - w:t]
            dqs = []
            for hd in range(2):
                col = hd * DIL_HD
                qh = qs[hd, rows, :]
                doh = dos[hd, rows, :]
                lse_h = lse_ref[rows, col:col + 1]
                dl_h = dls[rows, col:col + 1]
                p = jnp.exp(_nt(qh, ks[0:w, :]) + logc - lse_h)
                dp = _nt(doh, vs[0:w, :])
                ds = (p * (dp - dl_h)).astype(BF16)
                dqs.append(_nn(ds, ks[0:w, :]))
                dka[0:w, :] += _tn(ds, qh)
                dva[0:w, :] += _tn(p.astype(BF16), doh)
            dq = jnp.where(lane_b < DIL_HD, dqs[0], dqs[1]) * scale
            dq_ref[rows, :] = _rot_b_t(dq, cb_ref[rows, :], shi_ref[rows, :], slo_ref[rows, :]).astype(BF16)
        dk_ref[...] = _rot_b_t(dka[...], cb_, shi_, slo_).astype(BF16)
        dv_ref[...] = dva[...].astype(BF16)

    q, k, v, tab, strip_spec, pair = _dil_specs(t)
    dy_spec = pl.BlockSpec((t, LANES), lambda p: (0, RET_V_W // LANES + p))
    return pl.pallas_call(
        body, grid=(DIL_W // LANES,), in_specs=[q, k, v, tab, tab, tab, strip_spec, pair, pair, dy_spec],
        out_specs=[pair, pair, pair],
        out_shape=[jax.ShapeDtypeStruct((t, DIL_W), BF16)] * 3,
        scratch_shapes=[pltpu.VMEM((2, t, LANES), BF16), pltpu.VMEM((t, LANES), BF16), pltpu.VMEM((t, LANES), BF16),
                        pltpu.VMEM((2, t, LANES), BF16), pltpu.VMEM((t, LANES), F32),
                        pltpu.VMEM((t, LANES), F32), pltpu.VMEM((t, LANES), F32)],
        name=name, compiler_params=_cp())(h, h, h, cb, shi, slo, strip, o, lse, dy)


def _gdn_prep_fwd(h, cw, *, name):
    t = h.shape[0]
    qscale = GDN_DK ** -0.5

    def body(hq_ref, hk_ref, hv_ref, wq_ref, wk_ref, wv_ref, q_ref, k_ref, v_ref):
        row = lax.broadcasted_iota(jnp.int32, (t, GDN_DK), 0)
        sq = _silu(_dwconv(hq_ref[...], wq_ref, row))
        sk = _silu(_dwconv(hk_ref[...], wk_ref, row))
        q_ref[0] = sq * lax.rsqrt(jnp.sum(sq * sq, -1, keepdims=True) + 1e-6) * qscale
        k_ref[0] = sk * lax.rsqrt(jnp.sum(sk * sk, -1, keepdims=True) + 1e-6)
        v_ref[0] = _silu(_dwconv(hv_ref[...], wv_ref, row))

    hs = lambda off: pl.BlockSpec((t, GDN_DK), lambda i: (0, i + off))
    ws = lambda off: pl.BlockSpec((GDN_CONV, GDN_DK), lambda i: (0, i + off))
    out = pl.BlockSpec((1, t, GDN_DK), lambda i: (i, 0, 0))
    return pl.pallas_call(
        body, grid=(GDN_HEADS,), in_specs=[hs(0), hs(8), hs(16), ws(0), ws(8), ws(16)], out_specs=[out, out, out],
        out_shape=[jax.ShapeDtypeStruct((GDN_HEADS, t, GDN_DK), F32)] * 3,
        name=name, compiler_params=_cp())(h, h, h, cw, cw, cw)


def _gdn_prep_bwd(h, cw, dq, dk, dv, *, name):
    t = h.shape[0]
    qscale = GDN_DK ** -0.5

    def body(hq_ref, hk_ref, hv_ref, wq_ref, wk_ref, wv_ref, dq_ref, dk_ref, dv_ref,
             dhq_ref, dhk_ref, dhv_ref, dwq_ref, dwk_ref, dwv_ref):
        row = lax.broadcasted_iota(jnp.int32, (t, GDN_DK), 0)

        def one(h_ref, w_ref, d_ref, dh_ref, dw_ref, norm, sc):
            u = h_ref[...]
            c = _dwconv(u, w_ref, row)
            d = d_ref[0]
            if norm:
                s = _silu(c)
                r = lax.rsqrt(jnp.sum(s * s, -1, keepdims=True) + 1e-6)
                n = s * r
                d = d * sc
                d = r * (d - n * jnp.sum(d * n, -1, keepdims=True))
            dc = d * _dsilu(c)
            dh_ref[...] = _dwconv_bwd(u, w_ref, dc, row, dw_ref).astype(BF16)

        one(hq_ref, wq_ref, dq_ref, dhq_ref, dwq_ref, True, qscale)
        one(hk_ref, wk_ref, dk_ref, dhk_ref, dwk_ref, True, 1.0)
        one(hv_ref, wv_ref, dv_ref, dhv_ref, dwv_ref, False, 1.0)

    hs = lambda off: pl.BlockSpec((t, GDN_DK), lambda i: (0, i + off))
    ws = lambda off: pl.BlockSpec((GDN_CONV, GDN_DK), lambda i: (0, i + off))
    hd = pl.BlockSpec((1, t, GDN_DK), lambda i: (i, 0, 0))
    return pl.pallas_call(
        body, grid=(GDN_HEADS,), in_specs=[hs(0), hs(8), hs(16), ws(0), ws(8), ws(16), hd, hd, hd],
        out_specs=[hs(0), hs(0), hs(0), ws(0), ws(0), ws(0)],
        out_shape=[jax.ShapeDtypeStruct((t, GDN_W), BF16)] * 3 + [jax.ShapeDtypeStruct((GDN_CONV, GDN_W), F32)] * 3,
        name=name, compiler_params=_cp())(h, h, h, cw, cw, cw, dq, dk, dv)


def _make_mm2(wide):
    def raw(a, b, dims):
        if wide:
            return lax.dot_general(a, b, (dims, ((), ())), precision=lax.Precision.HIGHEST, preferred_element_type=F32)
        return lax.dot_general(a.astype(BF16), b.astype(BF16), (dims, ((), ())), preferred_element_type=F32)

    @jax.custom_vjp
    def nn(a, b):
        return raw(a, b, ((1,), (0,)))

    @jax.custom_vjp
    def nt(a, b):
        return raw(a, b, ((1,), (1,)))

    @jax.custom_vjp
    def tn(a, b):
        return raw(a, b, ((0,), (0,)))

    nn.defvjp(lambda a, b: (nn(a, b), (a, b)), lambda r, g: (nt(g, r[1]), tn(r[0], g)))
    nt.defvjp(lambda a, b: (nt(a, b), (a, b)), lambda r, g: (nn(g, r[1]), tn(g, r[0])))
    tn.defvjp(lambda a, b: (tn(a, b), (a, b)), lambda r, g: (nt(r[1], g), nn(r[0], g)))
    return nn, nt, tn


_NN, _NT, _TN = _make_mm2(False)
_NNW, _NTW, _TNW = _make_mm2(True)


def _square_masks(c):
    ri = lax.broadcasted_iota(jnp.int32, (c, c), 0)
    ci = lax.broadcasted_iota(jnp.int32, (c, c), 1)
    return ri >= ci, ri > ci, ri == ci


def _cumsum_rows(m):
    tri, _, _ = _square_masks(m.shape[0])
    return _NNW(tri.astype(F32), m)


def _transpose_sq(m):
    _, _, eye = _square_masks(m.shape[0])
    return _NTW(eye.astype(F32), m)


@jax.custom_vjp
def _inv_unit_lower(l):
    c = l.shape[0]
    _, _, eye = _square_masks(c)
    p = -l
    t = eye.astype(F32) + p
    for _ in range(int(math.log2(c)) - 1):
        p = _NNW(p, p)
        t = t + _NNW(t, p)
    return t


def _inv_fwd(l):
    t = _inv_unit_lower(l)
    return t, t


def _inv_bwd(t, dt):
    return (-_NTW(_TNW(t, dt), t),)


_inv_unit_lower.defvjp(_inv_fwd, _inv_bwd)


def _softplus(x):
    return jnp.maximum(x, 0.0) + jnp.log1p(jnp.exp(-jnp.abs(x)))


def _gdn_chunk(q, k, v, braw, araw, alog, dtb, state):
    c = q.shape[0]
    dv = v.shape[1]
    tri, strict, _ = _square_masks(c)
    beta = _sig(braw)
    g = -jnp.exp(alog) * _softplus(araw + dtb)
    gcm = _cumsum_rows(g * jnp.ones((c, c), F32))
    gct = _transpose_sq(gcm)
    decay = jnp.where(tri, jnp.exp(jnp.where(tri, gcm - gct, 0.0)), 0.0)
    gc = jnp.sum(gcm, 1, keepdims=True) * (1.0 / c)
    glast = jnp.sum(g, 0, keepdims=True)
    egc = jnp.exp(gc)
    kb = k * beta
    tm = _inv_unit_lower(jnp.where(strict, _NT(kb, k) * decay, 0.0))
    sol = _NNW(tm, jnp.concatenate([v * beta, kb * egc], 1))
    u, w = sol[:, :dv], sol[:, dv:]
    attn = jnp.where(tri, _NT(q, k) * decay, 0.0)
    k_dec = k * jnp.exp(glast - gc)
    q_dec = q * egc
    v_new = u - _NN(w, state)
    o = _NN(q_dec, state) + _NN(attn, v_new)
    new_state = state * jnp.exp(glast) + _TN(k_dec, v_new)
    return o, new_state


def _gdn_specs(t, rev):
    nch = t // GDN_CHUNK
    cm = (lambda n: nch - 1 - n) if rev else (lambda n: n)
    tok = pl.BlockSpec((GDN_HEADS, GDN_CHUNK, GDN_DK), lambda n: (0, cm(n), 0))
    par = pl.BlockSpec((GDN_HEADS, 1, LANES), lambda n: (0, 0, 0))
    st = pl.BlockSpec((GDN_HEADS, 1, GDN_DK, GDN_DV), lambda n: (0, cm(n), 0, 0))
    return tok, par, st


def _gdn_core_fwd(q, k, v, bb, ab, alog, dtb, *, name):
    t = q.shape[1]
    nch = t // GDN_CHUNK

    def body(q_ref, k_ref, v_ref, bb_ref, ab_ref, al_ref, dt_ref, o_ref, st_ref, state):
        @pl.when(pl.program_id(0) == 0)
        def _():
            state[...] = jnp.zeros_like(state)

        s0 = state[...]
        st_ref[:, 0] = s0
        o, s1 = jax.vmap(_gdn_chunk)(q_ref[...], k_ref[...], v_ref[...], bb_ref[:, :, 0:1], ab_ref[:, :, 0:1],
                                     al_ref[:, :, 0:1], dt_ref[:, :, 0:1], s0)
        o_ref[...] = o
        state[...] = s1

    tok, par, st = _gdn_specs(t, False)
    return pl.pallas_call(
        body, grid=(nch,), in_specs=[tok, tok, tok, tok, tok, par, par], out_specs=[tok, st],
        out_shape=[jax.ShapeDtypeStruct((GDN_HEADS, t, GDN_DV), F32),
                   jax.ShapeDtypeStruct((GDN_HEADS, nch, GDN_DK, GDN_DV), F32)],
        scratch_shapes=[pltpu.VMEM((GDN_HEADS, GDN_DK, GDN_DV), F32)],
        name=name, compiler_params=_cp())(q, k, v, bb, ab, alog, dtb)


def _gdn_core_bwd(q, k, v, bb, ab, alog, dtb, states, do, *, name):
    t = q.shape[1]
    nch = t // GDN_CHUNK

    def body(q_ref, k_ref, v_ref, bb_ref, ab_ref, al_ref, dt_ref, st_ref, do_ref,
             dq_ref, dk_ref, dv_ref, dbb_ref, dab_ref, dal_ref, ddt_ref, dstate):
        @pl.when(pl.program_id(0) == 0)
        def _():
            dstate[...] = jnp.zeros_like(dstate)
            dal_ref[...] = jnp.zeros_like(dal_ref)
            ddt_ref[...] = jnp.zeros_like(ddt_ref)

        args = (q_ref[...], k_ref[...], v_ref[...], bb_ref[:, :, 0:1], ab_ref[:, :, 0:1],
                al_ref[:, :, 0:1], dt_ref[:, :, 0:1], st_ref[:, 0])
        _, pull = jax.vjp(jax.vmap(_gdn_chunk), *args)
        dq, dk, dv, dbr, dar, dal, ddt, ds = pull((do_ref[...], dstate[...]))
        dq_ref[...] = dq
        dk_ref[...] = dk
        dv_ref[...] = dv
        dbb_ref[...] = dbr + jnp.zeros((GDN_HEADS, GDN_CHUNK, LANES), F32)
        dab_ref[...] = dar + jnp.zeros((GDN_HEADS, GDN_CHUNK, LANES), F32)
        dal_ref[...] += dal + jnp.zeros((GDN_HEADS, 1, LANES), F32)
        ddt_ref[...] += ddt + jnp.zeros((GDN_HEADS, 1, LANES), F32)
        dstate[...] = ds

    tok, par, st = _gdn_specs(t, True)
    tokshape = jax.ShapeDtypeStruct((GDN_HEADS, t, GDN_DK), F32)
    parshape = jax.ShapeDtypeStruct((GDN_HEADS, 1, LANES), F32)
    return pl.pallas_call(
        body, grid=(nch,), in_specs=[tok, tok, tok, tok, tok, par, par, st, tok],
        out_specs=[tok, tok, tok, tok, tok, par, par],
        out_shape=[tokshape] * 5 + [parshape] * 2,
        scratch_shapes=[pltpu.VMEM((GDN_HEADS, GDN_DK, GDN_DV), F32)],
        name=name, compiler_params=_cp())(q, k, v, bb, ab, alog, dtb, states, do)


GDN_ROWS = 512


def _gdn_post_fwd(o, h, nw, *, name):
    t = o.shape[1]

    def body(o_ref, g_ref, nw_ref, y_ref):
        oo = o_ref[0]
        r = lax.rsqrt(jnp.mean(oo * oo, -1, keepdims=True) + EPS)
        y_ref[...] = (oo * r * nw_ref[...] * _silu(g_ref[...])).astype(BF16)

    return pl.pallas_call(
        body, grid=(GDN_HEADS, t // GDN_ROWS),
        in_specs=[pl.BlockSpec((1, GDN_ROWS, GDN_DV), lambda hh, i: (hh, i, 0)),
                  pl.BlockSpec((GDN_ROWS, GDN_DV), lambda hh, i: (i, 3 * GDN_HEADS + hh)),
                  pl.BlockSpec((1, GDN_DV), lambda hh, i: (0, 0))],
        out_specs=pl.BlockSpec((GDN_ROWS, GDN_DV), lambda hh, i: (i, hh)),
        out_shape=jax.ShapeDtypeStruct((t, GDN_W), BF16), name=name, compiler_params=_cp())(o, h, nw)


def _gdn_post_bwd(o, h, nw, dy, *, name):
    t = o.shape[1]

    def body(o_ref, g_ref, nw_ref, dy_ref, do_ref, dg_ref, dnw_ref):
        oo, gg, nw_, dy_ = o_ref[0], g_ref[...], nw_ref[...], dy_ref[...]
        r = lax.rsqrt(jnp.mean(oo * oo, -1, keepdims=True) + EPS)
        n = oo * r
        sg = _silu(gg)
        dg_ref[...] = (dy_ * n * nw_ * _dsilu(gg)).astype(BF16)
        dn = dy_ * sg * nw_
        do_ref[0] = r * (dn - n * jnp.mean(dn * n, -1, keepdims=True))

        @pl.when((pl.program_id(0) == 0) & (pl.program_id(1) == 0))
        def _():
            dnw_ref[...] = jnp.zeros_like(dnw_ref)

        dnw_ref[...] += jnp.sum(dy_ * sg * n, 0, keepdims=True)

    return pl.pallas_call(
        body, grid=(GDN_HEADS, t // GDN_ROWS),
        in_specs=[pl.BlockSpec((1, GDN_ROWS, GDN_DV), lambda hh, i: (hh, i, 0)),
                  pl.BlockSpec((GDN_ROWS, GDN_DV), lambda hh, i: (i, 3 * GDN_HEADS + hh)),
                  pl.BlockSpec((1, GDN_DV), lambda hh, i: (0, 0)),
                  pl.BlockSpec((GDN_ROWS, GDN_DV), lambda hh, i: (i, hh))],
        out_specs=[pl.BlockSpec((1, GDN_ROWS, GDN_DV), lambda hh, i: (hh, i, 0)),
                   pl.BlockSpec((GDN_ROWS, GDN_DV), lambda hh, i: (i, hh)),
                   pl.BlockSpec((1, GDN_DV), lambda hh, i: (0, 0))],
        out_shape=[jax.ShapeDtypeStruct((GDN_HEADS, t, GDN_DV), F32), jax.ShapeDtypeStruct((t, GDN_W), BF16),
                   jax.ShapeDtypeStruct((1, GDN_DV), F32)],
        name=name, compiler_params=_cp())(o, h, nw, dy)


def _tables(positions):
    pos = positions.astype(F32)[:, None]
    half = RET_DK // 2
    inv = jnp.power(RET_THETA, -jnp.arange(half, dtype=F32) * 2.0 / RET_DK)
    ang = pos * inv
    cos, sin = jnp.cos(ang), jnp.sin(ang)
    c2a = jnp.concatenate([cos, cos], 1)
    s2a = jnp.concatenate([-sin, sin], 1)
    hb = ROPE_DIMS // 2
    invb = jnp.power(ROPE_THETA, -jnp.arange(hb, dtype=F32) * 2.0 / ROPE_DIMS)
    angb = pos * invb
    cosb, sinb = jnp.cos(angb), jnp.sin(angb)
    t = pos.shape[0]
    ones = jnp.ones((t, DIL_HD - ROPE_DIMS), F32)
    zeros = jnp.zeros((t, DIL_HD - ROPE_DIMS), F32)
    z8 = jnp.zeros((t, hb), F32)
    cb = jnp.concatenate([cosb, cosb, ones] * 2, 1)
    shi = jnp.concatenate([z8, sinb, zeros] * 2, 1)
    slo = jnp.concatenate([-sinb, z8, zeros] * 2, 1)
    lg = jnp.log1p(-jnp.power(2.0, -5.0 - jnp.arange(RET_HEADS, dtype=F32)))
    lgt = jnp.broadcast_to(lg[:, None, None], (RET_HEADS, 1, LANES))
    delta = jnp.arange(ATT_BLK, dtype=jnp.int32)[:, None] + (SEQ - ATT_BLK) - jnp.arange(SEQ, dtype=jnp.int32)[None, :]
    cnt = jnp.zeros(delta.shape, F32)
    for (w, d) in DIL_PAIRS:
        cnt = cnt + ((delta >= 0) & (delta <= w) & (delta % d == 0)).astype(F32)
    strip = jnp.where(cnt > 0, jnp.log(jnp.maximum(cnt, 1.0)), NEG)
    return c2a, s2a, cb, shi, slo, lgt, strip


def _local_step(x, positions, target, get_w, mid, put_g, small):
    c2a, s2a, cb, shi, slo, lgt, strip = _tables(positions)
    t = x.shape[0]
    saved = []
    xf = x
    xb = x.astype(BF16)
    for layer in range(DEPTH):
        j = layer // 2
        L = f"L{layer}_"
        W, dep = get_w(layer, "mixer", xb)
        rec = {"x": xf, "xb": xb}
        if layer % 2 == 0:
            h = _mm(xb, W["in_t"], tb=True, name=L + "ev_in", dep=dep)
            ro, ya = _ret_fwd(h, c2a, s2a, lgt, name=L + "ret_fwd")
            do_, yb, lse = _dil_fwd(h, cb, shi, slo, strip, name=L + "dil_fwd")
            y = jnp.concatenate([ya, yb], 1)
            rec.update(h=h, ro=ro, dil_o=do_, lse=lse, y=y)
        else:
            h = _mm(xb, W["in_t"], tb=True, name=L + "od_in", dep=dep)
            cw = W["conv"]
            q, k, v = _gdn_prep_fwd(h, cw, name=L + "gdn_prep")
            hs = h[:, 4 * GDN_W:4 * GDN_W + 2 * GDN_HEADS]
            bb = jnp.broadcast_to(hs[:, :GDN_HEADS].T[:, :, None], (GDN_HEADS, t, LANES))
            ab = jnp.broadcast_to(hs[:, GDN_HEADS:].T[:, :, None], (GDN_HEADS, t, LANES))
            alog = jnp.broadcast_to(small["od_a_log"][j][:, None, None], (GDN_HEADS, 1, LANES))
            dtb = jnp.broadcast_to(small["od_dt_bias"][j][:, None, None], (GDN_HEADS, 1, LANES))
            o, states = _gdn_core_fwd(q, k, v, bb, ab, alog, dtb, name=L + "gdn_fwd")
            nw = small["od_norm_w"][j][None, :]
            y = _gdn_post_fwd(o, h, nw, name=L + "gdn_post")
            rec.update(h=h, q=q, k=k, v=v, bb=bb, ab=ab, alog=alog, dtb=dtb, states=states, o=o, y=y, nw=nw, cw=cw)
        z1, x1, x1b = _mm_ln_fwd(y, W["out"], xf, small["ln1_g"][layer][None], small["ln1_b"][layer][None],
                                 name=L + "out_ln1", dep=mid(layer, "mixer", y))
        rec["Wm"] = W
        W, dep = get_w(layer, "ffn", x1b)
        rec["Wf"] = W
        fcw = W["fconv"]
        fcb = small["ffn_conv_b"][layer][None]
        ug, uv, a = _ffn_up_mid(x1b, W["up_t"], fcw, fcb, name=L + "ffn_up_mid", dep=dep)
        z2, x2, x2b = _mm_ln_fwd(a, W["down"], x1, small["ln2_g"][layer][None], small["ln2_b"][layer][None],
                                 name=L + "down_ln2", dep=mid(layer, "ffn", a))
        rec.update(z1=z1, x1b=x1b, ug=ug, uv=uv, a=a, z2=z2, fcw=fcw, fcb=fcb)
        saved.append(rec)
        xf, xb = x2, x2b

    dy, lossv = _loss_head(xf, target, name="loss_head")
    loss = lossv[0, 0]

    gS = {n: [None] * small[n].shape[0] for n in small}
    below = None
    for layer in reversed(range(DEPTH)):
        j = layer // 2
        L = f"L{layer}_"
        rec = saved[layer]
        Wm, Wf = rec["Wm"], rec["Wf"]
        g = {}
        if below is None:
            dz2, dz2b, dg2, db2 = _ln_bwd(rec["z2"], small["ln2_g"][layer][None], dy, None, name=L + "ln2_bwd")
        else:
            dz2, dz2b, dg2, db2 = _mm_ln_bwd(below[0], below[1], rec["z2"], small["ln2_g"][layer][None], below[2],
                                             name=L + "ln2_bwd", dep=below[3])
        gS["ln2_g"][layer], gS["ln2_b"][layer] = dg2[0], db2[0]
        g["down"] = _mm(rec["a"], dz2b, ta=True, name=L + "ffn_down_dw", out_dtype=BF16)
        du, dcw, dcb = _ffn_mid_bwd(rec["ug"], rec["uv"], rec["fcw"], rec["fcb"], dz2b, Wf["down"], name=L + "ffn_mid_bwd")
        g["fconv"] = dcw.astype(BF16)
        gS["ffn_conv_b"][layer] = dcb[0]
        g["up_t"] = _mm(du, rec["x1b"], ta=True, name=L + "ffn_up_dw", out_dtype=BF16)
        dep = put_g(layer, "ffn", g)
        dz1, dz1b, dg1, db1 = _mm_ln_bwd(du, Wf["up_t"], rec["z1"], small["ln1_g"][layer][None], dz2,
                                         name=L + "ln1_bwd", dep=dep)
        gS["ln1_g"][layer], gS["ln1_b"][layer] = dg1[0], db1[0]
        g = {}
        if layer % 2 == 0:
            g["out"] = _mm(rec["y"], dz1b, ta=True, name=L + "ev_out_dw", out_dtype=BF16)
            dyy = _mm(dz1b, Wm["out"], tb=True, name=L + "ev_out_dx")
            dqa, dka, dva, dga = _ret_bwd(rec["h"], c2a, s2a, lgt, rec["ro"], dyy, name=L + "ret_bwd")
            dqb, dkb, dvb = _dil_bwd(rec["h"], cb, shi, slo, strip, rec["dil_o"], rec["lse"], dyy, name=L + "dil_bwd")
            dh = jnp.concatenate([dqa, dka, dva, dga, dqb, dkb, dvb], 1)
            g["in_t"] = _mm(dh, rec["xb"], ta=True, name=L + "ev_in_dw", out_dtype=BF16)
            dep = put_g(layer, "mixer", g)
        else:
            g["out"] = _mm(rec["y"], dz1b, ta=True, name=L + "od_out_dw", out_dtype=BF16)
            dyy = _mm(dz1b, Wm["out"], tb=True, name=L + "od_out_dx")
            do, dgate, dnw = _gdn_post_bwd(rec["o"], rec["h"], rec["nw"], dyy, name=L + "gdn_post_bwd")
            gS["od_norm_w"][j] = dnw[0]
            dq, dk, dv, dbb, dab, dal, ddt = _gdn_core_bwd(
                rec["q"], rec["k"], rec["v"], rec["bb"], rec["ab"], rec["alog"], rec["dtb"], rec["states"], do,
                name=L + "gdn_bwd")
            gS["od_a_log"][j] = dal[:, 0, 0]
            gS["od_dt_bias"][j] = ddt[:, 0, 0]
            dhq, dhk, dhv, dwq, dwk, dwv = _gdn_prep_bwd(rec["h"], rec["cw"], dq, dk, dv, name=L + "gdn_prep_bwd")
            g["conv"] = jnp.concatenate([dwq, dwk, dwv], 1).astype(BF16)
            dsm = jnp.concatenate([dbb[:, :, 0].T, dab[:, :, 0].T,
                                   jnp.zeros((t, LANES - 2 * GDN_HEADS), F32)], 1).astype(BF16)
            dh = jnp.concatenate([dhq, dhk, dhv, dgate, dsm], 1)
            g["in_t"] = _mm(dh, rec["xb"], ta=True, name=L + "od_in_dw", out_dtype=BF16)
            dep = put_g(layer, "mixer", g)
        below = (dh, Wm["in_t"], dz1, dep)
    grad_x = _axpy(_mm(below[0], below[1], name="L0_in_dx", dep=below[3]), below[2], name="grad_x")
    gS = {n: jnp.stack(v) for n, v in gS.items()}
    return loss, grad_x, gS


HBM = pl.BlockSpec(memory_space=pltpu.HBM)


def _me():
    return lax.axis_index("x"), lax.axis_index("y"), lax.axis_index("c")


def _my_index():
    x, y, c = _me()
    return 4 * x + 2 * y + c


SEM = pl.BlockSpec(memory_space=pltpu.SEMAPHORE)
ANY = pl.BlockSpec(memory_space=pl.ANY)
PLANS = {"scatter": (1, 2, 3, 4, 5, 6, 7), "spread": (1, 2, 4, 6), "relay": (2, 4, 6)}
SIBLING = 1


def _peer(kk):
    x, y, c = _me()
    return x ^ (kk >> 2), y ^ ((kk >> 1) & 1), c ^ (kk & 1)


def _peer_index(kk):
    px, py, pc = _peer(kk)
    return 4 * px + 2 * py + pc


def _job_copies(mode, srcs, lands, send_sems, recv_sems, incoming):
    myid = _my_index()
    plan = PLANS[mode]
    out = []
    for a in range(len(lands)):
        for idx, kk in enumerate(plan):
            if mode == "relay":
                to, src = _peer(SIBLING), lands[a].at[_peer_index(kk)]
                slot_there, slot_here = _peer_index(kk), _peer_index(kk ^ SIBLING)
            else:
                to, src = _peer(kk), (srcs[a] if mode == "spread" else srcs[a].at[_peer_index(kk)])
                slot_there, slot_here = myid, _peer_index(kk)
            sem = a * len(plan) + idx
            out.append(pltpu.make_async_remote_copy(
                src_ref=src, dst_ref=lands[a].at[slot_here if incoming else slot_there],
                send_sem=send_sems.at[sem], recv_sem=recv_sems.at[sem], device_id=to, device_id_type=MESH))
    return out


def _split_jobs(jobs, arrays):
    out, o = [], 0
    for (_, srcs, lands) in jobs:
        out.append((arrays[o:o + len(srcs)], arrays[o + len(srcs):o + len(srcs) + len(lands)]))
        o += len(srcs) + len(lands)
    return out


def _exchange_start(jobs, after, *, name):
    jobs = [(mode, list(srcs), [lax.empty((N_DEV, *s.shape) if mode == "spread" else s.shape, s.dtype) for s in srcs]
             if lands is None else list(lands)) for (mode, srcs, lands) in jobs]
    flat = [a for (_, srcs, lands) in jobs for a in (*srcs, *lands)]
    n, nj = len(flat), len(jobs)
    nsem = [len(PLANS[mode]) * len(lands) for (mode, _, lands) in jobs]

    def body(*refs):
        o = n + (0 if after is None else 1)
        sems, token = refs[o:o + 2 * nj], refs[o + 2 * nj + n]
        for ji, ((mode, _, _), (src, land)) in enumerate(zip(jobs, _split_jobs(jobs, refs[:n]))):
            for cp in _job_copies(mode, src, land, sems[2 * ji], sems[2 * ji + 1], False):
                cp.start()
        token[...] = jnp.zeros_like(token)

    outs = pl.pallas_call(
        body, name=name,
        out_shape=(*[pltpu.SemaphoreType.DMA((ns,)) for ns in nsem for _ in range(2)],
                   *[pltpu.HBM(a.shape, a.dtype) for a in flat], jax.ShapeDtypeStruct((8, LANES), F32)),
        in_specs=[HBM] * n + ([] if after is None else [ANY]),
        out_specs=(*[SEM] * (2 * nj), *[HBM] * n, pl.BlockSpec(memory_space=pltpu.VMEM)),
        input_output_aliases={i: 2 * nj + i for i in range(n)},
        compiler_params=pltpu.CompilerParams(has_side_effects=pltpu.SideEffectType.DATAFLOW_SIDE_EFFECTING),
    )(*[pltpu.with_memory_space_constraint(a, pltpu.HBM) for a in flat], *([] if after is None else [after]))
    thru = _split_jobs(jobs, list(outs[2 * nj:2 * nj + n]))
    started = [(mode, outs[2 * ji], outs[2 * ji + 1], src, land) for ji, ((mode, _, _), (src, land)) in enumerate(zip(jobs, thru))]
    return started, outs[2 * nj + n]


def _exchange_wait(started, after, *, name):
    jobs = [(mode, srcs, lands) for (mode, _, _, srcs, lands) in started]
    flat = [a for (_, srcs, lands) in jobs for a in (*srcs, *lands)]
    n, nj = len(flat), len(jobs)

    def body(*refs):
        sems = refs[n:n + 2 * nj]
        for ji, ((mode, _, _), (src, land)) in enumerate(zip(jobs, _split_jobs(jobs, refs[:n]))):
            for cp in _job_copies(mode, src, land, sems[2 * ji], sems[2 * ji + 1], True):
                cp.wait_send()
                cp.wait_recv()

    outs = pl.pallas_call(
        body, name=name, out_shape=tuple(pltpu.HBM(a.shape, a.dtype) for a in flat),
        in_specs=[HBM] * n + [SEM] * (2 * nj) + [ANY], out_specs=tuple([HBM] * n),
        input_output_aliases={i: i for i in range(n)},
        compiler_params=pltpu.CompilerParams(has_side_effects=pltpu.SideEffectType.DATAFLOW_SIDE_EFFECTING),
    )(*flat, *[s for (_, ss, rs, _, _) in started for s in (ss, rs)], after)
    return _split_jobs(jobs, list(outs))


def _sum8(land, *, name):
    _, rr, cc = land.shape
    tr = _row_tile(rr)

    def body(l_ref, o_ref):
        acc = l_ref[0].astype(F32)
        for d in range(1, N_DEV):
            acc = acc + l_ref[d].astype(F32)
        o_ref[...] = acc

    return pl.pallas_call(
        body, grid=(rr // tr,), in_specs=[pl.BlockSpec((N_DEV, tr, cc), lambda i: (0, i, 0))],
        out_specs=pl.BlockSpec((tr, cc), lambda i: (i, 0)), out_shape=jax.ShapeDtypeStruct((rr, cc), F32),
        name=name, compiler_params=_cp())(land)


def _row_tile(rr):
    for cand in (512, 384, 256, 192, 176, 128, 64, 32, 16, 8):
        if rr % cand == 0:
            return cand
    return rr


def _small_exchange(vec, *, name):
    rr = vec.shape[0]

    def body(v_ref, o_ref, send_sems, recv_sems):
        x, y, c = _me()
        myid = 4 * x + 2 * y + c
        o_ref[myid] = v_ref[...]
        cps = []
        for kk in range(1, N_DEV):
            px, py, pc = x ^ (kk >> 2), y ^ ((kk >> 1) & 1), c ^ (kk & 1)
            cps.append(pltpu.make_async_remote_copy(
                src_ref=v_ref, dst_ref=o_ref.at[myid], send_sem=send_sems.at[kk], recv_sem=recv_sems.at[kk],
                device_id=(px, py, pc), device_id_type=MESH))
        for cp in cps:
            cp.start()
        for kk in range(1, N_DEV):
            px, py, pc = x ^ (kk >> 2), y ^ ((kk >> 1) & 1), c ^ (kk & 1)
            pltpu.make_async_remote_copy(
                src_ref=v_ref, dst_ref=o_ref.at[4 * px + 2 * py + pc], send_sem=send_sems.at[kk],
                recv_sem=recv_sems.at[kk], device_id=(px, py, pc), device_id_type=MESH).wait_recv()
        for cp in cps:
            cp.wait_send()

    return pl.pallas_call(
        body, in_specs=[pl.BlockSpec(memory_space=pltpu.VMEM)], out_specs=pl.BlockSpec(memory_space=pltpu.VMEM),
        out_shape=jax.ShapeDtypeStruct((N_DEV, rr, LANES), F32),
        scratch_shapes=[pltpu.SemaphoreType.DMA((N_DEV,)), pltpu.SemaphoreType.DMA((N_DEV,))],
        name=name, compiler_params=pltpu.CompilerParams(has_side_effects=True))(vec)


def _adam_math(w, g, m, v):
    m = ADAM_B1 * m + (1.0 - ADAM_B1) * g
    v = ADAM_B2 * v + (1.0 - ADAM_B2) * (g * g)
    m_hat = m / (1.0 - ADAM_B1 ** ADAM_STEP)
    v_hat = v / (1.0 - ADAM_B2 ** ADAM_STEP)
    delta = -ADAM_LR * (m_hat / (jnp.sqrt(v_hat) + ADAM_EPS) + ADAM_WD * w)
    return delta, m, v


def _adamw_sharded(w, m, v, g, *, name):
    ll, rr, cc = w.shape
    tr = _row_tile(rr)

    def body(w_ref, m_ref, v_ref, g_ref, d_ref, nm_ref, nv_ref):
        d, nm, nv = _adam_math(w_ref[...], g_ref[...], m_ref[...], v_ref[...])
        d_ref[...] = d
        nm_ref[...] = nm
        nv_ref[...] = nv

    blk = pl.BlockSpec((1, tr, cc), lambda l, i: (l, i, 0))
    sh = jax.ShapeDtypeStruct((ll, rr, cc), F32)
    return pl.pallas_call(
        body, grid=(ll, rr // tr), in_specs=[blk] * 4, out_specs=[blk] * 3, out_shape=[sh] * 3,
        name=name, compiler_params=_cp())(w, m, v, g)


def _adamw_small(w, m, v, gall, *, name):
    rr = w.shape[0]

    def body(w_ref, m_ref, v_ref, g_ref, go_ref, d_ref, nm_ref, nv_ref):
        g = g_ref[0]
        for kk in range(1, N_DEV):
            g = g + g_ref[kk]
        d, nm, nv = _adam_math(w_ref[...], g, m_ref[...], v_ref[...])
        go_ref[...] = g
        d_ref[...] = d
        nm_ref[...] = nm
        nv_ref[...] = nv

    sh = jax.ShapeDtypeStruct((rr, LANES), F32)
    return pl.pallas_call(body, out_shape=[sh] * 4, name=name, compiler_params=_cp())(w, m, v, gall)


SHARDED = ("ev_w_in", "ev_w_out", "od_w_in", "od_conv_w", "od_w_out", "ffn_w_up", "ffn_conv_w", "ffn_w_down")
SMALL = ("od_a_log", "od_dt_bias", "od_norm_w", "ffn_conv_b", "ln1_g", "ln1_b", "ln2_g", "ln2_b")
ALL_W = ("ev_w_in", "ev_w_out", "od_w_in", "od_conv_w", "od_a_log", "od_dt_bias", "od_norm_w", "od_w_out",
         "ffn_w_up", "ffn_conv_w", "ffn_conv_b", "ffn_w_down", "ln1_g", "ln1_b", "ln2_g", "ln2_b")


def _layer_items(layer):
    j = layer // 2
    if layer % 2 == 0:
        mixer = [("in_t", "ev_w_in", j, "colT"), ("out", "ev_w_out", j, "row")]
    else:
        mixer = [("in_t", "od_w_in", j, "colT"), ("conv", "od_conv_w", j, "colsmall"), ("out", "od_w_out", j, "row")]
    return mixer + [("up_t", "ffn_w_up", layer, "colT"), ("fconv", "ffn_conv_w", layer, "colsmall"),
                    ("down", "ffn_w_down", layer, "row")]


def _to_send(kind, w, j):
    if kind == "colT":
        return w[j].T.astype(BF16)
    return w[j].astype(BF16) if kind == "row" else w[j]


def _from_gather(kind, name, g):
    if kind == "colsmall":
        return jnp.transpose(g, (1, 0, 2)).reshape(g.shape[1], -1)
    full = g.reshape(-1, g.shape[-1])
    if name == "od_w_in":
        full = jnp.pad(full, ((0, OD_IN_PAD - OD_IN), (0, 0)))
    return full


def _by_owner(kind, name, gfull):
    if kind == "colsmall":
        kk, c8 = gfull.shape
        return jnp.transpose(gfull.reshape(kk, N_DEV, c8 // N_DEV), (1, 0, 2))
    if name == "od_w_in":
        gfull = gfull[:OD_IN]
    return gfull.reshape(N_DEV, gfull.shape[0] // N_DEV, gfull.shape[1])


def _pack_small(d):
    flat = jnp.concatenate([d[n].reshape(-1) for n in SMALL])
    pad = (-flat.shape[0]) % (8 * LANES)
    return jnp.pad(flat, (0, pad)).reshape(-1, LANES)


def _unpack_small(packed, like):
    flat = packed.reshape(-1)
    out, off = {}, 0
    for n in SMALL:
        sz = int(np.prod(like[n].shape))
        out[n] = flat[off:off + sz].reshape(like[n].shape)
        off += sz
    return out


def kernel(x, positions, ev_w_in, ev_w_out, od_w_in, od_conv_w, od_a_log, od_dt_bias, od_norm_w, od_w_out, ffn_w_up, ffn_conv_w, ffn_conv_b, ffn_w_down, ln1_g, ln1_b, ln2_g, ln2_b, loss_target, m_ev_w_in, m_ev_w_out, m_od_w_in, m_od_conv_w, m_od_a_log, m_od_dt_bias, m_od_norm_w, m_od_w_out, m_ffn_w_up, m_ffn_conv_w, m_ffn_conv_b, m_ffn_w_down, m_ln1_g, m_ln1_b, m_ln2_g, m_ln2_b, v_ev_w_in, v_ev_w_out, v_od_w_in, v_od_conv_w, v_od_a_log, v_od_dt_bias, v_od_norm_w, v_od_w_out, v_ffn_w_up, v_ffn_conv_w, v_ffn_conv_b, v_ffn_w_down, v_ln1_g, v_ln1_b, v_ln2_g, v_ln2_b):
    w = dict(ev_w_in=ev_w_in, ev_w_out=ev_w_out, od_w_in=od_w_in, od_conv_w=od_conv_w, od_a_log=od_a_log,
             od_dt_bias=od_dt_bias, od_norm_w=od_norm_w, od_w_out=od_w_out, ffn_w_up=ffn_w_up, ffn_conv_w=ffn_conv_w,
             ffn_conv_b=ffn_conv_b, ffn_w_down=ffn_w_down, ln1_g=ln1_g, ln1_b=ln1_b, ln2_g=ln2_g, ln2_b=ln2_b)
    mom = dict(ev_w_in=m_ev_w_in, ev_w_out=m_ev_w_out, od_w_in=m_od_w_in, od_conv_w=m_od_conv_w, od_a_log=m_od_a_log,
               od_dt_bias=m_od_dt_bias, od_norm_w=m_od_norm_w, od_w_out=m_od_w_out, ffn_w_up=m_ffn_w_up,
               ffn_conv_w=m_ffn_conv_w, ffn_conv_b=m_ffn_conv_b, ffn_w_down=m_ffn_w_down, ln1_g=m_ln1_g,
               ln1_b=m_ln1_b, ln2_g=m_ln2_g, ln2_b=m_ln2_b)
    var = dict(ev_w_in=v_ev_w_in, ev_w_out=v_ev_w_out, od_w_in=v_od_w_in, od_conv_w=v_od_conv_w, od_a_log=v_od_a_log,
               od_dt_bias=v_od_dt_bias, od_norm_w=v_od_norm_w, od_w_out=v_od_w_out, ffn_w_up=v_ffn_w_up,
               ffn_conv_w=v_ffn_conv_w, ffn_conv_b=v_ffn_conv_b, ffn_w_down=v_ffn_w_down, ln1_g=v_ln1_g,
               ln1_b=v_ln1_b, ln2_g=v_ln2_g, ln2_b=v_ln2_b)

    myid = _my_index()
    small = {n: w[n] for n in SMALL}
    groups = [(layer, part) for layer in range(DEPTH) for part in ("mixer", "ffn")]

    def group_items(gi):
        layer, part = groups[gi]
        its = _layer_items(layer)
        return its[:-3] if part == "mixer" else its[-3:]

    level1, level2 = {}, {}

    def spread_job(gi):
        return ("spread", [_to_send(kind, w[n], j) for (_, n, j, kind) in group_items(gi)], None)

    def relay(gi, after, name):
        (srcs, lands), = _exchange_wait([level1.pop(gi)], after, name=name + "_wait")
        more = [spread_job(gi + 1)] if gi + 1 < len(groups) else []
        started, token = _exchange_start([("relay", [], lands)] + more, None, name=name + "_start")
        level2[gi] = (started[0], srcs)
        if more:
            level1[gi + 1] = started[1]
        return token

    def get_w(layer, part, after):
        gi = groups.index((layer, part))
        started, srcs = level2.pop(gi)
        (_, lands), = _exchange_wait([started], after, name=f"gather{gi}_wait")
        lands = [lax.dynamic_update_index_in_dim(l, s, myid, 0) for l, s in zip(lands, srcs)]
        return {key: _from_gather(kind, n, l) for (key, n, _, kind), l in zip(group_items(gi), lands)}, None

    def mid(layer, part, after):
        gi = groups.index((layer, part)) + 1
        return relay(gi, after, f"gather{gi}_relay") if gi < len(groups) else None

    landed = {}
    pending = []

    def scatter_finish(after):
        started, gi = pending.pop()
        (srcs, lands), = _exchange_wait([started], after, name=f"scatter{gi}_wait")
        for (key, _, _, _), l, s in zip(group_items(gi), lands, srcs):
            own = lax.dynamic_index_in_dim(s, myid, 0, keepdims=False)
            landed[(groups[gi][0], key)] = lax.dynamic_update_index_in_dim(l, own, myid, 0)

    def put_g(layer, part, g):
        gi = groups.index((layer, part))
        srcs = [_by_owner(kind, n, g[key]) for (key, n, _, kind) in group_items(gi)]
        (started,), token = _exchange_start([("scatter", srcs, None)], None, name=f"scatter{gi}_start")
        if pending:
            scatter_finish(token)
        pending.append((started, gi))
        return token

    (level1[0],), token = _exchange_start([spread_job(0)], None, name="gather0_spread_start")
    relay(0, token, "gather0_relay")
    loss, grad_x, gS = _local_step(x[0], positions[0], loss_target[0], get_w, mid, put_g, small)
    loss = lax.psum(loss, ("x", "y", "c"))

    outs_g, outs_d, outs_m, outs_v = {}, {}, {}, {}
    where = {n: [None] * w[n].shape[0] for n in SHARDED}
    for layer in range(DEPTH):
        for (key, n, j, kind) in _layer_items(layer):
            where[n][j] = (layer, key, kind)

    def update(n):
        g = jnp.stack([_sum8(landed[(layer, key)], name=f"L{layer}_{key}_sum") for (layer, key, _) in where[n]])
        if where[n][0][2] == "colT":
            tr = lambda a: jnp.swapaxes(a, 1, 2)
            d, nm, nv = _adamw_sharded(tr(w[n]), tr(mom[n]), tr(var[n]), g, name=f"adamw_{n}")
            outs_g[n], outs_d[n], outs_m[n], outs_v[n] = tr(g), tr(d), tr(nm), tr(nv)
        else:
            outs_g[n] = g
            outs_d[n], outs_m[n], outs_v[n] = _adamw_sharded(w[n], mom[n], var[n], g, name=f"adamw_{n}")

    last = {n for (_, n, _, _) in group_items(pending[0][1])}
    for n in SHARDED:
        if n not in last:
            update(n)
    scatter_finish(outs_d[[n for n in SHARDED if n not in last][-1]])
    for n in SHARDED:
        if n in last:
            update(n)

    gall = _small_exchange(_pack_small(gS), name="small_grads_exchange")
    g, d, nm, nv = _adamw_small(_pack_small({n: w[n] for n in SMALL}), _pack_small({n: mom[n] for n in SMALL}),
                                _pack_small({n: var[n] for n in SMALL}), gall, name="adamw_small")
    for dst, packed in ((outs_g, g), (outs_d, d), (outs_m, nm), (outs_v, nv)):
        dst.update(_unpack_small(packed, {n: w[n] for n in SMALL}))

    return (loss, grad_x[None], *[outs_g[n] for n in ALL_W], *[outs_d[n] for n in ALL_W],
            *[outs_m[n] for n in ALL_W], *[outs_v[n] for n in ALL_W])
```

```python
import functools
import math

import numpy as np
import jax
import jax.numpy as jnp
from jax import lax
from jax.experimental import pallas as pl
from jax.experimental.pallas import tpu as pltpu

F32 = jnp.float32
BF16 = jnp.bfloat16
MESH = pl.DeviceIdType.MESH

D_MODEL = 1024
SEQ = 2048
DEPTH = 4
N_DEV = 8
RET_HEADS, RET_DK, RET_DV = 4, 128, 256
RET_THETA = 10000.0
DIL_HEADS, DIL_HD = 8, 64
DIL_PAIRS = ((128, 1), (512, 4), (2048, 16))
ROPE_THETA = 500000.0
ROPE_DIMS = DIL_HD // 4
GDN_HEADS, GDN_DK, GDN_DV, GDN_CHUNK, GDN_CONV = 8, 128, 128, 64, 4
D_FF = 2816
FFN_CONV = 3
ALPHA = (2.0 * DEPTH) ** 0.25
EPS = 1e-5
RET_QK_W = RET_HEADS * RET_DK
RET_V_W = RET_HEADS * RET_DV
DIL_W = DIL_HEADS * DIL_HD
EV_IN = 2 * RET_QK_W + 2 * RET_V_W + 3 * DIL_W
EV_MIX = RET_V_W + DIL_W
GDN_W = GDN_HEADS * GDN_DK
OD_IN = 4 * GDN_W + 2 * GDN_HEADS
OD_IN_PAD = 4 * GDN_W + 128
ADAM_LR, ADAM_B1, ADAM_B2, ADAM_EPS, ADAM_WD, ADAM_STEP = 0.001, 0.9, 0.999, 1e-08, 0.01, 10

LANES = 128
VMEM_LIMIT = 56 * 1024 * 1024
ATT_BLK = 256
NEG = -1e30


def _cp(**kw):
    return pltpu.CompilerParams(vmem_limit_bytes=VMEM_LIMIT, **kw)


def _tile(n, cap):
    if n <= cap:
        return n
    best = None
    for t in range(LANES, cap + 1, LANES):
        if n % t == 0:
            best = t
    assert best is not None, (n, cap)
    return best


def _mm(a, b, *, ta=False, tb=False, name, out_dtype=F32, dep=None, tm=None, tn=None):
    m = a.shape[1] if ta else a.shape[0]
    k = a.shape[0] if ta else a.shape[1]
    n = b.shape[0] if tb else b.shape[1]
    assert (b.shape[1] if tb else b.shape[0]) == k
    assert a.dtype == BF16 and b.dtype == BF16
    if tn is None:
        tn = n if n <= 1024 else _tile(n, 512)
    if tm is None:
        tm = m if (tn < n and k <= 1024 and m <= 2048) else _tile(m, 512)
    dims = (((0 if ta else 1,), (1 if tb else 0,)), ((), ()))

    def body(a_ref, b_ref, *rest):
        o_ref = rest[-1]
        o_ref[...] = lax.dot_general(a_ref[...], b_ref[...], dims,
                                     preferred_element_type=F32).astype(o_ref.dtype)

    a_spec = pl.BlockSpec((k, tm), lambda i, j: (0, i)) if ta else pl.BlockSpec((tm, k), lambda i, j: (i, 0))
    b_spec = pl.BlockSpec((tn, k), lambda i, j: (j, 0)) if tb else pl.BlockSpec((k, tn), lambda i, j: (0, j))
    extra = [] if dep is None else [dep]
    return pl.pallas_call(
        body, grid=(m // tm, n // tn), in_specs=[a_spec, b_spec] + [pl.BlockSpec(memory_space=pl.ANY)] * len(extra),
        out_specs=pl.BlockSpec((tm, tn), lambda i, j: (i, j)),
        out_shape=jax.ShapeDtypeStruct((m, n), out_dtype), name=name, compiler_params=_cp())(a, b, *extra)


LN_ROWS = 256


def _ln_bwd(z, g, dya, dyb, *, name):
    t, d = z.shape
    two = dyb is not None

    def body(*refs):
        if two:
            z_ref, g_ref, dya_ref, dyb_ref, dz_ref, dzb_ref, dg_ref, db_ref = refs
            dy = dya_ref[...] + ALPHA * dyb_ref[...]
        else:
            z_ref, g_ref, dya_ref, dz_ref, dzb_ref, dg_ref, db_ref = refs
            dy = dya_ref[...]
        zz = z_ref[...]
        mu = jnp.mean(zz, -1, keepdims=True)
        zc = zz - mu
        var = jnp.mean(zc * zc, -1, keepdims=True)
        r = lax.rsqrt(var + EPS)
        xh = zc * r
        dxh = dy * g_ref[...]
        dz = r * (dxh - jnp.mean(dxh, -1, keepdims=True) - xh * jnp.mean(dxh * xh, -1, keepdims=True))
        dz_ref[...] = dz
        dzb_ref[...] = dz.astype(BF16)

        @pl.when(pl.program_id(0) == 0)
        def _():
            dg_ref[...] = jnp.zeros_like(dg_ref)
            db_ref[...] = jnp.zeros_like(db_ref)

        dg_ref[...] += jnp.sum(dy * xh, 0, keepdims=True)
        db_ref[...] += jnp.sum(dy, 0, keepdims=True)

    row = pl.BlockSpec((LN_ROWS, d), lambda i: (i, 0))
    vec = pl.BlockSpec((1, d), lambda i: (0, 0))
    ins = [z, g, dya] + ([dyb] if two else [])
    return pl.pallas_call(
        body, grid=(t // LN_ROWS,), in_specs=[row, vec, row] + ([row] if two else []),
        out_specs=[row, row, vec, vec],
        out_shape=[jax.ShapeDtypeStruct((t, d), F32), jax.ShapeDtypeStruct((t, d), BF16),
                   jax.ShapeDtypeStruct((1, d), F32), jax.ShapeDtypeStruct((1, d), F32)],
        name=name, compiler_params=_cp())(*ins)


def _ln_rows(k):
    return 256 if k > 4096 else 512


def _mm_ln_fwd(a, w, x, g, b, *, name, dep=None):
    t, k = a.shape
    d = w.shape[1]
    tm = _ln_rows(k)

    def body(a_ref, w_ref, x_ref, g_ref, b_ref, *rest):
        z_ref, y_ref, yb_ref = rest[-3:]
        z = ALPHA * x_ref[...] + _nn(a_ref[...], w_ref[...])
        mu = jnp.mean(z, -1, keepdims=True)
        zc = z - mu
        var = jnp.mean(zc * zc, -1, keepdims=True)
        y = zc * lax.rsqrt(var + EPS) * g_ref[...] + b_ref[...]
        z_ref[...] = z
        y_ref[...] = y
        yb_ref[...] = y.astype(BF16)

    row = pl.BlockSpec((tm, d), lambda i: (i, 0))
    vec = pl.BlockSpec((1, d), lambda i: (0, 0))
    extra = [] if dep is None else [dep]
    return pl.pallas_call(
        body, grid=(t // tm,),
        in_specs=[pl.BlockSpec((tm, k), lambda i: (i, 0)), pl.BlockSpec((k, d), lambda i: (0, 0)), row, vec, vec]
        + [pl.BlockSpec(memory_space=pl.ANY)] * len(extra),
        out_specs=[row, row, row],
        out_shape=[jax.ShapeDtypeStruct((t, d), F32), jax.ShapeDtypeStruct((t, d), F32), jax.ShapeDtypeStruct((t, d), BF16)],
        name=name, compiler_params=_cp())(a, w, x, g, b, *extra)


def _mm_ln_bwd(a, w, z, g, dyb, *, name, dep=None):
    t, k = a.shape
    d = w.shape[1]
    tm = _ln_rows(k)

    def body(a_ref, w_ref, z_ref, g_ref, dyb_ref, *rest):
        dz_ref, dzb_ref, dg_ref, db_ref = rest[-4:]
        dy = _nn(a_ref[...], w_ref[...]) + ALPHA * dyb_ref[...]
        zz = z_ref[...]
        mu = jnp.mean(zz, -1, keepdims=True)
        zc = zz - mu
        var = jnp.mean(zc * zc, -1, keepdims=True)
        r = lax.rsqrt(var + EPS)
        xh = zc * r
        dxh = dy * g_ref[...]
        dz = r * (dxh - jnp.mean(dxh, -1, keepdims=True) - xh * jnp.mean(dxh * xh, -1, keepdims=True))
        dz_ref[...] = dz
        dzb_ref[...] = dz.astype(BF16)

        @pl.when(pl.program_id(0) == 0)
        def _():
            dg_ref[...] = jnp.zeros_like(dg_ref)
            db_ref[...] = jnp.zeros_like(db_ref)

        dg_ref[...] += jnp.sum(dy * xh, 0, keepdims=True)
        db_ref[...] += jnp.sum(dy, 0, keepdims=True)

    row = pl.BlockSpec((tm, d), lambda i: (i, 0))
    vec = pl.BlockSpec((1, d), lambda i: (0, 0))
    extra = [] if dep is None else [dep]
    return pl.pallas_call(
        body, grid=(t // tm,),
        in_specs=[pl.BlockSpec((tm, k), lambda i: (i, 0)), pl.BlockSpec((k, d), lambda i: (0, 0)), row, vec, row]
        + [pl.BlockSpec(memory_space=pl.ANY)] * len(extra),
        out_specs=[row, row, vec, vec],
        out_shape=[jax.ShapeDtypeStruct((t, d), F32), jax.ShapeDtypeStruct((t, d), BF16),
                   jax.ShapeDtypeStruct((1, d), F32), jax.ShapeDtypeStruct((1, d), F32)],
        name=name, compiler_params=_cp())(a, w, z, g, dyb, *extra)


def _axpy(a, b, *, name):
    t, d = a.shape

    def body(a_ref, b_ref, o_ref):
        o_ref[...] = a_ref[...] + ALPHA * b_ref[...]

    row = pl.BlockSpec((LN_ROWS, d), lambda i: (i, 0))
    return pl.pallas_call(body, grid=(t // LN_ROWS,), in_specs=[row, row], out_specs=row,
                          out_shape=jax.ShapeDtypeStruct((t, d), F32), name=name, compiler_params=_cp())(a, b)


def _loss_head(y, target, *, name):
    t, d = y.shape

    def body(y_ref, t_ref, dy_ref, l_ref):
        e = y_ref[...] - t_ref[...]
        dy_ref[...] = e * (1.0 / d)

        @pl.when(pl.program_id(0) == 0)
        def _():
            l_ref[...] = jnp.zeros_like(l_ref)

        l_ref[...] += jnp.zeros_like(l_ref) + 0.5 * jnp.sum(jnp.mean(e * e, -1, keepdims=True), 0, keepdims=True)

    row = pl.BlockSpec((LN_ROWS, d), lambda i: (i, 0))
    return pl.pallas_call(
        body, grid=(t // LN_ROWS,), in_specs=[row, row],
        out_specs=[row, pl.BlockSpec((1, LANES), lambda i: (0, 0))],
        out_shape=[jax.ShapeDtypeStruct((t, d), F32), jax.ShapeDtypeStruct((1, LANES), F32)],
        name=name, compiler_params=_cp())(y, target)


def _sig(x):
    return 1.0 / (1.0 + jnp.exp(-x))


def _silu(x):
    return x * _sig(x)


def _dsilu(x):
    s = _sig(x)
    return s * (1.0 + x * (1.0 - s))


def _shift_down(u, k, row):
    if k == 0:
        return u
    return jnp.where(row >= k, pltpu.roll(u, k, 0), 0.0)


def _shift_up(u, k, row):
    if k == 0:
        return u
    t = u.shape[0]
    return jnp.where(row < t - k, pltpu.roll(u, t - k, 0), 0.0)


def _dwconv(u, w_ref, row):
    kk = w_ref.shape[0]
    acc = None
    for j in range(kk):
        term = w_ref[j:j + 1, :] * _shift_down(u, kk - 1 - j, row)
        acc = term if acc is None else acc + term
    return acc


def _dwconv_bwd(u, w_ref, dc, row, dw_ref):
    kk = w_ref.shape[0]
    du = None
    for j in range(kk):
        term = w_ref[j:j + 1, :] * _shift_up(dc, kk - 1 - j, row)
        du = term if du is None else du + term
        dw_ref[j:j + 1, :] = jnp.sum(dc * _shift_down(u, kk - 1 - j, row), 0, keepdims=True)
    return du


CONV_ROWS = 64


def _rows(b):
    return pl.ds(pl.multiple_of(b * CONV_ROWS, CONV_ROWS), CONV_ROWS)


def _shifted_down(ref, b, k, row):
    cur = ref[_rows(b), :]
    if k == 0:
        return cur
    prev = jnp.where(b > 0, ref[_rows(jnp.maximum(b - 1, 0)), :], 0.0)
    return jnp.where(row >= k, pltpu.roll(cur, k, 0), pltpu.roll(prev, k, 0))


def _shifted_up(ref, b, k, row, nblk):
    cur = ref[_rows(b), :]
    if k == 0:
        return cur
    nxt = jnp.where(b < nblk - 1, ref[_rows(jnp.minimum(b + 1, nblk - 1)), :], 0.0)
    return jnp.where(row < CONV_ROWS - k, pltpu.roll(cur, CONV_ROWS - k, 0), pltpu.roll(nxt, CONV_ROWS - k, 0))


def _dwconv_blk(u_ref, w_ref, b, row):
    kk = w_ref.shape[0]
    views = [_shifted_down(u_ref, b, kk - 1 - j, row) for j in range(kk)]
    acc = None
    for j in range(kk):
        term = w_ref[j:j + 1, :] * views[j]
        acc = term if acc is None else acc + term
    return acc, views


def _dwconv_du_blk(dc_ref, w_ref, b, row, nblk):
    kk = w_ref.shape[0]
    du = None
    for j in range(kk):
        term = w_ref[j:j + 1, :] * _shifted_up(dc_ref, b, kk - 1 - j, row, nblk)
        du = term if du is None else du + term
    return du


FFN_TC = 256


def _ffn_up_mid(x, up_t, cw, cb, *, name, dep=None):
    t, d = x.shape
    nb = D_FF // FFN_TC

    def body(x_ref, ugt_ref, uvt_ref, wg_ref, wv_ref, bg_ref, bv_ref, *rest):
        ug_ref, uv_ref, a_ref = rest[-3:]
        xx = x_ref[...]
        row = lax.broadcasted_iota(jnp.int32, (t, FFN_TC), 0)
        ug = _nt(xx, ugt_ref[...])
        ug_ref[...] = ug
        uv = _nt(xx, uvt_ref[...])
        uv_ref[...] = uv
        cg = _dwconv(ug, wg_ref, row) + bg_ref[...]
        cv = _dwconv(uv, wv_ref, row) + bv_ref[...]
        a_ref[...] = (_silu(cg) * cv).astype(BF16)

    col = pl.BlockSpec((t, FFN_TC), lambda j: (0, j))
    wt = lambda off: pl.BlockSpec((FFN_TC, d), lambda j: (j + off, 0))
    wsp = lambda off: pl.BlockSpec((FFN_CONV, FFN_TC), lambda j: (0, j + off))
    bsp = lambda off: pl.BlockSpec((1, FFN_TC), lambda j: (0, j + off))
    extra = [] if dep is None else [dep]
    return pl.pallas_call(
        body, grid=(nb,),
        in_specs=[pl.BlockSpec((t, d), lambda j: (0, 0)), wt(0), wt(nb), wsp(0), wsp(nb), bsp(0), bsp(nb)]
        + [pl.BlockSpec(memory_space=pl.ANY)] * len(extra),
        out_specs=[col, col, col],
        out_shape=[jax.ShapeDtypeStruct((t, D_FF), F32), jax.ShapeDtypeStruct((t, D_FF), F32),
                   jax.ShapeDtypeStruct((t, D_FF), BF16)],
        name=name, compiler_params=_cp())(x, up_t, up_t, cw, cw, cb, cb, *extra)


def _ffn_mid_bwd(ug, uv, cw, cb, dz, down, *, name):
    t, d = dz.shape
    nb = D_FF // FFN_TC

    nblk = t // CONV_ROWS

    def body(ug_ref, uv_ref, wg_ref, wv_ref, bg_ref, bv_ref, dz_ref, dn_ref,
             dug_ref, duv_ref, dwg_ref, dwv_ref, dbg_ref, dbv_ref, da_ref, dcg_s, dcv_s):
        da_ref[...] = _nt(dz_ref[...], dn_ref[...])
        row = lax.broadcasted_iota(jnp.int32, (CONV_ROWS, FFN_TC), 0)
        zero = jnp.zeros((1, FFN_TC), F32)

        def first(b, acc):
            cg, ugs = _dwconv_blk(ug_ref, wg_ref, b, row)
            cv, uvs = _dwconv_blk(uv_ref, wv_ref, b, row)
            cg = cg + bg_ref[...]
            cv = cv + bv_ref[...]
            da_ = da_ref[_rows(b), :]
            dcv = da_ * _silu(cg)
            dcg = da_ * cv * _dsilu(cg)
            dcg_s[_rows(b), :] = dcg
            dcv_s[_rows(b), :] = dcv
            red = [jnp.sum(dcg * s, 0, keepdims=True) for s in ugs] + [jnp.sum(dcg, 0, keepdims=True)]
            red += [jnp.sum(dcv * s, 0, keepdims=True) for s in uvs] + [jnp.sum(dcv, 0, keepdims=True)]
            return tuple(a + r for a, r in zip(acc, red))

        acc = lax.fori_loop(0, nblk, first, (zero,) * (2 * FFN_CONV + 2))
        for j in range(FFN_CONV):
            dwg_ref[j:j + 1, :] = acc[j]
            dwv_ref[j:j + 1, :] = acc[FFN_CONV + 1 + j]
        dbg_ref[...] = acc[FFN_CONV]
        dbv_ref[...] = acc[2 * FFN_CONV + 1]

        def second(b, carry):
            dug_ref[_rows(b), :] = _dwconv_du_blk(dcg_s, wg_ref, b, row, nblk).astype(BF16)
            duv_ref[_rows(b), :] = _dwconv_du_blk(dcv_s, wv_ref, b, row, nblk).astype(BF16)
            return carry

        lax.fori_loop(0, nblk, second, 0)

    col = pl.BlockSpec((t, FFN_TC), lambda j: (0, j))
    wsp = lambda off: pl.BlockSpec((FFN_CONV, FFN_TC), lambda j: (0, j + off))
    bsp = lambda off: pl.BlockSpec((1, FFN_TC), lambda j: (0, j + off))
    outs = pl.pallas_call(
        body, grid=(nb,),
        in_specs=[col, col, wsp(0), wsp(nb), bsp(0), bsp(nb), pl.BlockSpec((t, d), lambda j: (0, 0)),
                  pl.BlockSpec((FFN_TC, d), lambda j: (j, 0))],
        out_specs=[col, col, wsp(0), wsp(0), bsp(0), bsp(0)],
        out_shape=[jax.ShapeDtypeStruct((t, D_FF), BF16), jax.ShapeDtypeStruct((t, D_FF), BF16),
                   jax.ShapeDtypeStruct((FFN_CONV, D_FF), F32), jax.ShapeDtypeStruct((FFN_CONV, D_FF), F32),
                   jax.ShapeDtypeStruct((1, D_FF), F32), jax.ShapeDtypeStruct((1, D_FF), F32)],
        scratch_shapes=[pltpu.VMEM((t, FFN_TC), F32), pltpu.VMEM((t, FFN_TC), F32), pltpu.VMEM((t, FFN_TC), F32)],
        name=name, compiler_params=_cp())(ug, uv, cw, cw, cb, cb, dz, down)
    dug, duv, dwg, dwv, dbg, dbv = outs
    return (jnp.concatenate([dug, duv], 1), jnp.concatenate([dwg, dwv], 1), jnp.concatenate([dbg, dbv], 1))


def _rot_a(x, c2, s2):
    return x * c2 + pltpu.roll(x, RET_DK // 2, 1) * s2


def _rot_a_t(dy, c2, s2):
    return dy * c2 + pltpu.roll(dy * s2, RET_DK // 2, 1)


def _decay_tile(lg, blk_diff):
    r = lax.broadcasted_iota(jnp.int32, (ATT_BLK, ATT_BLK), 0)
    c = lax.broadcasted_iota(jnp.int32, (ATT_BLK, ATT_BLK), 1)
    rel = r - c + blk_diff * ATT_BLK
    return jnp.where(rel >= 0, jnp.exp(jnp.maximum(rel, 0).astype(F32) * lg), 0.0)


def _nt(a, b):
    return lax.dot_general(a, b, (((1,), (1,)), ((), ())), preferred_element_type=F32)


def _nn(a, b):
    return lax.dot_general(a, b, (((1,), (0,)), ((), ())), preferred_element_type=F32)


def _tn(a, b):
    return lax.dot_general(a, b, (((0,), (0,)), ((), ())), preferred_element_type=F32)


def _ret_specs(t):
    q = pl.BlockSpec((t, RET_DK), lambda h: (0, h))
    k = pl.BlockSpec((t, RET_DK), lambda h: (0, RET_HEADS + h))
    v = pl.BlockSpec((t, RET_DV), lambda h: (0, RET_HEADS + h))
    g = pl.BlockSpec((t, RET_DV), lambda h: (0, 2 * RET_HEADS + h))
    tab = pl.BlockSpec((t, RET_DK), lambda h: (0, 0))
    lg = pl.BlockSpec((1, 1, LANES), lambda h: (h, 0, 0))
    return q, k, v, g, tab, lg


def _ret_fwd(h, c2, s2, lgt, *, name):
    t = h.shape[0]
    nblk = t // ATT_BLK
    scale = RET_DK ** -0.5

    def body(q_ref, k_ref, v_ref, g_ref, c_ref, s_ref, lg_ref, o_ref, ya_ref, qs, ks, vs):
        c2_, s2_ = c_ref[...], s_ref[...]
        qs[...] = _rot_a(q_ref[...], c2_, s2_).astype(BF16)
        ks[...] = (_rot_a(k_ref[...], c2_, s2_) * scale).astype(BF16)
        vs[...] = v_ref[...].astype(BF16)
        lg = lg_ref[0, :, 0:1]
        for i in range(nblk):
            qi = qs[pl.ds(i * ATT_BLK, ATT_BLK), :]
            acc = jnp.zeros((ATT_BLK, RET_DV), F32)
            for j in range(i + 1):
                sl = pl.ds(j * ATT_BLK, ATT_BLK)
                s = _nt(qi, ks[sl, :]) * _decay_tile(lg, i - j)
                acc = acc + _nn(s.astype(BF16), vs[sl, :])
            rows = pl.ds(i * ATT_BLK, ATT_BLK)
            o_ref[rows, :] = acc
            r = lax.rsqrt(jnp.mean(acc * acc, -1, keepdims=True) + EPS)
            ya_ref[rows, :] = (acc * r * _silu(g_ref[rows, :])).astype(BF16)

    q, k, v, g, tab, lg = _ret_specs(t)
    out = pl.BlockSpec((t, RET_DV), lambda hh: (0, hh))
    return pl.pallas_call(
        body, grid=(RET_HEADS,), in_specs=[q, k, v, g, tab, tab, lg], out_specs=[out, out],
        out_shape=[jax.ShapeDtypeStruct((t, RET_V_W), F32), jax.ShapeDtypeStruct((t, RET_V_W), BF16)],
        scratch_shapes=[pltpu.VMEM((t, RET_DK), BF16), pltpu.VMEM((t, RET_DK), BF16), pltpu.VMEM((t, RET_DV), BF16)],
        name=name, compiler_params=_cp())(h, h, h, h, c2, s2, lgt)


def _ret_bwd(h, c2, s2, lgt, o, dy, *, name):
    t = h.shape[0]
    nblk = t // ATT_BLK
    scale = RET_DK ** -0.5

    def body(q_ref, k_ref, v_ref, g_ref, c_ref, s_ref, lg_ref, o_ref, dy_ref,
             dq_ref, dk_ref, dv_ref, dg_ref, qs, ks, vs, dos, dka, dva):
        c2_, s2_ = c_ref[...], s_ref[...]
        qs[...] = _rot_a(q_ref[...], c2_, s2_).astype(BF16)
        ks[...] = (_rot_a(k_ref[...], c2_, s2_) * scale).astype(BF16)
        vs[...] = v_ref[...].astype(BF16)
        lg = lg_ref[0, :, 0:1]
        oo = o_ref[...]
        gg = g_ref[...]
        dya = dy_ref[...]
        r = lax.rsqrt(jnp.mean(oo * oo, -1, keepdims=True) + EPS)
        rn = oo * r
        dg_ref[...] = (dya * rn * _dsilu(gg)).astype(BF16)
        drn = dya * _silu(gg)
        dos[...] = (r * (drn - rn * jnp.mean(drn * rn, -1, keepdims=True))).astype(BF16)
        dka[...] = jnp.zeros_like(dka)
        dva[...] = jnp.zeros_like(dva)
        for i in range(nblk):
            rows = pl.ds(i * ATT_BLK, ATT_BLK)
            qi = qs[rows, :]
            doi = dos[rows, :]
            dqa = jnp.zeros((ATT_BLK, RET_DK), F32)
            for j in range(i + 1):
                sl = pl.ds(j * ATT_BLK, ATT_BLK)
                dt_ = _decay_tile(lg, i - j)
                kj = ks[sl, :]
                s = (_nt(qi, kj) * dt_).astype(BF16)
                ds = (_nt(doi, vs[sl, :]) * dt_).astype(BF16)
                dqa = dqa + _nn(ds, kj)
                dka[sl, :] += _tn(ds, qi)
                dva[sl, :] += _tn(s, doi)
            dq_ref[rows, :] = _rot_a_t(dqa, c_ref[rows, :], s_ref[rows, :]).astype(BF16)
        dk_ref[...] = (_rot_a_t(dka[...], c2_, s2_) * scale).astype(BF16)
        dv_ref[...] = dva[...].astype(BF16)

    q, k, v, g, tab, lg = _ret_specs(t)
    blk_v = pl.BlockSpec((t, RET_DV), lambda hh: (0, hh))
    blk_k = pl.BlockSpec((t, RET_DK), lambda hh: (0, hh))
    return pl.pallas_call(
        body, grid=(RET_HEADS,), in_specs=[q, k, v, g, tab, tab, lg, blk_v, blk_v],
        out_specs=[blk_k, blk_k, blk_v, blk_v],
        out_shape=[jax.ShapeDtypeStruct((t, RET_QK_W), BF16), jax.ShapeDtypeStruct((t, RET_QK_W), BF16),
                   jax.ShapeDtypeStruct((t, RET_V_W), BF16), jax.ShapeDtypeStruct((t, RET_V_W), BF16)],
        scratch_shapes=[pltpu.VMEM((t, RET_DK), BF16), pltpu.VMEM((t, RET_DK), BF16), pltpu.VMEM((t, RET_DV), BF16),
                        pltpu.VMEM((t, RET_DV), BF16), pltpu.VMEM((t, RET_DK), F32), pltpu.VMEM((t, RET_DV), F32)],
        name=name, compiler_params=_cp())(h, h, h, h, c2, s2, lgt, o, dy)


def _rot_b(x, cb, shi, slo):
    return x * cb + pltpu.roll(x, ROPE_DIMS // 2, 1) * shi + pltpu.roll(x, LANES - ROPE_DIMS // 2, 1) * slo


def _rot_b_t(dy, cb, shi, slo):
    return dy * cb + pltpu.roll(dy * shi, LANES - ROPE_DIMS // 2, 1) + pltpu.roll(dy * slo, ROPE_DIMS // 2, 1)


def _dil_specs(t):
    base = (2 * RET_QK_W + 2 * RET_V_W) // LANES
    npair = DIL_W // LANES
    q = pl.BlockSpec((t, LANES), lambda p: (0, base + p))
    k = pl.BlockSpec((t, LANES), lambda p: (0, base + npair + p))
    v = pl.BlockSpec((t, LANES), lambda p: (0, base + 2 * npair + p))
    tab = pl.BlockSpec((t, LANES), lambda p: (0, 0))
    strip = pl.BlockSpec((ATT_BLK, t), lambda p: (0, 0))
    pair = pl.BlockSpec((t, LANES), lambda p: (0, p))
    return q, k, v, tab, strip, pair


def _dil_fwd(h, cb, shi, slo, strip, *, name):
    t = h.shape[0]
    nblk = t // ATT_BLK
    scale = DIL_HD ** -0.5

    def body(q_ref, k_ref, v_ref, cb_ref, shi_ref, slo_ref, st_ref, o_ref, yb_ref, lse_ref, qs, ks, vs):
        cb_, shi_, slo_ = cb_ref[...], shi_ref[...], slo_ref[...]
        lane = lax.broadcasted_iota(jnp.int32, (t, LANES), 1)
        qr = _rot_b(q_ref[...], cb_, shi_, slo_) * scale
        qs[0] = jnp.where(lane < DIL_HD, qr, 0.0).astype(BF16)
        qs[1] = jnp.where(lane >= DIL_HD, qr, 0.0).astype(BF16)
        ks[...] = _rot_b(k_ref[...], cb_, shi_, slo_).astype(BF16)
        vs[...] = v_ref[...].astype(BF16)
        lane_b = lax.broadcasted_iota(jnp.int32, (ATT_BLK, LANES), 1)
        for i in range(nblk):
            w = (i + 1) * ATT_BLK
            rows = pl.ds(i * ATT_BLK, ATT_BLK)
            logc = st_ref[:, t - w:t]
            outs, lses = [], []
            for hd in range(2):
                s = _nt(qs[hd, rows, :], ks[0:w, :]) + logc
                m = jnp.max(s, -1, keepdims=True)
                p = jnp.exp(s - m)
                l = jnp.sum(p, -1, keepdims=True)
                outs.append(_nn(p.astype(BF16), vs[0:w, :]) / l)
                lses.append(m + jnp.log(l))
            o = jnp.where(lane_b < DIL_HD, outs[0], outs[1])
            o_ref[rows, :] = o
            yb_ref[rows, :] = o.astype(BF16)
            lse_ref[rows, :] = jnp.where(lane_b < DIL_HD, lses[0], lses[1])

    q, k, v, tab, strip_spec, pair = _dil_specs(t)
    return pl.pallas_call(
        body, grid=(DIL_W // LANES,), in_specs=[q, k, v, tab, tab, tab, strip_spec], out_specs=[pair, pair, pair],
        out_shape=[jax.ShapeDtypeStruct((t, DIL_W), F32), jax.ShapeDtypeStruct((t, DIL_W), BF16),
                   jax.ShapeDtypeStruct((t, DIL_W), F32)],
        scratch_shapes=[pltpu.VMEM((2, t, LANES), BF16), pltpu.VMEM((t, LANES), BF16), pltpu.VMEM((t, LANES), BF16)],
        name=name, compiler_params=_cp())(h, h, h, cb, shi, slo, strip)


def _dil_bwd(h, cb, shi, slo, strip, o, lse, dy, *, name):
    t = h.shape[0]
    nblk = t // ATT_BLK
    scale = DIL_HD ** -0.5

    def body(q_ref, k_ref, v_ref, cb_ref, shi_ref, slo_ref, st_ref, o_ref, lse_ref, dy_ref,
             dq_ref, dk_ref, dv_ref, qs, ks, vs, dos, dls, dka, dva):
        cb_, shi_, slo_ = cb_ref[...], shi_ref[...], slo_ref[...]
        lane = lax.broadcasted_iota(jnp.int32, (t, LANES), 1)
        qr = _rot_b(q_ref[...], cb_, shi_, slo_) * scale
        qs[0] = jnp.where(lane < DIL_HD, qr, 0.0).astype(BF16)
        qs[1] = jnp.where(lane >= DIL_HD, qr, 0.0).astype(BF16)
        ks[...] = _rot_b(k_ref[...], cb_, shi_, slo_).astype(BF16)
        vs[...] = v_ref[...].astype(BF16)
        do = dy_ref[...]
        prod = do * o_ref[...]
        d0 = jnp.sum(jnp.where(lane < DIL_HD, prod, 0.0), -1, keepdims=True)
        d1 = jnp.sum(jnp.where(lane >= DIL_HD, prod, 0.0), -1, keepdims=True)
        dls[...] = jnp.where(lane < DIL_HD, d0, d1)
        dos[0] = jnp.where(lane < DIL_HD, do, 0.0).astype(BF16)
        dos[1] = jnp.where(lane >= DIL_HD, do, 0.0).astype(BF16)
        dka[...] = jnp.zeros_like(dka)
        dva[...] = jnp.zeros_like(dva)
        lane_b = lax.broadcasted_iota(jnp.int32, (ATT_BLK, LANES), 1)
        for i in range(nblk):
            w = (i + 1) * ATT_BLK
            rows = pl.ds(i * ATT_BLK, ATT_BLK)
            logc = st_ref[:, t - w:t]
            dqs = []
            for hd in range(2):
                col = hd * DIL_HD
                qh = qs[hd, rows, :]
                doh = dos[hd, rows, :]
                lse_h = lse_ref[rows, col:col + 1]
                dl_h = dls[rows, col:col + 1]
                p = jnp.exp(_nt(qh, ks[0:w, :]) + logc - lse_h)
                dp = _nt(doh, vs[0:w, :])
                ds = (p * (dp - dl_h)).astype(BF16)
                dqs.append(_nn(ds, ks[0:w, :]))
                dka[0:w, :] += _tn(ds, qh)
                dva[0:w, :] += _tn(p.astype(BF16), doh)
            dq = jnp.where(lane_b < DIL_HD, dqs[0], dqs[1]) * scale
            dq_ref[rows, :] = _rot_b_t(dq, cb_ref[rows, :], shi_ref[rows, :], slo_ref[rows, :]).astype(BF16)
        dk_ref[...] = _rot_b_t(dka[...], cb_, shi_, slo_).astype(BF16)
        dv_ref[...] = dva[...].astype(BF16)

    q, k, v, tab, strip_spec, pair = _dil_specs(t)
    dy_spec = pl.BlockSpec((t, LANES), lambda p: (0, RET_V_W // LANES + p))
    return pl.pallas_call(
        body, grid=(DIL_W // LANES,), in_specs=[q, k, v, tab, tab, tab, strip_spec, pair, pair, dy_spec],
        out_specs=[pair, pair, pair],
        out_shape=[jax.ShapeDtypeStruct((t, DIL_W), BF16)] * 3,
        scratch_shapes=[pltpu.VMEM((2, t, LANES), BF16), pltpu.VMEM((t, LANES), BF16), pltpu.VMEM((t, LANES), BF16),
                        pltpu.VMEM((2, t, LANES), BF16), pltpu.VMEM((t, LANES), F32),
                        pltpu.VMEM((t, LANES), F32), pltpu.VMEM((t, LANES), F32)],
        name=name, compiler_params=_cp())(h, h, h, cb, shi, slo, strip, o, lse, dy)


def _gdn_prep_fwd(h, cw, *, name):
    t = h.shape[0]
    qscale = GDN_DK ** -0.5

    def body(hq_ref, hk_ref, hv_ref, wq_ref, wk_ref, wv_ref, q_ref, k_ref, v_ref):
        row = lax.broadcasted_iota(jnp.int32, (t, GDN_DK), 0)
        sq = _silu(_dwconv(hq_ref[...], wq_ref, row))
        sk = _silu(_dwconv(hk_ref[...], wk_ref, row))
        q_ref[0] = sq * lax.rsqrt(jnp.sum(sq * sq, -1, keepdims=True) + 1e-6) * qscale
        k_ref[0] = sk * lax.rsqrt(jnp.sum(sk * sk, -1, keepdims=True) + 1e-6)
        v_ref[0] = _silu(_dwconv(hv_ref[...], wv_ref, row))

    hs = lambda off: pl.BlockSpec((t, GDN_DK), lambda i: (0, i + off))
    ws = lambda off: pl.BlockSpec((GDN_CONV, GDN_DK), lambda i: (0, i + off))
    out = pl.BlockSpec((1, t, GDN_DK), lambda i: (i, 0, 0))
    return pl.pallas_call(
        body, grid=(GDN_HEADS,), in_specs=[hs(0), hs(8), hs(16), ws(0), ws(8), ws(16)], out_specs=[out, out, out],
        out_shape=[jax.ShapeDtypeStruct((GDN_HEADS, t, GDN_DK), F32)] * 3,
        name=name, compiler_params=_cp())(h, h, h, cw, cw, cw)


def _gdn_prep_bwd(h, cw, dq, dk, dv, *, name):
    t = h.shape[0]
    qscale = GDN_DK ** -0.5

    def body(hq_ref, hk_ref, hv_ref, wq_ref, wk_ref, wv_ref, dq_ref, dk_ref, dv_ref,
             dhq_ref, dhk_ref, dhv_ref, dwq_ref, dwk_ref, dwv_ref):
        row = lax.broadcasted_iota(jnp.int32, (t, GDN_DK), 0)

        def one(h_ref, w_ref, d_ref, dh_ref, dw_ref, norm, sc):
            u = h_ref[...]
            c = _dwconv(u, w_ref, row)
            d = d_ref[0]
            if norm:
                s = _silu(c)
                r = lax.rsqrt(jnp.sum(s * s, -1, keepdims=True) + 1e-6)
                n = s * r
                d = d * sc
                d = r * (d - n * jnp.sum(d * n, -1, keepdims=True))
            dc = d * _dsilu(c)
            dh_ref[...] = _dwconv_bwd(u, w_ref, dc, row, dw_ref).astype(BF16)

        one(hq_ref, wq_ref, dq_ref, dhq_ref, dwq_ref, True, qscale)
        one(hk_ref, wk_ref, dk_ref, dhk_ref, dwk_ref, True, 1.0)
        one(hv_ref, wv_ref, dv_ref, dhv_ref, dwv_ref, False, 1.0)

    hs = lambda off: pl.BlockSpec((t, GDN_DK), lambda i: (0, i + off))
    ws = lambda off: pl.BlockSpec((GDN_CONV, GDN_DK), lambda i: (0, i + off))
    hd = pl.BlockSpec((1, t, GDN_DK), lambda i: (i, 0, 0))
    return pl.pallas_call(
        body, grid=(GDN_HEADS,), in_specs=[hs(0), hs(8), hs(16), ws(0), ws(8), ws(16), hd, hd, hd],
        out_specs=[hs(0), hs(0), hs(0), ws(0), ws(0), ws(0)],
        out_shape=[jax.ShapeDtypeStruct((t, GDN_W), BF16)] * 3 + [jax.ShapeDtypeStruct((GDN_CONV, GDN_W), F32)] * 3,
        name=name, compiler_params=_cp())(h, h, h, cw, cw, cw, dq, dk, dv)


def _make_mm2(wide):
    def raw(a, b, dims):
        if wide:
            return lax.dot_general(a, b, (dims, ((), ())), precision=lax.Precision.HIGHEST, preferred_element_type=F32)
        return lax.dot_general(a.astype(BF16), b.astype(BF16), (dims, ((), ())), preferred_element_type=F32)

    @jax.custom_vjp
    def nn(a, b):
        return raw(a, b, ((1,), (0,)))

    @jax.custom_vjp
    def nt(a, b):
        return raw(a, b, ((1,), (1,)))

    @jax.custom_vjp
    def tn(a, b):
        return raw(a, b, ((0,), (0,)))

    nn.defvjp(lambda a, b: (nn(a, b), (a, b)), lambda r, g: (nt(g, r[1]), tn(r[0], g)))
    nt.defvjp(lambda a, b: (nt(a, b), (a, b)), lambda r, g: (nn(g, r[1]), tn(g, r[0])))
    tn.defvjp(lambda a, b: (tn(a, b), (a, b)), lambda r, g: (nt(r[1], g), nn(r[0], g)))
    return nn, nt, tn


_NN, _NT, _TN = _make_mm2(False)
_NNW, _NTW, _TNW = _make_mm2(True)


def _square_masks(c):
    ri = lax.broadcasted_iota(jnp.int32, (c, c), 0)
    ci = lax.broadcasted_iota(jnp.int32, (c, c), 1)
    return ri >= ci, ri > ci, ri == ci


def _cumsum_rows(m):
    tri, _, _ = _square_masks(m.shape[0])
    return _NNW(tri.astype(F32), m)


def _transpose_sq(m):
    _, _, eye = _square_masks(m.shape[0])
    return _NTW(eye.astype(F32), m)


@jax.custom_vjp
def _inv_unit_lower(l):
    c = l.shape[0]
    _, _, eye = _square_masks(c)
    p = -l
    t = eye.astype(F32) + p
    for _ in range(int(math.log2(c)) - 1):
        p = _NNW(p, p)
        t = t + _NNW(t, p)
    return t


def _inv_fwd(l):
    t = _inv_unit_lower(l)
    return t, t


def _inv_bwd(t, dt):
    return (-_NTW(_TNW(t, dt), t),)


_inv_unit_lower.defvjp(_inv_fwd, _inv_bwd)


@jax.custom_vjp
def _inv_known(l, t):
    return t


_inv_known.defvjp(lambda l, t: (t, t), lambda t, dt: (_inv_bwd(t, dt)[0], jnp.zeros_like(t)))


def _softplus(x):
    return jnp.maximum(x, 0.0) + jnp.log1p(jnp.exp(-jnp.abs(x)))


def _gdn_chunk(q, k, v, braw, araw, alog, dtb, state, inv=None):
    c = q.shape[0]
    dv = v.shape[1]
    tri, strict, _ = _square_masks(c)
    beta = _sig(braw)
    g = -jnp.exp(alog) * _softplus(araw + dtb)
    gcm = _cumsum_rows(g * jnp.ones((c, c), F32))
    gct = _transpose_sq(gcm)
    decay = jnp.where(tri, jnp.exp(jnp.where(tri, gcm - gct, 0.0)), 0.0)
    gc = jnp.sum(gcm, 1, keepdims=True) * (1.0 / c)
    glast = jnp.sum(g, 0, keepdims=True)
    egc = jnp.exp(gc)
    kb = k * beta
    low = jnp.where(strict, _NT(kb, k) * decay, 0.0)
    tm = _inv_unit_lower(low) if inv is None else _inv_known(low, inv)
    sol = _NNW(tm, jnp.concatenate([v * beta, kb * egc], 1))
    u, w = sol[:, :dv], sol[:, dv:]
    attn = jnp.where(tri, _NT(q, k) * decay, 0.0)
    k_dec = k * jnp.exp(glast - gc)
    q_dec = q * egc
    v_new = u - _NN(w, state)
    o = _NN(q_dec, state) + _NN(attn, v_new)
    new_state = state * jnp.exp(glast) + _TN(k_dec, v_new)
    return o, new_state, tm


def _gdn_specs(t, rev):
    nch = t // GDN_CHUNK
    cm = (lambda n: nch - 1 - n) if rev else (lambda n: n)
    tok = pl.BlockSpec((GDN_HEADS, GDN_CHUNK, GDN_DK), lambda n: (0, cm(n), 0))
    par = pl.BlockSpec((GDN_HEADS, 1, LANES), lambda n: (0, 0, 0))
    st = pl.BlockSpec((GDN_HEADS, 1, GDN_DK, GDN_DV), lambda n: (0, cm(n), 0, 0))
    inv = pl.BlockSpec((GDN_HEADS, GDN_CHUNK, GDN_CHUNK), lambda n: (0, cm(n), 0))
    return tok, par, st, inv


def _gdn_core_fwd(q, k, v, bb, ab, alog, dtb, *, name):
    t = q.shape[1]
    nch = t // GDN_CHUNK

    def body(q_ref, k_ref, v_ref, bb_ref, ab_ref, al_ref, dt_ref, o_ref, st_ref, inv_ref, state):
        @pl.when(pl.program_id(0) == 0)
        def _():
            state[...] = jnp.zeros_like(state)

        s0 = state[...]
        st_ref[:, 0] = s0
        o, s1, tm = jax.vmap(_gdn_chunk)(q_ref[...], k_ref[...], v_ref[...], bb_ref[:, :, 0:1], ab_ref[:, :, 0:1],
                                         al_ref[:, :, 0:1], dt_ref[:, :, 0:1], s0)
        o_ref[...] = o
        inv_ref[...] = tm
        state[...] = s1

    tok, par, st, inv = _gdn_specs(t, False)
    return pl.pallas_call(
        body, grid=(nch,), in_specs=[tok, tok, tok, tok, tok, par, par], out_specs=[tok, st, inv],
        out_shape=[jax.ShapeDtypeStruct((GDN_HEADS, t, GDN_DV), F32),
                   jax.ShapeDtypeStruct((GDN_HEADS, nch, GDN_DK, GDN_DV), F32),
                   jax.ShapeDtypeStruct((GDN_HEADS, t, GDN_CHUNK), F32)],
        scratch_shapes=[pltpu.VMEM((GDN_HEADS, GDN_DK, GDN_DV), F32)],
        name=name, compiler_params=_cp())(q, k, v, bb, ab, alog, dtb)


def _gdn_core_bwd(q, k, v, bb, ab, alog, dtb, states, invs, do, *, name):
    t = q.shape[1]
    nch = t // GDN_CHUNK

    def body(q_ref, k_ref, v_ref, bb_ref, ab_ref, al_ref, dt_ref, st_ref, inv_ref, do_ref,
             dq_ref, dk_ref, dv_ref, dbb_ref, dab_ref, dal_ref, ddt_ref, dstate):
        @pl.when(pl.program_id(0) == 0)
        def _():
            dstate[...] = jnp.zeros_like(dstate)
            dal_ref[...] = jnp.zeros_like(dal_ref)
            ddt_ref[...] = jnp.zeros_like(ddt_ref)

        args = (q_ref[...], k_ref[...], v_ref[...], bb_ref[:, :, 0:1], ab_ref[:, :, 0:1],
                al_ref[:, :, 0:1], dt_ref[:, :, 0:1], st_ref[:, 0])
        tm = inv_ref[...]

        def chunk(*a):
            return jax.vmap(_gdn_chunk)(*a, tm)[:2]

        _, pull = jax.vjp(chunk, *args)
        dq, dk, dv, dbr, dar, dal, ddt, ds = pull((do_ref[...], dstate[...]))
        dq_ref[...] = dq
        dk_ref[...] = dk
        dv_ref[...] = dv
        dbb_ref[...] = dbr + jnp.zeros((GDN_HEADS, GDN_CHUNK, LANES), F32)
        dab_ref[...] = dar + jnp.zeros((GDN_HEADS, GDN_CHUNK, LANES), F32)
        dal_ref[...] += dal + jnp.zeros((GDN_HEADS, 1, LANES), F32)
        ddt_ref[...] += ddt + jnp.zeros((GDN_HEADS, 1, LANES), F32)
        dstate[...] = ds

    tok, par, st, inv = _gdn_specs(t, True)
    tokshape = jax.ShapeDtypeStruct((GDN_HEADS, t, GDN_DK), F32)
    parshape = jax.ShapeDtypeStruct((GDN_HEADS, 1, LANES), F32)
    return pl.pallas_call(
        body, grid=(nch,), in_specs=[tok, tok, tok, tok, tok, par, par, st, inv, tok],
        out_specs=[tok, tok, tok, tok, tok, par, par],
        out_shape=[tokshape] * 5 + [parshape] * 2,
        scratch_shapes=[pltpu.VMEM((GDN_HEADS, GDN_DK, GDN_DV), F32)],
        name=name, compiler_params=_cp())(q, k, v, bb, ab, alog, dtb, states, invs, do)


GDN_ROWS = 512


def _gdn_post_fwd(o, h, nw, *, name):
    t = o.shape[1]

    def body(o_ref, g_ref, nw_ref, y_ref):
        oo = o_ref[0]
        r = lax.rsqrt(jnp.mean(oo * oo, -1, keepdims=True) + EPS)
        y_ref[...] = (oo * r * nw_ref[...] * _silu(g_ref[...])).astype(BF16)

    return pl.pallas_call(
        body, grid=(GDN_HEADS, t // GDN_ROWS),
        in_specs=[pl.BlockSpec((1, GDN_ROWS, GDN_DV), lambda hh, i: (hh, i, 0)),
                  pl.BlockSpec((GDN_ROWS, GDN_DV), lambda hh, i: (i, 3 * GDN_HEADS + hh)),
                  pl.BlockSpec((1, GDN_DV), lambda hh, i: (0, 0))],
        out_specs=pl.BlockSpec((GDN_ROWS, GDN_DV), lambda hh, i: (i, hh)),
        out_shape=jax.ShapeDtypeStruct((t, GDN_W), BF16), name=name, compiler_params=_cp())(o, h, nw)


def _gdn_post_bwd(o, h, nw, dy, *, name):
    t = o.shape[1]

    def body(o_ref, g_ref, nw_ref, dy_ref, do_ref, dg_ref, dnw_ref):
        oo, gg, nw_, dy_ = o_ref[0], g_ref[...], nw_ref[...], dy_ref[...]
        r = lax.rsqrt(jnp.mean(oo * oo, -1, keepdims=True) + EPS)
        n = oo * r
        sg = _silu(gg)
        dg_ref[...] = (dy_ * n * nw_ * _dsilu(gg)).astype(BF16)
        dn = dy_ * sg * nw_
        do_ref[0] = r * (dn - n * jnp.mean(dn * n, -1, keepdims=True))

        @pl.when((pl.program_id(0) == 0) & (pl.program_id(1) == 0))
        def _():
            dnw_ref[...] = jnp.zeros_like(dnw_ref)

        dnw_ref[...] += jnp.sum(dy_ * sg * n, 0, keepdims=True)

    return pl.pallas_call(
        body, grid=(GDN_HEADS, t // GDN_ROWS),
        in_specs=[pl.BlockSpec((1, GDN_ROWS, GDN_DV), lambda hh, i: (hh, i, 0)),
                  pl.BlockSpec((GDN_ROWS, GDN_DV), lambda hh, i: (i, 3 * GDN_HEADS + hh)),
                  pl.BlockSpec((1, GDN_DV), lambda hh, i: (0, 0)),
                  pl.BlockSpec((GDN_ROWS, GDN_DV), lambda hh, i: (i, hh))],
        out_specs=[pl.BlockSpec((1, GDN_ROWS, GDN_DV), lambda hh, i: (hh, i, 0)),
                   pl.BlockSpec((GDN_ROWS, GDN_DV), lambda hh, i: (i, hh)),
                   pl.BlockSpec((1, GDN_DV), lambda hh, i: (0, 0))],
        out_shape=[jax.ShapeDtypeStruct((GDN_HEADS, t, GDN_DV), F32), jax.ShapeDtypeStruct((t, GDN_W), BF16),
                   jax.ShapeDtypeStruct((1, GDN_DV), F32)],
        name=name, compiler_params=_cp())(o, h, nw, dy)


def _tables(positions):
    pos = positions.astype(F32)[:, None]
    half = RET_DK // 2
    inv = jnp.power(RET_THETA, -jnp.arange(half, dtype=F32) * 2.0 / RET_DK)
    ang = pos * inv
    cos, sin = jnp.cos(ang), jnp.sin(ang)
    c2a = jnp.concatenate([cos, cos], 1)
    s2a = jnp.concatenate([-sin, sin], 1)
    hb = ROPE_DIMS // 2
    invb = jnp.power(ROPE_THETA, -jnp.arange(hb, dtype=F32) * 2.0 / ROPE_DIMS)
    angb = pos * invb
    cosb, sinb = jnp.cos(angb), jnp.sin(angb)
    t = pos.shape[0]
    ones = jnp.ones((t, DIL_HD - ROPE_DIMS), F32)
    zeros = jnp.zeros((t, DIL_HD - ROPE_DIMS), F32)
    z8 = jnp.zeros((t, hb), F32)
    cb = jnp.concatenate([cosb, cosb, ones] * 2, 1)
    shi = jnp.concatenate([z8, sinb, zeros] * 2, 1)
    slo = jnp.concatenate([-sinb, z8, zeros] * 2, 1)
    lg = jnp.log1p(-jnp.power(2.0, -5.0 - jnp.arange(RET_HEADS, dtype=F32)))
    lgt = jnp.broadcast_to(lg[:, None, None], (RET_HEADS, 1, LANES))
    delta = jnp.arange(ATT_BLK, dtype=jnp.int32)[:, None] + (SEQ - ATT_BLK) - jnp.arange(SEQ, dtype=jnp.int32)[None, :]
    cnt = jnp.zeros(delta.shape, F32)
    for (w, d) in DIL_PAIRS:
        cnt = cnt + ((delta >= 0) & (delta <= w) & (delta % d == 0)).astype(F32)
    strip = jnp.where(cnt > 0, jnp.log(jnp.maximum(cnt, 1.0)), NEG)
    return c2a, s2a, cb, shi, slo, lgt, strip


def _local_step(x, positions, target, get_w, mid, put_g, small):
    c2a, s2a, cb, shi, slo, lgt, strip = _tables(positions)
    t = x.shape[0]
    saved = []
    xf = x
    xb = x.astype(BF16)
    for layer in range(DEPTH):
        j = layer // 2
        L = f"L{layer}_"
        W, dep = get_w(layer, "mixer", xb)
        rec = {"x": xf, "xb": xb}
        if layer % 2 == 0:
            h = _mm(xb, W["in_t"], tb=True, name=L + "ev_in", dep=dep)
            ro, ya = _ret_fwd(h, c2a, s2a, lgt, name=L + "ret_fwd")
            do_, yb, lse = _dil_fwd(h, cb, shi, slo, strip, name=L + "dil_fwd")
            y = jnp.concatenate([ya, yb], 1)
            rec.update(h=h, ro=ro, dil_o=do_, lse=lse, y=y)
        else:
            h = _mm(xb, W["in_t"], tb=True, name=L + "od_in", dep=dep)
            cw = W["conv"]
            q, k, v = _gdn_prep_fwd(h, cw, name=L + "gdn_prep")
            hs = h[:, 4 * GDN_W:4 * GDN_W + 2 * GDN_HEADS]
            bb = jnp.broadcast_to(hs[:, :GDN_HEADS].T[:, :, None], (GDN_HEADS, t, LANES))
            ab = jnp.broadcast_to(hs[:, GDN_HEADS:].T[:, :, None], (GDN_HEADS, t, LANES))
            alog = jnp.broadcast_to(small["od_a_log"][j][:, None, None], (GDN_HEADS, 1, LANES))
            dtb = jnp.broadcast_to(small["od_dt_bias"][j][:, None, None], (GDN_HEADS, 1, LANES))
            o, states, invs = _gdn_core_fwd(q, k, v, bb, ab, alog, dtb, name=L + "gdn_fwd")
            nw = small["od_norm_w"][j][None, :]
            y = _gdn_post_fwd(o, h, nw, name=L + "gdn_post")
            rec.update(h=h, q=q, k=k, v=v, bb=bb, ab=ab, alog=alog, dtb=dtb, states=states, invs=invs, o=o, y=y, nw=nw, cw=cw)
        z1, x1, x1b = _mm_ln_fwd(y, W["out"], xf, small["ln1_g"][layer][None], small["ln1_b"][layer][None],
                                 name=L + "out_ln1", dep=mid(layer, "mixer", y))
        rec["Wm"] = W
        W, dep = get_w(layer, "ffn", x1b)
        rec["Wf"] = W
        fcw = W["fconv"]
        fcb = small["ffn_conv_b"][layer][None]
        ug, uv, a = _ffn_up_mid(x1b, W["up_t"], fcw, fcb, name=L + "ffn_up_mid", dep=dep)
        z2, x2, x2b = _mm_ln_fwd(a, W["down"], x1, small["ln2_g"][layer][None], small["ln2_b"][layer][None],
                                 name=L + "down_ln2", dep=mid(layer, "ffn", a))
        rec.update(z1=z1, x1b=x1b, ug=ug, uv=uv, a=a, z2=z2, fcw=fcw, fcb=fcb)
        saved.append(rec)
        xf, xb = x2, x2b

    dy, lossv = _loss_head(xf, target, name="loss_head")
    loss = lossv[0, 0]

    gS = {n: [None] * small[n].shape[0] for n in small}
    below = None
    for layer in reversed(range(DEPTH)):
        j = layer // 2
        L = f"L{layer}_"
        rec = saved[layer]
        Wm, Wf = rec["Wm"], rec["Wf"]
        g = {}
        if below is None:
            dz2, dz2b, dg2, db2 = _ln_bwd(rec["z2"], small["ln2_g"][layer][None], dy, None, name=L + "ln2_bwd")
        else:
            dz2, dz2b, dg2, db2 = _mm_ln_bwd(below[0], below[1], rec["z2"], small["ln2_g"][layer][None], below[2],
                                             name=L + "ln2_bwd", dep=below[3])
        gS["ln2_g"][layer], gS["ln2_b"][layer] = dg2[0], db2[0]
        g["down"] = _mm(rec["a"], dz2b, ta=True, name=L + "ffn_down_dw", out_dtype=BF16)
        du, dcw, dcb = _ffn_mid_bwd(rec["ug"], rec["uv"], rec["fcw"], rec["fcb"], dz2b, Wf["down"], name=L + "ffn_mid_bwd")
        g["fconv"] = dcw.astype(BF16)
        gS["ffn_conv_b"][layer] = dcb[0]
        g["up_t"] = _mm(du, rec["x1b"], ta=True, name=L + "ffn_up_dw", out_dtype=BF16)
        dep = put_g(layer, "ffn", g)
        dz1, dz1b, dg1, db1 = _mm_ln_bwd(du, Wf["up_t"], rec["z1"], small["ln1_g"][layer][None], dz2,
                                         name=L + "ln1_bwd", dep=dep)
        gS["ln1_g"][layer], gS["ln1_b"][layer] = dg1[0], db1[0]
        g = {}
        if layer % 2 == 0:
            g["out"] = _mm(rec["y"], dz1b, ta=True, name=L + "ev_out_dw", out_dtype=BF16)
            dyy = _mm(dz1b, Wm["out"], tb=True, name=L + "ev_out_dx")
            dqa, dka, dva, dga = _ret_bwd(rec["h"], c2a, s2a, lgt, rec["ro"], dyy, name=L + "ret_bwd")
            dqb, dkb, dvb = _dil_bwd(rec["h"], cb, shi, slo, strip, rec["dil_o"], rec["lse"], dyy, name=L + "dil_bwd")
            dh = jnp.concatenate([dqa, dka, dva, dga, dqb, dkb, dvb], 1)
            g["in_t"] = _mm(dh, rec["xb"], ta=True, name=L + "ev_in_dw", out_dtype=BF16)
            dep = put_g(layer, "mixer", g)
        else:
            g["out"] = _mm(rec["y"], dz1b, ta=True, name=L + "od_out_dw", out_dtype=BF16)
            dyy = _mm(dz1b, Wm["out"], tb=True, name=L + "od_out_dx")
            do, dgate, dnw = _gdn_post_bwd(rec["o"], rec["h"], rec["nw"], dyy, name=L + "gdn_post_bwd")
            gS["od_norm_w"][j] = dnw[0]
            dq, dk, dv, dbb, dab, dal, ddt = _gdn_core_bwd(
                rec["q"], rec["k"], rec["v"], rec["bb"], rec["ab"], rec["alog"], rec["dtb"], rec["states"], rec["invs"], do,
                name=L + "gdn_bwd")
            gS["od_a_log"][j] = dal[:, 0, 0]
            gS["od_dt_bias"][j] = ddt[:, 0, 0]
            dhq, dhk, dhv, dwq, dwk, dwv = _gdn_prep_bwd(rec["h"], rec["cw"], dq, dk, dv, name=L + "gdn_prep_bwd")
            g["conv"] = jnp.concatenate([dwq, dwk, dwv], 1).astype(BF16)
            dsm = jnp.concatenate([dbb[:, :, 0].T, dab[:, :, 0].T,
                                   jnp.zeros((t, LANES - 2 * GDN_HEADS), F32)], 1).astype(BF16)
            dh = jnp.concatenate([dhq, dhk, dhv, dgate, dsm], 1)
            g["in_t"] = _mm(dh, rec["xb"], ta=True, name=L + "od_in_dw", out_dtype=BF16)
            dep = put_g(layer, "mixer", g)
        below = (dh, Wm["in_t"], dz1, dep)
    grad_x = _axpy(_mm(below[0], below[1], name="L0_in_dx", dep=below[3]), below[2], name="grad_x")
    gS = {n: jnp.stack(v) for n, v in gS.items()}
    return loss, grad_x, gS


HBM = pl.BlockSpec(memory_space=pltpu.HBM)


def _me():
    return lax.axis_index("x"), lax.axis_index("y"), lax.axis_index("c")


def _my_index():
    x, y, c = _me()
    return 4 * x + 2 * y + c


SEM = pl.BlockSpec(memory_space=pltpu.SEMAPHORE)
ANY = pl.BlockSpec(memory_space=pl.ANY)
PLANS = {"scatter": (1, 2, 3, 4, 5, 6, 7), "spread": (1, 2, 4, 6), "relay": (2, 4, 6)}
SIBLING = 1


def _peer(kk):
    x, y, c = _me()
    return x ^ (kk >> 2), y ^ ((kk >> 1) & 1), c ^ (kk & 1)


def _peer_index(kk):
    px, py, pc = _peer(kk)
    return 4 * px + 2 * py + pc


def _job_copies(mode, srcs, lands, send_sems, recv_sems, incoming):
    myid = _my_index()
    plan = PLANS[mode]
    out = []
    for a in range(len(lands)):
        for idx, kk in enumerate(plan):
            if mode == "relay":
                to, src = _peer(SIBLING), lands[a].at[_peer_index(kk)]
                slot_there, slot_here = _peer_index(kk), _peer_index(kk ^ SIBLING)
            else:
                to, src = _peer(kk), (srcs[a] if mode == "spread" else srcs[a].at[_peer_index(kk)])
                slot_there, slot_here = myid, _peer_index(kk)
            sem = a * len(plan) + idx
            out.append(pltpu.make_async_remote_copy(
                src_ref=src, dst_ref=lands[a].at[slot_here if incoming else slot_there],
                send_sem=send_sems.at[sem], recv_sem=recv_sems.at[sem], device_id=to, device_id_type=MESH))
    return out


def _split_jobs(jobs, arrays):
    out, o = [], 0
    for (_, srcs, lands) in jobs:
        out.append((arrays[o:o + len(srcs)], arrays[o + len(srcs):o + len(srcs) + len(lands)]))
        o += len(srcs) + len(lands)
    return out


def _exchange_start(jobs, after, *, name):
    jobs = [(mode, list(srcs), [lax.empty((N_DEV, *s.shape) if mode == "spread" else s.shape, s.dtype) for s in srcs]
             if lands is None else list(lands)) for (mode, srcs, lands) in jobs]
    flat = [a for (_, srcs, lands) in jobs for a in (*srcs, *lands)]
    n, nj = len(flat), len(jobs)
    nsem = [len(PLANS[mode]) * len(lands) for (mode, _, lands) in jobs]

    def body(*refs):
        o = n + (0 if after is None else 1)
        sems, token = refs[o:o + 2 * nj], refs[o + 2 * nj + n]
        for ji, ((mode, _, _), (src, land)) in enumerate(zip(jobs, _split_jobs(jobs, refs[:n]))):
            for cp in _job_copies(mode, src, land, sems[2 * ji], sems[2 * ji + 1], False):
                cp.start()
        token[...] = jnp.zeros_like(token)

    outs = pl.pallas_call(
        body, name=name,
        out_shape=(*[pltpu.SemaphoreType.DMA((ns,)) for ns in nsem for _ in range(2)],
                   *[pltpu.HBM(a.shape, a.dtype) for a in flat], jax.ShapeDtypeStruct((8, LANES), F32)),
        in_specs=[HBM] * n + ([] if after is None else [ANY]),
        out_specs=(*[SEM] * (2 * nj), *[HBM] * n, pl.BlockSpec(memory_space=pltpu.VMEM)),
        input_output_aliases={i: 2 * nj + i for i in range(n)},
        compiler_params=pltpu.CompilerParams(has_side_effects=pltpu.SideEffectType.DATAFLOW_SIDE_EFFECTING),
    )(*[pltpu.with_memory_space_constraint(a, pltpu.HBM) for a in flat], *([] if after is None else [after]))
    thru = _split_jobs(jobs, list(outs[2 * nj:2 * nj + n]))
    started = [(mode, outs[2 * ji], outs[2 * ji + 1], src, land) for ji, ((mode, _, _), (src, land)) in enumerate(zip(jobs, thru))]
    return started, outs[2 * nj + n]


def _exchange_wait(started, after, *, name):
    jobs = [(mode, srcs, lands) for (mode, _, _, srcs, lands) in started]
    flat = [a for (_, srcs, lands) in jobs for a in (*srcs, *lands)]
    n, nj = len(flat), len(jobs)

    def body(*refs):
        sems = refs[n:n + 2 * nj]
        for ji, ((mode, _, _), (src, land)) in enumerate(zip(jobs, _split_jobs(jobs, refs[:n]))):
            for cp in _job_copies(mode, src, land, sems[2 * ji], sems[2 * ji + 1], True):
                cp.wait_send()
                cp.wait_recv()

    outs = pl.pallas_call(
        body, name=name, out_shape=tuple(pltpu.HBM(a.shape, a.dtype) for a in flat),
        in_specs=[HBM] * n + [SEM] * (2 * nj) + [ANY], out_specs=tuple([HBM] * n),
        input_output_aliases={i: i for i in range(n)},
        compiler_params=pltpu.CompilerParams(has_side_effects=pltpu.SideEffectType.DATAFLOW_SIDE_EFFECTING),
    )(*flat, *[s for (_, ss, rs, _, _) in started for s in (ss, rs)], after)
    return _split_jobs(jobs, list(outs))


def _sum8(land, *, name):
    _, rr, cc = land.shape
    tr = _row_tile(rr)

    def body(l_ref, o_ref):
        acc = l_ref[0].astype(F32)
        for d in range(1, N_DEV):
            acc = acc + l_ref[d].astype(F32)
        o_ref[...] = acc

    return pl.pallas_call(
        body, grid=(rr // tr,), in_specs=[pl.BlockSpec((N_DEV, tr, cc), lambda i: (0, i, 0))],
        out_specs=pl.BlockSpec((tr, cc), lambda i: (i, 0)), out_shape=jax.ShapeDtypeStruct((rr, cc), F32),
        name=name, compiler_params=_cp())(land)


def _row_tile(rr):
    for cand in (512, 384, 256, 192, 176, 128, 64, 32, 16, 8):
        if rr % cand == 0:
            return cand
    return rr


def _small_exchange(vec, *, name):
    rr = vec.shape[0]

    def body(v_ref, o_ref, send_sems, recv_sems):
        x, y, c = _me()
        myid = 4 * x + 2 * y + c
        o_ref[myid] = v_ref[...]
        cps = []
        for kk in range(1, N_DEV):
            px, py, pc = x ^ (kk >> 2), y ^ ((kk >> 1) & 1), c ^ (kk & 1)
            cps.append(pltpu.make_async_remote_copy(
                src_ref=v_ref, dst_ref=o_ref.at[myid], send_sem=send_sems.at[kk], recv_sem=recv_sems.at[kk],
                device_id=(px, py, pc), device_id_type=MESH))
        for cp in cps:
            cp.start()
        for kk in range(1, N_DEV):
            px, py, pc = x ^ (kk >> 2), y ^ ((kk >> 1) & 1), c ^ (kk & 1)
            pltpu.make_async_remote_copy(
                src_ref=v_ref, dst_ref=o_ref.at[4 * px + 2 * py + pc], send_sem=send_sems.at[kk],
                recv_sem=recv_sems.at[kk], device_id=(px, py, pc), device_id_type=MESH).wait_recv()
        for cp in cps:
            cp.wait_send()

    return pl.pallas_call(
        body, in_specs=[pl.BlockSpec(memory_space=pltpu.VMEM)], out_specs=pl.BlockSpec(memory_space=pltpu.VMEM),
        out_shape=jax.ShapeDtypeStruct((N_DEV, rr, LANES), F32),
        scratch_shapes=[pltpu.SemaphoreType.DMA((N_DEV,)), pltpu.SemaphoreType.DMA((N_DEV,))],
        name=name, compiler_params=pltpu.CompilerParams(has_side_effects=True))(vec)


def _adam_math(w, g, m, v):
    m = ADAM_B1 * m + (1.0 - ADAM_B1) * g
    v = ADAM_B2 * v + (1.0 - ADAM_B2) * (g * g)
    m_hat = m / (1.0 - ADAM_B1 ** ADAM_STEP)
    v_hat = v / (1.0 - ADAM_B2 ** ADAM_STEP)
    delta = -ADAM_LR * (m_hat / (jnp.sqrt(v_hat) + ADAM_EPS) + ADAM_WD * w)
    return delta, m, v


def _adamw_sharded(w, m, v, g, *, name):
    ll, rr, cc = w.shape
    tr = _row_tile(rr)

    def body(w_ref, m_ref, v_ref, g_ref, d_ref, nm_ref, nv_ref):
        d, nm, nv = _adam_math(w_ref[...], g_ref[...], m_ref[...], v_ref[...])
        d_ref[...] = d
        nm_ref[...] = nm
        nv_ref[...] = nv

    blk = pl.BlockSpec((1, tr, cc), lambda l, i: (l, i, 0))
    sh = jax.ShapeDtypeStruct((ll, rr, cc), F32)
    return pl.pallas_call(
        body, grid=(ll, rr // tr), in_specs=[blk] * 4, out_specs=[blk] * 3, out_shape=[sh] * 3,
        name=name, compiler_params=_cp())(w, m, v, g)


def _adamw_small(w, m, v, gall, *, name):
    rr = w.shape[0]

    def body(w_ref, m_ref, v_ref, g_ref, go_ref, d_ref, nm_ref, nv_ref):
        g = g_ref[0]
        for kk in range(1, N_DEV):
            g = g + g_ref[kk]
        d, nm, nv = _adam_math(w_ref[...], g, m_ref[...], v_ref[...])
        go_ref[...] = g
        d_ref[...] = d
        nm_ref[...] = nm
        nv_ref[...] = nv

    sh = jax.ShapeDtypeStruct((rr, LANES), F32)
    return pl.pallas_call(body, out_shape=[sh] * 4, name=name, compiler_params=_cp())(w, m, v, gall)


SHARDED = ("ev_w_in", "ev_w_out", "od_w_in", "od_conv_w", "od_w_out", "ffn_w_up", "ffn_conv_w", "ffn_w_down")
SMALL = ("od_a_log", "od_dt_bias", "od_norm_w", "ffn_conv_b", "ln1_g", "ln1_b", "ln2_g", "ln2_b")
ALL_W = ("ev_w_in", "ev_w_out", "od_w_in", "od_conv_w", "od_a_log", "od_dt_bias", "od_norm_w", "od_w_out",
         "ffn_w_up", "ffn_conv_w", "ffn_conv_b", "ffn_w_down", "ln1_g", "ln1_b", "ln2_g", "ln2_b")


def _layer_items(layer):
    j = layer // 2
    if layer % 2 == 0:
        mixer = [("in_t", "ev_w_in", j, "colT"), ("out", "ev_w_out", j, "row")]
    else:
        mixer = [("in_t", "od_w_in", j, "colT"), ("conv", "od_conv_w", j, "colsmall"), ("out", "od_w_out", j, "row")]
    return mixer + [("up_t", "ffn_w_up", layer, "colT"), ("fconv", "ffn_conv_w", layer, "colsmall"),
                    ("down", "ffn_w_down", layer, "row")]


def _to_send(kind, w, j):
    if kind == "colT":
        return w[j].T.astype(BF16)
    return w[j].astype(BF16) if kind == "row" else w[j]


def _from_gather(kind, name, g):
    if kind == "colsmall":
        return jnp.transpose(g, (1, 0, 2)).reshape(g.shape[1], -1)
    full = g.reshape(-1, g.shape[-1])
    if name == "od_w_in":
        full = jnp.pad(full, ((0, OD_IN_PAD - OD_IN), (0, 0)))
    return full


def _by_owner(kind, name, gfull):
    if kind == "colsmall":
        kk, c8 = gfull.shape
        return jnp.transpose(gfull.reshape(kk, N_DEV, c8 // N_DEV), (1, 0, 2))
    if name == "od_w_in":
        gfull = gfull[:OD_IN]
    return gfull.reshape(N_DEV, gfull.shape[0] // N_DEV, gfull.shape[1])


def _pack_small(d):
    flat = jnp.concatenate([d[n].reshape(-1) for n in SMALL])
    pad = (-flat.shape[0]) % (8 * LANES)
    return jnp.pad(flat, (0, pad)).reshape(-1, LANES)


def _unpack_small(packed, like):
    flat = packed.reshape(-1)
    out, off = {}, 0
    for n in SMALL:
        sz = int(np.prod(like[n].shape))
        out[n] = flat[off:off + sz].reshape(like[n].shape)
        off += sz
    return out


def kernel(x, positions, ev_w_in, ev_w_out, od_w_in, od_conv_w, od_a_log, od_dt_bias, od_norm_w, od_w_out, ffn_w_up, ffn_conv_w, ffn_conv_b, ffn_w_down, ln1_g, ln1_b, ln2_g, ln2_b, loss_target, m_ev_w_in, m_ev_w_out, m_od_w_in, m_od_conv_w, m_od_a_log, m_od_dt_bias, m_od_norm_w, m_od_w_out, m_ffn_w_up, m_ffn_conv_w, m_ffn_conv_b, m_ffn_w_down, m_ln1_g, m_ln1_b, m_ln2_g, m_ln2_b, v_ev_w_in, v_ev_w_out, v_od_w_in, v_od_conv_w, v_od_a_log, v_od_dt_bias, v_od_norm_w, v_od_w_out, v_ffn_w_up, v_ffn_conv_w, v_ffn_conv_b, v_ffn_w_down, v_ln1_g, v_ln1_b, v_ln2_g, v_ln2_b):
    w = dict(ev_w_in=ev_w_in, ev_w_out=ev_w_out, od_w_in=od_w_in, od_conv_w=od_conv_w, od_a_log=od_a_log,
             od_dt_bias=od_dt_bias, od_norm_w=od_norm_w, od_w_out=od_w_out, ffn_w_up=ffn_w_up, ffn_conv_w=ffn_conv_w,
             ffn_conv_b=ffn_conv_b, ffn_w_down=ffn_w_down, ln1_g=ln1_g, ln1_b=ln1_b, ln2_g=ln2_g, ln2_b=ln2_b)
    mom = dict(ev_w_in=m_ev_w_in, ev_w_out=m_ev_w_out, od_w_in=m_od_w_in, od_conv_w=m_od_conv_w, od_a_log=m_od_a_log,
               od_dt_bias=m_od_dt_bias, od_norm_w=m_od_norm_w, od_w_out=m_od_w_out, ffn_w_up=m_ffn_w_up,
               ffn_conv_w=m_ffn_conv_w, ffn_conv_b=m_ffn_conv_b, ffn_w_down=m_ffn_w_down, ln1_g=m_ln1_g,
               ln1_b=m_ln1_b, ln2_g=m_ln2_g, ln2_b=m_ln2_b)
    var = dict(ev_w_in=v_ev_w_in, ev_w_out=v_ev_w_out, od_w_in=v_od_w_in, od_conv_w=v_od_conv_w, od_a_log=v_od_a_log,
               od_dt_bias=v_od_dt_bias, od_norm_w=v_od_norm_w, od_w_out=v_od_w_out, ffn_w_up=v_ffn_w_up,
               ffn_conv_w=v_ffn_conv_w, ffn_conv_b=v_ffn_conv_b, ffn_w_down=v_ffn_w_down, ln1_g=v_ln1_g,
               ln1_b=v_ln1_b, ln2_g=v_ln2_g, ln2_b=v_ln2_b)

    myid = _my_index()
    small = {n: w[n] for n in SMALL}
    groups = [(layer, part) for layer in range(DEPTH) for part in ("mixer", "ffn")]

    def group_items(gi):
        layer, part = groups[gi]
        its = _layer_items(layer)
        return its[:-3] if part == "mixer" else its[-3:]

    level1, level2 = {}, {}

    def spread_job(gi):
        return ("spread", [_to_send(kind, w[n], j) for (_, n, j, kind) in group_items(gi)], None)

    def relay(gi, after, name):
        (srcs, lands), = _exchange_wait([level1.pop(gi)], after, name=name + "_wait")
        more = [spread_job(gi + 1)] if gi + 1 < len(groups) else []
        started, token = _exchange_start([("relay", [], lands)] + more, None, name=name + "_start")
        level2[gi] = (started[0], srcs)
        if more:
            level1[gi + 1] = started[1]
        return token

    def get_w(layer, part, after):
        gi = groups.index((layer, part))
        started, srcs = level2.pop(gi)
        (_, lands), = _exchange_wait([started], after, name=f"gather{gi}_wait")
        lands = [lax.dynamic_update_index_in_dim(l, s, myid, 0) for l, s in zip(lands, srcs)]
        return {key: _from_gather(kind, n, l) for (key, n, _, kind), l in zip(group_items(gi), lands)}, None

    def mid(layer, part, after):
        gi = groups.index((layer, part)) + 1
        return relay(gi, after, f"gather{gi}_relay") if gi < len(groups) else None

    landed = {}
    pending = []

    def scatter_finish(after):
        started, gi = pending.pop()
        (srcs, lands), = _exchange_wait([started], after, name=f"scatter{gi}_wait")
        for (key, _, _, _), l, s in zip(group_items(gi), lands, srcs):
            own = lax.dynamic_index_in_dim(s, myid, 0, keepdims=False)
            landed[(groups[gi][0], key)] = lax.dynamic_update_index_in_dim(l, own, myid, 0)

    def put_g(layer, part, g):
        gi = groups.index((layer, part))
        srcs = [_by_owner(kind, n, g[key]) for (key, n, _, kind) in group_items(gi)]
        (started,), token = _exchange_start([("scatter", srcs, None)], None, name=f"scatter{gi}_start")
        if pending:
            scatter_finish(token)
        pending.append((started, gi))
        return token

    (level1[0],), token = _exchange_start([spread_job(0)], None, name="gather0_spread_start")
    relay(0, token, "gather0_relay")
    loss, grad_x, gS = _local_step(x[0], positions[0], loss_target[0], get_w, mid, put_g, small)
    loss = lax.psum(loss, ("x", "y", "c"))

    outs_g, outs_d, outs_m, outs_v = {}, {}, {}, {}
    where = {n: [None] * w[n].shape[0] for n in SHARDED}
    for layer in range(DEPTH):
        for (key, n, j, kind) in _layer_items(layer):
            where[n][j] = (layer, key, kind)

    def update(n):
        g = jnp.stack([_sum8(landed[(layer, key)], name=f"L{layer}_{key}_sum") for (layer, key, _) in where[n]])
        if where[n][0][2] == "colT":
            tr = lambda a: jnp.swapaxes(a, 1, 2)
            d, nm, nv = _adamw_sharded(tr(w[n]), tr(mom[n]), tr(var[n]), g, name=f"adamw_{n}")
            outs_g[n], outs_d[n], outs_m[n], outs_v[n] = tr(g), tr(d), tr(nm), tr(nv)
        else:
            outs_g[n] = g
            outs_d[n], outs_m[n], outs_v[n] = _adamw_sharded(w[n], mom[n], var[n], g, name=f"adamw_{n}")

    last = {n for (_, n, _, _) in group_items(pending[0][1])}
    for n in SHARDED:
        if n not in last:
            update(n)
    scatter_finish(outs_d[[n for n in SHARDED if n not in last][-1]])
    for n in SHARDED:
        if n in last:
            update(n)

    gall = _small_exchange(_pack_small(gS), name="small_grads_exchange")
    g, d, nm, nv = _adamw_small(_pack_small({n: w[n] for n in SMALL}), _pack_small({n: mom[n] for n in SMALL}),
                                _pack_small({n: var[n] for n in SMALL}), gall, name="adamw_small")
    for dst, packed in ((outs_g, g), (outs_d, d), (outs_m, nm), (outs_v, nv)):
        dst.update(_unpack_small(packed, {n: w[n] for n in SMALL}))

    return (loss, grad_x[None], *[outs_g[n] for n in ALL_W], *[outs_d[n] for n in ALL_W],
            *[outs_m[n] for n in ALL_W], *[outs_v[n] for n in ALL_W])
```

```python
import functools
import math

import numpy as np
import jax
import jax.numpy as jnp
from jax import lax
from jax.experimental import pallas as pl
from jax.experimental.pallas import tpu as pltpu

F32 = jnp.float32
BF16 = jnp.bfloat16
MESH = pl.DeviceIdType.MESH

D_MODEL = 1024
SEQ = 2048
DEPTH = 4
N_DEV = 8
RET_HEADS, RET_DK, RET_DV = 4, 128, 256
RET_THETA = 10000.0
DIL_HEADS, DIL_HD = 8, 64
DIL_PAIRS = ((128, 1), (512, 4), (2048, 16))
ROPE_THETA = 500000.0
ROPE_DIMS = DIL_HD // 4
GDN_HEADS, GDN_DK, GDN_DV, GDN_CHUNK, GDN_CONV = 8, 128, 128, 64, 4
D_FF = 2816
FFN_CONV = 3
ALPHA = (2.0 * DEPTH) ** 0.25
EPS = 1e-5
RET_QK_W = RET_HEADS * RET_DK
RET_V_W = RET_HEADS * RET_DV
DIL_W = DIL_HEADS * DIL_HD
EV_IN = 2 * RET_QK_W + 2 * RET_V_W + 3 * DIL_W
EV_MIX = RET_V_W + DIL_W
GDN_W = GDN_HEADS * GDN_DK
OD_IN = 4 * GDN_W + 2 * GDN_HEADS
OD_IN_PAD = 4 * GDN_W + 128
ADAM_LR, ADAM_B1, ADAM_B2, ADAM_EPS, ADAM_WD, ADAM_STEP = 0.001, 0.9, 0.999, 1e-08, 0.01, 10

LANES = 128
VMEM_LIMIT = 56 * 1024 * 1024
ATT_BLK = 256
NEG = -1e30


def _cp(**kw):
    return pltpu.CompilerParams(vmem_limit_bytes=VMEM_LIMIT, **kw)


def _tile(n, cap):
    if n <= cap:
        return n
    best = None
    for t in range(LANES, cap + 1, LANES):
        if n % t == 0:
            best = t
    assert best is not None, (n, cap)
    return best


def _mm(a, b, *, ta=False, tb=False, name, out_dtype=F32, dep=None, tm=None, tn=None):
    m = a.shape[1] if ta else a.shape[0]
    k = a.shape[0] if ta else a.shape[1]
    n = b.shape[0] if tb else b.shape[1]
    assert (b.shape[1] if tb else b.shape[0]) == k
    assert a.dtype == BF16 and b.dtype == BF16
    if tn is None:
        tn = n if n <= 1024 else _tile(n, 512)
    if tm is None:
        tm = m if (tn < n and k <= 1024 and m <= 2048) else _tile(m, 512)
    dims = (((0 if ta else 1,), (1 if tb else 0,)), ((), ()))

    def body(a_ref, b_ref, *rest):
        o_ref = rest[-1]
        o_ref[...] = lax.dot_general(a_ref[...], b_ref[...], dims,
                                     preferred_element_type=F32).astype(o_ref.dtype)

    a_spec = pl.BlockSpec((k, tm), lambda i, j: (0, i)) if ta else pl.BlockSpec((tm, k), lambda i, j: (i, 0))
    b_spec = pl.BlockSpec((tn, k), lambda i, j: (j, 0)) if tb else pl.BlockSpec((k, tn), lambda i, j: (0, j))
    extra = [] if dep is None else [dep]
    return pl.pallas_call(
        body, grid=(m // tm, n // tn), in_specs=[a_spec, b_spec] + [pl.BlockSpec(memory_space=pl.ANY)] * len(extra),
        out_specs=pl.BlockSpec((tm, tn), lambda i, j: (i, j)),
        out_shape=jax.ShapeDtypeStruct((m, n), out_dtype), name=name, compiler_params=_cp())(a, b, *extra)


LN_ROWS = 256


def _ln_bwd(z, g, dya, dyb, *, name):
    t, d = z.shape
    two = dyb is not None

    def body(*refs):
        if two:
            z_ref, g_ref, dya_ref, dyb_ref, dz_ref, dzb_ref, dg_ref, db_ref = refs
            dy = dya_ref[...] + ALPHA * dyb_ref[...]
        else:
            z_ref, g_ref, dya_ref, dz_ref, dzb_ref, dg_ref, db_ref = refs
            dy = dya_ref[...]
        zz = z_ref[...]
        mu = jnp.mean(zz, -1, keepdims=True)
        zc = zz - mu
        var = jnp.mean(zc * zc, -1, keepdims=True)
        r = lax.rsqrt(var + EPS)
        xh = zc * r
        dxh = dy * g_ref[...]
        dz = r * (dxh - jnp.mean(dxh, -1, keepdims=True) - xh * jnp.mean(dxh * xh, -1, keepdims=True))
        dz_ref[...] = dz
        dzb_ref[...] = dz.astype(BF16)

        @pl.when(pl.program_id(0) == 0)
        def _():
            dg_ref[...] = jnp.zeros_like(dg_ref)
            db_ref[...] = jnp.zeros_like(db_ref)

        dg_ref[...] += jnp.sum(dy * xh, 0, keepdims=True)
        db_ref[...] += jnp.sum(dy, 0, keepdims=True)

    row = pl.BlockSpec((LN_ROWS, d), lambda i: (i, 0))
    vec = pl.BlockSpec((1, d), lambda i: (0, 0))
    ins = [z, g, dya] + ([dyb] if two else [])
    return pl.pallas_call(
        body, grid=(t // LN_ROWS,), in_specs=[row, vec, row] + ([row] if two else []),
        out_specs=[row, row, vec, vec],
        out_shape=[jax.ShapeDtypeStruct((t, d), F32), jax.ShapeDtypeStruct((t, d), BF16),
                   jax.ShapeDtypeStruct((1, d), F32), jax.ShapeDtypeStruct((1, d), F32)],
        name=name, compiler_params=_cp())(*ins)


def _ln_rows(k):
    return 256 if k > 4096 else 512


def _mm_ln_fwd(a, w, x, g, b, *, name, dep=None):
    t, k = a.shape
    d = w.shape[1]
    tm = _ln_rows(k)

    def body(a_ref, w_ref, x_ref, g_ref, b_ref, *rest):
        z_ref, y_ref, yb_ref = rest[-3:]
        z = ALPHA * x_ref[...] + _nn(a_ref[...], w_ref[...])
        mu = jnp.mean(z, -1, keepdims=True)
        zc = z - mu
        var = jnp.mean(zc * zc, -1, keepdims=True)
        y = zc * lax.rsqrt(var + EPS) * g_ref[...] + b_ref[...]
        z_ref[...] = z
        y_ref[...] = y
        yb_ref[...] = y.astype(BF16)

    row = pl.BlockSpec((tm, d), lambda i: (i, 0))
    vec = pl.BlockSpec((1, d), lambda i: (0, 0))
    extra = [] if dep is None else [dep]
    return pl.pallas_call(
        body, grid=(t // tm,),
        in_specs=[pl.BlockSpec((tm, k), lambda i: (i, 0)), pl.BlockSpec((k, d), lambda i: (0, 0)), row, vec, vec]
        + [pl.BlockSpec(memory_space=pl.ANY)] * len(extra),
        out_specs=[row, row, row],
        out_shape=[jax.ShapeDtypeStruct((t, d), F32), jax.ShapeDtypeStruct((t, d), F32), jax.ShapeDtypeStruct((t, d), BF16)],
        name=name, compiler_params=_cp())(a, w, x, g, b, *extra)


def _mm_ln_bwd(a, w, z, g, dyb, *, name, dep=None):
    t, k = a.shape
    d = w.shape[1]
    tm = _ln_rows(k)

    def body(a_ref, w_ref, z_ref, g_ref, dyb_ref, *rest):
        dz_ref, dzb_ref, dg_ref, db_ref = rest[-4:]
        dy = _nn(a_ref[...], w_ref[...]) + ALPHA * dyb_ref[...]
        zz = z_ref[...]
        mu = jnp.mean(zz, -1, keepdims=True)
        zc = zz - mu
        var = jnp.mean(zc * zc, -1, keepdims=True)
        r = lax.rsqrt(var + EPS)
        xh = zc * r
        dxh = dy * g_ref[...]
        dz = r * (dxh - jnp.mean(dxh, -1, keepdims=True) - xh * jnp.mean(dxh * xh, -1, keepdims=True))
        dz_ref[...] = dz
        dzb_ref[...] = dz.astype(BF16)

        @pl.when(pl.program_id(0) == 0)
        def _():
            dg_ref[...] = jnp.zeros_like(dg_ref)
            db_ref[...] = jnp.zeros_like(db_ref)

        dg_ref[...] += jnp.sum(dy * xh, 0, keepdims=True)
        db_ref[...] += jnp.sum(dy, 0, keepdims=True)

    row = pl.BlockSpec((tm, d), lambda i: (i, 0))
    vec = pl.BlockSpec((1, d), lambda i: (0, 0))
    extra = [] if dep is None else [dep]
    return pl.pallas_call(
        body, grid=(t // tm,),
        in_specs=[pl.BlockSpec((tm, k), lambda i: (i, 0)), pl.BlockSpec((k, d), lambda i: (0, 0)), row, vec, row]
        + [pl.BlockSpec(memory_space=pl.ANY)] * len(extra),
        out_specs=[row, row, vec, vec],
        out_shape=[jax.ShapeDtypeStruct((t, d), F32), jax.ShapeDtypeStruct((t, d), BF16),
                   jax.ShapeDtypeStruct((1, d), F32), jax.ShapeDtypeStruct((1, d), F32)],
        name=name, compiler_params=_cp())(a, w, z, g, dyb, *extra)


def _axpy(a, b, *, name):
    t, d = a.shape

    def body(a_ref, b_ref, o_ref):
        o_ref[...] = a_ref[...] + ALPHA * b_ref[...]

    row = pl.BlockSpec((LN_ROWS, d), lambda i: (i, 0))
    return pl.pallas_call(body, grid=(t // LN_ROWS,), in_specs=[row, row], out_specs=row,
                          out_shape=jax.ShapeDtypeStruct((t, d), F32), name=name, compiler_params=_cp())(a, b)


def _loss_head(y, target, *, name):
    t, d = y.shape

    def body(y_ref, t_ref, dy_ref, l_ref):
        e = y_ref[...] - t_ref[...]
        dy_ref[...] = e * (1.0 / d)

        @pl.when(pl.program_id(0) == 0)
        def _():
            l_ref[...] = jnp.zeros_like(l_ref)

        l_ref[...] += jnp.zeros_like(l_ref) + 0.5 * jnp.sum(jnp.mean(e * e, -1, keepdims=True), 0, keepdims=True)

    row = pl.BlockSpec((LN_ROWS, d), lambda i: (i, 0))
    return pl.pallas_call(
        body, grid=(t // LN_ROWS,), in_specs=[row, row],
        out_specs=[row, pl.BlockSpec((1, LANES), lambda i: (0, 0))],
        out_shape=[jax.ShapeDtypeStruct((t, d), F32), jax.ShapeDtypeStruct((1, LANES), F32)],
        name=name, compiler_params=_cp())(y, target)


def _sig(x):
    return 1.0 / (1.0 + jnp.exp(-x))


def _silu(x):
    return x * _sig(x)


def _dsilu(x):
    s = _sig(x)
    return s * (1.0 + x * (1.0 - s))


def _shift_down(u, k, row):
    if k == 0:
        return u
    return jnp.where(row >= k, pltpu.roll(u, k, 0), 0.0)


def _shift_up(u, k, row):
    if k == 0:
        return u
    t = u.shape[0]
    return jnp.where(row < t - k, pltpu.roll(u, t - k, 0), 0.0)


def _dwconv(u, w_ref, row):
    kk = w_ref.shape[0]
    acc = None
    for j in range(kk):
        term = w_ref[j:j + 1, :] * _shift_down(u, kk - 1 - j, row)
        acc = term if acc is None else acc + term
    return acc


def _dwconv_bwd(u, w_ref, dc, row, dw_ref):
    kk = w_ref.shape[0]
    du = None
    for j in range(kk):
        term = w_ref[j:j + 1, :] * _shift_up(dc, kk - 1 - j, row)
        du = term if du is None else du + term
        dw_ref[j:j + 1, :] = jnp.sum(dc * _shift_down(u, kk - 1 - j, row), 0, keepdims=True)
    return du


CONV_ROWS = 64


def _rows(b):
    return pl.ds(pl.multiple_of(b * CONV_ROWS, CONV_ROWS), CONV_ROWS)


def _shifted_down(ref, b, k, row):
    cur = ref[_rows(b), :]
    if k == 0:
        return cur
    prev = jnp.where(b > 0, ref[_rows(jnp.maximum(b - 1, 0)), :], 0.0)
    return jnp.where(row >= k, pltpu.roll(cur, k, 0), pltpu.roll(prev, k, 0))


def _shifted_up(ref, b, k, row, nblk):
    cur = ref[_rows(b), :]
    if k == 0:
        return cur
    nxt = jnp.where(b < nblk - 1, ref[_rows(jnp.minimum(b + 1, nblk - 1)), :], 0.0)
    return jnp.where(row < CONV_ROWS - k, pltpu.roll(cur, CONV_ROWS - k, 0), pltpu.roll(nxt, CONV_ROWS - k, 0))


def _dwconv_blk(u_ref, w_ref, b, row):
    kk = w_ref.shape[0]
    views = [_shifted_down(u_ref, b, kk - 1 - j, row) for j in range(kk)]
    acc = None
    for j in range(kk):
        term = w_ref[j:j + 1, :] * views[j]
        acc = term if acc is None else acc + term
    return acc, views


def _dwconv_du_blk(dc_ref, w_ref, b, row, nblk):
    kk = w_ref.shape[0]
    du = None
    for j in range(kk):
        term = w_ref[j:j + 1, :] * _shifted_up(dc_ref, b, kk - 1 - j, row, nblk)
        du = term if du is None else du + term
    return du


FFN_TC = 256


def _ffn_up_mid(x, up_t, cw, cb, *, name, dep=None):
    t, d = x.shape
    nb = D_FF // FFN_TC

    def body(x_ref, ugt_ref, uvt_ref, wg_ref, wv_ref, bg_ref, bv_ref, *rest):
        ug_ref, uv_ref, a_ref = rest[-3:]
        xx = x_ref[...]
        row = lax.broadcasted_iota(jnp.int32, (t, FFN_TC), 0)
        ug = _nt(xx, ugt_ref[...])
        ug_ref[...] = ug
        uv = _nt(xx, uvt_ref[...])
        uv_ref[...] = uv
        cg = _dwconv(ug, wg_ref, row) + bg_ref[...]
        cv = _dwconv(uv, wv_ref, row) + bv_ref[...]
        a_ref[...] = (_silu(cg) * cv).astype(BF16)

    col = pl.BlockSpec((t, FFN_TC), lambda j: (0, j))
    wt = lambda off: pl.BlockSpec((FFN_TC, d), lambda j: (j + off, 0))
    wsp = lambda off: pl.BlockSpec((FFN_CONV, FFN_TC), lambda j: (0, j + off))
    bsp = lambda off: pl.BlockSpec((1, FFN_TC), lambda j: (0, j + off))
    extra = [] if dep is None else [dep]
    return pl.pallas_call(
        body, grid=(nb,),
        in_specs=[pl.BlockSpec((t, d), lambda j: (0, 0)), wt(0), wt(nb), wsp(0), wsp(nb), bsp(0), bsp(nb)]
        + [pl.BlockSpec(memory_space=pl.ANY)] * len(extra),
        out_specs=[col, col, col],
        out_shape=[jax.ShapeDtypeStruct((t, D_FF), F32), jax.ShapeDtypeStruct((t, D_FF), F32),
                   jax.ShapeDtypeStruct((t, D_FF), BF16)],
        name=name, compiler_params=_cp())(x, up_t, up_t, cw, cw, cb, cb, *extra)


def _ffn_mid_bwd(ug, uv, cw, cb, dz, down, *, name):
    t, d = dz.shape
    nb = D_FF // FFN_TC

    nblk = t // CONV_ROWS

    def body(ug_ref, uv_ref, wg_ref, wv_ref, bg_ref, bv_ref, dz_ref, dn_ref,
             dug_ref, duv_ref, dwg_ref, dwv_ref, dbg_ref, dbv_ref, da_ref, dcg_s, dcv_s):
        da_ref[...] = _nt(dz_ref[...], dn_ref[...])
        row = lax.broadcasted_iota(jnp.int32, (CONV_ROWS, FFN_TC), 0)
        zero = jnp.zeros((1, FFN_TC), F32)

        def first(b, acc):
            cg, ugs = _dwconv_blk(ug_ref, wg_ref, b, row)
            cv, uvs = _dwconv_blk(uv_ref, wv_ref, b, row)
            cg = cg + bg_ref[...]
            cv = cv + bv_ref[...]
            da_ = da_ref[_rows(b), :]
            dcv = da_ * _silu(cg)
            dcg = da_ * cv * _dsilu(cg)
            dcg_s[_rows(b), :] = dcg
            dcv_s[_rows(b), :] = dcv
            red = [jnp.sum(dcg * s, 0, keepdims=True) for s in ugs] + [jnp.sum(dcg, 0, keepdims=True)]
            red += [jnp.sum(dcv * s, 0, keepdims=True) for s in uvs] + [jnp.sum(dcv, 0, keepdims=True)]
            return tuple(a + r for a, r in zip(acc, red))

        acc = lax.fori_loop(0, nblk, first, (zero,) * (2 * FFN_CONV + 2))
        for j in range(FFN_CONV):
            dwg_ref[j:j + 1, :] = acc[j]
            dwv_ref[j:j + 1, :] = acc[FFN_CONV + 1 + j]
        dbg_ref[...] = acc[FFN_CONV]
        dbv_ref[...] = acc[2 * FFN_CONV + 1]

        def second(b, carry):
            dug_ref[_rows(b), :] = _dwconv_du_blk(dcg_s, wg_ref, b, row, nblk).astype(BF16)
            duv_ref[_rows(b), :] = _dwconv_du_blk(dcv_s, wv_ref, b, row, nblk).astype(BF16)
            return carry

        lax.fori_loop(0, nblk, second, 0)

    col = pl.BlockSpec((t, FFN_TC), lambda j: (0, j))
    wsp = lambda off: pl.BlockSpec((FFN_CONV, FFN_TC), lambda j: (0, j + off))
    bsp = lambda off: pl.BlockSpec((1, FFN_TC), lambda j: (0, j + off))
    outs = pl.pallas_call(
        body, grid=(nb,),
        in_specs=[col, col, wsp(0), wsp(nb), bsp(0), bsp(nb), pl.BlockSpec((t, d), lambda j: (0, 0)),
                  pl.BlockSpec((FFN_TC, d), lambda j: (j, 0))],
        out_specs=[col, col, wsp(0), wsp(0), bsp(0), bsp(0)],
        out_shape=[jax.ShapeDtypeStruct((t, D_FF), BF16), jax.ShapeDtypeStruct((t, D_FF), BF16),
                   jax.ShapeDtypeStruct((FFN_CONV, D_FF), F32), jax.ShapeDtypeStruct((FFN_CONV, D_FF), F32),
                   jax.ShapeDtypeStruct((1, D_FF), F32), jax.ShapeDtypeStruct((1, D_FF), F32)],
        scratch_shapes=[pltpu.VMEM((t, FFN_TC), F32), pltpu.VMEM((t, FFN_TC), F32), pltpu.VMEM((t, FFN_TC), F32)],
        name=name, compiler_params=_cp())(ug, uv, cw, cw, cb, cb, dz, down)
    dug, duv, dwg, dwv, dbg, dbv = outs
    return (jnp.concatenate([dug, duv], 1), jnp.concatenate([dwg, dwv], 1), jnp.concatenate([dbg, dbv], 1))


def _rot_a(x, c2, s2):
    return x * c2 + pltpu.roll(x, RET_DK // 2, 1) * s2


def _rot_a_t(dy, c2, s2):
    return dy * c2 + pltpu.roll(dy * s2, RET_DK // 2, 1)


def _decay_tile(lg, blk_diff):
    r = lax.broadcasted_iota(jnp.int32, (ATT_BLK, ATT_BLK), 0)
    c = lax.broadcasted_iota(jnp.int32, (ATT_BLK, ATT_BLK), 1)
    rel = r - c + blk_diff * ATT_BLK
    return jnp.where(rel >= 0, jnp.exp(jnp.maximum(rel, 0).astype(F32) * lg), 0.0)


def _nt(a, b):
    return lax.dot_general(a, b, (((1,), (1,)), ((), ())), preferred_element_type=F32)


def _nn(a, b):
    return lax.dot_general(a, b, (((1,), (0,)), ((), ())), preferred_element_type=F32)


def _tn(a, b):
    return lax.dot_general(a, b, (((0,), (0,)), ((), ())), preferred_element_type=F32)


def _ret_specs(t):
    q = pl.BlockSpec((t, RET_DK), lambda h: (0, h))
    k = pl.BlockSpec((t, RET_DK), lambda h: (0, RET_HEADS + h))
    v = pl.BlockSpec((t, RET_DV), lambda h: (0, RET_HEADS + h))
    g = pl.BlockSpec((t, RET_DV), lambda h: (0, 2 * RET_HEADS + h))
    tab = pl.BlockSpec((t, RET_DK), lambda h: (0, 0))
    lg = pl.BlockSpec((1, 1, LANES), lambda h: (h, 0, 0))
    return q, k, v, g, tab, lg


def _ret_fwd(h, c2, s2, lgt, *, name):
    t = h.shape[0]
    nblk = t // ATT_BLK
    scale = RET_DK ** -0.5

    def body(q_ref, k_ref, v_ref, g_ref, c_ref, s_ref, lg_ref, o_ref, ya_ref, qs, ks, vs):
        c2_, s2_ = c_ref[...], s_ref[...]
        qs[...] = _rot_a(q_ref[...], c2_, s2_).astype(BF16)
        ks[...] = (_rot_a(k_ref[...], c2_, s2_) * scale).astype(BF16)
        vs[...] = v_ref[...].astype(BF16)
        lg = lg_ref[0, :, 0:1]
        for i in range(nblk):
            qi = qs[pl.ds(i * ATT_BLK, ATT_BLK), :]
            acc = jnp.zeros((ATT_BLK, RET_DV), F32)
            for j in range(i + 1):
                sl = pl.ds(j * ATT_BLK, ATT_BLK)
                s = _nt(qi, ks[sl, :]) * _decay_tile(lg, i - j)
                acc = acc + _nn(s.astype(BF16), vs[sl, :])
            rows = pl.ds(i * ATT_BLK, ATT_BLK)
            o_ref[rows, :] = acc
            r = lax.rsqrt(jnp.mean(acc * acc, -1, keepdims=True) + EPS)
            ya_ref[rows, :] = (acc * r * _silu(g_ref[rows, :])).astype(BF16)

    q, k, v, g, tab, lg = _ret_specs(t)
    out = pl.BlockSpec((t, RET_DV), lambda hh: (0, hh))
    return pl.pallas_call(
        body, grid=(RET_HEADS,), in_specs=[q, k, v, g, tab, tab, lg], out_specs=[out, out],
        out_shape=[jax.ShapeDtypeStruct((t, RET_V_W), F32), jax.ShapeDtypeStruct((t, RET_V_W), BF16)],
        scratch_shapes=[pltpu.VMEM((t, RET_DK), BF16), pltpu.VMEM((t, RET_DK), BF16), pltpu.VMEM((t, RET_DV), BF16)],
        name=name, compiler_params=_cp())(h, h, h, h, c2, s2, lgt)


def _ret_bwd(h, c2, s2, lgt, o, dy, *, name):
    t = h.shape[0]
    nblk = t // ATT_BLK
    scale = RET_DK ** -0.5

    def body(q_ref, k_ref, v_ref, g_ref, c_ref, s_ref, lg_ref, o_ref, dy_ref,
             dq_ref, dk_ref, dv_ref, dg_ref, qs, ks, vs, dos, dka, dva):
        c2_, s2_ = c_ref[...], s_ref[...]
        qs[...] = _rot_a(q_ref[...], c2_, s2_).astype(BF16)
        ks[...] = (_rot_a(k_ref[...], c2_, s2_) * scale).astype(BF16)
        vs[...] = v_ref[...].astype(BF16)
        lg = lg_ref[0, :, 0:1]
        oo = o_ref[...]
        gg = g_ref[...]
        dya = dy_ref[...]
        r = lax.rsqrt(jnp.mean(oo * oo, -1, keepdims=True) + EPS)
        rn = oo * r
        dg_ref[...] = (dya * rn * _dsilu(gg)).astype(BF16)
        drn = dya * _silu(gg)
        dos[...] = (r * (drn - rn * jnp.mean(drn * rn, -1, keepdims=True))).astype(BF16)
        dka[...] = jnp.zeros_like(dka)
        dva[...] = jnp.zeros_like(dva)
        for i in range(nblk):
            rows = pl.ds(i * ATT_BLK, ATT_BLK)
            qi = qs[rows, :]
            doi = dos[rows, :]
            dqa = jnp.zeros((ATT_BLK, RET_DK), F32)
            for j in range(i + 1):
                sl = pl.ds(j * ATT_BLK, ATT_BLK)
                dt_ = _decay_tile(lg, i - j)
                kj = ks[sl, :]
                s = (_nt(qi, kj) * dt_).astype(BF16)
                ds = (_nt(doi, vs[sl, :]) * dt_).astype(BF16)
                dqa = dqa + _nn(ds, kj)
                dka[sl, :] += _tn(ds, qi)
                dva[sl, :] += _tn(s, doi)
            dq_ref[rows, :] = _rot_a_t(dqa, c_ref[rows, :], s_ref[rows, :]).astype(BF16)
        dk_ref[...] = (_rot_a_t(dka[...], c2_, s2_) * scale).astype(BF16)
        dv_ref[...] = dva[...].astype(BF16)

    q, k, v, g, tab, lg = _ret_specs(t)
    blk_v = pl.BlockSpec((t, RET_DV), lambda hh: (0, hh))
    blk_k = pl.BlockSpec((t, RET_DK), lambda hh: (0, hh))
    return pl.pallas_call(
        body, grid=(RET_HEADS,), in_specs=[q, k, v, g, tab, tab, lg, blk_v, blk_v],
        out_specs=[blk_k, blk_k, blk_v, blk_v],
        out_shape=[jax.ShapeDtypeStruct((t, RET_QK_W), BF16), jax.ShapeDtypeStruct((t, RET_QK_W), BF16),
                   jax.ShapeDtypeStruct((t, RET_V_W), BF16), jax.ShapeDtypeStruct((t, RET_V_W), BF16)],
        scratch_shapes=[pltpu.VMEM((t, RET_DK), BF16), pltpu.VMEM((t, RET_DK), BF16), pltpu.VMEM((t, RET_DV), BF16),
                        pltpu.VMEM((t, RET_DV), BF16), pltpu.VMEM((t, RET_DK), F32), pltpu.VMEM((t, RET_DV), F32)],
        name=name, compiler_params=_cp())(h, h, h, h, c2, s2, lgt, o, dy)


def _rot_b(x, cb, shi, slo):
    return x * cb + pltpu.roll(x, ROPE_DIMS // 2, 1) * shi + pltpu.roll(x, LANES - ROPE_DIMS // 2, 1) * slo


def _rot_b_t(dy, cb, shi, slo):
    return dy * cb + pltpu.roll(dy * shi, LANES - ROPE_DIMS // 2, 1) + pltpu.roll(dy * slo, ROPE_DIMS // 2, 1)


def _dil_specs(t):
    base = (2 * RET_QK_W + 2 * RET_V_W) // LANES
    npair = DIL_W // LANES
    q = pl.BlockSpec((t, LANES), lambda p: (0, base + p))
    k = pl.BlockSpec((t, LANES), lambda p: (0, base + npair + p))
    v = pl.BlockSpec((t, LANES), lambda p: (0, base + 2 * npair + p))
    tab = pl.BlockSpec((t, LANES), lambda p: (0, 0))
    strip = pl.BlockSpec((ATT_BLK, t), lambda p: (0, 0))
    pair = pl.BlockSpec((t, LANES), lambda p: (0, p))
    return q, k, v, tab, strip, pair


def _dil_fwd(h, cb, shi, slo, strip, *, name):
    t = h.shape[0]
    nblk = t // ATT_BLK
    scale = DIL_HD ** -0.5

    def body(q_ref, k_ref, v_ref, cb_ref, shi_ref, slo_ref, st_ref, o_ref, yb_ref, lse_ref, qs, ks, vs):
        cb_, shi_, slo_ = cb_ref[...], shi_ref[...], slo_ref[...]
        lane = lax.broadcasted_iota(jnp.int32, (t, LANES), 1)
        qr = _rot_b(q_ref[...], cb_, shi_, slo_) * scale
        qs[0] = jnp.where(lane < DIL_HD, qr, 0.0).astype(BF16)
        qs[1] = jnp.where(lane >= DIL_HD, qr, 0.0).astype(BF16)
        ks[...] = _rot_b(k_ref[...], cb_, shi_, slo_).astype(BF16)
        vs[...] = v_ref[...].astype(BF16)
        lane_b = lax.broadcasted_iota(jnp.int32, (ATT_BLK, LANES), 1)
        for i in range(nblk):
            w = (i + 1) * ATT_BLK
            rows = pl.ds(i * ATT_BLK, ATT_BLK)
            logc = st_ref[:, t - w:t]
            outs, lses = [], []
            for hd in range(2):
                s = _nt(qs[hd, rows, :], ks[0:w, :]) + logc
                m = jnp.max(s, -1, keepdims=True)
                p = jnp.exp(s - m)
                l = jnp.sum(p, -1, keepdims=True)
                outs.append(_nn(p.astype(BF16), vs[0:w, :]) / l)
                lses.append(m + jnp.log(l))
            o = jnp.where(lane_b < DIL_HD, outs[0], outs[1])
            o_ref[rows, :] = o
            yb_ref[rows, :] = o.astype(BF16)
            lse_ref[rows, :] = jnp.where(lane_b < DIL_HD, lses[0], lses[1])

    q, k, v, tab, strip_spec, pair = _dil_specs(t)
    return pl.pallas_call(
        body, grid=(DIL_W // LANES,), in_specs=[q, k, v, tab, tab, tab, strip_spec], out_specs=[pair, pair, pair],
        out_shape=[jax.ShapeDtypeStruct((t, DIL_W), F32), jax.ShapeDtypeStruct((t, DIL_W), BF16),
                   jax.ShapeDtypeStruct((t, DIL_W), F32)],
        scratch_shapes=[pltpu.VMEM((2, t, LANES), BF16), pltpu.VMEM((t, LANES), BF16), pltpu.VMEM((t, LANES), BF16)],
        name=name, compiler_params=_cp())(h, h, h, cb, shi, slo, strip)


def _dil_bwd(h, cb, shi, slo, strip, o, lse, dy, *, name):
    t = h.shape[0]
    nblk = t // ATT_BLK
    scale = DIL_HD ** -0.5

    def body(q_ref, k_ref, v_ref, cb_ref, shi_ref, slo_ref, st_ref, o_ref, lse_ref, dy_ref,
             dq_ref, dk_ref, dv_ref, qs, ks, vs, dos, dls, dka, dva):
        cb_, shi_, slo_ = cb_ref[...], shi_ref[...], slo_ref[...]
        lane = lax.broadcasted_iota(jnp.int32, (t, LANES), 1)
        qr = _rot_b(q_ref[...], cb_, shi_, slo_) * scale
        qs[0] = jnp.where(lane < DIL_HD, qr, 0.0).astype(BF16)
        qs[1] = jnp.where(lane >= DIL_HD, qr, 0.0).astype(BF16)
        ks[...] = _rot_b(k_ref[...], cb_, shi_, slo_).astype(BF16)
        vs[...] = v_ref[...].astype(BF16)
        do = dy_ref[...]
        prod = do * o_ref[...]
        d0 = jnp.sum(jnp.where(lane < DIL_HD, prod, 0.0), -1, keepdims=True)
        d1 = jnp.sum(jnp.where(lane >= DIL_HD, prod, 0.0), -1, keepdims=True)
        dls[...] = jnp.where(lane < DIL_HD, d0, d1)
        dos[0] = jnp.where(lane < DIL_HD, do, 0.0).astype(BF16)
        dos[1] = jnp.where(lane >= DIL_HD, do, 0.0).astype(BF16)
        dka[...] = jnp.zeros_like(dka)
        dva[...] = jnp.zeros_like(dva)
        lane_b = lax.broadcasted_iota(jnp.int32, (ATT_BLK, LANES), 1)
        for i in range(nblk):
            w = (i + 1) * ATT_BLK
            rows = pl.ds(i * ATT_BLK, ATT_BLK)
            logc = st_ref[:, t - w:t]
            dqs = []
            for hd in range(2):
                col = hd * DIL_HD
                qh = qs[hd, rows, :]
                doh = dos[hd, rows, :]
                lse_h = lse_ref[rows, col:col + 1]
                dl_h = dls[rows, col:col + 1]
                p = jnp.exp(_nt(qh, ks[0:w, :]) + logc - lse_h)
                dp = _nt(doh, vs[0:w, :])
                ds = (p * (dp - dl_h)).astype(BF16)
                dqs.append(_nn(ds, ks[0:w, :]))
                dka[0:w, :] += _tn(ds, qh)
                dva[0:w, :] += _tn(p.astype(BF16), doh)
            dq = jnp.where(lane_b < DIL_HD, dqs[0], dqs[1]) * scale
            dq_ref[rows, :] = _rot_b_t(dq, cb_ref[rows, :], shi_ref[rows, :], slo_ref[rows, :]).astype(BF16)
        dk_ref[...] = _rot_b_t(dka[...], cb_, shi_, slo_).astype(BF16)
        dv_ref[...] = dva[...].astype(BF16)

    q, k, v, tab, strip_spec, pair = _dil_specs(t)
    dy_spec = pl.BlockSpec((t, LANES), lambda p: (0, RET_V_W // LANES + p))
    return pl.pallas_call(
        body, grid=(DIL_W // LANES,), in_specs=[q, k, v, tab, tab, tab, strip_spec, pair, pair, dy_spec],
        out_specs=[pair, pair, pair],
        out_shape=[jax.ShapeDtypeStruct((t, DIL_W), BF16)] * 3,
        scratch_shapes=[pltpu.VMEM((2, t, LANES), BF16), pltpu.VMEM((t, LANES), BF16), pltpu.VMEM((t, LANES), BF16),
                        pltpu.VMEM((2, t, LANES), BF16), pltpu.VMEM((t, LANES), F32),
                        pltpu.VMEM((t, LANES), F32), pltpu.VMEM((t, LANES), F32)],
        name=name, compiler_params=_cp())(h, h, h, cb, shi, slo, strip, o, lse, dy)


def _gdn_prep_fwd(h, cw, *, name):
    t = h.shape[0]
    qscale = GDN_DK ** -0.5

    def body(hq_ref, hk_ref, hv_ref, wq_ref, wk_ref, wv_ref, q_ref, k_ref, v_ref):
        row = lax.broadcasted_iota(jnp.int32, (t, GDN_DK), 0)
        sq = _silu(_dwconv(hq_ref[...], wq_ref, row))
        sk = _silu(_dwconv(hk_ref[...], wk_ref, row))
        q_ref[0] = sq * lax.rsqrt(jnp.sum(sq * sq, -1, keepdims=True) + 1e-6) * qscale
        k_ref[0] = sk * lax.rsqrt(jnp.sum(sk * sk, -1, keepdims=True) + 1e-6)
        v_ref[0] = _silu(_dwconv(hv_ref[...], wv_ref, row))

    hs = lambda off: pl.BlockSpec((t, GDN_DK), lambda i: (0, i + off))
    ws = lambda off: pl.BlockSpec((GDN_CONV, GDN_DK), lambda i: (0, i + off))
    out = pl.BlockSpec((1, t, GDN_DK), lambda i: (i, 0, 0))
    return pl.pallas_call(
        body, grid=(GDN_HEADS,), in_specs=[hs(0), hs(8), hs(16), ws(0), ws(8), ws(16)], out_specs=[out, out, out],
        out_shape=[jax.ShapeDtypeStruct((GDN_HEADS, t, GDN_DK), F32)] * 3,
        name=name, compiler_params=_cp())(h, h, h, cw, cw, cw)


def _gdn_prep_bwd(h, cw, dq, dk, dv, *, name):
    t = h.shape[0]
    qscale = GDN_DK ** -0.5

    def body(hq_ref, hk_ref, hv_ref, wq_ref, wk_ref, wv_ref, dq_ref, dk_ref, dv_ref,
             dhq_ref, dhk_ref, dhv_ref, dwq_ref, dwk_ref, dwv_ref):
        row = lax.broadcasted_iota(jnp.int32, (t, GDN_DK), 0)

        def one(h_ref, w_ref, d_ref, dh_ref, dw_ref, norm, sc):
            u = h_ref[...]
            c = _dwconv(u, w_ref, row)
            d = d_ref[0]
            if norm:
                s = _silu(c)
                r = lax.rsqrt(jnp.sum(s * s, -1, keepdims=True) + 1e-6)
                n = s * r
                d = d * sc
                d = r * (d - n * jnp.sum(d * n, -1, keepdims=True))
            dc = d * _dsilu(c)
            dh_ref[...] = _dwconv_bwd(u, w_ref, dc, row, dw_ref).astype(BF16)

        one(hq_ref, wq_ref, dq_ref, dhq_ref, dwq_ref, True, qscale)
        one(hk_ref, wk_ref, dk_ref, dhk_ref, dwk_ref, True, 1.0)
        one(hv_ref, wv_ref, dv_ref, dhv_ref, dwv_ref, False, 1.0)

    hs = lambda off: pl.BlockSpec((t, GDN_DK), lambda i: (0, i + off))
    ws = lambda off: pl.BlockSpec((GDN_CONV, GDN_DK), lambda i: (0, i + off))
    hd = pl.BlockSpec((1, t, GDN_DK), lambda i: (i, 0, 0))
    return pl.pallas_call(
        body, grid=(GDN_HEADS,), in_specs=[hs(0), hs(8), hs(16), ws(0), ws(8), ws(16), hd, hd, hd],
        out_specs=[hs(0), hs(0), hs(0), ws(0), ws(0), ws(0)],
        out_shape=[jax.ShapeDtypeStruct((t, GDN_W), BF16)] * 3 + [jax.ShapeDtypeStruct((GDN_CONV, GDN_W), F32)] * 3,
        name=name, compiler_params=_cp())(h, h, h, cw, cw, cw, dq, dk, dv)


def _make_mm2(wide):
    def raw(a, b, dims):
        if wide:
            return lax.dot_general(a, b, (dims, ((), ())), precision=lax.Precision.HIGHEST, preferred_element_type=F32)
        return lax.dot_general(a.astype(BF16), b.astype(BF16), (dims, ((), ())), preferred_element_type=F32)

    @jax.custom_vjp
    def nn(a, b):
        return raw(a, b, ((1,), (0,)))

    @jax.custom_vjp
    def nt(a, b):
        return raw(a, b, ((1,), (1,)))

    @jax.custom_vjp
    def tn(a, b):
        return raw(a, b, ((0,), (0,)))

    nn.defvjp(lambda a, b: (nn(a, b), (a, b)), lambda r, g: (nt(g, r[1]), tn(r[0], g)))
    nt.defvjp(lambda a, b: (nt(a, b), (a, b)), lambda r, g: (nn(g, r[1]), tn(g, r[0])))
    tn.defvjp(lambda a, b: (tn(a, b), (a, b)), lambda r, g: (nt(r[1], g), nn(r[0], g)))
    return nn, nt, tn


_NN, _NT, _TN = _make_mm2(False)
_NNW, _NTW, _TNW = _make_mm2(True)


def _square_masks(c):
    ri = lax.broadcasted_iota(jnp.int32, (c, c), 0)
    ci = lax.broadcasted_iota(jnp.int32, (c, c), 1)
    return ri >= ci, ri > ci, ri == ci


def _cumsum_rows(m):
    tri, _, _ = _square_masks(m.shape[0])
    return _NNW(tri.astype(F32), m)


def _transpose_sq(m):
    _, _, eye = _square_masks(m.shape[0])
    return _NTW(eye.astype(F32), m)


@jax.custom_vjp
def _inv_unit_lower(l):
    c = l.shape[0]
    _, _, eye = _square_masks(c)
    p = -l
    t = eye.astype(F32) + p
    for _ in range(int(math.log2(c)) - 1):
        p = _NNW(p, p)
        t = t + _NNW(t, p)
    return t


def _inv_fwd(l):
    t = _inv_unit_lower(l)
    return t, t


def _inv_bwd(t, dt):
    return (-_NTW(_TNW(t, dt), t),)


_inv_unit_lower.defvjp(_inv_fwd, _inv_bwd)


@jax.custom_vjp
def _inv_known(l, t):
    return t


_inv_known.defvjp(lambda l, t: (t, t), lambda t, dt: (_inv_bwd(t, dt)[0], jnp.zeros_like(t)))


def _softplus(x):
    return jnp.maximum(x, 0.0) + jnp.log1p(jnp.exp(-jnp.abs(x)))


def _gdn_chunk(q, k, v, braw, araw, alog, dtb, state, inv=None):
    c = q.shape[0]
    dv = v.shape[1]
    tri, strict, _ = _square_masks(c)
    beta = _sig(braw)
    g = -jnp.exp(alog) * _softplus(araw + dtb)
    gcm = _cumsum_rows(g * jnp.ones((c, c), F32))
    gct = _transpose_sq(gcm)
    decay = jnp.where(tri, jnp.exp(jnp.where(tri, gcm - gct, 0.0)), 0.0)
    gc = jnp.sum(gcm, 1, keepdims=True) * (1.0 / c)
    glast = jnp.sum(g, 0, keepdims=True)
    egc = jnp.exp(gc)
    kb = k * beta
    low = jnp.where(strict, _NT(kb, k) * decay, 0.0)
    tm = _inv_unit_lower(low) if inv is None else _inv_known(low, inv)
    sol = _NNW(tm, jnp.concatenate([v * beta, kb * egc], 1))
    u, w = sol[:, :dv], sol[:, dv:]
    attn = jnp.where(tri, _NT(q, k) * decay, 0.0)
    k_dec = k * jnp.exp(glast - gc)
    q_dec = q * egc
    v_new = u - _NN(w, state)
    o = _NN(q_dec, state) + _NN(attn, v_new)
    new_state = state * jnp.exp(glast) + _TN(k_dec, v_new)
    return o, new_state, tm


def _gdn_specs(t, rev):
    nch = t // GDN_CHUNK
    cm = (lambda n: nch - 1 - n) if rev else (lambda n: n)
    tok = pl.BlockSpec((GDN_HEADS, GDN_CHUNK, GDN_DK), lambda n: (0, cm(n), 0))
    par = pl.BlockSpec((GDN_HEADS, 1, LANES), lambda n: (0, 0, 0))
    st = pl.BlockSpec((GDN_HEADS, 1, GDN_DK, GDN_DV), lambda n: (0, cm(n), 0, 0))
    inv = pl.BlockSpec((GDN_HEADS, GDN_CHUNK, GDN_CHUNK), lambda n: (0, cm(n), 0))
    sc = pl.BlockSpec((GDN_CHUNK, LANES), lambda n: (cm(n), 4 * GDN_W // LANES))
    return tok, par, st, inv, sc


def _head_columns(sc_ref, first):
    return jnp.stack([sc_ref[:, first + hh:first + hh + 1] for hh in range(GDN_HEADS)])


def _gdn_core_fwd(q, k, v, h, alog, dtb, *, name):
    t = q.shape[1]
    nch = t // GDN_CHUNK

    def body(q_ref, k_ref, v_ref, sc_ref, al_ref, dt_ref, o_ref, st_ref, inv_ref, state):
        @pl.when(pl.program_id(0) == 0)
        def _():
            state[...] = jnp.zeros_like(state)

        s0 = state[...]
        st_ref[:, 0] = s0
        o, s1, tm = jax.vmap(_gdn_chunk)(q_ref[...], k_ref[...], v_ref[...], _head_columns(sc_ref, 0),
                                         _head_columns(sc_ref, GDN_HEADS), al_ref[:, :, 0:1], dt_ref[:, :, 0:1], s0)
        o_ref[...] = o
        inv_ref[...] = tm
        state[...] = s1

    tok, par, st, inv, sc = _gdn_specs(t, False)
    return pl.pallas_call(
        body, grid=(nch,), in_specs=[tok, tok, tok, sc, par, par], out_specs=[tok, st, inv],
        out_shape=[jax.ShapeDtypeStruct((GDN_HEADS, t, GDN_DV), F32),
                   jax.ShapeDtypeStruct((GDN_HEADS, nch, GDN_DK, GDN_DV), F32),
                   jax.ShapeDtypeStruct((GDN_HEADS, t, GDN_CHUNK), F32)],
        scratch_shapes=[pltpu.VMEM((GDN_HEADS, GDN_DK, GDN_DV), F32)],
        name=name, compiler_params=_cp())(q, k, v, h, alog, dtb)


def _gdn_core_bwd(q, k, v, h, alog, dtb, states, invs, do, *, name):
    t = q.shape[1]
    nch = t // GDN_CHUNK

    def body(q_ref, k_ref, v_ref, sc_ref, al_ref, dt_ref, st_ref, inv_ref, do_ref,
             dq_ref, dk_ref, dv_ref, dsc_ref, dal_ref, ddt_ref, dstate):
        @pl.when(pl.program_id(0) == 0)
        def _():
            dstate[...] = jnp.zeros_like(dstate)
            dal_ref[...] = jnp.zeros_like(dal_ref)
            ddt_ref[...] = jnp.zeros_like(ddt_ref)

        args = (q_ref[...], k_ref[...], v_ref[...], _head_columns(sc_ref, 0), _head_columns(sc_ref, GDN_HEADS),
                al_ref[:, :, 0:1], dt_ref[:, :, 0:1], st_ref[:, 0])
        tm = inv_ref[...]

        def chunk(*a):
            return jax.vmap(_gdn_chunk)(*a, tm)[:2]

        _, pull = jax.vjp(chunk, *args)
        dq, dk, dv, dbr, dar, dal, ddt, ds = pull((do_ref[...], dstate[...]))
        dq_ref[...] = dq
        dk_ref[...] = dk
        dv_ref[...] = dv
        lane = lax.broadcasted_iota(jnp.int32, (GDN_CHUNK, LANES), 1)
        dsc = jnp.zeros((GDN_CHUNK, LANES), F32)
        for hh in range(GDN_HEADS):
            dsc = jnp.where(lane == hh, dbr[hh], dsc)
            dsc = jnp.where(lane == GDN_HEADS + hh, dar[hh], dsc)
        dsc_ref[...] = dsc
        dal_ref[...] += dal + jnp.zeros((GDN_HEADS, 1, LANES), F32)
        ddt_ref[...] += ddt + jnp.zeros((GDN_HEADS, 1, LANES), F32)
        dstate[...] = ds

    tok, par, st, inv, sc = _gdn_specs(t, True)
    tokshape = jax.ShapeDtypeStruct((GDN_HEADS, t, GDN_DK), F32)
    parshape = jax.ShapeDtypeStruct((GDN_HEADS, 1, LANES), F32)
    nch_map = pl.BlockSpec((GDN_CHUNK, LANES), lambda n: (nch - 1 - n, 0))
    return pl.pallas_call(
        body, grid=(nch,), in_specs=[tok, tok, tok, sc, par, par, st, inv, tok],
        out_specs=[tok, tok, tok, nch_map, par, par],
        out_shape=[tokshape] * 3 + [jax.ShapeDtypeStruct((t, LANES), F32)] + [parshape] * 2,
        scratch_shapes=[pltpu.VMEM((GDN_HEADS, GDN_DK, GDN_DV), F32)],
        name=name, compiler_params=_cp())(q, k, v, h, alog, dtb, states, invs, do)


GDN_ROWS = 512


def _gdn_post_fwd(o, h, nw, *, name):
    t = o.shape[1]

    def body(o_ref, g_ref, nw_ref, y_ref):
        oo = o_ref[0]
        r = lax.rsqrt(jnp.mean(oo * oo, -1, keepdims=True) + EPS)
        y_ref[...] = (oo * r * nw_ref[...] * _silu(g_ref[...])).astype(BF16)

    return pl.pallas_call(
        body, grid=(GDN_HEADS, t // GDN_ROWS),
        in_specs=[pl.BlockSpec((1, GDN_ROWS, GDN_DV), lambda hh, i: (hh, i, 0)),
                  pl.BlockSpec((GDN_ROWS, GDN_DV), lambda hh, i: (i, 3 * GDN_HEADS + hh)),
                  pl.BlockSpec((1, GDN_DV), lambda hh, i: (0, 0))],
        out_specs=pl.BlockSpec((GDN_ROWS, GDN_DV), lambda hh, i: (i, hh)),
        out_shape=jax.ShapeDtypeStruct((t, GDN_W), BF16), name=name, compiler_params=_cp())(o, h, nw)


def _gdn_post_bwd(o, h, nw, dy, *, name):
    t = o.shape[1]

    def body(o_ref, g_ref, nw_ref, dy_ref, do_ref, dg_ref, dnw_ref):
        oo, gg, nw_, dy_ = o_ref[0], g_ref[...], nw_ref[...], dy_ref[...]
        r = lax.rsqrt(jnp.mean(oo * oo, -1, keepdims=True) + EPS)
        n = oo * r
        sg = _silu(gg)
        dg_ref[...] = (dy_ * n * nw_ * _dsilu(gg)).astype(BF16)
        dn = dy_ * sg * nw_
        do_ref[0] = r * (dn - n * jnp.mean(dn * n, -1, keepdims=True))

        @pl.when((pl.program_id(0) == 0) & (pl.program_id(1) == 0))
        def _():
            dnw_ref[...] = jnp.zeros_like(dnw_ref)

        dnw_ref[...] += jnp.sum(dy_ * sg * n, 0, keepdims=True)

    return pl.pallas_call(
        body, grid=(GDN_HEADS, t // GDN_ROWS),
        in_specs=[pl.BlockSpec((1, GDN_ROWS, GDN_DV), lambda hh, i: (hh, i, 0)),
                  pl.BlockSpec((GDN_ROWS, GDN_DV), lambda hh, i: (i, 3 * GDN_HEADS + hh)),
                  pl.BlockSpec((1, GDN_DV), lambda hh, i: (0, 0)),
                  pl.BlockSpec((GDN_ROWS, GDN_DV), lambda hh, i: (i, hh))],
        out_specs=[pl.BlockSpec((1, GDN_ROWS, GDN_DV), lambda hh, i: (hh, i, 0)),
                   pl.BlockSpec((GDN_ROWS, GDN_DV), lambda hh, i: (i, hh)),
                   pl.BlockSpec((1, GDN_DV), lambda hh, i: (0, 0))],
        out_shape=[jax.ShapeDtypeStruct((GDN_HEADS, t, GDN_DV), F32), jax.ShapeDtypeStruct((t, GDN_W), BF16),
                   jax.ShapeDtypeStruct((1, GDN_DV), F32)],
        name=name, compiler_params=_cp())(o, h, nw, dy)


def _tables(positions):
    pos = positions.astype(F32)[:, None]
    half = RET_DK // 2
    inv = jnp.power(RET_THETA, -jnp.arange(half, dtype=F32) * 2.0 / RET_DK)
    ang = pos * inv
    cos, sin = jnp.cos(ang), jnp.sin(ang)
    c2a = jnp.concatenate([cos, cos], 1)
    s2a = jnp.concatenate([-sin, sin], 1)
    hb = ROPE_DIMS // 2
    invb = jnp.power(ROPE_THETA, -jnp.arange(hb, dtype=F32) * 2.0 / ROPE_DIMS)
    angb = pos * invb
    cosb, sinb = jnp.cos(angb), jnp.sin(angb)
    t = pos.shape[0]
    ones = jnp.ones((t, DIL_HD - ROPE_DIMS), F32)
    zeros = jnp.zeros((t, DIL_HD - ROPE_DIMS), F32)
    z8 = jnp.zeros((t, hb), F32)
    cb = jnp.concatenate([cosb, cosb, ones] * 2, 1)
    shi = jnp.concatenate([z8, sinb, zeros] * 2, 1)
    slo = jnp.concatenate([-sinb, z8, zeros] * 2, 1)
    lg = jnp.log1p(-jnp.power(2.0, -5.0 - jnp.arange(RET_HEADS, dtype=F32)))
    lgt = jnp.broadcast_to(lg[:, None, None], (RET_HEADS, 1, LANES))
    delta = jnp.arange(ATT_BLK, dtype=jnp.int32)[:, None] + (SEQ - ATT_BLK) - jnp.arange(SEQ, dtype=jnp.int32)[None, :]
    cnt = jnp.zeros(delta.shape, F32)
    for (w, d) in DIL_PAIRS:
        cnt = cnt + ((delta >= 0) & (delta <= w) & (delta % d == 0)).astype(F32)
    strip = jnp.where(cnt > 0, jnp.log(jnp.maximum(cnt, 1.0)), NEG)
    return c2a, s2a, cb, shi, slo, lgt, strip


def _local_step(x, positions, target, get_w, mid, put_g, small):
    c2a, s2a, cb, shi, slo, lgt, strip = _tables(positions)
    t = x.shape[0]
    saved = []
    xf = x
    xb = x.astype(BF16)
    for layer in range(DEPTH):
        j = layer // 2
        L = f"L{layer}_"
        W, dep = get_w(layer, "mixer", xb)
        rec = {"x": xf, "xb": xb}
        if layer % 2 == 0:
            h = _mm(xb, W["in_t"], tb=True, name=L + "ev_in", dep=dep)
            ro, ya = _ret_fwd(h, c2a, s2a, lgt, name=L + "ret_fwd")
            do_, yb, lse = _dil_fwd(h, cb, shi, slo, strip, name=L + "dil_fwd")
            y = jnp.concatenate([ya, yb], 1)
            rec.update(h=h, ro=ro, dil_o=do_, lse=lse, y=y)
        else:
            h = _mm(xb, W["in_t"], tb=True, name=L + "od_in", dep=dep)
            cw = W["conv"]
            q, k, v = _gdn_prep_fwd(h, cw, name=L + "gdn_prep")
            alog = jnp.broadcast_to(small["od_a_log"][j][:, None, None], (GDN_HEADS, 1, LANES))
            dtb = jnp.broadcast_to(small["od_dt_bias"][j][:, None, None], (GDN_HEADS, 1, LANES))
            o, states, invs = _gdn_core_fwd(q, k, v, h, alog, dtb, name=L + "gdn_fwd")
            nw = small["od_norm_w"][j][None, :]
            y = _gdn_post_fwd(o, h, nw, name=L + "gdn_post")
            rec.update(h=h, q=q, k=k, v=v, alog=alog, dtb=dtb, states=states, invs=invs, o=o, y=y, nw=nw, cw=cw)
        z1, x1, x1b = _mm_ln_fwd(y, W["out"], xf, small["ln1_g"][layer][None], small["ln1_b"][layer][None],
                                 name=L + "out_ln1", dep=mid(layer, "mixer", y))
        rec["Wm"] = W
        W, dep = get_w(layer, "ffn", x1b)
        rec["Wf"] = W
        fcw = W["fconv"]
        fcb = small["ffn_conv_b"][layer][None]
        ug, uv, a = _ffn_up_mid(x1b, W["up_t"], fcw, fcb, name=L + "ffn_up_mid", dep=dep)
        z2, x2, x2b = _mm_ln_fwd(a, W["down"], x1, small["ln2_g"][layer][None], small["ln2_b"][layer][None],
                                 name=L + "down_ln2", dep=mid(layer, "ffn", a))
        rec.update(z1=z1, x1b=x1b, ug=ug, uv=uv, a=a, z2=z2, fcw=fcw, fcb=fcb)
        saved.append(rec)
        xf, xb = x2, x2b

    dy, lossv = _loss_head(xf, target, name="loss_head")
    loss = lossv[0, 0]

    gS = {n: [None] * small[n].shape[0] for n in small}
    below = None
    for layer in reversed(range(DEPTH)):
        j = layer // 2
        L = f"L{layer}_"
        rec = saved[layer]
        Wm, Wf = rec["Wm"], rec["Wf"]
        g = {}
        if below is None:
            dz2, dz2b, dg2, db2 = _ln_bwd(rec["z2"], small["ln2_g"][layer][None], dy, None, name=L + "ln2_bwd")
        else:
            dz2, dz2b, dg2, db2 = _mm_ln_bwd(below[0], below[1], rec["z2"], small["ln2_g"][layer][None], below[2],
                                             name=L + "ln2_bwd", dep=below[3])
        gS["ln2_g"][layer], gS["ln2_b"][layer] = dg2[0], db2[0]
        g["down"] = _mm(rec["a"], dz2b, ta=True, name=L + "ffn_down_dw", out_dtype=BF16)
        du, dcw, dcb = _ffn_mid_bwd(rec["ug"], rec["uv"], rec["fcw"], rec["fcb"], dz2b, Wf["down"], name=L + "ffn_mid_bwd")
        g["fconv"] = dcw.astype(BF16)
        gS["ffn_conv_b"][layer] = dcb[0]
        g["up_t"] = _mm(du, rec["x1b"], ta=True, name=L + "ffn_up_dw", out_dtype=BF16)
        dep = put_g(layer, "ffn", g)
        dz1, dz1b, dg1, db1 = _mm_ln_bwd(du, Wf["up_t"], rec["z1"], small["ln1_g"][layer][None], dz2,
                                         name=L + "ln1_bwd", dep=dep)
        gS["ln1_g"][layer], gS["ln1_b"][layer] = dg1[0], db1[0]
        g = {}
        if layer % 2 == 0:
            g["out"] = _mm(rec["y"], dz1b, ta=True, name=L + "ev_out_dw", out_dtype=BF16)
            dyy = _mm(dz1b, Wm["out"], tb=True, name=L + "ev_out_dx")
            dqa, dka, dva, dga = _ret_bwd(rec["h"], c2a, s2a, lgt, rec["ro"], dyy, name=L + "ret_bwd")
            dqb, dkb, dvb = _dil_bwd(rec["h"], cb, shi, slo, strip, rec["dil_o"], rec["lse"], dyy, name=L + "dil_bwd")
            dh = jnp.concatenate([dqa, dka, dva, dga, dqb, dkb, dvb], 1)
            g["in_t"] = _mm(dh, rec["xb"], ta=True, name=L + "ev_in_dw", out_dtype=BF16)
            dep = put_g(layer, "mixer", g)
        else:
            g["out"] = _mm(rec["y"], dz1b, ta=True, name=L + "od_out_dw", out_dtype=BF16)
            dyy = _mm(dz1b, Wm["out"], tb=True, name=L + "od_out_dx")
            do, dgate, dnw = _gdn_post_bwd(rec["o"], rec["h"], rec["nw"], dyy, name=L + "gdn_post_bwd")
            gS["od_norm_w"][j] = dnw[0]
            dq, dk, dv, dsc, dal, ddt = _gdn_core_bwd(
                rec["q"], rec["k"], rec["v"], rec["h"], rec["alog"], rec["dtb"], rec["states"], rec["invs"], do,
                name=L + "gdn_bwd")
            gS["od_a_log"][j] = dal[:, 0, 0]
            gS["od_dt_bias"][j] = ddt[:, 0, 0]
            dhq, dhk, dhv, dwq, dwk, dwv = _gdn_prep_bwd(rec["h"], rec["cw"], dq, dk, dv, name=L + "gdn_prep_bwd")
            g["conv"] = jnp.concatenate([dwq, dwk, dwv], 1).astype(BF16)
            dh = jnp.concatenate([dhq, dhk, dhv, dgate, dsc.astype(BF16)], 1)
            g["in_t"] = _mm(dh, rec["xb"], ta=True, name=L + "od_in_dw", out_dtype=BF16)
            dep = put_g(layer, "mixer", g)
        below = (dh, Wm["in_t"], dz1, dep)
    grad_x = _axpy(_mm(below[0], below[1], name="L0_in_dx", dep=below[3]), below[2], name="grad_x")
    gS = {n: jnp.stack(v) for n, v in gS.items()}
    return loss, grad_x, gS


HBM = pl.BlockSpec(memory_space=pltpu.HBM)


def _me():
    return lax.axis_index("x"), lax.axis_index("y"), lax.axis_index("c")


def _my_index():
    x, y, c = _me()
    return 4 * x + 2 * y + c


SEM = pl.BlockSpec(memory_space=pltpu.SEMAPHORE)
ANY = pl.BlockSpec(memory_space=pl.ANY)
PLANS = {"scatter": (1, 2, 3, 4, 5, 6, 7), "spread": (1, 2, 4, 6), "relay": (2, 4, 6)}
SIBLING = 1


def _peer(kk):
    x, y, c = _me()
    return x ^ (kk >> 2), y ^ ((kk >> 1) & 1), c ^ (kk & 1)


def _peer_index(kk):
    px, py, pc = _peer(kk)
    return 4 * px + 2 * py + pc


def _job_copies(mode, srcs, lands, send_sems, recv_sems, incoming):
    myid = _my_index()
    plan = PLANS[mode]
    out = []
    for a in range(len(lands)):
        for idx, kk in enumerate(plan):
            if mode == "relay":
                to, src = _peer(SIBLING), lands[a].at[_peer_index(kk)]
                slot_there, slot_here = _peer_index(kk), _peer_index(kk ^ SIBLING)
            else:
                to, src = _peer(kk), (srcs[a] if mode == "spread" else srcs[a].at[_peer_index(kk)])
                slot_there, slot_here = myid, _peer_index(kk)
            sem = a * len(plan) + idx
            out.append(pltpu.make_async_remote_copy(
                src_ref=src, dst_ref=lands[a].at[slot_here if incoming else slot_there],
                send_sem=send_sems.at[sem], recv_sem=recv_sems.at[sem], device_id=to, device_id_type=MESH))
    return out


def _split_jobs(jobs, arrays):
    out, o = [], 0
    for (_, srcs, lands) in jobs:
        out.append((arrays[o:o + len(srcs)], arrays[o + len(srcs):o + len(srcs) + len(lands)]))
        o += len(srcs) + len(lands)
    return out


def _exchange_start(jobs, after, *, name):
    jobs = [(mode, list(srcs), [lax.empty((N_DEV, *s.shape) if mode == "spread" else s.shape, s.dtype) for s in srcs]
             if lands is None else list(lands)) for (mode, srcs, lands) in jobs]
    flat = [a for (_, srcs, lands) in jobs for a in (*srcs, *lands)]
    n, nj = len(flat), len(jobs)
    nsem = [len(PLANS[mode]) * len(lands) for (mode, _, lands) in jobs]

    def body(*refs):
        o = n + (0 if after is None else 1)
        sems, token = refs[o:o + 2 * nj], refs[o + 2 * nj + n]
        for ji, ((mode, _, _), (src, land)) in enumerate(zip(jobs, _split_jobs(jobs, refs[:n]))):
            for cp in _job_copies(mode, src, land, sems[2 * ji], sems[2 * ji + 1], False):
                cp.start()
        token[...] = jnp.zeros_like(token)

    outs = pl.pallas_call(
        body, name=name,
        out_shape=(*[pltpu.SemaphoreType.DMA((ns,)) for ns in nsem for _ in range(2)],
                   *[pltpu.HBM(a.shape, a.dtype) for a in flat], jax.ShapeDtypeStruct((8, LANES), F32)),
        in_specs=[HBM] * n + ([] if after is None else [ANY]),
        out_specs=(*[SEM] * (2 * nj), *[HBM] * n, pl.BlockSpec(memory_space=pltpu.VMEM)),
        input_output_aliases={i: 2 * nj + i for i in range(n)},
        compiler_params=pltpu.CompilerParams(has_side_effects=pltpu.SideEffectType.DATAFLOW_SIDE_EFFECTING),
    )(*[pltpu.with_memory_space_constraint(a, pltpu.HBM) for a in flat], *([] if after is None else [after]))
    thru = _split_jobs(jobs, list(outs[2 * nj:2 * nj + n]))
    started = [(mode, outs[2 * ji], outs[2 * ji + 1], src, land) for ji, ((mode, _, _), (src, land)) in enumerate(zip(jobs, thru))]
    return started, outs[2 * nj + n]


def _exchange_wait(started, after, *, name):
    jobs = [(mode, srcs, lands) for (mode, _, _, srcs, lands) in started]
    flat = [a for (_, srcs, lands) in jobs for a in (*srcs, *lands)]
    n, nj = len(flat), len(jobs)

    def body(*refs):
        sems = refs[n:n + 2 * nj]
        for ji, ((mode, _, _), (src, land)) in enumerate(zip(jobs, _split_jobs(jobs, refs[:n]))):
            for cp in _job_copies(mode, src, land, sems[2 * ji], sems[2 * ji + 1], True):
                cp.wait_send()
                cp.wait_recv()

    outs = pl.pallas_call(
        body, name=name, out_shape=tuple(pltpu.HBM(a.shape, a.dtype) for a in flat),
        in_specs=[HBM] * n + [SEM] * (2 * nj) + [ANY], out_specs=tuple([HBM] * n),
        input_output_aliases={i: i for i in range(n)},
        compiler_params=pltpu.CompilerParams(has_side_effects=pltpu.SideEffectType.DATAFLOW_SIDE_EFFECTING),
    )(*flat, *[s for (_, ss, rs, _, _) in started for s in (ss, rs)], after)
    return _split_jobs(jobs, list(outs))


def _sum8(land, *, name):
    _, rr, cc = land.shape
    tr = _row_tile(rr)

    def body(l_ref, o_ref):
        acc = l_ref[0].astype(F32)
        for d in range(1, N_DEV):
            acc = acc + l_ref[d].astype(F32)
        o_ref[...] = acc

    return pl.pallas_call(
        body, grid=(rr // tr,), in_specs=[pl.BlockSpec((N_DEV, tr, cc), lambda i: (0, i, 0))],
        out_specs=pl.BlockSpec((tr, cc), lambda i: (i, 0)), out_shape=jax.ShapeDtypeStruct((rr, cc), F32),
        name=name, compiler_params=_cp())(land)


def _row_tile(rr):
    for cand in (512, 384, 256, 192, 176, 128, 64, 32, 16, 8):
        if rr % cand == 0:
            return cand
    return rr


def _small_exchange(vec, *, name):
    rr = vec.shape[0]

    def body(v_ref, o_ref, send_sems, recv_sems):
        x, y, c = _me()
        myid = 4 * x + 2 * y + c
        o_ref[myid] = v_ref[...]
        cps = []
        for kk in range(1, N_DEV):
            px, py, pc = x ^ (kk >> 2), y ^ ((kk >> 1) & 1), c ^ (kk & 1)
            cps.append(pltpu.make_async_remote_copy(
                src_ref=v_ref, dst_ref=o_ref.at[myid], send_sem=send_sems.at[kk], recv_sem=recv_sems.at[kk],
                device_id=(px, py, pc), device_id_type=MESH))
        for cp in cps:
            cp.start()
        for kk in range(1, N_DEV):
            px, py, pc = x ^ (kk >> 2), y ^ ((kk >> 1) & 1), c ^ (kk & 1)
            pltpu.make_async_remote_copy(
                src_ref=v_ref, dst_ref=o_ref.at[4 * px + 2 * py + pc], send_sem=send_sems.at[kk],
                recv_sem=recv_sems.at[kk], device_id=(px, py, pc), device_id_type=MESH).wait_recv()
        for cp in cps:
            cp.wait_send()

    return pl.pallas_call(
        body, in_specs=[pl.BlockSpec(memory_space=pltpu.VMEM)], out_specs=pl.BlockSpec(memory_space=pltpu.VMEM),
        out_shape=jax.ShapeDtypeStruct((N_DEV, rr, LANES), F32),
        scratch_shapes=[pltpu.SemaphoreType.DMA((N_DEV,)), pltpu.SemaphoreType.DMA((N_DEV,))],
        name=name, compiler_params=pltpu.CompilerParams(has_side_effects=True))(vec)


def _adam_math(w, g, m, v):
    m = ADAM_B1 * m + (1.0 - ADAM_B1) * g
    v = ADAM_B2 * v + (1.0 - ADAM_B2) * (g * g)
    m_hat = m / (1.0 - ADAM_B1 ** ADAM_STEP)
    v_hat = v / (1.0 - ADAM_B2 ** ADAM_STEP)
    delta = -ADAM_LR * (m_hat / (jnp.sqrt(v_hat) + ADAM_EPS) + ADAM_WD * w)
    return delta, m, v


def _adamw_sharded(w, m, v, g, *, name):
    ll, rr, cc = w.shape
    tr = _row_tile(rr)

    def body(w_ref, m_ref, v_ref, g_ref, d_ref, nm_ref, nv_ref):
        d, nm, nv = _adam_math(w_ref[...], g_ref[...], m_ref[...], v_ref[...])
        d_ref[...] = d
        nm_ref[...] = nm
        nv_ref[...] = nv

    blk = pl.BlockSpec((1, tr, cc), lambda l, i: (l, i, 0))
    sh = jax.ShapeDtypeStruct((ll, rr, cc), F32)
    return pl.pallas_call(
        body, grid=(ll, rr // tr), in_specs=[blk] * 4, out_specs=[blk] * 3, out_shape=[sh] * 3,
        name=name, compiler_params=_cp())(w, m, v, g)


def _adamw_small(w, m, v, gall, *, name):
    rr = w.shape[0]

    def body(w_ref, m_ref, v_ref, g_ref, go_ref, d_ref, nm_ref, nv_ref):
        g = g_ref[0]
        for kk in range(1, N_DEV):
            g = g + g_ref[kk]
        d, nm, nv = _adam_math(w_ref[...], g, m_ref[...], v_ref[...])
        go_ref[...] = g
        d_ref[...] = d
        nm_ref[...] = nm
        nv_ref[...] = nv

    sh = jax.ShapeDtypeStruct((rr, LANES), F32)
    return pl.pallas_call(body, out_shape=[sh] * 4, name=name, compiler_params=_cp())(w, m, v, gall)


SHARDED = ("ev_w_in", "ev_w_out", "od_w_in", "od_conv_w", "od_w_out", "ffn_w_up", "ffn_conv_w", "ffn_w_down")
SMALL = ("od_a_log", "od_dt_bias", "od_norm_w", "ffn_conv_b", "ln1_g", "ln1_b", "ln2_g", "ln2_b")
ALL_W = ("ev_w_in", "ev_w_out", "od_w_in", "od_conv_w", "od_a_log", "od_dt_bias", "od_norm_w", "od_w_out",
         "ffn_w_up", "ffn_conv_w", "ffn_conv_b", "ffn_w_down", "ln1_g", "ln1_b", "ln2_g", "ln2_b")


def _layer_items(layer):
    j = layer // 2
    if layer % 2 == 0:
        mixer = [("in_t", "ev_w_in", j, "colT"), ("out", "ev_w_out", j, "row")]
    else:
        mixer = [("in_t", "od_w_in", j, "colT"), ("conv", "od_conv_w", j, "colsmall"), ("out", "od_w_out", j, "row")]
    return mixer + [("up_t", "ffn_w_up", layer, "colT"), ("fconv", "ffn_conv_w", layer, "colsmall"),
                    ("down", "ffn_w_down", layer, "row")]


def _to_send(kind, w, j):
    if kind == "colT":
        return w[j].T.astype(BF16)
    return w[j].astype(BF16) if kind == "row" else w[j]


def _from_gather(kind, name, g):
    if kind == "colsmall":
        return jnp.transpose(g, (1, 0, 2)).reshape(g.shape[1], -1)
    full = g.reshape(-1, g.shape[-1])
    if name == "od_w_in":
        full = jnp.pad(full, ((0, OD_IN_PAD - OD_IN), (0, 0)))
    return full


def _by_owner(kind, name, gfull):
    if kind == "colsmall":
        kk, c8 = gfull.shape
        return jnp.transpose(gfull.reshape(kk, N_DEV, c8 // N_DEV), (1, 0, 2))
    if name == "od_w_in":
        gfull = gfull[:OD_IN]
    return gfull.reshape(N_DEV, gfull.shape[0] // N_DEV, gfull.shape[1])


def _pack_small(d):
    flat = jnp.concatenate([d[n].reshape(-1) for n in SMALL])
    pad = (-flat.shape[0]) % (8 * LANES)
    return jnp.pad(flat, (0, pad)).reshape(-1, LANES)


def _unpack_small(packed, like):
    flat = packed.reshape(-1)
    out, off = {}, 0
    for n in SMALL:
        sz = int(np.prod(like[n].shape))
        out[n] = flat[off:off + sz].reshape(like[n].shape)
        off += sz
    return out


def kernel(x, positions, ev_w_in, ev_w_out, od_w_in, od_conv_w, od_a_log, od_dt_bias, od_norm_w, od_w_out, ffn_w_up, ffn_conv_w, ffn_conv_b, ffn_w_down, ln1_g, ln1_b, ln2_g, ln2_b, loss_target, m_ev_w_in, m_ev_w_out, m_od_w_in, m_od_conv_w, m_od_a_log, m_od_dt_bias, m_od_norm_w, m_od_w_out, m_ffn_w_up, m_ffn_conv_w, m_ffn_conv_b, m_ffn_w_down, m_ln1_g, m_ln1_b, m_ln2_g, m_ln2_b, v_ev_w_in, v_ev_w_out, v_od_w_in, v_od_conv_w, v_od_a_log, v_od_dt_bias, v_od_norm_w, v_od_w_out, v_ffn_w_up, v_ffn_conv_w, v_ffn_conv_b, v_ffn_w_down, v_ln1_g, v_ln1_b, v_ln2_g, v_ln2_b):
    w = dict(ev_w_in=ev_w_in, ev_w_out=ev_w_out, od_w_in=od_w_in, od_conv_w=od_conv_w, od_a_log=od_a_log,
             od_dt_bias=od_dt_bias, od_norm_w=od_norm_w, od_w_out=od_w_out, ffn_w_up=ffn_w_up, ffn_conv_w=ffn_conv_w,
             ffn_conv_b=ffn_conv_b, ffn_w_down=ffn_w_down, ln1_g=ln1_g, ln1_b=ln1_b, ln2_g=ln2_g, ln2_b=ln2_b)
    mom = dict(ev_w_in=m_ev_w_in, ev_w_out=m_ev_w_out, od_w_in=m_od_w_in, od_conv_w=m_od_conv_w, od_a_log=m_od_a_log,
               od_dt_bias=m_od_dt_bias, od_norm_w=m_od_norm_w, od_w_out=m_od_w_out, ffn_w_up=m_ffn_w_up,
               ffn_conv_w=m_ffn_conv_w, ffn_conv_b=m_ffn_conv_b, ffn_w_down=m_ffn_w_down, ln1_g=m_ln1_g,
               ln1_b=m_ln1_b, ln2_g=m_ln2_g, ln2_b=m_ln2_b)
    var = dict(ev_w_in=v_ev_w_in, ev_w_out=v_ev_w_out, od_w_in=v_od_w_in, od_conv_w=v_od_conv_w, od_a_log=v_od_a_log,
               od_dt_bias=v_od_dt_bias, od_norm_w=v_od_norm_w, od_w_out=v_od_w_out, ffn_w_up=v_ffn_w_up,
               ffn_conv_w=v_ffn_conv_w, ffn_conv_b=v_ffn_conv_b, ffn_w_down=v_ffn_w_down, ln1_g=v_ln1_g,
               ln1_b=v_ln1_b, ln2_g=v_ln2_g, ln2_b=v_ln2_b)

    myid = _my_index()
    small = {n: w[n] for n in SMALL}
    groups = [(layer, part) for layer in range(DEPTH) for part in ("mixer", "ffn")]

    def group_items(gi):
        layer, part = groups[gi]
        its = _layer_items(layer)
        return its[:-3] if part == "mixer" else its[-3:]

    level1, level2 = {}, {}

    def spread_job(gi):
        return ("spread", [_to_send(kind, w[n], j) for (_, n, j, kind) in group_items(gi)], None)

    def relay(gi, after, name):
        (srcs, lands), = _exchange_wait([level1.pop(gi)], after, name=name + "_wait")
        more = [spread_job(gi + 1)] if gi + 1 < len(groups) else []
        started, token = _exchange_start([("relay", [], lands)] + more, None, name=name + "_start")
        level2[gi] = (started[0], srcs)
        if more:
            level1[gi + 1] = started[1]
        return token

    def get_w(layer, part, after):
        gi = groups.index((layer, part))
        started, srcs = level2.pop(gi)
        (_, lands), = _exchange_wait([started], after, name=f"gather{gi}_wait")
        lands = [lax.dynamic_update_index_in_dim(l, s, myid, 0) for l, s in zip(lands, srcs)]
        return {key: _from_gather(kind, n, l) for (key, n, _, kind), l in zip(group_items(gi), lands)}, None

    def mid(layer, part, after):
        gi = groups.index((layer, part)) + 1
        return relay(gi, after, f"gather{gi}_relay") if gi < len(groups) else None

    landed = {}
    pending = []

    def scatter_finish(after):
        started, gi = pending.pop()
        (srcs, lands), = _exchange_wait([started], after, name=f"scatter{gi}_wait")
        for (key, _, _, _), l, s in zip(group_items(gi), lands, srcs):
            own = lax.dynamic_index_in_dim(s, myid, 0, keepdims=False)
            landed[(groups[gi][0], key)] = lax.dynamic_update_index_in_dim(l, own, myid, 0)

    def put_g(layer, part, g):
        gi = groups.index((layer, part))
        srcs = [_by_owner(kind, n, g[key]) for (key, n, _, kind) in group_items(gi)]
        (started,), token = _exchange_start([("scatter", srcs, None)], None, name=f"scatter{gi}_start")
        if pending:
            scatter_finish(token)
        pending.append((started, gi))
        return token

    (level1[0],), token = _exchange_start([spread_job(0)], None, name="gather0_spread_start")
    relay(0, token, "gather0_relay")
    loss, grad_x, gS = _local_step(x[0], positions[0], loss_target[0], get_w, mid, put_g, small)
    loss = lax.psum(loss, ("x", "y", "c"))

    outs_g, outs_d, outs_m, outs_v = {}, {}, {}, {}
    where = {n: [None] * w[n].shape[0] for n in SHARDED}
    for layer in range(DEPTH):
        for (key, n, j, kind) in _layer_items(layer):
            where[n][j] = (layer, key, kind)

    def update(n):
        g = jnp.stack([_sum8(landed[(layer, key)], name=f"L{layer}_{key}_sum") for (layer, key, _) in where[n]])
        if where[n][0][2] == "colT":
            tr = lambda a: jnp.swapaxes(a, 1, 2)
            d, nm, nv = _adamw_sharded(tr(w[n]), tr(mom[n]), tr(var[n]), g, name=f"adamw_{n}")
            outs_g[n], outs_d[n], outs_m[n], outs_v[n] = tr(g), tr(d), tr(nm), tr(nv)
        else:
            outs_g[n] = g
            outs_d[n], outs_m[n], outs_v[n] = _adamw_sharded(w[n], mom[n], var[n], g, name=f"adamw_{n}")

    last = {n for (_, n, _, _) in group_items(pending[0][1])}
    for n in SHARDED:
        if n not in last:
            update(n)
    scatter_finish(outs_d[[n for n in SHARDED if n not in last][-1]])
    for n in SHARDED:
        if n in last:
            update(n)

    gall = _small_exchange(_pack_small(gS), name="small_grads_exchange")
    g, d, nm, nv = _adamw_small(_pack_small({n: w[n] for n in SMALL}), _pack_small({n: mom[n] for n in SMALL}),
                                _pack_small({n: var[n] for n in SMALL}), gall, name="adamw_small")
    for dst, packed in ((outs_g, g), (outs_d, d), (outs_m, nm), (outs_v, nv)):
        dst.update(_unpack_small(packed, {n: w[n] for n in SMALL}))

    return (loss, grad_x[None], *[outs_g[n] for n in ALL_W], *[outs_d[n] for n in ALL_W],
            *[outs_m[n] for n in ALL_W], *[outs_v[n] for n in ALL_W])
```

```python
import functools
import math

import numpy as np
import jax
import jax.numpy as jnp
from jax import lax
from jax.experimental import pallas as pl
from jax.experimental.pallas import tpu as pltpu

F32 = jnp.float32
BF16 = jnp.bfloat16
MESH = pl.DeviceIdType.MESH

D_MODEL = 1024
SEQ = 2048
DEPTH = 4
N_DEV = 8
RET_HEADS, RET_DK, RET_DV = 4, 128, 256
RET_THETA = 10000.0
DIL_HEADS, DIL_HD = 8, 64
DIL_PAIRS = ((128, 1), (512, 4), (2048, 16))
ROPE_THETA = 500000.0
ROPE_DIMS = DIL_HD // 4
GDN_HEADS, GDN_DK, GDN_DV, GDN_CHUNK, GDN_CONV = 8, 128, 128, 64, 4
D_FF = 2816
FFN_CONV = 3
ALPHA = (2.0 * DEPTH) ** 0.25
EPS = 1e-5
RET_QK_W = RET_HEADS * RET_DK
RET_V_W = RET_HEADS * RET_DV
DIL_W = DIL_HEADS * DIL_HD
EV_IN = 2 * RET_QK_W + 2 * RET_V_W + 3 * DIL_W
EV_MIX = RET_V_W + DIL_W
GDN_W = GDN_HEADS * GDN_DK
OD_IN = 4 * GDN_W + 2 * GDN_HEADS
OD_IN_PAD = 4 * GDN_W + 128
ADAM_LR, ADAM_B1, ADAM_B2, ADAM_EPS, ADAM_WD, ADAM_STEP = 0.001, 0.9, 0.999, 1e-08, 0.01, 10

LANES = 128
VMEM_LIMIT = 56 * 1024 * 1024
ATT_BLK = 256
NEG = -1e30


def _cp(**kw):
    return pltpu.CompilerParams(vmem_limit_bytes=VMEM_LIMIT, **kw)


def _tile(n, cap):
    if n <= cap:
        return n
    best = None
    for t in range(LANES, cap + 1, LANES):
        if n % t == 0:
            best = t
    assert best is not None, (n, cap)
    return best


def _mm(a, b, *, ta=False, tb=False, name, out_dtype=F32, dep=None, tm=None, tn=None):
    m = a.shape[1] if ta else a.shape[0]
    k = a.shape[0] if ta else a.shape[1]
    n = b.shape[0] if tb else b.shape[1]
    assert (b.shape[1] if tb else b.shape[0]) == k
    assert a.dtype == BF16 and b.dtype == BF16
    if tn is None:
        tn = n if n <= 1024 else _tile(n, 512)
    if tm is None:
        tm = m if (tn < n and k <= 1024 and m <= 2048) else _tile(m, 512)
    dims = (((0 if ta else 1,), (1 if tb else 0,)), ((), ()))

    def body(a_ref, b_ref, *rest):
        o_ref = rest[-1]
        o_ref[...] = lax.dot_general(a_ref[...], b_ref[...], dims,
                                     preferred_element_type=F32).astype(o_ref.dtype)

    a_spec = pl.BlockSpec((k, tm), lambda i, j: (0, i)) if ta else pl.BlockSpec((tm, k), lambda i, j: (i, 0))
    b_spec = pl.BlockSpec((tn, k), lambda i, j: (j, 0)) if tb else pl.BlockSpec((k, tn), lambda i, j: (0, j))
    extra = [] if dep is None else [dep]
    return pl.pallas_call(
        body, grid=(m // tm, n // tn), in_specs=[a_spec, b_spec] + [pl.BlockSpec(memory_space=pl.ANY)] * len(extra),
        out_specs=pl.BlockSpec((tm, tn), lambda i, j: (i, j)),
        out_shape=jax.ShapeDtypeStruct((m, n), out_dtype), name=name, compiler_params=_cp())(a, b, *extra)


LN_ROWS = 256


def _ln_bwd(z, g, dya, dyb, *, name):
    t, d = z.shape
    two = dyb is not None

    def body(*refs):
        if two:
            z_ref, g_ref, dya_ref, dyb_ref, dz_ref, dzb_ref, dg_ref, db_ref = refs
            dy = dya_ref[...] + ALPHA * dyb_ref[...]
        else:
            z_ref, g_ref, dya_ref, dz_ref, dzb_ref, dg_ref, db_ref = refs
            dy = dya_ref[...]
        zz = z_ref[...]
        mu = jnp.mean(zz, -1, keepdims=True)
        zc = zz - mu
        var = jnp.mean(zc * zc, -1, keepdims=True)
        r = lax.rsqrt(var + EPS)
        xh = zc * r
        dxh = dy * g_ref[...]
        dz = r * (dxh - jnp.mean(dxh, -1, keepdims=True) - xh * jnp.mean(dxh * xh, -1, keepdims=True))
        dz_ref[...] = dz
        dzb_ref[...] = dz.astype(BF16)

        @pl.when(pl.program_id(0) == 0)
        def _():
            dg_ref[...] = jnp.zeros_like(dg_ref)
            db_ref[...] = jnp.zeros_like(db_ref)

        dg_ref[...] += jnp.sum(dy * xh, 0, keepdims=True)
        db_ref[...] += jnp.sum(dy, 0, keepdims=True)

    row = pl.BlockSpec((LN_ROWS, d), lambda i: (i, 0))
    vec = pl.BlockSpec((1, d), lambda i: (0, 0))
    ins = [z, g, dya] + ([dyb] if two else [])
    return pl.pallas_call(
        body, grid=(t // LN_ROWS,), in_specs=[row, vec, row] + ([row] if two else []),
        out_specs=[row, row, vec, vec],
        out_shape=[jax.ShapeDtypeStruct((t, d), F32), jax.ShapeDtypeStruct((t, d), BF16),
                   jax.ShapeDtypeStruct((1, d), F32), jax.ShapeDtypeStruct((1, d), F32)],
        name=name, compiler_params=_cp())(*ins)


def _ln_rows(k):
    return 256 if k > 4096 else 512


def _mm_ln_fwd(a, w, x, g, b, *, name, dep=None):
    t, k = a.shape
    d = w.shape[1]
    tm = _ln_rows(k)

    def body(a_ref, w_ref, x_ref, g_ref, b_ref, *rest):
        z_ref, y_ref, yb_ref = rest[-3:]
        z = ALPHA * x_ref[...] + _nn(a_ref[...], w_ref[...])
        mu = jnp.mean(z, -1, keepdims=True)
        zc = z - mu
        var = jnp.mean(zc * zc, -1, keepdims=True)
        y = zc * lax.rsqrt(var + EPS) * g_ref[...] + b_ref[...]
        z_ref[...] = z
        y_ref[...] = y
        yb_ref[...] = y.astype(BF16)

    row = pl.BlockSpec((tm, d), lambda i: (i, 0))
    vec = pl.BlockSpec((1, d), lambda i: (0, 0))
    extra = [] if dep is None else [dep]
    return pl.pallas_call(
        body, grid=(t // tm,),
        in_specs=[pl.BlockSpec((tm, k), lambda i: (i, 0)), pl.BlockSpec((k, d), lambda i: (0, 0)), row, vec, vec]
        + [pl.BlockSpec(memory_space=pl.ANY)] * len(extra),
        out_specs=[row, row, row],
        out_shape=[jax.ShapeDtypeStruct((t, d), F32), jax.ShapeDtypeStruct((t, d), F32), jax.ShapeDtypeStruct((t, d), BF16)],
        name=name, compiler_params=_cp())(a, w, x, g, b, *extra)


def _part_offsets(parts):
    offs, o = [], 0
    for p in parts:
        assert o % p.shape[1] == 0
        offs.append(o)
        o += p.shape[1]
    return offs, o


def _mm_ln_bwd(parts, w, z, g, dyb, *, name, dep=None):
    t = parts[0].shape[0]
    offs, k = _part_offsets(parts)
    d = w.shape[1]
    tm = _ln_rows(k)
    npart = len(parts)

    def body(*refs):
        a_refs, w_refs = refs[:npart], refs[npart:2 * npart]
        z_ref, g_ref, dyb_ref = refs[2 * npart:2 * npart + 3]
        dz_ref, dzb_ref, dg_ref, db_ref = refs[-4:]
        dy = ALPHA * dyb_ref[...]
        for a_ref, w_ref in zip(a_refs, w_refs):
            dy = dy + _nn(a_ref[...], w_ref[...])
        zz = z_ref[...]
        mu = jnp.mean(zz, -1, keepdims=True)
        zc = zz - mu
        var = jnp.mean(zc * zc, -1, keepdims=True)
        r = lax.rsqrt(var + EPS)
        xh = zc * r
        dxh = dy * g_ref[...]
        dz = r * (dxh - jnp.mean(dxh, -1, keepdims=True) - xh * jnp.mean(dxh * xh, -1, keepdims=True))
        dz_ref[...] = dz
        dzb_ref[...] = dz.astype(BF16)

        @pl.when(pl.program_id(0) == 0)
        def _():
            dg_ref[...] = jnp.zeros_like(dg_ref)
            db_ref[...] = jnp.zeros_like(db_ref)

        dg_ref[...] += jnp.sum(dy * xh, 0, keepdims=True)
        db_ref[...] += jnp.sum(dy, 0, keepdims=True)

    row = pl.BlockSpec((tm, d), lambda i: (i, 0))
    vec = pl.BlockSpec((1, d), lambda i: (0, 0))
    extra = [] if dep is None else [dep]
    a_specs = [pl.BlockSpec((tm, p.shape[1]), lambda i: (i, 0)) for p in parts]
    w_specs = [pl.BlockSpec((p.shape[1], d), functools.partial(lambda i, blk: (blk, 0), blk=o // p.shape[1]))
               for p, o in zip(parts, offs)]
    return pl.pallas_call(
        body, grid=(t // tm,),
        in_specs=a_specs + w_specs + [row, vec, row] + [pl.BlockSpec(memory_space=pl.ANY)] * len(extra),
        out_specs=[row, row, vec, vec],
        out_shape=[jax.ShapeDtypeStruct((t, d), F32), jax.ShapeDtypeStruct((t, d), BF16),
                   jax.ShapeDtypeStruct((1, d), F32), jax.ShapeDtypeStruct((1, d), F32)],
        name=name, compiler_params=_cp())(*parts, *([w] * npart), z, g, dyb, *extra)


def _mm_tn_parts(parts, b, *, name):
    t, n = b.shape
    offs, m = _part_offsets(parts)
    tm = min(512, min(_tile(p.shape[1], 512) for p in parts))
    assert all(p.shape[1] % tm == 0 for p in parts)
    first = [o // tm for o in offs]
    count = [p.shape[1] // tm for p in parts]
    npart = len(parts)

    def body(*refs):
        a_refs, b_ref, o_ref = refs[:npart], refs[npart], refs[npart + 1]
        i = pl.program_id(0)
        for a_ref, f, c in zip(a_refs, first, count):
            @pl.when((i >= f) & (i < f + c))
            def _(a_ref=a_ref):
                o_ref[...] = _tn(a_ref[...], b_ref[...]).astype(BF16)

    a_specs = [pl.BlockSpec((t, tm), functools.partial(lambda i, f, c: (0, jnp.clip(i - f, 0, c - 1)), f=f, c=c))
               for f, c in zip(first, count)]
    return pl.pallas_call(
        body, grid=(m // tm,), in_specs=a_specs + [pl.BlockSpec((t, n), lambda i: (0, 0))],
        out_specs=pl.BlockSpec((tm, n), lambda i: (i, 0)), out_shape=jax.ShapeDtypeStruct((m, n), BF16),
        name=name, compiler_params=_cp())(*parts, b)


def _axpy(a, b, *, name):
    t, d = a.shape

    def body(a_ref, b_ref, o_ref):
        o_ref[...] = a_ref[...] + ALPHA * b_ref[...]

    row = pl.BlockSpec((LN_ROWS, d), lambda i: (i, 0))
    return pl.pallas_call(body, grid=(t // LN_ROWS,), in_specs=[row, row], out_specs=row,
                          out_shape=jax.ShapeDtypeStruct((t, d), F32), name=name, compiler_params=_cp())(a, b)


def _loss_head(y, target, *, name):
    t, d = y.shape

    def body(y_ref, t_ref, dy_ref, l_ref):
        e = y_ref[...] - t_ref[...]
        dy_ref[...] = e * (1.0 / d)

        @pl.when(pl.program_id(0) == 0)
        def _():
            l_ref[...] = jnp.zeros_like(l_ref)

        l_ref[...] += jnp.zeros_like(l_ref) + 0.5 * jnp.sum(jnp.mean(e * e, -1, keepdims=True), 0, keepdims=True)

    row = pl.BlockSpec((LN_ROWS, d), lambda i: (i, 0))
    return pl.pallas_call(
        body, grid=(t // LN_ROWS,), in_specs=[row, row],
        out_specs=[row, pl.BlockSpec((1, LANES), lambda i: (0, 0))],
        out_shape=[jax.ShapeDtypeStruct((t, d), F32), jax.ShapeDtypeStruct((1, LANES), F32)],
        name=name, compiler_params=_cp())(y, target)


def _sig(x):
    return 1.0 / (1.0 + jnp.exp(-x))


def _silu(x):
    return x * _sig(x)


def _dsilu(x):
    s = _sig(x)
    return s * (1.0 + x * (1.0 - s))


def _shift_down(u, k, row):
    if k == 0:
        return u
    return jnp.where(row >= k, pltpu.roll(u, k, 0), 0.0)


def _shift_up(u, k, row):
    if k == 0:
        return u
    t = u.shape[0]
    return jnp.where(row < t - k, pltpu.roll(u, t - k, 0), 0.0)


def _dwconv(u, w_ref, row):
    kk = w_ref.shape[0]
    acc = None
    for j in range(kk):
        term = w_ref[j:j + 1, :] * _shift_down(u, kk - 1 - j, row)
        acc = term if acc is None else acc + term
    return acc


def _dwconv_bwd(u, w_ref, dc, row, dw_ref):
    kk = w_ref.shape[0]
    du = None
    for j in range(kk):
        term = w_ref[j:j + 1, :] * _shift_up(dc, kk - 1 - j, row)
        du = term if du is None else du + term
        dw_ref[j:j + 1, :] = jnp.sum(dc * _shift_down(u, kk - 1 - j, row), 0, keepdims=True)
    return du


CONV_ROWS = 256


def _rows(b):
    return pl.ds(pl.multiple_of(b * CONV_ROWS, CONV_ROWS), CONV_ROWS)


def _shifted_down(ref, b, k, row):
    cur = ref[_rows(b), :]
    if k == 0:
        return cur
    prev = jnp.where(b > 0, ref[_rows(jnp.maximum(b - 1, 0)), :], 0.0)
    return jnp.where(row >= k, pltpu.roll(cur, k, 0), pltpu.roll(prev, k, 0))


def _shifted_up(ref, b, k, row, nblk):
    cur = ref[_rows(b), :]
    if k == 0:
        return cur
    nxt = jnp.where(b < nblk - 1, ref[_rows(jnp.minimum(b + 1, nblk - 1)), :], 0.0)
    return jnp.where(row < CONV_ROWS - k, pltpu.roll(cur, CONV_ROWS - k, 0), pltpu.roll(nxt, CONV_ROWS - k, 0))


def _dwconv_blk(u_ref, w_ref, b, row):
    kk = w_ref.shape[0]
    views = [_shifted_down(u_ref, b, kk - 1 - j, row) for j in range(kk)]
    acc = None
    for j in range(kk):
        term = w_ref[j:j + 1, :] * views[j]
        acc = term if acc is None else acc + term
    return acc, views


def _dwconv_du_blk(dc_ref, w_ref, b, row, nblk):
    kk = w_ref.shape[0]
    du = None
    for j in range(kk):
        term = w_ref[j:j + 1, :] * _shifted_up(dc_ref, b, kk - 1 - j, row, nblk)
        du = term if du is None else du + term
    return du


FFN_TC = 256


def _ffn_up_mid(x, up_t, cw, cb, *, name, dep=None):
    t, d = x.shape
    nb = D_FF // FFN_TC

    def body(x_ref, ugt_ref, uvt_ref, wg_ref, wv_ref, bg_ref, bv_ref, *rest):
        ug_ref, uv_ref, a_ref = rest[-3:]
        xx = x_ref[...]
        row = lax.broadcasted_iota(jnp.int32, (t, FFN_TC), 0)
        ug = _nt(xx, ugt_ref[...])
        ug_ref[...] = ug
        uv = _nt(xx, uvt_ref[...])
        uv_ref[...] = uv
        cg = _dwconv(ug, wg_ref, row) + bg_ref[...]
        cv = _dwconv(uv, wv_ref, row) + bv_ref[...]
        a_ref[...] = (_silu(cg) * cv).astype(BF16)

    col = pl.BlockSpec((t, FFN_TC), lambda j: (0, j))
    wt = lambda off: pl.BlockSpec((FFN_TC, d), lambda j: (j + off, 0))
    wsp = lambda off: pl.BlockSpec((FFN_CONV, FFN_TC), lambda j: (0, j + off))
    bsp = lambda off: pl.BlockSpec((1, FFN_TC), lambda j: (0, j + off))
    extra = [] if dep is None else [dep]
    return pl.pallas_call(
        body, grid=(nb,),
        in_specs=[pl.BlockSpec((t, d), lambda j: (0, 0)), wt(0), wt(nb), wsp(0), wsp(nb), bsp(0), bsp(nb)]
        + [pl.BlockSpec(memory_space=pl.ANY)] * len(extra),
        out_specs=[col, col, col],
        out_shape=[jax.ShapeDtypeStruct((t, D_FF), F32), jax.ShapeDtypeStruct((t, D_FF), F32),
                   jax.ShapeDtypeStruct((t, D_FF), BF16)],
        name=name, compiler_params=_cp())(x, up_t, up_t, cw, cw, cb, cb, *extra)


def _ffn_mid_bwd(ug, uv, cw, cb, dz, down, *, name):
    t, d = dz.shape
    nb = D_FF // FFN_TC

    nblk = t // CONV_ROWS

    def body(ug_ref, uv_ref, wg_ref, wv_ref, bg_ref, bv_ref, dz_ref, dn_ref,
             dug_ref, duv_ref, dwg_ref, dwv_ref, dbg_ref, dbv_ref, da_ref, dcg_s, dcv_s):
        da_ref[...] = _nt(dz_ref[...], dn_ref[...])
        row = lax.broadcasted_iota(jnp.int32, (CONV_ROWS, FFN_TC), 0)
        zero = jnp.zeros((1, FFN_TC), F32)

        def first(b, acc):
            cg, ugs = _dwconv_blk(ug_ref, wg_ref, b, row)
            cv, uvs = _dwconv_blk(uv_ref, wv_ref, b, row)
            cg = cg + bg_ref[...]
            cv = cv + bv_ref[...]
            da_ = da_ref[_rows(b), :]
            dcv = da_ * _silu(cg)
            dcg = da_ * cv * _dsilu(cg)
            dcg_s[_rows(b), :] = dcg
            dcv_s[_rows(b), :] = dcv
            red = [jnp.sum(dcg * s, 0, keepdims=True) for s in ugs] + [jnp.sum(dcg, 0, keepdims=True)]
            red += [jnp.sum(dcv * s, 0, keepdims=True) for s in uvs] + [jnp.sum(dcv, 0, keepdims=True)]
            return tuple(a + r for a, r in zip(acc, red))

        acc = lax.fori_loop(0, nblk, first, (zero,) * (2 * FFN_CONV + 2))
        for j in range(FFN_CONV):
            dwg_ref[j:j + 1, :] = acc[j]
            dwv_ref[j:j + 1, :] = acc[FFN_CONV + 1 + j]
        dbg_ref[...] = acc[FFN_CONV]
        dbv_ref[...] = acc[2 * FFN_CONV + 1]

        def second(b, carry):
            dug_ref[_rows(b), :] = _dwconv_du_blk(dcg_s, wg_ref, b, row, nblk).astype(BF16)
            duv_ref[_rows(b), :] = _dwconv_du_blk(dcv_s, wv_ref, b, row, nblk).astype(BF16)
            return carry

        lax.fori_loop(0, nblk, second, 0)

    col = pl.BlockSpec((t, FFN_TC), lambda j: (0, j))
    wsp = lambda off: pl.BlockSpec((FFN_CONV, FFN_TC), lambda j: (0, j + off))
    bsp = lambda off: pl.BlockSpec((1, FFN_TC), lambda j: (0, j + off))
    outs = pl.pallas_call(
        body, grid=(nb,),
        in_specs=[col, col, wsp(0), wsp(nb), bsp(0), bsp(nb), pl.BlockSpec((t, d), lambda j: (0, 0)),
                  pl.BlockSpec((FFN_TC, d), lambda j: (j, 0))],
        out_specs=[col, col, wsp(0), wsp(0), bsp(0), bsp(0)],
        out_shape=[jax.ShapeDtypeStruct((t, D_FF), BF16), jax.ShapeDtypeStruct((t, D_FF), BF16),
                   jax.ShapeDtypeStruct((FFN_CONV, D_FF), F32), jax.ShapeDtypeStruct((FFN_CONV, D_FF), F32),
                   jax.ShapeDtypeStruct((1, D_FF), F32), jax.ShapeDtypeStruct((1, D_FF), F32)],
        scratch_shapes=[pltpu.VMEM((t, FFN_TC), F32), pltpu.VMEM((t, FFN_TC), F32), pltpu.VMEM((t, FFN_TC), F32)],
        name=name, compiler_params=_cp())(ug, uv, cw, cw, cb, cb, dz, down)
    dug, duv, dwg, dwv, dbg, dbv = outs
    return [dug, duv], jnp.concatenate([dwg, dwv], 1), jnp.concatenate([dbg, dbv], 1)


def _rot_a(x, c2, s2):
    return x * c2 + pltpu.roll(x, RET_DK // 2, 1) * s2


def _rot_a_t(dy, c2, s2):
    return dy * c2 + pltpu.roll(dy * s2, RET_DK // 2, 1)


def _decay_tile(lg, blk_diff):
    r = lax.broadcasted_iota(jnp.int32, (ATT_BLK, ATT_BLK), 0)
    c = lax.broadcasted_iota(jnp.int32, (ATT_BLK, ATT_BLK), 1)
    rel = r - c + blk_diff * ATT_BLK
    return jnp.where(rel >= 0, jnp.exp(jnp.maximum(rel, 0).astype(F32) * lg), 0.0)


def _nt(a, b):
    return lax.dot_general(a, b, (((1,), (1,)), ((), ())), preferred_element_type=F32)


def _nn(a, b):
    return lax.dot_general(a, b, (((1,), (0,)), ((), ())), preferred_element_type=F32)


def _tn(a, b):
    return lax.dot_general(a, b, (((0,), (0,)), ((), ())), preferred_element_type=F32)


def _ret_specs(t):
    q = pl.BlockSpec((t, RET_DK), lambda h: (0, h))
    k = pl.BlockSpec((t, RET_DK), lambda h: (0, RET_HEADS + h))
    v = pl.BlockSpec((t, RET_DV), lambda h: (0, RET_HEADS + h))
    g = pl.BlockSpec((t, RET_DV), lambda h: (0, 2 * RET_HEADS + h))
    tab = pl.BlockSpec((t, RET_DK), lambda h: (0, 0))
    lg = pl.BlockSpec((1, 1, LANES), lambda h: (h, 0, 0))
    return q, k, v, g, tab, lg


def _ret_fwd(h, c2, s2, lgt, *, name):
    t = h.shape[0]
    nblk = t // ATT_BLK
    scale = RET_DK ** -0.5

    def body(q_ref, k_ref, v_ref, g_ref, c_ref, s_ref, lg_ref, o_ref, ya_ref, qs, ks, vs):
        c2_, s2_ = c_ref[...], s_ref[...]
        qs[...] = _rot_a(q_ref[...], c2_, s2_).astype(BF16)
        ks[...] = (_rot_a(k_ref[...], c2_, s2_) * scale).astype(BF16)
        vs[...] = v_ref[...].astype(BF16)
        lg = lg_ref[0, :, 0:1]
        for i in range(nblk):
            qi = qs[pl.ds(i * ATT_BLK, ATT_BLK), :]
            acc = jnp.zeros((ATT_BLK, RET_DV), F32)
            for j in range(i + 1):
                sl = pl.ds(j * ATT_BLK, ATT_BLK)
                s = _nt(qi, ks[sl, :]) * _decay_tile(lg, i - j)
                acc = acc + _nn(s.astype(BF16), vs[sl, :])
            rows = pl.ds(i * ATT_BLK, ATT_BLK)
            o_ref[rows, :] = acc
            r = lax.rsqrt(jnp.mean(acc * acc, -1, keepdims=True) + EPS)
            ya_ref[rows, :] = (acc * r * _silu(g_ref[rows, :])).astype(BF16)

    q, k, v, g, tab, lg = _ret_specs(t)
    out = pl.BlockSpec((t, RET_DV), lambda hh: (0, hh))
    return pl.pallas_call(
        body, grid=(RET_HEADS,), in_specs=[q, k, v, g, tab, tab, lg], out_specs=[out, out],
        out_shape=[jax.ShapeDtypeStruct((t, RET_V_W), F32), jax.ShapeDtypeStruct((t, RET_V_W), BF16)],
        scratch_shapes=[pltpu.VMEM((t, RET_DK), BF16), pltpu.VMEM((t, RET_DK), BF16), pltpu.VMEM((t, RET_DV), BF16)],
        name=name, compiler_params=_cp())(h, h, h, h, c2, s2, lgt)


def _ret_bwd(h, c2, s2, lgt, o, dy, *, name):
    t = h.shape[0]
    nblk = t // ATT_BLK
    scale = RET_DK ** -0.5

    def body(q_ref, k_ref, v_ref, g_ref, c_ref, s_ref, lg_ref, o_ref, dy_ref,
             dq_ref, dk_ref, dv_ref, dg_ref, qs, ks, vs, dos, dka, dva):
        c2_, s2_ = c_ref[...], s_ref[...]
        qs[...] = _rot_a(q_ref[...], c2_, s2_).astype(BF16)
        ks[...] = (_rot_a(k_ref[...], c2_, s2_) * scale).astype(BF16)
        vs[...] = v_ref[...].astype(BF16)
        lg = lg_ref[0, :, 0:1]
        oo = o_ref[...]
        gg = g_ref[...]
        dya = dy_ref[...]
        r = lax.rsqrt(jnp.mean(oo * oo, -1, keepdims=True) + EPS)
        rn = oo * r
        dg_ref[...] = (dya * rn * _dsilu(gg)).astype(BF16)
        drn = dya * _silu(gg)
        dos[...] = (r * (drn - rn * jnp.mean(drn * rn, -1, keepdims=True))).astype(BF16)
        dka[...] = jnp.zeros_like(dka)
        dva[...] = jnp.zeros_like(dva)
        for i in range(nblk):
            rows = pl.ds(i * ATT_BLK, ATT_BLK)
            qi = qs[rows, :]
            doi = dos[rows, :]
            dqa = jnp.zeros((ATT_BLK, RET_DK), F32)
            for j in range(i + 1):
                sl = pl.ds(j * ATT_BLK, ATT_BLK)
                dt_ = _decay_tile(lg, i - j)
                kj = ks[sl, :]
                s = (_nt(qi, kj) * dt_).astype(BF16)
                ds = (_nt(doi, vs[sl, :]) * dt_).astype(BF16)
                dqa = dqa + _nn(ds, kj)
                dka[sl, :] += _tn(ds, qi)
                dva[sl, :] += _tn(s, doi)
            dq_ref[rows, :] = _rot_a_t(dqa, c_ref[rows, :], s_ref[rows, :]).astype(BF16)
        dk_ref[...] = (_rot_a_t(dka[...], c2_, s2_) * scale).astype(BF16)
        dv_ref[...] = dva[...].astype(BF16)

    q, k, v, g, tab, lg = _ret_specs(t)
    blk_v = pl.BlockSpec((t, RET_DV), lambda hh: (0, hh))
    blk_k = pl.BlockSpec((t, RET_DK), lambda hh: (0, hh))
    return pl.pallas_call(
        body, grid=(RET_HEADS,), in_specs=[q, k, v, g, tab, tab, lg, blk_v, blk_v],
        out_specs=[blk_k, blk_k, blk_v, blk_v],
        out_shape=[jax.ShapeDtypeStruct((t, RET_QK_W), BF16), jax.ShapeDtypeStruct((t, RET_QK_W), BF16),
                   jax.ShapeDtypeStruct((t, RET_V_W), BF16), jax.ShapeDtypeStruct((t, RET_V_W), BF16)],
        scratch_shapes=[pltpu.VMEM((t, RET_DK), BF16), pltpu.VMEM((t, RET_DK), BF16), pltpu.VMEM((t, RET_DV), BF16),
                        pltpu.VMEM((t, RET_DV), BF16), pltpu.VMEM((t, RET_DK), F32), pltpu.VMEM((t, RET_DV), F32)],
        name=name, compiler_params=_cp())(h, h, h, h, c2, s2, lgt, o, dy)


def _rot_b(x, cb, shi, slo):
    return x * cb + pltpu.roll(x, ROPE_DIMS // 2, 1) * shi + pltpu.roll(x, LANES - ROPE_DIMS // 2, 1) * slo


def _rot_b_t(dy, cb, shi, slo):
    return dy * cb + pltpu.roll(dy * shi, LANES - ROPE_DIMS // 2, 1) + pltpu.roll(dy * slo, ROPE_DIMS // 2, 1)


def _dil_specs(t):
    base = (2 * RET_QK_W + 2 * RET_V_W) // LANES
    npair = DIL_W // LANES
    q = pl.BlockSpec((t, LANES), lambda p: (0, base + p))
    k = pl.BlockSpec((t, LANES), lambda p: (0, base + npair + p))
    v = pl.BlockSpec((t, LANES), lambda p: (0, base + 2 * npair + p))
    tab = pl.BlockSpec((t, LANES), lambda p: (0, 0))
    strip = pl.BlockSpec((ATT_BLK, t), lambda p: (0, 0))
    pair = pl.BlockSpec((t, LANES), lambda p: (0, p))
    return q, k, v, tab, strip, pair


def _dil_fwd(h, cb, shi, slo, strip, *, name):
    t = h.shape[0]
    nblk = t // ATT_BLK
    scale = DIL_HD ** -0.5

    def body(q_ref, k_ref, v_ref, cb_ref, shi_ref, slo_ref, st_ref, o_ref, yb_ref, lse_ref, qs, ks, vs):
        cb_, shi_, slo_ = cb_ref[...], shi_ref[...], slo_ref[...]
        lane = lax.broadcasted_iota(jnp.int32, (t, LANES), 1)
        qr = _rot_b(q_ref[...], cb_, shi_, slo_) * scale
        qs[0] = jnp.where(lane < DIL_HD, qr, 0.0).astype(BF16)
        qs[1] = jnp.where(lane >= DIL_HD, qr, 0.0).astype(BF16)
        ks[...] = _rot_b(k_ref[...], cb_, shi_, slo_).astype(BF16)
        vs[...] = v_ref[...].astype(BF16)
        lane_b = lax.broadcasted_iota(jnp.int32, (ATT_BLK, LANES), 1)
        for i in range(nblk):
            w = (i + 1) * ATT_BLK
            rows = pl.ds(i * ATT_BLK, ATT_BLK)
            logc = st_ref[:, t - w:t]
            outs, lses = [], []
            for hd in range(2):
                s = _nt(qs[hd, rows, :], ks[0:w, :]) + logc
                m = jnp.max(s, -1, keepdims=True)
                p = jnp.exp(s - m)
                l = jnp.sum(p, -1, keepdims=True)
                outs.append(_nn(p.astype(BF16), vs[0:w, :]) / l)
                lses.append(m + jnp.log(l))
            o = jnp.where(lane_b < DIL_HD, outs[0], outs[1])
            o_ref[rows, :] = o
            yb_ref[rows, :] = o.astype(BF16)
            lse_ref[rows, :] = jnp.where(lane_b < DIL_HD, lses[0], lses[1])

    q, k, v, tab, strip_spec, pair = _dil_specs(t)
    return pl.pallas_call(
        body, grid=(DIL_W // LANES,), in_specs=[q, k, v, tab, tab, tab, strip_spec], out_specs=[pair, pair, pair],
        out_shape=[jax.ShapeDtypeStruct((t, DIL_W), F32), jax.ShapeDtypeStruct((t, DIL_W), BF16),
                   jax.ShapeDtypeStruct((t, DIL_W), F32)],
        scratch_shapes=[pltpu.VMEM((2, t, LANES), BF16), pltpu.VMEM((t, LANES), BF16), pltpu.VMEM((t, LANES), BF16)],
        name=name, compiler_params=_cp())(h, h, h, cb, shi, slo, strip)


def _dil_bwd(h, cb, shi, slo, strip, o, lse, dy, *, name):
    t = h.shape[0]
    nblk = t // ATT_BLK
    scale = DIL_HD ** -0.5

    def body(q_ref, k_ref, v_ref, cb_ref, shi_ref, slo_ref, st_ref, o_ref, lse_ref, dy_ref,
             dq_ref, dk_ref, dv_ref, qs, ks, vs, dos, dls, dka, dva):
        cb_, shi_, slo_ = cb_ref[...], shi_ref[...], slo_ref[...]
        lane = lax.broadcasted_iota(jnp.int32, (t, LANES), 1)
        qr = _rot_b(q_ref[...], cb_, shi_, slo_) * scale
        qs[0] = jnp.where(lane < DIL_HD, qr, 0.0).astype(BF16)
        qs[1] = jnp.where(lane >= DIL_HD, qr, 0.0).astype(BF16)
        ks[...] = _rot_b(k_ref[...], cb_, shi_, slo_).astype(BF16)
        vs[...] = v_ref[...].astype(BF16)
        do = dy_ref[...]
        prod = do * o_ref[...]
        d0 = jnp.sum(jnp.where(lane < DIL_HD, prod, 0.0), -1, keepdims=True)
        d1 = jnp.sum(jnp.where(lane >= DIL_HD, prod, 0.0), -1, keepdims=True)
        dls[...] = jnp.where(lane < DIL_HD, d0, d1)
        dos[0] = jnp.where(lane < DIL_HD, do, 0.0).astype(BF16)
        dos[1] = jnp.where(lane >= DIL_HD, do, 0.0).astype(BF16)
        dka[...] = jnp.zeros_like(dka)
        dva[...] = jnp.zeros_like(dva)
        lane_b = lax.broadcasted_iota(jnp.int32, (ATT_BLK, LANES), 1)
        for i in range(nblk):
            w = (i + 1) * ATT_BLK
            rows = pl.ds(i * ATT_BLK, ATT_BLK)
            logc = st_ref[:, t - w:t]
            dqs = []
            for hd in range(2):
                col = hd * DIL_HD
                qh = qs[hd, rows, :]
                doh = dos[hd, rows, :]
                lse_h = lse_ref[rows, col:col + 1]
                dl_h = dls[rows, col:col + 1]
                p = jnp.exp(_nt(qh, ks[0:w, :]) + logc - lse_h)
                dp = _nt(doh, vs[0:w, :])
                ds = (p * (dp - dl_h)).astype(BF16)
                dqs.append(_nn(ds, ks[0:w, :]))
                dka[0:w, :] += _tn(ds, qh)
                dva[0:w, :] += _tn(p.astype(BF16), doh)
            dq = jnp.where(lane_b < DIL_HD, dqs[0], dqs[1]) * scale
            dq_ref[rows, :] = _rot_b_t(dq, cb_ref[rows, :], shi_ref[rows, :], slo_ref[rows, :]).astype(BF16)
        dk_ref[...] = _rot_b_t(dka[...], cb_, shi_, slo_).astype(BF16)
        dv_ref[...] = dva[...].astype(BF16)

    q, k, v, tab, strip_spec, pair = _dil_specs(t)
    dy_spec = pl.BlockSpec((t, LANES), lambda p: (0, RET_V_W // LANES + p))
    return pl.pallas_call(
        body, grid=(DIL_W // LANES,), in_specs=[q, k, v, tab, tab, tab, strip_spec, pair, pair, dy_spec],
        out_specs=[pair, pair, pair],
        out_shape=[jax.ShapeDtypeStruct((t, DIL_W), BF16)] * 3,
        scratch_shapes=[pltpu.VMEM((2, t, LANES), BF16), pltpu.VMEM((t, LANES), BF16), pltpu.VMEM((t, LANES), BF16),
                        pltpu.VMEM((2, t, LANES), BF16), pltpu.VMEM((t, LANES), F32),
                        pltpu.VMEM((t, LANES), F32), pltpu.VMEM((t, LANES), F32)],
        name=name, compiler_params=_cp())(h, h, h, cb, shi, slo, strip, o, lse, dy)


def _gdn_prep_fwd(h, cw, *, name):
    t = h.shape[0]
    qscale = GDN_DK ** -0.5

    def body(hq_ref, hk_ref, hv_ref, wq_ref, wk_ref, wv_ref, q_ref, k_ref, v_ref):
        row = lax.broadcasted_iota(jnp.int32, (t, GDN_DK), 0)
        sq = _silu(_dwconv(hq_ref[...], wq_ref, row))
        sk = _silu(_dwconv(hk_ref[...], wk_ref, row))
        q_ref[0] = sq * lax.rsqrt(jnp.sum(sq * sq, -1, keepdims=True) + 1e-6) * qscale
        k_ref[0] = sk * lax.rsqrt(jnp.sum(sk * sk, -1, keepdims=True) + 1e-6)
        v_ref[0] = _silu(_dwconv(hv_ref[...], wv_ref, row))

    hs = lambda off: pl.BlockSpec((t, GDN_DK), lambda i: (0, i + off))
    ws = lambda off: pl.BlockSpec((GDN_CONV, GDN_DK), lambda i: (0, i + off))
    out = pl.BlockSpec((1, t, GDN_DK), lambda i: (i, 0, 0))
    return pl.pallas_call(
        body, grid=(GDN_HEADS,), in_specs=[hs(0), hs(8), hs(16), ws(0), ws(8), ws(16)], out_specs=[out, out, out],
        out_shape=[jax.ShapeDtypeStruct((GDN_HEADS, t, GDN_DK), F32)] * 3,
        name=name, compiler_params=_cp())(h, h, h, cw, cw, cw)


def _gdn_prep_bwd(h, cw, dq, dk, dv, *, name):
    t = h.shape[0]
    qscale = GDN_DK ** -0.5

    def body(hq_ref, hk_ref, hv_ref, wq_ref, wk_ref, wv_ref, dq_ref, dk_ref, dv_ref,
             dhq_ref, dhk_ref, dhv_ref, dwq_ref, dwk_ref, dwv_ref):
        row = lax.broadcasted_iota(jnp.int32, (t, GDN_DK), 0)

        def one(h_ref, w_ref, d_ref, dh_ref, dw_ref, norm, sc):
            u = h_ref[...]
            c = _dwconv(u, w_ref, row)
            d = d_ref[0]
            if norm:
                s = _silu(c)
                r = lax.rsqrt(jnp.sum(s * s, -1, keepdims=True) + 1e-6)
                n = s * r
                d = d * sc
                d = r * (d - n * jnp.sum(d * n, -1, keepdims=True))
            dc = d * _dsilu(c)
            dh_ref[...] = _dwconv_bwd(u, w_ref, dc, row, dw_ref).astype(BF16)

        one(hq_ref, wq_ref, dq_ref, dhq_ref, dwq_ref, True, qscale)
        one(hk_ref, wk_ref, dk_ref, dhk_ref, dwk_ref, True, 1.0)
        one(hv_ref, wv_ref, dv_ref, dhv_ref, dwv_ref, False, 1.0)

    hs = lambda off: pl.BlockSpec((t, GDN_DK), lambda i: (0, i + off))
    ws = lambda off: pl.BlockSpec((GDN_CONV, GDN_DK), lambda i: (0, i + off))
    hd = pl.BlockSpec((1, t, GDN_DK), lambda i: (i, 0, 0))
    return pl.pallas_call(
        body, grid=(GDN_HEADS,), in_specs=[hs(0), hs(8), hs(16), ws(0), ws(8), ws(16), hd, hd, hd],
        out_specs=[hs(0), hs(0), hs(0), ws(0), ws(0), ws(0)],
        out_shape=[jax.ShapeDtypeStruct((t, GDN_W), BF16)] * 3 + [jax.ShapeDtypeStruct((GDN_CONV, GDN_W), F32)] * 3,
        name=name, compiler_params=_cp())(h, h, h, cw, cw, cw, dq, dk, dv)


def _make_mm2(wide):
    def raw(a, b, dims):
        if wide:
            return lax.dot_general(a, b, (dims, ((), ())), precision=lax.Precision.HIGHEST, preferred_element_type=F32)
        return lax.dot_general(a.astype(BF16), b.astype(BF16), (dims, ((), ())), preferred_element_type=F32)

    @jax.custom_vjp
    def nn(a, b):
        return raw(a, b, ((1,), (0,)))

    @jax.custom_vjp
    def nt(a, b):
        return raw(a, b, ((1,), (1,)))

    @jax.custom_vjp
    def tn(a, b):
        return raw(a, b, ((0,), (0,)))

    nn.defvjp(lambda a, b: (nn(a, b), (a, b)), lambda r, g: (nt(g, r[1]), tn(r[0], g)))
    nt.defvjp(lambda a, b: (nt(a, b), (a, b)), lambda r, g: (nn(g, r[1]), tn(g, r[0])))
    tn.defvjp(lambda a, b: (tn(a, b), (a, b)), lambda r, g: (nt(r[1], g), nn(r[0], g)))
    return nn, nt, tn


_NN, _NT, _TN = _make_mm2(False)
_NNW, _NTW, _TNW = _make_mm2(True)


def _square_masks(c):
    ri = lax.broadcasted_iota(jnp.int32, (c, c), 0)
    ci = lax.broadcasted_iota(jnp.int32, (c, c), 1)
    return ri >= ci, ri > ci, ri == ci


def _cumsum_rows(m):
    tri, _, _ = _square_masks(m.shape[0])
    return _NNW(tri.astype(F32), m)


def _transpose_sq(m):
    _, _, eye = _square_masks(m.shape[0])
    return _NTW(eye.astype(F32), m)


@jax.custom_vjp
def _inv_unit_lower(l):
    c = l.shape[0]
    _, _, eye = _square_masks(c)
    p = -l
    t = eye.astype(F32) + p
    for _ in range(int(math.log2(c)) - 1):
        p = _NNW(p, p)
        t = t + _NNW(t, p)
    return t


def _inv_fwd(l):
    t = _inv_unit_lower(l)
    return t, t


def _inv_bwd(t, dt):
    return (-_NTW(_TNW(t, dt), t),)


_inv_unit_lower.defvjp(_inv_fwd, _inv_bwd)


@jax.custom_vjp
def _inv_known(l, t):
    return t


_inv_known.defvjp(lambda l, t: (t, t), lambda t, dt: (_inv_bwd(t, dt)[0], jnp.zeros_like(t)))


def _softplus(x):
    return jnp.maximum(x, 0.0) + jnp.log1p(jnp.exp(-jnp.abs(x)))


def _gdn_chunk(q, k, v, braw, araw, alog, dtb, state, inv=None):
    c = q.shape[0]
    dv = v.shape[1]
    tri, strict, _ = _square_masks(c)
    beta = _sig(braw)
    g = -jnp.exp(alog) * _softplus(araw + dtb)
    gcm = _cumsum_rows(g * jnp.ones((c, c), F32))
    gct = _transpose_sq(gcm)
    decay = jnp.where(tri, jnp.exp(jnp.where(tri, gcm - gct, 0.0)), 0.0)
    gc = jnp.sum(gcm, 1, keepdims=True) * (1.0 / c)
    glast = jnp.sum(g, 0, keepdims=True)
    egc = jnp.exp(gc)
    kb = k * beta
    low = jnp.where(strict, _NT(kb, k) * decay, 0.0)
    tm = _inv_unit_lower(low) if inv is None else _inv_known(low, inv)
    sol = _NNW(tm, jnp.concatenate([v * beta, kb * egc], 1))
    u, w = sol[:, :dv], sol[:, dv:]
    attn = jnp.where(tri, _NT(q, k) * decay, 0.0)
    k_dec = k * jnp.exp(glast - gc)
    q_dec = q * egc
    v_new = u - _NN(w, state)
    o = _NN(q_dec, state) + _NN(attn, v_new)
    new_state = state * jnp.exp(glast) + _TN(k_dec, v_new)
    return o, new_state, tm


def _gdn_specs(t, rev):
    nch = t // GDN_CHUNK
    cm = (lambda n: nch - 1 - n) if rev else (lambda n: n)
    tok = pl.BlockSpec((GDN_HEADS, GDN_CHUNK, GDN_DK), lambda n: (0, cm(n), 0))
    par = pl.BlockSpec((GDN_HEADS, 1, LANES), lambda n: (0, 0, 0))
    st = pl.BlockSpec((GDN_HEADS, 1, GDN_DK, GDN_DV), lambda n: (0, cm(n), 0, 0))
    inv = pl.BlockSpec((GDN_HEADS, GDN_CHUNK, GDN_CHUNK), lambda n: (0, cm(n), 0))
    sc = pl.BlockSpec((GDN_CHUNK, LANES), lambda n: (cm(n), 4 * GDN_W // LANES))
    return tok, par, st, inv, sc


def _head_columns(sc_ref, first):
    return jnp.stack([sc_ref[:, first + hh:first + hh + 1] for hh in range(GDN_HEADS)])


def _gdn_core_fwd(q, k, v, h, alog, dtb, *, name):
    t = q.shape[1]
    nch = t // GDN_CHUNK

    def body(q_ref, k_ref, v_ref, sc_ref, al_ref, dt_ref, o_ref, st_ref, inv_ref, state):
        @pl.when(pl.program_id(0) == 0)
        def _():
            state[...] = jnp.zeros_like(state)

        s0 = state[...]
        st_ref[:, 0] = s0
        o, s1, tm = jax.vmap(_gdn_chunk)(q_ref[...], k_ref[...], v_ref[...], _head_columns(sc_ref, 0),
                                         _head_columns(sc_ref, GDN_HEADS), al_ref[:, :, 0:1], dt_ref[:, :, 0:1], s0)
        o_ref[...] = o
        inv_ref[...] = tm
        state[...] = s1

    tok, par, st, inv, sc = _gdn_specs(t, False)
    return pl.pallas_call(
        body, grid=(nch,), in_specs=[tok, tok, tok, sc, par, par], out_specs=[tok, st, inv],
        out_shape=[jax.ShapeDtypeStruct((GDN_HEADS, t, GDN_DV), F32),
                   jax.ShapeDtypeStruct((GDN_HEADS, nch, GDN_DK, GDN_DV), F32),
                   jax.ShapeDtypeStruct((GDN_HEADS, t, GDN_CHUNK), F32)],
        scratch_shapes=[pltpu.VMEM((GDN_HEADS, GDN_DK, GDN_DV), F32)],
        name=name, compiler_params=_cp())(q, k, v, h, alog, dtb)


def _gdn_core_bwd(q, k, v, h, alog, dtb, states, invs, do, *, name):
    t = q.shape[1]
    nch = t // GDN_CHUNK

    def body(q_ref, k_ref, v_ref, sc_ref, al_ref, dt_ref, st_ref, inv_ref, do_ref,
             dq_ref, dk_ref, dv_ref, dsc_ref, dal_ref, ddt_ref, dstate):
        @pl.when(pl.program_id(0) == 0)
        def _():
            dstate[...] = jnp.zeros_like(dstate)
            dal_ref[...] = jnp.zeros_like(dal_ref)
            ddt_ref[...] = jnp.zeros_like(ddt_ref)

        args = (q_ref[...], k_ref[...], v_ref[...], _head_columns(sc_ref, 0), _head_columns(sc_ref, GDN_HEADS),
                al_ref[:, :, 0:1], dt_ref[:, :, 0:1], st_ref[:, 0])
        tm = inv_ref[...]

        def chunk(*a):
            return jax.vmap(_gdn_chunk)(*a, tm)[:2]

        _, pull = jax.vjp(chunk, *args)
        dq, dk, dv, dbr, dar, dal, ddt, ds = pull((do_ref[...], dstate[...]))
        dq_ref[...] = dq
        dk_ref[...] = dk
        dv_ref[...] = dv
        lane = lax.broadcasted_iota(jnp.int32, (GDN_CHUNK, LANES), 1)
        dsc = jnp.zeros((GDN_CHUNK, LANES), F32)
        for hh in range(GDN_HEADS):
            dsc = jnp.where(lane == hh, dbr[hh], dsc)
            dsc = jnp.where(lane == GDN_HEADS + hh, dar[hh], dsc)
        dsc_ref[...] = dsc
        dal_ref[...] += dal + jnp.zeros((GDN_HEADS, 1, LANES), F32)
        ddt_ref[...] += ddt + jnp.zeros((GDN_HEADS, 1, LANES), F32)
        dstate[...] = ds

    tok, par, st, inv, sc = _gdn_specs(t, True)
    tokshape = jax.ShapeDtypeStruct((GDN_HEADS, t, GDN_DK), F32)
    parshape = jax.ShapeDtypeStruct((GDN_HEADS, 1, LANES), F32)
    nch_map = pl.BlockSpec((GDN_CHUNK, LANES), lambda n: (nch - 1 - n, 0))
    return pl.pallas_call(
        body, grid=(nch,), in_specs=[tok, tok, tok, sc, par, par, st, inv, tok],
        out_specs=[tok, tok, tok, nch_map, par, par],
        out_shape=[tokshape] * 3 + [jax.ShapeDtypeStruct((t, LANES), F32)] + [parshape] * 2,
        scratch_shapes=[pltpu.VMEM((GDN_HEADS, GDN_DK, GDN_DV), F32)],
        name=name, compiler_params=_cp())(q, k, v, h, alog, dtb, states, invs, do)


GDN_ROWS = 512


def _gdn_post_fwd(o, h, nw, *, name):
    t = o.shape[1]

    def body(o_ref, g_ref, nw_ref, y_ref):
        oo = o_ref[0]
        r = lax.rsqrt(jnp.mean(oo * oo, -1, keepdims=True) + EPS)
        y_ref[...] = (oo * r * nw_ref[...] * _silu(g_ref[...])).astype(BF16)

    return pl.pallas_call(
        body, grid=(GDN_HEADS, t // GDN_ROWS),
        in_specs=[pl.BlockSpec((1, GDN_ROWS, GDN_DV), lambda hh, i: (hh, i, 0)),
                  pl.BlockSpec((GDN_ROWS, GDN_DV), lambda hh, i: (i, 3 * GDN_HEADS + hh)),
                  pl.BlockSpec((1, GDN_DV), lambda hh, i: (0, 0))],
        out_specs=pl.BlockSpec((GDN_ROWS, GDN_DV), lambda hh, i: (i, hh)),
        out_shape=jax.ShapeDtypeStruct((t, GDN_W), BF16), name=name, compiler_params=_cp())(o, h, nw)


def _gdn_post_bwd(o, h, nw, dy, *, name):
    t = o.shape[1]

    def body(o_ref, g_ref, nw_ref, dy_ref, do_ref, dg_ref, dnw_ref):
        oo, gg, nw_, dy_ = o_ref[0], g_ref[...], nw_ref[...], dy_ref[...]
        r = lax.rsqrt(jnp.mean(oo * oo, -1, keepdims=True) + EPS)
        n = oo * r
        sg = _silu(gg)
        dg_ref[...] = (dy_ * n * nw_ * _dsilu(gg)).astype(BF16)
        dn = dy_ * sg * nw_
        do_ref[0] = r * (dn - n * jnp.mean(dn * n, -1, keepdims=True))

        @pl.when((pl.program_id(0) == 0) & (pl.program_id(1) == 0))
        def _():
            dnw_ref[...] = jnp.zeros_like(dnw_ref)

        dnw_ref[...] += jnp.sum(dy_ * sg * n, 0, keepdims=True)

    return pl.pallas_call(
        body, grid=(GDN_HEADS, t // GDN_ROWS),
        in_specs=[pl.BlockSpec((1, GDN_ROWS, GDN_DV), lambda hh, i: (hh, i, 0)),
                  pl.BlockSpec((GDN_ROWS, GDN_DV), lambda hh, i: (i, 3 * GDN_HEADS + hh)),
                  pl.BlockSpec((1, GDN_DV), lambda hh, i: (0, 0)),
                  pl.BlockSpec((GDN_ROWS, GDN_DV), lambda hh, i: (i, hh))],
        out_specs=[pl.BlockSpec((1, GDN_ROWS, GDN_DV), lambda hh, i: (hh, i, 0)),
                   pl.BlockSpec((GDN_ROWS, GDN_DV), lambda hh, i: (i, hh)),
                   pl.BlockSpec((1, GDN_DV), lambda hh, i: (0, 0))],
        out_shape=[jax.ShapeDtypeStruct((GDN_HEADS, t, GDN_DV), F32), jax.ShapeDtypeStruct((t, GDN_W), BF16),
                   jax.ShapeDtypeStruct((1, GDN_DV), F32)],
        name=name, compiler_params=_cp())(o, h, nw, dy)


def _tables(positions):
    pos = positions.astype(F32)[:, None]
    half = RET_DK // 2
    inv = jnp.power(RET_THETA, -jnp.arange(half, dtype=F32) * 2.0 / RET_DK)
    ang = pos * inv
    cos, sin = jnp.cos(ang), jnp.sin(ang)
    c2a = jnp.concatenate([cos, cos], 1)
    s2a = jnp.concatenate([-sin, sin], 1)
    hb = ROPE_DIMS // 2
    invb = jnp.power(ROPE_THETA, -jnp.arange(hb, dtype=F32) * 2.0 / ROPE_DIMS)
    angb = pos * invb
    cosb, sinb = jnp.cos(angb), jnp.sin(angb)
    t = pos.shape[0]
    ones = jnp.ones((t, DIL_HD - ROPE_DIMS), F32)
    zeros = jnp.zeros((t, DIL_HD - ROPE_DIMS), F32)
    z8 = jnp.zeros((t, hb), F32)
    cb = jnp.concatenate([cosb, cosb, ones] * 2, 1)
    shi = jnp.concatenate([z8, sinb, zeros] * 2, 1)
    slo = jnp.concatenate([-sinb, z8, zeros] * 2, 1)
    lg = jnp.log1p(-jnp.power(2.0, -5.0 - jnp.arange(RET_HEADS, dtype=F32)))
    lgt = jnp.broadcast_to(lg[:, None, None], (RET_HEADS, 1, LANES))
    delta = jnp.arange(ATT_BLK, dtype=jnp.int32)[:, None] + (SEQ - ATT_BLK) - jnp.arange(SEQ, dtype=jnp.int32)[None, :]
    cnt = jnp.zeros(delta.shape, F32)
    for (w, d) in DIL_PAIRS:
        cnt = cnt + ((delta >= 0) & (delta <= w) & (delta % d == 0)).astype(F32)
    strip = jnp.where(cnt > 0, jnp.log(jnp.maximum(cnt, 1.0)), NEG)
    return c2a, s2a, cb, shi, slo, lgt, strip


def _local_step(x, tables, target, get_w, mid, put_g, small):
    c2a, s2a, cb, shi, slo, lgt, strip = tables
    t = x.shape[0]
    saved = []
    xf = x
    xb = x.astype(BF16)
    for layer in range(DEPTH):
        j = layer // 2
        L = f"L{layer}_"
        W, dep = get_w(layer, "mixer", xb)
        rec = {"x": xf, "xb": xb}
        if layer % 2 == 0:
            h = _mm(xb, W["in_t"], tb=True, name=L + "ev_in", dep=dep)
            ro, ya = _ret_fwd(h, c2a, s2a, lgt, name=L + "ret_fwd")
            do_, yb, lse = _dil_fwd(h, cb, shi, slo, strip, name=L + "dil_fwd")
            y = jnp.concatenate([ya, yb], 1)
            rec.update(h=h, ro=ro, dil_o=do_, lse=lse, y=y)
        else:
            h = _mm(xb, W["in_t"], tb=True, name=L + "od_in", dep=dep)
            cw = W["conv"]
            q, k, v = _gdn_prep_fwd(h, cw, name=L + "gdn_prep")
            alog = jnp.broadcast_to(small["od_a_log"][j][:, None, None], (GDN_HEADS, 1, LANES))
            dtb = jnp.broadcast_to(small["od_dt_bias"][j][:, None, None], (GDN_HEADS, 1, LANES))
            o, states, invs = _gdn_core_fwd(q, k, v, h, alog, dtb, name=L + "gdn_fwd")
            nw = small["od_norm_w"][j][None, :]
            y = _gdn_post_fwd(o, h, nw, name=L + "gdn_post")
            rec.update(h=h, q=q, k=k, v=v, alog=alog, dtb=dtb, states=states, invs=invs, o=o, y=y, nw=nw, cw=cw)
        z1, x1, x1b = _mm_ln_fwd(y, W["out"], xf, small["ln1_g"][layer][None], small["ln1_b"][layer][None],
                                 name=L + "out_ln1", dep=mid(layer, "mixer", y))
        rec["Wm"] = W
        W, dep = get_w(layer, "ffn", x1b)
        rec["Wf"] = W
        fcw = W["fconv"]
        fcb = small["ffn_conv_b"][layer][None]
        ug, uv, a = _ffn_up_mid(x1b, W["up_t"], fcw, fcb, name=L + "ffn_up_mid", dep=dep)
        z2, x2, x2b = _mm_ln_fwd(a, W["down"], x1, small["ln2_g"][layer][None], small["ln2_b"][layer][None],
                                 name=L + "down_ln2", dep=mid(layer, "ffn", a))
        rec.update(z1=z1, x1b=x1b, ug=ug, uv=uv, a=a, z2=z2, fcw=fcw, fcb=fcb)
        saved.append(rec)
        xf, xb = x2, x2b

    dy, lossv = _loss_head(xf, target, name="loss_head")
    loss = lossv[0, 0]

    gS = {n: [None] * small[n].shape[0] for n in small}
    below = None
    for layer in reversed(range(DEPTH)):
        j = layer // 2
        L = f"L{layer}_"
        rec = saved[layer]
        Wm, Wf = rec["Wm"], rec["Wf"]
        g = {}
        if below is None:
            dz2, dz2b, dg2, db2 = _ln_bwd(rec["z2"], small["ln2_g"][layer][None], dy, None, name=L + "ln2_bwd")
        else:
            dz2, dz2b, dg2, db2 = _mm_ln_bwd(below[0], below[1], rec["z2"], small["ln2_g"][layer][None], below[2],
                                             name=L + "ln2_bwd", dep=below[3])
        gS["ln2_g"][layer], gS["ln2_b"][layer] = dg2[0], db2[0]
        g["down"] = _mm(rec["a"], dz2b, ta=True, name=L + "ffn_down_dw", out_dtype=BF16)
        du, dcw, dcb = _ffn_mid_bwd(rec["ug"], rec["uv"], rec["fcw"], rec["fcb"], dz2b, Wf["down"], name=L + "ffn_mid_bwd")
        g["fconv"] = dcw.astype(BF16)
        gS["ffn_conv_b"][layer] = dcb[0]
        g["up_t"] = _mm_tn_parts(du, rec["x1b"], name=L + "ffn_up_dw")
        dep = put_g(layer, "ffn", g)
        dz1, dz1b, dg1, db1 = _mm_ln_bwd(du, Wf["up_t"], rec["z1"], small["ln1_g"][layer][None], dz2,
                                         name=L + "ln1_bwd", dep=dep)
        gS["ln1_g"][layer], gS["ln1_b"][layer] = dg1[0], db1[0]
        g = {}
        if layer % 2 == 0:
            g["out"] = _mm(rec["y"], dz1b, ta=True, name=L + "ev_out_dw", out_dtype=BF16)
            dyy = _mm(dz1b, Wm["out"], tb=True, name=L + "ev_out_dx")
            dqa, dka, dva, dga = _ret_bwd(rec["h"], c2a, s2a, lgt, rec["ro"], dyy, name=L + "ret_bwd")
            dqb, dkb, dvb = _dil_bwd(rec["h"], cb, shi, slo, strip, rec["dil_o"], rec["lse"], dyy, name=L + "dil_bwd")
            dh = [dqa, dka, dva, dga, dqb, dkb, dvb]
            g["in_t"] = _mm_tn_parts(dh, rec["xb"], name=L + "ev_in_dw")
            dep = put_g(layer, "mixer", g)
        else:
            g["out"] = _mm(rec["y"], dz1b, ta=True, name=L + "od_out_dw", out_dtype=BF16)
            dyy = _mm(dz1b, Wm["out"], tb=True, name=L + "od_out_dx")
            do, dgate, dnw = _gdn_post_bwd(rec["o"], rec["h"], rec["nw"], dyy, name=L + "gdn_post_bwd")
            gS["od_norm_w"][j] = dnw[0]
            dq, dk, dv, dsc, dal, ddt = _gdn_core_bwd(
                rec["q"], rec["k"], rec["v"], rec["h"], rec["alog"], rec["dtb"], rec["states"], rec["invs"], do,
                name=L + "gdn_bwd")
            gS["od_a_log"][j] = dal[:, 0, 0]
            gS["od_dt_bias"][j] = ddt[:, 0, 0]
            dhq, dhk, dhv, dwq, dwk, dwv = _gdn_prep_bwd(rec["h"], rec["cw"], dq, dk, dv, name=L + "gdn_prep_bwd")
            g["conv"] = jnp.concatenate([dwq, dwk, dwv], 1).astype(BF16)
            dh = [jnp.concatenate([dhq, dhk, dhv, dgate, dsc.astype(BF16)], 1)]
            g["in_t"] = _mm(dh[0], rec["xb"], ta=True, name=L + "od_in_dw", out_dtype=BF16)
            dep = put_g(layer, "mixer", g)
        below = (dh, Wm["in_t"], dz1, dep)
    grad_x = _axpy(_mm(jnp.concatenate(below[0], 1), below[1], name="L0_in_dx", dep=below[3]), below[2], name="grad_x")
    gS = {n: jnp.stack(v) for n, v in gS.items()}
    return loss, grad_x, gS


HBM = pl.BlockSpec(memory_space=pltpu.HBM)


def _me():
    return lax.axis_index("x"), lax.axis_index("y"), lax.axis_index("c")


def _my_index():
    x, y, c = _me()
    return 4 * x + 2 * y + c


SEM = pl.BlockSpec(memory_space=pltpu.SEMAPHORE)
ANY = pl.BlockSpec(memory_space=pl.ANY)
PLANS = {"scatter": (1, 2, 3, 4, 5, 6, 7), "spread": (1, 2, 4, 6), "relay": (2, 4, 6)}
SIBLING = 1


def _peer(kk):
    x, y, c = _me()
    return x ^ (kk >> 2), y ^ ((kk >> 1) & 1), c ^ (kk & 1)


def _peer_index(kk):
    px, py, pc = _peer(kk)
    return 4 * px + 2 * py + pc


def _job_copies(mode, srcs, lands, send_sems, recv_sems, incoming):
    myid = _my_index()
    plan = PLANS[mode]
    out = []
    for a in range(len(lands)):
        for idx, kk in enumerate(plan):
            if mode == "relay":
                to, src = _peer(SIBLING), lands[a].at[_peer_index(kk)]
                slot_there, slot_here = _peer_index(kk), _peer_index(kk ^ SIBLING)
            else:
                to, src = _peer(kk), (srcs[a] if mode == "spread" else srcs[a].at[_peer_index(kk)])
                slot_there, slot_here = myid, _peer_index(kk)
            sem = a * len(plan) + idx
            out.append(pltpu.make_async_remote_copy(
                src_ref=src, dst_ref=lands[a].at[slot_here if incoming else slot_there],
                send_sem=send_sems.at[sem], recv_sem=recv_sems.at[sem], device_id=to, device_id_type=MESH))
    return out


def _split_jobs(jobs, arrays):
    out, o = [], 0
    for (_, srcs, lands) in jobs:
        out.append((arrays[o:o + len(srcs)], arrays[o + len(srcs):o + len(srcs) + len(lands)]))
        o += len(srcs) + len(lands)
    return out


def _exchange_start(jobs, after, *, name):
    jobs = [(mode, list(srcs), [lax.empty((N_DEV, *s.shape) if mode == "spread" else s.shape, s.dtype) for s in srcs]
             if lands is None else list(lands)) for (mode, srcs, lands) in jobs]
    flat = [a for (_, srcs, lands) in jobs for a in (*srcs, *lands)]
    n, nj = len(flat), len(jobs)
    nsem = [len(PLANS[mode]) * len(lands) for (mode, _, lands) in jobs]

    def body(*refs):
        o = n + (0 if after is None else 1)
        sems, token = refs[o:o + 2 * nj], refs[o + 2 * nj + n]
        for ji, ((mode, _, _), (src, land)) in enumerate(zip(jobs, _split_jobs(jobs, refs[:n]))):
            for cp in _job_copies(mode, src, land, sems[2 * ji], sems[2 * ji + 1], False):
                cp.start()
        token[...] = jnp.zeros_like(token)

    outs = pl.pallas_call(
        body, name=name,
        out_shape=(*[pltpu.SemaphoreType.DMA((ns,)) for ns in nsem for _ in range(2)],
                   *[pltpu.HBM(a.shape, a.dtype) for a in flat], jax.ShapeDtypeStruct((8, LANES), F32)),
        in_specs=[HBM] * n + ([] if after is None else [ANY]),
        out_specs=(*[SEM] * (2 * nj), *[HBM] * n, pl.BlockSpec(memory_space=pltpu.VMEM)),
        input_output_aliases={i: 2 * nj + i for i in range(n)},
        compiler_params=pltpu.CompilerParams(has_side_effects=pltpu.SideEffectType.DATAFLOW_SIDE_EFFECTING),
    )(*[pltpu.with_memory_space_constraint(a, pltpu.HBM) for a in flat], *([] if after is None else [after]))
    thru = _split_jobs(jobs, list(outs[2 * nj:2 * nj + n]))
    started = [(mode, outs[2 * ji], outs[2 * ji + 1], src, land) for ji, ((mode, _, _), (src, land)) in enumerate(zip(jobs, thru))]
    return started, outs[2 * nj + n]


def _exchange_wait(started, after, *, name):
    jobs = [(mode, srcs, lands) for (mode, _, _, srcs, lands) in started]
    flat = [a for (_, srcs, lands) in jobs for a in (*srcs, *lands)]
    n, nj = len(flat), len(jobs)

    def body(*refs):
        sems = refs[n:n + 2 * nj]
        for ji, ((mode, _, _), (src, land)) in enumerate(zip(jobs, _split_jobs(jobs, refs[:n]))):
            for cp in _job_copies(mode, src, land, sems[2 * ji], sems[2 * ji + 1], True):
                cp.wait_send()
                cp.wait_recv()

    outs = pl.pallas_call(
        body, name=name, out_shape=tuple(pltpu.HBM(a.shape, a.dtype) for a in flat),
        in_specs=[HBM] * n + [SEM] * (2 * nj) + [ANY], out_specs=tuple([HBM] * n),
        input_output_aliases={i: i for i in range(n)},
        compiler_params=pltpu.CompilerParams(has_side_effects=pltpu.SideEffectType.DATAFLOW_SIDE_EFFECTING),
    )(*flat, *[s for (_, ss, rs, _, _) in started for s in (ss, rs)], after)
    return _split_jobs(jobs, list(outs))


def _sum8(land, *, name):
    _, rr, cc = land.shape
    tr = _row_tile(rr)

    def body(l_ref, o_ref):
        acc = l_ref[0].astype(F32)
        for d in range(1, N_DEV):
            acc = acc + l_ref[d].astype(F32)
        o_ref[...] = acc

    return pl.pallas_call(
        body, grid=(rr // tr,), in_specs=[pl.BlockSpec((N_DEV, tr, cc), lambda i: (0, i, 0))],
        out_specs=pl.BlockSpec((tr, cc), lambda i: (i, 0)), out_shape=jax.ShapeDtypeStruct((rr, cc), F32),
        name=name, compiler_params=_cp())(land)


def _row_tile(rr):
    for cand in (512, 384, 256, 192, 176, 128, 64, 32, 16, 8):
        if rr % cand == 0:
            return cand
    return rr


def _small_exchange(vec, *, name):
    rr = vec.shape[0]

    def body(v_ref, o_ref, send_sems, recv_sems):
        x, y, c = _me()
        myid = 4 * x + 2 * y + c
        o_ref[myid] = v_ref[...]
        cps = []
        for kk in range(1, N_DEV):
            px, py, pc = x ^ (kk >> 2), y ^ ((kk >> 1) & 1), c ^ (kk & 1)
            cps.append(pltpu.make_async_remote_copy(
                src_ref=v_ref, dst_ref=o_ref.at[myid], send_sem=send_sems.at[kk], recv_sem=recv_sems.at[kk],
                device_id=(px, py, pc), device_id_type=MESH))
        for cp in cps:
            cp.start()
        for kk in range(1, N_DEV):
            px, py, pc = x ^ (kk >> 2), y ^ ((kk >> 1) & 1), c ^ (kk & 1)
            pltpu.make_async_remote_copy(
                src_ref=v_ref, dst_ref=o_ref.at[4 * px + 2 * py + pc], send_sem=send_sems.at[kk],
                recv_sem=recv_sems.at[kk], device_id=(px, py, pc), device_id_type=MESH).wait_recv()
        for cp in cps:
            cp.wait_send()

    return pl.pallas_call(
        body, in_specs=[pl.BlockSpec(memory_space=pltpu.VMEM)], out_specs=pl.BlockSpec(memory_space=pltpu.VMEM),
        out_shape=jax.ShapeDtypeStruct((N_DEV, rr, LANES), F32),
        scratch_shapes=[pltpu.SemaphoreType.DMA((N_DEV,)), pltpu.SemaphoreType.DMA((N_DEV,))],
        name=name, compiler_params=pltpu.CompilerParams(has_side_effects=True))(vec)


def _adam_math(w, g, m, v):
    m = ADAM_B1 * m + (1.0 - ADAM_B1) * g
    v = ADAM_B2 * v + (1.0 - ADAM_B2) * (g * g)
    m_hat = m / (1.0 - ADAM_B1 ** ADAM_STEP)
    v_hat = v / (1.0 - ADAM_B2 ** ADAM_STEP)
    delta = -ADAM_LR * (m_hat / (jnp.sqrt(v_hat) + ADAM_EPS) + ADAM_WD * w)
    return delta, m, v


def _adamw_sharded(w, m, v, g, *, name):
    ll, rr, cc = w.shape
    tr = _row_tile(rr)

    def body(w_ref, m_ref, v_ref, g_ref, d_ref, nm_ref, nv_ref):
        d, nm, nv = _adam_math(w_ref[...], g_ref[...], m_ref[...], v_ref[...])
        d_ref[...] = d
        nm_ref[...] = nm
        nv_ref[...] = nv

    blk = pl.BlockSpec((1, tr, cc), lambda l, i: (l, i, 0))
    sh = jax.ShapeDtypeStruct((ll, rr, cc), F32)
    return pl.pallas_call(
        body, grid=(ll, rr // tr), in_specs=[blk] * 4, out_specs=[blk] * 3, out_shape=[sh] * 3,
        name=name, compiler_params=_cp())(w, m, v, g)


def _adamw_small(w, m, v, gall, *, name):
    rr = w.shape[0]

    def body(w_ref, m_ref, v_ref, g_ref, go_ref, d_ref, nm_ref, nv_ref):
        g = g_ref[0]
        for kk in range(1, N_DEV):
            g = g + g_ref[kk]
        d, nm, nv = _adam_math(w_ref[...], g, m_ref[...], v_ref[...])
        go_ref[...] = g
        d_ref[...] = d
        nm_ref[...] = nm
        nv_ref[...] = nv

    sh = jax.ShapeDtypeStruct((rr, LANES), F32)
    return pl.pallas_call(body, out_shape=[sh] * 4, name=name, compiler_params=_cp())(w, m, v, gall)


SHARDED = ("ev_w_in", "ev_w_out", "od_w_in", "od_conv_w", "od_w_out", "ffn_w_up", "ffn_conv_w", "ffn_w_down")
SMALL = ("od_a_log", "od_dt_bias", "od_norm_w", "ffn_conv_b", "ln1_g", "ln1_b", "ln2_g", "ln2_b")
ALL_W = ("ev_w_in", "ev_w_out", "od_w_in", "od_conv_w", "od_a_log", "od_dt_bias", "od_norm_w", "od_w_out",
         "ffn_w_up", "ffn_conv_w", "ffn_conv_b", "ffn_w_down", "ln1_g", "ln1_b", "ln2_g", "ln2_b")


def _layer_items(layer):
    j = layer // 2
    if layer % 2 == 0:
        mixer = [("in_t", "ev_w_in", j, "colT"), ("out", "ev_w_out", j, "row")]
    else:
        mixer = [("in_t", "od_w_in", j, "colT"), ("conv", "od_conv_w", j, "colsmall"), ("out", "od_w_out", j, "row")]
    return mixer + [("up_t", "ffn_w_up", layer, "colT"), ("fconv", "ffn_conv_w", layer, "colsmall"),
                    ("down", "ffn_w_down", layer, "row")]


def _to_send(kind, w, j):
    if kind == "colT":
        return w[j].T.astype(BF16)
    return w[j].astype(BF16) if kind == "row" else w[j]


def _from_gather(kind, name, g):
    if kind == "colsmall":
        return jnp.transpose(g, (1, 0, 2)).reshape(g.shape[1], -1)
    full = g.reshape(-1, g.shape[-1])
    if name == "od_w_in":
        full = jnp.pad(full, ((0, OD_IN_PAD - OD_IN), (0, 0)))
    return full


def _by_owner(kind, name, gfull):
    if kind == "colsmall":
        kk, c8 = gfull.shape
        return jnp.transpose(gfull.reshape(kk, N_DEV, c8 // N_DEV), (1, 0, 2))
    if name == "od_w_in":
        gfull = gfull[:OD_IN]
    return gfull.reshape(N_DEV, gfull.shape[0] // N_DEV, gfull.shape[1])


def _pack_small(d):
    flat = jnp.concatenate([d[n].reshape(-1) for n in SMALL])
    pad = (-flat.shape[0]) % (8 * LANES)
    return jnp.pad(flat, (0, pad)).reshape(-1, LANES)


def _unpack_small(packed, like):
    flat = packed.reshape(-1)
    out, off = {}, 0
    for n in SMALL:
        sz = int(np.prod(like[n].shape))
        out[n] = flat[off:off + sz].reshape(like[n].shape)
        off += sz
    return out


def kernel(x, positions, ev_w_in, ev_w_out, od_w_in, od_conv_w, od_a_log, od_dt_bias, od_norm_w, od_w_out, ffn_w_up, ffn_conv_w, ffn_conv_b, ffn_w_down, ln1_g, ln1_b, ln2_g, ln2_b, loss_target, m_ev_w_in, m_ev_w_out, m_od_w_in, m_od_conv_w, m_od_a_log, m_od_dt_bias, m_od_norm_w, m_od_w_out, m_ffn_w_up, m_ffn_conv_w, m_ffn_conv_b, m_ffn_w_down, m_ln1_g, m_ln1_b, m_ln2_g, m_ln2_b, v_ev_w_in, v_ev_w_out, v_od_w_in, v_od_conv_w, v_od_a_log, v_od_dt_bias, v_od_norm_w, v_od_w_out, v_ffn_w_up, v_ffn_conv_w, v_ffn_conv_b, v_ffn_w_down, v_ln1_g, v_ln1_b, v_ln2_g, v_ln2_b):
    w = dict(ev_w_in=ev_w_in, ev_w_out=ev_w_out, od_w_in=od_w_in, od_conv_w=od_conv_w, od_a_log=od_a_log,
             od_dt_bias=od_dt_bias, od_norm_w=od_norm_w, od_w_out=od_w_out, ffn_w_up=ffn_w_up, ffn_conv_w=ffn_conv_w,
             ffn_conv_b=ffn_conv_b, ffn_w_down=ffn_w_down, ln1_g=ln1_g, ln1_b=ln1_b, ln2_g=ln2_g, ln2_b=ln2_b)
    mom = dict(ev_w_in=m_ev_w_in, ev_w_out=m_ev_w_out, od_w_in=m_od_w_in, od_conv_w=m_od_conv_w, od_a_log=m_od_a_log,
               od_dt_bias=m_od_dt_bias, od_norm_w=m_od_norm_w, od_w_out=m_od_w_out, ffn_w_up=m_ffn_w_up,
               ffn_conv_w=m_ffn_conv_w, ffn_conv_b=m_ffn_conv_b, ffn_w_down=m_ffn_w_down, ln1_g=m_ln1_g,
               ln1_b=m_ln1_b, ln2_g=m_ln2_g, ln2_b=m_ln2_b)
    var = dict(ev_w_in=v_ev_w_in, ev_w_out=v_ev_w_out, od_w_in=v_od_w_in, od_conv_w=v_od_conv_w, od_a_log=v_od_a_log,
               od_dt_bias=v_od_dt_bias, od_norm_w=v_od_norm_w, od_w_out=v_od_w_out, ffn_w_up=v_ffn_w_up,
               ffn_conv_w=v_ffn_conv_w, ffn_conv_b=v_ffn_conv_b, ffn_w_down=v_ffn_w_down, ln1_g=v_ln1_g,
               ln1_b=v_ln1_b, ln2_g=v_ln2_g, ln2_b=v_ln2_b)

    myid = _my_index()
    small = {n: w[n] for n in SMALL}
    groups = [(layer, part) for layer in range(DEPTH) for part in ("mixer", "ffn")]

    def group_items(gi):
        layer, part = groups[gi]
        its = _layer_items(layer)
        return its[:-3] if part == "mixer" else its[-3:]

    level1, level2 = {}, {}

    def spread_job(gi):
        return ("spread", [_to_send(kind, w[n], j) for (_, n, j, kind) in group_items(gi)], None)

    def relay(gi, after, name):
        (srcs, lands), = _exchange_wait([level1.pop(gi)], after, name=name + "_wait")
        more = [spread_job(gi + 1)] if gi + 1 < len(groups) else []
        started, token = _exchange_start([("relay", [], lands)] + more, None, name=name + "_start")
        level2[gi] = (started[0], srcs)
        if more:
            level1[gi + 1] = started[1]
        return token

    def get_w(layer, part, after):
        gi = groups.index((layer, part))
        started, srcs = level2.pop(gi)
        (_, lands), = _exchange_wait([started], after, name=f"gather{gi}_wait")
        lands = [lax.dynamic_update_index_in_dim(l, s, myid, 0) for l, s in zip(lands, srcs)]
        return {key: _from_gather(kind, n, l) for (key, n, _, kind), l in zip(group_items(gi), lands)}, None

    def mid(layer, part, after):
        gi = groups.index((layer, part)) + 1
        return relay(gi, after, f"gather{gi}_relay") if gi < len(groups) else None

    landed = {}
    pending = []

    def scatter_finish(after):
        started, gi = pending.pop()
        (srcs, lands), = _exchange_wait([started], after, name=f"scatter{gi}_wait")
        for (key, _, _, _), l, s in zip(group_items(gi), lands, srcs):
            own = lax.dynamic_index_in_dim(s, myid, 0, keepdims=False)
            landed[(groups[gi][0], key)] = lax.dynamic_update_index_in_dim(l, own, myid, 0)

    def put_g(layer, part, g):
        gi = groups.index((layer, part))
        srcs = [_by_owner(kind, n, g[key]) for (key, n, _, kind) in group_items(gi)]
        (started,), token = _exchange_start([("scatter", srcs, None)], None, name=f"scatter{gi}_start")
        if pending:
            scatter_finish(token)
        pending.append((started, gi))
        return token

    (level1[0],), token = _exchange_start([spread_job(0)], None, name="gather0_spread_start")
    tables = _tables(positions[0] + token[0, 0].astype(jnp.int32))
    relay(0, tables[-1], "gather0_relay")
    loss, grad_x, gS = _local_step(x[0], tables, loss_target[0], get_w, mid, put_g, small)
    loss = lax.psum(loss, ("x", "y", "c"))

    outs_g, outs_d, outs_m, outs_v = {}, {}, {}, {}
    where = {n: [None] * w[n].shape[0] for n in SHARDED}
    for layer in range(DEPTH):
        for (key, n, j, kind) in _layer_items(layer):
            where[n][j] = (layer, key, kind)

    def update(n):
        g = jnp.stack([_sum8(landed[(layer, key)], name=f"L{layer}_{key}_sum") for (layer, key, _) in where[n]])
        if where[n][0][2] == "colT":
            tr = lambda a: jnp.swapaxes(a, 1, 2)
            d, nm, nv = _adamw_sharded(tr(w[n]), tr(mom[n]), tr(var[n]), g, name=f"adamw_{n}")
            outs_g[n], outs_d[n], outs_m[n], outs_v[n] = tr(g), tr(d), tr(nm), tr(nv)
        else:
            outs_g[n] = g
            outs_d[n], outs_m[n], outs_v[n] = _adamw_sharded(w[n], mom[n], var[n], g, name=f"adamw_{n}")

    last = {n for (_, n, _, _) in group_items(pending[0][1])}
    for n in SHARDED:
        if n not in last:
            update(n)
    scatter_finish(outs_d[[n for n in SHARDED if n not in last][-1]])
    for n in SHARDED:
        if n in last:
            update(n)

    gall = _small_exchange(_pack_small(gS), name="small_grads_exchange")
    g, d, nm, nv = _adamw_small(_pack_small({n: w[n] for n in SMALL}), _pack_small({n: mom[n] for n in SMALL}),
                                _pack_small({n: var[n] for n in SMALL}), gall, name="adamw_small")
    for dst, packed in ((outs_g, g), (outs_d, d), (outs_m, nm), (outs_v, nv)):
        dst.update(_unpack_small(packed, {n: w[n] for n in SMALL}))

    return (loss, grad_x[None], *[outs_g[n] for n in ALL_W], *[outs_d[n] for n in ALL_W],
            *[outs_m[n] for n in ALL_W], *[outs_v[n] for n in ALL_W])
```

```python
import functools
import math

import numpy as np
import jax
import jax.numpy as jnp
from jax import lax
from jax.experimental import pallas as pl
from jax.experimental.pallas import tpu as pltpu

F32 = jnp.float32
BF16 = jnp.bfloat16
MESH = pl.DeviceIdType.MESH

D_MODEL = 1024
SEQ = 2048
DEPTH = 4
N_DEV = 8
RET_HEADS, RET_DK, RET_DV = 4, 128, 256
RET_THETA = 10000.0
DIL_HEADS, DIL_HD = 8, 64
DIL_PAIRS = ((128, 1), (512, 4), (2048, 16))
ROPE_THETA = 500000.0
ROPE_DIMS = DIL_HD // 4
GDN_HEADS, GDN_DK, GDN_DV, GDN_CHUNK, GDN_CONV = 8, 128, 128, 64, 4
D_FF = 2816
FFN_CONV = 3
ALPHA = (2.0 * DEPTH) ** 0.25
EPS = 1e-5
RET_QK_W = RET_HEADS * RET_DK
RET_V_W = RET_HEADS * RET_DV
DIL_W = DIL_HEADS * DIL_HD
EV_IN = 2 * RET_QK_W + 2 * RET_V_W + 3 * DIL_W
EV_MIX = RET_V_W + DIL_W
GDN_W = GDN_HEADS * GDN_DK
OD_IN = 4 * GDN_W + 2 * GDN_HEADS
OD_IN_PAD = 4 * GDN_W + 128
ADAM_LR, ADAM_B1, ADAM_B2, ADAM_EPS, ADAM_WD, ADAM_STEP = 0.001, 0.9, 0.999, 1e-08, 0.01, 10

LANES = 128
VMEM_LIMIT = 56 * 1024 * 1024
ATT_BLK = 256
NEG = -1e30


def _cp(**kw):
    return pltpu.CompilerParams(vmem_limit_bytes=VMEM_LIMIT, **kw)


def _tile(n, cap):
    if n <= cap:
        return n
    best = None
    for t in range(LANES, cap + 1, LANES):
        if n % t == 0:
            best = t
    assert best is not None, (n, cap)
    return best


def _mm(a, b, *, ta=False, tb=False, name, out_dtype=F32, dep=None, tm=None, tn=None):
    m = a.shape[1] if ta else a.shape[0]
    k = a.shape[0] if ta else a.shape[1]
    n = b.shape[0] if tb else b.shape[1]
    assert (b.shape[1] if tb else b.shape[0]) == k
    assert a.dtype == BF16 and b.dtype == BF16
    if tn is None:
        tn = n if n <= 1024 else _tile(n, 512)
    if tm is None:
        tm = m if (tn < n and k <= 1024 and m <= 2048) else _tile(m, 512)
    dims = (((0 if ta else 1,), (1 if tb else 0,)), ((), ()))

    def body(a_ref, b_ref, *rest):
        o_ref = rest[-1]
        o_ref[...] = lax.dot_general(a_ref[...], b_ref[...], dims,
                                     preferred_element_type=F32).astype(o_ref.dtype)

    a_spec = pl.BlockSpec((k, tm), lambda i, j: (0, i)) if ta else pl.BlockSpec((tm, k), lambda i, j: (i, 0))
    b_spec = pl.BlockSpec((tn, k), lambda i, j: (j, 0)) if tb else pl.BlockSpec((k, tn), lambda i, j: (0, j))
    extra = [] if dep is None else [dep]
    return pl.pallas_call(
        body, grid=(m // tm, n // tn), in_specs=[a_spec, b_spec] + [pl.BlockSpec(memory_space=pl.ANY)] * len(extra),
        out_specs=pl.BlockSpec((tm, tn), lambda i, j: (i, j)),
        out_shape=jax.ShapeDtypeStruct((m, n), out_dtype), name=name, compiler_params=_cp())(a, b, *extra)


LN_ROWS = 256


def _ln_bwd(z, g, dya, dyb, *, name):
    t, d = z.shape
    two = dyb is not None

    def body(*refs):
        if two:
            z_ref, g_ref, dya_ref, dyb_ref, dz_ref, dzb_ref, dg_ref, db_ref = refs
            dy = dya_ref[...] + ALPHA * dyb_ref[...]
        else:
            z_ref, g_ref, dya_ref, dz_ref, dzb_ref, dg_ref, db_ref = refs
            dy = dya_ref[...]
        zz = z_ref[...]
        mu = jnp.mean(zz, -1, keepdims=True)
        zc = zz - mu
        var = jnp.mean(zc * zc, -1, keepdims=True)
        r = lax.rsqrt(var + EPS)
        xh = zc * r
        dxh = dy * g_ref[...]
        dz = r * (dxh - jnp.mean(dxh, -1, keepdims=True) - xh * jnp.mean(dxh * xh, -1, keepdims=True))
        dz_ref[...] = dz
        dzb_ref[...] = dz.astype(BF16)

        @pl.when(pl.program_id(0) == 0)
        def _():
            dg_ref[...] = jnp.zeros_like(dg_ref)
            db_ref[...] = jnp.zeros_like(db_ref)

        dg_ref[...] += jnp.sum(dy * xh, 0, keepdims=True)
        db_ref[...] += jnp.sum(dy, 0, keepdims=True)

    row = pl.BlockSpec((LN_ROWS, d), lambda i: (i, 0))
    vec = pl.BlockSpec((1, d), lambda i: (0, 0))
    ins = [z, g, dya] + ([dyb] if two else [])
    return pl.pallas_call(
        body, grid=(t // LN_ROWS,), in_specs=[row, vec, row] + ([row] if two else []),
        out_specs=[row, row, vec, vec],
        out_shape=[jax.ShapeDtypeStruct((t, d), F32), jax.ShapeDtypeStruct((t, d), BF16),
                   jax.ShapeDtypeStruct((1, d), F32), jax.ShapeDtypeStruct((1, d), F32)],
        name=name, compiler_params=_cp())(*ins)


def _ln_rows(k):
    return 256 if k > 4096 else 512


def _mm_ln_fwd(a, w, x, g, b, *, name, dep=None):
    t, k = a.shape
    d = w.shape[1]
    tm = _ln_rows(k)

    def body(a_ref, w_ref, x_ref, g_ref, b_ref, *rest):
        z_ref, y_ref, yb_ref = rest[-3:]
        z = ALPHA * x_ref[...] + _nn(a_ref[...], w_ref[...])
        mu = jnp.mean(z, -1, keepdims=True)
        zc = z - mu
        var = jnp.mean(zc * zc, -1, keepdims=True)
        y = zc * lax.rsqrt(var + EPS) * g_ref[...] + b_ref[...]
        z_ref[...] = z
        y_ref[...] = y
        yb_ref[...] = y.astype(BF16)

    row = pl.BlockSpec((tm, d), lambda i: (i, 0))
    vec = pl.BlockSpec((1, d), lambda i: (0, 0))
    extra = [] if dep is None else [dep]
    return pl.pallas_call(
        body, grid=(t // tm,),
        in_specs=[pl.BlockSpec((tm, k), lambda i: (i, 0)), pl.BlockSpec((k, d), lambda i: (0, 0)), row, vec, vec]
        + [pl.BlockSpec(memory_space=pl.ANY)] * len(extra),
        out_specs=[row, row, row],
        out_shape=[jax.ShapeDtypeStruct((t, d), F32), jax.ShapeDtypeStruct((t, d), F32), jax.ShapeDtypeStruct((t, d), BF16)],
        name=name, compiler_params=_cp())(a, w, x, g, b, *extra)


def _part_offsets(parts):
    offs, o = [], 0
    for p in parts:
        assert o % p.shape[1] == 0
        offs.append(o)
        o += p.shape[1]
    return offs, o


def _mm_ln_bwd(parts, w, z, g, dyb, *, name, dep=None):
    t = parts[0].shape[0]
    offs, k = _part_offsets(parts)
    d = w.shape[1]
    tm = _ln_rows(k)
    npart = len(parts)

    def body(*refs):
        a_refs, w_refs = refs[:npart], refs[npart:2 * npart]
        z_ref, g_ref, dyb_ref = refs[2 * npart:2 * npart + 3]
        dz_ref, dzb_ref, dg_ref, db_ref = refs[-4:]
        dy = ALPHA * dyb_ref[...]
        for a_ref, w_ref in zip(a_refs, w_refs):
            dy = dy + _nn(a_ref[...], w_ref[...])
        zz = z_ref[...]
        mu = jnp.mean(zz, -1, keepdims=True)
        zc = zz - mu
        var = jnp.mean(zc * zc, -1, keepdims=True)
        r = lax.rsqrt(var + EPS)
        xh = zc * r
        dxh = dy * g_ref[...]
        dz = r * (dxh - jnp.mean(dxh, -1, keepdims=True) - xh * jnp.mean(dxh * xh, -1, keepdims=True))
        dz_ref[...] = dz
        dzb_ref[...] = dz.astype(BF16)

        @pl.when(pl.program_id(0) == 0)
        def _():
            dg_ref[...] = jnp.zeros_like(dg_ref)
            db_ref[...] = jnp.zeros_like(db_ref)

        dg_ref[...] += jnp.sum(dy * xh, 0, keepdims=True)
        db_ref[...] += jnp.sum(dy, 0, keepdims=True)

    row = pl.BlockSpec((tm, d), lambda i: (i, 0))
    vec = pl.BlockSpec((1, d), lambda i: (0, 0))
    extra = [] if dep is None else [dep]
    a_specs = [pl.BlockSpec((tm, p.shape[1]), lambda i: (i, 0)) for p in parts]
    w_specs = [pl.BlockSpec((p.shape[1], d), functools.partial(lambda i, blk: (blk, 0), blk=o // p.shape[1]))
               for p, o in zip(parts, offs)]
    return pl.pallas_call(
        body, grid=(t // tm,),
        in_specs=a_specs + w_specs + [row, vec, row] + [pl.BlockSpec(memory_space=pl.ANY)] * len(extra),
        out_specs=[row, row, vec, vec],
        out_shape=[jax.ShapeDtypeStruct((t, d), F32), jax.ShapeDtypeStruct((t, d), BF16),
                   jax.ShapeDtypeStruct((1, d), F32), jax.ShapeDtypeStruct((1, d), F32)],
        name=name, compiler_params=_cp())(*parts, *([w] * npart), z, g, dyb, *extra)


def _mm_tn_parts(parts, b, *, name):
    t, n = b.shape
    offs, m = _part_offsets(parts)
    tm = min(512, min(_tile(p.shape[1], 512) for p in parts))
    assert all(p.shape[1] % tm == 0 for p in parts)
    first = [o // tm for o in offs]
    count = [p.shape[1] // tm for p in parts]
    npart = len(parts)

    def body(*refs):
        a_refs, b_ref, o_ref = refs[:npart], refs[npart], refs[npart + 1]
        i = pl.program_id(0)
        for a_ref, f, c in zip(a_refs, first, count):
            @pl.when((i >= f) & (i < f + c))
            def _(a_ref=a_ref):
                o_ref[...] = _tn(a_ref[...], b_ref[...]).astype(BF16)

    a_specs = [pl.BlockSpec((t, tm), functools.partial(lambda i, f, c: (0, jnp.clip(i - f, 0, c - 1)), f=f, c=c))
               for f, c in zip(first, count)]
    return pl.pallas_call(
        body, grid=(m // tm,), in_specs=a_specs + [pl.BlockSpec((t, n), lambda i: (0, 0))],
        out_specs=pl.BlockSpec((tm, n), lambda i: (i, 0)), out_shape=jax.ShapeDtypeStruct((m, n), BF16),
        name=name, compiler_params=_cp())(*parts, b)


def _axpy(a, b, *, name):
    t, d = a.shape

    def body(a_ref, b_ref, o_ref):
        o_ref[...] = a_ref[...] + ALPHA * b_ref[...]

    row = pl.BlockSpec((LN_ROWS, d), lambda i: (i, 0))
    return pl.pallas_call(body, grid=(t // LN_ROWS,), in_specs=[row, row], out_specs=row,
                          out_shape=jax.ShapeDtypeStruct((t, d), F32), name=name, compiler_params=_cp())(a, b)


def _loss_head(y, target, *, name):
    t, d = y.shape

    def body(y_ref, t_ref, dy_ref, l_ref):
        e = y_ref[...] - t_ref[...]
        dy_ref[...] = e * (1.0 / d)

        @pl.when(pl.program_id(0) == 0)
        def _():
            l_ref[...] = jnp.zeros_like(l_ref)

        l_ref[...] += jnp.zeros_like(l_ref) + 0.5 * jnp.sum(jnp.mean(e * e, -1, keepdims=True), 0, keepdims=True)

    row = pl.BlockSpec((LN_ROWS, d), lambda i: (i, 0))
    return pl.pallas_call(
        body, grid=(t // LN_ROWS,), in_specs=[row, row],
        out_specs=[row, pl.BlockSpec((1, LANES), lambda i: (0, 0))],
        out_shape=[jax.ShapeDtypeStruct((t, d), F32), jax.ShapeDtypeStruct((1, LANES), F32)],
        name=name, compiler_params=_cp())(y, target)


def _sig(x):
    return 1.0 / (1.0 + jnp.exp(-x))


def _silu(x):
    return x * _sig(x)


def _dsilu(x):
    s = _sig(x)
    return s * (1.0 + x * (1.0 - s))


def _shift_down(u, k, row):
    if k == 0:
        return u
    return jnp.where(row >= k, pltpu.roll(u, k, 0), 0.0)


def _shift_up(u, k, row):
    if k == 0:
        return u
    t = u.shape[0]
    return jnp.where(row < t - k, pltpu.roll(u, t - k, 0), 0.0)


def _dwconv(u, w_ref, row):
    kk = w_ref.shape[0]
    acc = None
    for j in range(kk):
        term = w_ref[j:j + 1, :] * _shift_down(u, kk - 1 - j, row)
        acc = term if acc is None else acc + term
    return acc


def _dwconv_bwd(u, w_ref, dc, row, dw_ref):
    kk = w_ref.shape[0]
    du = None
    for j in range(kk):
        term = w_ref[j:j + 1, :] * _shift_up(dc, kk - 1 - j, row)
        du = term if du is None else du + term
        dw_ref[j:j + 1, :] = jnp.sum(dc * _shift_down(u, kk - 1 - j, row), 0, keepdims=True)
    return du


CONV_ROWS = 256


def _rows(b):
    return pl.ds(pl.multiple_of(b * CONV_ROWS, CONV_ROWS), CONV_ROWS)


def _shifted_down(ref, b, k, row):
    cur = ref[_rows(b), :]
    if k == 0:
        return cur
    prev = jnp.where(b > 0, ref[_rows(jnp.maximum(b - 1, 0)), :], 0.0)
    return jnp.where(row >= k, pltpu.roll(cur, k, 0), pltpu.roll(prev, k, 0))


def _shifted_up(ref, b, k, row, nblk):
    cur = ref[_rows(b), :]
    if k == 0:
        return cur
    nxt = jnp.where(b < nblk - 1, ref[_rows(jnp.minimum(b + 1, nblk - 1)), :], 0.0)
    return jnp.where(row < CONV_ROWS - k, pltpu.roll(cur, CONV_ROWS - k, 0), pltpu.roll(nxt, CONV_ROWS - k, 0))


def _dwconv_blk(u_ref, w_ref, b, row):
    kk = w_ref.shape[0]
    views = [_shifted_down(u_ref, b, kk - 1 - j, row) for j in range(kk)]
    acc = None
    for j in range(kk):
        term = w_ref[j:j + 1, :] * views[j]
        acc = term if acc is None else acc + term
    return acc, views


def _dwconv_du_blk(dc_ref, w_ref, b, row, nblk):
    kk = w_ref.shape[0]
    du = None
    for j in range(kk):
        term = w_ref[j:j + 1, :] * _shifted_up(dc_ref, b, kk - 1 - j, row, nblk)
        du = term if du is None else du + term
    return du


FFN_TC = 256


def _ffn_up_mid(x, up_t, cw, cb, *, name, dep=None):
    t, d = x.shape
    nb = D_FF // FFN_TC

    def body(x_ref, ugt_ref, uvt_ref, wg_ref, wv_ref, bg_ref, bv_ref, *rest):
        ug_ref, uv_ref, a_ref = rest[-3:]
        xx = x_ref[...]
        row = lax.broadcasted_iota(jnp.int32, (t, FFN_TC), 0)
        ug = _nt(xx, ugt_ref[...])
        ug_ref[...] = ug
        uv = _nt(xx, uvt_ref[...])
        uv_ref[...] = uv
        cg = _dwconv(ug, wg_ref, row) + bg_ref[...]
        cv = _dwconv(uv, wv_ref, row) + bv_ref[...]
        a_ref[...] = (_silu(cg) * cv).astype(BF16)

    col = pl.BlockSpec((t, FFN_TC), lambda j: (0, j))
    wt = lambda off: pl.BlockSpec((FFN_TC, d), lambda j: (j + off, 0))
    wsp = lambda off: pl.BlockSpec((FFN_CONV, FFN_TC), lambda j: (0, j + off))
    bsp = lambda off: pl.BlockSpec((1, FFN_TC), lambda j: (0, j + off))
    extra = [] if dep is None else [dep]
    return pl.pallas_call(
        body, grid=(nb,),
        in_specs=[pl.BlockSpec((t, d), lambda j: (0, 0)), wt(0), wt(nb), wsp(0), wsp(nb), bsp(0), bsp(nb)]
        + [pl.BlockSpec(memory_space=pl.ANY)] * len(extra),
        out_specs=[col, col, col],
        out_shape=[jax.ShapeDtypeStruct((t, D_FF), F32), jax.ShapeDtypeStruct((t, D_FF), F32),
                   jax.ShapeDtypeStruct((t, D_FF), BF16)],
        name=name, compiler_params=_cp())(x, up_t, up_t, cw, cw, cb, cb, *extra)


def _ffn_mid_bwd(ug, uv, cw, cb, dz, down, *, name):
    t, d = dz.shape
    nb = D_FF // FFN_TC

    nblk = t // CONV_ROWS

    def body(ug_ref, uv_ref, wg_ref, wv_ref, bg_ref, bv_ref, dz_ref, dn_ref,
             dug_ref, duv_ref, dwg_ref, dwv_ref, dbg_ref, dbv_ref, da_ref, dcg_s, dcv_s):
        da_ref[...] = _nt(dz_ref[...], dn_ref[...])
        row = lax.broadcasted_iota(jnp.int32, (CONV_ROWS, FFN_TC), 0)
        zero = jnp.zeros((1, FFN_TC), F32)

        def first(b, acc):
            cg, ugs = _dwconv_blk(ug_ref, wg_ref, b, row)
            cv, uvs = _dwconv_blk(uv_ref, wv_ref, b, row)
            cg = cg + bg_ref[...]
            cv = cv + bv_ref[...]
            da_ = da_ref[_rows(b), :]
            dcv = da_ * _silu(cg)
            dcg = da_ * cv * _dsilu(cg)
            dcg_s[_rows(b), :] = dcg
            dcv_s[_rows(b), :] = dcv
            red = [jnp.sum(dcg * s, 0, keepdims=True) for s in ugs] + [jnp.sum(dcg, 0, keepdims=True)]
            red += [jnp.sum(dcv * s, 0, keepdims=True) for s in uvs] + [jnp.sum(dcv, 0, keepdims=True)]
            return tuple(a + r for a, r in zip(acc, red))

        acc = lax.fori_loop(0, nblk, first, (zero,) * (2 * FFN_CONV + 2))
        for j in range(FFN_CONV):
            dwg_ref[j:j + 1, :] = acc[j]
            dwv_ref[j:j + 1, :] = acc[FFN_CONV + 1 + j]
        dbg_ref[...] = acc[FFN_CONV]
        dbv_ref[...] = acc[2 * FFN_CONV + 1]

        def second(b, carry):
            dug_ref[_rows(b), :] = _dwconv_du_blk(dcg_s, wg_ref, b, row, nblk).astype(BF16)
            duv_ref[_rows(b), :] = _dwconv_du_blk(dcv_s, wv_ref, b, row, nblk).astype(BF16)
            return carry

        lax.fori_loop(0, nblk, second, 0)

    col = pl.BlockSpec((t, FFN_TC), lambda j: (0, j))
    wsp = lambda off: pl.BlockSpec((FFN_CONV, FFN_TC), lambda j: (0, j + off))
    bsp = lambda off: pl.BlockSpec((1, FFN_TC), lambda j: (0, j + off))
    outs = pl.pallas_call(
        body, grid=(nb,),
        in_specs=[col, col, wsp(0), wsp(nb), bsp(0), bsp(nb), pl.BlockSpec((t, d), lambda j: (0, 0)),
                  pl.BlockSpec((FFN_TC, d), lambda j: (j, 0))],
        out_specs=[col, col, wsp(0), wsp(0), bsp(0), bsp(0)],
        out_shape=[jax.ShapeDtypeStruct((t, D_FF), BF16), jax.ShapeDtypeStruct((t, D_FF), BF16),
                   jax.ShapeDtypeStruct((FFN_CONV, D_FF), F32), jax.ShapeDtypeStruct((FFN_CONV, D_FF), F32),
                   jax.ShapeDtypeStruct((1, D_FF), F32), jax.ShapeDtypeStruct((1, D_FF), F32)],
        scratch_shapes=[pltpu.VMEM((t, FFN_TC), F32), pltpu.VMEM((t, FFN_TC), F32), pltpu.VMEM((t, FFN_TC), F32)],
        name=name, compiler_params=_cp())(ug, uv, cw, cw, cb, cb, dz, down)
    dug, duv, dwg, dwv, dbg, dbv = outs
    return [dug, duv], jnp.concatenate([dwg, dwv], 1), jnp.concatenate([dbg, dbv], 1)


def _rot_a(x, c2, s2):
    return x * c2 + pltpu.roll(x, RET_DK // 2, 1) * s2


def _rot_a_t(dy, c2, s2):
    return dy * c2 + pltpu.roll(dy * s2, RET_DK // 2, 1)


def _decay_tile(lg, blk_diff):
    r = lax.broadcasted_iota(jnp.int32, (ATT_BLK, ATT_BLK), 0)
    c = lax.broadcasted_iota(jnp.int32, (ATT_BLK, ATT_BLK), 1)
    rel = r - c + blk_diff * ATT_BLK
    return jnp.where(rel >= 0, jnp.exp(jnp.maximum(rel, 0).astype(F32) * lg), 0.0)


def _nt(a, b):
    return lax.dot_general(a, b, (((1,), (1,)), ((), ())), preferred_element_type=F32)


def _nn(a, b):
    return lax.dot_general(a, b, (((1,), (0,)), ((), ())), preferred_element_type=F32)


def _tn(a, b):
    return lax.dot_general(a, b, (((0,), (0,)), ((), ())), preferred_element_type=F32)


def _ret_specs(t):
    q = pl.BlockSpec((t, RET_DK), lambda h: (0, h))
    k = pl.BlockSpec((t, RET_DK), lambda h: (0, RET_HEADS + h))
    v = pl.BlockSpec((t, RET_DV), lambda h: (0, RET_HEADS + h))
    g = pl.BlockSpec((t, RET_DV), lambda h: (0, 2 * RET_HEADS + h))
    tab = pl.BlockSpec((t, RET_DK), lambda h: (0, 0))
    lg = pl.BlockSpec((1, 1, LANES), lambda h: (h, 0, 0))
    return q, k, v, g, tab, lg


def _ret_fwd(h, c2, s2, lgt, *, name):
    t = h.shape[0]
    nblk = t // ATT_BLK
    scale = RET_DK ** -0.5

    def body(q_ref, k_ref, v_ref, g_ref, c_ref, s_ref, lg_ref, o_ref, ya_ref, qs, ks, vs):
        c2_, s2_ = c_ref[...], s_ref[...]
        qs[...] = _rot_a(q_ref[...], c2_, s2_).astype(BF16)
        ks[...] = (_rot_a(k_ref[...], c2_, s2_) * scale).astype(BF16)
        vs[...] = v_ref[...].astype(BF16)
        lg = lg_ref[0, :, 0:1]
        for i in range(nblk):
            qi = qs[pl.ds(i * ATT_BLK, ATT_BLK), :]
            acc = jnp.zeros((ATT_BLK, RET_DV), F32)
            for j in range(i + 1):
                sl = pl.ds(j * ATT_BLK, ATT_BLK)
                s = _nt(qi, ks[sl, :]) * _decay_tile(lg, i - j)
                acc = acc + _nn(s.astype(BF16), vs[sl, :])
            rows = pl.ds(i * ATT_BLK, ATT_BLK)
            o_ref[rows, :] = acc
            r = lax.rsqrt(jnp.mean(acc * acc, -1, keepdims=True) + EPS)
            ya_ref[rows, :] = (acc * r * _silu(g_ref[rows, :])).astype(BF16)

    q, k, v, g, tab, lg = _ret_specs(t)
    out = pl.BlockSpec((t, RET_DV), lambda hh: (0, hh))
    return pl.pallas_call(
        body, grid=(RET_HEADS,), in_specs=[q, k, v, g, tab, tab, lg], out_specs=[out, out],
        out_shape=[jax.ShapeDtypeStruct((t, RET_V_W), F32), jax.ShapeDtypeStruct((t, RET_V_W), BF16)],
        scratch_shapes=[pltpu.VMEM((t, RET_DK), BF16), pltpu.VMEM((t, RET_DK), BF16), pltpu.VMEM((t, RET_DV), BF16)],
        name=name, compiler_params=_cp())(h, h, h, h, c2, s2, lgt)


def _ret_bwd(h, c2, s2, lgt, o, dy, *, name):
    t = h.shape[0]
    nblk = t // ATT_BLK
    scale = RET_DK ** -0.5

    def body(q_ref, k_ref, v_ref, g_ref, c_ref, s_ref, lg_ref, o_ref, dy_ref,
             dq_ref, dk_ref, dv_ref, dg_ref, qs, ks, vs, dos, dka, dva):
        c2_, s2_ = c_ref[...], s_ref[...]
        qs[...] = _rot_a(q_ref[...], c2_, s2_).astype(BF16)
        ks[...] = (_rot_a(k_ref[...], c2_, s2_) * scale).astype(BF16)
        vs[...] = v_ref[...].astype(BF16)
        lg = lg_ref[0, :, 0:1]
        oo = o_ref[...]
        gg = g_ref[...]
        dya = dy_ref[...]
        r = lax.rsqrt(jnp.mean(oo * oo, -1, keepdims=True) + EPS)
        rn = oo * r
        dg_ref[...] = (dya * rn * _dsilu(gg)).astype(BF16)
        drn = dya * _silu(gg)
        dos[...] = (r * (drn - rn * jnp.mean(drn * rn, -1, keepdims=True))).astype(BF16)
        dka[...] = jnp.zeros_like(dka)
        dva[...] = jnp.zeros_like(dva)
        for i in range(nblk):
            rows = pl.ds(i * ATT_BLK, ATT_BLK)
            qi = qs[rows, :]
            doi = dos[rows, :]
            dqa = jnp.zeros((ATT_BLK, RET_DK), F32)
            for j in range(i + 1):
                sl = pl.ds(j * ATT_BLK, ATT_BLK)
                dt_ = _decay_tile(lg, i - j)
                kj = ks[sl, :]
                s = (_nt(qi, kj) * dt_).astype(BF16)
                ds = (_nt(doi, vs[sl, :]) * dt_).astype(BF16)
                dqa = dqa + _nn(ds, kj)
                dka[sl, :] += _tn(ds, qi)
                dva[sl, :] += _tn(s, doi)
            dq_ref[rows, :] = _rot_a_t(dqa, c_ref[rows, :], s_ref[rows, :]).astype(BF16)
        dk_ref[...] = (_rot_a_t(dka[...], c2_, s2_) * scale).astype(BF16)
        dv_ref[...] = dva[...].astype(BF16)

    q, k, v, g, tab, lg = _ret_specs(t)
    blk_v = pl.BlockSpec((t, RET_DV), lambda hh: (0, hh))
    blk_k = pl.BlockSpec((t, RET_DK), lambda hh: (0, hh))
    return pl.pallas_call(
        body, grid=(RET_HEADS,), in_specs=[q, k, v, g, tab, tab, lg, blk_v, blk_v],
        out_specs=[blk_k, blk_k, blk_v, blk_v],
        out_shape=[jax.ShapeDtypeStruct((t, RET_QK_W), BF16), jax.ShapeDtypeStruct((t, RET_QK_W), BF16),
                   jax.ShapeDtypeStruct((t, RET_V_W), BF16), jax.ShapeDtypeStruct((t, RET_V_W), BF16)],
        scratch_shapes=[pltpu.VMEM((t, RET_DK), BF16), pltpu.VMEM((t, RET_DK), BF16), pltpu.VMEM((t, RET_DV), BF16),
                        pltpu.VMEM((t, RET_DV), BF16), pltpu.VMEM((t, RET_DK), F32), pltpu.VMEM((t, RET_DV), F32)],
        name=name, compiler_params=_cp())(h, h, h, h, c2, s2, lgt, o, dy)


def _rot_b(x, cb, shi, slo):
    return x * cb + pltpu.roll(x, ROPE_DIMS // 2, 1) * shi + pltpu.roll(x, LANES - ROPE_DIMS // 2, 1) * slo


def _rot_b_t(dy, cb, shi, slo):
    return dy * cb + pltpu.roll(dy * shi, LANES - ROPE_DIMS // 2, 1) + pltpu.roll(dy * slo, ROPE_DIMS // 2, 1)


def _dil_specs(t):
    base = (2 * RET_QK_W + 2 * RET_V_W) // LANES
    npair = DIL_W // LANES
    q = pl.BlockSpec((t, LANES), lambda p: (0, base + p))
    k = pl.BlockSpec((t, LANES), lambda p: (0, base + npair + p))
    v = pl.BlockSpec((t, LANES), lambda p: (0, base + 2 * npair + p))
    tab = pl.BlockSpec((t, LANES), lambda p: (0, 0))
    strip = pl.BlockSpec((ATT_BLK, t), lambda p: (0, 0))
    pair = pl.BlockSpec((t, LANES), lambda p: (0, p))
    return q, k, v, tab, strip, pair


def _dil_fwd(h, cb, shi, slo, strip, *, name):
    t = h.shape[0]
    nblk = t // ATT_BLK
    scale = DIL_HD ** -0.5

    def body(q_ref, k_ref, v_ref, cb_ref, shi_ref, slo_ref, st_ref, o_ref, yb_ref, lse_ref, qs, ks, vs):
        cb_, shi_, slo_ = cb_ref[...], shi_ref[...], slo_ref[...]
        lane = lax.broadcasted_iota(jnp.int32, (t, LANES), 1)
        qr = _rot_b(q_ref[...], cb_, shi_, slo_) * scale
        qs[0] = jnp.where(lane < DIL_HD, qr, 0.0).astype(BF16)
        qs[1] = jnp.where(lane >= DIL_HD, qr, 0.0).astype(BF16)
        ks[...] = _rot_b(k_ref[...], cb_, shi_, slo_).astype(BF16)
        vs[...] = v_ref[...].astype(BF16)
        lane_b = lax.broadcasted_iota(jnp.int32, (ATT_BLK, LANES), 1)
        for i in range(nblk):
            w = (i + 1) * ATT_BLK
            rows = pl.ds(i * ATT_BLK, ATT_BLK)
            logc = st_ref[:, t - w:t]
            outs, lses = [], []
            for hd in range(2):
                s = _nt(qs[hd, rows, :], ks[0:w, :]) + logc
                m = jnp.max(s, -1, keepdims=True)
                p = jnp.exp(s - m)
                l = jnp.sum(p, -1, keepdims=True)
                outs.append(_nn(p.astype(BF16), vs[0:w, :]) / l)
                lses.append(m + jnp.log(l))
            o = jnp.where(lane_b < DIL_HD, outs[0], outs[1])
            o_ref[rows, :] = o
            yb_ref[rows, :] = o.astype(BF16)
            lse_ref[rows, :] = jnp.where(lane_b < DIL_HD, lses[0], lses[1])

    q, k, v, tab, strip_spec, pair = _dil_specs(t)
    return pl.pallas_call(
        body, grid=(DIL_W // LANES,), in_specs=[q, k, v, tab, tab, tab, strip_spec], out_specs=[pair, pair, pair],
        out_shape=[jax.ShapeDtypeStruct((t, DIL_W), F32), jax.ShapeDtypeStruct((t, DIL_W), BF16),
                   jax.ShapeDtypeStruct((t, DIL_W), F32)],
        scratch_shapes=[pltpu.VMEM((2, t, LANES), BF16), pltpu.VMEM((t, LANES), BF16), pltpu.VMEM((t, LANES), BF16)],
        name=name, compiler_params=_cp())(h, h, h, cb, shi, slo, strip)


def _dil_bwd(h, cb, shi, slo, strip, o, lse, dy, *, name):
    t = h.shape[0]
    nblk = t // ATT_BLK
    scale = DIL_HD ** -0.5

    def body(q_ref, k_ref, v_ref, cb_ref, shi_ref, slo_ref, st_ref, o_ref, lse_ref, dy_ref,
             dq_ref, dk_ref, dv_ref, qs, ks, vs, dos, dls, dka, dva):
        cb_, shi_, slo_ = cb_ref[...], shi_ref[...], slo_ref[...]
        lane = lax.broadcasted_iota(jnp.int32, (t, LANES), 1)
        qr = _rot_b(q_ref[...], cb_, shi_, slo_) * scale
        qs[0] = jnp.where(lane < DIL_HD, qr, 0.0).astype(BF16)
        qs[1] = jnp.where(lane >= DIL_HD, qr, 0.0).astype(BF16)
        ks[...] = _rot_b(k_ref[...], cb_, shi_, slo_).astype(BF16)
        vs[...] = v_ref[...].astype(BF16)
        do = dy_ref[...]
        prod = do * o_ref[...]
        d0 = jnp.sum(jnp.where(lane < DIL_HD, prod, 0.0), -1, keepdims=True)
        d1 = jnp.sum(jnp.where(lane >= DIL_HD, prod, 0.0), -1, keepdims=True)
        dls[...] = jnp.where(lane < DIL_HD, d0, d1)
        dos[0] = jnp.where(lane < DIL_HD, do, 0.0).astype(BF16)
        dos[1] = jnp.where(lane >= DIL_HD, do, 0.0).astype(BF16)
        dka[...] = jnp.zeros_like(dka)
        dva[...] = jnp.zeros_like(dva)
        lane_b = lax.broadcasted_iota(jnp.int32, (ATT_BLK, LANES), 1)
        for i in range(nblk):
            w = (i + 1) * ATT_BLK
            rows = pl.ds(i * ATT_BLK, ATT_BLK)
            logc = st_ref[:, t - w:t]
            dqs = []
            for hd in range(2):
                col = hd * DIL_HD
                qh = qs[hd, rows, :]
                doh = dos[hd, rows, :]
                lse_h = lse_ref[rows, col:col + 1]
                dl_h = dls[rows, col:col + 1]
                p = jnp.exp(_nt(qh, ks[0:w, :]) + logc - lse_h)
                dp = _nt(doh, vs[0:w, :])
                ds = (p * (dp - dl_h)).astype(BF16)
                dqs.append(_nn(ds, ks[0:w, :]))
                dka[0:w, :] += _tn(ds, qh)
                dva[0:w, :] += _tn(p.astype(BF16), doh)
            dq = jnp.where(lane_b < DIL_HD, dqs[0], dqs[1]) * scale
            dq_ref[rows, :] = _rot_b_t(dq, cb_ref[rows, :], shi_ref[rows, :], slo_ref[rows, :]).astype(BF16)
        dk_ref[...] = _rot_b_t(dka[...], cb_, shi_, slo_).astype(BF16)
        dv_ref[...] = dva[...].astype(BF16)

    q, k, v, tab, strip_spec, pair = _dil_specs(t)
    dy_spec = pl.BlockSpec((t, LANES), lambda p: (0, RET_V_W // LANES + p))
    return pl.pallas_call(
        body, grid=(DIL_W // LANES,), in_specs=[q, k, v, tab, tab, tab, strip_spec, pair, pair, dy_spec],
        out_specs=[pair, pair, pair],
        out_shape=[jax.ShapeDtypeStruct((t, DIL_W), BF16)] * 3,
        scratch_shapes=[pltpu.VMEM((2, t, LANES), BF16), pltpu.VMEM((t, LANES), BF16), pltpu.VMEM((t, LANES), BF16),
                        pltpu.VMEM((2, t, LANES), BF16), pltpu.VMEM((t, LANES), F32),
                        pltpu.VMEM((t, LANES), F32), pltpu.VMEM((t, LANES), F32)],
        name=name, compiler_params=_cp())(h, h, h, cb, shi, slo, strip, o, lse, dy)


def _gdn_prep_fwd(h, cw, *, name):
    t = h.shape[0]
    qscale = GDN_DK ** -0.5

    def body(hq_ref, hk_ref, hv_ref, wq_ref, wk_ref, wv_ref, q_ref, k_ref, v_ref):
        row = lax.broadcasted_iota(jnp.int32, (t, GDN_DK), 0)
        sq = _silu(_dwconv(hq_ref[...], wq_ref, row))
        sk = _silu(_dwconv(hk_ref[...], wk_ref, row))
        q_ref[0] = sq * lax.rsqrt(jnp.sum(sq * sq, -1, keepdims=True) + 1e-6) * qscale
        k_ref[0] = sk * lax.rsqrt(jnp.sum(sk * sk, -1, keepdims=True) + 1e-6)
        v_ref[0] = _silu(_dwconv(hv_ref[...], wv_ref, row))

    hs = lambda off: pl.BlockSpec((t, GDN_DK), lambda i: (0, i + off))
    ws = lambda off: pl.BlockSpec((GDN_CONV, GDN_DK), lambda i: (0, i + off))
    out = pl.BlockSpec((1, t, GDN_DK), lambda i: (i, 0, 0))
    return pl.pallas_call(
        body, grid=(GDN_HEADS,), in_specs=[hs(0), hs(8), hs(16), ws(0), ws(8), ws(16)], out_specs=[out, out, out],
        out_shape=[jax.ShapeDtypeStruct((GDN_HEADS, t, GDN_DK), F32)] * 3,
        name=name, compiler_params=_cp())(h, h, h, cw, cw, cw)


def _gdn_prep_bwd(h, cw, dq, dk, dv, *, name):
    t = h.shape[0]
    qscale = GDN_DK ** -0.5

    def body(hq_ref, hk_ref, hv_ref, wq_ref, wk_ref, wv_ref, dq_ref, dk_ref, dv_ref,
             dhq_ref, dhk_ref, dhv_ref, dwq_ref, dwk_ref, dwv_ref):
        row = lax.broadcasted_iota(jnp.int32, (t, GDN_DK), 0)

        def one(h_ref, w_ref, d_ref, dh_ref, dw_ref, norm, sc):
            u = h_ref[...]
            c = _dwconv(u, w_ref, row)
            d = d_ref[0]
            if norm:
                s = _silu(c)
                r = lax.rsqrt(jnp.sum(s * s, -1, keepdims=True) + 1e-6)
                n = s * r
                d = d * sc
                d = r * (d - n * jnp.sum(d * n, -1, keepdims=True))
            dc = d * _dsilu(c)
            dh_ref[...] = _dwconv_bwd(u, w_ref, dc, row, dw_ref).astype(BF16)

        one(hq_ref, wq_ref, dq_ref, dhq_ref, dwq_ref, True, qscale)
        one(hk_ref, wk_ref, dk_ref, dhk_ref, dwk_ref, True, 1.0)
        one(hv_ref, wv_ref, dv_ref, dhv_ref, dwv_ref, False, 1.0)

    hs = lambda off: pl.BlockSpec((t, GDN_DK), lambda i: (0, i + off))
    ws = lambda off: pl.BlockSpec((GDN_CONV, GDN_DK), lambda i: (0, i + off))
    hd = pl.BlockSpec((1, t, GDN_DK), lambda i: (i, 0, 0))
    return pl.pallas_call(
        body, grid=(GDN_HEADS,), in_specs=[hs(0), hs(8), hs(16), ws(0), ws(8), ws(16), hd, hd, hd],
        out_specs=[hs(0), hs(0), hs(0), ws(0), ws(0), ws(0)],
        out_shape=[jax.ShapeDtypeStruct((t, GDN_W), BF16)] * 3 + [jax.ShapeDtypeStruct((GDN_CONV, GDN_W), F32)] * 3,
        name=name, compiler_params=_cp())(h, h, h, cw, cw, cw, dq, dk, dv)


def _make_mm2(wide):
    def raw(a, b, dims):
        if wide:
            return lax.dot_general(a, b, (dims, ((), ())), precision=lax.Precision.HIGHEST, preferred_element_type=F32)
        return lax.dot_general(a.astype(BF16), b.astype(BF16), (dims, ((), ())), preferred_element_type=F32)

    @jax.custom_vjp
    def nn(a, b):
        return raw(a, b, ((1,), (0,)))

    @jax.custom_vjp
    def nt(a, b):
        return raw(a, b, ((1,), (1,)))

    @jax.custom_vjp
    def tn(a, b):
        return raw(a, b, ((0,), (0,)))

    nn.defvjp(lambda a, b: (nn(a, b), (a, b)), lambda r, g: (nt(g, r[1]), tn(r[0], g)))
    nt.defvjp(lambda a, b: (nt(a, b), (a, b)), lambda r, g: (nn(g, r[1]), tn(g, r[0])))
    tn.defvjp(lambda a, b: (tn(a, b), (a, b)), lambda r, g: (nt(r[1], g), nn(r[0], g)))
    return nn, nt, tn


_NN, _NT, _TN = _make_mm2(False)
_NNW, _NTW, _TNW = _make_mm2(True)


def _square_masks(c):
    ri = lax.broadcasted_iota(jnp.int32, (c, c), 0)
    ci = lax.broadcasted_iota(jnp.int32, (c, c), 1)
    return ri >= ci, ri > ci, ri == ci


def _cumsum_rows(m):
    tri, _, _ = _square_masks(m.shape[0])
    return _NNW(tri.astype(F32), m)


def _transpose_sq(m):
    _, _, eye = _square_masks(m.shape[0])
    return _NTW(eye.astype(F32), m)


@jax.custom_vjp
def _inv_unit_lower(l):
    c = l.shape[0]
    _, _, eye = _square_masks(c)
    p = -l
    t = eye.astype(F32) + p
    for _ in range(int(math.log2(c)) - 1):
        p = _NNW(p, p)
        t = t + _NNW(t, p)
    return t


def _inv_fwd(l):
    t = _inv_unit_lower(l)
    return t, t


def _inv_bwd(t, dt):
    return (-_NTW(_TNW(t, dt), t),)


_inv_unit_lower.defvjp(_inv_fwd, _inv_bwd)


@jax.custom_vjp
def _inv_known(l, t):
    return t


_inv_known.defvjp(lambda l, t: (t, t), lambda t, dt: (_inv_bwd(t, dt)[0], jnp.zeros_like(t)))


def _softplus(x):
    return jnp.maximum(x, 0.0) + jnp.log1p(jnp.exp(-jnp.abs(x)))


def _gdn_chunk(q, k, v, braw, araw, alog, dtb, state, inv=None):
    c = q.shape[0]
    dv = v.shape[1]
    tri, strict, _ = _square_masks(c)
    beta = _sig(braw)
    g = -jnp.exp(alog) * _softplus(araw + dtb)
    gcm = _cumsum_rows(g * jnp.ones((c, c), F32))
    gct = _transpose_sq(gcm)
    decay = jnp.where(tri, jnp.exp(jnp.where(tri, gcm - gct, 0.0)), 0.0)
    gc = jnp.sum(gcm, 1, keepdims=True) * (1.0 / c)
    glast = jnp.sum(g, 0, keepdims=True)
    egc = jnp.exp(gc)
    kb = k * beta
    low = jnp.where(strict, _NT(kb, k) * decay, 0.0)
    tm = _inv_unit_lower(low) if inv is None else _inv_known(low, inv)
    sol = _NNW(tm, jnp.concatenate([v * beta, kb * egc], 1))
    u, w = sol[:, :dv], sol[:, dv:]
    attn = jnp.where(tri, _NT(q, k) * decay, 0.0)
    k_dec = k * jnp.exp(glast - gc)
    q_dec = q * egc
    v_new = u - _NN(w, state)
    o = _NN(q_dec, state) + _NN(attn, v_new)
    new_state = state * jnp.exp(glast) + _TN(k_dec, v_new)
    return o, new_state, tm


def _gdn_specs(t, rev):
    nch = t // GDN_CHUNK
    cm = (lambda n: nch - 1 - n) if rev else (lambda n: n)
    tok = pl.BlockSpec((GDN_HEADS, GDN_CHUNK, GDN_DK), lambda n: (0, cm(n), 0))
    par = pl.BlockSpec((GDN_HEADS, 1, LANES), lambda n: (0, 0, 0))
    st = pl.BlockSpec((GDN_HEADS, 1, GDN_DK, GDN_DV), lambda n: (0, cm(n), 0, 0))
    inv = pl.BlockSpec((GDN_HEADS, GDN_CHUNK, GDN_CHUNK), lambda n: (0, cm(n), 0))
    sc = pl.BlockSpec((GDN_CHUNK, LANES), lambda n: (cm(n), 4 * GDN_W // LANES))
    return tok, par, st, inv, sc


def _head_columns(sc_ref, first):
    return jnp.stack([sc_ref[:, first + hh:first + hh + 1] for hh in range(GDN_HEADS)])


def _gdn_core_fwd(q, k, v, h, alog, dtb, *, name):
    t = q.shape[1]
    nch = t // GDN_CHUNK

    def body(q_ref, k_ref, v_ref, sc_ref, al_ref, dt_ref, o_ref, st_ref, inv_ref, state):
        @pl.when(pl.program_id(0) == 0)
        def _():
            state[...] = jnp.zeros_like(state)

        s0 = state[...]
        st_ref[:, 0] = s0
        o, s1, tm = jax.vmap(_gdn_chunk)(q_ref[...], k_ref[...], v_ref[...], _head_columns(sc_ref, 0),
                                         _head_columns(sc_ref, GDN_HEADS), al_ref[:, :, 0:1], dt_ref[:, :, 0:1], s0)
        o_ref[...] = o
        inv_ref[...] = tm
        state[...] = s1

    tok, par, st, inv, sc = _gdn_specs(t, False)
    return pl.pallas_call(
        body, grid=(nch,), in_specs=[tok, tok, tok, sc, par, par], out_specs=[tok, st, inv],
        out_shape=[jax.ShapeDtypeStruct((GDN_HEADS, t, GDN_DV), F32),
                   jax.ShapeDtypeStruct((GDN_HEADS, nch, GDN_DK, GDN_DV), F32),
                   jax.ShapeDtypeStruct((GDN_HEADS, t, GDN_CHUNK), F32)],
        scratch_shapes=[pltpu.VMEM((GDN_HEADS, GDN_DK, GDN_DV), F32)],
        name=name, compiler_params=_cp())(q, k, v, h, alog, dtb)


def _gdn_core_bwd(q, k, v, h, alog, dtb, states, invs, do, *, name):
    t = q.shape[1]
    nch = t // GDN_CHUNK

    def body(q_ref, k_ref, v_ref, sc_ref, al_ref, dt_ref, st_ref, inv_ref, do_ref,
             dq_ref, dk_ref, dv_ref, dsc_ref, dal_ref, ddt_ref, dstate):
        @pl.when(pl.program_id(0) == 0)
        def _():
            dstate[...] = jnp.zeros_like(dstate)
            dal_ref[...] = jnp.zeros_like(dal_ref)
            ddt_ref[...] = jnp.zeros_like(ddt_ref)

        args = (q_ref[...], k_ref[...], v_ref[...], _head_columns(sc_ref, 0), _head_columns(sc_ref, GDN_HEADS),
                al_ref[:, :, 0:1], dt_ref[:, :, 0:1], st_ref[:, 0])
        tm = inv_ref[...]

        def chunk(*a):
            return jax.vmap(_gdn_chunk)(*a, tm)[:2]

        _, pull = jax.vjp(chunk, *args)
        dq, dk, dv, dbr, dar, dal, ddt, ds = pull((do_ref[...], dstate[...]))
        dq_ref[...] = dq
        dk_ref[...] = dk
        dv_ref[...] = dv
        lane = lax.broadcasted_iota(jnp.int32, (GDN_CHUNK, LANES), 1)
        dsc = jnp.zeros((GDN_CHUNK, LANES), F32)
        for hh in range(GDN_HEADS):
            dsc = jnp.where(lane == hh, dbr[hh], dsc)
            dsc = jnp.where(lane == GDN_HEADS + hh, dar[hh], dsc)
        dsc_ref[...] = dsc
        dal_ref[...] += dal + jnp.zeros((GDN_HEADS, 1, LANES), F32)
        ddt_ref[...] += ddt + jnp.zeros((GDN_HEADS, 1, LANES), F32)
        dstate[...] = ds

    tok, par, st, inv, sc = _gdn_specs(t, True)
    tokshape = jax.ShapeDtypeStruct((GDN_HEADS, t, GDN_DK), F32)
    parshape = jax.ShapeDtypeStruct((GDN_HEADS, 1, LANES), F32)
    nch_map = pl.BlockSpec((GDN_CHUNK, LANES), lambda n: (nch - 1 - n, 0))
    return pl.pallas_call(
        body, grid=(nch,), in_specs=[tok, tok, tok, sc, par, par, st, inv, tok],
        out_specs=[tok, tok, tok, nch_map, par, par],
        out_shape=[tokshape] * 3 + [jax.ShapeDtypeStruct((t, LANES), F32)] + [parshape] * 2,
        scratch_shapes=[pltpu.VMEM((GDN_HEADS, GDN_DK, GDN_DV), F32)],
        name=name, compiler_params=_cp())(q, k, v, h, alog, dtb, states, invs, do)


GDN_ROWS = 512


def _gdn_post_fwd(o, h, nw, *, name):
    t = o.shape[1]

    def body(o_ref, g_ref, nw_ref, y_ref):
        oo = o_ref[0]
        r = lax.rsqrt(jnp.mean(oo * oo, -1, keepdims=True) + EPS)
        y_ref[...] = (oo * r * nw_ref[...] * _silu(g_ref[...])).astype(BF16)

    return pl.pallas_call(
        body, grid=(GDN_HEADS, t // GDN_ROWS),
        in_specs=[pl.BlockSpec((1, GDN_ROWS, GDN_DV), lambda hh, i: (hh, i, 0)),
                  pl.BlockSpec((GDN_ROWS, GDN_DV), lambda hh, i: (i, 3 * GDN_HEADS + hh)),
                  pl.BlockSpec((1, GDN_DV), lambda hh, i: (0, 0))],
        out_specs=pl.BlockSpec((GDN_ROWS, GDN_DV), lambda hh, i: (i, hh)),
        out_shape=jax.ShapeDtypeStruct((t, GDN_W), BF16), name=name, compiler_params=_cp())(o, h, nw)


def _gdn_post_bwd(o, h, nw, dy, *, name):
    t = o.shape[1]

    def body(o_ref, g_ref, nw_ref, dy_ref, do_ref, dg_ref, dnw_ref):
        oo, gg, nw_, dy_ = o_ref[0], g_ref[...], nw_ref[...], dy_ref[...]
        r = lax.rsqrt(jnp.mean(oo * oo, -1, keepdims=True) + EPS)
        n = oo * r
        sg = _silu(gg)
        dg_ref[...] = (dy_ * n * nw_ * _dsilu(gg)).astype(BF16)
        dn = dy_ * sg * nw_
        do_ref[0] = r * (dn - n * jnp.mean(dn * n, -1, keepdims=True))

        @pl.when((pl.program_id(0) == 0) & (pl.program_id(1) == 0))
        def _():
            dnw_ref[...] = jnp.zeros_like(dnw_ref)

        dnw_ref[...] += jnp.sum(dy_ * sg * n, 0, keepdims=True)

    return pl.pallas_call(
        body, grid=(GDN_HEADS, t // GDN_ROWS),
        in_specs=[pl.BlockSpec((1, GDN_ROWS, GDN_DV), lambda hh, i: (hh, i, 0)),
                  pl.BlockSpec((GDN_ROWS, GDN_DV), lambda hh, i: (i, 3 * GDN_HEADS + hh)),
                  pl.BlockSpec((1, GDN_DV), lambda hh, i: (0, 0)),
                  pl.BlockSpec((GDN_ROWS, GDN_DV), lambda hh, i: (i, hh))],
        out_specs=[pl.BlockSpec((1, GDN_ROWS, GDN_DV), lambda hh, i: (hh, i, 0)),
                   pl.BlockSpec((GDN_ROWS, GDN_DV), lambda hh, i: (i, hh)),
                   pl.BlockSpec((1, GDN_DV), lambda hh, i: (0, 0))],
        out_shape=[jax.ShapeDtypeStruct((GDN_HEADS, t, GDN_DV), F32), jax.ShapeDtypeStruct((t, GDN_W), BF16),
                   jax.ShapeDtypeStruct((1, GDN_DV), F32)],
        name=name, compiler_params=_cp())(o, h, nw, dy)


def _tables(positions):
    pos = positions.astype(F32)[:, None]
    half = RET_DK // 2
    inv = jnp.power(RET_THETA, -jnp.arange(half, dtype=F32) * 2.0 / RET_DK)
    ang = pos * inv
    cos, sin = jnp.cos(ang), jnp.sin(ang)
    c2a = jnp.concatenate([cos, cos], 1)
    s2a = jnp.concatenate([-sin, sin], 1)
    hb = ROPE_DIMS // 2
    invb = jnp.power(ROPE_THETA, -jnp.arange(hb, dtype=F32) * 2.0 / ROPE_DIMS)
    angb = pos * invb
    cosb, sinb = jnp.cos(angb), jnp.sin(angb)
    t = pos.shape[0]
    ones = jnp.ones((t, DIL_HD - ROPE_DIMS), F32)
    zeros = jnp.zeros((t, DIL_HD - ROPE_DIMS), F32)
    z8 = jnp.zeros((t, hb), F32)
    cb = jnp.concatenate([cosb, cosb, ones] * 2, 1)
    shi = jnp.concatenate([z8, sinb, zeros] * 2, 1)
    slo = jnp.concatenate([-sinb, z8, zeros] * 2, 1)
    lg = jnp.log1p(-jnp.power(2.0, -5.0 - jnp.arange(RET_HEADS, dtype=F32)))
    lgt = jnp.broadcast_to(lg[:, None, None], (RET_HEADS, 1, LANES))
    delta = jnp.arange(ATT_BLK, dtype=jnp.int32)[:, None] + (SEQ - ATT_BLK) - jnp.arange(SEQ, dtype=jnp.int32)[None, :]
    cnt = jnp.zeros(delta.shape, F32)
    for (w, d) in DIL_PAIRS:
        cnt = cnt + ((delta >= 0) & (delta <= w) & (delta % d == 0)).astype(F32)
    strip = jnp.where(cnt > 0, jnp.log(jnp.maximum(cnt, 1.0)), NEG)
    return c2a, s2a, cb, shi, slo, lgt, strip


def _local_step(x, tables, target, get_w, mid, put_g, small):
    c2a, s2a, cb, shi, slo, lgt, strip = tables
    t = x.shape[0]
    saved = []
    xf = x
    xb = x.astype(BF16)
    for layer in range(DEPTH):
        j = layer // 2
        L = f"L{layer}_"
        W, dep = get_w(layer, "mixer", xb)
        rec = {"x": xf, "xb": xb}
        if layer % 2 == 0:
            h = _mm(xb, W["in_t"], tb=True, name=L + "ev_in", dep=dep)
            ro, ya = _ret_fwd(h, c2a, s2a, lgt, name=L + "ret_fwd")
            do_, yb, lse = _dil_fwd(h, cb, shi, slo, strip, name=L + "dil_fwd")
            y = jnp.concatenate([ya, yb], 1)
            rec.update(h=h, ro=ro, dil_o=do_, lse=lse, y=y)
        else:
            h = _mm(xb, W["in_t"], tb=True, name=L + "od_in", dep=dep)
            cw = W["conv"]
            q, k, v = _gdn_prep_fwd(h, cw, name=L + "gdn_prep")
            alog = jnp.broadcast_to(small["od_a_log"][j][:, None, None], (GDN_HEADS, 1, LANES))
            dtb = jnp.broadcast_to(small["od_dt_bias"][j][:, None, None], (GDN_HEADS, 1, LANES))
            o, states, invs = _gdn_core_fwd(q, k, v, h, alog, dtb, name=L + "gdn_fwd")
            nw = small["od_norm_w"][j][None, :]
            y = _gdn_post_fwd(o, h, nw, name=L + "gdn_post")
            rec.update(h=h, q=q, k=k, v=v, alog=alog, dtb=dtb, states=states, invs=invs, o=o, y=y, nw=nw, cw=cw)
        z1, x1, x1b = _mm_ln_fwd(y, W["out"], xf, small["ln1_g"][layer][None], small["ln1_b"][layer][None],
                                 name=L + "out_ln1", dep=mid(layer, "mixer", y))
        rec["Wm"] = W
        W, dep = get_w(layer, "ffn", x1b)
        rec["Wf"] = W
        fcw = W["fconv"]
        fcb = small["ffn_conv_b"][layer][None]
        ug, uv, a = _ffn_up_mid(x1b, W["up_t"], fcw, fcb, name=L + "ffn_up_mid", dep=dep)
        z2, x2, x2b = _mm_ln_fwd(a, W["down"], x1, small["ln2_g"][layer][None], small["ln2_b"][layer][None],
                                 name=L + "down_ln2", dep=mid(layer, "ffn", a))
        rec.update(z1=z1, x1b=x1b, ug=ug, uv=uv, a=a, z2=z2, fcw=fcw, fcb=fcb)
        saved.append(rec)
        xf, xb = x2, x2b

    dy, lossv = _loss_head(xf, target, name="loss_head")
    loss = lossv[0, 0]

    gS = {n: [None] * small[n].shape[0] for n in small}
    below = None
    for layer in reversed(range(DEPTH)):
        j = layer // 2
        L = f"L{layer}_"
        rec = saved[layer]
        Wm, Wf = rec["Wm"], rec["Wf"]
        g = {}
        if below is None:
            dz2, dz2b, dg2, db2 = _ln_bwd(rec["z2"], small["ln2_g"][layer][None], dy, None, name=L + "ln2_bwd")
        else:
            dz2, dz2b, dg2, db2 = _mm_ln_bwd(below[0], below[1], rec["z2"], small["ln2_g"][layer][None], below[2],
                                             name=L + "ln2_bwd", dep=below[3])
        gS["ln2_g"][layer], gS["ln2_b"][layer] = dg2[0], db2[0]
        g["down"] = _mm(rec["a"], dz2b, ta=True, name=L + "ffn_down_dw", out_dtype=BF16)
        du, dcw, dcb = _ffn_mid_bwd(rec["ug"], rec["uv"], rec["fcw"], rec["fcb"], dz2b, Wf["down"], name=L + "ffn_mid_bwd")
        g["fconv"] = dcw.astype(BF16)
        gS["ffn_conv_b"][layer] = dcb[0]
        g["up_t"] = _mm_tn_parts(du, rec["x1b"], name=L + "ffn_up_dw")
        dep = put_g(layer, "ffn", g)
        dz1, dz1b, dg1, db1 = _mm_ln_bwd(du, Wf["up_t"], rec["z1"], small["ln1_g"][layer][None], dz2,
                                         name=L + "ln1_bwd", dep=dep)
        gS["ln1_g"][layer], gS["ln1_b"][layer] = dg1[0], db1[0]
        g = {}
        if layer % 2 == 0:
            g["out"] = _mm(rec["y"], dz1b, ta=True, name=L + "ev_out_dw", out_dtype=BF16)
            dyy = _mm(dz1b, Wm["out"], tb=True, name=L + "ev_out_dx")
            dqa, dka, dva, dga = _ret_bwd(rec["h"], c2a, s2a, lgt, rec["ro"], dyy, name=L + "ret_bwd")
            dqb, dkb, dvb = _dil_bwd(rec["h"], cb, shi, slo, strip, rec["dil_o"], rec["lse"], dyy, name=L + "dil_bwd")
            dh = [dqa, dka, dva, dga, dqb, dkb, dvb]
            g["in_t"] = _mm_tn_parts(dh, rec["xb"], name=L + "ev_in_dw")
            dep = put_g(layer, "mixer", g)
        else:
            g["out"] = _mm(rec["y"], dz1b, ta=True, name=L + "od_out_dw", out_dtype=BF16)
            dyy = _mm(dz1b, Wm["out"], tb=True, name=L + "od_out_dx")
            do, dgate, dnw = _gdn_post_bwd(rec["o"], rec["h"], rec["nw"], dyy, name=L + "gdn_post_bwd")
            gS["od_norm_w"][j] = dnw[0]
            dq, dk, dv, dsc, dal, ddt = _gdn_core_bwd(
                rec["q"], rec["k"], rec["v"], rec["h"], rec["alog"], rec["dtb"], rec["states"], rec["invs"], do,
                name=L + "gdn_bwd")
            gS["od_a_log"][j] = dal[:, 0, 0]
            gS["od_dt_bias"][j] = ddt[:, 0, 0]
            dhq, dhk, dhv, dwq, dwk, dwv = _gdn_prep_bwd(rec["h"], rec["cw"], dq, dk, dv, name=L + "gdn_prep_bwd")
            g["conv"] = jnp.concatenate([dwq, dwk, dwv], 1).astype(BF16)
            dh = [jnp.concatenate([dhq, dhk, dhv, dgate, dsc.astype(BF16)], 1)]
            g["in_t"] = _mm(dh[0], rec["xb"], ta=True, name=L + "od_in_dw", out_dtype=BF16)
            dep = put_g(layer, "mixer", g)
        below = (dh, Wm["in_t"], dz1, dep)
    grad_x = _axpy(_mm(jnp.concatenate(below[0], 1), below[1], name="L0_in_dx", dep=below[3]), below[2], name="grad_x")
    gS = {n: jnp.stack(v) for n, v in gS.items()}
    return loss, grad_x, gS


HBM = pl.BlockSpec(memory_space=pltpu.HBM)


def _me():
    return lax.axis_index("x"), lax.axis_index("y"), lax.axis_index("c")


def _my_index():
    x, y, c = _me()
    return 4 * x + 2 * y + c


SEM = pl.BlockSpec(memory_space=pltpu.SEMAPHORE)
ANY = pl.BlockSpec(memory_space=pl.ANY)
PLANS = {"scatter": (1, 2, 3, 4, 5, 6, 7), "spread": (1, 2, 4, 6), "relay": (2, 4, 6)}
SIBLING = 1


def _peer(kk):
    x, y, c = _me()
    return x ^ (kk >> 2), y ^ ((kk >> 1) & 1), c ^ (kk & 1)


def _peer_index(kk):
    px, py, pc = _peer(kk)
    return 4 * px + 2 * py + pc


def _job_copies(mode, srcs, lands, send_sems, recv_sems, incoming):
    myid = _my_index()
    plan = PLANS[mode]
    out = []
    for a in range(len(lands)):
        for idx, kk in enumerate(plan):
            if mode == "relay":
                to, src = _peer(SIBLING), lands[a].at[_peer_index(kk)]
                slot_there, slot_here = _peer_index(kk), _peer_index(kk ^ SIBLING)
            else:
                to, src = _peer(kk), (srcs[a] if mode == "spread" else srcs[a].at[_peer_index(kk)])
                slot_there, slot_here = myid, _peer_index(kk)
            sem = a * len(plan) + idx
            out.append(pltpu.make_async_remote_copy(
                src_ref=src, dst_ref=lands[a].at[slot_here if incoming else slot_there],
                send_sem=send_sems.at[sem], recv_sem=recv_sems.at[sem], device_id=to, device_id_type=MESH))
    return out


def _split_jobs(jobs, arrays):
    out, o = [], 0
    for (_, srcs, lands) in jobs:
        out.append((arrays[o:o + len(srcs)], arrays[o + len(srcs):o + len(srcs) + len(lands)]))
        o += len(srcs) + len(lands)
    return out


def _exchange_start(jobs, after, *, name):
    jobs = [(mode, list(srcs), [lax.empty((N_DEV, *s.shape) if mode == "spread" else s.shape, s.dtype) for s in srcs]
             if lands is None else list(lands)) for (mode, srcs, lands) in jobs]
    flat = [a for (_, srcs, lands) in jobs for a in (*srcs, *lands)]
    n, nj = len(flat), len(jobs)
    nsem = [len(PLANS[mode]) * len(lands) for (mode, _, lands) in jobs]

    def body(*refs):
        o = n + (0 if after is None else 1)
        sems, token = refs[o:o + 2 * nj], refs[o + 2 * nj + n]
        for ji, ((mode, _, _), (src, land)) in enumerate(zip(jobs, _split_jobs(jobs, refs[:n]))):
            for cp in _job_copies(mode, src, land, sems[2 * ji], sems[2 * ji + 1], False):
                cp.start()
        token[...] = jnp.zeros_like(token)

    outs = pl.pallas_call(
        body, name=name,
        out_shape=(*[pltpu.SemaphoreType.DMA((ns,)) for ns in nsem for _ in range(2)],
                   *[pltpu.HBM(a.shape, a.dtype) for a in flat], jax.ShapeDtypeStruct((8, LANES), F32)),
        in_specs=[HBM] * n + ([] if after is None else [ANY]),
        out_specs=(*[SEM] * (2 * nj), *[HBM] * n, pl.BlockSpec(memory_space=pltpu.VMEM)),
        input_output_aliases={i: 2 * nj + i for i in range(n)},
        compiler_params=pltpu.CompilerParams(has_side_effects=pltpu.SideEffectType.DATAFLOW_SIDE_EFFECTING),
    )(*[pltpu.with_memory_space_constraint(a, pltpu.HBM) for a in flat], *([] if after is None else [after]))
    thru = _split_jobs(jobs, list(outs[2 * nj:2 * nj + n]))
    started = [(mode, outs[2 * ji], outs[2 * ji + 1], src, land) for ji, ((mode, _, _), (src, land)) in enumerate(zip(jobs, thru))]
    return started, outs[2 * nj + n]


def _exchange_wait(started, after, *, name):
    jobs = [(mode, srcs, lands) for (mode, _, _, srcs, lands) in started]
    flat = [a for (_, srcs, lands) in jobs for a in (*srcs, *lands)]
    n, nj = len(flat), len(jobs)

    def body(*refs):
        sems = refs[n:n + 2 * nj]
        for ji, ((mode, _, _), (src, land)) in enumerate(zip(jobs, _split_jobs(jobs, refs[:n]))):
            for cp in _job_copies(mode, src, land, sems[2 * ji], sems[2 * ji + 1], True):
                cp.wait_send()
                cp.wait_recv()

    outs = pl.pallas_call(
        body, name=name, out_shape=tuple(pltpu.HBM(a.shape, a.dtype) for a in flat),
        in_specs=[HBM] * n + [SEM] * (2 * nj) + [ANY], out_specs=tuple([HBM] * n),
        input_output_aliases={i: i for i in range(n)},
        compiler_params=pltpu.CompilerParams(has_side_effects=pltpu.SideEffectType.DATAFLOW_SIDE_EFFECTING),
    )(*flat, *[s for (_, ss, rs, _, _) in started for s in (ss, rs)], after)
    return _split_jobs(jobs, list(outs))


def _sum8(land, *, name):
    _, rr, cc = land.shape
    tr = _row_tile(rr)

    def body(l_ref, o_ref):
        acc = l_ref[0].astype(F32)
        for d in range(1, N_DEV):
            acc = acc + l_ref[d].astype(F32)
        o_ref[...] = acc

    return pl.pallas_call(
        body, grid=(rr // tr,), in_specs=[pl.BlockSpec((N_DEV, tr, cc), lambda i: (0, i, 0))],
        out_specs=pl.BlockSpec((tr, cc), lambda i: (i, 0)), out_shape=jax.ShapeDtypeStruct((rr, cc), F32),
        name=name, compiler_params=_cp())(land)


def _row_tile(rr):
    for cand in (512, 384, 256, 192, 176, 128, 64, 32, 16, 8):
        if rr % cand == 0:
            return cand
    return rr


def _small_exchange(vec, *, name):
    rr = vec.shape[0]

    def body(v_ref, o_ref, send_sems, recv_sems):
        x, y, c = _me()
        myid = 4 * x + 2 * y + c
        o_ref[myid] = v_ref[...]
        cps = []
        for kk in range(1, N_DEV):
            px, py, pc = x ^ (kk >> 2), y ^ ((kk >> 1) & 1), c ^ (kk & 1)
            cps.append(pltpu.make_async_remote_copy(
                src_ref=v_ref, dst_ref=o_ref.at[myid], send_sem=send_sems.at[kk], recv_sem=recv_sems.at[kk],
                device_id=(px, py, pc), device_id_type=MESH))
        for cp in cps:
            cp.start()
        for kk in range(1, N_DEV):
            px, py, pc = x ^ (kk >> 2), y ^ ((kk >> 1) & 1), c ^ (kk & 1)
            pltpu.make_async_remote_copy(
                src_ref=v_ref, dst_ref=o_ref.at[4 * px + 2 * py + pc], send_sem=send_sems.at[kk],
                recv_sem=recv_sems.at[kk], device_id=(px, py, pc), device_id_type=MESH).wait_recv()
        for cp in cps:
            cp.wait_send()

    return pl.pallas_call(
        body, in_specs=[pl.BlockSpec(memory_space=pltpu.VMEM)], out_specs=pl.BlockSpec(memory_space=pltpu.VMEM),
        out_shape=jax.ShapeDtypeStruct((N_DEV, rr, LANES), F32),
        scratch_shapes=[pltpu.SemaphoreType.DMA((N_DEV,)), pltpu.SemaphoreType.DMA((N_DEV,))],
        name=name, compiler_params=pltpu.CompilerParams(has_side_effects=True))(vec)


def _adam_math(w, g, m, v):
    m = ADAM_B1 * m + (1.0 - ADAM_B1) * g
    v = ADAM_B2 * v + (1.0 - ADAM_B2) * (g * g)
    m_hat = m / (1.0 - ADAM_B1 ** ADAM_STEP)
    v_hat = v / (1.0 - ADAM_B2 ** ADAM_STEP)
    delta = -ADAM_LR * (m_hat / (jnp.sqrt(v_hat) + ADAM_EPS) + ADAM_WD * w)
    return delta, m, v


def _adamw_sharded(w, m, v, g, *, name):
    ll, rr, cc = w.shape
    tr = _row_tile(rr)

    def body(w_ref, m_ref, v_ref, g_ref, d_ref, nm_ref, nv_ref):
        d, nm, nv = _adam_math(w_ref[...], g_ref[...], m_ref[...], v_ref[...])
        d_ref[...] = d
        nm_ref[...] = nm
        nv_ref[...] = nv

    blk = pl.BlockSpec((1, tr, cc), lambda l, i: (l, i, 0))
    sh = jax.ShapeDtypeStruct((ll, rr, cc), F32)
    return pl.pallas_call(
        body, grid=(ll, rr // tr), in_specs=[blk] * 4, out_specs=[blk] * 3, out_shape=[sh] * 3,
        name=name, compiler_params=_cp())(w, m, v, g)


def _adamw_small(w, m, v, gall, *, name):
    rr = w.shape[0]

    def body(w_ref, m_ref, v_ref, g_ref, go_ref, d_ref, nm_ref, nv_ref):
        g = g_ref[0]
        for kk in range(1, N_DEV):
            g = g + g_ref[kk]
        d, nm, nv = _adam_math(w_ref[...], g, m_ref[...], v_ref[...])
        go_ref[...] = g
        d_ref[...] = d
        nm_ref[...] = nm
        nv_ref[...] = nv

    sh = jax.ShapeDtypeStruct((rr, LANES), F32)
    return pl.pallas_call(body, out_shape=[sh] * 4, name=name, compiler_params=_cp())(w, m, v, gall)


SHARDED = ("ev_w_in", "ev_w_out", "od_w_in", "od_conv_w", "od_w_out", "ffn_w_up", "ffn_conv_w", "ffn_w_down")
SMALL = ("od_a_log", "od_dt_bias", "od_norm_w", "ffn_conv_b", "ln1_g", "ln1_b", "ln2_g", "ln2_b")
ALL_W = ("ev_w_in", "ev_w_out", "od_w_in", "od_conv_w", "od_a_log", "od_dt_bias", "od_norm_w", "od_w_out",
         "ffn_w_up", "ffn_conv_w", "ffn_conv_b", "ffn_w_down", "ln1_g", "ln1_b", "ln2_g", "ln2_b")


def _layer_items(layer):
    j = layer // 2
    if layer % 2 == 0:
        mixer = [("in_t", "ev_w_in", j, "colT"), ("out", "ev_w_out", j, "row")]
    else:
        mixer = [("in_t", "od_w_in", j, "colT"), ("conv", "od_conv_w", j, "colsmall"), ("out", "od_w_out", j, "row")]
    return mixer + [("up_t", "ffn_w_up", layer, "colT"), ("fconv", "ffn_conv_w", layer, "colsmall"),
                    ("down", "ffn_w_down", layer, "row")]


OD_SHARD = OD_IN // N_DEV
OD_SHARD_PAD = OD_IN_PAD // N_DEV


def _od_pack(g, *, name):
    d = g.shape[-1]

    def body(g_ref, o_ref):
        for n in range(N_DEV):
            o_ref[OD_SHARD * n:OD_SHARD * (n + 1), :] = g_ref[n, 0:OD_SHARD, :]
        o_ref[OD_IN:OD_IN_PAD, :] = jnp.zeros((OD_IN_PAD - OD_IN, d), g.dtype)

    return pl.pallas_call(body, out_shape=jax.ShapeDtypeStruct((OD_IN_PAD, d), g.dtype), name=name,
                          compiler_params=_cp())(g)


def _od_unpack(full, *, name):
    d = full.shape[-1]

    def body(f_ref, o_ref):
        for n in range(N_DEV):
            o_ref[n, 0:OD_SHARD, :] = f_ref[OD_SHARD * n:OD_SHARD * (n + 1), :]
            o_ref[n, OD_SHARD:OD_SHARD_PAD, :] = jnp.zeros((OD_SHARD_PAD - OD_SHARD, d), full.dtype)

    return pl.pallas_call(body, out_shape=jax.ShapeDtypeStruct((N_DEV, OD_SHARD_PAD, d), full.dtype), name=name,
                          compiler_params=_cp())(full)


def _to_send(kind, name, w, j):
    if kind == "colT":
        s = w[j].T.astype(BF16)
        return jnp.pad(s, ((0, OD_SHARD_PAD - OD_SHARD), (0, 0))) if name == "od_w_in" else s
    return w[j].astype(BF16) if kind == "row" else w[j]


def _from_gather(kind, name, g, tag):
    if kind == "colsmall":
        return jnp.transpose(g, (1, 0, 2)).reshape(g.shape[1], -1)
    if name == "od_w_in":
        return _od_pack(g, name=tag + "_pack")
    return g.reshape(-1, g.shape[-1])


def _by_owner(kind, name, gfull, tag):
    if kind == "colsmall":
        kk, c8 = gfull.shape
        return jnp.transpose(gfull.reshape(kk, N_DEV, c8 // N_DEV), (1, 0, 2))
    if name == "od_w_in":
        return _od_unpack(gfull, name=tag + "_unpack")
    return gfull.reshape(N_DEV, gfull.shape[0] // N_DEV, gfull.shape[1])


def _pack_small(d):
    flat = jnp.concatenate([d[n].reshape(-1) for n in SMALL])
    pad = (-flat.shape[0]) % (8 * LANES)
    return jnp.pad(flat, (0, pad)).reshape(-1, LANES)


def _unpack_small(packed, like):
    flat = packed.reshape(-1)
    out, off = {}, 0
    for n in SMALL:
        sz = int(np.prod(like[n].shape))
        out[n] = flat[off:off + sz].reshape(like[n].shape)
        off += sz
    return out


def kernel(x, positions, ev_w_in, ev_w_out, od_w_in, od_conv_w, od_a_log, od_dt_bias, od_norm_w, od_w_out, ffn_w_up, ffn_conv_w, ffn_conv_b, ffn_w_down, ln1_g, ln1_b, ln2_g, ln2_b, loss_target, m_ev_w_in, m_ev_w_out, m_od_w_in, m_od_conv_w, m_od_a_log, m_od_dt_bias, m_od_norm_w, m_od_w_out, m_ffn_w_up, m_ffn_conv_w, m_ffn_conv_b, m_ffn_w_down, m_ln1_g, m_ln1_b, m_ln2_g, m_ln2_b, v_ev_w_in, v_ev_w_out, v_od_w_in, v_od_conv_w, v_od_a_log, v_od_dt_bias, v_od_norm_w, v_od_w_out, v_ffn_w_up, v_ffn_conv_w, v_ffn_conv_b, v_ffn_w_down, v_ln1_g, v_ln1_b, v_ln2_g, v_ln2_b):
    w = dict(ev_w_in=ev_w_in, ev_w_out=ev_w_out, od_w_in=od_w_in, od_conv_w=od_conv_w, od_a_log=od_a_log,
             od_dt_bias=od_dt_bias, od_norm_w=od_norm_w, od_w_out=od_w_out, ffn_w_up=ffn_w_up, ffn_conv_w=ffn_conv_w,
             ffn_conv_b=ffn_conv_b, ffn_w_down=ffn_w_down, ln1_g=ln1_g, ln1_b=ln1_b, ln2_g=ln2_g, ln2_b=ln2_b)
    mom = dict(ev_w_in=m_ev_w_in, ev_w_out=m_ev_w_out, od_w_in=m_od_w_in, od_conv_w=m_od_conv_w, od_a_log=m_od_a_log,
               od_dt_bias=m_od_dt_bias, od_norm_w=m_od_norm_w, od_w_out=m_od_w_out, ffn_w_up=m_ffn_w_up,
               ffn_conv_w=m_ffn_conv_w, ffn_conv_b=m_ffn_conv_b, ffn_w_down=m_ffn_w_down, ln1_g=m_ln1_g,
               ln1_b=m_ln1_b, ln2_g=m_ln2_g, ln2_b=m_ln2_b)
    var = dict(ev_w_in=v_ev_w_in, ev_w_out=v_ev_w_out, od_w_in=v_od_w_in, od_conv_w=v_od_conv_w, od_a_log=v_od_a_log,
               od_dt_bias=v_od_dt_bias, od_norm_w=v_od_norm_w, od_w_out=v_od_w_out, ffn_w_up=v_ffn_w_up,
               ffn_conv_w=v_ffn_conv_w, ffn_conv_b=v_ffn_conv_b, ffn_w_down=v_ffn_w_down, ln1_g=v_ln1_g,
               ln1_b=v_ln1_b, ln2_g=v_ln2_g, ln2_b=v_ln2_b)

    myid = _my_index()
    small = {n: w[n] for n in SMALL}
    groups = [(layer, part) for layer in range(DEPTH) for part in ("mixer", "ffn")]

    def group_items(gi):
        layer, part = groups[gi]
        its = _layer_items(layer)
        return its[:-3] if part == "mixer" else its[-3:]

    level1, level2 = {}, {}

    def spread_job(gi):
        return ("spread", [_to_send(kind, n, w[n], j) for (_, n, j, kind) in group_items(gi)], None)

    def relay(gi, after, name):
        (srcs, lands), = _exchange_wait([level1.pop(gi)], after, name=name + "_wait")
        more = [spread_job(gi + 1)] if gi + 1 < len(groups) else []
        started, token = _exchange_start([("relay", [], lands)] + more, None, name=name + "_start")
        level2[gi] = (started[0], srcs)
        if more:
            level1[gi + 1] = started[1]
        return token

    def get_w(layer, part, after):
        gi = groups.index((layer, part))
        started, srcs = level2.pop(gi)
        (_, lands), = _exchange_wait([started], after, name=f"gather{gi}_wait")
        lands = [lax.dynamic_update_index_in_dim(l, s, myid, 0) for l, s in zip(lands, srcs)]
        return {key: _from_gather(kind, n, l, f"L{layer}_{key}")
                for (key, n, _, kind), l in zip(group_items(gi), lands)}, None

    def mid(layer, part, after):
        gi = groups.index((layer, part)) + 1
        return relay(gi, after, f"gather{gi}_relay") if gi < len(groups) else None

    landed = {}
    pending = []

    def scatter_finish(after):
        started, gi = pending.pop()
        (srcs, lands), = _exchange_wait([started], after, name=f"scatter{gi}_wait")
        for (key, _, _, _), l, s in zip(group_items(gi), lands, srcs):
            own = lax.dynamic_index_in_dim(s, myid, 0, keepdims=False)
            landed[(groups[gi][0], key)] = lax.dynamic_update_index_in_dim(l, own, myid, 0)

    def put_g(layer, part, g):
        gi = groups.index((layer, part))
        srcs = [_by_owner(kind, n, g[key], f"L{layer}_{key}") for (key, n, _, kind) in group_items(gi)]
        (started,), token = _exchange_start([("scatter", srcs, None)], None, name=f"scatter{gi}_start")
        if pending:
            scatter_finish(token)
        pending.append((started, gi))
        return token

    (level1[0],), token = _exchange_start([spread_job(0)], None, name="gather0_spread_start")
    tables = _tables(positions[0] + token[0, 0].astype(jnp.int32))
    relay(0, tables[-1], "gather0_relay")
    loss, grad_x, gS = _local_step(x[0], tables, loss_target[0], get_w, mid, put_g, small)
    loss = lax.psum(loss, ("x", "y", "c"))

    outs_g, outs_d, outs_m, outs_v = {}, {}, {}, {}
    where = {n: [None] * w[n].shape[0] for n in SHARDED}
    for layer in range(DEPTH):
        for (key, n, j, kind) in _layer_items(layer):
            where[n][j] = (layer, key, kind)

    def update(n):
        g = jnp.stack([_sum8(landed[(layer, key)], name=f"L{layer}_{key}_sum") for (layer, key, _) in where[n]])
        if n == "od_w_in":
            g = g[:, :OD_SHARD]
        if where[n][0][2] == "colT":
            tr = lambda a: jnp.swapaxes(a, 1, 2)
            d, nm, nv = _adamw_sharded(tr(w[n]), tr(mom[n]), tr(var[n]), g, name=f"adamw_{n}")
            outs_g[n], outs_d[n], outs_m[n], outs_v[n] = tr(g), tr(d), tr(nm), tr(nv)
        else:
            outs_g[n] = g
            outs_d[n], outs_m[n], outs_v[n] = _adamw_sharded(w[n], mom[n], var[n], g, name=f"adamw_{n}")

    last = {n for (_, n, _, _) in group_items(pending[0][1])}
    for n in SHARDED:
        if n not in last:
            update(n)
    scatter_finish(outs_d[[n for n in SHARDED if n not in last][-1]])
    for n in SHARDED:
        if n in last:
            update(n)

    gall = _small_exchange(_pack_small(gS), name="small_grads_exchange")
    g, d, nm, nv = _adamw_small(_pack_small({n: w[n] for n in SMALL}), _pack_small({n: mom[n] for n in SMALL}),
                                _pack_small({n: var[n] for n in SMALL}), gall, name="adamw_small")
    for dst, packed in ((outs_g, g), (outs_d, d), (outs_m, nm), (outs_v, nv)):
        dst.update(_unpack_small(packed, {n: w[n] for n in SMALL}))

    return (loss, grad_x[None], *[outs_g[n] for n in ALL_W], *[outs_d[n] for n in ALL_W],
            *[outs_m[n] for n in ALL_W], *[outs_v[n] for n in ALL_W])
```

```python
import functools
import math

import numpy as np
import jax
import jax.numpy as jnp
from jax import lax
from jax.experimental import pallas as pl
from jax.experimental.pallas import tpu as pltpu

F32 = jnp.float32
BF16 = jnp.bfloat16
MESH = pl.DeviceIdType.MESH

D_MODEL = 1024
SEQ = 2048
DEPTH = 4
N_DEV = 8
RET_HEADS, RET_DK, RET_DV = 4, 128, 256
RET_THETA = 10000.0
DIL_HEADS, DIL_HD = 8, 64
DIL_PAIRS = ((128, 1), (512, 4), (2048, 16))
ROPE_THETA = 500000.0
ROPE_DIMS = DIL_HD // 4
GDN_HEADS, GDN_DK, GDN_DV, GDN_CHUNK, GDN_CONV = 8, 128, 128, 64, 4
D_FF = 2816
FFN_CONV = 3
ALPHA = (2.0 * DEPTH) ** 0.25
EPS = 1e-5
RET_QK_W = RET_HEADS * RET_DK
RET_V_W = RET_HEADS * RET_DV
DIL_W = DIL_HEADS * DIL_HD
EV_IN = 2 * RET_QK_W + 2 * RET_V_W + 3 * DIL_W
EV_MIX = RET_V_W + DIL_W
GDN_W = GDN_HEADS * GDN_DK
OD_IN = 4 * GDN_W + 2 * GDN_HEADS
OD_IN_PAD = 4 * GDN_W + 128
ADAM_LR, ADAM_B1, ADAM_B2, ADAM_EPS, ADAM_WD, ADAM_STEP = 0.001, 0.9, 0.999, 1e-08, 0.01, 10

LANES = 128
VMEM_LIMIT = 56 * 1024 * 1024
ATT_BLK = 256
NEG = -1e30


def _cp(**kw):
    return pltpu.CompilerParams(vmem_limit_bytes=VMEM_LIMIT, **kw)


def _tile(n, cap):
    if n <= cap:
        return n
    best = None
    for t in range(LANES, cap + 1, LANES):
        if n % t == 0:
            best = t
    assert best is not None, (n, cap)
    return best


def _mm(a, b, *, ta=False, tb=False, name, out_dtype=F32, dep=None, tm=None, tn=None):
    m = a.shape[1] if ta else a.shape[0]
    k = a.shape[0] if ta else a.shape[1]
    n = b.shape[0] if tb else b.shape[1]
    assert (b.shape[1] if tb else b.shape[0]) == k
    assert a.dtype == BF16 and b.dtype == BF16
    if tn is None:
        tn = n if n <= 1024 else _tile(n, 512)
    if tm is None:
        tm = m if (tn < n and k <= 1024 and m <= 2048) else _tile(m, 512)
    dims = (((0 if ta else 1,), (1 if tb else 0,)), ((), ()))

    def body(a_ref, b_ref, *rest):
        o_ref = rest[-1]
        o_ref[...] = lax.dot_general(a_ref[...], b_ref[...], dims,
                                     preferred_element_type=F32).astype(o_ref.dtype)

    a_spec = pl.BlockSpec((k, tm), lambda i, j: (0, i)) if ta else pl.BlockSpec((tm, k), lambda i, j: (i, 0))
    b_spec = pl.BlockSpec((tn, k), lambda i, j: (j, 0)) if tb else pl.BlockSpec((k, tn), lambda i, j: (0, j))
    extra = [] if dep is None else [dep]
    return pl.pallas_call(
        body, grid=(m // tm, n // tn), in_specs=[a_spec, b_spec] + [pl.BlockSpec(memory_space=pl.ANY)] * len(extra),
        out_specs=pl.BlockSpec((tm, tn), lambda i, j: (i, j)),
        out_shape=jax.ShapeDtypeStruct((m, n), out_dtype), name=name, compiler_params=_cp())(a, b, *extra)


LN_ROWS = 256


def _ln_bwd(z, g, dya, dyb, *, name):
    t, d = z.shape
    two = dyb is not None

    def body(*refs):
        if two:
            z_ref, g_ref, dya_ref, dyb_ref, dz_ref, dzb_ref, dg_ref, db_ref = refs
            dy = dya_ref[...] + ALPHA * dyb_ref[...]
        else:
            z_ref, g_ref, dya_ref, dz_ref, dzb_ref, dg_ref, db_ref = refs
            dy = dya_ref[...]
        zz = z_ref[...]
        mu = jnp.mean(zz, -1, keepdims=True)
        zc = zz - mu
        var = jnp.mean(zc * zc, -1, keepdims=True)
        r = lax.rsqrt(var + EPS)
        xh = zc * r
        dxh = dy * g_ref[...]
        dz = r * (dxh - jnp.mean(dxh, -1, keepdims=True) - xh * jnp.mean(dxh * xh, -1, keepdims=True))
        dz_ref[...] = dz
        dzb_ref[...] = dz.astype(BF16)

        @pl.when(pl.program_id(0) == 0)
        def _():
            dg_ref[...] = jnp.zeros_like(dg_ref)
            db_ref[...] = jnp.zeros_like(db_ref)

        dg_ref[...] += jnp.sum(dy * xh, 0, keepdims=True)
        db_ref[...] += jnp.sum(dy, 0, keepdims=True)

    row = pl.BlockSpec((LN_ROWS, d), lambda i: (i, 0))
    vec = pl.BlockSpec((1, d), lambda i: (0, 0))
    ins = [z, g, dya] + ([dyb] if two else [])
    return pl.pallas_call(
        body, grid=(t // LN_ROWS,), in_specs=[row, vec, row] + ([row] if two else []),
        out_specs=[row, row, vec, vec],
        out_shape=[jax.ShapeDtypeStruct((t, d), F32), jax.ShapeDtypeStruct((t, d), BF16),
                   jax.ShapeDtypeStruct((1, d), F32), jax.ShapeDtypeStruct((1, d), F32)],
        name=name, compiler_params=_cp())(*ins)


def _ln_rows(k):
    return 256 if k > 4096 else 512


def _mm_ln_fwd(a, w, x, g, b, *, name, dep=None):
    t, k = a.shape
    d = w.shape[1]
    tm = _ln_rows(k)

    def body(a_ref, w_ref, x_ref, g_ref, b_ref, *rest):
        z_ref, y_ref, yb_ref = rest[-3:]
        z = ALPHA * x_ref[...] + _nn(a_ref[...], w_ref[...])
        mu = jnp.mean(z, -1, keepdims=True)
        zc = z - mu
        var = jnp.mean(zc * zc, -1, keepdims=True)
        y = zc * lax.rsqrt(var + EPS) * g_ref[...] + b_ref[...]
        z_ref[...] = z
        y_ref[...] = y
        yb_ref[...] = y.astype(BF16)

    row = pl.BlockSpec((tm, d), lambda i: (i, 0))
    vec = pl.BlockSpec((1, d), lambda i: (0, 0))
    extra = [] if dep is None else [dep]
    return pl.pallas_call(
        body, grid=(t // tm,),
        in_specs=[pl.BlockSpec((tm, k), lambda i: (i, 0)), pl.BlockSpec((k, d), lambda i: (0, 0)), row, vec, vec]
        + [pl.BlockSpec(memory_space=pl.ANY)] * len(extra),
        out_specs=[row, row, row],
        out_shape=[jax.ShapeDtypeStruct((t, d), F32), jax.ShapeDtypeStruct((t, d), F32), jax.ShapeDtypeStruct((t, d), BF16)],
        name=name, compiler_params=_cp())(a, w, x, g, b, *extra)


def _part_offsets(parts):
    offs, o = [], 0
    for p in parts:
        assert o % p.shape[1] == 0
        offs.append(o)
        o += p.shape[1]
    return offs, o


def _mm_ln_bwd(parts, w, z, g, dyb, *, name, dep=None):
    t = parts[0].shape[0]
    offs, k = _part_offsets(parts)
    d = w.shape[1]
    tm = _ln_rows(k)
    npart = len(parts)

    def body(*refs):
        a_refs, w_refs = refs[:npart], refs[npart:2 * npart]
        z_ref, g_ref, dyb_ref = refs[2 * npart:2 * npart + 3]
        dz_ref, dzb_ref, dg_ref, db_ref = refs[-4:]
        dy = ALPHA * dyb_ref[...]
        for a_ref, w_ref in zip(a_refs, w_refs):
            dy = dy + _nn(a_ref[...], w_ref[...])
        zz = z_ref[...]
        mu = jnp.mean(zz, -1, keepdims=True)
        zc = zz - mu
        var = jnp.mean(zc * zc, -1, keepdims=True)
        r = lax.rsqrt(var + EPS)
        xh = zc * r
        dxh = dy * g_ref[...]
        dz = r * (dxh - jnp.mean(dxh, -1, keepdims=True) - xh * jnp.mean(dxh * xh, -1, keepdims=True))
        dz_ref[...] = dz
        dzb_ref[...] = dz.astype(BF16)

        @pl.when(pl.program_id(0) == 0)
        def _():
            dg_ref[...] = jnp.zeros_like(dg_ref)
            db_ref[...] = jnp.zeros_like(db_ref)

        dg_ref[...] += jnp.sum(dy * xh, 0, keepdims=True)
        db_ref[...] += jnp.sum(dy, 0, keepdims=True)

    row = pl.BlockSpec((tm, d), lambda i: (i, 0))
    vec = pl.BlockSpec((1, d), lambda i: (0, 0))
    extra = [] if dep is None else [dep]
    a_specs = [pl.BlockSpec((tm, p.shape[1]), lambda i: (i, 0)) for p in parts]
    w_specs = [pl.BlockSpec((p.shape[1], d), functools.partial(lambda i, blk: (blk, 0), blk=o // p.shape[1]))
               for p, o in zip(parts, offs)]
    return pl.pallas_call(
        body, grid=(t // tm,),
        in_specs=a_specs + w_specs + [row, vec, row] + [pl.BlockSpec(memory_space=pl.ANY)] * len(extra),
        out_specs=[row, row, vec, vec],
        out_shape=[jax.ShapeDtypeStruct((t, d), F32), jax.ShapeDtypeStruct((t, d), BF16),
                   jax.ShapeDtypeStruct((1, d), F32), jax.ShapeDtypeStruct((1, d), F32)],
        name=name, compiler_params=_cp())(*parts, *([w] * npart), z, g, dyb, *extra)


def _mm_tn_parts(parts, b, *, name):
    t, n = b.shape
    offs, m = _part_offsets(parts)
    tm = min(_tile(p.shape[1], 1408 if p.shape[1] > 2048 else 512) for p in parts)
    assert all(p.shape[1] % tm == 0 for p in parts)
    first = [o // tm for o in offs]
    count = [p.shape[1] // tm for p in parts]
    npart = len(parts)

    def body(*refs):
        a_refs, b_ref, o_ref = refs[:npart], refs[npart], refs[npart + 1]
        i = pl.program_id(0)
        for a_ref, f, c in zip(a_refs, first, count):
            @pl.when((i >= f) & (i < f + c))
            def _(a_ref=a_ref):
                o_ref[...] = _tn(a_ref[...], b_ref[...]).astype(BF16)

    a_specs = [pl.BlockSpec((t, tm), functools.partial(lambda i, f, c: (0, jnp.clip(i - f, 0, c - 1)), f=f, c=c))
               for f, c in zip(first, count)]
    return pl.pallas_call(
        body, grid=(m // tm,), in_specs=a_specs + [pl.BlockSpec((t, n), lambda i: (0, 0))],
        out_specs=pl.BlockSpec((tm, n), lambda i: (i, 0)), out_shape=jax.ShapeDtypeStruct((m, n), BF16),
        name=name, compiler_params=_cp())(*parts, b)


def _axpy(a, b, *, name):
    t, d = a.shape

    def body(a_ref, b_ref, o_ref):
        o_ref[...] = a_ref[...] + ALPHA * b_ref[...]

    row = pl.BlockSpec((LN_ROWS, d), lambda i: (i, 0))
    return pl.pallas_call(body, grid=(t // LN_ROWS,), in_specs=[row, row], out_specs=row,
                          out_shape=jax.ShapeDtypeStruct((t, d), F32), name=name, compiler_params=_cp())(a, b)


def _loss_head(y, target, *, name):
    t, d = y.shape

    def body(y_ref, t_ref, dy_ref, l_ref):
        e = y_ref[...] - t_ref[...]
        dy_ref[...] = e * (1.0 / d)

        @pl.when(pl.program_id(0) == 0)
        def _():
            l_ref[...] = jnp.zeros_like(l_ref)

        l_ref[...] += jnp.zeros_like(l_ref) + 0.5 * jnp.sum(jnp.mean(e * e, -1, keepdims=True), 0, keepdims=True)

    row = pl.BlockSpec((LN_ROWS, d), lambda i: (i, 0))
    return pl.pallas_call(
        body, grid=(t // LN_ROWS,), in_specs=[row, row],
        out_specs=[row, pl.BlockSpec((1, LANES), lambda i: (0, 0))],
        out_shape=[jax.ShapeDtypeStruct((t, d), F32), jax.ShapeDtypeStruct((1, LANES), F32)],
        name=name, compiler_params=_cp())(y, target)


def _sig(x):
    return 1.0 / (1.0 + jnp.exp(-x))


def _silu(x):
    return x * _sig(x)


def _dsilu(x):
    s = _sig(x)
    return s * (1.0 + x * (1.0 - s))


def _shift_down(u, k, row):
    if k == 0:
        return u
    return jnp.where(row >= k, pltpu.roll(u, k, 0), 0.0)


def _shift_up(u, k, row):
    if k == 0:
        return u
    t = u.shape[0]
    return jnp.where(row < t - k, pltpu.roll(u, t - k, 0), 0.0)


def _dwconv(u, w_ref, row):
    kk = w_ref.shape[0]
    acc = None
    for j in range(kk):
        term = w_ref[j:j + 1, :] * _shift_down(u, kk - 1 - j, row)
        acc = term if acc is None else acc + term
    return acc


def _dwconv_bwd(u, w_ref, dc, row, dw_ref):
    kk = w_ref.shape[0]
    du = None
    for j in range(kk):
        term = w_ref[j:j + 1, :] * _shift_up(dc, kk - 1 - j, row)
        du = term if du is None else du + term
        dw_ref[j:j + 1, :] = jnp.sum(dc * _shift_down(u, kk - 1 - j, row), 0, keepdims=True)
    return du


CONV_ROWS = 256


def _rows(b):
    return pl.ds(pl.multiple_of(b * CONV_ROWS, CONV_ROWS), CONV_ROWS)


def _shifted_down(ref, b, k, row):
    cur = ref[_rows(b), :]
    if k == 0:
        return cur
    prev = jnp.where(b > 0, ref[_rows(jnp.maximum(b - 1, 0)), :], 0.0)
    return jnp.where(row >= k, pltpu.roll(cur, k, 0), pltpu.roll(prev, k, 0))


def _shifted_up(ref, b, k, row, nblk):
    cur = ref[_rows(b), :]
    if k == 0:
        return cur
    nxt = jnp.where(b < nblk - 1, ref[_rows(jnp.minimum(b + 1, nblk - 1)), :], 0.0)
    return jnp.where(row < CONV_ROWS - k, pltpu.roll(cur, CONV_ROWS - k, 0), pltpu.roll(nxt, CONV_ROWS - k, 0))


def _dwconv_blk(u_ref, w_ref, b, row):
    kk = w_ref.shape[0]
    views = [_shifted_down(u_ref, b, kk - 1 - j, row) for j in range(kk)]
    acc = None
    for j in range(kk):
        term = w_ref[j:j + 1, :] * views[j]
        acc = term if acc is None else acc + term
    return acc, views


def _dwconv_du_blk(dc_ref, w_ref, b, row, nblk):
    kk = w_ref.shape[0]
    du = None
    for j in range(kk):
        term = w_ref[j:j + 1, :] * _shifted_up(dc_ref, b, kk - 1 - j, row, nblk)
        du = term if du is None else du + term
    return du


FFN_TC = 256


def _ffn_up_mid(x, up_t, cw, cb, *, name, dep=None):
    t, d = x.shape
    nb = D_FF // FFN_TC

    def body(x_ref, ugt_ref, uvt_ref, wg_ref, wv_ref, bg_ref, bv_ref, *rest):
        ug_ref, uv_ref, a_ref = rest[-3:]
        xx = x_ref[...]
        row = lax.broadcasted_iota(jnp.int32, (t, FFN_TC), 0)
        ug = _nt(xx, ugt_ref[...])
        ug_ref[...] = ug
        uv = _nt(xx, uvt_ref[...])
        uv_ref[...] = uv
        cg = _dwconv(ug, wg_ref, row) + bg_ref[...]
        cv = _dwconv(uv, wv_ref, row) + bv_ref[...]
        a_ref[...] = (_silu(cg) * cv).astype(BF16)

    col = pl.BlockSpec((t, FFN_TC), lambda j: (0, j))
    wt = lambda off: pl.BlockSpec((FFN_TC, d), lambda j: (j + off, 0))
    wsp = lambda off: pl.BlockSpec((FFN_CONV, FFN_TC), lambda j: (0, j + off))
    bsp = lambda off: pl.BlockSpec((1, FFN_TC), lambda j: (0, j + off))
    extra = [] if dep is None else [dep]
    return pl.pallas_call(
        body, grid=(nb,),
        in_specs=[pl.BlockSpec((t, d), lambda j: (0, 0)), wt(0), wt(nb), wsp(0), wsp(nb), bsp(0), bsp(nb)]
        + [pl.BlockSpec(memory_space=pl.ANY)] * len(extra),
        out_specs=[col, col, col],
        out_shape=[jax.ShapeDtypeStruct((t, D_FF), F32), jax.ShapeDtypeStruct((t, D_FF), F32),
                   jax.ShapeDtypeStruct((t, D_FF), BF16)],
        name=name, compiler_params=_cp())(x, up_t, up_t, cw, cw, cb, cb, *extra)


def _ffn_mid_bwd(ug, uv, cw, cb, dz, down, *, name):
    t, d = dz.shape
    nb = D_FF // FFN_TC

    nblk = t // CONV_ROWS

    def body(ug_ref, uv_ref, wg_ref, wv_ref, bg_ref, bv_ref, dz_ref, dn_ref,
             dug_ref, duv_ref, dwg_ref, dwv_ref, dbg_ref, dbv_ref, da_ref, dcg_s, dcv_s):
        da_ref[...] = _nt(dz_ref[...], dn_ref[...])
        row = lax.broadcasted_iota(jnp.int32, (CONV_ROWS, FFN_TC), 0)
        zero = jnp.zeros((1, FFN_TC), F32)

        def first(b, acc):
            cg, ugs = _dwconv_blk(ug_ref, wg_ref, b, row)
            cv, uvs = _dwconv_blk(uv_ref, wv_ref, b, row)
            cg = cg + bg_ref[...]
            cv = cv + bv_ref[...]
            da_ = da_ref[_rows(b), :]
            dcv = da_ * _silu(cg)
            dcg = da_ * cv * _dsilu(cg)
            dcg_s[_rows(b), :] = dcg
            dcv_s[_rows(b), :] = dcv
            red = [jnp.sum(dcg * s, 0, keepdims=True) for s in ugs] + [jnp.sum(dcg, 0, keepdims=True)]
            red += [jnp.sum(dcv * s, 0, keepdims=True) for s in uvs] + [jnp.sum(dcv, 0, keepdims=True)]
            return tuple(a + r for a, r in zip(acc, red))

        acc = lax.fori_loop(0, nblk, first, (zero,) * (2 * FFN_CONV + 2))
        for j in range(FFN_CONV):
            dwg_ref[j:j + 1, :] = acc[j]
            dwv_ref[j:j + 1, :] = acc[FFN_CONV + 1 + j]
        dbg_ref[...] = acc[FFN_CONV]
        dbv_ref[...] = acc[2 * FFN_CONV + 1]

        def second(b, carry):
            dug_ref[_rows(b), :] = _dwconv_du_blk(dcg_s, wg_ref, b, row, nblk).astype(BF16)
            duv_ref[_rows(b), :] = _dwconv_du_blk(dcv_s, wv_ref, b, row, nblk).astype(BF16)
            return carry

        lax.fori_loop(0, nblk, second, 0)

    col = pl.BlockSpec((t, FFN_TC), lambda j: (0, j))
    wsp = lambda off: pl.BlockSpec((FFN_CONV, FFN_TC), lambda j: (0, j + off))
    bsp = lambda off: pl.BlockSpec((1, FFN_TC), lambda j: (0, j + off))
    outs = pl.pallas_call(
        body, grid=(nb,),
        in_specs=[col, col, wsp(0), wsp(nb), bsp(0), bsp(nb), pl.BlockSpec((t, d), lambda j: (0, 0)),
                  pl.BlockSpec((FFN_TC, d), lambda j: (j, 0))],
        out_specs=[col, col, wsp(0), wsp(0), bsp(0), bsp(0)],
        out_shape=[jax.ShapeDtypeStruct((t, D_FF), BF16), jax.ShapeDtypeStruct((t, D_FF), BF16),
                   jax.ShapeDtypeStruct((FFN_CONV, D_FF), F32), jax.ShapeDtypeStruct((FFN_CONV, D_FF), F32),
                   jax.ShapeDtypeStruct((1, D_FF), F32), jax.ShapeDtypeStruct((1, D_FF), F32)],
        scratch_shapes=[pltpu.VMEM((t, FFN_TC), F32), pltpu.VMEM((t, FFN_TC), F32), pltpu.VMEM((t, FFN_TC), F32)],
        name=name, compiler_params=_cp())(ug, uv, cw, cw, cb, cb, dz, down)
    dug, duv, dwg, dwv, dbg, dbv = outs
    return [dug, duv], jnp.concatenate([dwg, dwv], 1), jnp.concatenate([dbg, dbv], 1)


def _rot_a(x, c2, s2):
    return x * c2 + pltpu.roll(x, RET_DK // 2, 1) * s2


def _rot_a_t(dy, c2, s2):
    return dy * c2 + pltpu.roll(dy * s2, RET_DK // 2, 1)


def _decay_tile(lg, blk_diff):
    r = lax.broadcasted_iota(jnp.int32, (ATT_BLK, ATT_BLK), 0)
    c = lax.broadcasted_iota(jnp.int32, (ATT_BLK, ATT_BLK), 1)
    rel = r - c + blk_diff * ATT_BLK
    return jnp.where(rel >= 0, jnp.exp(jnp.maximum(rel, 0).astype(F32) * lg), 0.0)


def _nt(a, b):
    return lax.dot_general(a, b, (((1,), (1,)), ((), ())), preferred_element_type=F32)


def _nn(a, b):
    return lax.dot_general(a, b, (((1,), (0,)), ((), ())), preferred_element_type=F32)


def _tn(a, b):
    return lax.dot_general(a, b, (((0,), (0,)), ((), ())), preferred_element_type=F32)


def _ret_specs(t):
    q = pl.BlockSpec((t, RET_DK), lambda h: (0, h))
    k = pl.BlockSpec((t, RET_DK), lambda h: (0, RET_HEADS + h))
    v = pl.BlockSpec((t, RET_DV), lambda h: (0, RET_HEADS + h))
    g = pl.BlockSpec((t, RET_DV), lambda h: (0, 2 * RET_HEADS + h))
    tab = pl.BlockSpec((t, RET_DK), lambda h: (0, 0))
    lg = pl.BlockSpec((1, 1, LANES), lambda h: (h, 0, 0))
    return q, k, v, g, tab, lg


def _ret_fwd(h, c2, s2, lgt, *, name):
    t = h.shape[0]
    nblk = t // ATT_BLK
    scale = RET_DK ** -0.5

    def body(q_ref, k_ref, v_ref, g_ref, c_ref, s_ref, lg_ref, o_ref, ya_ref, qs, ks, vs):
        c2_, s2_ = c_ref[...], s_ref[...]
        qs[...] = _rot_a(q_ref[...], c2_, s2_).astype(BF16)
        ks[...] = (_rot_a(k_ref[...], c2_, s2_) * scale).astype(BF16)
        vs[...] = v_ref[...].astype(BF16)
        lg = lg_ref[0, :, 0:1]
        for i in range(nblk):
            qi = qs[pl.ds(i * ATT_BLK, ATT_BLK), :]
            acc = jnp.zeros((ATT_BLK, RET_DV), F32)
            for j in range(i + 1):
                sl = pl.ds(j * ATT_BLK, ATT_BLK)
                s = _nt(qi, ks[sl, :]) * _decay_tile(lg, i - j)
                acc = acc + _nn(s.astype(BF16), vs[sl, :])
            rows = pl.ds(i * ATT_BLK, ATT_BLK)
            o_ref[rows, :] = acc
            r = lax.rsqrt(jnp.mean(acc * acc, -1, keepdims=True) + EPS)
            ya_ref[rows, :] = (acc * r * _silu(g_ref[rows, :])).astype(BF16)

    q, k, v, g, tab, lg = _ret_specs(t)
    out = pl.BlockSpec((t, RET_DV), lambda hh: (0, hh))
    return pl.pallas_call(
        body, grid=(RET_HEADS,), in_specs=[q, k, v, g, tab, tab, lg], out_specs=[out, out],
        out_shape=[jax.ShapeDtypeStruct((t, RET_V_W), F32), jax.ShapeDtypeStruct((t, RET_V_W), BF16)],
        scratch_shapes=[pltpu.VMEM((t, RET_DK), BF16), pltpu.VMEM((t, RET_DK), BF16), pltpu.VMEM((t, RET_DV), BF16)],
        name=name, compiler_params=_cp())(h, h, h, h, c2, s2, lgt)


def _ret_bwd(h, c2, s2, lgt, o, dy, *, name):
    t = h.shape[0]
    nblk = t // ATT_BLK
    scale = RET_DK ** -0.5

    def body(q_ref, k_ref, v_ref, g_ref, c_ref, s_ref, lg_ref, o_ref, dy_ref,
             dq_ref, dk_ref, dv_ref, dg_ref, qs, ks, vs, dos, dka, dva):
        c2_, s2_ = c_ref[...], s_ref[...]
        qs[...] = _rot_a(q_ref[...], c2_, s2_).astype(BF16)
        ks[...] = (_rot_a(k_ref[...], c2_, s2_) * scale).astype(BF16)
        vs[...] = v_ref[...].astype(BF16)
        lg = lg_ref[0, :, 0:1]
        oo = o_ref[...]
        gg = g_ref[...]
        dya = dy_ref[...]
        r = lax.rsqrt(jnp.mean(oo * oo, -1, keepdims=True) + EPS)
        rn = oo * r
        dg_ref[...] = (dya * rn * _dsilu(gg)).astype(BF16)
        drn = dya * _silu(gg)
        dos[...] = (r * (drn - rn * jnp.mean(drn * rn, -1, keepdims=True))).astype(BF16)
        dka[...] = jnp.zeros_like(dka)
        dva[...] = jnp.zeros_like(dva)
        for i in range(nblk):
            rows = pl.ds(i * ATT_BLK, ATT_BLK)
            qi = qs[rows, :]
            doi = dos[rows, :]
            dqa = jnp.zeros((ATT_BLK, RET_DK), F32)
            for j in range(i + 1):
                sl = pl.ds(j * ATT_BLK, ATT_BLK)
                dt_ = _decay_tile(lg, i - j)
                kj = ks[sl, :]
                s = (_nt(qi, kj) * dt_).astype(BF16)
                ds = (_nt(doi, vs[sl, :]) * dt_).astype(BF16)
                dqa = dqa + _nn(ds, kj)
                dka[sl, :] += _tn(ds, qi)
                dva[sl, :] += _tn(s, doi)
            dq_ref[rows, :] = _rot_a_t(dqa, c_ref[rows, :], s_ref[rows, :]).astype(BF16)
        dk_ref[...] = (_rot_a_t(dka[...], c2_, s2_) * scale).astype(BF16)
        dv_ref[...] = dva[...].astype(BF16)

    q, k, v, g, tab, lg = _ret_specs(t)
    blk_v = pl.BlockSpec((t, RET_DV), lambda hh: (0, hh))
    blk_k = pl.BlockSpec((t, RET_DK), lambda hh: (0, hh))
    return pl.pallas_call(
        body, grid=(RET_HEADS,), in_specs=[q, k, v, g, tab, tab, lg, blk_v, blk_v],
        out_specs=[blk_k, blk_k, blk_v, blk_v],
        out_shape=[jax.ShapeDtypeStruct((t, RET_QK_W), BF16), jax.ShapeDtypeStruct((t, RET_QK_W), BF16),
                   jax.ShapeDtypeStruct((t, RET_V_W), BF16), jax.ShapeDtypeStruct((t, RET_V_W), BF16)],
        scratch_shapes=[pltpu.VMEM((t, RET_DK), BF16), pltpu.VMEM((t, RET_DK), BF16), pltpu.VMEM((t, RET_DV), BF16),
                        pltpu.VMEM((t, RET_DV), BF16), pltpu.VMEM((t, RET_DK), F32), pltpu.VMEM((t, RET_DV), F32)],
        name=name, compiler_params=_cp())(h, h, h, h, c2, s2, lgt, o, dy)


def _rot_b(x, cb, shi, slo):
    return x * cb + pltpu.roll(x, ROPE_DIMS // 2, 1) * shi + pltpu.roll(x, LANES - ROPE_DIMS // 2, 1) * slo


def _rot_b_t(dy, cb, shi, slo):
    return dy * cb + pltpu.roll(dy * shi, LANES - ROPE_DIMS // 2, 1) + pltpu.roll(dy * slo, ROPE_DIMS // 2, 1)


def _dil_specs(t):
    base = (2 * RET_QK_W + 2 * RET_V_W) // LANES
    npair = DIL_W // LANES
    q = pl.BlockSpec((t, LANES), lambda p: (0, base + p))
    k = pl.BlockSpec((t, LANES), lambda p: (0, base + npair + p))
    v = pl.BlockSpec((t, LANES), lambda p: (0, base + 2 * npair + p))
    tab = pl.BlockSpec((t, LANES), lambda p: (0, 0))
    strip = pl.BlockSpec((ATT_BLK, t), lambda p: (0, 0))
    pair = pl.BlockSpec((t, LANES), lambda p: (0, p))
    return q, k, v, tab, strip, pair


def _dil_fwd(h, cb, shi, slo, strip, *, name):
    t = h.shape[0]
    nblk = t // ATT_BLK
    scale = DIL_HD ** -0.5

    def body(q_ref, k_ref, v_ref, cb_ref, shi_ref, slo_ref, st_ref, o_ref, yb_ref, lse_ref, qs, ks, vs):
        cb_, shi_, slo_ = cb_ref[...], shi_ref[...], slo_ref[...]
        lane = lax.broadcasted_iota(jnp.int32, (t, LANES), 1)
        qr = _rot_b(q_ref[...], cb_, shi_, slo_) * scale
        qs[0] = jnp.where(lane < DIL_HD, qr, 0.0).astype(BF16)
        qs[1] = jnp.where(lane >= DIL_HD, qr, 0.0).astype(BF16)
        ks[...] = _rot_b(k_ref[...], cb_, shi_, slo_).astype(BF16)
        vs[...] = v_ref[...].astype(BF16)
        lane_b = lax.broadcasted_iota(jnp.int32, (ATT_BLK, LANES), 1)
        for i in range(nblk):
            w = (i + 1) * ATT_BLK
            rows = pl.ds(i * ATT_BLK, ATT_BLK)
            logc = st_ref[:, t - w:t]
            outs, lses = [], []
            for hd in range(2):
                s = _nt(qs[hd, rows, :], ks[0:w, :]) + logc
                m = jnp.max(s, -1, keepdims=True)
                p = jnp.exp(s - m)
                l = jnp.sum(p, -1, keepdims=True)
                outs.append(_nn(p.astype(BF16), vs[0:w, :]) / l)
                lses.append(m + jnp.log(l))
            o = jnp.where(lane_b < DIL_HD, outs[0], outs[1])
            o_ref[rows, :] = o
            yb_ref[rows, :] = o.astype(BF16)
            lse_ref[rows, :] = jnp.where(lane_b < DIL_HD, lses[0], lses[1])

    q, k, v, tab, strip_spec, pair = _dil_specs(t)
    return pl.pallas_call(
        body, grid=(DIL_W // LANES,), in_specs=[q, k, v, tab, tab, tab, strip_spec], out_specs=[pair, pair, pair],
        out_shape=[jax.ShapeDtypeStruct((t, DIL_W), F32), jax.ShapeDtypeStruct((t, DIL_W), BF16),
                   jax.ShapeDtypeStruct((t, DIL_W), F32)],
        scratch_shapes=[pltpu.VMEM((2, t, LANES), BF16), pltpu.VMEM((t, LANES), BF16), pltpu.VMEM((t, LANES), BF16)],
        name=name, compiler_params=_cp())(h, h, h, cb, shi, slo, strip)


def _dil_bwd(h, cb, shi, slo, strip, o, lse, dy, *, name):
    t = h.shape[0]
    nblk = t // ATT_BLK
    scale = DIL_HD ** -0.5

    def body(q_ref, k_ref, v_ref, cb_ref, shi_ref, slo_ref, st_ref, o_ref, lse_ref, dy_ref,
             dq_ref, dk_ref, dv_ref, qs, ks, vs, dos, dls, dka, dva):
        cb_, shi_, slo_ = cb_ref[...], shi_ref[...], slo_ref[...]
        lane = lax.broadcasted_iota(jnp.int32, (t, LANES), 1)
        qr = _rot_b(q_ref[...], cb_, shi_, slo_) * scale
        qs[0] = jnp.where(lane < DIL_HD, qr, 0.0).astype(BF16)
        qs[1] = jnp.where(lane >= DIL_HD, qr, 0.0).astype(BF16)
        ks[...] = _rot_b(k_ref[...], cb_, shi_, slo_).astype(BF16)
        vs[...] = v_ref[...].astype(BF16)
        do = dy_ref[...]
        prod = do * o_ref[...]
        d0 = jnp.sum(jnp.where(lane < DIL_HD, prod, 0.0), -1, keepdims=True)
        d1 = jnp.sum(jnp.where(lane >= DIL_HD, prod, 0.0), -1, keepdims=True)
        dls[...] = jnp.where(lane < DIL_HD, d0, d1)
        dos[0] = jnp.where(lane < DIL_HD, do, 0.0).astype(BF16)
        dos[1] = jnp.where(lane >= DIL_HD, do, 0.0).astype(BF16)
        dka[...] = jnp.zeros_like(dka)
        dva[...] = jnp.zeros_like(dva)
        lane_b = lax.broadcasted_iota(jnp.int32, (ATT_BLK, LANES), 1)
        for i in range(nblk):
            w = (i + 1) * ATT_BLK
            rows = pl.ds(i * ATT_BLK, ATT_BLK)
            logc = st_ref[:, t - w:t]
            dqs = []
            for hd in range(2):
                col = hd * DIL_HD
                qh = qs[hd, rows, :]
                doh = dos[hd, rows, :]
                lse_h = lse_ref[rows, col:col + 1]
                dl_h = dls[rows, col:col + 1]
                p = jnp.exp(_nt(qh, ks[0:w, :]) + logc - lse_h)
                dp = _nt(doh, vs[0:w, :])
                ds = (p * (dp - dl_h)).astype(BF16)
                dqs.append(_nn(ds, ks[0:w, :]))
                dka[0:w, :] += _tn(ds, qh)
                dva[0:w, :] += _tn(p.astype(BF16), doh)
            dq = jnp.where(lane_b < DIL_HD, dqs[0], dqs[1]) * scale
            dq_ref[rows, :] = _rot_b_t(dq, cb_ref[rows, :], shi_ref[rows, :], slo_ref[rows, :]).astype(BF16)
        dk_ref[...] = _rot_b_t(dka[...], cb_, shi_, slo_).astype(BF16)
        dv_ref[...] = dva[...].astype(BF16)

    q, k, v, tab, strip_spec, pair = _dil_specs(t)
    dy_spec = pl.BlockSpec((t, LANES), lambda p: (0, RET_V_W // LANES + p))
    return pl.pallas_call(
        body, grid=(DIL_W // LANES,), in_specs=[q, k, v, tab, tab, tab, strip_spec, pair, pair, dy_spec],
        out_specs=[pair, pair, pair],
        out_shape=[jax.ShapeDtypeStruct((t, DIL_W), BF16)] * 3,
        scratch_shapes=[pltpu.VMEM((2, t, LANES), BF16), pltpu.VMEM((t, LANES), BF16), pltpu.VMEM((t, LANES), BF16),
                        pltpu.VMEM((2, t, LANES), BF16), pltpu.VMEM((t, LANES), F32),
                        pltpu.VMEM((t, LANES), F32), pltpu.VMEM((t, LANES), F32)],
        name=name, compiler_params=_cp())(h, h, h, cb, shi, slo, strip, o, lse, dy)


def _gdn_prep_fwd(h, cw, *, name):
    t = h.shape[0]
    qscale = GDN_DK ** -0.5

    def body(hq_ref, hk_ref, hv_ref, wq_ref, wk_ref, wv_ref, q_ref, k_ref, v_ref):
        row = lax.broadcasted_iota(jnp.int32, (t, GDN_DK), 0)
        sq = _silu(_dwconv(hq_ref[...], wq_ref, row))
        sk = _silu(_dwconv(hk_ref[...], wk_ref, row))
        q_ref[0] = sq * lax.rsqrt(jnp.sum(sq * sq, -1, keepdims=True) + 1e-6) * qscale
        k_ref[0] = sk * lax.rsqrt(jnp.sum(sk * sk, -1, keepdims=True) + 1e-6)
        v_ref[0] = _silu(_dwconv(hv_ref[...], wv_ref, row))

    hs = lambda off: pl.BlockSpec((t, GDN_DK), lambda i: (0, i + off))
    ws = lambda off: pl.BlockSpec((GDN_CONV, GDN_DK), lambda i: (0, i + off))
    out = pl.BlockSpec((1, t, GDN_DK), lambda i: (i, 0, 0))
    return pl.pallas_call(
        body, grid=(GDN_HEADS,), in_specs=[hs(0), hs(8), hs(16), ws(0), ws(8), ws(16)], out_specs=[out, out, out],
        out_shape=[jax.ShapeDtypeStruct((GDN_HEADS, t, GDN_DK), F32)] * 3,
        name=name, compiler_params=_cp())(h, h, h, cw, cw, cw)


def _gdn_prep_bwd(h, cw, dq, dk, dv, *, name):
    t = h.shape[0]
    qscale = GDN_DK ** -0.5

    def body(hq_ref, hk_ref, hv_ref, wq_ref, wk_ref, wv_ref, dq_ref, dk_ref, dv_ref,
             dhq_ref, dhk_ref, dhv_ref, dwq_ref, dwk_ref, dwv_ref):
        row = lax.broadcasted_iota(jnp.int32, (t, GDN_DK), 0)

        def one(h_ref, w_ref, d_ref, dh_ref, dw_ref, norm, sc):
            u = h_ref[...]
            c = _dwconv(u, w_ref, row)
            d = d_ref[0]
            if norm:
                s = _silu(c)
                r = lax.rsqrt(jnp.sum(s * s, -1, keepdims=True) + 1e-6)
                n = s * r
                d = d * sc
                d = r * (d - n * jnp.sum(d * n, -1, keepdims=True))
            dc = d * _dsilu(c)
            dh_ref[...] = _dwconv_bwd(u, w_ref, dc, row, dw_ref).astype(BF16)

        one(hq_ref, wq_ref, dq_ref, dhq_ref, dwq_ref, True, qscale)
        one(hk_ref, wk_ref, dk_ref, dhk_ref, dwk_ref, True, 1.0)
        one(hv_ref, wv_ref, dv_ref, dhv_ref, dwv_ref, False, 1.0)

    hs = lambda off: pl.BlockSpec((t, GDN_DK), lambda i: (0, i + off))
    ws = lambda off: pl.BlockSpec((GDN_CONV, GDN_DK), lambda i: (0, i + off))
    hd = pl.BlockSpec((1, t, GDN_DK), lambda i: (i, 0, 0))
    return pl.pallas_call(
        body, grid=(GDN_HEADS,), in_specs=[hs(0), hs(8), hs(16), ws(0), ws(8), ws(16), hd, hd, hd],
        out_specs=[hs(0), hs(0), hs(0), ws(0), ws(0), ws(0)],
        out_shape=[jax.ShapeDtypeStruct((t, GDN_W), BF16)] * 3 + [jax.ShapeDtypeStruct((GDN_CONV, GDN_W), F32)] * 3,
        name=name, compiler_params=_cp())(h, h, h, cw, cw, cw, dq, dk, dv)


def _make_mm2(wide):
    def raw(a, b, dims):
        if wide:
            return lax.dot_general(a, b, (dims, ((), ())), precision=lax.Precision.HIGHEST, preferred_element_type=F32)
        return lax.dot_general(a.astype(BF16), b.astype(BF16), (dims, ((), ())), preferred_element_type=F32)

    @jax.custom_vjp
    def nn(a, b):
        return raw(a, b, ((1,), (0,)))

    @jax.custom_vjp
    def nt(a, b):
        return raw(a, b, ((1,), (1,)))

    @jax.custom_vjp
    def tn(a, b):
        return raw(a, b, ((0,), (0,)))

    nn.defvjp(lambda a, b: (nn(a, b), (a, b)), lambda r, g: (nt(g, r[1]), tn(r[0], g)))
    nt.defvjp(lambda a, b: (nt(a, b), (a, b)), lambda r, g: (nn(g, r[1]), tn(g, r[0])))
    tn.defvjp(lambda a, b: (tn(a, b), (a, b)), lambda r, g: (nt(r[1], g), nn(r[0], g)))
    return nn, nt, tn


_NN, _NT, _TN = _make_mm2(False)
_NNW, _NTW, _TNW = _make_mm2(True)


def _square_masks(c):
    ri = lax.broadcasted_iota(jnp.int32, (c, c), 0)
    ci = lax.broadcasted_iota(jnp.int32, (c, c), 1)
    return ri >= ci, ri > ci, ri == ci


def _cumsum_rows(m):
    tri, _, _ = _square_masks(m.shape[0])
    return _NNW(tri.astype(F32), m)


def _transpose_sq(m):
    _, _, eye = _square_masks(m.shape[0])
    return _NTW(eye.astype(F32), m)


@jax.custom_vjp
def _inv_unit_lower(l):
    c = l.shape[0]
    _, _, eye = _square_masks(c)
    p = -l
    t = eye.astype(F32) + p
    for _ in range(int(math.log2(c)) - 1):
        p = _NNW(p, p)
        t = t + _NNW(t, p)
    return t


def _inv_fwd(l):
    t = _inv_unit_lower(l)
    return t, t


def _inv_bwd(t, dt):
    return (-_NTW(_TNW(t, dt), t),)


_inv_unit_lower.defvjp(_inv_fwd, _inv_bwd)


@jax.custom_vjp
def _inv_known(l, t):
    return t


_inv_known.defvjp(lambda l, t: (t, t), lambda t, dt: (_inv_bwd(t, dt)[0], jnp.zeros_like(t)))


def _softplus(x):
    return jnp.maximum(x, 0.0) + jnp.log1p(jnp.exp(-jnp.abs(x)))


def _gdn_chunk(q, k, v, braw, araw, alog, dtb, state, inv=None):
    c = q.shape[0]
    dv = v.shape[1]
    tri, strict, _ = _square_masks(c)
    beta = _sig(braw)
    g = -jnp.exp(alog) * _softplus(araw + dtb)
    gcm = _cumsum_rows(g * jnp.ones((c, c), F32))
    gct = _transpose_sq(gcm)
    decay = jnp.where(tri, jnp.exp(jnp.where(tri, gcm - gct, 0.0)), 0.0)
    gc = jnp.sum(gcm, 1, keepdims=True) * (1.0 / c)
    glast = jnp.sum(g, 0, keepdims=True)
    egc = jnp.exp(gc)
    kb = k * beta
    low = jnp.where(strict, _NT(kb, k) * decay, 0.0)
    tm = _inv_unit_lower(low) if inv is None else _inv_known(low, inv)
    sol = _NNW(tm, jnp.concatenate([v * beta, kb * egc], 1))
    u, w = sol[:, :dv], sol[:, dv:]
    attn = jnp.where(tri, _NT(q, k) * decay, 0.0)
    k_dec = k * jnp.exp(glast - gc)
    q_dec = q * egc
    v_new = u - _NN(w, state)
    o = _NN(q_dec, state) + _NN(attn, v_new)
    new_state = state * jnp.exp(glast) + _TN(k_dec, v_new)
    return o, new_state, tm


def _gdn_specs(t, rev):
    nch = t // GDN_CHUNK
    cm = (lambda n: nch - 1 - n) if rev else (lambda n: n)
    tok = pl.BlockSpec((GDN_HEADS, GDN_CHUNK, GDN_DK), lambda n: (0, cm(n), 0))
    par = pl.BlockSpec((GDN_HEADS, 1, LANES), lambda n: (0, 0, 0))
    st = pl.BlockSpec((GDN_HEADS, 1, GDN_DK, GDN_DV), lambda n: (0, cm(n), 0, 0))
    inv = pl.BlockSpec((GDN_HEADS, GDN_CHUNK, GDN_CHUNK), lambda n: (0, cm(n), 0))
    sc = pl.BlockSpec((GDN_CHUNK, LANES), lambda n: (cm(n), 4 * GDN_W // LANES))
    return tok, par, st, inv, sc


def _head_columns(sc_ref, first):
    return jnp.stack([sc_ref[:, first + hh:first + hh + 1] for hh in range(GDN_HEADS)])


def _gdn_core_fwd(q, k, v, h, alog, dtb, *, name):
    t = q.shape[1]
    nch = t // GDN_CHUNK

    def body(q_ref, k_ref, v_ref, sc_ref, al_ref, dt_ref, o_ref, st_ref, inv_ref, state):
        @pl.when(pl.program_id(0) == 0)
        def _():
            state[...] = jnp.zeros_like(state)

        s0 = state[...]
        st_ref[:, 0] = s0
        o, s1, tm = jax.vmap(_gdn_chunk)(q_ref[...], k_ref[...], v_ref[...], _head_columns(sc_ref, 0),
                                         _head_columns(sc_ref, GDN_HEADS), al_ref[:, :, 0:1], dt_ref[:, :, 0:1], s0)
        o_ref[...] = o
        inv_ref[...] = tm
        state[...] = s1

    tok, par, st, inv, sc = _gdn_specs(t, False)
    return pl.pallas_call(
        body, grid=(nch,), in_specs=[tok, tok, tok, sc, par, par], out_specs=[tok, st, inv],
        out_shape=[jax.ShapeDtypeStruct((GDN_HEADS, t, GDN_DV), F32),
                   jax.ShapeDtypeStruct((GDN_HEADS, nch, GDN_DK, GDN_DV), F32),
                   jax.ShapeDtypeStruct((GDN_HEADS, t, GDN_CHUNK), F32)],
        scratch_shapes=[pltpu.VMEM((GDN_HEADS, GDN_DK, GDN_DV), F32)],
        name=name, compiler_params=_cp())(q, k, v, h, alog, dtb)


def _gdn_core_bwd(q, k, v, h, alog, dtb, states, invs, do, *, name):
    t = q.shape[1]
    nch = t // GDN_CHUNK

    def body(q_ref, k_ref, v_ref, sc_ref, al_ref, dt_ref, st_ref, inv_ref, do_ref,
             dq_ref, dk_ref, dv_ref, dsc_ref, dal_ref, ddt_ref, dstate):
        @pl.when(pl.program_id(0) == 0)
        def _():
            dstate[...] = jnp.zeros_like(dstate)
            dal_ref[...] = jnp.zeros_like(dal_ref)
            ddt_ref[...] = jnp.zeros_like(ddt_ref)

        args = (q_ref[...], k_ref[...], v_ref[...], _head_columns(sc_ref, 0), _head_columns(sc_ref, GDN_HEADS),
                al_ref[:, :, 0:1], dt_ref[:, :, 0:1], st_ref[:, 0])
        tm = inv_ref[...]

        def chunk(*a):
            return jax.vmap(_gdn_chunk)(*a, tm)[:2]

        _, pull = jax.vjp(chunk, *args)
        dq, dk, dv, dbr, dar, dal, ddt, ds = pull((do_ref[...], dstate[...]))
        dq_ref[...] = dq
        dk_ref[...] = dk
        dv_ref[...] = dv
        lane = lax.broadcasted_iota(jnp.int32, (GDN_CHUNK, LANES), 1)
        dsc = jnp.zeros((GDN_CHUNK, LANES), F32)
        for hh in range(GDN_HEADS):
            dsc = jnp.where(lane == hh, dbr[hh], dsc)
            dsc = jnp.where(lane == GDN_HEADS + hh, dar[hh], dsc)
        dsc_ref[...] = dsc
        dal_ref[...] += dal + jnp.zeros((GDN_HEADS, 1, LANES), F32)
        ddt_ref[...] += ddt + jnp.zeros((GDN_HEADS, 1, LANES), F32)
        dstate[...] = ds

    tok, par, st, inv, sc = _gdn_specs(t, True)
    tokshape = jax.ShapeDtypeStruct((GDN_HEADS, t, GDN_DK), F32)
    parshape = jax.ShapeDtypeStruct((GDN_HEADS, 1, LANES), F32)
    nch_map = pl.BlockSpec((GDN_CHUNK, LANES), lambda n: (nch - 1 - n, 0))
    return pl.pallas_call(
        body, grid=(nch,), in_specs=[tok, tok, tok, sc, par, par, st, inv, tok],
        out_specs=[tok, tok, tok, nch_map, par, par],
        out_shape=[tokshape] * 3 + [jax.ShapeDtypeStruct((t, LANES), F32)] + [parshape] * 2,
        scratch_shapes=[pltpu.VMEM((GDN_HEADS, GDN_DK, GDN_DV), F32)],
        name=name, compiler_params=_cp())(q, k, v, h, alog, dtb, states, invs, do)


GDN_ROWS = 512


def _gdn_post_fwd(o, h, nw, *, name):
    t = o.shape[1]

    def body(o_ref, g_ref, nw_ref, y_ref):
        oo = o_ref[0]
        r = lax.rsqrt(jnp.mean(oo * oo, -1, keepdims=True) + EPS)
        y_ref[...] = (oo * r * nw_ref[...] * _silu(g_ref[...])).astype(BF16)

    return pl.pallas_call(
        body, grid=(GDN_HEADS, t // GDN_ROWS),
        in_specs=[pl.BlockSpec((1, GDN_ROWS, GDN_DV), lambda hh, i: (hh, i, 0)),
                  pl.BlockSpec((GDN_ROWS, GDN_DV), lambda hh, i: (i, 3 * GDN_HEADS + hh)),
                  pl.BlockSpec((1, GDN_DV), lambda hh, i: (0, 0))],
        out_specs=pl.BlockSpec((GDN_ROWS, GDN_DV), lambda hh, i: (i, hh)),
        out_shape=jax.ShapeDtypeStruct((t, GDN_W), BF16), name=name, compiler_params=_cp())(o, h, nw)


def _gdn_post_bwd(o, h, nw, dy, *, name):
    t = o.shape[1]

    def body(o_ref, g_ref, nw_ref, dy_ref, do_ref, dg_ref, dnw_ref):
        oo, gg, nw_, dy_ = o_ref[0], g_ref[...], nw_ref[...], dy_ref[...]
        r = lax.rsqrt(jnp.mean(oo * oo, -1, keepdims=True) + EPS)
        n = oo * r
        sg = _silu(gg)
        dg_ref[...] = (dy_ * n * nw_ * _dsilu(gg)).astype(BF16)
        dn = dy_ * sg * nw_
        do_ref[0] = r * (dn - n * jnp.mean(dn * n, -1, keepdims=True))

        @pl.when((pl.program_id(0) == 0) & (pl.program_id(1) == 0))
        def _():
            dnw_ref[...] = jnp.zeros_like(dnw_ref)

        dnw_ref[...] += jnp.sum(dy_ * sg * n, 0, keepdims=True)

    return pl.pallas_call(
        body, grid=(GDN_HEADS, t // GDN_ROWS),
        in_specs=[pl.BlockSpec((1, GDN_ROWS, GDN_DV), lambda hh, i: (hh, i, 0)),
                  pl.BlockSpec((GDN_ROWS, GDN_DV), lambda hh, i: (i, 3 * GDN_HEADS + hh)),
                  pl.BlockSpec((1, GDN_DV), lambda hh, i: (0, 0)),
                  pl.BlockSpec((GDN_ROWS, GDN_DV), lambda hh, i: (i, hh))],
        out_specs=[pl.BlockSpec((1, GDN_ROWS, GDN_DV), lambda hh, i: (hh, i, 0)),
                   pl.BlockSpec((GDN_ROWS, GDN_DV), lambda hh, i: (i, hh)),
                   pl.BlockSpec((1, GDN_DV), lambda hh, i: (0, 0))],
        out_shape=[jax.ShapeDtypeStruct((GDN_HEADS, t, GDN_DV), F32), jax.ShapeDtypeStruct((t, GDN_W), BF16),
                   jax.ShapeDtypeStruct((1, GDN_DV), F32)],
        name=name, compiler_params=_cp())(o, h, nw, dy)


def _tables(positions):
    pos = positions.astype(F32)[:, None]
    half = RET_DK // 2
    inv = jnp.power(RET_THETA, -jnp.arange(half, dtype=F32) * 2.0 / RET_DK)
    ang = pos * inv
    cos, sin = jnp.cos(ang), jnp.sin(ang)
    c2a = jnp.concatenate([cos, cos], 1)
    s2a = jnp.concatenate([-sin, sin], 1)
    hb = ROPE_DIMS // 2
    invb = jnp.power(ROPE_THETA, -jnp.arange(hb, dtype=F32) * 2.0 / ROPE_DIMS)
    angb = pos * invb
    cosb, sinb = jnp.cos(angb), jnp.sin(angb)
    t = pos.shape[0]
    ones = jnp.ones((t, DIL_HD - ROPE_DIMS), F32)
    zeros = jnp.zeros((t, DIL_HD - ROPE_DIMS), F32)
    z8 = jnp.zeros((t, hb), F32)
    cb = jnp.concatenate([cosb, cosb, ones] * 2, 1)
    shi = jnp.concatenate([z8, sinb, zeros] * 2, 1)
    slo = jnp.concatenate([-sinb, z8, zeros] * 2, 1)
    lg = jnp.log1p(-jnp.power(2.0, -5.0 - jnp.arange(RET_HEADS, dtype=F32)))
    lgt = jnp.broadcast_to(lg[:, None, None], (RET_HEADS, 1, LANES))
    delta = jnp.arange(ATT_BLK, dtype=jnp.int32)[:, None] + (SEQ - ATT_BLK) - jnp.arange(SEQ, dtype=jnp.int32)[None, :]
    cnt = jnp.zeros(delta.shape, F32)
    for (w, d) in DIL_PAIRS:
        cnt = cnt + ((delta >= 0) & (delta <= w) & (delta % d == 0)).astype(F32)
    strip = jnp.where(cnt > 0, jnp.log(jnp.maximum(cnt, 1.0)), NEG)
    return c2a, s2a, cb, shi, slo, lgt, strip


def _local_step(x, tables, target, get_w, mid, put_g, small):
    c2a, s2a, cb, shi, slo, lgt, strip = tables
    t = x.shape[0]
    saved = []
    xf = x
    xb = x.astype(BF16)
    for layer in range(DEPTH):
        j = layer // 2
        L = f"L{layer}_"
        W, dep = get_w(layer, "mixer", xb)
        rec = {"x": xf, "xb": xb}
        if layer % 2 == 0:
            h = _mm(xb, W["in_t"], tb=True, name=L + "ev_in", dep=dep)
            ro, ya = _ret_fwd(h, c2a, s2a, lgt, name=L + "ret_fwd")
            do_, yb, lse = _dil_fwd(h, cb, shi, slo, strip, name=L + "dil_fwd")
            y = jnp.concatenate([ya, yb], 1)
            rec.update(h=h, ro=ro, dil_o=do_, lse=lse, y=y)
        else:
            h = _mm(xb, W["in_t"], tb=True, name=L + "od_in", dep=dep)
            cw = W["conv"]
            q, k, v = _gdn_prep_fwd(h, cw, name=L + "gdn_prep")
            alog = jnp.broadcast_to(small["od_a_log"][j][:, None, None], (GDN_HEADS, 1, LANES))
            dtb = jnp.broadcast_to(small["od_dt_bias"][j][:, None, None], (GDN_HEADS, 1, LANES))
            o, states, invs = _gdn_core_fwd(q, k, v, h, alog, dtb, name=L + "gdn_fwd")
            nw = small["od_norm_w"][j][None, :]
            y = _gdn_post_fwd(o, h, nw, name=L + "gdn_post")
            rec.update(h=h, q=q, k=k, v=v, alog=alog, dtb=dtb, states=states, invs=invs, o=o, y=y, nw=nw, cw=cw)
        z1, x1, x1b = _mm_ln_fwd(y, W["out"], xf, small["ln1_g"][layer][None], small["ln1_b"][layer][None],
                                 name=L + "out_ln1", dep=mid(layer, "mixer", y))
        rec["Wm"] = W
        W, dep = get_w(layer, "ffn", x1b)
        rec["Wf"] = W
        fcw = W["fconv"]
        fcb = small["ffn_conv_b"][layer][None]
        ug, uv, a = _ffn_up_mid(x1b, W["up_t"], fcw, fcb, name=L + "ffn_up_mid", dep=dep)
        z2, x2, x2b = _mm_ln_fwd(a, W["down"], x1, small["ln2_g"][layer][None], small["ln2_b"][layer][None],
                                 name=L + "down_ln2", dep=mid(layer, "ffn", a))
        rec.update(z1=z1, x1b=x1b, ug=ug, uv=uv, a=a, z2=z2, fcw=fcw, fcb=fcb)
        saved.append(rec)
        xf, xb = x2, x2b

    dy, lossv = _loss_head(xf, target, name="loss_head")
    loss = lossv[0, 0]

    gS = {n: [None] * small[n].shape[0] for n in small}
    below = None
    for layer in reversed(range(DEPTH)):
        j = layer // 2
        L = f"L{layer}_"
        rec = saved[layer]
        Wm, Wf = rec["Wm"], rec["Wf"]
        g = {}
        if below is None:
            dz2, dz2b, dg2, db2 = _ln_bwd(rec["z2"], small["ln2_g"][layer][None], dy, None, name=L + "ln2_bwd")
        else:
            dz2, dz2b, dg2, db2 = _mm_ln_bwd(below[0], below[1], rec["z2"], small["ln2_g"][layer][None], below[2],
                                             name=L + "ln2_bwd", dep=below[3])
        gS["ln2_g"][layer], gS["ln2_b"][layer] = dg2[0], db2[0]
        g["down"] = _mm(rec["a"], dz2b, ta=True, name=L + "ffn_down_dw", out_dtype=BF16)
        du, dcw, dcb = _ffn_mid_bwd(rec["ug"], rec["uv"], rec["fcw"], rec["fcb"], dz2b, Wf["down"], name=L + "ffn_mid_bwd")
        g["fconv"] = dcw.astype(BF16)
        gS["ffn_conv_b"][layer] = dcb[0]
        g["up_t"] = _mm_tn_parts(du, rec["x1b"], name=L + "ffn_up_dw")
        dep = put_g(layer, "ffn", g)
        dz1, dz1b, dg1, db1 = _mm_ln_bwd(du, Wf["up_t"], rec["z1"], small["ln1_g"][layer][None], dz2,
                                         name=L + "ln1_bwd", dep=dep)
        gS["ln1_g"][layer], gS["ln1_b"][layer] = dg1[0], db1[0]
        g = {}
        if layer % 2 == 0:
            g["out"] = _mm(rec["y"], dz1b, ta=True, name=L + "ev_out_dw", out_dtype=BF16)
            dyy = _mm(dz1b, Wm["out"], tb=True, name=L + "ev_out_dx")
            dqa, dka, dva, dga = _ret_bwd(rec["h"], c2a, s2a, lgt, rec["ro"], dyy, name=L + "ret_bwd")
            dqb, dkb, dvb = _dil_bwd(rec["h"], cb, shi, slo, strip, rec["dil_o"], rec["lse"], dyy, name=L + "dil_bwd")
            dh = [dqa, dka, dva, dga, dqb, dkb, dvb]
            g["in_t"] = _mm_tn_parts(dh, rec["xb"], name=L + "ev_in_dw")
            dep = put_g(layer, "mixer", g)
        else:
            g["out"] = _mm(rec["y"], dz1b, ta=True, name=L + "od_out_dw", out_dtype=BF16)
            dyy = _mm(dz1b, Wm["out"], tb=True, name=L + "od_out_dx")
            do, dgate, dnw = _gdn_post_bwd(rec["o"], rec["h"], rec["nw"], dyy, name=L + "gdn_post_bwd")
            gS["od_norm_w"][j] = dnw[0]
            dq, dk, dv, dsc, dal, ddt = _gdn_core_bwd(
                rec["q"], rec["k"], rec["v"], rec["h"], rec["alog"], rec["dtb"], rec["states"], rec["invs"], do,
                name=L + "gdn_bwd")
            gS["od_a_log"][j] = dal[:, 0, 0]
            gS["od_dt_bias"][j] = ddt[:, 0, 0]
            dhq, dhk, dhv, dwq, dwk, dwv = _gdn_prep_bwd(rec["h"], rec["cw"], dq, dk, dv, name=L + "gdn_prep_bwd")
            g["conv"] = jnp.concatenate([dwq, dwk, dwv], 1).astype(BF16)
            dh = [dhq, dhk, dhv, dgate, dsc.astype(BF16)]
            g["in_t"] = (_mm_tn_parts(dh[:4], rec["xb"], name=L + "od_in_dw"),
                         _mm(dh[4], rec["xb"], ta=True, name=L + "od_in_dw_logits", out_dtype=BF16))
            dep = put_g(layer, "mixer", g)
        below = (dh, Wm["in_t"], dz1, dep)
    grad_x = _axpy(_mm(jnp.concatenate(below[0], 1), below[1], name="L0_in_dx", dep=below[3]), below[2], name="grad_x")
    gS = {n: jnp.stack(v) for n, v in gS.items()}
    return loss, grad_x, gS


HBM = pl.BlockSpec(memory_space=pltpu.HBM)


def _me():
    return lax.axis_index("x"), lax.axis_index("y"), lax.axis_index("c")


def _my_index():
    x, y, c = _me()
    return 4 * x + 2 * y + c


SEM = pl.BlockSpec(memory_space=pltpu.SEMAPHORE)
ANY = pl.BlockSpec(memory_space=pl.ANY)
PLANS = {"scatter": (1, 2, 3, 4, 5, 6, 7), "spread": (1, 2, 4, 6), "relay": (2, 4, 6)}
SIBLING = 1


def _peer(kk):
    x, y, c = _me()
    return x ^ (kk >> 2), y ^ ((kk >> 1) & 1), c ^ (kk & 1)


def _peer_index(kk):
    px, py, pc = _peer(kk)
    return 4 * px + 2 * py + pc


def _job_copies(mode, srcs, lands, send_sems, recv_sems, incoming):
    myid = _my_index()
    plan = PLANS[mode]
    out = []
    for a in range(len(lands)):
        for idx, kk in enumerate(plan):
            if mode == "relay":
                to, src = _peer(SIBLING), lands[a].at[_peer_index(kk)]
                slot_there, slot_here = _peer_index(kk), _peer_index(kk ^ SIBLING)
            else:
                to, src = _peer(kk), (srcs[a] if mode == "spread" else srcs[a].at[_peer_index(kk)])
                slot_there, slot_here = myid, _peer_index(kk)
            sem = a * len(plan) + idx
            out.append(pltpu.make_async_remote_copy(
                src_ref=src, dst_ref=lands[a].at[slot_here if incoming else slot_there],
                send_sem=send_sems.at[sem], recv_sem=recv_sems.at[sem], device_id=to, device_id_type=MESH))
    return out


def _split_jobs(jobs, arrays):
    out, o = [], 0
    for (_, srcs, lands) in jobs:
        out.append((arrays[o:o + len(srcs)], arrays[o + len(srcs):o + len(srcs) + len(lands)]))
        o += len(srcs) + len(lands)
    return out


def _exchange_start(jobs, after, *, name):
    jobs = [(mode, list(srcs), [lax.empty((N_DEV, *s.shape) if mode == "spread" else s.shape, s.dtype) for s in srcs]
             if lands is None else list(lands)) for (mode, srcs, lands) in jobs]
    flat = [a for (_, srcs, lands) in jobs for a in (*srcs, *lands)]
    n, nj = len(flat), len(jobs)
    nsem = [len(PLANS[mode]) * len(lands) for (mode, _, lands) in jobs]

    def body(*refs):
        o = n + (0 if after is None else 1)
        sems, token = refs[o:o + 2 * nj], refs[o + 2 * nj + n]
        for ji, ((mode, _, _), (src, land)) in enumerate(zip(jobs, _split_jobs(jobs, refs[:n]))):
            for cp in _job_copies(mode, src, land, sems[2 * ji], sems[2 * ji + 1], False):
                cp.start()
        token[...] = jnp.zeros_like(token)

    outs = pl.pallas_call(
        body, name=name,
        out_shape=(*[pltpu.SemaphoreType.DMA((ns,)) for ns in nsem for _ in range(2)],
                   *[pltpu.HBM(a.shape, a.dtype) for a in flat], jax.ShapeDtypeStruct((8, LANES), F32)),
        in_specs=[HBM] * n + ([] if after is None else [ANY]),
        out_specs=(*[SEM] * (2 * nj), *[HBM] * n, pl.BlockSpec(memory_space=pltpu.VMEM)),
        input_output_aliases={i: 2 * nj + i for i in range(n)},
        compiler_params=pltpu.CompilerParams(has_side_effects=pltpu.SideEffectType.DATAFLOW_SIDE_EFFECTING),
    )(*[pltpu.with_memory_space_constraint(a, pltpu.HBM) for a in flat], *([] if after is None else [after]))
    thru = _split_jobs(jobs, list(outs[2 * nj:2 * nj + n]))
    started = [(mode, outs[2 * ji], outs[2 * ji + 1], src, land) for ji, ((mode, _, _), (src, land)) in enumerate(zip(jobs, thru))]
    return started, outs[2 * nj + n]


def _exchange_wait(started, after, *, name):
    jobs = [(mode, srcs, lands) for (mode, _, _, srcs, lands) in started]
    flat = [a for (_, srcs, lands) in jobs for a in (*srcs, *lands)]
    n, nj = len(flat), len(jobs)

    def body(*refs):
        sems = refs[n:n + 2 * nj]
        for ji, ((mode, _, _), (src, land)) in enumerate(zip(jobs, _split_jobs(jobs, refs[:n]))):
            for cp in _job_copies(mode, src, land, sems[2 * ji], sems[2 * ji + 1], True):
                cp.wait_send()
                cp.wait_recv()

    outs = pl.pallas_call(
        body, name=name, out_shape=tuple(pltpu.HBM(a.shape, a.dtype) for a in flat),
        in_specs=[HBM] * n + [SEM] * (2 * nj) + [ANY], out_specs=tuple([HBM] * n),
        input_output_aliases={i: i for i in range(n)},
        compiler_params=pltpu.CompilerParams(has_side_effects=pltpu.SideEffectType.DATAFLOW_SIDE_EFFECTING),
    )(*flat, *[s for (_, ss, rs, _, _) in started for s in (ss, rs)], after)
    return _split_jobs(jobs, list(outs))


def _sum8(land, stack, j, depth, *, name):
    _, rr, cc = land.shape
    tr = _row_tile(rr)

    def body(l_ref, *rest):
        o_ref = rest[-1]
        acc = l_ref[0].astype(F32)
        for d in range(1, N_DEV):
            acc = acc + l_ref[d].astype(F32)
        o_ref[0] = acc

    prev = [] if stack is None else [stack]
    return pl.pallas_call(
        body, grid=(rr // tr,),
        in_specs=[pl.BlockSpec((N_DEV, tr, cc), lambda i: (0, i, 0))] + [pl.BlockSpec(memory_space=pl.ANY)] * len(prev),
        out_specs=pl.BlockSpec((1, tr, cc), lambda i: (j, i, 0)), out_shape=jax.ShapeDtypeStruct((depth, rr, cc), F32),
        input_output_aliases={1: 0} if prev else {}, name=name, compiler_params=_cp())(land, *prev)


def _row_tile(rr):
    for cand in (512, 384, 256, 192, 176, 128, 64, 32, 16, 8):
        if rr % cand == 0:
            return cand
    return rr


def _small_exchange(vec, *, name):
    rr = vec.shape[0]

    def body(v_ref, o_ref, send_sems, recv_sems):
        x, y, c = _me()
        myid = 4 * x + 2 * y + c
        o_ref[myid] = v_ref[...]
        cps = []
        for kk in range(1, N_DEV):
            px, py, pc = x ^ (kk >> 2), y ^ ((kk >> 1) & 1), c ^ (kk & 1)
            cps.append(pltpu.make_async_remote_copy(
                src_ref=v_ref, dst_ref=o_ref.at[myid], send_sem=send_sems.at[kk], recv_sem=recv_sems.at[kk],
                device_id=(px, py, pc), device_id_type=MESH))
        for cp in cps:
            cp.start()
        for kk in range(1, N_DEV):
            px, py, pc = x ^ (kk >> 2), y ^ ((kk >> 1) & 1), c ^ (kk & 1)
            pltpu.make_async_remote_copy(
                src_ref=v_ref, dst_ref=o_ref.at[4 * px + 2 * py + pc], send_sem=send_sems.at[kk],
                recv_sem=recv_sems.at[kk], device_id=(px, py, pc), device_id_type=MESH).wait_recv()
        for cp in cps:
            cp.wait_send()

    return pl.pallas_call(
        body, in_specs=[pl.BlockSpec(memory_space=pltpu.VMEM)], out_specs=pl.BlockSpec(memory_space=pltpu.VMEM),
        out_shape=jax.ShapeDtypeStruct((N_DEV, rr, LANES), F32),
        scratch_shapes=[pltpu.SemaphoreType.DMA((N_DEV,)), pltpu.SemaphoreType.DMA((N_DEV,))],
        name=name, compiler_params=pltpu.CompilerParams(has_side_effects=True))(vec)


def _adam_math(w, g, m, v):
    m = ADAM_B1 * m + (1.0 - ADAM_B1) * g
    v = ADAM_B2 * v + (1.0 - ADAM_B2) * (g * g)
    m_hat = m / (1.0 - ADAM_B1 ** ADAM_STEP)
    v_hat = v / (1.0 - ADAM_B2 ** ADAM_STEP)
    delta = -ADAM_LR * (m_hat / (jnp.sqrt(v_hat) + ADAM_EPS) + ADAM_WD * w)
    return delta, m, v


def _adamw_sharded(w, m, v, g, *, name):
    ll, rr, cc = w.shape
    tr = _row_tile(rr)

    def body(w_ref, m_ref, v_ref, g_ref, d_ref, nm_ref, nv_ref):
        d, nm, nv = _adam_math(w_ref[...], g_ref[...], m_ref[...], v_ref[...])
        d_ref[...] = d
        nm_ref[...] = nm
        nv_ref[...] = nv

    blk = pl.BlockSpec((1, tr, cc), lambda l, i: (l, i, 0))
    sh = jax.ShapeDtypeStruct((ll, rr, cc), F32)
    return pl.pallas_call(
        body, grid=(ll, rr // tr), in_specs=[blk] * 4, out_specs=[blk] * 3, out_shape=[sh] * 3,
        name=name, compiler_params=_cp())(w, m, v, g)


def _adamw_small(w, m, v, gall, *, name):
    rr = w.shape[0]

    def body(w_ref, m_ref, v_ref, g_ref, go_ref, d_ref, nm_ref, nv_ref):
        g = g_ref[0]
        for kk in range(1, N_DEV):
            g = g + g_ref[kk]
        d, nm, nv = _adam_math(w_ref[...], g, m_ref[...], v_ref[...])
        go_ref[...] = g
        d_ref[...] = d
        nm_ref[...] = nm
        nv_ref[...] = nv

    sh = jax.ShapeDtypeStruct((rr, LANES), F32)
    return pl.pallas_call(body, out_shape=[sh] * 4, name=name, compiler_params=_cp())(w, m, v, gall)


SHARDED = ("ev_w_in", "ev_w_out", "od_w_in", "od_conv_w", "od_w_out", "ffn_w_up", "ffn_conv_w", "ffn_w_down")
SMALL = ("od_a_log", "od_dt_bias", "od_norm_w", "ffn_conv_b", "ln1_g", "ln1_b", "ln2_g", "ln2_b")
ALL_W = ("ev_w_in", "ev_w_out", "od_w_in", "od_conv_w", "od_a_log", "od_dt_bias", "od_norm_w", "od_w_out",
         "ffn_w_up", "ffn_conv_w", "ffn_conv_b", "ffn_w_down", "ln1_g", "ln1_b", "ln2_g", "ln2_b")


def _layer_items(layer):
    j = layer // 2
    if layer % 2 == 0:
        mixer = [("in_t", "ev_w_in", j, "colT"), ("out", "ev_w_out", j, "row")]
    else:
        mixer = [("in_t", "od_w_in", j, "colT"), ("conv", "od_conv_w", j, "colsmall"), ("out", "od_w_out", j, "row")]
    return mixer + [("up_t", "ffn_w_up", layer, "colT"), ("fconv", "ffn_conv_w", layer, "colsmall"),
                    ("down", "ffn_w_down", layer, "row")]


OD_SHARD = OD_IN // N_DEV
OD_SHARD_PAD = OD_IN_PAD // N_DEV


def _od_pack(g, *, name):
    d = g.shape[-1]

    def body(g_ref, o_ref):
        for n in range(N_DEV):
            o_ref[OD_SHARD * n:OD_SHARD * (n + 1), :] = g_ref[n, 0:OD_SHARD, :]
        o_ref[OD_IN:OD_IN_PAD, :] = jnp.zeros((OD_IN_PAD - OD_IN, d), g.dtype)

    return pl.pallas_call(body, out_shape=jax.ShapeDtypeStruct((OD_IN_PAD, d), g.dtype), name=name,
                          compiler_params=_cp())(g)


def _od_unpack(main, tail, *, name):
    d = main.shape[-1]
    split = main.shape[0]

    def body(m_ref, t_ref, o_ref):
        for n in range(N_DEV):
            lo, hi = OD_SHARD * n, OD_SHARD * (n + 1)
            from_main = min(hi, split) - lo
            o_ref[n, 0:from_main, :] = m_ref[lo:lo + from_main, :]
            if hi > split:
                o_ref[n, from_main:OD_SHARD, :] = t_ref[0:hi - split, :]
            o_ref[n, OD_SHARD:OD_SHARD_PAD, :] = jnp.zeros((OD_SHARD_PAD - OD_SHARD, d), main.dtype)

    return pl.pallas_call(body, out_shape=jax.ShapeDtypeStruct((N_DEV, OD_SHARD_PAD, d), main.dtype), name=name,
                          compiler_params=_cp())(main, tail)


def _to_send(kind, name, w, j):
    if kind == "colT":
        s = w[j].T.astype(BF16)
        return jnp.pad(s, ((0, OD_SHARD_PAD - OD_SHARD), (0, 0))) if name == "od_w_in" else s
    return w[j].astype(BF16) if kind == "row" else w[j]


def _from_gather(kind, name, g, tag):
    if kind == "colsmall":
        return jnp.transpose(g, (1, 0, 2)).reshape(g.shape[1], -1)
    if name == "od_w_in":
        return _od_pack(g, name=tag + "_pack")
    return g.reshape(-1, g.shape[-1])


def _by_owner(kind, name, gfull, tag):
    if kind == "colsmall":
        kk, c8 = gfull.shape
        return jnp.transpose(gfull.reshape(kk, N_DEV, c8 // N_DEV), (1, 0, 2))
    if name == "od_w_in":
        return _od_unpack(*gfull, name=tag + "_unpack")
    return gfull.reshape(N_DEV, gfull.shape[0] // N_DEV, gfull.shape[1])


def _pack_small(d):
    flat = jnp.concatenate([d[n].reshape(-1) for n in SMALL])
    pad = (-flat.shape[0]) % (8 * LANES)
    return jnp.pad(flat, (0, pad)).reshape(-1, LANES)


def _unpack_small(packed, like):
    flat = packed.reshape(-1)
    out, off = {}, 0
    for n in SMALL:
        sz = int(np.prod(like[n].shape))
        out[n] = flat[off:off + sz].reshape(like[n].shape)
        off += sz
    return out


def kernel(x, positions, ev_w_in, ev_w_out, od_w_in, od_conv_w, od_a_log, od_dt_bias, od_norm_w, od_w_out, ffn_w_up, ffn_conv_w, ffn_conv_b, ffn_w_down, ln1_g, ln1_b, ln2_g, ln2_b, loss_target, m_ev_w_in, m_ev_w_out, m_od_w_in, m_od_conv_w, m_od_a_log, m_od_dt_bias, m_od_norm_w, m_od_w_out, m_ffn_w_up, m_ffn_conv_w, m_ffn_conv_b, m_ffn_w_down, m_ln1_g, m_ln1_b, m_ln2_g, m_ln2_b, v_ev_w_in, v_ev_w_out, v_od_w_in, v_od_conv_w, v_od_a_log, v_od_dt_bias, v_od_norm_w, v_od_w_out, v_ffn_w_up, v_ffn_conv_w, v_ffn_conv_b, v_ffn_w_down, v_ln1_g, v_ln1_b, v_ln2_g, v_ln2_b):
    w = dict(ev_w_in=ev_w_in, ev_w_out=ev_w_out, od_w_in=od_w_in, od_conv_w=od_conv_w, od_a_log=od_a_log,
             od_dt_bias=od_dt_bias, od_norm_w=od_norm_w, od_w_out=od_w_out, ffn_w_up=ffn_w_up, ffn_conv_w=ffn_conv_w,
             ffn_conv_b=ffn_conv_b, ffn_w_down=ffn_w_down, ln1_g=ln1_g, ln1_b=ln1_b, ln2_g=ln2_g, ln2_b=ln2_b)
    mom = dict(ev_w_in=m_ev_w_in, ev_w_out=m_ev_w_out, od_w_in=m_od_w_in, od_conv_w=m_od_conv_w, od_a_log=m_od_a_log,
               od_dt_bias=m_od_dt_bias, od_norm_w=m_od_norm_w, od_w_out=m_od_w_out, ffn_w_up=m_ffn_w_up,
               ffn_conv_w=m_ffn_conv_w, ffn_conv_b=m_ffn_conv_b, ffn_w_down=m_ffn_w_down, ln1_g=m_ln1_g,
               ln1_b=m_ln1_b, ln2_g=m_ln2_g, ln2_b=m_ln2_b)
    var = dict(ev_w_in=v_ev_w_in, ev_w_out=v_ev_w_out, od_w_in=v_od_w_in, od_conv_w=v_od_conv_w, od_a_log=v_od_a_log,
               od_dt_bias=v_od_dt_bias, od_norm_w=v_od_norm_w, od_w_out=v_od_w_out, ffn_w_up=v_ffn_w_up,
               ffn_conv_w=v_ffn_conv_w, ffn_conv_b=v_ffn_conv_b, ffn_w_down=v_ffn_w_down, ln1_g=v_ln1_g,
               ln1_b=v_ln1_b, ln2_g=v_ln2_g, ln2_b=v_ln2_b)

    myid = _my_index()
    small = {n: w[n] for n in SMALL}
    groups = [(layer, part) for layer in range(DEPTH) for part in ("mixer", "ffn")]

    def group_items(gi):
        layer, part = groups[gi]
        its = _layer_items(layer)
        return its[:-3] if part == "mixer" else its[-3:]

    level1, level2 = {}, {}

    def spread_job(gi):
        return ("spread", [_to_send(kind, n, w[n], j) for (_, n, j, kind) in group_items(gi)], None)

    def relay(gi, after, name):
        (srcs, lands), = _exchange_wait([level1.pop(gi)], after, name=name + "_wait")
        more = [spread_job(gi + 1)] if gi + 1 < len(groups) else []
        started, token = _exchange_start([("relay", [], lands)] + more, None, name=name + "_start")
        level2[gi] = (started[0], srcs)
        if more:
            level1[gi + 1] = started[1]
        return token

    def get_w(layer, part, after):
        gi = groups.index((layer, part))
        started, srcs = level2.pop(gi)
        (_, lands), = _exchange_wait([started], after, name=f"gather{gi}_wait")
        lands = [lax.dynamic_update_index_in_dim(l, s, myid, 0) for l, s in zip(lands, srcs)]
        return {key: _from_gather(kind, n, l, f"L{layer}_{key}")
                for (key, n, _, kind), l in zip(group_items(gi), lands)}, None

    def mid(layer, part, after):
        gi = groups.index((layer, part)) + 1
        return relay(gi, after, f"gather{gi}_relay") if gi < len(groups) else None

    landed = {}
    pending = []

    def scatter_finish(after):
        started, gi = pending.pop()
        (srcs, lands), = _exchange_wait([started], after, name=f"scatter{gi}_wait")
        for (key, _, _, _), l, s in zip(group_items(gi), lands, srcs):
            own = lax.dynamic_index_in_dim(s, myid, 0, keepdims=False)
            landed[(groups[gi][0], key)] = lax.dynamic_update_index_in_dim(l, own, myid, 0)

    def put_g(layer, part, g):
        gi = groups.index((layer, part))
        srcs = [_by_owner(kind, n, g[key], f"L{layer}_{key}") for (key, n, _, kind) in group_items(gi)]
        (started,), token = _exchange_start([("scatter", srcs, None)], None, name=f"scatter{gi}_start")
        if pending:
            scatter_finish(token)
        pending.append((started, gi))
        return token

    (level1[0],), token = _exchange_start([spread_job(0)], None, name="gather0_spread_start")
    tables = _tables(positions[0] + token[0, 0].astype(jnp.int32))
    relay(0, tables[-1], "gather0_relay")
    loss, grad_x, gS = _local_step(x[0], tables, loss_target[0], get_w, mid, put_g, small)
    loss = lax.psum(loss, ("x", "y", "c"))

    outs_g, outs_d, outs_m, outs_v = {}, {}, {}, {}
    where = {n: [None] * w[n].shape[0] for n in SHARDED}
    for layer in range(DEPTH):
        for (key, n, j, kind) in _layer_items(layer):
            where[n][j] = (layer, key, kind)

    def update(n):
        g = None
        for j, (layer, key, _) in enumerate(where[n]):
            g = _sum8(landed[(layer, key)], g, j, len(where[n]), name=f"L{layer}_{key}_sum")
        if n == "od_w_in":
            g = g[:, :OD_SHARD]
        if where[n][0][2] == "colT":
            tr = lambda a: jnp.swapaxes(a, 1, 2)
            d, nm, nv = _adamw_sharded(tr(w[n]), tr(mom[n]), tr(var[n]), g, name=f"adamw_{n}")
            outs_g[n], outs_d[n], outs_m[n], outs_v[n] = tr(g), tr(d), tr(nm), tr(nv)
        else:
            outs_g[n] = g
            outs_d[n], outs_m[n], outs_v[n] = _adamw_sharded(w[n], mom[n], var[n], g, name=f"adamw_{n}")

    last = {n for (_, n, _, _) in group_items(pending[0][1])}
    for n in SHARDED:
        if n not in last:
            update(n)
    scatter_finish(outs_d[[n for n in SHARDED if n not in last][-1]])
    for n in SHARDED:
        if n in last:
            update(n)

    gall = _small_exchange(_pack_small(gS), name="small_grads_exchange")
    g, d, nm, nv = _adamw_small(_pack_small({n: w[n] for n in SMALL}), _pack_small({n: mom[n] for n in SMALL}),
                                _pack_small({n: var[n] for n in SMALL}), gall, name="adamw_small")
    for dst, packed in ((outs_g, g), (outs_d, d), (outs_m, nm), (outs_v, nv)):
        dst.update(_unpack_small(packed, {n: w[n] for n in SMALL}))

    return (loss, grad_x[None], *[outs_g[n] for n in ALL_W], *[outs_d[n] for n in ALL_W],
            *[outs_m[n] for n in ALL_W], *[outs_v[n] for n in ALL_W])
```

```python
import functools
import math

import numpy as np
import jax
import jax.numpy as jnp
from jax import lax
from jax.experimental import pallas as pl
from jax.experimental.pallas import tpu as pltpu

F32 = jnp.float32
BF16 = jnp.bfloat16
MESH = pl.DeviceIdType.MESH

D_MODEL = 1024
SEQ = 2048
DEPTH = 4
N_DEV = 8
RET_HEADS, RET_DK, RET_DV = 4, 128, 256
RET_THETA = 10000.0
DIL_HEADS, DIL_HD = 8, 64
DIL_PAIRS = ((128, 1), (512, 4), (2048, 16))
ROPE_THETA = 500000.0
ROPE_DIMS = DIL_HD // 4
GDN_HEADS, GDN_DK, GDN_DV, GDN_CHUNK, GDN_CONV = 8, 128, 128, 64, 4
D_FF = 2816
FFN_CONV = 3
ALPHA = (2.0 * DEPTH) ** 0.25
EPS = 1e-5
RET_QK_W = RET_HEADS * RET_DK
RET_V_W = RET_HEADS * RET_DV
DIL_W = DIL_HEADS * DIL_HD
EV_IN = 2 * RET_QK_W + 2 * RET_V_W + 3 * DIL_W
EV_MIX = RET_V_W + DIL_W
GDN_W = GDN_HEADS * GDN_DK
OD_IN = 4 * GDN_W + 2 * GDN_HEADS
OD_IN_PAD = 4 * GDN_W + 128
ADAM_LR, ADAM_B1, ADAM_B2, ADAM_EPS, ADAM_WD, ADAM_STEP = 0.001, 0.9, 0.999, 1e-08, 0.01, 10

LANES = 128
VMEM_LIMIT = 56 * 1024 * 1024
ATT_BLK = 256
NEG = -1e30


def _cp(**kw):
    return pltpu.CompilerParams(vmem_limit_bytes=VMEM_LIMIT, **kw)


def _tile(n, cap):
    if n <= cap:
        return n
    best = None
    for t in range(LANES, cap + 1, LANES):
        if n % t == 0:
            best = t
    assert best is not None, (n, cap)
    return best


def _mm(a, b, *, ta=False, tb=False, name, out_dtype=F32, dep=None, tm=None, tn=None):
    m = a.shape[1] if ta else a.shape[0]
    k = a.shape[0] if ta else a.shape[1]
    n = b.shape[0] if tb else b.shape[1]
    assert (b.shape[1] if tb else b.shape[0]) == k
    assert a.dtype == BF16 and b.dtype == BF16
    if tn is None:
        tn = n if n <= 1024 else _tile(n, 512)
    if tm is None:
        tm = m if (tn < n and k <= 1024 and m <= 2048) else _tile(m, 512)
    dims = (((0 if ta else 1,), (1 if tb else 0,)), ((), ()))

    def body(a_ref, b_ref, *rest):
        o_ref = rest[-1]
        o_ref[...] = lax.dot_general(a_ref[...], b_ref[...], dims,
                                     preferred_element_type=F32).astype(o_ref.dtype)

    a_spec = pl.BlockSpec((k, tm), lambda i, j: (0, i)) if ta else pl.BlockSpec((tm, k), lambda i, j: (i, 0))
    b_spec = pl.BlockSpec((tn, k), lambda i, j: (j, 0)) if tb else pl.BlockSpec((k, tn), lambda i, j: (0, j))
    extra = [] if dep is None else [dep]
    return pl.pallas_call(
        body, grid=(m // tm, n // tn), in_specs=[a_spec, b_spec] + [pl.BlockSpec(memory_space=pl.ANY)] * len(extra),
        out_specs=pl.BlockSpec((tm, tn), lambda i, j: (i, j)),
        out_shape=jax.ShapeDtypeStruct((m, n), out_dtype), name=name, compiler_params=_cp())(a, b, *extra)


LN_ROWS = 256


def _ln_bwd(z, g, dya, dyb, *, name):
    t, d = z.shape
    two = dyb is not None

    def body(*refs):
        if two:
            z_ref, g_ref, dya_ref, dyb_ref, dz_ref, dzb_ref, dg_ref, db_ref = refs
            dy = dya_ref[...] + ALPHA * dyb_ref[...]
        else:
            z_ref, g_ref, dya_ref, dz_ref, dzb_ref, dg_ref, db_ref = refs
            dy = dya_ref[...]
        zz = z_ref[...]
        mu = jnp.mean(zz, -1, keepdims=True)
        zc = zz - mu
        var = jnp.mean(zc * zc, -1, keepdims=True)
        r = lax.rsqrt(var + EPS)
        xh = zc * r
        dxh = dy * g_ref[...]
        dz = r * (dxh - jnp.mean(dxh, -1, keepdims=True) - xh * jnp.mean(dxh * xh, -1, keepdims=True))
        dz_ref[...] = dz
        dzb_ref[...] = dz.astype(BF16)

        @pl.when(pl.program_id(0) == 0)
        def _():
            dg_ref[...] = jnp.zeros_like(dg_ref)
            db_ref[...] = jnp.zeros_like(db_ref)

        dg_ref[...] += jnp.sum(dy * xh, 0, keepdims=True)
        db_ref[...] += jnp.sum(dy, 0, keepdims=True)

    row = pl.BlockSpec((LN_ROWS, d), lambda i: (i, 0))
    vec = pl.BlockSpec((1, d), lambda i: (0, 0))
    ins = [z, g, dya] + ([dyb] if two else [])
    return pl.pallas_call(
        body, grid=(t // LN_ROWS,), in_specs=[row, vec, row] + ([row] if two else []),
        out_specs=[row, row, vec, vec],
        out_shape=[jax.ShapeDtypeStruct((t, d), F32), jax.ShapeDtypeStruct((t, d), BF16),
                   jax.ShapeDtypeStruct((1, d), F32), jax.ShapeDtypeStruct((1, d), F32)],
        name=name, compiler_params=_cp())(*ins)


def _ln_rows(k):
    return 256 if k > 4096 else 512


def _mm_ln_fwd(a, w, x, g, b, *, name, dep=None):
    t, k = a.shape
    d = w.shape[1]
    tm = _ln_rows(k)

    def body(a_ref, w_ref, x_ref, g_ref, b_ref, *rest):
        z_ref, y_ref, yb_ref = rest[-3:]
        z = ALPHA * x_ref[...] + _nn(a_ref[...], w_ref[...])
        mu = jnp.mean(z, -1, keepdims=True)
        zc = z - mu
        var = jnp.mean(zc * zc, -1, keepdims=True)
        y = zc * lax.rsqrt(var + EPS) * g_ref[...] + b_ref[...]
        z_ref[...] = z
        y_ref[...] = y
        yb_ref[...] = y.astype(BF16)

    row = pl.BlockSpec((tm, d), lambda i: (i, 0))
    vec = pl.BlockSpec((1, d), lambda i: (0, 0))
    extra = [] if dep is None else [dep]
    return pl.pallas_call(
        body, grid=(t // tm,),
        in_specs=[pl.BlockSpec((tm, k), lambda i: (i, 0)), pl.BlockSpec((k, d), lambda i: (0, 0)), row, vec, vec]
        + [pl.BlockSpec(memory_space=pl.ANY)] * len(extra),
        out_specs=[row, row, row],
        out_shape=[jax.ShapeDtypeStruct((t, d), F32), jax.ShapeDtypeStruct((t, d), F32), jax.ShapeDtypeStruct((t, d), BF16)],
        name=name, compiler_params=_cp())(a, w, x, g, b, *extra)


def _part_offsets(parts):
    offs, o = [], 0
    for p in parts:
        assert o % p.shape[1] == 0
        offs.append(o)
        o += p.shape[1]
    return offs, o


def _mm_ln_bwd(parts, w, z, g, dyb, *, name, dep=None):
    t = parts[0].shape[0]
    offs, k = _part_offsets(parts)
    d = w.shape[1]
    tm = _ln_rows(k)
    npart = len(parts)

    def body(*refs):
        a_refs, w_refs = refs[:npart], refs[npart:2 * npart]
        z_ref, g_ref, dyb_ref = refs[2 * npart:2 * npart + 3]
        dz_ref, dzb_ref, dg_ref, db_ref = refs[-4:]
        dy = ALPHA * dyb_ref[...]
        for a_ref, w_ref in zip(a_refs, w_refs):
            dy = dy + _nn(a_ref[...], w_ref[...])
        zz = z_ref[...]
        mu = jnp.mean(zz, -1, keepdims=True)
        zc = zz - mu
        var = jnp.mean(zc * zc, -1, keepdims=True)
        r = lax.rsqrt(var + EPS)
        xh = zc * r
        dxh = dy * g_ref[...]
        dz = r * (dxh - jnp.mean(dxh, -1, keepdims=True) - xh * jnp.mean(dxh * xh, -1, keepdims=True))
        dz_ref[...] = dz
        dzb_ref[...] = dz.astype(BF16)

        @pl.when(pl.program_id(0) == 0)
        def _():
            dg_ref[...] = jnp.zeros_like(dg_ref)
            db_ref[...] = jnp.zeros_like(db_ref)

        dg_ref[...] += jnp.sum(dy * xh, 0, keepdims=True)
        db_ref[...] += jnp.sum(dy, 0, keepdims=True)

    row = pl.BlockSpec((tm, d), lambda i: (i, 0))
    vec = pl.BlockSpec((1, d), lambda i: (0, 0))
    extra = [] if dep is None else [dep]
    a_specs = [pl.BlockSpec((tm, p.shape[1]), lambda i: (i, 0)) for p in parts]
    w_specs = [pl.BlockSpec((p.shape[1], d), functools.partial(lambda i, blk: (blk, 0), blk=o // p.shape[1]))
               for p, o in zip(parts, offs)]
    return pl.pallas_call(
        body, grid=(t // tm,),
        in_specs=a_specs + w_specs + [row, vec, row] + [pl.BlockSpec(memory_space=pl.ANY)] * len(extra),
        out_specs=[row, row, vec, vec],
        out_shape=[jax.ShapeDtypeStruct((t, d), F32), jax.ShapeDtypeStruct((t, d), BF16),
                   jax.ShapeDtypeStruct((1, d), F32), jax.ShapeDtypeStruct((1, d), F32)],
        name=name, compiler_params=_cp())(*parts, *([w] * npart), z, g, dyb, *extra)


def _mm_tn_parts(parts, b, *, name):
    t, n = b.shape
    offs, m = _part_offsets(parts)
    tm = min(_tile(p.shape[1], 1408 if p.shape[1] > 2048 else 512) for p in parts)
    assert all(p.shape[1] % tm == 0 for p in parts)
    first = [o // tm for o in offs]
    count = [p.shape[1] // tm for p in parts]
    npart = len(parts)

    def body(*refs):
        a_refs, b_ref, o_ref = refs[:npart], refs[npart], refs[npart + 1]
        i = pl.program_id(0)
        for a_ref, f, c in zip(a_refs, first, count):
            @pl.when((i >= f) & (i < f + c))
            def _(a_ref=a_ref):
                o_ref[...] = _tn(a_ref[...], b_ref[...]).astype(BF16)

    a_specs = [pl.BlockSpec((t, tm), functools.partial(lambda i, f, c: (0, jnp.clip(i - f, 0, c - 1)), f=f, c=c))
               for f, c in zip(first, count)]
    return pl.pallas_call(
        body, grid=(m // tm,), in_specs=a_specs + [pl.BlockSpec((t, n), lambda i: (0, 0))],
        out_specs=pl.BlockSpec((tm, n), lambda i: (i, 0)), out_shape=jax.ShapeDtypeStruct((m, n), BF16),
        name=name, compiler_params=_cp())(*parts, b)


def _axpy(a, b, *, name):
    t, d = a.shape

    def body(a_ref, b_ref, o_ref):
        o_ref[...] = a_ref[...] + ALPHA * b_ref[...]

    row = pl.BlockSpec((LN_ROWS, d), lambda i: (i, 0))
    return pl.pallas_call(body, grid=(t // LN_ROWS,), in_specs=[row, row], out_specs=row,
                          out_shape=jax.ShapeDtypeStruct((t, d), F32), name=name, compiler_params=_cp())(a, b)


def _loss_head(y, target, *, name):
    t, d = y.shape

    def body(y_ref, t_ref, dy_ref, l_ref):
        e = y_ref[...] - t_ref[...]
        dy_ref[...] = e * (1.0 / d)

        @pl.when(pl.program_id(0) == 0)
        def _():
            l_ref[...] = jnp.zeros_like(l_ref)

        l_ref[...] += jnp.zeros_like(l_ref) + 0.5 * jnp.sum(jnp.mean(e * e, -1, keepdims=True), 0, keepdims=True)

    row = pl.BlockSpec((LN_ROWS, d), lambda i: (i, 0))
    return pl.pallas_call(
        body, grid=(t // LN_ROWS,), in_specs=[row, row],
        out_specs=[row, pl.BlockSpec((1, LANES), lambda i: (0, 0))],
        out_shape=[jax.ShapeDtypeStruct((t, d), F32), jax.ShapeDtypeStruct((1, LANES), F32)],
        name=name, compiler_params=_cp())(y, target)


def _sig(x):
    return 1.0 / (1.0 + jnp.exp(-x))


def _silu(x):
    return x * _sig(x)


def _dsilu(x):
    s = _sig(x)
    return s * (1.0 + x * (1.0 - s))


def _shift_down(u, k, row):
    if k == 0:
        return u
    return jnp.where(row >= k, pltpu.roll(u, k, 0), 0.0)


def _shift_up(u, k, row):
    if k == 0:
        return u
    t = u.shape[0]
    return jnp.where(row < t - k, pltpu.roll(u, t - k, 0), 0.0)


def _dwconv(u, w_ref, row):
    kk = w_ref.shape[0]
    acc = None
    for j in range(kk):
        term = w_ref[j:j + 1, :] * _shift_down(u, kk - 1 - j, row)
        acc = term if acc is None else acc + term
    return acc


def _dwconv_bwd(u, w_ref, dc, row, dw_ref):
    kk = w_ref.shape[0]
    du = None
    for j in range(kk):
        term = w_ref[j:j + 1, :] * _shift_up(dc, kk - 1 - j, row)
        du = term if du is None else du + term
        dw_ref[j:j + 1, :] = jnp.sum(dc * _shift_down(u, kk - 1 - j, row), 0, keepdims=True)
    return du


CONV_ROWS = 256


def _rows(b):
    return pl.ds(pl.multiple_of(b * CONV_ROWS, CONV_ROWS), CONV_ROWS)


def _shifted_down(ref, b, k, row):
    cur = ref[_rows(b), :]
    if k == 0:
        return cur
    prev = jnp.where(b > 0, ref[_rows(jnp.maximum(b - 1, 0)), :], 0.0)
    return jnp.where(row >= k, pltpu.roll(cur, k, 0), pltpu.roll(prev, k, 0))


def _shifted_up(ref, b, k, row, nblk):
    cur = ref[_rows(b), :]
    if k == 0:
        return cur
    nxt = jnp.where(b < nblk - 1, ref[_rows(jnp.minimum(b + 1, nblk - 1)), :], 0.0)
    return jnp.where(row < CONV_ROWS - k, pltpu.roll(cur, CONV_ROWS - k, 0), pltpu.roll(nxt, CONV_ROWS - k, 0))


def _dwconv_blk(u_ref, w_ref, b, row):
    kk = w_ref.shape[0]
    views = [_shifted_down(u_ref, b, kk - 1 - j, row) for j in range(kk)]
    acc = None
    for j in range(kk):
        term = w_ref[j:j + 1, :] * views[j]
        acc = term if acc is None else acc + term
    return acc, views


def _dwconv_du_blk(dc_ref, w_ref, b, row, nblk):
    kk = w_ref.shape[0]
    du = None
    for j in range(kk):
        term = w_ref[j:j + 1, :] * _shifted_up(dc_ref, b, kk - 1 - j, row, nblk)
        du = term if du is None else du + term
    return du


FFN_TC = 256


def _ffn_up_mid(x, up_t, cw, cb, *, name, dep=None):
    t, d = x.shape
    nb = D_FF // FFN_TC

    def body(x_ref, ugt_ref, uvt_ref, wg_ref, wv_ref, bg_ref, bv_ref, *rest):
        ug_ref, uv_ref, a_ref = rest[-3:]
        xx = x_ref[...]
        row = lax.broadcasted_iota(jnp.int32, (t, FFN_TC), 0)
        ug = _nt(xx, ugt_ref[...])
        ug_ref[...] = ug
        uv = _nt(xx, uvt_ref[...])
        uv_ref[...] = uv
        cg = _dwconv(ug, wg_ref, row) + bg_ref[...]
        cv = _dwconv(uv, wv_ref, row) + bv_ref[...]
        a_ref[...] = (_silu(cg) * cv).astype(BF16)

    col = pl.BlockSpec((t, FFN_TC), lambda j: (0, j))
    wt = lambda off: pl.BlockSpec((FFN_TC, d), lambda j: (j + off, 0))
    wsp = lambda off: pl.BlockSpec((FFN_CONV, FFN_TC), lambda j: (0, j + off))
    bsp = lambda off: pl.BlockSpec((1, FFN_TC), lambda j: (0, j + off))
    extra = [] if dep is None else [dep]
    return pl.pallas_call(
        body, grid=(nb,),
        in_specs=[pl.BlockSpec((t, d), lambda j: (0, 0)), wt(0), wt(nb), wsp(0), wsp(nb), bsp(0), bsp(nb)]
        + [pl.BlockSpec(memory_space=pl.ANY)] * len(extra),
        out_specs=[col, col, col],
        out_shape=[jax.ShapeDtypeStruct((t, D_FF), F32), jax.ShapeDtypeStruct((t, D_FF), F32),
                   jax.ShapeDtypeStruct((t, D_FF), BF16)],
        name=name, compiler_params=_cp())(x, up_t, up_t, cw, cw, cb, cb, *extra)


def _ffn_mid_bwd(ug, uv, cw, cb, dz, down, *, name):
    t, d = dz.shape
    nb = D_FF // FFN_TC

    nblk = t // CONV_ROWS

    def body(ug_ref, uv_ref, wg_ref, wv_ref, bg_ref, bv_ref, dz_ref, dn_ref,
             dug_ref, duv_ref, dwg_ref, dwv_ref, dbg_ref, dbv_ref, da_ref, dcg_s, dcv_s):
        da_ref[...] = _nt(dz_ref[...], dn_ref[...])
        row = lax.broadcasted_iota(jnp.int32, (CONV_ROWS, FFN_TC), 0)
        zero = jnp.zeros((1, FFN_TC), F32)

        def first(b, acc):
            cg, ugs = _dwconv_blk(ug_ref, wg_ref, b, row)
            cv, uvs = _dwconv_blk(uv_ref, wv_ref, b, row)
            cg = cg + bg_ref[...]
            cv = cv + bv_ref[...]
            da_ = da_ref[_rows(b), :]
            dcv = da_ * _silu(cg)
            dcg = da_ * cv * _dsilu(cg)
            dcg_s[_rows(b), :] = dcg
            dcv_s[_rows(b), :] = dcv
            red = [jnp.sum(dcg * s, 0, keepdims=True) for s in ugs] + [jnp.sum(dcg, 0, keepdims=True)]
            red += [jnp.sum(dcv * s, 0, keepdims=True) for s in uvs] + [jnp.sum(dcv, 0, keepdims=True)]
            return tuple(a + r for a, r in zip(acc, red))

        acc = lax.fori_loop(0, nblk, first, (zero,) * (2 * FFN_CONV + 2))
        for j in range(FFN_CONV):
            dwg_ref[j:j + 1, :] = acc[j]
            dwv_ref[j:j + 1, :] = acc[FFN_CONV + 1 + j]
        dbg_ref[...] = acc[FFN_CONV]
        dbv_ref[...] = acc[2 * FFN_CONV + 1]

        def second(b, carry):
            dug_ref[_rows(b), :] = _dwconv_du_blk(dcg_s, wg_ref, b, row, nblk).astype(BF16)
            duv_ref[_rows(b), :] = _dwconv_du_blk(dcv_s, wv_ref, b, row, nblk).astype(BF16)
            return carry

        lax.fori_loop(0, nblk, second, 0)

    col = pl.BlockSpec((t, FFN_TC), lambda j: (0, j))
    wsp = lambda off: pl.BlockSpec((FFN_CONV, FFN_TC), lambda j: (0, j + off))
    bsp = lambda off: pl.BlockSpec((1, FFN_TC), lambda j: (0, j + off))
    outs = pl.pallas_call(
        body, grid=(nb,),
        in_specs=[col, col, wsp(0), wsp(nb), bsp(0), bsp(nb), pl.BlockSpec((t, d), lambda j: (0, 0)),
                  pl.BlockSpec((FFN_TC, d), lambda j: (j, 0))],
        out_specs=[col, col, wsp(0), wsp(0), bsp(0), bsp(0)],
        out_shape=[jax.ShapeDtypeStruct((t, D_FF), BF16), jax.ShapeDtypeStruct((t, D_FF), BF16),
                   jax.ShapeDtypeStruct((FFN_CONV, D_FF), F32), jax.ShapeDtypeStruct((FFN_CONV, D_FF), F32),
                   jax.ShapeDtypeStruct((1, D_FF), F32), jax.ShapeDtypeStruct((1, D_FF), F32)],
        scratch_shapes=[pltpu.VMEM((t, FFN_TC), F32), pltpu.VMEM((t, FFN_TC), F32), pltpu.VMEM((t, FFN_TC), F32)],
        name=name, compiler_params=_cp())(ug, uv, cw, cw, cb, cb, dz, down)
    dug, duv, dwg, dwv, dbg, dbv = outs
    return [dug, duv], jnp.concatenate([dwg, dwv], 1), jnp.concatenate([dbg, dbv], 1)


def _rot_a(x, c2, s2):
    return x * c2 + pltpu.roll(x, RET_DK // 2, 1) * s2


def _rot_a_t(dy, c2, s2):
    return dy * c2 + pltpu.roll(dy * s2, RET_DK // 2, 1)


RET_BWD_BLK = 512


def _decay_tile(lg, blk_diff, blk=ATT_BLK):
    r = lax.broadcasted_iota(jnp.int32, (blk, blk), 0)
    c = lax.broadcasted_iota(jnp.int32, (blk, blk), 1)
    rel = r - c + blk_diff * blk
    return jnp.where(rel >= 0, jnp.exp(jnp.maximum(rel, 0).astype(F32) * lg), 0.0)


def _nt(a, b):
    return lax.dot_general(a, b, (((1,), (1,)), ((), ())), preferred_element_type=F32)


def _nn(a, b):
    return lax.dot_general(a, b, (((1,), (0,)), ((), ())), preferred_element_type=F32)


def _tn(a, b):
    return lax.dot_general(a, b, (((0,), (0,)), ((), ())), preferred_element_type=F32)


def _ret_specs(t):
    q = pl.BlockSpec((t, RET_DK), lambda h: (0, h))
    k = pl.BlockSpec((t, RET_DK), lambda h: (0, RET_HEADS + h))
    v = pl.BlockSpec((t, RET_DV), lambda h: (0, RET_HEADS + h))
    g = pl.BlockSpec((t, RET_DV), lambda h: (0, 2 * RET_HEADS + h))
    tab = pl.BlockSpec((t, RET_DK), lambda h: (0, 0))
    lg = pl.BlockSpec((1, 1, LANES), lambda h: (h, 0, 0))
    return q, k, v, g, tab, lg


def _ret_fwd(h, c2, s2, lgt, *, name):
    t = h.shape[0]
    nblk = t // ATT_BLK
    scale = RET_DK ** -0.5

    def body(q_ref, k_ref, v_ref, g_ref, c_ref, s_ref, lg_ref, o_ref, ya_ref, qs, ks, vs):
        c2_, s2_ = c_ref[...], s_ref[...]
        qs[...] = _rot_a(q_ref[...], c2_, s2_).astype(BF16)
        ks[...] = (_rot_a(k_ref[...], c2_, s2_) * scale).astype(BF16)
        vs[...] = v_ref[...].astype(BF16)
        lg = lg_ref[0, :, 0:1]
        for i in range(nblk):
            qi = qs[pl.ds(i * ATT_BLK, ATT_BLK), :]
            acc = jnp.zeros((ATT_BLK, RET_DV), F32)
            for j in range(i + 1):
                sl = pl.ds(j * ATT_BLK, ATT_BLK)
                s = _nt(qi, ks[sl, :]) * _decay_tile(lg, i - j)
                acc = acc + _nn(s.astype(BF16), vs[sl, :])
            rows = pl.ds(i * ATT_BLK, ATT_BLK)
            o_ref[rows, :] = acc
            r = lax.rsqrt(jnp.mean(acc * acc, -1, keepdims=True) + EPS)
            ya_ref[rows, :] = (acc * r * _silu(g_ref[rows, :])).astype(BF16)

    q, k, v, g, tab, lg = _ret_specs(t)
    out = pl.BlockSpec((t, RET_DV), lambda hh: (0, hh))
    return pl.pallas_call(
        body, grid=(RET_HEADS,), in_specs=[q, k, v, g, tab, tab, lg], out_specs=[out, out],
        out_shape=[jax.ShapeDtypeStruct((t, RET_V_W), F32), jax.ShapeDtypeStruct((t, RET_V_W), BF16)],
        scratch_shapes=[pltpu.VMEM((t, RET_DK), BF16), pltpu.VMEM((t, RET_DK), BF16), pltpu.VMEM((t, RET_DV), BF16)],
        name=name, compiler_params=_cp())(h, h, h, h, c2, s2, lgt)


def _ret_bwd(h, c2, s2, lgt, o, dy, *, name):
    t = h.shape[0]
    blk = RET_BWD_BLK
    nblk = t // blk
    scale = RET_DK ** -0.5

    def body(q_ref, k_ref, v_ref, g_ref, c_ref, s_ref, lg_ref, o_ref, dy_ref,
             dq_ref, dk_ref, dv_ref, dg_ref, qs, ks, vs, dos, dka, dva):
        c2_, s2_ = c_ref[...], s_ref[...]
        qs[...] = _rot_a(q_ref[...], c2_, s2_).astype(BF16)
        ks[...] = (_rot_a(k_ref[...], c2_, s2_) * scale).astype(BF16)
        vs[...] = v_ref[...].astype(BF16)
        lg = lg_ref[0, :, 0:1]
        oo = o_ref[...]
        gg = g_ref[...]
        dya = dy_ref[...]
        r = lax.rsqrt(jnp.mean(oo * oo, -1, keepdims=True) + EPS)
        rn = oo * r
        dg_ref[...] = (dya * rn * _dsilu(gg)).astype(BF16)
        drn = dya * _silu(gg)
        dos[...] = (r * (drn - rn * jnp.mean(drn * rn, -1, keepdims=True))).astype(BF16)
        dka[...] = jnp.zeros_like(dka)
        dva[...] = jnp.zeros_like(dva)
        for i in range(nblk):
            rows = pl.ds(i * blk, blk)
            qi = qs[rows, :]
            doi = dos[rows, :]
            dqa = jnp.zeros((blk, RET_DK), F32)
            for j in range(i + 1):
                sl = pl.ds(j * blk, blk)
                dt_ = _decay_tile(lg, i - j, blk)
                kj = ks[sl, :]
                s = (_nt(qi, kj) * dt_).astype(BF16)
                ds = (_nt(doi, vs[sl, :]) * dt_).astype(BF16)
                dqa = dqa + _nn(ds, kj)
                dka[sl, :] += _tn(ds, qi)
                dva[sl, :] += _tn(s, doi)
            dq_ref[rows, :] = _rot_a_t(dqa, c_ref[rows, :], s_ref[rows, :]).astype(BF16)
        dk_ref[...] = (_rot_a_t(dka[...], c2_, s2_) * scale).astype(BF16)
        dv_ref[...] = dva[...].astype(BF16)

    q, k, v, g, tab, lg = _ret_specs(t)
    blk_v = pl.BlockSpec((t, RET_DV), lambda hh: (0, hh))
    blk_k = pl.BlockSpec((t, RET_DK), lambda hh: (0, hh))
    return pl.pallas_call(
        body, grid=(RET_HEADS,), in_specs=[q, k, v, g, tab, tab, lg, blk_v, blk_v],
        out_specs=[blk_k, blk_k, blk_v, blk_v],
        out_shape=[jax.ShapeDtypeStruct((t, RET_QK_W), BF16), jax.ShapeDtypeStruct((t, RET_QK_W), BF16),
                   jax.ShapeDtypeStruct((t, RET_V_W), BF16), jax.ShapeDtypeStruct((t, RET_V_W), BF16)],
        scratch_shapes=[pltpu.VMEM((t, RET_DK), BF16), pltpu.VMEM((t, RET_DK), BF16), pltpu.VMEM((t, RET_DV), BF16),
                        pltpu.VMEM((t, RET_DV), BF16), pltpu.VMEM((t, RET_DK), F32), pltpu.VMEM((t, RET_DV), F32)],
        name=name, compiler_params=_cp())(h, h, h, h, c2, s2, lgt, o, dy)


def _rot_b(x, cb, shi, slo):
    return x * cb + pltpu.roll(x, ROPE_DIMS // 2, 1) * shi + pltpu.roll(x, LANES - ROPE_DIMS // 2, 1) * slo


def _rot_b_t(dy, cb, shi, slo):
    return dy * cb + pltpu.roll(dy * shi, LANES - ROPE_DIMS // 2, 1) + pltpu.roll(dy * slo, ROPE_DIMS // 2, 1)


def _dil_specs(t):
    base = (2 * RET_QK_W + 2 * RET_V_W) // LANES
    npair = DIL_W // LANES
    q = pl.BlockSpec((t, LANES), lambda p: (0, base + p))
    k = pl.BlockSpec((t, LANES), lambda p: (0, base + npair + p))
    v = pl.BlockSpec((t, LANES), lambda p: (0, base + 2 * npair + p))
    tab = pl.BlockSpec((t, LANES), lambda p: (0, 0))
    strip = pl.BlockSpec((ATT_BLK, t), lambda p: (0, 0))
    pair = pl.BlockSpec((t, LANES), lambda p: (0, p))
    return q, k, v, tab, strip, pair


def _dil_fwd(h, cb, shi, slo, strip, *, name):
    t = h.shape[0]
    nblk = t // ATT_BLK
    scale = DIL_HD ** -0.5

    def body(q_ref, k_ref, v_ref, cb_ref, shi_ref, slo_ref, st_ref, o_ref, yb_ref, lse_ref, qs, ks, vs):
        cb_, shi_, slo_ = cb_ref[...], shi_ref[...], slo_ref[...]
        lane = lax.broadcasted_iota(jnp.int32, (t, LANES), 1)
        qr = _rot_b(q_ref[...], cb_, shi_, slo_) * scale
        qs[0] = jnp.where(lane < DIL_HD, qr, 0.0).astype(BF16)
        qs[1] = jnp.where(lane >= DIL_HD, qr, 0.0).astype(BF16)
        ks[...] = _rot_b(k_ref[...], cb_, shi_, slo_).astype(BF16)
        vs[...] = v_ref[...].astype(BF16)
        lane_b = lax.broadcasted_iota(jnp.int32, (ATT_BLK, LANES), 1)
        for i in range(nblk):
            w = (i + 1) * ATT_BLK
            rows = pl.ds(i * ATT_BLK, ATT_BLK)
            logc = st_ref[:, t - w:t]
            outs, lses = [], []
            for hd in range(2):
                s = _nt(qs[hd, rows, :], ks[0:w, :]) + logc
                m = jnp.max(s, -1, keepdims=True)
                p = jnp.exp(s - m)
                l = jnp.sum(p, -1, keepdims=True)
                outs.append(_nn(p.astype(BF16), vs[0:w, :]) / l)
                lses.append(m + jnp.log(l))
            o = jnp.where(lane_b < DIL_HD, outs[0], outs[1])
            o_ref[rows, :] = o
            yb_ref[rows, :] = o.astype(BF16)
            lse_ref[rows, :] = jnp.where(lane_b < DIL_HD, lses[0], lses[1])

    q, k, v, tab, strip_spec, pair = _dil_specs(t)
    return pl.pallas_call(
        body, grid=(DIL_W // LANES,), in_specs=[q, k, v, tab, tab, tab, strip_spec], out_specs=[pair, pair, pair],
        out_shape=[jax.ShapeDtypeStruct((t, DIL_W), F32), jax.ShapeDtypeStruct((t, DIL_W), BF16),
                   jax.ShapeDtypeStruct((t, DIL_W), F32)],
        scratch_shapes=[pltpu.VMEM((2, t, LANES), BF16), pltpu.VMEM((t, LANES), BF16), pltpu.VMEM((t, LANES), BF16)],
        name=name, compiler_params=_cp())(h, h, h, cb, shi, slo, strip)


def _dil_bwd(h, cb, shi, slo, strip, o, lse, dy, *, name):
    t = h.shape[0]
    nblk = t // ATT_BLK
    scale = DIL_HD ** -0.5

    def body(q_ref, k_ref, v_ref, cb_ref, shi_ref, slo_ref, st_ref, o_ref, lse_ref, dy_ref,
             dq_ref, dk_ref, dv_ref, qs, ks, vs, dos, dls, dka, dva):
        cb_, shi_, slo_ = cb_ref[...], shi_ref[...], slo_ref[...]
        lane = lax.broadcasted_iota(jnp.int32, (t, LANES), 1)
        qr = _rot_b(q_ref[...], cb_, shi_, slo_) * scale
        qs[0] = jnp.where(lane < DIL_HD, qr, 0.0).astype(BF16)
        qs[1] = jnp.where(lane >= DIL_HD, qr, 0.0).astype(BF16)
        ks[...] = _rot_b(k_ref[...], cb_, shi_, slo_).astype(BF16)
        vs[...] = v_ref[...].astype(BF16)
        do = dy_ref[...]
        prod = do * o_ref[...]
        d0 = jnp.sum(jnp.where(lane < DIL_HD, prod, 0.0), -1, keepdims=True)
        d1 = jnp.sum(jnp.where(lane >= DIL_HD, prod, 0.0), -1, keepdims=True)
        dls[...] = jnp.where(lane < DIL_HD, d0, d1)
        dos[0] = jnp.where(lane < DIL_HD, do, 0.0).astype(BF16)
        dos[1] = jnp.where(lane >= DIL_HD, do, 0.0).astype(BF16)
        dka[...] = jnp.zeros_like(dka)
        dva[...] = jnp.zeros_like(dva)
        lane_b = lax.broadcasted_iota(jnp.int32, (ATT_BLK, LANES), 1)
        for i in range(nblk):
            w = (i + 1) * ATT_BLK
            rows = pl.ds(i * ATT_BLK, ATT_BLK)
            logc = st_ref[:, t - w:t]
            dqs = []
            for hd in range(2):
                col = hd * DIL_HD
                qh = qs[hd, rows, :]
                doh = dos[hd, rows, :]
                lse_h = lse_ref[rows, col:col + 1]
                dl_h = dls[rows, col:col + 1]
                p = jnp.exp(_nt(qh, ks[0:w, :]) + logc - lse_h)
                dp = _nt(doh, vs[0:w, :])
                ds = (p * (dp - dl_h)).astype(BF16)
                dqs.append(_nn(ds, ks[0:w, :]))
                dka[0:w, :] += _tn(ds, qh)
                dva[0:w, :] += _tn(p.astype(BF16), doh)
            dq = jnp.where(lane_b < DIL_HD, dqs[0], dqs[1]) * scale
            dq_ref[rows, :] = _rot_b_t(dq, cb_ref[rows, :], shi_ref[rows, :], slo_ref[rows, :]).astype(BF16)
        dk_ref[...] = _rot_b_t(dka[...], cb_, shi_, slo_).astype(BF16)
        dv_ref[...] = dva[...].astype(BF16)

    q, k, v, tab, strip_spec, pair = _dil_specs(t)
    dy_spec = pl.BlockSpec((t, LANES), lambda p: (0, RET_V_W // LANES + p))
    return pl.pallas_call(
        body, grid=(DIL_W // LANES,), in_specs=[q, k, v, tab, tab, tab, strip_spec, pair, pair, dy_spec],
        out_specs=[pair, pair, pair],
        out_shape=[jax.ShapeDtypeStruct((t, DIL_W), BF16)] * 3,
        scratch_shapes=[pltpu.VMEM((2, t, LANES), BF16), pltpu.VMEM((t, LANES), BF16), pltpu.VMEM((t, LANES), BF16),
                        pltpu.VMEM((2, t, LANES), BF16), pltpu.VMEM((t, LANES), F32),
                        pltpu.VMEM((t, LANES), F32), pltpu.VMEM((t, LANES), F32)],
        name=name, compiler_params=_cp())(h, h, h, cb, shi, slo, strip, o, lse, dy)


def _gdn_prep_fwd(h, cw, *, name):
    t = h.shape[0]
    qscale = GDN_DK ** -0.5

    def body(hq_ref, hk_ref, hv_ref, wq_ref, wk_ref, wv_ref, q_ref, k_ref, v_ref):
        row = lax.broadcasted_iota(jnp.int32, (t, GDN_DK), 0)
        sq = _silu(_dwconv(hq_ref[...], wq_ref, row))
        sk = _silu(_dwconv(hk_ref[...], wk_ref, row))
        q_ref[0] = sq * lax.rsqrt(jnp.sum(sq * sq, -1, keepdims=True) + 1e-6) * qscale
        k_ref[0] = sk * lax.rsqrt(jnp.sum(sk * sk, -1, keepdims=True) + 1e-6)
        v_ref[0] = _silu(_dwconv(hv_ref[...], wv_ref, row))

    hs = lambda off: pl.BlockSpec((t, GDN_DK), lambda i: (0, i + off))
    ws = lambda off: pl.BlockSpec((GDN_CONV, GDN_DK), lambda i: (0, i + off))
    out = pl.BlockSpec((1, t, GDN_DK), lambda i: (i, 0, 0))
    return pl.pallas_call(
        body, grid=(GDN_HEADS,), in_specs=[hs(0), hs(8), hs(16), ws(0), ws(8), ws(16)], out_specs=[out, out, out],
        out_shape=[jax.ShapeDtypeStruct((GDN_HEADS, t, GDN_DK), F32)] * 3,
        name=name, compiler_params=_cp())(h, h, h, cw, cw, cw)


def _gdn_prep_bwd(h, cw, dq, dk, dv, *, name):
    t = h.shape[0]
    qscale = GDN_DK ** -0.5

    def body(hq_ref, hk_ref, hv_ref, wq_ref, wk_ref, wv_ref, dq_ref, dk_ref, dv_ref,
             dhq_ref, dhk_ref, dhv_ref, dwq_ref, dwk_ref, dwv_ref):
        row = lax.broadcasted_iota(jnp.int32, (t, GDN_DK), 0)

        def one(h_ref, w_ref, d_ref, dh_ref, dw_ref, norm, sc):
            u = h_ref[...]
            c = _dwconv(u, w_ref, row)
            d = d_ref[0]
            if norm:
                s = _silu(c)
                r = lax.rsqrt(jnp.sum(s * s, -1, keepdims=True) + 1e-6)
                n = s * r
                d = d * sc
                d = r * (d - n * jnp.sum(d * n, -1, keepdims=True))
            dc = d * _dsilu(c)
            dh_ref[...] = _dwconv_bwd(u, w_ref, dc, row, dw_ref).astype(BF16)

        one(hq_ref, wq_ref, dq_ref, dhq_ref, dwq_ref, True, qscale)
        one(hk_ref, wk_ref, dk_ref, dhk_ref, dwk_ref, True, 1.0)
        one(hv_ref, wv_ref, dv_ref, dhv_ref, dwv_ref, False, 1.0)

    hs = lambda off: pl.BlockSpec((t, GDN_DK), lambda i: (0, i + off))
    ws = lambda off: pl.BlockSpec((GDN_CONV, GDN_DK), lambda i: (0, i + off))
    hd = pl.BlockSpec((1, t, GDN_DK), lambda i: (i, 0, 0))
    return pl.pallas_call(
        body, grid=(GDN_HEADS,), in_specs=[hs(0), hs(8), hs(16), ws(0), ws(8), ws(16), hd, hd, hd],
        out_specs=[hs(0), hs(0), hs(0), ws(0), ws(0), ws(0)],
        out_shape=[jax.ShapeDtypeStruct((t, GDN_W), BF16)] * 3 + [jax.ShapeDtypeStruct((GDN_CONV, GDN_W), F32)] * 3,
        name=name, compiler_params=_cp())(h, h, h, cw, cw, cw, dq, dk, dv)


def _make_mm2(wide):
    def raw(a, b, dims):
        if wide:
            return lax.dot_general(a, b, (dims, ((), ())), precision=lax.Precision.HIGHEST, preferred_element_type=F32)
        return lax.dot_general(a.astype(BF16), b.astype(BF16), (dims, ((), ())), preferred_element_type=F32)

    @jax.custom_vjp
    def nn(a, b):
        return raw(a, b, ((1,), (0,)))

    @jax.custom_vjp
    def nt(a, b):
        return raw(a, b, ((1,), (1,)))

    @jax.custom_vjp
    def tn(a, b):
        return raw(a, b, ((0,), (0,)))

    nn.defvjp(lambda a, b: (nn(a, b), (a, b)), lambda r, g: (nt(g, r[1]), tn(r[0], g)))
    nt.defvjp(lambda a, b: (nt(a, b), (a, b)), lambda r, g: (nn(g, r[1]), tn(g, r[0])))
    tn.defvjp(lambda a, b: (tn(a, b), (a, b)), lambda r, g: (nt(r[1], g), nn(r[0], g)))
    return nn, nt, tn


_NN, _NT, _TN = _make_mm2(False)
_NNW, _NTW, _TNW = _make_mm2(True)


def _square_masks(c):
    ri = lax.broadcasted_iota(jnp.int32, (c, c), 0)
    ci = lax.broadcasted_iota(jnp.int32, (c, c), 1)
    return ri >= ci, ri > ci, ri == ci


def _cumsum_rows(m):
    tri, _, _ = _square_masks(m.shape[0])
    return _NNW(tri.astype(F32), m)


def _transpose_sq(m):
    _, _, eye = _square_masks(m.shape[0])
    return _NTW(eye.astype(F32), m)


@jax.custom_vjp
def _inv_unit_lower(l):
    c = l.shape[0]
    _, _, eye = _square_masks(c)
    p = -l
    t = eye.astype(F32) + p
    for _ in range(int(math.log2(c)) - 1):
        p = _NNW(p, p)
        t = t + _NNW(t, p)
    return t


def _inv_fwd(l):
    t = _inv_unit_lower(l)
    return t, t


def _inv_bwd(t, dt):
    return (-_NTW(_TNW(t, dt), t),)


_inv_unit_lower.defvjp(_inv_fwd, _inv_bwd)


@jax.custom_vjp
def _inv_known(l, t):
    return t


_inv_known.defvjp(lambda l, t: (t, t), lambda t, dt: (_inv_bwd(t, dt)[0], jnp.zeros_like(t)))


def _softplus(x):
    return jnp.maximum(x, 0.0) + jnp.log1p(jnp.exp(-jnp.abs(x)))


def _gdn_chunk(q, k, v, braw, araw, alog, dtb, state, inv=None):
    c = q.shape[0]
    dv = v.shape[1]
    tri, strict, _ = _square_masks(c)
    beta = _sig(braw)
    g = -jnp.exp(alog) * _softplus(araw + dtb)
    gcm = _cumsum_rows(g * jnp.ones((c, c), F32))
    gct = _transpose_sq(gcm)
    decay = jnp.where(tri, jnp.exp(jnp.where(tri, gcm - gct, 0.0)), 0.0)
    gc = jnp.sum(gcm, 1, keepdims=True) * (1.0 / c)
    glast = jnp.sum(g, 0, keepdims=True)
    egc = jnp.exp(gc)
    kb = k * beta
    low = jnp.where(strict, _NT(kb, k) * decay, 0.0)
    tm = _inv_unit_lower(low) if inv is None else _inv_known(low, inv)
    sol = _NNW(tm, jnp.concatenate([v * beta, kb * egc], 1))
    u, w = sol[:, :dv], sol[:, dv:]
    attn = jnp.where(tri, _NT(q, k) * decay, 0.0)
    k_dec = k * jnp.exp(glast - gc)
    q_dec = q * egc
    v_new = u - _NN(w, state)
    o = _NN(q_dec, state) + _NN(attn, v_new)
    new_state = state * jnp.exp(glast) + _TN(k_dec, v_new)
    return o, new_state, tm


def _gdn_specs(t, rev):
    nch = t // GDN_CHUNK
    cm = (lambda n: nch - 1 - n) if rev else (lambda n: n)
    tok = pl.BlockSpec((GDN_HEADS, GDN_CHUNK, GDN_DK), lambda n: (0, cm(n), 0))
    par = pl.BlockSpec((GDN_HEADS, 1, LANES), lambda n: (0, 0, 0))
    st = pl.BlockSpec((GDN_HEADS, 1, GDN_DK, GDN_DV), lambda n: (0, cm(n), 0, 0))
    inv = pl.BlockSpec((GDN_HEADS, GDN_CHUNK, GDN_CHUNK), lambda n: (0, cm(n), 0))
    sc = pl.BlockSpec((GDN_CHUNK, LANES), lambda n: (cm(n), 4 * GDN_W // LANES))
    return tok, par, st, inv, sc


def _head_columns(sc_ref, first):
    return jnp.stack([sc_ref[:, first + hh:first + hh + 1] for hh in range(GDN_HEADS)])


def _gdn_core_fwd(q, k, v, h, alog, dtb, *, name):
    t = q.shape[1]
    nch = t // GDN_CHUNK

    def body(q_ref, k_ref, v_ref, sc_ref, al_ref, dt_ref, o_ref, st_ref, inv_ref, state):
        @pl.when(pl.program_id(0) == 0)
        def _():
            state[...] = jnp.zeros_like(state)

        s0 = state[...]
        st_ref[:, 0] = s0
        o, s1, tm = jax.vmap(_gdn_chunk)(q_ref[...], k_ref[...], v_ref[...], _head_columns(sc_ref, 0),
                                         _head_columns(sc_ref, GDN_HEADS), al_ref[:, :, 0:1], dt_ref[:, :, 0:1], s0)
        o_ref[...] = o
        inv_ref[...] = tm
        state[...] = s1

    tok, par, st, inv, sc = _gdn_specs(t, False)
    return pl.pallas_call(
        body, grid=(nch,), in_specs=[tok, tok, tok, sc, par, par], out_specs=[tok, st, inv],
        out_shape=[jax.ShapeDtypeStruct((GDN_HEADS, t, GDN_DV), F32),
                   jax.ShapeDtypeStruct((GDN_HEADS, nch, GDN_DK, GDN_DV), F32),
                   jax.ShapeDtypeStruct((GDN_HEADS, t, GDN_CHUNK), F32)],
        scratch_shapes=[pltpu.VMEM((GDN_HEADS, GDN_DK, GDN_DV), F32)],
        name=name, compiler_params=_cp())(q, k, v, h, alog, dtb)


def _gdn_core_bwd(q, k, v, h, alog, dtb, states, invs, do, *, name):
    t = q.shape[1]
    nch = t // GDN_CHUNK

    def body(q_ref, k_ref, v_ref, sc_ref, al_ref, dt_ref, st_ref, inv_ref, do_ref,
             dq_ref, dk_ref, dv_ref, dsc_ref, dal_ref, ddt_ref, dstate):
        @pl.when(pl.program_id(0) == 0)
        def _():
            dstate[...] = jnp.zeros_like(dstate)
            dal_ref[...] = jnp.zeros_like(dal_ref)
            ddt_ref[...] = jnp.zeros_like(ddt_ref)

        args = (q_ref[...], k_ref[...], v_ref[...], _head_columns(sc_ref, 0), _head_columns(sc_ref, GDN_HEADS),
                al_ref[:, :, 0:1], dt_ref[:, :, 0:1], st_ref[:, 0])
        tm = inv_ref[...]

        def chunk(*a):
            return jax.vmap(_gdn_chunk)(*a, tm)[:2]

        _, pull = jax.vjp(chunk, *args)
        dq, dk, dv, dbr, dar, dal, ddt, ds = pull((do_ref[...], dstate[...]))
        dq_ref[...] = dq
        dk_ref[...] = dk
        dv_ref[...] = dv
        lane = lax.broadcasted_iota(jnp.int32, (GDN_CHUNK, LANES), 1)
        dsc = jnp.zeros((GDN_CHUNK, LANES), F32)
        for hh in range(GDN_HEADS):
            dsc = jnp.where(lane == hh, dbr[hh], dsc)
            dsc = jnp.where(lane == GDN_HEADS + hh, dar[hh], dsc)
        dsc_ref[...] = dsc
        dal_ref[...] += dal + jnp.zeros((GDN_HEADS, 1, LANES), F32)
        ddt_ref[...] += ddt + jnp.zeros((GDN_HEADS, 1, LANES), F32)
        dstate[...] = ds

    tok, par, st, inv, sc = _gdn_specs(t, True)
    tokshape = jax.ShapeDtypeStruct((GDN_HEADS, t, GDN_DK), F32)
    parshape = jax.ShapeDtypeStruct((GDN_HEADS, 1, LANES), F32)
    nch_map = pl.BlockSpec((GDN_CHUNK, LANES), lambda n: (nch - 1 - n, 0))
    return pl.pallas_call(
        body, grid=(nch,), in_specs=[tok, tok, tok, sc, par, par, st, inv, tok],
        out_specs=[tok, tok, tok, nch_map, par, par],
        out_shape=[tokshape] * 3 + [jax.ShapeDtypeStruct((t, LANES), F32)] + [parshape] * 2,
        scratch_shapes=[pltpu.VMEM((GDN_HEADS, GDN_DK, GDN_DV), F32)],
        name=name, compiler_params=_cp())(q, k, v, h, alog, dtb, states, invs, do)


GDN_ROWS = 2048


def _gdn_post_fwd(o, h, nw, *, name):
    t = o.shape[1]

    def body(o_ref, g_ref, nw_ref, y_ref):
        oo = o_ref[0]
        r = lax.rsqrt(jnp.mean(oo * oo, -1, keepdims=True) + EPS)
        y_ref[...] = (oo * r * nw_ref[...] * _silu(g_ref[...])).astype(BF16)

    return pl.pallas_call(
        body, grid=(GDN_HEADS, t // GDN_ROWS),
        in_specs=[pl.BlockSpec((1, GDN_ROWS, GDN_DV), lambda hh, i: (hh, i, 0)),
                  pl.BlockSpec((GDN_ROWS, GDN_DV), lambda hh, i: (i, 3 * GDN_HEADS + hh)),
                  pl.BlockSpec((1, GDN_DV), lambda hh, i: (0, 0))],
        out_specs=pl.BlockSpec((GDN_ROWS, GDN_DV), lambda hh, i: (i, hh)),
        out_shape=jax.ShapeDtypeStruct((t, GDN_W), BF16), name=name, compiler_params=_cp())(o, h, nw)


def _gdn_post_bwd(o, h, nw, dy, *, name):
    t = o.shape[1]

    def body(o_ref, g_ref, nw_ref, dy_ref, do_ref, dg_ref, dnw_ref):
        oo, gg, nw_, dy_ = o_ref[0], g_ref[...], nw_ref[...], dy_ref[...]
        r = lax.rsqrt(jnp.mean(oo * oo, -1, keepdims=True) + EPS)
        n = oo * r
        sg = _silu(gg)
        dg_ref[...] = (dy_ * n * nw_ * _dsilu(gg)).astype(BF16)
        dn = dy_ * sg * nw_
        do_ref[0] = r * (dn - n * jnp.mean(dn * n, -1, keepdims=True))

        @pl.when((pl.program_id(0) == 0) & (pl.program_id(1) == 0))
        def _():
            dnw_ref[...] = jnp.zeros_like(dnw_ref)

        dnw_ref[...] += jnp.sum(dy_ * sg * n, 0, keepdims=True)

    return pl.pallas_call(
        body, grid=(GDN_HEADS, t // GDN_ROWS),
        in_specs=[pl.BlockSpec((1, GDN_ROWS, GDN_DV), lambda hh, i: (hh, i, 0)),
                  pl.BlockSpec((GDN_ROWS, GDN_DV), lambda hh, i: (i, 3 * GDN_HEADS + hh)),
                  pl.BlockSpec((1, GDN_DV), lambda hh, i: (0, 0)),
                  pl.BlockSpec((GDN_ROWS, GDN_DV), lambda hh, i: (i, hh))],
        out_specs=[pl.BlockSpec((1, GDN_ROWS, GDN_DV), lambda hh, i: (hh, i, 0)),
                   pl.BlockSpec((GDN_ROWS, GDN_DV), lambda hh, i: (i, hh)),
                   pl.BlockSpec((1, GDN_DV), lambda hh, i: (0, 0))],
        out_shape=[jax.ShapeDtypeStruct((GDN_HEADS, t, GDN_DV), F32), jax.ShapeDtypeStruct((t, GDN_W), BF16),
                   jax.ShapeDtypeStruct((1, GDN_DV), F32)],
        name=name, compiler_params=_cp())(o, h, nw, dy)


def _tables(positions):
    pos = positions.astype(F32)[:, None]
    half = RET_DK // 2
    inv = jnp.power(RET_THETA, -jnp.arange(half, dtype=F32) * 2.0 / RET_DK)
    ang = pos * inv
    cos, sin = jnp.cos(ang), jnp.sin(ang)
    c2a = jnp.concatenate([cos, cos], 1)
    s2a = jnp.concatenate([-sin, sin], 1)
    hb = ROPE_DIMS // 2
    invb = jnp.power(ROPE_THETA, -jnp.arange(hb, dtype=F32) * 2.0 / ROPE_DIMS)
    angb = pos * invb
    cosb, sinb = jnp.cos(angb), jnp.sin(angb)
    t = pos.shape[0]
    ones = jnp.ones((t, DIL_HD - ROPE_DIMS), F32)
    zeros = jnp.zeros((t, DIL_HD - ROPE_DIMS), F32)
    z8 = jnp.zeros((t, hb), F32)
    cb = jnp.concatenate([cosb, cosb, ones] * 2, 1)
    shi = jnp.concatenate([z8, sinb, zeros] * 2, 1)
    slo = jnp.concatenate([-sinb, z8, zeros] * 2, 1)
    lg = jnp.log1p(-jnp.power(2.0, -5.0 - jnp.arange(RET_HEADS, dtype=F32)))
    lgt = jnp.broadcast_to(lg[:, None, None], (RET_HEADS, 1, LANES))
    delta = jnp.arange(ATT_BLK, dtype=jnp.int32)[:, None] + (SEQ - ATT_BLK) - jnp.arange(SEQ, dtype=jnp.int32)[None, :]
    cnt = jnp.zeros(delta.shape, F32)
    for (w, d) in DIL_PAIRS:
        cnt = cnt + ((delta >= 0) & (delta <= w) & (delta % d == 0)).astype(F32)
    strip = jnp.where(cnt > 0, jnp.log(jnp.maximum(cnt, 1.0)), NEG)
    return c2a, s2a, cb, shi, slo, lgt, strip


def _local_step(x, tables, target, get_w, mid, put_g, small):
    c2a, s2a, cb, shi, slo, lgt, strip = tables
    t = x.shape[0]
    saved = []
    xf = x
    xb = x.astype(BF16)
    for layer in range(DEPTH):
        j = layer // 2
        L = f"L{layer}_"
        W, dep = get_w(layer, "mixer", xb)
        rec = {"x": xf, "xb": xb}
        if layer % 2 == 0:
            h = _mm(xb, W["in_t"], tb=True, name=L + "ev_in", dep=dep)
            ro, ya = _ret_fwd(h, c2a, s2a, lgt, name=L + "ret_fwd")
            do_, yb, lse = _dil_fwd(h, cb, shi, slo, strip, name=L + "dil_fwd")
            y = jnp.concatenate([ya, yb], 1)
            rec.update(h=h, ro=ro, dil_o=do_, lse=lse, y=y)
        else:
            h = _mm(xb, W["in_t"], tb=True, name=L + "od_in", dep=dep)
            cw = W["conv"]
            q, k, v = _gdn_prep_fwd(h, cw, name=L + "gdn_prep")
            alog = jnp.broadcast_to(small["od_a_log"][j][:, None, None], (GDN_HEADS, 1, LANES))
            dtb = jnp.broadcast_to(small["od_dt_bias"][j][:, None, None], (GDN_HEADS, 1, LANES))
            o, states, invs = _gdn_core_fwd(q, k, v, h, alog, dtb, name=L + "gdn_fwd")
            nw = small["od_norm_w"][j][None, :]
            y = _gdn_post_fwd(o, h, nw, name=L + "gdn_post")
            rec.update(h=h, q=q, k=k, v=v, alog=alog, dtb=dtb, states=states, invs=invs, o=o, y=y, nw=nw, cw=cw)
        z1, x1, x1b = _mm_ln_fwd(y, W["out"], xf, small["ln1_g"][layer][None], small["ln1_b"][layer][None],
                                 name=L + "out_ln1", dep=mid(layer, "mixer", y))
        rec["Wm"] = W
        W, dep = get_w(layer, "ffn", x1b)
        rec["Wf"] = W
        fcw = W["fconv"]
        fcb = small["ffn_conv_b"][layer][None]
        ug, uv, a = _ffn_up_mid(x1b, W["up_t"], fcw, fcb, name=L + "ffn_up_mid", dep=dep)
        z2, x2, x2b = _mm_ln_fwd(a, W["down"], x1, small["ln2_g"][layer][None], small["ln2_b"][layer][None],
                                 name=L + "down_ln2", dep=mid(layer, "ffn", a))
        rec.update(z1=z1, x1b=x1b, ug=ug, uv=uv, a=a, z2=z2, fcw=fcw, fcb=fcb)
        saved.append(rec)
        xf, xb = x2, x2b

    dy, lossv = _loss_head(xf, target, name="loss_head")
    loss = lossv[0, 0]

    gS = {n: [None] * small[n].shape[0] for n in small}
    below = None
    for layer in reversed(range(DEPTH)):
        j = layer // 2
        L = f"L{layer}_"
        rec = saved[layer]
        Wm, Wf = rec["Wm"], rec["Wf"]
        g = {}
        if below is None:
            dz2, dz2b, dg2, db2 = _ln_bwd(rec["z2"], small["ln2_g"][layer][None], dy, None, name=L + "ln2_bwd")
        else:
            dz2, dz2b, dg2, db2 = _mm_ln_bwd(below[0], below[1], rec["z2"], small["ln2_g"][layer][None], below[2],
                                             name=L + "ln2_bwd", dep=below[3])
        gS["ln2_g"][layer], gS["ln2_b"][layer] = dg2[0], db2[0]
        g["down"] = _mm(rec["a"], dz2b, ta=True, name=L + "ffn_down_dw", out_dtype=BF16)
        du, dcw, dcb = _ffn_mid_bwd(rec["ug"], rec["uv"], rec["fcw"], rec["fcb"], dz2b, Wf["down"], name=L + "ffn_mid_bwd")
        g["fconv"] = dcw.astype(BF16)
        gS["ffn_conv_b"][layer] = dcb[0]
        g["up_t"] = _mm_tn_parts(du, rec["x1b"], name=L + "ffn_up_dw")
        dep = put_g(layer, "ffn", g)
        dz1, dz1b, dg1, db1 = _mm_ln_bwd(du, Wf["up_t"], rec["z1"], small["ln1_g"][layer][None], dz2,
                                         name=L + "ln1_bwd", dep=dep)
        gS["ln1_g"][layer], gS["ln1_b"][layer] = dg1[0], db1[0]
        g = {}
        if layer % 2 == 0:
            g["out"] = _mm(rec["y"], dz1b, ta=True, name=L + "ev_out_dw", out_dtype=BF16)
            dyy = _mm(dz1b, Wm["out"], tb=True, name=L + "ev_out_dx")
            dqa, dka, dva, dga = _ret_bwd(rec["h"], c2a, s2a, lgt, rec["ro"], dyy, name=L + "ret_bwd")
            dqb, dkb, dvb = _dil_bwd(rec["h"], cb, shi, slo, strip, rec["dil_o"], rec["lse"], dyy, name=L + "dil_bwd")
            dh = [dqa, dka, dva, dga, dqb, dkb, dvb]
            g["in_t"] = _mm_tn_parts(dh, rec["xb"], name=L + "ev_in_dw")
            dep = put_g(layer, "mixer", g)
        else:
            g["out"] = _mm(rec["y"], dz1b, ta=True, name=L + "od_out_dw", out_dtype=BF16)
            dyy = _mm(dz1b, Wm["out"], tb=True, name=L + "od_out_dx")
            do, dgate, dnw = _gdn_post_bwd(rec["o"], rec["h"], rec["nw"], dyy, name=L + "gdn_post_bwd")
            gS["od_norm_w"][j] = dnw[0]
            dq, dk, dv, dsc, dal, ddt = _gdn_core_bwd(
                rec["q"], rec["k"], rec["v"], rec["h"], rec["alog"], rec["dtb"], rec["states"], rec["invs"], do,
                name=L + "gdn_bwd")
            gS["od_a_log"][j] = dal[:, 0, 0]
            gS["od_dt_bias"][j] = ddt[:, 0, 0]
            dhq, dhk, dhv, dwq, dwk, dwv = _gdn_prep_bwd(rec["h"], rec["cw"], dq, dk, dv, name=L + "gdn_prep_bwd")
            g["conv"] = jnp.concatenate([dwq, dwk, dwv], 1).astype(BF16)
            dh = [dhq, dhk, dhv, dgate, dsc.astype(BF16)]
            g["in_t"] = (_mm_tn_parts(dh[:4], rec["xb"], name=L + "od_in_dw"),
                         _mm(dh[4], rec["xb"], ta=True, name=L + "od_in_dw_logits", out_dtype=BF16))
            dep = put_g(layer, "mixer", g)
        below = (dh, Wm["in_t"], dz1, dep)
    grad_x = _axpy(_mm(jnp.concatenate(below[0], 1), below[1], name="L0_in_dx", dep=below[3]), below[2], name="grad_x")
    gS = {n: jnp.stack(v) for n, v in gS.items()}
    return loss, grad_x, gS


HBM = pl.BlockSpec(memory_space=pltpu.HBM)


def _me():
    return lax.axis_index("x"), lax.axis_index("y"), lax.axis_index("c")


def _my_index():
    x, y, c = _me()
    return 4 * x + 2 * y + c


SEM = pl.BlockSpec(memory_space=pltpu.SEMAPHORE)
ANY = pl.BlockSpec(memory_space=pl.ANY)
PLANS = {"scatter": (1, 2, 3, 4, 5, 6, 7), "spread": (1, 2, 4, 6), "relay": (2, 4, 6)}
SIBLING = 1


def _peer(kk):
    x, y, c = _me()
    return x ^ (kk >> 2), y ^ ((kk >> 1) & 1), c ^ (kk & 1)


def _peer_index(kk):
    px, py, pc = _peer(kk)
    return 4 * px + 2 * py + pc


def _job_copies(mode, srcs, lands, send_sems, recv_sems, incoming):
    myid = _my_index()
    plan = PLANS[mode]
    out = []
    for a in range(len(lands)):
        for idx, kk in enumerate(plan):
            if mode == "relay":
                to, src = _peer(SIBLING), lands[a].at[_peer_index(kk)]
                slot_there, slot_here = _peer_index(kk), _peer_index(kk ^ SIBLING)
            else:
                to, src = _peer(kk), (srcs[a] if mode == "spread" else srcs[a].at[_peer_index(kk)])
                slot_there, slot_here = myid, _peer_index(kk)
            sem = a * len(plan) + idx
            out.append(pltpu.make_async_remote_copy(
                src_ref=src, dst_ref=lands[a].at[slot_here if incoming else slot_there],
                send_sem=send_sems.at[sem], recv_sem=recv_sems.at[sem], device_id=to, device_id_type=MESH))
    return out


def _split_jobs(jobs, arrays):
    out, o = [], 0
    for (_, srcs, lands) in jobs:
        out.append((arrays[o:o + len(srcs)], arrays[o + len(srcs):o + len(srcs) + len(lands)]))
        o += len(srcs) + len(lands)
    return out


def _exchange_start(jobs, after, *, name):
    jobs = [(mode, list(srcs), [lax.empty((N_DEV, *s.shape) if mode == "spread" else s.shape, s.dtype) for s in srcs]
             if lands is None else list(lands)) for (mode, srcs, lands) in jobs]
    flat = [a for (_, srcs, lands) in jobs for a in (*srcs, *lands)]
    n, nj = len(flat), len(jobs)
    nsem = [len(PLANS[mode]) * len(lands) for (mode, _, lands) in jobs]

    def body(*refs):
        o = n + (0 if after is None else 1)
        sems, token = refs[o:o + 2 * nj], refs[o + 2 * nj + n]
        for ji, ((mode, _, _), (src, land)) in enumerate(zip(jobs, _split_jobs(jobs, refs[:n]))):
            for cp in _job_copies(mode, src, land, sems[2 * ji], sems[2 * ji + 1], False):
                cp.start()
        token[...] = jnp.zeros_like(token)

    outs = pl.pallas_call(
        body, name=name,
        out_shape=(*[pltpu.SemaphoreType.DMA((ns,)) for ns in nsem for _ in range(2)],
                   *[pltpu.HBM(a.shape, a.dtype) for a in flat], jax.ShapeDtypeStruct((8, LANES), F32)),
        in_specs=[HBM] * n + ([] if after is None else [ANY]),
        out_specs=(*[SEM] * (2 * nj), *[HBM] * n, pl.BlockSpec(memory_space=pltpu.VMEM)),
        input_output_aliases={i: 2 * nj + i for i in range(n)},
        compiler_params=pltpu.CompilerParams(has_side_effects=pltpu.SideEffectType.DATAFLOW_SIDE_EFFECTING),
    )(*[pltpu.with_memory_space_constraint(a, pltpu.HBM) for a in flat], *([] if after is None else [after]))
    thru = _split_jobs(jobs, list(outs[2 * nj:2 * nj + n]))
    started = [(mode, outs[2 * ji], outs[2 * ji + 1], src, land) for ji, ((mode, _, _), (src, land)) in enumerate(zip(jobs, thru))]
    return started, outs[2 * nj + n]


def _exchange_wait(started, after, *, name):
    jobs = [(mode, srcs, lands) for (mode, _, _, srcs, lands) in started]
    flat = [a for (_, srcs, lands) in jobs for a in (*srcs, *lands)]
    n, nj = len(flat), len(jobs)

    def body(*refs):
        sems = refs[n:n + 2 * nj]
        for ji, ((mode, _, _), (src, land)) in enumerate(zip(jobs, _split_jobs(jobs, refs[:n]))):
            for cp in _job_copies(mode, src, land, sems[2 * ji], sems[2 * ji + 1], True):
                cp.wait_send()
                cp.wait_recv()

    outs = pl.pallas_call(
        body, name=name, out_shape=tuple(pltpu.HBM(a.shape, a.dtype) for a in flat),
        in_specs=[HBM] * n + [SEM] * (2 * nj) + [ANY], out_specs=tuple([HBM] * n),
        input_output_aliases={i: i for i in range(n)},
        compiler_params=pltpu.CompilerParams(has_side_effects=pltpu.SideEffectType.DATAFLOW_SIDE_EFFECTING),
    )(*flat, *[s for (_, ss, rs, _, _) in started for s in (ss, rs)], after)
    return _split_jobs(jobs, list(outs))


def _sum8(land, stack, j, depth, *, name):
    _, rr, cc = land.shape
    tr = _row_tile(rr)

    def body(l_ref, *rest):
        o_ref = rest[-1]
        acc = l_ref[0].astype(F32)
        for d in range(1, N_DEV):
            acc = acc + l_ref[d].astype(F32)
        o_ref[0] = acc

    prev = [] if stack is None else [stack]
    return pl.pallas_call(
        body, grid=(rr // tr,),
        in_specs=[pl.BlockSpec((N_DEV, tr, cc), lambda i: (0, i, 0))] + [pl.BlockSpec(memory_space=pl.ANY)] * len(prev),
        out_specs=pl.BlockSpec((1, tr, cc), lambda i: (j, i, 0)), out_shape=jax.ShapeDtypeStruct((depth, rr, cc), F32),
        input_output_aliases={1: 0} if prev else {}, name=name, compiler_params=_cp())(land, *prev)


def _row_tile(rr):
    for cand in (512, 384, 256, 192, 176, 128, 64, 32, 16, 8):
        if rr % cand == 0:
            return cand
    return rr


def _small_exchange(vec, *, name):
    rr = vec.shape[0]

    def body(v_ref, o_ref, send_sems, recv_sems):
        x, y, c = _me()
        myid = 4 * x + 2 * y + c
        o_ref[myid] = v_ref[...]
        cps = []
        for kk in range(1, N_DEV):
            px, py, pc = x ^ (kk >> 2), y ^ ((kk >> 1) & 1), c ^ (kk & 1)
            cps.append(pltpu.make_async_remote_copy(
                src_ref=v_ref, dst_ref=o_ref.at[myid], send_sem=send_sems.at[kk], recv_sem=recv_sems.at[kk],
                device_id=(px, py, pc), device_id_type=MESH))
        for cp in cps:
            cp.start()
        for kk in range(1, N_DEV):
            px, py, pc = x ^ (kk >> 2), y ^ ((kk >> 1) & 1), c ^ (kk & 1)
            pltpu.make_async_remote_copy(
                src_ref=v_ref, dst_ref=o_ref.at[4 * px + 2 * py + pc], send_sem=send_sems.at[kk],
                recv_sem=recv_sems.at[kk], device_id=(px, py, pc), device_id_type=MESH).wait_recv()
        for cp in cps:
            cp.wait_send()

    return pl.pallas_call(
        body, in_specs=[pl.BlockSpec(memory_space=pltpu.VMEM)], out_specs=pl.BlockSpec(memory_space=pltpu.VMEM),
        out_shape=jax.ShapeDtypeStruct((N_DEV, rr, LANES), F32),
        scratch_shapes=[pltpu.SemaphoreType.DMA((N_DEV,)), pltpu.SemaphoreType.DMA((N_DEV,))],
        name=name, compiler_params=pltpu.CompilerParams(has_side_effects=True))(vec)


def _adam_math(w, g, m, v):
    m = ADAM_B1 * m + (1.0 - ADAM_B1) * g
    v = ADAM_B2 * v + (1.0 - ADAM_B2) * (g * g)
    m_hat = m / (1.0 - ADAM_B1 ** ADAM_STEP)
    v_hat = v / (1.0 - ADAM_B2 ** ADAM_STEP)
    delta = -ADAM_LR * (m_hat / (jnp.sqrt(v_hat) + ADAM_EPS) + ADAM_WD * w)
    return delta, m, v


def _adamw_sharded(w, m, v, g, *, name):
    ll, rr, cc = w.shape
    tr = _row_tile(rr)

    def body(w_ref, m_ref, v_ref, g_ref, d_ref, nm_ref, nv_ref):
        d, nm, nv = _adam_math(w_ref[...], g_ref[...], m_ref[...], v_ref[...])
        d_ref[...] = d
        nm_ref[...] = nm
        nv_ref[...] = nv

    blk = pl.BlockSpec((1, tr, cc), lambda l, i: (l, i, 0))
    sh = jax.ShapeDtypeStruct((ll, rr, cc), F32)
    return pl.pallas_call(
        body, grid=(ll, rr // tr), in_specs=[blk] * 4, out_specs=[blk] * 3, out_shape=[sh] * 3,
        name=name, compiler_params=_cp())(w, m, v, g)


def _adamw_small(w, m, v, gall, *, name):
    rr = w.shape[0]

    def body(w_ref, m_ref, v_ref, g_ref, go_ref, d_ref, nm_ref, nv_ref):
        g = g_ref[0]
        for kk in range(1, N_DEV):
            g = g + g_ref[kk]
        d, nm, nv = _adam_math(w_ref[...], g, m_ref[...], v_ref[...])
        go_ref[...] = g
        d_ref[...] = d
        nm_ref[...] = nm
        nv_ref[...] = nv

    sh = jax.ShapeDtypeStruct((rr, LANES), F32)
    return pl.pallas_call(body, out_shape=[sh] * 4, name=name, compiler_params=_cp())(w, m, v, gall)


SHARDED = ("ev_w_in", "ev_w_out", "od_w_in", "od_conv_w", "od_w_out", "ffn_w_up", "ffn_conv_w", "ffn_w_down")
SMALL = ("od_a_log", "od_dt_bias", "od_norm_w", "ffn_conv_b", "ln1_g", "ln1_b", "ln2_g", "ln2_b")
ALL_W = ("ev_w_in", "ev_w_out", "od_w_in", "od_conv_w", "od_a_log", "od_dt_bias", "od_norm_w", "od_w_out",
         "ffn_w_up", "ffn_conv_w", "ffn_conv_b", "ffn_w_down", "ln1_g", "ln1_b", "ln2_g", "ln2_b")


def _layer_items(layer):
    j = layer // 2
    if layer % 2 == 0:
        mixer = [("in_t", "ev_w_in", j, "colT"), ("out", "ev_w_out", j, "row")]
    else:
        mixer = [("in_t", "od_w_in", j, "colT"), ("conv", "od_conv_w", j, "colsmall"), ("out", "od_w_out", j, "row")]
    return mixer + [("up_t", "ffn_w_up", layer, "colT"), ("fconv", "ffn_conv_w", layer, "colsmall"),
                    ("down", "ffn_w_down", layer, "row")]


OD_SHARD = OD_IN // N_DEV
OD_SHARD_PAD = OD_IN_PAD // N_DEV


def _od_pack(g, *, name):
    d = g.shape[-1]

    def body(g_ref, o_ref):
        for n in range(N_DEV):
            o_ref[OD_SHARD * n:OD_SHARD * (n + 1), :] = g_ref[n, 0:OD_SHARD, :]
        o_ref[OD_IN:OD_IN_PAD, :] = jnp.zeros((OD_IN_PAD - OD_IN, d), g.dtype)

    return pl.pallas_call(body, out_shape=jax.ShapeDtypeStruct((OD_IN_PAD, d), g.dtype), name=name,
                          compiler_params=_cp())(g)


def _od_unpack(main, tail, *, name):
    d = main.shape[-1]
    split = main.shape[0]

    def body(m_ref, t_ref, o_ref):
        for n in range(N_DEV):
            lo, hi = OD_SHARD * n, OD_SHARD * (n + 1)
            from_main = min(hi, split) - lo
            o_ref[n, 0:from_main, :] = m_ref[lo:lo + from_main, :]
            if hi > split:
                o_ref[n, from_main:OD_SHARD, :] = t_ref[0:hi - split, :]
            o_ref[n, OD_SHARD:OD_SHARD_PAD, :] = jnp.zeros((OD_SHARD_PAD - OD_SHARD, d), main.dtype)

    return pl.pallas_call(body, out_shape=jax.ShapeDtypeStruct((N_DEV, OD_SHARD_PAD, d), main.dtype), name=name,
                          compiler_params=_cp())(main, tail)


def _to_send(kind, name, w, j):
    if kind == "colT":
        s = w[j].T.astype(BF16)
        return jnp.pad(s, ((0, OD_SHARD_PAD - OD_SHARD), (0, 0))) if name == "od_w_in" else s
    return w[j].astype(BF16) if kind == "row" else w[j]


def _from_gather(kind, name, g, tag):
    if kind == "colsmall":
        return jnp.transpose(g, (1, 0, 2)).reshape(g.shape[1], -1)
    if name == "od_w_in":
        return _od_pack(g, name=tag + "_pack")
    return g.reshape(-1, g.shape[-1])


def _by_owner(kind, name, gfull, tag):
    if kind == "colsmall":
        kk, c8 = gfull.shape
        return jnp.transpose(gfull.reshape(kk, N_DEV, c8 // N_DEV), (1, 0, 2))
    if name == "od_w_in":
        return _od_unpack(*gfull, name=tag + "_unpack")
    return gfull.reshape(N_DEV, gfull.shape[0] // N_DEV, gfull.shape[1])


def _pack_small(d):
    flat = jnp.concatenate([d[n].reshape(-1) for n in SMALL])
    pad = (-flat.shape[0]) % (8 * LANES)
    return jnp.pad(flat, (0, pad)).reshape(-1, LANES)


def _unpack_small(packed, like):
    flat = packed.reshape(-1)
    out, off = {}, 0
    for n in SMALL:
        sz = int(np.prod(like[n].shape))
        out[n] = flat[off:off + sz].reshape(like[n].shape)
        off += sz
    return out


def kernel(x, positions, ev_w_in, ev_w_out, od_w_in, od_conv_w, od_a_log, od_dt_bias, od_norm_w, od_w_out, ffn_w_up, ffn_conv_w, ffn_conv_b, ffn_w_down, ln1_g, ln1_b, ln2_g, ln2_b, loss_target, m_ev_w_in, m_ev_w_out, m_od_w_in, m_od_conv_w, m_od_a_log, m_od_dt_bias, m_od_norm_w, m_od_w_out, m_ffn_w_up, m_ffn_conv_w, m_ffn_conv_b, m_ffn_w_down, m_ln1_g, m_ln1_b, m_ln2_g, m_ln2_b, v_ev_w_in, v_ev_w_out, v_od_w_in, v_od_conv_w, v_od_a_log, v_od_dt_bias, v_od_norm_w, v_od_w_out, v_ffn_w_up, v_ffn_conv_w, v_ffn_conv_b, v_ffn_w_down, v_ln1_g, v_ln1_b, v_ln2_g, v_ln2_b):
    w = dict(ev_w_in=ev_w_in, ev_w_out=ev_w_out, od_w_in=od_w_in, od_conv_w=od_conv_w, od_a_log=od_a_log,
             od_dt_bias=od_dt_bias, od_norm_w=od_norm_w, od_w_out=od_w_out, ffn_w_up=ffn_w_up, ffn_conv_w=ffn_conv_w,
             ffn_conv_b=ffn_conv_b, ffn_w_down=ffn_w_down, ln1_g=ln1_g, ln1_b=ln1_b, ln2_g=ln2_g, ln2_b=ln2_b)
    mom = dict(ev_w_in=m_ev_w_in, ev_w_out=m_ev_w_out, od_w_in=m_od_w_in, od_conv_w=m_od_conv_w, od_a_log=m_od_a_log,
               od_dt_bias=m_od_dt_bias, od_norm_w=m_od_norm_w, od_w_out=m_od_w_out, ffn_w_up=m_ffn_w_up,
               ffn_conv_w=m_ffn_conv_w, ffn_conv_b=m_ffn_conv_b, ffn_w_down=m_ffn_w_down, ln1_g=m_ln1_g,
               ln1_b=m_ln1_b, ln2_g=m_ln2_g, ln2_b=m_ln2_b)
    var = dict(ev_w_in=v_ev_w_in, ev_w_out=v_ev_w_out, od_w_in=v_od_w_in, od_conv_w=v_od_conv_w, od_a_log=v_od_a_log,
               od_dt_bias=v_od_dt_bias, od_norm_w=v_od_norm_w, od_w_out=v_od_w_out, ffn_w_up=v_ffn_w_up,
               ffn_conv_w=v_ffn_conv_w, ffn_conv_b=v_ffn_conv_b, ffn_w_down=v_ffn_w_down, ln1_g=v_ln1_g,
               ln1_b=v_ln1_b, ln2_g=v_ln2_g, ln2_b=v_ln2_b)

    myid = _my_index()
    small = {n: w[n] for n in SMALL}
    groups = [(layer, part) for layer in range(DEPTH) for part in ("mixer", "ffn")]

    def group_items(gi):
        layer, part = groups[gi]
        its = _layer_items(layer)
        return its[:-3] if part == "mixer" else its[-3:]

    level1, level2 = {}, {}

    def spread_job(gi):
        return ("spread", [_to_send(kind, n, w[n], j) for (_, n, j, kind) in group_items(gi)], None)

    def relay(gi, after, name):
        (srcs, lands), = _exchange_wait([level1.pop(gi)], after, name=name + "_wait")
        more = [spread_job(gi + 1)] if gi + 1 < len(groups) else []
        started, token = _exchange_start([("relay", [], lands)] + more, None, name=name + "_start")
        level2[gi] = (started[0], srcs)
        if more:
            level1[gi + 1] = started[1]
        return token

    def get_w(layer, part, after):
        gi = groups.index((layer, part))
        started, srcs = level2.pop(gi)
        (_, lands), = _exchange_wait([started], after, name=f"gather{gi}_wait")
        lands = [lax.dynamic_update_index_in_dim(l, s, myid, 0) for l, s in zip(lands, srcs)]
        return {key: _from_gather(kind, n, l, f"L{layer}_{key}")
                for (key, n, _, kind), l in zip(group_items(gi), lands)}, None

    def mid(layer, part, after):
        gi = groups.index((layer, part)) + 1
        return relay(gi, after, f"gather{gi}_relay") if gi < len(groups) else None

    landed = {}
    pending = []

    def scatter_finish(after):
        started, gi = pending.pop()
        (srcs, lands), = _exchange_wait([started], after, name=f"scatter{gi}_wait")
        for (key, _, _, _), l, s in zip(group_items(gi), lands, srcs):
            own = lax.dynamic_index_in_dim(s, myid, 0, keepdims=False)
            landed[(groups[gi][0], key)] = lax.dynamic_update_index_in_dim(l, own, myid, 0)

    def put_g(layer, part, g):
        gi = groups.index((layer, part))
        srcs = [_by_owner(kind, n, g[key], f"L{layer}_{key}") for (key, n, _, kind) in group_items(gi)]
        (started,), token = _exchange_start([("scatter", srcs, None)], None, name=f"scatter{gi}_start")
        if pending:
            scatter_finish(token)
        pending.append((started, gi))
        return token

    (level1[0],), token = _exchange_start([spread_job(0)], None, name="gather0_spread_start")
    tables = _tables(positions[0] + token[0, 0].astype(jnp.int32))
    relay(0, tables[-1], "gather0_relay")
    loss, grad_x, gS = _local_step(x[0], tables, loss_target[0], get_w, mid, put_g, small)
    loss = lax.psum(loss, ("x", "y", "c"))

    outs_g, outs_d, outs_m, outs_v = {}, {}, {}, {}
    where = {n: [None] * w[n].shape[0] for n in SHARDED}
    for layer in range(DEPTH):
        for (key, n, j, kind) in _layer_items(layer):
            where[n][j] = (layer, key, kind)

    def update(n):
        g = None
        for j, (layer, key, _) in enumerate(where[n]):
            g = _sum8(landed[(layer, key)], g, j, len(where[n]), name=f"L{layer}_{key}_sum")
        if n == "od_w_in":
            g = g[:, :OD_SHARD]
        if where[n][0][2] == "colT":
            tr = lambda a: jnp.swapaxes(a, 1, 2)
            d, nm, nv = _adamw_sharded(tr(w[n]), tr(mom[n]), tr(var[n]), g, name=f"adamw_{n}")
            outs_g[n], outs_d[n], outs_m[n], outs_v[n] = tr(g), tr(d), tr(nm), tr(nv)
        else:
            outs_g[n] = g
            outs_d[n], outs_m[n], outs_v[n] = _adamw_sharded(w[n], mom[n], var[n], g, name=f"adamw_{n}")

    last = {n for (_, n, _, _) in group_items(pending[0][1])}
    for n in SHARDED:
        if n not in last:
            update(n)
    scatter_finish(outs_d[[n for n in SHARDED if n not in last][-1]])
    for n in SHARDED:
        if n in last:
            update(n)

    gall = _small_exchange(_pack_small(gS), name="small_grads_exchange")
    g, d, nm, nv = _adamw_small(_pack_small({n: w[n] for n in SMALL}), _pack_small({n: mom[n] for n in SMALL}),
                                _pack_small({n: var[n] for n in SMALL}), gall, name="adamw_small")
    for dst, packed in ((outs_g, g), (outs_d, d), (outs_m, nm), (outs_v, nv)):
        dst.update(_unpack_small(packed, {n: w[n] for n in SMALL}))

    return (loss, grad_x[None], *[outs_g[n] for n in ALL_W], *[outs_d[n] for n in ALL_W],
            *[outs_m[n] for n in ALL_W], *[outs_v[n] for n in ALL_W])
```

```python
import functools
import math

import numpy as np
import jax
import jax.numpy as jnp
from jax import lax
from jax.experimental import pallas as pl
from jax.experimental.pallas import tpu as pltpu

F32 = jnp.float32
BF16 = jnp.bfloat16
MESH = pl.DeviceIdType.MESH

D_MODEL = 1024
SEQ = 2048
DEPTH = 4
N_DEV = 8
RET_HEADS, RET_DK, RET_DV = 4, 128, 256
RET_THETA = 10000.0
DIL_HEADS, DIL_HD = 8, 64
DIL_PAIRS = ((128, 1), (512, 4), (2048, 16))
ROPE_THETA = 500000.0
ROPE_DIMS = DIL_HD // 4
GDN_HEADS, GDN_DK, GDN_DV, GDN_CHUNK, GDN_CONV = 8, 128, 128, 64, 4
D_FF = 2816
FFN_CONV = 3
ALPHA = (2.0 * DEPTH) ** 0.25
EPS = 1e-5
RET_QK_W = RET_HEADS * RET_DK
RET_V_W = RET_HEADS * RET_DV
DIL_W = DIL_HEADS * DIL_HD
EV_IN = 2 * RET_QK_W + 2 * RET_V_W + 3 * DIL_W
EV_MIX = RET_V_W + DIL_W
GDN_W = GDN_HEADS * GDN_DK
OD_IN = 4 * GDN_W + 2 * GDN_HEADS
OD_IN_PAD = 4 * GDN_W + 128
ADAM_LR, ADAM_B1, ADAM_B2, ADAM_EPS, ADAM_WD, ADAM_STEP = 0.001, 0.9, 0.999, 1e-08, 0.01, 10

LANES = 128
VMEM_LIMIT = 56 * 1024 * 1024
ATT_BLK = 256
NEG = -1e30


def _cp(**kw):
    return pltpu.CompilerParams(vmem_limit_bytes=VMEM_LIMIT, **kw)


def _tile(n, cap):
    if n <= cap:
        return n
    best = None
    for t in range(LANES, cap + 1, LANES):
        if n % t == 0:
            best = t
    assert best is not None, (n, cap)
    return best


def _mm(a, b, *, ta=False, tb=False, name, out_dtype=F32, dep=None, tm=None, tn=None):
    m = a.shape[1] if ta else a.shape[0]
    k = a.shape[0] if ta else a.shape[1]
    n = b.shape[0] if tb else b.shape[1]
    assert (b.shape[1] if tb else b.shape[0]) == k
    assert a.dtype == BF16 and b.dtype == BF16
    if tn is None:
        tn = n if n <= 1024 else _tile(n, 512)
    if tm is None:
        tm = m if (tn < n and k <= 1024 and m <= 2048) else _tile(m, 512)
    dims = (((0 if ta else 1,), (1 if tb else 0,)), ((), ()))

    def body(a_ref, b_ref, *rest):
        o_ref = rest[-1]
        o_ref[...] = lax.dot_general(a_ref[...], b_ref[...], dims,
                                     preferred_element_type=F32).astype(o_ref.dtype)

    a_spec = pl.BlockSpec((k, tm), lambda i, j: (0, i)) if ta else pl.BlockSpec((tm, k), lambda i, j: (i, 0))
    b_spec = pl.BlockSpec((tn, k), lambda i, j: (j, 0)) if tb else pl.BlockSpec((k, tn), lambda i, j: (0, j))
    extra = [] if dep is None else [dep]
    return pl.pallas_call(
        body, grid=(m // tm, n // tn), in_specs=[a_spec, b_spec] + [pl.BlockSpec(memory_space=pl.ANY)] * len(extra),
        out_specs=pl.BlockSpec((tm, tn), lambda i, j: (i, j)),
        out_shape=jax.ShapeDtypeStruct((m, n), out_dtype), name=name, compiler_params=_cp())(a, b, *extra)


LN_ROWS = 256


def _ln_bwd(z, g, dya, dyb, *, name):
    t, d = z.shape
    two = dyb is not None

    def body(*refs):
        if two:
            z_ref, g_ref, dya_ref, dyb_ref, dz_ref, dzb_ref, dg_ref, db_ref = refs
            dy = dya_ref[...] + ALPHA * dyb_ref[...]
        else:
            z_ref, g_ref, dya_ref, dz_ref, dzb_ref, dg_ref, db_ref = refs
            dy = dya_ref[...]
        zz = z_ref[...]
        mu = jnp.mean(zz, -1, keepdims=True)
        zc = zz - mu
        var = jnp.mean(zc * zc, -1, keepdims=True)
        r = lax.rsqrt(var + EPS)
        xh = zc * r
        dxh = dy * g_ref[...]
        dz = r * (dxh - jnp.mean(dxh, -1, keepdims=True) - xh * jnp.mean(dxh * xh, -1, keepdims=True))
        dz_ref[...] = dz
        dzb_ref[...] = dz.astype(BF16)

        @pl.when(pl.program_id(0) == 0)
        def _():
            dg_ref[...] = jnp.zeros_like(dg_ref)
            db_ref[...] = jnp.zeros_like(db_ref)

        dg_ref[...] += jnp.sum(dy * xh, 0, keepdims=True)
        db_ref[...] += jnp.sum(dy, 0, keepdims=True)

    row = pl.BlockSpec((LN_ROWS, d), lambda i: (i, 0))
    vec = pl.BlockSpec((1, d), lambda i: (0, 0))
    ins = [z, g, dya] + ([dyb] if two else [])
    return pl.pallas_call(
        body, grid=(t // LN_ROWS,), in_specs=[row, vec, row] + ([row] if two else []),
        out_specs=[row, row, vec, vec],
        out_shape=[jax.ShapeDtypeStruct((t, d), F32), jax.ShapeDtypeStruct((t, d), BF16),
                   jax.ShapeDtypeStruct((1, d), F32), jax.ShapeDtypeStruct((1, d), F32)],
        name=name, compiler_params=_cp())(*ins)


def _ln_rows(k):
    return 256 if k > 4096 else 512


def _mm_ln_fwd(parts, w, x, g, b, *, name, dep=None):
    t = parts[0].shape[0]
    offs, k = _part_offsets(parts)
    d = w.shape[1]
    tm = _ln_rows(k)
    npart = len(parts)

    def body(*refs):
        a_refs, w_refs = refs[:npart], refs[npart:2 * npart]
        x_ref, g_ref, b_ref = refs[2 * npart:2 * npart + 3]
        z_ref, y_ref, yb_ref = refs[-3:]
        z = ALPHA * x_ref[...]
        for a_ref, w_ref in zip(a_refs, w_refs):
            z = z + _nn(a_ref[...], w_ref[...])
        mu = jnp.mean(z, -1, keepdims=True)
        zc = z - mu
        var = jnp.mean(zc * zc, -1, keepdims=True)
        y = zc * lax.rsqrt(var + EPS) * g_ref[...] + b_ref[...]
        z_ref[...] = z
        y_ref[...] = y
        yb_ref[...] = y.astype(BF16)

    row = pl.BlockSpec((tm, d), lambda i: (i, 0))
    vec = pl.BlockSpec((1, d), lambda i: (0, 0))
    extra = [] if dep is None else [dep]
    a_specs = [pl.BlockSpec((tm, p.shape[1]), lambda i: (i, 0)) for p in parts]
    w_specs = [pl.BlockSpec((p.shape[1], d), functools.partial(lambda i, blk: (blk, 0), blk=o // p.shape[1]))
               for p, o in zip(parts, offs)]
    return pl.pallas_call(
        body, grid=(t // tm,),
        in_specs=a_specs + w_specs + [row, vec, vec] + [pl.BlockSpec(memory_space=pl.ANY)] * len(extra),
        out_specs=[row, row, row],
        out_shape=[jax.ShapeDtypeStruct((t, d), F32), jax.ShapeDtypeStruct((t, d), F32), jax.ShapeDtypeStruct((t, d), BF16)],
        name=name, compiler_params=_cp())(*parts, *([w] * npart), x, g, b, *extra)


def _part_offsets(parts):
    offs, o = [], 0
    for p in parts:
        assert o % p.shape[1] == 0
        offs.append(o)
        o += p.shape[1]
    return offs, o


def _mm_ln_bwd(parts, w, z, g, dyb, *, name, dep=None):
    t = parts[0].shape[0]
    offs, k = _part_offsets(parts)
    d = w.shape[1]
    tm = _ln_rows(k)
    npart = len(parts)

    def body(*refs):
        a_refs, w_refs = refs[:npart], refs[npart:2 * npart]
        z_ref, g_ref, dyb_ref = refs[2 * npart:2 * npart + 3]
        dz_ref, dzb_ref, dg_ref, db_ref = refs[-4:]
        dy = ALPHA * dyb_ref[...]
        for a_ref, w_ref in zip(a_refs, w_refs):
            dy = dy + _nn(a_ref[...], w_ref[...])
        zz = z_ref[...]
        mu = jnp.mean(zz, -1, keepdims=True)
        zc = zz - mu
        var = jnp.mean(zc * zc, -1, keepdims=True)
        r = lax.rsqrt(var + EPS)
        xh = zc * r
        dxh = dy * g_ref[...]
        dz = r * (dxh - jnp.mean(dxh, -1, keepdims=True) - xh * jnp.mean(dxh * xh, -1, keepdims=True))
        dz_ref[...] = dz
        dzb_ref[...] = dz.astype(BF16)

        @pl.when(pl.program_id(0) == 0)
        def _():
            dg_ref[...] = jnp.zeros_like(dg_ref)
            db_ref[...] = jnp.zeros_like(db_ref)

        dg_ref[...] += jnp.sum(dy * xh, 0, keepdims=True)
        db_ref[...] += jnp.sum(dy, 0, keepdims=True)

    row = pl.BlockSpec((tm, d), lambda i: (i, 0))
    vec = pl.BlockSpec((1, d), lambda i: (0, 0))
    extra = [] if dep is None else [dep]
    a_specs = [pl.BlockSpec((tm, p.shape[1]), lambda i: (i, 0)) for p in parts]
    w_specs = [pl.BlockSpec((p.shape[1], d), functools.partial(lambda i, blk: (blk, 0), blk=o // p.shape[1]))
               for p, o in zip(parts, offs)]
    return pl.pallas_call(
        body, grid=(t // tm,),
        in_specs=a_specs + w_specs + [row, vec, row] + [pl.BlockSpec(memory_space=pl.ANY)] * len(extra),
        out_specs=[row, row, vec, vec],
        out_shape=[jax.ShapeDtypeStruct((t, d), F32), jax.ShapeDtypeStruct((t, d), BF16),
                   jax.ShapeDtypeStruct((1, d), F32), jax.ShapeDtypeStruct((1, d), F32)],
        name=name, compiler_params=_cp())(*parts, *([w] * npart), z, g, dyb, *extra)


def _mm_tn_parts(parts, b, *, name):
    t, n = b.shape
    offs, m = _part_offsets(parts)
    tm = min(_tile(p.shape[1], 1408 if p.shape[1] > 2048 else 512) for p in parts)
    assert all(p.shape[1] % tm == 0 for p in parts)
    first = [o // tm for o in offs]
    count = [p.shape[1] // tm for p in parts]
    npart = len(parts)

    def body(*refs):
        a_refs, b_ref, o_ref = refs[:npart], refs[npart], refs[npart + 1]
        i = pl.program_id(0)
        for a_ref, f, c in zip(a_refs, first, count):
            @pl.when((i >= f) & (i < f + c))
            def _(a_ref=a_ref):
                o_ref[...] = _tn(a_ref[...], b_ref[...]).astype(BF16)

    a_specs = [pl.BlockSpec((t, tm), functools.partial(lambda i, f, c: (0, jnp.clip(i - f, 0, c - 1)), f=f, c=c))
               for f, c in zip(first, count)]
    return pl.pallas_call(
        body, grid=(m // tm,), in_specs=a_specs + [pl.BlockSpec((t, n), lambda i: (0, 0))],
        out_specs=pl.BlockSpec((tm, n), lambda i: (i, 0)), out_shape=jax.ShapeDtypeStruct((m, n), BF16),
        name=name, compiler_params=_cp())(*parts, b)


def _axpy(a, b, *, name):
    t, d = a.shape

    def body(a_ref, b_ref, o_ref):
        o_ref[...] = a_ref[...] + ALPHA * b_ref[...]

    row = pl.BlockSpec((LN_ROWS, d), lambda i: (i, 0))
    return pl.pallas_call(body, grid=(t // LN_ROWS,), in_specs=[row, row], out_specs=row,
                          out_shape=jax.ShapeDtypeStruct((t, d), F32), name=name, compiler_params=_cp())(a, b)


def _loss_head(y, target, *, name):
    t, d = y.shape

    def body(y_ref, t_ref, dy_ref, l_ref):
        e = y_ref[...] - t_ref[...]
        dy_ref[...] = e * (1.0 / d)

        @pl.when(pl.program_id(0) == 0)
        def _():
            l_ref[...] = jnp.zeros_like(l_ref)

        l_ref[...] += jnp.zeros_like(l_ref) + 0.5 * jnp.sum(jnp.mean(e * e, -1, keepdims=True), 0, keepdims=True)

    row = pl.BlockSpec((LN_ROWS, d), lambda i: (i, 0))
    return pl.pallas_call(
        body, grid=(t // LN_ROWS,), in_specs=[row, row],
        out_specs=[row, pl.BlockSpec((1, LANES), lambda i: (0, 0))],
        out_shape=[jax.ShapeDtypeStruct((t, d), F32), jax.ShapeDtypeStruct((1, LANES), F32)],
        name=name, compiler_params=_cp())(y, target)


def _sig(x):
    return 1.0 / (1.0 + jnp.exp(-x))


def _silu(x):
    return x * _sig(x)


def _dsilu(x):
    s = _sig(x)
    return s * (1.0 + x * (1.0 - s))


def _shift_down(u, k, row):
    if k == 0:
        return u
    return jnp.where(row >= k, pltpu.roll(u, k, 0), 0.0)


def _shift_up(u, k, row):
    if k == 0:
        return u
    t = u.shape[0]
    return jnp.where(row < t - k, pltpu.roll(u, t - k, 0), 0.0)


def _dwconv(u, w_ref, row):
    kk = w_ref.shape[0]
    acc = None
    for j in range(kk):
        term = w_ref[j:j + 1, :] * _shift_down(u, kk - 1 - j, row)
        acc = term if acc is None else acc + term
    return acc


def _dwconv_bwd(u, w_ref, dc, row, dw_ref):
    kk = w_ref.shape[0]
    du = None
    for j in range(kk):
        term = w_ref[j:j + 1, :] * _shift_up(dc, kk - 1 - j, row)
        du = term if du is None else du + term
        dw_ref[j:j + 1, :] = jnp.sum(dc * _shift_down(u, kk - 1 - j, row), 0, keepdims=True)
    return du


CONV_ROWS = 256


def _rows(b):
    return pl.ds(pl.multiple_of(b * CONV_ROWS, CONV_ROWS), CONV_ROWS)


def _shifted_down(ref, b, k, row):
    cur = ref[_rows(b), :]
    if k == 0:
        return cur
    prev = jnp.where(b > 0, ref[_rows(jnp.maximum(b - 1, 0)), :], 0.0)
    return jnp.where(row >= k, pltpu.roll(cur, k, 0), pltpu.roll(prev, k, 0))


def _shifted_up(ref, b, k, row, nblk):
    cur = ref[_rows(b), :]
    if k == 0:
        return cur
    nxt = jnp.where(b < nblk - 1, ref[_rows(jnp.minimum(b + 1, nblk - 1)), :], 0.0)
    return jnp.where(row < CONV_ROWS - k, pltpu.roll(cur, CONV_ROWS - k, 0), pltpu.roll(nxt, CONV_ROWS - k, 0))


def _dwconv_blk(u_ref, w_ref, b, row):
    kk = w_ref.shape[0]
    views = [_shifted_down(u_ref, b, kk - 1 - j, row) for j in range(kk)]
    acc = None
    for j in range(kk):
        term = w_ref[j:j + 1, :] * views[j]
        acc = term if acc is None else acc + term
    return acc, views


def _dwconv_du_blk(dc_ref, w_ref, b, row, nblk):
    kk = w_ref.shape[0]
    du = None
    for j in range(kk):
        term = w_ref[j:j + 1, :] * _shifted_up(dc_ref, b, kk - 1 - j, row, nblk)
        du = term if du is None else du + term
    return du


FFN_TC = 256


def _ffn_up_mid(x, up_t, cw, cb, *, name, dep=None):
    t, d = x.shape
    nb = D_FF // FFN_TC

    def body(x_ref, ugt_ref, uvt_ref, wg_ref, wv_ref, bg_ref, bv_ref, *rest):
        ug_ref, uv_ref, a_ref = rest[-3:]
        xx = x_ref[...]
        row = lax.broadcasted_iota(jnp.int32, (t, FFN_TC), 0)
        ug = _nt(xx, ugt_ref[...])
        ug_ref[...] = ug
        uv = _nt(xx, uvt_ref[...])
        uv_ref[...] = uv
        cg = _dwconv(ug, wg_ref, row) + bg_ref[...]
        cv = _dwconv(uv, wv_ref, row) + bv_ref[...]
        a_ref[...] = (_silu(cg) * cv).astype(BF16)

    col = pl.BlockSpec((t, FFN_TC), lambda j: (0, j))
    wt = lambda off: pl.BlockSpec((FFN_TC, d), lambda j: (j + off, 0))
    wsp = lambda off: pl.BlockSpec((FFN_CONV, FFN_TC), lambda j: (0, j + off))
    bsp = lambda off: pl.BlockSpec((1, FFN_TC), lambda j: (0, j + off))
    extra = [] if dep is None else [dep]
    return pl.pallas_call(
        body, grid=(nb,),
        in_specs=[pl.BlockSpec((t, d), lambda j: (0, 0)), wt(0), wt(nb), wsp(0), wsp(nb), bsp(0), bsp(nb)]
        + [pl.BlockSpec(memory_space=pl.ANY)] * len(extra),
        out_specs=[col, col, col],
        out_shape=[jax.ShapeDtypeStruct((t, D_FF), F32), jax.ShapeDtypeStruct((t, D_FF), F32),
                   jax.ShapeDtypeStruct((t, D_FF), BF16)],
        name=name, compiler_params=_cp())(x, up_t, up_t, cw, cw, cb, cb, *extra)


def _ffn_mid_bwd(ug, uv, cw, cb, dz, down, *, name):
    t, d = dz.shape
    nb = D_FF // FFN_TC

    nblk = t // CONV_ROWS

    def body(ug_ref, uv_ref, wg_ref, wv_ref, bg_ref, bv_ref, dz_ref, dn_ref,
             dug_ref, duv_ref, dwg_ref, dwv_ref, dbg_ref, dbv_ref, da_ref, dcg_s, dcv_s):
        da_ref[...] = _nt(dz_ref[...], dn_ref[...])
        row = lax.broadcasted_iota(jnp.int32, (CONV_ROWS, FFN_TC), 0)
        zero = jnp.zeros((1, FFN_TC), F32)

        def first(b, acc):
            cg, ugs = _dwconv_blk(ug_ref, wg_ref, b, row)
            cv, uvs = _dwconv_blk(uv_ref, wv_ref, b, row)
            cg = cg + bg_ref[...]
            cv = cv + bv_ref[...]
            da_ = da_ref[_rows(b), :]
            dcv = da_ * _silu(cg)
            dcg = da_ * cv * _dsilu(cg)
            dcg_s[_rows(b), :] = dcg
            dcv_s[_rows(b), :] = dcv
            red = [jnp.sum(dcg * s, 0, keepdims=True) for s in ugs] + [jnp.sum(dcg, 0, keepdims=True)]
            red += [jnp.sum(dcv * s, 0, keepdims=True) for s in uvs] + [jnp.sum(dcv, 0, keepdims=True)]
            return tuple(a + r for a, r in zip(acc, red))

        acc = lax.fori_loop(0, nblk, first, (zero,) * (2 * FFN_CONV + 2))
        for j in range(FFN_CONV):
            dwg_ref[j:j + 1, :] = acc[j]
            dwv_ref[j:j + 1, :] = acc[FFN_CONV + 1 + j]
        dbg_ref[...] = acc[FFN_CONV]
        dbv_ref[...] = acc[2 * FFN_CONV + 1]

        def second(b, carry):
            dug_ref[_rows(b), :] = _dwconv_du_blk(dcg_s, wg_ref, b, row, nblk).astype(BF16)
            duv_ref[_rows(b), :] = _dwconv_du_blk(dcv_s, wv_ref, b, row, nblk).astype(BF16)
            return carry

        lax.fori_loop(0, nblk, second, 0)

    col = pl.BlockSpec((t, FFN_TC), lambda j: (0, j))
    wsp = lambda off: pl.BlockSpec((FFN_CONV, FFN_TC), lambda j: (0, j + off))
    bsp = lambda off: pl.BlockSpec((1, FFN_TC), lambda j: (0, j + off))
    outs = pl.pallas_call(
        body, grid=(nb,),
        in_specs=[col, col, wsp(0), wsp(nb), bsp(0), bsp(nb), pl.BlockSpec((t, d), lambda j: (0, 0)),
                  pl.BlockSpec((FFN_TC, d), lambda j: (j, 0))],
        out_specs=[col, col, wsp(0), wsp(0), bsp(0), bsp(0)],
        out_shape=[jax.ShapeDtypeStruct((t, D_FF), BF16), jax.ShapeDtypeStruct((t, D_FF), BF16),
                   jax.ShapeDtypeStruct((FFN_CONV, D_FF), F32), jax.ShapeDtypeStruct((FFN_CONV, D_FF), F32),
                   jax.ShapeDtypeStruct((1, D_FF), F32), jax.ShapeDtypeStruct((1, D_FF), F32)],
        scratch_shapes=[pltpu.VMEM((t, FFN_TC), F32), pltpu.VMEM((t, FFN_TC), F32), pltpu.VMEM((t, FFN_TC), F32)],
        name=name, compiler_params=_cp())(ug, uv, cw, cw, cb, cb, dz, down)
    dug, duv, dwg, dwv, dbg, dbv = outs
    return [dug, duv], jnp.concatenate([dwg, dwv], 1), jnp.concatenate([dbg, dbv], 1)


def _rot_a(x, c2, s2):
    return x * c2 + pltpu.roll(x, RET_DK // 2, 1) * s2


def _rot_a_t(dy, c2, s2):
    return dy * c2 + pltpu.roll(dy * s2, RET_DK // 2, 1)


RET_BWD_BLK = 512


def _decay_tile(lg, blk_diff, blk=ATT_BLK):
    r = lax.broadcasted_iota(jnp.int32, (blk, blk), 0)
    c = lax.broadcasted_iota(jnp.int32, (blk, blk), 1)
    rel = r - c + blk_diff * blk
    return jnp.where(rel >= 0, jnp.exp(jnp.maximum(rel, 0).astype(F32) * lg), 0.0)


def _nt(a, b):
    return lax.dot_general(a, b, (((1,), (1,)), ((), ())), preferred_element_type=F32)


def _nn(a, b):
    return lax.dot_general(a, b, (((1,), (0,)), ((), ())), preferred_element_type=F32)


def _tn(a, b):
    return lax.dot_general(a, b, (((0,), (0,)), ((), ())), preferred_element_type=F32)


def _ret_specs(t):
    q = pl.BlockSpec((t, RET_DK), lambda h: (0, h))
    k = pl.BlockSpec((t, RET_DK), lambda h: (0, RET_HEADS + h))
    v = pl.BlockSpec((t, RET_DV), lambda h: (0, RET_HEADS + h))
    g = pl.BlockSpec((t, RET_DV), lambda h: (0, 2 * RET_HEADS + h))
    tab = pl.BlockSpec((t, RET_DK), lambda h: (0, 0))
    lg = pl.BlockSpec((1, 1, LANES), lambda h: (h, 0, 0))
    return q, k, v, g, tab, lg


def _ret_fwd(h, c2, s2, lgt, *, name):
    t = h.shape[0]
    nblk = t // ATT_BLK
    scale = RET_DK ** -0.5

    def body(q_ref, k_ref, v_ref, g_ref, c_ref, s_ref, lg_ref, o_ref, ya_ref, qs, ks, vs):
        c2_, s2_ = c_ref[...], s_ref[...]
        qs[...] = _rot_a(q_ref[...], c2_, s2_).astype(BF16)
        ks[...] = (_rot_a(k_ref[...], c2_, s2_) * scale).astype(BF16)
        vs[...] = v_ref[...].astype(BF16)
        lg = lg_ref[0, :, 0:1]
        for i in range(nblk):
            qi = qs[pl.ds(i * ATT_BLK, ATT_BLK), :]
            acc = jnp.zeros((ATT_BLK, RET_DV), F32)
            for j in range(i + 1):
                sl = pl.ds(j * ATT_BLK, ATT_BLK)
                s = _nt(qi, ks[sl, :]) * _decay_tile(lg, i - j)
                acc = acc + _nn(s.astype(BF16), vs[sl, :])
            rows = pl.ds(i * ATT_BLK, ATT_BLK)
            o_ref[rows, :] = acc
            r = lax.rsqrt(jnp.mean(acc * acc, -1, keepdims=True) + EPS)
            ya_ref[rows, :] = (acc * r * _silu(g_ref[rows, :])).astype(BF16)

    q, k, v, g, tab, lg = _ret_specs(t)
    out = pl.BlockSpec((t, RET_DV), lambda hh: (0, hh))
    return pl.pallas_call(
        body, grid=(RET_HEADS,), in_specs=[q, k, v, g, tab, tab, lg], out_specs=[out, out],
        out_shape=[jax.ShapeDtypeStruct((t, RET_V_W), F32), jax.ShapeDtypeStruct((t, RET_V_W), BF16)],
        scratch_shapes=[pltpu.VMEM((t, RET_DK), BF16), pltpu.VMEM((t, RET_DK), BF16), pltpu.VMEM((t, RET_DV), BF16)],
        name=name, compiler_params=_cp())(h, h, h, h, c2, s2, lgt)


def _ret_bwd(h, c2, s2, lgt, o, dy, *, name):
    t = h.shape[0]
    blk = RET_BWD_BLK
    nblk = t // blk
    scale = RET_DK ** -0.5

    def body(q_ref, k_ref, v_ref, g_ref, c_ref, s_ref, lg_ref, o_ref, dy_ref,
             dq_ref, dk_ref, dv_ref, dg_ref, qs, ks, vs, dos, dka, dva):
        c2_, s2_ = c_ref[...], s_ref[...]
        qs[...] = _rot_a(q_ref[...], c2_, s2_).astype(BF16)
        ks[...] = (_rot_a(k_ref[...], c2_, s2_) * scale).astype(BF16)
        vs[...] = v_ref[...].astype(BF16)
        lg = lg_ref[0, :, 0:1]
        oo = o_ref[...]
        gg = g_ref[...]
        dya = dy_ref[...]
        r = lax.rsqrt(jnp.mean(oo * oo, -1, keepdims=True) + EPS)
        rn = oo * r
        dg_ref[...] = (dya * rn * _dsilu(gg)).astype(BF16)
        drn = dya * _silu(gg)
        dos[...] = (r * (drn - rn * jnp.mean(drn * rn, -1, keepdims=True))).astype(BF16)
        dka[...] = jnp.zeros_like(dka)
        dva[...] = jnp.zeros_like(dva)
        for i in range(nblk):
            rows = pl.ds(i * blk, blk)
            qi = qs[rows, :]
            doi = dos[rows, :]
            dqa = jnp.zeros((blk, RET_DK), F32)
            for j in range(i + 1):
                sl = pl.ds(j * blk, blk)
                dt_ = _decay_tile(lg, i - j, blk)
                kj = ks[sl, :]
                s = (_nt(qi, kj) * dt_).astype(BF16)
                ds = (_nt(doi, vs[sl, :]) * dt_).astype(BF16)
                dqa = dqa + _nn(ds, kj)
                dka[sl, :] += _tn(ds, qi)
                dva[sl, :] += _tn(s, doi)
            dq_ref[rows, :] = _rot_a_t(dqa, c_ref[rows, :], s_ref[rows, :]).astype(BF16)
        dk_ref[...] = (_rot_a_t(dka[...], c2_, s2_) * scale).astype(BF16)
        dv_ref[...] = dva[...].astype(BF16)

    q, k, v, g, tab, lg = _ret_specs(t)
    blk_v = pl.BlockSpec((t, RET_DV), lambda hh: (0, hh))
    blk_k = pl.BlockSpec((t, RET_DK), lambda hh: (0, hh))
    return pl.pallas_call(
        body, grid=(RET_HEADS,), in_specs=[q, k, v, g, tab, tab, lg, blk_v, blk_v],
        out_specs=[blk_k, blk_k, blk_v, blk_v],
        out_shape=[jax.ShapeDtypeStruct((t, RET_QK_W), BF16), jax.ShapeDtypeStruct((t, RET_QK_W), BF16),
                   jax.ShapeDtypeStruct((t, RET_V_W), BF16), jax.ShapeDtypeStruct((t, RET_V_W), BF16)],
        scratch_shapes=[pltpu.VMEM((t, RET_DK), BF16), pltpu.VMEM((t, RET_DK), BF16), pltpu.VMEM((t, RET_DV), BF16),
                        pltpu.VMEM((t, RET_DV), BF16), pltpu.VMEM((t, RET_DK), F32), pltpu.VMEM((t, RET_DV), F32)],
        name=name, compiler_params=_cp())(h, h, h, h, c2, s2, lgt, o, dy)


def _rot_b(x, cb, shi, slo):
    return x * cb + pltpu.roll(x, ROPE_DIMS // 2, 1) * shi + pltpu.roll(x, LANES - ROPE_DIMS // 2, 1) * slo


def _rot_b_t(dy, cb, shi, slo):
    return dy * cb + pltpu.roll(dy * shi, LANES - ROPE_DIMS // 2, 1) + pltpu.roll(dy * slo, ROPE_DIMS // 2, 1)


def _dil_specs(t):
    base = (2 * RET_QK_W + 2 * RET_V_W) // LANES
    npair = DIL_W // LANES
    q = pl.BlockSpec((t, LANES), lambda p: (0, base + p))
    k = pl.BlockSpec((t, LANES), lambda p: (0, base + npair + p))
    v = pl.BlockSpec((t, LANES), lambda p: (0, base + 2 * npair + p))
    tab = pl.BlockSpec((t, LANES), lambda p: (0, 0))
    strip = pl.BlockSpec((ATT_BLK, t), lambda p: (0, 0))
    pair = pl.BlockSpec((t, LANES), lambda p: (0, p))
    return q, k, v, tab, strip, pair


def _dil_fwd(h, cb, shi, slo, strip, *, name):
    t = h.shape[0]
    nblk = t // ATT_BLK
    scale = DIL_HD ** -0.5

    def body(q_ref, k_ref, v_ref, cb_ref, shi_ref, slo_ref, st_ref, o_ref, yb_ref, lse_ref, qs, ks, vs):
        cb_, shi_, slo_ = cb_ref[...], shi_ref[...], slo_ref[...]
        lane = lax.broadcasted_iota(jnp.int32, (t, LANES), 1)
        qr = _rot_b(q_ref[...], cb_, shi_, slo_) * scale
        qs[0] = jnp.where(lane < DIL_HD, qr, 0.0).astype(BF16)
        qs[1] = jnp.where(lane >= DIL_HD, qr, 0.0).astype(BF16)
        ks[...] = _rot_b(k_ref[...], cb_, shi_, slo_).astype(BF16)
        vs[...] = v_ref[...].astype(BF16)
        lane_b = lax.broadcasted_iota(jnp.int32, (ATT_BLK, LANES), 1)
        for i in range(nblk):
            w = (i + 1) * ATT_BLK
            rows = pl.ds(i * ATT_BLK, ATT_BLK)
            logc = st_ref[:, t - w:t]
            outs, lses = [], []
            for hd in range(2):
                s = _nt(qs[hd, rows, :], ks[0:w, :]) + logc
                m = jnp.max(s, -1, keepdims=True)
                p = jnp.exp(s - m)
                l = jnp.sum(p, -1, keepdims=True)
                outs.append(_nn(p.astype(BF16), vs[0:w, :]) / l)
                lses.append(m + jnp.log(l))
            o = jnp.where(lane_b < DIL_HD, outs[0], outs[1])
            o_ref[rows, :] = o
            yb_ref[rows, :] = o.astype(BF16)
            lse_ref[rows, :] = jnp.where(lane_b < DIL_HD, lses[0], lses[1])

    q, k, v, tab, strip_spec, pair = _dil_specs(t)
    return pl.pallas_call(
        body, grid=(DIL_W // LANES,), in_specs=[q, k, v, tab, tab, tab, strip_spec], out_specs=[pair, pair, pair],
        out_shape=[jax.ShapeDtypeStruct((t, DIL_W), F32), jax.ShapeDtypeStruct((t, DIL_W), BF16),
                   jax.ShapeDtypeStruct((t, DIL_W), F32)],
        scratch_shapes=[pltpu.VMEM((2, t, LANES), BF16), pltpu.VMEM((t, LANES), BF16), pltpu.VMEM((t, LANES), BF16)],
        name=name, compiler_params=_cp())(h, h, h, cb, shi, slo, strip)


def _dil_bwd(h, cb, shi, slo, strip, o, lse, dy, *, name):
    t = h.shape[0]
    nblk = t // ATT_BLK
    scale = DIL_HD ** -0.5

    def body(q_ref, k_ref, v_ref, cb_ref, shi_ref, slo_ref, st_ref, o_ref, lse_ref, dy_ref,
             dq_ref, dk_ref, dv_ref, qs, ks, vs, dos, dls, dka, dva):
        cb_, shi_, slo_ = cb_ref[...], shi_ref[...], slo_ref[...]
        lane = lax.broadcasted_iota(jnp.int32, (t, LANES), 1)
        qr = _rot_b(q_ref[...], cb_, shi_, slo_) * scale
        qs[0] = jnp.where(lane < DIL_HD, qr, 0.0).astype(BF16)
        qs[1] = jnp.where(lane >= DIL_HD, qr, 0.0).astype(BF16)
        ks[...] = _rot_b(k_ref[...], cb_, shi_, slo_).astype(BF16)
        vs[...] = v_ref[...].astype(BF16)
        do = dy_ref[...]
        prod = do * o_ref[...]
        d0 = jnp.sum(jnp.where(lane < DIL_HD, prod, 0.0), -1, keepdims=True)
        d1 = jnp.sum(jnp.where(lane >= DIL_HD, prod, 0.0), -1, keepdims=True)
        dls[...] = jnp.where(lane < DIL_HD, d0, d1)
        dos[0] = jnp.where(lane < DIL_HD, do, 0.0).astype(BF16)
        dos[1] = jnp.where(lane >= DIL_HD, do, 0.0).astype(BF16)
        dka[...] = jnp.zeros_like(dka)
        dva[...] = jnp.zeros_like(dva)
        lane_b = lax.broadcasted_iota(jnp.int32, (ATT_BLK, LANES), 1)
        for i in range(nblk):
            w = (i + 1) * ATT_BLK
            rows = pl.ds(i * ATT_BLK, ATT_BLK)
            logc = st_ref[:, t - w:t]
            dqs = []
            for hd in range(2):
                col = hd * DIL_HD
                qh = qs[hd, rows, :]
                doh = dos[hd, rows, :]
                lse_h = lse_ref[rows, col:col + 1]
                dl_h = dls[rows, col:col + 1]
                p = jnp.exp(_nt(qh, ks[0:w, :]) + logc - lse_h)
                dp = _nt(doh, vs[0:w, :])
                ds = (p * (dp - dl_h)).astype(BF16)
                dqs.append(_nn(ds, ks[0:w, :]))
                dka[0:w, :] += _tn(ds, qh)
                dva[0:w, :] += _tn(p.astype(BF16), doh)
            dq = jnp.where(lane_b < DIL_HD, dqs[0], dqs[1]) * scale
            dq_ref[rows, :] = _rot_b_t(dq, cb_ref[rows, :], shi_ref[rows, :], slo_ref[rows, :]).astype(BF16)
        dk_ref[...] = _rot_b_t(dka[...], cb_, shi_, slo_).astype(BF16)
        dv_ref[...] = dva[...].astype(BF16)

    q, k, v, tab, strip_spec, pair = _dil_specs(t)
    dy_spec = pl.BlockSpec((t, LANES), lambda p: (0, RET_V_W // LANES + p))
    return pl.pallas_call(
        body, grid=(DIL_W // LANES,), in_specs=[q, k, v, tab, tab, tab, strip_spec, pair, pair, dy_spec],
        out_specs=[pair, pair, pair],
        out_shape=[jax.ShapeDtypeStruct((t, DIL_W), BF16)] * 3,
        scratch_shapes=[pltpu.VMEM((2, t, LANES), BF16), pltpu.VMEM((t, LANES), BF16), pltpu.VMEM((t, LANES), BF16),
                        pltpu.VMEM((2, t, LANES), BF16), pltpu.VMEM((t, LANES), F32),
                        pltpu.VMEM((t, LANES), F32), pltpu.VMEM((t, LANES), F32)],
        name=name, compiler_params=_cp())(h, h, h, cb, shi, slo, strip, o, lse, dy)


def _gdn_prep_fwd(h, cw, *, name):
    t = h.shape[0]
    qscale = GDN_DK ** -0.5

    def body(hq_ref, hk_ref, hv_ref, wq_ref, wk_ref, wv_ref, q_ref, k_ref, v_ref):
        row = lax.broadcasted_iota(jnp.int32, (t, GDN_DK), 0)
        sq = _silu(_dwconv(hq_ref[...], wq_ref, row))
        sk = _silu(_dwconv(hk_ref[...], wk_ref, row))
        q_ref[0] = sq * lax.rsqrt(jnp.sum(sq * sq, -1, keepdims=True) + 1e-6) * qscale
        k_ref[0] = sk * lax.rsqrt(jnp.sum(sk * sk, -1, keepdims=True) + 1e-6)
        v_ref[0] = _silu(_dwconv(hv_ref[...], wv_ref, row))

    hs = lambda off: pl.BlockSpec((t, GDN_DK), lambda i: (0, i + off))
    ws = lambda off: pl.BlockSpec((GDN_CONV, GDN_DK), lambda i: (0, i + off))
    out = pl.BlockSpec((1, t, GDN_DK), lambda i: (i, 0, 0))
    return pl.pallas_call(
        body, grid=(GDN_HEADS,), in_specs=[hs(0), hs(8), hs(16), ws(0), ws(8), ws(16)], out_specs=[out, out, out],
        out_shape=[jax.ShapeDtypeStruct((GDN_HEADS, t, GDN_DK), F32)] * 3,
        name=name, compiler_params=_cp())(h, h, h, cw, cw, cw)


def _gdn_prep_bwd(h, cw, dq, dk, dv, *, name):
    t = h.shape[0]
    qscale = GDN_DK ** -0.5

    def body(hq_ref, hk_ref, hv_ref, wq_ref, wk_ref, wv_ref, dq_ref, dk_ref, dv_ref,
             dhq_ref, dhk_ref, dhv_ref, dwq_ref, dwk_ref, dwv_ref):
        row = lax.broadcasted_iota(jnp.int32, (t, GDN_DK), 0)

        def one(h_ref, w_ref, d_ref, dh_ref, dw_ref, norm, sc):
            u = h_ref[...]
            c = _dwconv(u, w_ref, row)
            d = d_ref[0]
            if norm:
                s = _silu(c)
                r = lax.rsqrt(jnp.sum(s * s, -1, keepdims=True) + 1e-6)
                n = s * r
                d = d * sc
                d = r * (d - n * jnp.sum(d * n, -1, keepdims=True))
            dc = d * _dsilu(c)
            dh_ref[...] = _dwconv_bwd(u, w_ref, dc, row, dw_ref).astype(BF16)

        one(hq_ref, wq_ref, dq_ref, dhq_ref, dwq_ref, True, qscale)
        one(hk_ref, wk_ref, dk_ref, dhk_ref, dwk_ref, True, 1.0)
        one(hv_ref, wv_ref, dv_ref, dhv_ref, dwv_ref, False, 1.0)

    hs = lambda off: pl.BlockSpec((t, GDN_DK), lambda i: (0, i + off))
    ws = lambda off: pl.BlockSpec((GDN_CONV, GDN_DK), lambda i: (0, i + off))
    hd = pl.BlockSpec((1, t, GDN_DK), lambda i: (i, 0, 0))
    return pl.pallas_call(
        body, grid=(GDN_HEADS,), in_specs=[hs(0), hs(8), hs(16), ws(0), ws(8), ws(16), hd, hd, hd],
        out_specs=[hs(0), hs(0), hs(0), ws(0), ws(0), ws(0)],
        out_shape=[jax.ShapeDtypeStruct((t, GDN_W), BF16)] * 3 + [jax.ShapeDtypeStruct((GDN_CONV, GDN_W), F32)] * 3,
        name=name, compiler_params=_cp())(h, h, h, cw, cw, cw, dq, dk, dv)


def _make_mm2(wide):
    def raw(a, b, dims):
        if wide:
            return lax.dot_general(a, b, (dims, ((), ())), precision=lax.Precision.HIGHEST, preferred_element_type=F32)
        return lax.dot_general(a.astype(BF16), b.astype(BF16), (dims, ((), ())), preferred_element_type=F32)

    @jax.custom_vjp
    def nn(a, b):
        return raw(a, b, ((1,), (0,)))

    @jax.custom_vjp
    def nt(a, b):
        return raw(a, b, ((1,), (1,)))

    @jax.custom_vjp
    def tn(a, b):
        return raw(a, b, ((0,), (0,)))

    nn.defvjp(lambda a, b: (nn(a, b), (a, b)), lambda r, g: (nt(g, r[1]), tn(r[0], g)))
    nt.defvjp(lambda a, b: (nt(a, b), (a, b)), lambda r, g: (nn(g, r[1]), tn(g, r[0])))
    tn.defvjp(lambda a, b: (tn(a, b), (a, b)), lambda r, g: (nt(r[1], g), nn(r[0], g)))
    return nn, nt, tn


_NN, _NT, _TN = _make_mm2(False)
_NNW, _NTW, _TNW = _make_mm2(True)


def _square_masks(c):
    ri = lax.broadcasted_iota(jnp.int32, (c, c), 0)
    ci = lax.broadcasted_iota(jnp.int32, (c, c), 1)
    return ri >= ci, ri > ci, ri == ci


def _cumsum_rows(m):
    tri, _, _ = _square_masks(m.shape[0])
    return _NNW(tri.astype(F32), m)


def _transpose_sq(m):
    _, _, eye = _square_masks(m.shape[0])
    return _NTW(eye.astype(F32), m)


@jax.custom_vjp
def _inv_unit_lower(l):
    c = l.shape[0]
    _, _, eye = _square_masks(c)
    p = -l
    t = eye.astype(F32) + p
    for _ in range(int(math.log2(c)) - 1):
        p = _NNW(p, p)
        t = t + _NNW(t, p)
    return t


def _inv_fwd(l):
    t = _inv_unit_lower(l)
    return t, t


def _inv_bwd(t, dt):
    return (-_NTW(_TNW(t, dt), t),)


_inv_unit_lower.defvjp(_inv_fwd, _inv_bwd)


@jax.custom_vjp
def _inv_known(l, t):
    return t


_inv_known.defvjp(lambda l, t: (t, t), lambda t, dt: (_inv_bwd(t, dt)[0], jnp.zeros_like(t)))


def _softplus(x):
    return jnp.maximum(x, 0.0) + jnp.log1p(jnp.exp(-jnp.abs(x)))


def _gdn_chunk(q, k, v, braw, araw, alog, dtb, state, inv=None):
    c = q.shape[0]
    dv = v.shape[1]
    tri, strict, _ = _square_masks(c)
    beta = _sig(braw)
    g = -jnp.exp(alog) * _softplus(araw + dtb)
    gcm = _cumsum_rows(g * jnp.ones((c, c), F32))
    gct = _transpose_sq(gcm)
    decay = jnp.where(tri, jnp.exp(jnp.where(tri, gcm - gct, 0.0)), 0.0)
    gc = jnp.sum(gcm, 1, keepdims=True) * (1.0 / c)
    glast = jnp.sum(g, 0, keepdims=True)
    egc = jnp.exp(gc)
    kb = k * beta
    low = jnp.where(strict, _NT(kb, k) * decay, 0.0)
    tm = _inv_unit_lower(low) if inv is None else _inv_known(low, inv)
    sol = _NNW(tm, jnp.concatenate([v * beta, kb * egc], 1))
    u, w = sol[:, :dv], sol[:, dv:]
    attn = jnp.where(tri, _NT(q, k) * decay, 0.0)
    k_dec = k * jnp.exp(glast - gc)
    q_dec = q * egc
    v_new = u - _NN(w, state)
    o = _NN(q_dec, state) + _NN(attn, v_new)
    new_state = state * jnp.exp(glast) + _TN(k_dec, v_new)
    return o, new_state, tm


def _gdn_specs(t, rev):
    nch = t // GDN_CHUNK
    cm = (lambda n: nch - 1 - n) if rev else (lambda n: n)
    tok = pl.BlockSpec((GDN_HEADS, GDN_CHUNK, GDN_DK), lambda n: (0, cm(n), 0))
    par = pl.BlockSpec((GDN_HEADS, 1, LANES), lambda n: (0, 0, 0))
    st = pl.BlockSpec((GDN_HEADS, 1, GDN_DK, GDN_DV), lambda n: (0, cm(n), 0, 0))
    inv = pl.BlockSpec((GDN_HEADS, GDN_CHUNK, GDN_CHUNK), lambda n: (0, cm(n), 0))
    sc = pl.BlockSpec((GDN_CHUNK, LANES), lambda n: (cm(n), 4 * GDN_W // LANES))
    return tok, par, st, inv, sc


def _head_columns(sc_ref, first):
    return jnp.stack([sc_ref[:, first + hh:first + hh + 1] for hh in range(GDN_HEADS)])


def _gdn_core_fwd(q, k, v, h, alog, dtb, *, name):
    t = q.shape[1]
    nch = t // GDN_CHUNK

    def body(q_ref, k_ref, v_ref, sc_ref, al_ref, dt_ref, o_ref, st_ref, inv_ref, state):
        @pl.when(pl.program_id(0) == 0)
        def _():
            state[...] = jnp.zeros_like(state)

        s0 = state[...]
        st_ref[:, 0] = s0
        o, s1, tm = jax.vmap(_gdn_chunk)(q_ref[...], k_ref[...], v_ref[...], _head_columns(sc_ref, 0),
                                         _head_columns(sc_ref, GDN_HEADS), al_ref[:, :, 0:1], dt_ref[:, :, 0:1], s0)
        o_ref[...] = o
        inv_ref[...] = tm
        state[...] = s1

    tok, par, st, inv, sc = _gdn_specs(t, False)
    return pl.pallas_call(
        body, grid=(nch,), in_specs=[tok, tok, tok, sc, par, par], out_specs=[tok, st, inv],
        out_shape=[jax.ShapeDtypeStruct((GDN_HEADS, t, GDN_DV), F32),
                   jax.ShapeDtypeStruct((GDN_HEADS, nch, GDN_DK, GDN_DV), F32),
                   jax.ShapeDtypeStruct((GDN_HEADS, t, GDN_CHUNK), F32)],
        scratch_shapes=[pltpu.VMEM((GDN_HEADS, GDN_DK, GDN_DV), F32)],
        name=name, compiler_params=_cp())(q, k, v, h, alog, dtb)


def _gdn_core_bwd(q, k, v, h, alog, dtb, states, invs, do, *, name):
    t = q.shape[1]
    nch = t // GDN_CHUNK

    def body(q_ref, k_ref, v_ref, sc_ref, al_ref, dt_ref, st_ref, inv_ref, do_ref,
             dq_ref, dk_ref, dv_ref, dsc_ref, dal_ref, ddt_ref, dstate):
        @pl.when(pl.program_id(0) == 0)
        def _():
            dstate[...] = jnp.zeros_like(dstate)
            dal_ref[...] = jnp.zeros_like(dal_ref)
            ddt_ref[...] = jnp.zeros_like(ddt_ref)

        args = (q_ref[...], k_ref[...], v_ref[...], _head_columns(sc_ref, 0), _head_columns(sc_ref, GDN_HEADS),
                al_ref[:, :, 0:1], dt_ref[:, :, 0:1], st_ref[:, 0])
        tm = inv_ref[...]

        def chunk(*a):
            return jax.vmap(_gdn_chunk)(*a, tm)[:2]

        _, pull = jax.vjp(chunk, *args)
        dq, dk, dv, dbr, dar, dal, ddt, ds = pull((do_ref[...], dstate[...]))
        dq_ref[...] = dq
        dk_ref[...] = dk
        dv_ref[...] = dv
        lane = lax.broadcasted_iota(jnp.int32, (GDN_CHUNK, LANES), 1)
        dsc = jnp.zeros((GDN_CHUNK, LANES), F32)
        for hh in range(GDN_HEADS):
            dsc = jnp.where(lane == hh, dbr[hh], dsc)
            dsc = jnp.where(lane == GDN_HEADS + hh, dar[hh], dsc)
        dsc_ref[...] = dsc
        dal_ref[...] += dal + jnp.zeros((GDN_HEADS, 1, LANES), F32)
        ddt_ref[...] += ddt + jnp.zeros((GDN_HEADS, 1, LANES), F32)
        dstate[...] = ds

    tok, par, st, inv, sc = _gdn_specs(t, True)
    tokshape = jax.ShapeDtypeStruct((GDN_HEADS, t, GDN_DK), F32)
    parshape = jax.ShapeDtypeStruct((GDN_HEADS, 1, LANES), F32)
    nch_map = pl.BlockSpec((GDN_CHUNK, LANES), lambda n: (nch - 1 - n, 0))
    return pl.pallas_call(
        body, grid=(nch,), in_specs=[tok, tok, tok, sc, par, par, st, inv, tok],
        out_specs=[tok, tok, tok, nch_map, par, par],
        out_shape=[tokshape] * 3 + [jax.ShapeDtypeStruct((t, LANES), F32)] + [parshape] * 2,
        scratch_shapes=[pltpu.VMEM((GDN_HEADS, GDN_DK, GDN_DV), F32)],
        name=name, compiler_params=_cp())(q, k, v, h, alog, dtb, states, invs, do)


GDN_ROWS = 2048


def _gdn_post_fwd(o, h, nw, *, name):
    t = o.shape[1]

    def body(o_ref, g_ref, nw_ref, y_ref):
        oo = o_ref[0]
        r = lax.rsqrt(jnp.mean(oo * oo, -1, keepdims=True) + EPS)
        y_ref[...] = (oo * r * nw_ref[...] * _silu(g_ref[...])).astype(BF16)

    return pl.pallas_call(
        body, grid=(GDN_HEADS, t // GDN_ROWS),
        in_specs=[pl.BlockSpec((1, GDN_ROWS, GDN_DV), lambda hh, i: (hh, i, 0)),
                  pl.BlockSpec((GDN_ROWS, GDN_DV), lambda hh, i: (i, 3 * GDN_HEADS + hh)),
                  pl.BlockSpec((1, GDN_DV), lambda hh, i: (0, 0))],
        out_specs=pl.BlockSpec((GDN_ROWS, GDN_DV), lambda hh, i: (i, hh)),
        out_shape=jax.ShapeDtypeStruct((t, GDN_W), BF16), name=name, compiler_params=_cp())(o, h, nw)


def _gdn_post_bwd(o, h, nw, dy, *, name):
    t = o.shape[1]

    def body(o_ref, g_ref, nw_ref, dy_ref, do_ref, dg_ref, dnw_ref):
        oo, gg, nw_, dy_ = o_ref[0], g_ref[...], nw_ref[...], dy_ref[...]
        r = lax.rsqrt(jnp.mean(oo * oo, -1, keepdims=True) + EPS)
        n = oo * r
        sg = _silu(gg)
        dg_ref[...] = (dy_ * n * nw_ * _dsilu(gg)).astype(BF16)
        dn = dy_ * sg * nw_
        do_ref[0] = r * (dn - n * jnp.mean(dn * n, -1, keepdims=True))

        @pl.when((pl.program_id(0) == 0) & (pl.program_id(1) == 0))
        def _():
            dnw_ref[...] = jnp.zeros_like(dnw_ref)

        dnw_ref[...] += jnp.sum(dy_ * sg * n, 0, keepdims=True)

    return pl.pallas_call(
        body, grid=(GDN_HEADS, t // GDN_ROWS),
        in_specs=[pl.BlockSpec((1, GDN_ROWS, GDN_DV), lambda hh, i: (hh, i, 0)),
                  pl.BlockSpec((GDN_ROWS, GDN_DV), lambda hh, i: (i, 3 * GDN_HEADS + hh)),
                  pl.BlockSpec((1, GDN_DV), lambda hh, i: (0, 0)),
                  pl.BlockSpec((GDN_ROWS, GDN_DV), lambda hh, i: (i, hh))],
        out_specs=[pl.BlockSpec((1, GDN_ROWS, GDN_DV), lambda hh, i: (hh, i, 0)),
                   pl.BlockSpec((GDN_ROWS, GDN_DV), lambda hh, i: (i, hh)),
                   pl.BlockSpec((1, GDN_DV), lambda hh, i: (0, 0))],
        out_shape=[jax.ShapeDtypeStruct((GDN_HEADS, t, GDN_DV), F32), jax.ShapeDtypeStruct((t, GDN_W), BF16),
                   jax.ShapeDtypeStruct((1, GDN_DV), F32)],
        name=name, compiler_params=_cp())(o, h, nw, dy)


def _tables(positions):
    pos = positions.astype(F32)[:, None]
    half = RET_DK // 2
    inv = jnp.power(RET_THETA, -jnp.arange(half, dtype=F32) * 2.0 / RET_DK)
    ang = pos * inv
    cos, sin = jnp.cos(ang), jnp.sin(ang)
    c2a = jnp.concatenate([cos, cos], 1)
    s2a = jnp.concatenate([-sin, sin], 1)
    hb = ROPE_DIMS // 2
    invb = jnp.power(ROPE_THETA, -jnp.arange(hb, dtype=F32) * 2.0 / ROPE_DIMS)
    angb = pos * invb
    cosb, sinb = jnp.cos(angb), jnp.sin(angb)
    t = pos.shape[0]
    ones = jnp.ones((t, DIL_HD - ROPE_DIMS), F32)
    zeros = jnp.zeros((t, DIL_HD - ROPE_DIMS), F32)
    z8 = jnp.zeros((t, hb), F32)
    cb = jnp.concatenate([cosb, cosb, ones] * 2, 1)
    shi = jnp.concatenate([z8, sinb, zeros] * 2, 1)
    slo = jnp.concatenate([-sinb, z8, zeros] * 2, 1)
    lg = jnp.log1p(-jnp.power(2.0, -5.0 - jnp.arange(RET_HEADS, dtype=F32)))
    lgt = jnp.broadcast_to(lg[:, None, None], (RET_HEADS, 1, LANES))
    delta = jnp.arange(ATT_BLK, dtype=jnp.int32)[:, None] + (SEQ - ATT_BLK) - jnp.arange(SEQ, dtype=jnp.int32)[None, :]
    cnt = jnp.zeros(delta.shape, F32)
    for (w, d) in DIL_PAIRS:
        cnt = cnt + ((delta >= 0) & (delta <= w) & (delta % d == 0)).astype(F32)
    strip = jnp.where(cnt > 0, jnp.log(jnp.maximum(cnt, 1.0)), NEG)
    return c2a, s2a, cb, shi, slo, lgt, strip


def _local_step(x, tables, target, get_w, mid, put_g, small):
    c2a, s2a, cb, shi, slo, lgt, strip = tables
    t = x.shape[0]
    saved = []
    xf = x
    xb = x.astype(BF16)
    for layer in range(DEPTH):
        j = layer // 2
        L = f"L{layer}_"
        W, dep = get_w(layer, "mixer", xb)
        rec = {"x": xf, "xb": xb}
        if layer % 2 == 0:
            h = _mm(xb, W["in_t"], tb=True, name=L + "ev_in", dep=dep)
            ro, ya = _ret_fwd(h, c2a, s2a, lgt, name=L + "ret_fwd")
            do_, yb, lse = _dil_fwd(h, cb, shi, slo, strip, name=L + "dil_fwd")
            y = [ya, yb]
            rec.update(h=h, ro=ro, dil_o=do_, lse=lse, y=y)
        else:
            h = _mm(xb, W["in_t"], tb=True, name=L + "od_in", dep=dep)
            cw = W["conv"]
            q, k, v = _gdn_prep_fwd(h, cw, name=L + "gdn_prep")
            alog = jnp.broadcast_to(small["od_a_log"][j][:, None, None], (GDN_HEADS, 1, LANES))
            dtb = jnp.broadcast_to(small["od_dt_bias"][j][:, None, None], (GDN_HEADS, 1, LANES))
            o, states, invs = _gdn_core_fwd(q, k, v, h, alog, dtb, name=L + "gdn_fwd")
            nw = small["od_norm_w"][j][None, :]
            y = [_gdn_post_fwd(o, h, nw, name=L + "gdn_post")]
            rec.update(h=h, q=q, k=k, v=v, alog=alog, dtb=dtb, states=states, invs=invs, o=o, y=y, nw=nw, cw=cw)
        z1, x1, x1b = _mm_ln_fwd(y, W["out"], xf, small["ln1_g"][layer][None], small["ln1_b"][layer][None],
                                 name=L + "out_ln1", dep=mid(layer, "mixer", y[0]))
        rec["Wm"] = W
        W, dep = get_w(layer, "ffn", x1b)
        rec["Wf"] = W
        fcw = W["fconv"]
        fcb = small["ffn_conv_b"][layer][None]
        ug, uv, a = _ffn_up_mid(x1b, W["up_t"], fcw, fcb, name=L + "ffn_up_mid", dep=dep)
        z2, x2, x2b = _mm_ln_fwd([a], W["down"], x1, small["ln2_g"][layer][None], small["ln2_b"][layer][None],
                                 name=L + "down_ln2", dep=mid(layer, "ffn", a))
        rec.update(z1=z1, x1b=x1b, ug=ug, uv=uv, a=a, z2=z2, fcw=fcw, fcb=fcb)
        saved.append(rec)
        xf, xb = x2, x2b

    dy, lossv = _loss_head(xf, target, name="loss_head")
    loss = lossv[0, 0]

    gS = {n: [None] * small[n].shape[0] for n in small}
    below = None
    for layer in reversed(range(DEPTH)):
        j = layer // 2
        L = f"L{layer}_"
        rec = saved[layer]
        Wm, Wf = rec["Wm"], rec["Wf"]
        g = {}
        if below is None:
            dz2, dz2b, dg2, db2 = _ln_bwd(rec["z2"], small["ln2_g"][layer][None], dy, None, name=L + "ln2_bwd")
        else:
            dz2, dz2b, dg2, db2 = _mm_ln_bwd(below[0], below[1], rec["z2"], small["ln2_g"][layer][None], below[2],
                                             name=L + "ln2_bwd", dep=below[3])
        gS["ln2_g"][layer], gS["ln2_b"][layer] = dg2[0], db2[0]
        g["down"] = _mm(rec["a"], dz2b, ta=True, name=L + "ffn_down_dw", out_dtype=BF16)
        du, dcw, dcb = _ffn_mid_bwd(rec["ug"], rec["uv"], rec["fcw"], rec["fcb"], dz2b, Wf["down"], name=L + "ffn_mid_bwd")
        g["fconv"] = dcw.astype(BF16)
        gS["ffn_conv_b"][layer] = dcb[0]
        g["up_t"] = _mm_tn_parts(du, rec["x1b"], name=L + "ffn_up_dw")
        dep = put_g(layer, "ffn", g)
        dz1, dz1b, dg1, db1 = _mm_ln_bwd(du, Wf["up_t"], rec["z1"], small["ln1_g"][layer][None], dz2,
                                         name=L + "ln1_bwd", dep=dep)
        gS["ln1_g"][layer], gS["ln1_b"][layer] = dg1[0], db1[0]
        g = {}
        if layer % 2 == 0:
            g["out"] = _mm_tn_parts(rec["y"], dz1b, name=L + "ev_out_dw")
            dyy = _mm(dz1b, Wm["out"], tb=True, name=L + "ev_out_dx")
            dqa, dka, dva, dga = _ret_bwd(rec["h"], c2a, s2a, lgt, rec["ro"], dyy, name=L + "ret_bwd")
            dqb, dkb, dvb = _dil_bwd(rec["h"], cb, shi, slo, strip, rec["dil_o"], rec["lse"], dyy, name=L + "dil_bwd")
            dh = [dqa, dka, dva, dga, dqb, dkb, dvb]
            g["in_t"] = _mm_tn_parts(dh, rec["xb"], name=L + "ev_in_dw")
            dep = put_g(layer, "mixer", g)
        else:
            g["out"] = _mm(rec["y"][0], dz1b, ta=True, name=L + "od_out_dw", out_dtype=BF16)
            dyy = _mm(dz1b, Wm["out"], tb=True, name=L + "od_out_dx")
            do, dgate, dnw = _gdn_post_bwd(rec["o"], rec["h"], rec["nw"], dyy, name=L + "gdn_post_bwd")
            gS["od_norm_w"][j] = dnw[0]
            dq, dk, dv, dsc, dal, ddt = _gdn_core_bwd(
                rec["q"], rec["k"], rec["v"], rec["h"], rec["alog"], rec["dtb"], rec["states"], rec["invs"], do,
                name=L + "gdn_bwd")
            gS["od_a_log"][j] = dal[:, 0, 0]
            gS["od_dt_bias"][j] = ddt[:, 0, 0]
            dhq, dhk, dhv, dwq, dwk, dwv = _gdn_prep_bwd(rec["h"], rec["cw"], dq, dk, dv, name=L + "gdn_prep_bwd")
            g["conv"] = jnp.concatenate([dwq, dwk, dwv], 1).astype(BF16)
            dh = [dhq, dhk, dhv, dgate, dsc.astype(BF16)]
            g["in_t"] = (_mm_tn_parts(dh[:4], rec["xb"], name=L + "od_in_dw"),
                         _mm(dh[4], rec["xb"], ta=True, name=L + "od_in_dw_logits", out_dtype=BF16))
            dep = put_g(layer, "mixer", g)
        below = (dh, Wm["in_t"], dz1, dep)
    grad_x = _axpy(_mm(jnp.concatenate(below[0], 1), below[1], name="L0_in_dx", dep=below[3]), below[2], name="grad_x")
    gS = {n: jnp.stack(v) for n, v in gS.items()}
    return loss, grad_x, gS


HBM = pl.BlockSpec(memory_space=pltpu.HBM)


def _me():
    return lax.axis_index("x"), lax.axis_index("y"), lax.axis_index("c")


def _my_index():
    x, y, c = _me()
    return 4 * x + 2 * y + c


SEM = pl.BlockSpec(memory_space=pltpu.SEMAPHORE)
ANY = pl.BlockSpec(memory_space=pl.ANY)
PLANS = {"scatter": (1, 2, 3, 4, 5, 6, 7), "spread": (1, 2, 4, 6), "relay": (2, 4, 6), "all": (1, 2, 3, 4, 5, 6, 7)}
ONE_SOURCE = ("spread", "all")
SIBLING = 1


def _peer(kk):
    x, y, c = _me()
    return x ^ (kk >> 2), y ^ ((kk >> 1) & 1), c ^ (kk & 1)


def _peer_index(kk):
    px, py, pc = _peer(kk)
    return 4 * px + 2 * py + pc


def _job_copies(mode, srcs, lands, send_sems, recv_sems, incoming):
    myid = _my_index()
    plan = PLANS[mode]
    out = []
    for a in range(len(lands)):
        for idx, kk in enumerate(plan):
            if mode == "relay":
                to, src = _peer(SIBLING), lands[a].at[_peer_index(kk)]
                slot_there, slot_here = _peer_index(kk), _peer_index(kk ^ SIBLING)
            else:
                to, src = _peer(kk), (srcs[a] if mode in ONE_SOURCE else srcs[a].at[_peer_index(kk)])
                slot_there, slot_here = myid, _peer_index(kk)
            sem = a * len(plan) + idx
            out.append(pltpu.make_async_remote_copy(
                src_ref=src, dst_ref=lands[a].at[slot_here if incoming else slot_there],
                send_sem=send_sems.at[sem], recv_sem=recv_sems.at[sem], device_id=to, device_id_type=MESH))
    return out


def _split_jobs(jobs, arrays):
    out, o = [], 0
    for (_, srcs, lands) in jobs:
        out.append((arrays[o:o + len(srcs)], arrays[o + len(srcs):o + len(srcs) + len(lands)]))
        o += len(srcs) + len(lands)
    return out


def _exchange_start(jobs, after, *, name):
    jobs = [(mode, list(srcs), [lax.empty((N_DEV, *s.shape) if mode in ONE_SOURCE else s.shape, s.dtype) for s in srcs]
             if lands is None else list(lands)) for (mode, srcs, lands) in jobs]
    flat = [a for (_, srcs, lands) in jobs for a in (*srcs, *lands)]
    n, nj = len(flat), len(jobs)
    nsem = [len(PLANS[mode]) * len(lands) for (mode, _, lands) in jobs]

    def body(*refs):
        o = n + (0 if after is None else 1)
        sems, token = refs[o:o + 2 * nj], refs[o + 2 * nj + n]
        for ji, ((mode, _, _), (src, land)) in enumerate(zip(jobs, _split_jobs(jobs, refs[:n]))):
            for cp in _job_copies(mode, src, land, sems[2 * ji], sems[2 * ji + 1], False):
                cp.start()
        token[...] = jnp.zeros_like(token)

    outs = pl.pallas_call(
        body, name=name,
        out_shape=(*[pltpu.SemaphoreType.DMA((ns,)) for ns in nsem for _ in range(2)],
                   *[pltpu.HBM(a.shape, a.dtype) for a in flat], jax.ShapeDtypeStruct((8, LANES), F32)),
        in_specs=[HBM] * n + ([] if after is None else [ANY]),
        out_specs=(*[SEM] * (2 * nj), *[HBM] * n, pl.BlockSpec(memory_space=pltpu.VMEM)),
        input_output_aliases={i: 2 * nj + i for i in range(n)},
        compiler_params=pltpu.CompilerParams(has_side_effects=pltpu.SideEffectType.DATAFLOW_SIDE_EFFECTING),
    )(*[pltpu.with_memory_space_constraint(a, pltpu.HBM) for a in flat], *([] if after is None else [after]))
    thru = _split_jobs(jobs, list(outs[2 * nj:2 * nj + n]))
    started = [(mode, outs[2 * ji], outs[2 * ji + 1], src, land) for ji, ((mode, _, _), (src, land)) in enumerate(zip(jobs, thru))]
    return started, outs[2 * nj + n]


def _exchange_wait(started, after, *, name):
    jobs = [(mode, srcs, lands) for (mode, _, _, srcs, lands) in started]
    flat = [a for (_, srcs, lands) in jobs for a in (*srcs, *lands)]
    n, nj = len(flat), len(jobs)

    def body(*refs):
        sems = refs[n:n + 2 * nj]
        for ji, ((mode, _, _), (src, land)) in enumerate(zip(jobs, _split_jobs(jobs, refs[:n]))):
            for cp in _job_copies(mode, src, land, sems[2 * ji], sems[2 * ji + 1], True):
                cp.wait_send()
                cp.wait_recv()

    outs = pl.pallas_call(
        body, name=name, out_shape=tuple(pltpu.HBM(a.shape, a.dtype) for a in flat),
        in_specs=[HBM] * n + [SEM] * (2 * nj) + [ANY], out_specs=tuple([HBM] * n),
        input_output_aliases={i: i for i in range(n)},
        compiler_params=pltpu.CompilerParams(has_side_effects=pltpu.SideEffectType.DATAFLOW_SIDE_EFFECTING),
    )(*flat, *[s for (_, ss, rs, _, _) in started for s in (ss, rs)], after)
    return _split_jobs(jobs, list(outs))


def _sum8(land, stack, j, depth, *, name):
    _, rr, cc = land.shape
    tr = _row_tile(rr)

    def body(l_ref, *rest):
        o_ref = rest[-1]
        acc = l_ref[0].astype(F32)
        for d in range(1, N_DEV):
            acc = acc + l_ref[d].astype(F32)
        o_ref[0] = acc

    prev = [] if stack is None else [stack]
    return pl.pallas_call(
        body, grid=(rr // tr,),
        in_specs=[pl.BlockSpec((N_DEV, tr, cc), lambda i: (0, i, 0))] + [pl.BlockSpec(memory_space=pl.ANY)] * len(prev),
        out_specs=pl.BlockSpec((1, tr, cc), lambda i: (j, i, 0)), out_shape=jax.ShapeDtypeStruct((depth, rr, cc), F32),
        input_output_aliases={1: 0} if prev else {}, name=name, compiler_params=_cp())(land, *prev)


def _row_tile(rr):
    for cand in (512, 384, 256, 192, 176, 128, 64, 32, 16, 8):
        if rr % cand == 0:
            return cand
    return rr


def _adam_math(w, g, m, v):
    m = ADAM_B1 * m + (1.0 - ADAM_B1) * g
    v = ADAM_B2 * v + (1.0 - ADAM_B2) * (g * g)
    m_hat = m / (1.0 - ADAM_B1 ** ADAM_STEP)
    v_hat = v / (1.0 - ADAM_B2 ** ADAM_STEP)
    delta = -ADAM_LR * (m_hat / (jnp.sqrt(v_hat) + ADAM_EPS) + ADAM_WD * w)
    return delta, m, v


def _adamw_sharded(w, m, v, g, *, name):
    ll, rr, cc = w.shape
    tr = _row_tile(rr)

    def body(w_ref, m_ref, v_ref, g_ref, d_ref, nm_ref, nv_ref):
        d, nm, nv = _adam_math(w_ref[...], g_ref[...], m_ref[...], v_ref[...])
        d_ref[...] = d
        nm_ref[...] = nm
        nv_ref[...] = nv

    blk = pl.BlockSpec((1, tr, cc), lambda l, i: (l, i, 0))
    sh = jax.ShapeDtypeStruct((ll, rr, cc), F32)
    return pl.pallas_call(
        body, grid=(ll, rr // tr), in_specs=[blk] * 4, out_specs=[blk] * 3, out_shape=[sh] * 3,
        name=name, compiler_params=_cp())(w, m, v, g)


def _adamw_small(w, m, v, gall, *, name):
    rr = w.shape[0]

    def body(w_ref, m_ref, v_ref, g_ref, go_ref, d_ref, nm_ref, nv_ref):
        g = g_ref[0]
        for kk in range(1, N_DEV):
            g = g + g_ref[kk]
        d, nm, nv = _adam_math(w_ref[...], g, m_ref[...], v_ref[...])
        go_ref[...] = g
        d_ref[...] = d
        nm_ref[...] = nm
        nv_ref[...] = nv

    sh = jax.ShapeDtypeStruct((rr, LANES), F32)
    return pl.pallas_call(body, out_shape=[sh] * 4, name=name, compiler_params=_cp())(w, m, v, gall)


SHARDED = ("ev_w_in", "ev_w_out", "od_w_in", "od_conv_w", "od_w_out", "ffn_w_up", "ffn_conv_w", "ffn_w_down")
SMALL = ("od_a_log", "od_dt_bias", "od_norm_w", "ffn_conv_b", "ln1_g", "ln1_b", "ln2_g", "ln2_b")
ALL_W = ("ev_w_in", "ev_w_out", "od_w_in", "od_conv_w", "od_a_log", "od_dt_bias", "od_norm_w", "od_w_out",
         "ffn_w_up", "ffn_conv_w", "ffn_conv_b", "ffn_w_down", "ln1_g", "ln1_b", "ln2_g", "ln2_b")


def _layer_items(layer):
    j = layer // 2
    if layer % 2 == 0:
        mixer = [("in_t", "ev_w_in", j, "colT"), ("out", "ev_w_out", j, "row")]
    else:
        mixer = [("in_t", "od_w_in", j, "colT"), ("conv", "od_conv_w", j, "colsmall"), ("out", "od_w_out", j, "row")]
    return mixer + [("up_t", "ffn_w_up", layer, "colT"), ("fconv", "ffn_conv_w", layer, "colsmall"),
                    ("down", "ffn_w_down", layer, "row")]


OD_SHARD = OD_IN // N_DEV
OD_SHARD_PAD = OD_IN_PAD // N_DEV


def _od_pack(g, *, name):
    d = g.shape[-1]

    def body(g_ref, o_ref):
        for n in range(N_DEV):
            o_ref[OD_SHARD * n:OD_SHARD * (n + 1), :] = g_ref[n, 0:OD_SHARD, :]
        o_ref[OD_IN:OD_IN_PAD, :] = jnp.zeros((OD_IN_PAD - OD_IN, d), g.dtype)

    return pl.pallas_call(body, out_shape=jax.ShapeDtypeStruct((OD_IN_PAD, d), g.dtype), name=name,
                          compiler_params=_cp())(g)


def _od_unpack(main, tail, *, name):
    d = main.shape[-1]
    split = main.shape[0]

    def body(m_ref, t_ref, o_ref):
        for n in range(N_DEV):
            lo, hi = OD_SHARD * n, OD_SHARD * (n + 1)
            from_main = min(hi, split) - lo
            o_ref[n, 0:from_main, :] = m_ref[lo:lo + from_main, :]
            if hi > split:
                o_ref[n, from_main:OD_SHARD, :] = t_ref[0:hi - split, :]
            o_ref[n, OD_SHARD:OD_SHARD_PAD, :] = jnp.zeros((OD_SHARD_PAD - OD_SHARD, d), main.dtype)

    return pl.pallas_call(body, out_shape=jax.ShapeDtypeStruct((N_DEV, OD_SHARD_PAD, d), main.dtype), name=name,
                          compiler_params=_cp())(main, tail)


def _to_send(kind, name, w, j):
    if kind == "colT":
        s = w[j].T.astype(BF16)
        return jnp.pad(s, ((0, OD_SHARD_PAD - OD_SHARD), (0, 0))) if name == "od_w_in" else s
    return w[j].astype(BF16) if kind == "row" else w[j]


def _from_gather(kind, name, g, tag):
    if kind == "colsmall":
        return jnp.transpose(g, (1, 0, 2)).reshape(g.shape[1], -1)
    if name == "od_w_in":
        return _od_pack(g, name=tag + "_pack")
    return g.reshape(-1, g.shape[-1])


def _by_owner(kind, name, gfull, tag):
    if kind == "colsmall":
        kk, c8 = gfull.shape
        return jnp.transpose(gfull.reshape(kk, N_DEV, c8 // N_DEV), (1, 0, 2))
    if name == "od_w_in":
        return _od_unpack(*gfull, name=tag + "_unpack")
    return gfull.reshape(N_DEV, gfull.shape[0] // N_DEV, gfull.shape[1])


def _pack_small(d):
    flat = jnp.concatenate([d[n].reshape(-1) for n in SMALL])
    pad = (-flat.shape[0]) % (8 * LANES)
    return jnp.pad(flat, (0, pad)).reshape(-1, LANES)


def _unpack_small(packed, like):
    flat = packed.reshape(-1)
    out, off = {}, 0
    for n in SMALL:
        sz = int(np.prod(like[n].shape))
        out[n] = flat[off:off + sz].reshape(like[n].shape)
        off += sz
    return out


def kernel(x, positions, ev_w_in, ev_w_out, od_w_in, od_conv_w, od_a_log, od_dt_bias, od_norm_w, od_w_out, ffn_w_up, ffn_conv_w, ffn_conv_b, ffn_w_down, ln1_g, ln1_b, ln2_g, ln2_b, loss_target, m_ev_w_in, m_ev_w_out, m_od_w_in, m_od_conv_w, m_od_a_log, m_od_dt_bias, m_od_norm_w, m_od_w_out, m_ffn_w_up, m_ffn_conv_w, m_ffn_conv_b, m_ffn_w_down, m_ln1_g, m_ln1_b, m_ln2_g, m_ln2_b, v_ev_w_in, v_ev_w_out, v_od_w_in, v_od_conv_w, v_od_a_log, v_od_dt_bias, v_od_norm_w, v_od_w_out, v_ffn_w_up, v_ffn_conv_w, v_ffn_conv_b, v_ffn_w_down, v_ln1_g, v_ln1_b, v_ln2_g, v_ln2_b):
    w = dict(ev_w_in=ev_w_in, ev_w_out=ev_w_out, od_w_in=od_w_in, od_conv_w=od_conv_w, od_a_log=od_a_log,
             od_dt_bias=od_dt_bias, od_norm_w=od_norm_w, od_w_out=od_w_out, ffn_w_up=ffn_w_up, ffn_conv_w=ffn_conv_w,
             ffn_conv_b=ffn_conv_b, ffn_w_down=ffn_w_down, ln1_g=ln1_g, ln1_b=ln1_b, ln2_g=ln2_g, ln2_b=ln2_b)
    mom = dict(ev_w_in=m_ev_w_in, ev_w_out=m_ev_w_out, od_w_in=m_od_w_in, od_conv_w=m_od_conv_w, od_a_log=m_od_a_log,
               od_dt_bias=m_od_dt_bias, od_norm_w=m_od_norm_w, od_w_out=m_od_w_out, ffn_w_up=m_ffn_w_up,
               ffn_conv_w=m_ffn_conv_w, ffn_conv_b=m_ffn_conv_b, ffn_w_down=m_ffn_w_down, ln1_g=m_ln1_g,
               ln1_b=m_ln1_b, ln2_g=m_ln2_g, ln2_b=m_ln2_b)
    var = dict(ev_w_in=v_ev_w_in, ev_w_out=v_ev_w_out, od_w_in=v_od_w_in, od_conv_w=v_od_conv_w, od_a_log=v_od_a_log,
               od_dt_bias=v_od_dt_bias, od_norm_w=v_od_norm_w, od_w_out=v_od_w_out, ffn_w_up=v_ffn_w_up,
               ffn_conv_w=v_ffn_conv_w, ffn_conv_b=v_ffn_conv_b, ffn_w_down=v_ffn_w_down, ln1_g=v_ln1_g,
               ln1_b=v_ln1_b, ln2_g=v_ln2_g, ln2_b=v_ln2_b)

    myid = _my_index()
    small = {n: w[n] for n in SMALL}
    groups = [(layer, part) for layer in range(DEPTH) for part in ("mixer", "ffn")]

    def group_items(gi):
        layer, part = groups[gi]
        its = _layer_items(layer)
        return its[:-3] if part == "mixer" else its[-3:]

    level1, level2 = {}, {}

    def spread_job(gi):
        return ("spread", [_to_send(kind, n, w[n], j) for (_, n, j, kind) in group_items(gi)], None)

    def relay(gi, after, name):
        (srcs, lands), = _exchange_wait([level1.pop(gi)], after, name=name + "_wait")
        more = [spread_job(gi + 1)] if gi + 1 < len(groups) else []
        started, token = _exchange_start([("relay", [], lands)] + more, None, name=name + "_start")
        level2[gi] = (started[0], srcs)
        if more:
            level1[gi + 1] = started[1]
        return token

    def get_w(layer, part, after):
        gi = groups.index((layer, part))
        started, srcs = level2.pop(gi)
        (_, lands), = _exchange_wait([started], after, name=f"gather{gi}_wait")
        lands = [lax.dynamic_update_index_in_dim(l, s, myid, 0) for l, s in zip(lands, srcs)]
        return {key: _from_gather(kind, n, l, f"L{layer}_{key}")
                for (key, n, _, kind), l in zip(group_items(gi), lands)}, None

    def mid(layer, part, after):
        gi = groups.index((layer, part)) + 1
        return relay(gi, after, f"gather{gi}_relay") if gi < len(groups) else None

    landed = {}
    pending = []

    def scatter_finish(after):
        started, gi = pending.pop()
        (srcs, lands), = _exchange_wait([started], after, name=f"scatter{gi}_wait")
        for (key, _, _, _), l, s in zip(group_items(gi), lands, srcs):
            own = lax.dynamic_index_in_dim(s, myid, 0, keepdims=False)
            landed[(groups[gi][0], key)] = lax.dynamic_update_index_in_dim(l, own, myid, 0)

    def put_g(layer, part, g):
        gi = groups.index((layer, part))
        srcs = [_by_owner(kind, n, g[key], f"L{layer}_{key}") for (key, n, _, kind) in group_items(gi)]
        (started,), token = _exchange_start([("scatter", srcs, None)], None, name=f"scatter{gi}_start")
        if pending:
            scatter_finish(token)
        pending.append((started, gi))
        return token

    (level1[0],), token = _exchange_start([spread_job(0)], None, name="gather0_spread_start")
    tables = _tables(positions[0] + token[0, 0].astype(jnp.int32))
    relay(0, tables[-1], "gather0_relay")
    loss, grad_x, gS = _local_step(x[0], tables, loss_target[0], get_w, mid, put_g, small)
    loss = lax.psum(loss, ("x", "y", "c"))

    outs_g, outs_d, outs_m, outs_v = {}, {}, {}, {}
    where = {n: [None] * w[n].shape[0] for n in SHARDED}
    for layer in range(DEPTH):
        for (key, n, j, kind) in _layer_items(layer):
            where[n][j] = (layer, key, kind)

    def update(n):
        g = None
        for j, (layer, key, _) in enumerate(where[n]):
            g = _sum8(landed[(layer, key)], g, j, len(where[n]), name=f"L{layer}_{key}_sum")
        if n == "od_w_in":
            g = g[:, :OD_SHARD]
        if where[n][0][2] == "colT":
            tr = lambda a: jnp.swapaxes(a, 1, 2)
            d, nm, nv = _adamw_sharded(tr(w[n]), tr(mom[n]), tr(var[n]), g, name=f"adamw_{n}")
            outs_g[n], outs_d[n], outs_m[n], outs_v[n] = tr(g), tr(d), tr(nm), tr(nv)
        else:
            outs_g[n] = g
            outs_d[n], outs_m[n], outs_v[n] = _adamw_sharded(w[n], mom[n], var[n], g, name=f"adamw_{n}")

    (small_job,), _ = _exchange_start([("all", [_pack_small(gS)], None)], None, name="small_grads_start")
    last = {n for (_, n, _, _) in group_items(pending[0][1])}
    for n in SHARDED:
        if n not in last:
            update(n)
    scatter_finish(outs_d[[n for n in SHARDED if n not in last][-1]])
    for n in SHARDED:
        if n in last:
            update(n)
    ((mine,), (gall,)), = _exchange_wait([small_job], outs_d[[n for n in SHARDED if n in last][-1]], name="small_grads_wait")
    gall = lax.dynamic_update_index_in_dim(gall, mine, myid, 0)
    g, d, nm, nv = _adamw_small(_pack_small({n: w[n] for n in SMALL}), _pack_small({n: mom[n] for n in SMALL}),
                                _pack_small({n: var[n] for n in SMALL}), gall, name="adamw_small")
    for dst, packed in ((outs_g, g), (outs_d, d), (outs_m, nm), (outs_v, nv)):
        dst.update(_unpack_small(packed, {n: w[n] for n in SMALL}))

    return (loss, grad_x[None], *[outs_g[n] for n in ALL_W], *[outs_d[n] for n in ALL_W],
            *[outs_m[n] for n in ALL_W], *[outs_v[n] for n in ALL_W])
```

```python
import functools
import math

import numpy as np
import jax
import jax.numpy as jnp
from jax import lax
from jax.experimental import pallas as pl
from jax.experimental.pallas import tpu as pltpu

F32 = jnp.float32
BF16 = jnp.bfloat16
MESH = pl.DeviceIdType.MESH

D_MODEL = 1024
SEQ = 2048
DEPTH = 4
N_DEV = 8
RET_HEADS, RET_DK, RET_DV = 4, 128, 256
RET_THETA = 10000.0
DIL_HEADS, DIL_HD = 8, 64
DIL_PAIRS = ((128, 1), (512, 4), (2048, 16))
ROPE_THETA = 500000.0
ROPE_DIMS = DIL_HD // 4
GDN_HEADS, GDN_DK, GDN_DV, GDN_CHUNK, GDN_CONV = 8, 128, 128, 64, 4
D_FF = 2816
FFN_CONV = 3
ALPHA = (2.0 * DEPTH) ** 0.25
EPS = 1e-5
RET_QK_W = RET_HEADS * RET_DK
RET_V_W = RET_HEADS * RET_DV
DIL_W = DIL_HEADS * DIL_HD
EV_IN = 2 * RET_QK_W + 2 * RET_V_W + 3 * DIL_W
EV_MIX = RET_V_W + DIL_W
GDN_W = GDN_HEADS * GDN_DK
OD_IN = 4 * GDN_W + 2 * GDN_HEADS
OD_IN_PAD = 4 * GDN_W + 128
ADAM_LR, ADAM_B1, ADAM_B2, ADAM_EPS, ADAM_WD, ADAM_STEP = 0.001, 0.9, 0.999, 1e-08, 0.01, 10

LANES = 128
VMEM_LIMIT = 56 * 1024 * 1024
ATT_BLK = 256
NEG = -1e30


def _cp(**kw):
    return pltpu.CompilerParams(vmem_limit_bytes=VMEM_LIMIT, **kw)


def _tile(n, cap):
    if n <= cap:
        return n
    best = None
    for t in range(LANES, cap + 1, LANES):
        if n % t == 0:
            best = t
    assert best is not None, (n, cap)
    return best


def _mm(a, b, *, ta=False, tb=False, name, out_dtype=F32, dep=None, tm=None, tn=None):
    m = a.shape[1] if ta else a.shape[0]
    k = a.shape[0] if ta else a.shape[1]
    n = b.shape[0] if tb else b.shape[1]
    assert (b.shape[1] if tb else b.shape[0]) == k
    assert a.dtype == BF16 and b.dtype == BF16
    if tn is None:
        tn = n if n <= 1024 else _tile(n, 512)
    if tm is None:
        tm = m if (tn < n and k <= 1024 and m <= 2048) else _tile(m, 512)
    dims = (((0 if ta else 1,), (1 if tb else 0,)), ((), ()))

    def body(a_ref, b_ref, *rest):
        o_ref = rest[-1]
        o_ref[...] = lax.dot_general(a_ref[...], b_ref[...], dims,
                                     preferred_element_type=F32).astype(o_ref.dtype)

    a_spec = pl.BlockSpec((k, tm), lambda i, j: (0, i)) if ta else pl.BlockSpec((tm, k), lambda i, j: (i, 0))
    b_spec = pl.BlockSpec((tn, k), lambda i, j: (j, 0)) if tb else pl.BlockSpec((k, tn), lambda i, j: (0, j))
    extra = [] if dep is None else [dep]
    return pl.pallas_call(
        body, grid=(m // tm, n // tn), in_specs=[a_spec, b_spec] + [pl.BlockSpec(memory_space=pl.ANY)] * len(extra),
        out_specs=pl.BlockSpec((tm, tn), lambda i, j: (i, j)),
        out_shape=jax.ShapeDtypeStruct((m, n), out_dtype), name=name, compiler_params=_cp())(a, b, *extra)


LN_ROWS = 256


def _ln_bwd(z, g, dya, dyb, *, name):
    t, d = z.shape
    two = dyb is not None

    def body(*refs):
        if two:
            z_ref, g_ref, dya_ref, dyb_ref, dz_ref, dzb_ref, dg_ref, db_ref = refs
            dy = dya_ref[...] + ALPHA * dyb_ref[...]
        else:
            z_ref, g_ref, dya_ref, dz_ref, dzb_ref, dg_ref, db_ref = refs
            dy = dya_ref[...]
        zz = z_ref[...]
        mu = jnp.mean(zz, -1, keepdims=True)
        zc = zz - mu
        var = jnp.mean(zc * zc, -1, keepdims=True)
        r = lax.rsqrt(var + EPS)
        xh = zc * r
        dxh = dy * g_ref[...]
        dz = r * (dxh - jnp.mean(dxh, -1, keepdims=True) - xh * jnp.mean(dxh * xh, -1, keepdims=True))
        dz_ref[...] = dz
        dzb_ref[...] = dz.astype(BF16)

        @pl.when(pl.program_id(0) == 0)
        def _():
            dg_ref[...] = jnp.zeros_like(dg_ref)
            db_ref[...] = jnp.zeros_like(db_ref)

        dg_ref[...] += jnp.sum(dy * xh, 0, keepdims=True)
        db_ref[...] += jnp.sum(dy, 0, keepdims=True)

    row = pl.BlockSpec((LN_ROWS, d), lambda i: (i, 0))
    vec = pl.BlockSpec((1, d), lambda i: (0, 0))
    ins = [z, g, dya] + ([dyb] if two else [])
    return pl.pallas_call(
        body, grid=(t // LN_ROWS,), in_specs=[row, vec, row] + ([row] if two else []),
        out_specs=[row, row, vec, vec],
        out_shape=[jax.ShapeDtypeStruct((t, d), F32), jax.ShapeDtypeStruct((t, d), BF16),
                   jax.ShapeDtypeStruct((1, d), F32), jax.ShapeDtypeStruct((1, d), F32)],
        name=name, compiler_params=_cp())(*ins)


def _ln_rows(k):
    return 256 if k > 4096 else 512


def _mm_ln_fwd(parts, w, x, g, b, *, name, dep=None):
    t = parts[0].shape[0]
    offs, k = _part_offsets(parts)
    d = w.shape[1]
    tm = _ln_rows(k)
    npart = len(parts)

    def body(*refs):
        a_refs, w_refs = refs[:npart], refs[npart:2 * npart]
        x_ref, g_ref, b_ref = refs[2 * npart:2 * npart + 3]
        z_ref, y_ref, yb_ref = refs[-3:]
        z = ALPHA * x_ref[...]
        for a_ref, w_ref in zip(a_refs, w_refs):
            z = z + _nn(a_ref[...], w_ref[...])
        mu = jnp.mean(z, -1, keepdims=True)
        zc = z - mu
        var = jnp.mean(zc * zc, -1, keepdims=True)
        y = zc * lax.rsqrt(var + EPS) * g_ref[...] + b_ref[...]
        z_ref[...] = z
        y_ref[...] = y
        yb_ref[...] = y.astype(BF16)

    row = pl.BlockSpec((tm, d), lambda i: (i, 0))
    vec = pl.BlockSpec((1, d), lambda i: (0, 0))
    extra = [] if dep is None else [dep]
    a_specs = [pl.BlockSpec((tm, p.shape[1]), lambda i: (i, 0)) for p in parts]
    w_specs = [pl.BlockSpec((p.shape[1], d), functools.partial(lambda i, blk: (blk, 0), blk=o // p.shape[1]))
               for p, o in zip(parts, offs)]
    return pl.pallas_call(
        body, grid=(t // tm,),
        in_specs=a_specs + w_specs + [row, vec, vec] + [pl.BlockSpec(memory_space=pl.ANY)] * len(extra),
        out_specs=[row, row, row],
        out_shape=[jax.ShapeDtypeStruct((t, d), F32), jax.ShapeDtypeStruct((t, d), F32), jax.ShapeDtypeStruct((t, d), BF16)],
        name=name, compiler_params=_cp())(*parts, *([w] * npart), x, g, b, *extra)


def _part_offsets(parts):
    offs, o = [], 0
    for p in parts:
        assert o % p.shape[1] == 0
        offs.append(o)
        o += p.shape[1]
    return offs, o


def _mm_ln_bwd(parts, w, z, g, dyb, *, name, dep=None):
    t = parts[0].shape[0]
    offs, k = _part_offsets(parts)
    d = w.shape[1]
    tm = _ln_rows(k)
    npart = len(parts)

    def body(*refs):
        a_refs, w_refs = refs[:npart], refs[npart:2 * npart]
        z_ref, g_ref, dyb_ref = refs[2 * npart:2 * npart + 3]
        dz_ref, dzb_ref, dg_ref, db_ref = refs[-4:]
        dy = ALPHA * dyb_ref[...]
        for a_ref, w_ref in zip(a_refs, w_refs):
            dy = dy + _nn(a_ref[...], w_ref[...])
        zz = z_ref[...]
        mu = jnp.mean(zz, -1, keepdims=True)
        zc = zz - mu
        var = jnp.mean(zc * zc, -1, keepdims=True)
        r = lax.rsqrt(var + EPS)
        xh = zc * r
        dxh = dy * g_ref[...]
        dz = r * (dxh - jnp.mean(dxh, -1, keepdims=True) - xh * jnp.mean(dxh * xh, -1, keepdims=True))
        dz_ref[...] = dz
        dzb_ref[...] = dz.astype(BF16)

        @pl.when(pl.program_id(0) == 0)
        def _():
            dg_ref[...] = jnp.zeros_like(dg_ref)
            db_ref[...] = jnp.zeros_like(db_ref)

        dg_ref[...] += jnp.sum(dy * xh, 0, keepdims=True)
        db_ref[...] += jnp.sum(dy, 0, keepdims=True)

    row = pl.BlockSpec((tm, d), lambda i: (i, 0))
    vec = pl.BlockSpec((1, d), lambda i: (0, 0))
    extra = [] if dep is None else [dep]
    a_specs = [pl.BlockSpec((tm, p.shape[1]), lambda i: (i, 0)) for p in parts]
    w_specs = [pl.BlockSpec((p.shape[1], d), functools.partial(lambda i, blk: (blk, 0), blk=o // p.shape[1]))
               for p, o in zip(parts, offs)]
    return pl.pallas_call(
        body, grid=(t // tm,),
        in_specs=a_specs + w_specs + [row, vec, row] + [pl.BlockSpec(memory_space=pl.ANY)] * len(extra),
        out_specs=[row, row, vec, vec],
        out_shape=[jax.ShapeDtypeStruct((t, d), F32), jax.ShapeDtypeStruct((t, d), BF16),
                   jax.ShapeDtypeStruct((1, d), F32), jax.ShapeDtypeStruct((1, d), F32)],
        name=name, compiler_params=_cp())(*parts, *([w] * npart), z, g, dyb, *extra)


def _mm_tn_parts(parts, b, *, name):
    t, n = b.shape
    offs, m = _part_offsets(parts)
    tm = min(_tile(p.shape[1], 1408 if p.shape[1] > 2048 else 512) for p in parts)
    assert all(p.shape[1] % tm == 0 for p in parts)
    first = [o // tm for o in offs]
    count = [p.shape[1] // tm for p in parts]
    npart = len(parts)

    def body(*refs):
        a_refs, b_ref, o_ref = refs[:npart], refs[npart], refs[npart + 1]
        i = pl.program_id(0)
        for a_ref, f, c in zip(a_refs, first, count):
            @pl.when((i >= f) & (i < f + c))
            def _(a_ref=a_ref):
                o_ref[...] = _tn(a_ref[...], b_ref[...]).astype(BF16)

    a_specs = [pl.BlockSpec((t, tm), functools.partial(lambda i, f, c: (0, jnp.clip(i - f, 0, c - 1)), f=f, c=c))
               for f, c in zip(first, count)]
    return pl.pallas_call(
        body, grid=(m // tm,), in_specs=a_specs + [pl.BlockSpec((t, n), lambda i: (0, 0))],
        out_specs=pl.BlockSpec((tm, n), lambda i: (i, 0)), out_shape=jax.ShapeDtypeStruct((m, n), BF16),
        name=name, compiler_params=_cp())(*parts, b)


def _axpy(a, b, *, name):
    t, d = a.shape

    def body(a_ref, b_ref, o_ref):
        o_ref[...] = a_ref[...] + ALPHA * b_ref[...]

    row = pl.BlockSpec((LN_ROWS, d), lambda i: (i, 0))
    return pl.pallas_call(body, grid=(t // LN_ROWS,), in_specs=[row, row], out_specs=row,
                          out_shape=jax.ShapeDtypeStruct((t, d), F32), name=name, compiler_params=_cp())(a, b)


def _loss_head(y, target, *, name):
    t, d = y.shape

    def body(y_ref, t_ref, dy_ref, l_ref):
        e = y_ref[...] - t_ref[...]
        dy_ref[...] = e * (1.0 / d)

        @pl.when(pl.program_id(0) == 0)
        def _():
            l_ref[...] = jnp.zeros_like(l_ref)

        l_ref[...] += jnp.zeros_like(l_ref) + 0.5 * jnp.sum(jnp.mean(e * e, -1, keepdims=True), 0, keepdims=True)

    row = pl.BlockSpec((LN_ROWS, d), lambda i: (i, 0))
    return pl.pallas_call(
        body, grid=(t // LN_ROWS,), in_specs=[row, row],
        out_specs=[row, pl.BlockSpec((1, LANES), lambda i: (0, 0))],
        out_shape=[jax.ShapeDtypeStruct((t, d), F32), jax.ShapeDtypeStruct((1, LANES), F32)],
        name=name, compiler_params=_cp())(y, target)


def _sig(x):
    return 1.0 / (1.0 + jnp.exp(-x))


def _silu(x):
    return x * _sig(x)


def _dsilu(x):
    s = _sig(x)
    return s * (1.0 + x * (1.0 - s))


def _shift_down(u, k, row):
    if k == 0:
        return u
    return jnp.where(row >= k, pltpu.roll(u, k, 0), 0.0)


def _shift_up(u, k, row):
    if k == 0:
        return u
    t = u.shape[0]
    return jnp.where(row < t - k, pltpu.roll(u, t - k, 0), 0.0)


def _dwconv(u, w_ref, row):
    kk = w_ref.shape[0]
    acc = None
    for j in range(kk):
        term = w_ref[j:j + 1, :] * _shift_down(u, kk - 1 - j, row)
        acc = term if acc is None else acc + term
    return acc


def _dwconv_bwd(u, w_ref, dc, row, dw_ref):
    kk = w_ref.shape[0]
    du = None
    for j in range(kk):
        term = w_ref[j:j + 1, :] * _shift_up(dc, kk - 1 - j, row)
        du = term if du is None else du + term
        dw_ref[j:j + 1, :] = jnp.sum(dc * _shift_down(u, kk - 1 - j, row), 0, keepdims=True)
    return du


CONV_ROWS = 1024


def _rows(b):
    return pl.ds(pl.multiple_of(b * CONV_ROWS, CONV_ROWS), CONV_ROWS)


def _shifted_down(ref, b, k, row):
    cur = ref[_rows(b), :]
    if k == 0:
        return cur
    prev = jnp.where(b > 0, ref[_rows(jnp.maximum(b - 1, 0)), :], 0.0)
    return jnp.where(row >= k, pltpu.roll(cur, k, 0), pltpu.roll(prev, k, 0))


def _shifted_up(ref, b, k, row, nblk):
    cur = ref[_rows(b), :]
    if k == 0:
        return cur
    nxt = jnp.where(b < nblk - 1, ref[_rows(jnp.minimum(b + 1, nblk - 1)), :], 0.0)
    return jnp.where(row < CONV_ROWS - k, pltpu.roll(cur, CONV_ROWS - k, 0), pltpu.roll(nxt, CONV_ROWS - k, 0))


def _dwconv_blk(u_ref, w_ref, b, row):
    kk = w_ref.shape[0]
    views = [_shifted_down(u_ref, b, kk - 1 - j, row) for j in range(kk)]
    acc = None
    for j in range(kk):
        term = w_ref[j:j + 1, :] * views[j]
        acc = term if acc is None else acc + term
    return acc, views


def _dwconv_du_blk(dc_ref, w_ref, b, row, nblk):
    kk = w_ref.shape[0]
    du = None
    for j in range(kk):
        term = w_ref[j:j + 1, :] * _shifted_up(dc_ref, b, kk - 1 - j, row, nblk)
        du = term if du is None else du + term
    return du


FFN_TC = 256


def _ffn_up_mid(x, up_t, cw, cb, *, name, dep=None):
    t, d = x.shape
    nb = D_FF // FFN_TC

    def body(x_ref, ugt_ref, uvt_ref, wg_ref, wv_ref, bg_ref, bv_ref, *rest):
        ug_ref, uv_ref, a_ref = rest[-3:]
        xx = x_ref[...]
        row = lax.broadcasted_iota(jnp.int32, (t, FFN_TC), 0)
        ug = _nt(xx, ugt_ref[...])
        ug_ref[...] = ug
        uv = _nt(xx, uvt_ref[...])
        uv_ref[...] = uv
        cg = _dwconv(ug, wg_ref, row) + bg_ref[...]
        cv = _dwconv(uv, wv_ref, row) + bv_ref[...]
        a_ref[...] = (_silu(cg) * cv).astype(BF16)

    col = pl.BlockSpec((t, FFN_TC), lambda j: (0, j))
    wt = lambda off: pl.BlockSpec((FFN_TC, d), lambda j: (j + off, 0))
    wsp = lambda off: pl.BlockSpec((FFN_CONV, FFN_TC), lambda j: (0, j + off))
    bsp = lambda off: pl.BlockSpec((1, FFN_TC), lambda j: (0, j + off))
    extra = [] if dep is None else [dep]
    return pl.pallas_call(
        body, grid=(nb,),
        in_specs=[pl.BlockSpec((t, d), lambda j: (0, 0)), wt(0), wt(nb), wsp(0), wsp(nb), bsp(0), bsp(nb)]
        + [pl.BlockSpec(memory_space=pl.ANY)] * len(extra),
        out_specs=[col, col, col],
        out_shape=[jax.ShapeDtypeStruct((t, D_FF), F32), jax.ShapeDtypeStruct((t, D_FF), F32),
                   jax.ShapeDtypeStruct((t, D_FF), BF16)],
        name=name, compiler_params=_cp())(x, up_t, up_t, cw, cw, cb, cb, *extra)


def _ffn_mid_bwd(ug, uv, cw, cb, dz, down, *, name):
    t, d = dz.shape
    nb = D_FF // FFN_TC

    nblk = t // CONV_ROWS

    def body(ug_ref, uv_ref, wg_ref, wv_ref, bg_ref, bv_ref, dz_ref, dn_ref,
             dug_ref, duv_ref, dwg_ref, dwv_ref, dbg_ref, dbv_ref, da_ref, dcg_s, dcv_s):
        da_ref[...] = _nt(dz_ref[...], dn_ref[...])
        row = lax.broadcasted_iota(jnp.int32, (CONV_ROWS, FFN_TC), 0)
        zero = jnp.zeros((1, FFN_TC), F32)

        def first(b, acc):
            cg, ugs = _dwconv_blk(ug_ref, wg_ref, b, row)
            cv, uvs = _dwconv_blk(uv_ref, wv_ref, b, row)
            cg = cg + bg_ref[...]
            cv = cv + bv_ref[...]
            da_ = da_ref[_rows(b), :]
            dcv = da_ * _silu(cg)
            dcg = da_ * cv * _dsilu(cg)
            dcg_s[_rows(b), :] = dcg
            dcv_s[_rows(b), :] = dcv
            red = [jnp.sum(dcg * s, 0, keepdims=True) for s in ugs] + [jnp.sum(dcg, 0, keepdims=True)]
            red += [jnp.sum(dcv * s, 0, keepdims=True) for s in uvs] + [jnp.sum(dcv, 0, keepdims=True)]
            return tuple(a + r for a, r in zip(acc, red))

        acc = lax.fori_loop(0, nblk, first, (zero,) * (2 * FFN_CONV + 2))
        for j in range(FFN_CONV):
            dwg_ref[j:j + 1, :] = acc[j]
            dwv_ref[j:j + 1, :] = acc[FFN_CONV + 1 + j]
        dbg_ref[...] = acc[FFN_CONV]
        dbv_ref[...] = acc[2 * FFN_CONV + 1]

        def second(b, carry):
            dug_ref[_rows(b), :] = _dwconv_du_blk(dcg_s, wg_ref, b, row, nblk).astype(BF16)
            duv_ref[_rows(b), :] = _dwconv_du_blk(dcv_s, wv_ref, b, row, nblk).astype(BF16)
            return carry

        lax.fori_loop(0, nblk, second, 0)

    col = pl.BlockSpec((t, FFN_TC), lambda j: (0, j))
    wsp = lambda off: pl.BlockSpec((FFN_CONV, FFN_TC), lambda j: (0, j + off))
    bsp = lambda off: pl.BlockSpec((1, FFN_TC), lambda j: (0, j + off))
    outs = pl.pallas_call(
        body, grid=(nb,),
        in_specs=[col, col, wsp(0), wsp(nb), bsp(0), bsp(nb), pl.BlockSpec((t, d), lambda j: (0, 0)),
                  pl.BlockSpec((FFN_TC, d), lambda j: (j, 0))],
        out_specs=[col, col, wsp(0), wsp(0), bsp(0), bsp(0)],
        out_shape=[jax.ShapeDtypeStruct((t, D_FF), BF16), jax.ShapeDtypeStruct((t, D_FF), BF16),
                   jax.ShapeDtypeStruct((FFN_CONV, D_FF), F32), jax.ShapeDtypeStruct((FFN_CONV, D_FF), F32),
                   jax.ShapeDtypeStruct((1, D_FF), F32), jax.ShapeDtypeStruct((1, D_FF), F32)],
        scratch_shapes=[pltpu.VMEM((t, FFN_TC), F32), pltpu.VMEM((t, FFN_TC), F32), pltpu.VMEM((t, FFN_TC), F32)],
        name=name, compiler_params=_cp())(ug, uv, cw, cw, cb, cb, dz, down)
    dug, duv, dwg, dwv, dbg, dbv = outs
    return [dug, duv], jnp.concatenate([dwg, dwv], 1), jnp.concatenate([dbg, dbv], 1)


def _rot_a(x, c2, s2):
    return x * c2 + pltpu.roll(x, RET_DK // 2, 1) * s2


def _rot_a_t(dy, c2, s2):
    return dy * c2 + pltpu.roll(dy * s2, RET_DK // 2, 1)


RET_BWD_BLK = 512


def _decay_tile(lg, blk_diff, blk=ATT_BLK):
    r = lax.broadcasted_iota(jnp.int32, (blk, blk), 0)
    c = lax.broadcasted_iota(jnp.int32, (blk, blk), 1)
    rel = r - c + blk_diff * blk
    return jnp.where(rel >= 0, jnp.exp(jnp.maximum(rel, 0).astype(F32) * lg), 0.0)


def _nt(a, b):
    return lax.dot_general(a, b, (((1,), (1,)), ((), ())), preferred_element_type=F32)


def _nn(a, b):
    return lax.dot_general(a, b, (((1,), (0,)), ((), ())), preferred_element_type=F32)


def _tn(a, b):
    return lax.dot_general(a, b, (((0,), (0,)), ((), ())), preferred_element_type=F32)


def _ret_specs(t):
    q = pl.BlockSpec((t, RET_DK), lambda h: (0, h))
    k = pl.BlockSpec((t, RET_DK), lambda h: (0, RET_HEADS + h))
    v = pl.BlockSpec((t, RET_DV), lambda h: (0, RET_HEADS + h))
    g = pl.BlockSpec((t, RET_DV), lambda h: (0, 2 * RET_HEADS + h))
    tab = pl.BlockSpec((t, RET_DK), lambda h: (0, 0))
    lg = pl.BlockSpec((1, 1, LANES), lambda h: (h, 0, 0))
    return q, k, v, g, tab, lg


def _ret_fwd(h, c2, s2, lgt, *, name):
    t = h.shape[0]
    nblk = t // ATT_BLK
    scale = RET_DK ** -0.5

    def body(q_ref, k_ref, v_ref, g_ref, c_ref, s_ref, lg_ref, o_ref, ya_ref, qs, ks, vs):
        c2_, s2_ = c_ref[...], s_ref[...]
        qs[...] = _rot_a(q_ref[...], c2_, s2_).astype(BF16)
        ks[...] = (_rot_a(k_ref[...], c2_, s2_) * scale).astype(BF16)
        vs[...] = v_ref[...].astype(BF16)
        lg = lg_ref[0, :, 0:1]
        for i in range(nblk):
            qi = qs[pl.ds(i * ATT_BLK, ATT_BLK), :]
            acc = jnp.zeros((ATT_BLK, RET_DV), F32)
            for j in range(i + 1):
                sl = pl.ds(j * ATT_BLK, ATT_BLK)
                s = _nt(qi, ks[sl, :]) * _decay_tile(lg, i - j)
                acc = acc + _nn(s.astype(BF16), vs[sl, :])
            rows = pl.ds(i * ATT_BLK, ATT_BLK)
            o_ref[rows, :] = acc
            r = lax.rsqrt(jnp.mean(acc * acc, -1, keepdims=True) + EPS)
            ya_ref[rows, :] = (acc * r * _silu(g_ref[rows, :])).astype(BF16)

    q, k, v, g, tab, lg = _ret_specs(t)
    out = pl.BlockSpec((t, RET_DV), lambda hh: (0, hh))
    return pl.pallas_call(
        body, grid=(RET_HEADS,), in_specs=[q, k, v, g, tab, tab, lg], out_specs=[out, out],
        out_shape=[jax.ShapeDtypeStruct((t, RET_V_W), F32), jax.ShapeDtypeStruct((t, RET_V_W), BF16)],
        scratch_shapes=[pltpu.VMEM((t, RET_DK), BF16), pltpu.VMEM((t, RET_DK), BF16), pltpu.VMEM((t, RET_DV), BF16)],
        name=name, compiler_params=_cp())(h, h, h, h, c2, s2, lgt)


def _ret_bwd(h, c2, s2, lgt, o, dy, *, name):
    t = h.shape[0]
    blk = RET_BWD_BLK
    nblk = t // blk
    scale = RET_DK ** -0.5

    def body(q_ref, k_ref, v_ref, g_ref, c_ref, s_ref, lg_ref, o_ref, dy_ref,
             dq_ref, dk_ref, dv_ref, dg_ref, qs, ks, vs, dos, dka, dva):
        c2_, s2_ = c_ref[...], s_ref[...]
        qs[...] = _rot_a(q_ref[...], c2_, s2_).astype(BF16)
        ks[...] = (_rot_a(k_ref[...], c2_, s2_) * scale).astype(BF16)
        vs[...] = v_ref[...].astype(BF16)
        lg = lg_ref[0, :, 0:1]
        oo = o_ref[...]
        gg = g_ref[...]
        dya = dy_ref[...]
        r = lax.rsqrt(jnp.mean(oo * oo, -1, keepdims=True) + EPS)
        rn = oo * r
        dg_ref[...] = (dya * rn * _dsilu(gg)).astype(BF16)
        drn = dya * _silu(gg)
        dos[...] = (r * (drn - rn * jnp.mean(drn * rn, -1, keepdims=True))).astype(BF16)
        dka[...] = jnp.zeros_like(dka)
        dva[...] = jnp.zeros_like(dva)
        for i in range(nblk):
            rows = pl.ds(i * blk, blk)
            qi = qs[rows, :]
            doi = dos[rows, :]
            dqa = jnp.zeros((blk, RET_DK), F32)
            for j in range(i + 1):
                sl = pl.ds(j * blk, blk)
                dt_ = _decay_tile(lg, i - j, blk)
                kj = ks[sl, :]
                s = (_nt(qi, kj) * dt_).astype(BF16)
                ds = (_nt(doi, vs[sl, :]) * dt_).astype(BF16)
                dqa = dqa + _nn(ds, kj)
                dka[sl, :] += _tn(ds, qi)
                dva[sl, :] += _tn(s, doi)
            dq_ref[rows, :] = _rot_a_t(dqa, c_ref[rows, :], s_ref[rows, :]).astype(BF16)
        dk_ref[...] = (_rot_a_t(dka[...], c2_, s2_) * scale).astype(BF16)
        dv_ref[...] = dva[...].astype(BF16)

    q, k, v, g, tab, lg = _ret_specs(t)
    blk_v = pl.BlockSpec((t, RET_DV), lambda hh: (0, hh))
    blk_k = pl.BlockSpec((t, RET_DK), lambda hh: (0, hh))
    return pl.pallas_call(
        body, grid=(RET_HEADS,), in_specs=[q, k, v, g, tab, tab, lg, blk_v, blk_v],
        out_specs=[blk_k, blk_k, blk_v, blk_v],
        out_shape=[jax.ShapeDtypeStruct((t, RET_QK_W), BF16), jax.ShapeDtypeStruct((t, RET_QK_W), BF16),
                   jax.ShapeDtypeStruct((t, RET_V_W), BF16), jax.ShapeDtypeStruct((t, RET_V_W), BF16)],
        scratch_shapes=[pltpu.VMEM((t, RET_DK), BF16), pltpu.VMEM((t, RET_DK), BF16), pltpu.VMEM((t, RET_DV), BF16),
                        pltpu.VMEM((t, RET_DV), BF16), pltpu.VMEM((t, RET_DK), F32), pltpu.VMEM((t, RET_DV), F32)],
        name=name, compiler_params=_cp())(h, h, h, h, c2, s2, lgt, o, dy)


def _rot_b(x, cb, shi, slo):
    return x * cb + pltpu.roll(x, ROPE_DIMS // 2, 1) * shi + pltpu.roll(x, LANES - ROPE_DIMS // 2, 1) * slo


def _rot_b_t(dy, cb, shi, slo):
    return dy * cb + pltpu.roll(dy * shi, LANES - ROPE_DIMS // 2, 1) + pltpu.roll(dy * slo, ROPE_DIMS // 2, 1)


def _dil_specs(t):
    base = (2 * RET_QK_W + 2 * RET_V_W) // LANES
    npair = DIL_W // LANES
    q = pl.BlockSpec((t, LANES), lambda p: (0, base + p))
    k = pl.BlockSpec((t, LANES), lambda p: (0, base + npair + p))
    v = pl.BlockSpec((t, LANES), lambda p: (0, base + 2 * npair + p))
    tab = pl.BlockSpec((t, LANES), lambda p: (0, 0))
    strip = pl.BlockSpec((ATT_BLK, t), lambda p: (0, 0))
    pair = pl.BlockSpec((t, LANES), lambda p: (0, p))
    return q, k, v, tab, strip, pair


def _dil_fwd(h, cb, shi, slo, strip, *, name):
    t = h.shape[0]
    nblk = t // ATT_BLK
    scale = DIL_HD ** -0.5

    def body(q_ref, k_ref, v_ref, cb_ref, shi_ref, slo_ref, st_ref, o_ref, yb_ref, lse_ref, qs, ks, vs):
        cb_, shi_, slo_ = cb_ref[...], shi_ref[...], slo_ref[...]
        lane = lax.broadcasted_iota(jnp.int32, (t, LANES), 1)
        qr = _rot_b(q_ref[...], cb_, shi_, slo_) * scale
        qs[0] = jnp.where(lane < DIL_HD, qr, 0.0).astype(BF16)
        qs[1] = jnp.where(lane >= DIL_HD, qr, 0.0).astype(BF16)
        ks[...] = _rot_b(k_ref[...], cb_, shi_, slo_).astype(BF16)
        vs[...] = v_ref[...].astype(BF16)
        lane_b = lax.broadcasted_iota(jnp.int32, (ATT_BLK, LANES), 1)
        for i in range(nblk):
            w = (i + 1) * ATT_BLK
            rows = pl.ds(i * ATT_BLK, ATT_BLK)
            logc = st_ref[:, t - w:t]
            outs, lses = [], []
            for hd in range(2):
                s = _nt(qs[hd, rows, :], ks[0:w, :]) + logc
                m = jnp.max(s, -1, keepdims=True)
                p = jnp.exp(s - m)
                l = jnp.sum(p, -1, keepdims=True)
                outs.append(_nn(p.astype(BF16), vs[0:w, :]) / l)
                lses.append(m + jnp.log(l))
            o = jnp.where(lane_b < DIL_HD, outs[0], outs[1])
            o_ref[rows, :] = o
            yb_ref[rows, :] = o.astype(BF16)
            lse_ref[rows, :] = jnp.where(lane_b < DIL_HD, lses[0], lses[1])

    q, k, v, tab, strip_spec, pair = _dil_specs(t)
    return pl.pallas_call(
        body, grid=(DIL_W // LANES,), in_specs=[q, k, v, tab, tab, tab, strip_spec], out_specs=[pair, pair, pair],
        out_shape=[jax.ShapeDtypeStruct((t, DIL_W), F32), jax.ShapeDtypeStruct((t, DIL_W), BF16),
                   jax.ShapeDtypeStruct((t, DIL_W), F32)],
        scratch_shapes=[pltpu.VMEM((2, t, LANES), BF16), pltpu.VMEM((t, LANES), BF16), pltpu.VMEM((t, LANES), BF16)],
        name=name, compiler_params=_cp())(h, h, h, cb, shi, slo, strip)


def _dil_bwd(h, cb, shi, slo, strip, o, lse, dy, *, name):
    t = h.shape[0]
    nblk = t // ATT_BLK
    scale = DIL_HD ** -0.5

    def body(q_ref, k_ref, v_ref, cb_ref, shi_ref, slo_ref, st_ref, o_ref, lse_ref, dy_ref,
             dq_ref, dk_ref, dv_ref, qs, ks, vs, dos, dls, dka, dva):
        cb_, shi_, slo_ = cb_ref[...], shi_ref[...], slo_ref[...]
        lane = lax.broadcasted_iota(jnp.int32, (t, LANES), 1)
        qr = _rot_b(q_ref[...], cb_, shi_, slo_) * scale
        qs[0] = jnp.where(lane < DIL_HD, qr, 0.0).astype(BF16)
        qs[1] = jnp.where(lane >= DIL_HD, qr, 0.0).astype(BF16)
        ks[...] = _rot_b(k_ref[...], cb_, shi_, slo_).astype(BF16)
        vs[...] = v_ref[...].astype(BF16)
        do = dy_ref[...]
        prod = do * o_ref[...]
        d0 = jnp.sum(jnp.where(lane < DIL_HD, prod, 0.0), -1, keepdims=True)
        d1 = jnp.sum(jnp.where(lane >= DIL_HD, prod, 0.0), -1, keepdims=True)
        dls[...] = jnp.where(lane < DIL_HD, d0, d1)
        dos[0] = jnp.where(lane < DIL_HD, do, 0.0).astype(BF16)
        dos[1] = jnp.where(lane >= DIL_HD, do, 0.0).astype(BF16)
        dka[...] = jnp.zeros_like(dka)
        dva[...] = jnp.zeros_like(dva)
        lane_b = lax.broadcasted_iota(jnp.int32, (ATT_BLK, LANES), 1)
        for i in range(nblk):
            w = (i + 1) * ATT_BLK
            rows = pl.ds(i * ATT_BLK, ATT_BLK)
            logc = st_ref[:, t - w:t]
            dqs = []
            for hd in range(2):
                col = hd * DIL_HD
                qh = qs[hd, rows, :]
                doh = dos[hd, rows, :]
                lse_h = lse_ref[rows, col:col + 1]
                dl_h = dls[rows, col:col + 1]
                p = jnp.exp(_nt(qh, ks[0:w, :]) + logc - lse_h)
                dp = _nt(doh, vs[0:w, :])
                ds = (p * (dp - dl_h)).astype(BF16)
                dqs.append(_nn(ds, ks[0:w, :]))
                dka[0:w, :] += _tn(ds, qh)
                dva[0:w, :] += _tn(p.astype(BF16), doh)
            dq = jnp.where(lane_b < DIL_HD, dqs[0], dqs[1]) * scale
            dq_ref[rows, :] = _rot_b_t(dq, cb_ref[rows, :], shi_ref[rows, :], slo_ref[rows, :]).astype(BF16)
        dk_ref[...] = _rot_b_t(dka[...], cb_, shi_, slo_).astype(BF16)
        dv_ref[...] = dva[...].astype(BF16)

    q, k, v, tab, strip_spec, pair = _dil_specs(t)
    dy_spec = pl.BlockSpec((t, LANES), lambda p: (0, RET_V_W // LANES + p))
    return pl.pallas_call(
        body, grid=(DIL_W // LANES,), in_specs=[q, k, v, tab, tab, tab, strip_spec, pair, pair, dy_spec],
        out_specs=[pair, pair, pair],
        out_shape=[jax.ShapeDtypeStruct((t, DIL_W), BF16)] * 3,
        scratch_shapes=[pltpu.VMEM((2, t, LANES), BF16), pltpu.VMEM((t, LANES), BF16), pltpu.VMEM((t, LANES), BF16),
                        pltpu.VMEM((2, t, LANES), BF16), pltpu.VMEM((t, LANES), F32),
                        pltpu.VMEM((t, LANES), F32), pltpu.VMEM((t, LANES), F32)],
        name=name, compiler_params=_cp())(h, h, h, cb, shi, slo, strip, o, lse, dy)


def _gdn_prep_fwd(h, cw, *, name):
    t = h.shape[0]
    qscale = GDN_DK ** -0.5

    def body(hq_ref, hk_ref, hv_ref, wq_ref, wk_ref, wv_ref, q_ref, k_ref, v_ref):
        row = lax.broadcasted_iota(jnp.int32, (t, GDN_DK), 0)
        sq = _silu(_dwconv(hq_ref[...], wq_ref, row))
        sk = _silu(_dwconv(hk_ref[...], wk_ref, row))
        q_ref[0] = sq * lax.rsqrt(jnp.sum(sq * sq, -1, keepdims=True) + 1e-6) * qscale
        k_ref[0] = sk * lax.rsqrt(jnp.sum(sk * sk, -1, keepdims=True) + 1e-6)
        v_ref[0] = _silu(_dwconv(hv_ref[...], wv_ref, row))

    hs = lambda off: pl.BlockSpec((t, GDN_DK), lambda i: (0, i + off))
    ws = lambda off: pl.BlockSpec((GDN_CONV, GDN_DK), lambda i: (0, i + off))
    out = pl.BlockSpec((1, t, GDN_DK), lambda i: (i, 0, 0))
    return pl.pallas_call(
        body, grid=(GDN_HEADS,), in_specs=[hs(0), hs(8), hs(16), ws(0), ws(8), ws(16)], out_specs=[out, out, out],
        out_shape=[jax.ShapeDtypeStruct((GDN_HEADS, t, GDN_DK), F32)] * 3,
        name=name, compiler_params=_cp())(h, h, h, cw, cw, cw)


def _gdn_prep_bwd(h, cw, dq, dk, dv, *, name):
    t = h.shape[0]
    qscale = GDN_DK ** -0.5

    def body(hq_ref, hk_ref, hv_ref, wq_ref, wk_ref, wv_ref, dq_ref, dk_ref, dv_ref,
             dhq_ref, dhk_ref, dhv_ref, dwq_ref, dwk_ref, dwv_ref):
        row = lax.broadcasted_iota(jnp.int32, (t, GDN_DK), 0)

        def one(h_ref, w_ref, d_ref, dh_ref, dw_ref, norm, sc):
            u = h_ref[...]
            c = _dwconv(u, w_ref, row)
            d = d_ref[0]
            if norm:
                s = _silu(c)
                r = lax.rsqrt(jnp.sum(s * s, -1, keepdims=True) + 1e-6)
                n = s * r
                d = d * sc
                d = r * (d - n * jnp.sum(d * n, -1, keepdims=True))
            dc = d * _dsilu(c)
            dh_ref[...] = _dwconv_bwd(u, w_ref, dc, row, dw_ref).astype(BF16)

        one(hq_ref, wq_ref, dq_ref, dhq_ref, dwq_ref, True, qscale)
        one(hk_ref, wk_ref, dk_ref, dhk_ref, dwk_ref, True, 1.0)
        one(hv_ref, wv_ref, dv_ref, dhv_ref, dwv_ref, False, 1.0)

    hs = lambda off: pl.BlockSpec((t, GDN_DK), lambda i: (0, i + off))
    ws = lambda off: pl.BlockSpec((GDN_CONV, GDN_DK), lambda i: (0, i + off))
    hd = pl.BlockSpec((1, t, GDN_DK), lambda i: (i, 0, 0))
    return pl.pallas_call(
        body, grid=(GDN_HEADS,), in_specs=[hs(0), hs(8), hs(16), ws(0), ws(8), ws(16), hd, hd, hd],
        out_specs=[hs(0), hs(0), hs(0), ws(0), ws(0), ws(0)],
        out_shape=[jax.ShapeDtypeStruct((t, GDN_W), BF16)] * 3 + [jax.ShapeDtypeStruct((GDN_CONV, GDN_W), F32)] * 3,
        name=name, compiler_params=_cp())(h, h, h, cw, cw, cw, dq, dk, dv)


def _make_mm2(wide):
    def raw(a, b, dims):
        if wide:
            return lax.dot_general(a, b, (dims, ((), ())), precision=lax.Precision.HIGHEST, preferred_element_type=F32)
        return lax.dot_general(a.astype(BF16), b.astype(BF16), (dims, ((), ())), preferred_element_type=F32)

    @jax.custom_vjp
    def nn(a, b):
        return raw(a, b, ((1,), (0,)))

    @jax.custom_vjp
    def nt(a, b):
        return raw(a, b, ((1,), (1,)))

    @jax.custom_vjp
    def tn(a, b):
        return raw(a, b, ((0,), (0,)))

    nn.defvjp(lambda a, b: (nn(a, b), (a, b)), lambda r, g: (nt(g, r[1]), tn(r[0], g)))
    nt.defvjp(lambda a, b: (nt(a, b), (a, b)), lambda r, g: (nn(g, r[1]), tn(g, r[0])))
    tn.defvjp(lambda a, b: (tn(a, b), (a, b)), lambda r, g: (nt(r[1], g), nn(r[0], g)))
    return nn, nt, tn


_NN, _NT, _TN = _make_mm2(False)
_NNW, _NTW, _TNW = _make_mm2(True)


def _square_masks(c):
    ri = lax.broadcasted_iota(jnp.int32, (c, c), 0)
    ci = lax.broadcasted_iota(jnp.int32, (c, c), 1)
    return ri >= ci, ri > ci, ri == ci


def _cumsum_rows(m):
    tri, _, _ = _square_masks(m.shape[0])
    return _NNW(tri.astype(F32), m)


def _transpose_sq(m):
    _, _, eye = _square_masks(m.shape[0])
    return _NTW(eye.astype(F32), m)


@jax.custom_vjp
def _inv_unit_lower(l):
    c = l.shape[0]
    _, _, eye = _square_masks(c)
    p = -l
    t = eye.astype(F32) + p
    for _ in range(int(math.log2(c)) - 1):
        p = _NNW(p, p)
        t = t + _NNW(t, p)
    return t


def _inv_fwd(l):
    t = _inv_unit_lower(l)
    return t, t


def _inv_bwd(t, dt):
    return (-_NTW(_TNW(t, dt), t),)


_inv_unit_lower.defvjp(_inv_fwd, _inv_bwd)


@jax.custom_vjp
def _inv_known(l, t):
    return t


_inv_known.defvjp(lambda l, t: (t, t), lambda t, dt: (_inv_bwd(t, dt)[0], jnp.zeros_like(t)))


def _softplus(x):
    return jnp.maximum(x, 0.0) + jnp.log1p(jnp.exp(-jnp.abs(x)))


def _gdn_chunk(q, k, v, braw, araw, alog, dtb, state, inv=None):
    c = q.shape[0]
    dv = v.shape[1]
    tri, strict, _ = _square_masks(c)
    beta = _sig(braw)
    g = -jnp.exp(alog) * _softplus(araw + dtb)
    gcm = _cumsum_rows(g * jnp.ones((c, c), F32))
    gct = _transpose_sq(gcm)
    decay = jnp.where(tri, jnp.exp(jnp.where(tri, gcm - gct, 0.0)), 0.0)
    gc = jnp.sum(gcm, 1, keepdims=True) * (1.0 / c)
    glast = jnp.sum(g, 0, keepdims=True)
    egc = jnp.exp(gc)
    kb = k * beta
    low = jnp.where(strict, _NT(kb, k) * decay, 0.0)
    tm = _inv_unit_lower(low) if inv is None else _inv_known(low, inv)
    sol = _NNW(tm, jnp.concatenate([v * beta, kb * egc], 1))
    u, w = sol[:, :dv], sol[:, dv:]
    attn = jnp.where(tri, _NT(q, k) * decay, 0.0)
    k_dec = k * jnp.exp(glast - gc)
    q_dec = q * egc
    v_new = u - _NN(w, state)
    o = _NN(q_dec, state) + _NN(attn, v_new)
    new_state = state * jnp.exp(glast) + _TN(k_dec, v_new)
    return o, new_state, tm


def _gdn_specs(t, rev):
    nch = t // GDN_CHUNK
    cm = (lambda n: nch - 1 - n) if rev else (lambda n: n)
    tok = pl.BlockSpec((GDN_HEADS, GDN_CHUNK, GDN_DK), lambda n: (0, cm(n), 0))
    par = pl.BlockSpec((GDN_HEADS, 1, LANES), lambda n: (0, 0, 0))
    st = pl.BlockSpec((GDN_HEADS, 1, GDN_DK, GDN_DV), lambda n: (0, cm(n), 0, 0))
    inv = pl.BlockSpec((GDN_HEADS, GDN_CHUNK, GDN_CHUNK), lambda n: (0, cm(n), 0))
    sc = pl.BlockSpec((GDN_CHUNK, LANES), lambda n: (cm(n), 4 * GDN_W // LANES))
    return tok, par, st, inv, sc


def _head_columns(sc_ref, first):
    return jnp.stack([sc_ref[:, first + hh:first + hh + 1] for hh in range(GDN_HEADS)])


def _gdn_core_fwd(q, k, v, h, alog, dtb, *, name):
    t = q.shape[1]
    nch = t // GDN_CHUNK

    def body(q_ref, k_ref, v_ref, sc_ref, al_ref, dt_ref, o_ref, st_ref, inv_ref, state):
        @pl.when(pl.program_id(0) == 0)
        def _():
            state[...] = jnp.zeros_like(state)

        s0 = state[...]
        st_ref[:, 0] = s0
        o, s1, tm = jax.vmap(_gdn_chunk)(q_ref[...], k_ref[...], v_ref[...], _head_columns(sc_ref, 0),
                                         _head_columns(sc_ref, GDN_HEADS), al_ref[:, :, 0:1], dt_ref[:, :, 0:1], s0)
        o_ref[...] = o
        inv_ref[...] = tm
        state[...] = s1

    tok, par, st, inv, sc = _gdn_specs(t, False)
    return pl.pallas_call(
        body, grid=(nch,), in_specs=[tok, tok, tok, sc, par, par], out_specs=[tok, st, inv],
        out_shape=[jax.ShapeDtypeStruct((GDN_HEADS, t, GDN_DV), F32),
                   jax.ShapeDtypeStruct((GDN_HEADS, nch, GDN_DK, GDN_DV), F32),
                   jax.ShapeDtypeStruct((GDN_HEADS, t, GDN_CHUNK), F32)],
        scratch_shapes=[pltpu.VMEM((GDN_HEADS, GDN_DK, GDN_DV), F32)],
        name=name, compiler_params=_cp())(q, k, v, h, alog, dtb)


def _gdn_core_bwd(q, k, v, h, alog, dtb, states, invs, do, *, name):
    t = q.shape[1]
    nch = t // GDN_CHUNK

    def body(q_ref, k_ref, v_ref, sc_ref, al_ref, dt_ref, st_ref, inv_ref, do_ref,
             dq_ref, dk_ref, dv_ref, dsc_ref, dal_ref, ddt_ref, dstate):
        @pl.when(pl.program_id(0) == 0)
        def _():
            dstate[...] = jnp.zeros_like(dstate)
            dal_ref[...] = jnp.zeros_like(dal_ref)
            ddt_ref[...] = jnp.zeros_like(ddt_ref)

        args = (q_ref[...], k_ref[...], v_ref[...], _head_columns(sc_ref, 0), _head_columns(sc_ref, GDN_HEADS),
                al_ref[:, :, 0:1], dt_ref[:, :, 0:1], st_ref[:, 0])
        tm = inv_ref[...]

        def chunk(*a):
            return jax.vmap(_gdn_chunk)(*a, tm)[:2]

        _, pull = jax.vjp(chunk, *args)
        dq, dk, dv, dbr, dar, dal, ddt, ds = pull((do_ref[...], dstate[...]))
        dq_ref[...] = dq
        dk_ref[...] = dk
        dv_ref[...] = dv
        lane = lax.broadcasted_iota(jnp.int32, (GDN_CHUNK, LANES), 1)
        dsc = jnp.zeros((GDN_CHUNK, LANES), F32)
        for hh in range(GDN_HEADS):
            dsc = jnp.where(lane == hh, dbr[hh], dsc)
            dsc = jnp.where(lane == GDN_HEADS + hh, dar[hh], dsc)
        dsc_ref[...] = dsc
        dal_ref[...] += dal + jnp.zeros((GDN_HEADS, 1, LANES), F32)
        ddt_ref[...] += ddt + jnp.zeros((GDN_HEADS, 1, LANES), F32)
        dstate[...] = ds

    tok, par, st, inv, sc = _gdn_specs(t, True)
    tokshape = jax.ShapeDtypeStruct((GDN_HEADS, t, GDN_DK), F32)
    parshape = jax.ShapeDtypeStruct((GDN_HEADS, 1, LANES), F32)
    nch_map = pl.BlockSpec((GDN_CHUNK, LANES), lambda n: (nch - 1 - n, 0))
    return pl.pallas_call(
        body, grid=(nch,), in_specs=[tok, tok, tok, sc, par, par, st, inv, tok],
        out_specs=[tok, tok, tok, nch_map, par, par],
        out_shape=[tokshape] * 3 + [jax.ShapeDtypeStruct((t, LANES), F32)] + [parshape] * 2,
        scratch_shapes=[pltpu.VMEM((GDN_HEADS, GDN_DK, GDN_DV), F32)],
        name=name, compiler_params=_cp())(q, k, v, h, alog, dtb, states, invs, do)


GDN_ROWS = 2048


def _gdn_post_fwd(o, h, nw, *, name):
    t = o.shape[1]

    def body(o_ref, g_ref, nw_ref, y_ref):
        oo = o_ref[0]
        r = lax.rsqrt(jnp.mean(oo * oo, -1, keepdims=True) + EPS)
        y_ref[...] = (oo * r * nw_ref[...] * _silu(g_ref[...])).astype(BF16)

    return pl.pallas_call(
        body, grid=(GDN_HEADS, t // GDN_ROWS),
        in_specs=[pl.BlockSpec((1, GDN_ROWS, GDN_DV), lambda hh, i: (hh, i, 0)),
                  pl.BlockSpec((GDN_ROWS, GDN_DV), lambda hh, i: (i, 3 * GDN_HEADS + hh)),
                  pl.BlockSpec((1, GDN_DV), lambda hh, i: (0, 0))],
        out_specs=pl.BlockSpec((GDN_ROWS, GDN_DV), lambda hh, i: (i, hh)),
        out_shape=jax.ShapeDtypeStruct((t, GDN_W), BF16), name=name, compiler_params=_cp())(o, h, nw)


def _gdn_post_bwd(o, h, nw, dy, *, name):
    t = o.shape[1]

    def body(o_ref, g_ref, nw_ref, dy_ref, do_ref, dg_ref, dnw_ref):
        oo, gg, nw_, dy_ = o_ref[0], g_ref[...], nw_ref[...], dy_ref[...]
        r = lax.rsqrt(jnp.mean(oo * oo, -1, keepdims=True) + EPS)
        n = oo * r
        sg = _silu(gg)
        dg_ref[...] = (dy_ * n * nw_ * _dsilu(gg)).astype(BF16)
        dn = dy_ * sg * nw_
        do_ref[0] = r * (dn - n * jnp.mean(dn * n, -1, keepdims=True))

        @pl.when((pl.program_id(0) == 0) & (pl.program_id(1) == 0))
        def _():
            dnw_ref[...] = jnp.zeros_like(dnw_ref)

        dnw_ref[...] += jnp.sum(dy_ * sg * n, 0, keepdims=True)

    return pl.pallas_call(
        body, grid=(GDN_HEADS, t // GDN_ROWS),
        in_specs=[pl.BlockSpec((1, GDN_ROWS, GDN_DV), lambda hh, i: (hh, i, 0)),
                  pl.BlockSpec((GDN_ROWS, GDN_DV), lambda hh, i: (i, 3 * GDN_HEADS + hh)),
                  pl.BlockSpec((1, GDN_DV), lambda hh, i: (0, 0)),
                  pl.BlockSpec((GDN_ROWS, GDN_DV), lambda hh, i: (i, hh))],
        out_specs=[pl.BlockSpec((1, GDN_ROWS, GDN_DV), lambda hh, i: (hh, i, 0)),
                   pl.BlockSpec((GDN_ROWS, GDN_DV), lambda hh, i: (i, hh)),
                   pl.BlockSpec((1, GDN_DV), lambda hh, i: (0, 0))],
        out_shape=[jax.ShapeDtypeStruct((GDN_HEADS, t, GDN_DV), F32), jax.ShapeDtypeStruct((t, GDN_W), BF16),
                   jax.ShapeDtypeStruct((1, GDN_DV), F32)],
        name=name, compiler_params=_cp())(o, h, nw, dy)


def _tables(positions):
    pos = positions.astype(F32)[:, None]
    half = RET_DK // 2
    inv = jnp.power(RET_THETA, -jnp.arange(half, dtype=F32) * 2.0 / RET_DK)
    ang = pos * inv
    cos, sin = jnp.cos(ang), jnp.sin(ang)
    c2a = jnp.concatenate([cos, cos], 1)
    s2a = jnp.concatenate([-sin, sin], 1)
    hb = ROPE_DIMS // 2
    invb = jnp.power(ROPE_THETA, -jnp.arange(hb, dtype=F32) * 2.0 / ROPE_DIMS)
    angb = pos * invb
    cosb, sinb = jnp.cos(angb), jnp.sin(angb)
    t = pos.shape[0]
    ones = jnp.ones((t, DIL_HD - ROPE_DIMS), F32)
    zeros = jnp.zeros((t, DIL_HD - ROPE_DIMS), F32)
    z8 = jnp.zeros((t, hb), F32)
    cb = jnp.concatenate([cosb, cosb, ones] * 2, 1)
    shi = jnp.concatenate([z8, sinb, zeros] * 2, 1)
    slo = jnp.concatenate([-sinb, z8, zeros] * 2, 1)
    lg = jnp.log1p(-jnp.power(2.0, -5.0 - jnp.arange(RET_HEADS, dtype=F32)))
    lgt = jnp.broadcast_to(lg[:, None, None], (RET_HEADS, 1, LANES))
    delta = jnp.arange(ATT_BLK, dtype=jnp.int32)[:, None] + (SEQ - ATT_BLK) - jnp.arange(SEQ, dtype=jnp.int32)[None, :]
    cnt = jnp.zeros(delta.shape, F32)
    for (w, d) in DIL_PAIRS:
        cnt = cnt + ((delta >= 0) & (delta <= w) & (delta % d == 0)).astype(F32)
    strip = jnp.where(cnt > 0, jnp.log(jnp.maximum(cnt, 1.0)), NEG)
    return c2a, s2a, cb, shi, slo, lgt, strip


def _local_step(x, tables, target, get_w, mid, put_g, small):
    c2a, s2a, cb, shi, slo, lgt, strip = tables
    t = x.shape[0]
    saved = []
    xf = x
    xb = x.astype(BF16)
    for layer in range(DEPTH):
        j = layer // 2
        L = f"L{layer}_"
        W, dep = get_w(layer, "mixer", xb)
        rec = {"x": xf, "xb": xb}
        if layer % 2 == 0:
            h = _mm(xb, W["in_t"], tb=True, name=L + "ev_in", dep=dep)
            ro, ya = _ret_fwd(h, c2a, s2a, lgt, name=L + "ret_fwd")
            do_, yb, lse = _dil_fwd(h, cb, shi, slo, strip, name=L + "dil_fwd")
            y = [ya, yb]
            rec.update(h=h, ro=ro, dil_o=do_, lse=lse, y=y)
        else:
            h = _mm(xb, W["in_t"], tb=True, name=L + "od_in", dep=dep)
            cw = W["conv"]
            q, k, v = _gdn_prep_fwd(h, cw, name=L + "gdn_prep")
            alog = jnp.broadcast_to(small["od_a_log"][j][:, None, None], (GDN_HEADS, 1, LANES))
            dtb = jnp.broadcast_to(small["od_dt_bias"][j][:, None, None], (GDN_HEADS, 1, LANES))
            o, states, invs = _gdn_core_fwd(q, k, v, h, alog, dtb, name=L + "gdn_fwd")
            nw = small["od_norm_w"][j][None, :]
            y = [_gdn_post_fwd(o, h, nw, name=L + "gdn_post")]
            rec.update(h=h, q=q, k=k, v=v, alog=alog, dtb=dtb, states=states, invs=invs, o=o, y=y, nw=nw, cw=cw)
        z1, x1, x1b = _mm_ln_fwd(y, W["out"], xf, small["ln1_g"][layer][None], small["ln1_b"][layer][None],
                                 name=L + "out_ln1", dep=mid(layer, "mixer", y[0]))
        rec["Wm"] = W
        W, dep = get_w(layer, "ffn", x1b)
        rec["Wf"] = W
        fcw = W["fconv"]
        fcb = small["ffn_conv_b"][layer][None]
        ug, uv, a = _ffn_up_mid(x1b, W["up_t"], fcw, fcb, name=L + "ffn_up_mid", dep=dep)
        z2, x2, x2b = _mm_ln_fwd([a], W["down"], x1, small["ln2_g"][layer][None], small["ln2_b"][layer][None],
                                 name=L + "down_ln2", dep=mid(layer, "ffn", a))
        rec.update(z1=z1, x1b=x1b, ug=ug, uv=uv, a=a, z2=z2, fcw=fcw, fcb=fcb)
        saved.append(rec)
        xf, xb = x2, x2b

    dy, lossv = _loss_head(xf, target, name="loss_head")
    loss = lossv[0, 0]

    gS = {n: [None] * small[n].shape[0] for n in small}
    below = None
    for layer in reversed(range(DEPTH)):
        j = layer // 2
        L = f"L{layer}_"
        rec = saved[layer]
        Wm, Wf = rec["Wm"], rec["Wf"]
        g = {}
        if below is None:
            dz2, dz2b, dg2, db2 = _ln_bwd(rec["z2"], small["ln2_g"][layer][None], dy, None, name=L + "ln2_bwd")
        else:
            dz2, dz2b, dg2, db2 = _mm_ln_bwd(below[0], below[1], rec["z2"], small["ln2_g"][layer][None], below[2],
                                             name=L + "ln2_bwd", dep=below[3])
        gS["ln2_g"][layer], gS["ln2_b"][layer] = dg2[0], db2[0]
        g["down"] = _mm(rec["a"], dz2b, ta=True, name=L + "ffn_down_dw", out_dtype=BF16)
        du, dcw, dcb = _ffn_mid_bwd(rec["ug"], rec["uv"], rec["fcw"], rec["fcb"], dz2b, Wf["down"], name=L + "ffn_mid_bwd")
        g["fconv"] = dcw.astype(BF16)
        gS["ffn_conv_b"][layer] = dcb[0]
        g["up_t"] = _mm_tn_parts(du, rec["x1b"], name=L + "ffn_up_dw")
        dep = put_g(layer, "ffn", g)
        dz1, dz1b, dg1, db1 = _mm_ln_bwd(du, Wf["up_t"], rec["z1"], small["ln1_g"][layer][None], dz2,
                                         name=L + "ln1_bwd", dep=dep)
        gS["ln1_g"][layer], gS["ln1_b"][layer] = dg1[0], db1[0]
        g = {}
        if layer % 2 == 0:
            g["out"] = _mm_tn_parts(rec["y"], dz1b, name=L + "ev_out_dw")
            dyy = _mm(dz1b, Wm["out"], tb=True, name=L + "ev_out_dx")
            dqa, dka, dva, dga = _ret_bwd(rec["h"], c2a, s2a, lgt, rec["ro"], dyy, name=L + "ret_bwd")
            dqb, dkb, dvb = _dil_bwd(rec["h"], cb, shi, slo, strip, rec["dil_o"], rec["lse"], dyy, name=L + "dil_bwd")
            dh = [dqa, dka, dva, dga, dqb, dkb, dvb]
            g["in_t"] = _mm_tn_parts(dh, rec["xb"], name=L + "ev_in_dw")
            dep = put_g(layer, "mixer", g)
        else:
            g["out"] = _mm(rec["y"][0], dz1b, ta=True, name=L + "od_out_dw", out_dtype=BF16)
            dyy = _mm(dz1b, Wm["out"], tb=True, name=L + "od_out_dx")
            do, dgate, dnw = _gdn_post_bwd(rec["o"], rec["h"], rec["nw"], dyy, name=L + "gdn_post_bwd")
            gS["od_norm_w"][j] = dnw[0]
            dq, dk, dv, dsc, dal, ddt = _gdn_core_bwd(
                rec["q"], rec["k"], rec["v"], rec["h"], rec["alog"], rec["dtb"], rec["states"], rec["invs"], do,
                name=L + "gdn_bwd")
            gS["od_a_log"][j] = dal[:, 0, 0]
            gS["od_dt_bias"][j] = ddt[:, 0, 0]
            dhq, dhk, dhv, dwq, dwk, dwv = _gdn_prep_bwd(rec["h"], rec["cw"], dq, dk, dv, name=L + "gdn_prep_bwd")
            g["conv"] = jnp.concatenate([dwq, dwk, dwv], 1).astype(BF16)
            dh = [dhq, dhk, dhv, dgate, dsc.astype(BF16)]
            g["in_t"] = (_mm_tn_parts(dh[:4], rec["xb"], name=L + "od_in_dw"),
                         _mm(dh[4], rec["xb"], ta=True, name=L + "od_in_dw_logits", out_dtype=BF16))
            dep = put_g(layer, "mixer", g)
        below = (dh, Wm["in_t"], dz1, dep)
    grad_x = _axpy(_mm(jnp.concatenate(below[0], 1), below[1], name="L0_in_dx", dep=below[3]), below[2], name="grad_x")
    gS = {n: jnp.stack(v) for n, v in gS.items()}
    return loss, grad_x, gS


HBM = pl.BlockSpec(memory_space=pltpu.HBM)


def _me():
    return lax.axis_index("x"), lax.axis_index("y"), lax.axis_index("c")


def _my_index():
    x, y, c = _me()
    return 4 * x + 2 * y + c


SEM = pl.BlockSpec(memory_space=pltpu.SEMAPHORE)
ANY = pl.BlockSpec(memory_space=pl.ANY)
PLANS = {"scatter": (1, 2, 3, 4, 5, 6, 7), "spread": (1, 2, 4, 6), "relay": (2, 4, 6), "all": (1, 2, 3, 4, 5, 6, 7)}
ONE_SOURCE = ("spread", "all")
SIBLING = 1


def _peer(kk):
    x, y, c = _me()
    return x ^ (kk >> 2), y ^ ((kk >> 1) & 1), c ^ (kk & 1)


def _peer_index(kk):
    px, py, pc = _peer(kk)
    return 4 * px + 2 * py + pc


def _job_copies(mode, srcs, lands, send_sems, recv_sems, incoming):
    myid = _my_index()
    plan = PLANS[mode]
    out = []
    for a in range(len(lands)):
        for idx, kk in enumerate(plan):
            if mode == "relay":
                to, src = _peer(SIBLING), lands[a].at[_peer_index(kk)]
                slot_there, slot_here = _peer_index(kk), _peer_index(kk ^ SIBLING)
            else:
                to, src = _peer(kk), (srcs[a] if mode in ONE_SOURCE else srcs[a].at[_peer_index(kk)])
                slot_there, slot_here = myid, _peer_index(kk)
            sem = a * len(plan) + idx
            out.append(pltpu.make_async_remote_copy(
                src_ref=src, dst_ref=lands[a].at[slot_here if incoming else slot_there],
                send_sem=send_sems.at[sem], recv_sem=recv_sems.at[sem], device_id=to, device_id_type=MESH))
    return out


def _split_jobs(jobs, arrays):
    out, o = [], 0
    for (_, srcs, lands) in jobs:
        out.append((arrays[o:o + len(srcs)], arrays[o + len(srcs):o + len(srcs) + len(lands)]))
        o += len(srcs) + len(lands)
    return out


def _exchange_start(jobs, after, *, name):
    jobs = [(mode, list(srcs), [lax.empty((N_DEV, *s.shape) if mode in ONE_SOURCE else s.shape, s.dtype) for s in srcs]
             if lands is None else list(lands)) for (mode, srcs, lands) in jobs]
    flat = [a for (_, srcs, lands) in jobs for a in (*srcs, *lands)]
    n, nj = len(flat), len(jobs)
    nsem = [len(PLANS[mode]) * len(lands) for (mode, _, lands) in jobs]

    def body(*refs):
        o = n + (0 if after is None else 1)
        sems, token = refs[o:o + 2 * nj], refs[o + 2 * nj + n]
        for ji, ((mode, _, _), (src, land)) in enumerate(zip(jobs, _split_jobs(jobs, refs[:n]))):
            for cp in _job_copies(mode, src, land, sems[2 * ji], sems[2 * ji + 1], False):
                cp.start()
        token[...] = jnp.zeros_like(token)

    outs = pl.pallas_call(
        body, name=name,
        out_shape=(*[pltpu.SemaphoreType.DMA((ns,)) for ns in nsem for _ in range(2)],
                   *[pltpu.HBM(a.shape, a.dtype) for a in flat], jax.ShapeDtypeStruct((8, LANES), F32)),
        in_specs=[HBM] * n + ([] if after is None else [ANY]),
        out_specs=(*[SEM] * (2 * nj), *[HBM] * n, pl.BlockSpec(memory_space=pltpu.VMEM)),
        input_output_aliases={i: 2 * nj + i for i in range(n)},
        compiler_params=pltpu.CompilerParams(has_side_effects=pltpu.SideEffectType.DATAFLOW_SIDE_EFFECTING),
    )(*[pltpu.with_memory_space_constraint(a, pltpu.HBM) for a in flat], *([] if after is None else [after]))
    thru = _split_jobs(jobs, list(outs[2 * nj:2 * nj + n]))
    started = [(mode, outs[2 * ji], outs[2 * ji + 1], src, land) for ji, ((mode, _, _), (src, land)) in enumerate(zip(jobs, thru))]
    return started, outs[2 * nj + n]


def _exchange_wait(started, after, *, name):
    jobs = [(mode, srcs, lands) for (mode, _, _, srcs, lands) in started]
    flat = [a for (_, srcs, lands) in jobs for a in (*srcs, *lands)]
    n, nj = len(flat), len(jobs)

    def body(*refs):
        sems = refs[n:n + 2 * nj]
        for ji, ((mode, _, _), (src, land)) in enumerate(zip(jobs, _split_jobs(jobs, refs[:n]))):
            for cp in _job_copies(mode, src, land, sems[2 * ji], sems[2 * ji + 1], True):
                cp.wait_send()
                cp.wait_recv()

    outs = pl.pallas_call(
        body, name=name, out_shape=tuple(pltpu.HBM(a.shape, a.dtype) for a in flat),
        in_specs=[HBM] * n + [SEM] * (2 * nj) + [ANY], out_specs=tuple([HBM] * n),
        input_output_aliases={i: i for i in range(n)},
        compiler_params=pltpu.CompilerParams(has_side_effects=pltpu.SideEffectType.DATAFLOW_SIDE_EFFECTING),
    )(*flat, *[s for (_, ss, rs, _, _) in started for s in (ss, rs)], after)
    return _split_jobs(jobs, list(outs))


def _sum8(land, stack, j, depth, *, name):
    _, rr, cc = land.shape
    tr = _row_tile(rr)

    def body(l_ref, *rest):
        o_ref = rest[-1]
        acc = l_ref[0].astype(F32)
        for d in range(1, N_DEV):
            acc = acc + l_ref[d].astype(F32)
        o_ref[0] = acc

    prev = [] if stack is None else [stack]
    return pl.pallas_call(
        body, grid=(rr // tr,),
        in_specs=[pl.BlockSpec((N_DEV, tr, cc), lambda i: (0, i, 0))] + [pl.BlockSpec(memory_space=pl.ANY)] * len(prev),
        out_specs=pl.BlockSpec((1, tr, cc), lambda i: (j, i, 0)), out_shape=jax.ShapeDtypeStruct((depth, rr, cc), F32),
        input_output_aliases={1: 0} if prev else {}, name=name, compiler_params=_cp())(land, *prev)


def _row_tile(rr):
    for cand in (512, 384, 256, 192, 176, 128, 64, 32, 16, 8):
        if rr % cand == 0:
            return cand
    return rr


def _adam_math(w, g, m, v):
    m = ADAM_B1 * m + (1.0 - ADAM_B1) * g
    v = ADAM_B2 * v + (1.0 - ADAM_B2) * (g * g)
    m_hat = m / (1.0 - ADAM_B1 ** ADAM_STEP)
    v_hat = v / (1.0 - ADAM_B2 ** ADAM_STEP)
    delta = -ADAM_LR * (m_hat / (jnp.sqrt(v_hat) + ADAM_EPS) + ADAM_WD * w)
    return delta, m, v


def _adamw_sharded(w, m, v, g, *, name):
    ll, rr, cc = w.shape
    tr = _row_tile(rr)

    def body(w_ref, m_ref, v_ref, g_ref, d_ref, nm_ref, nv_ref):
        d, nm, nv = _adam_math(w_ref[...], g_ref[...], m_ref[...], v_ref[...])
        d_ref[...] = d
        nm_ref[...] = nm
        nv_ref[...] = nv

    blk = pl.BlockSpec((1, tr, cc), lambda l, i: (l, i, 0))
    sh = jax.ShapeDtypeStruct((ll, rr, cc), F32)
    return pl.pallas_call(
        body, grid=(ll, rr // tr), in_specs=[blk] * 4, out_specs=[blk] * 3, out_shape=[sh] * 3,
        name=name, compiler_params=_cp())(w, m, v, g)


def _adamw_small(w, m, v, gall, *, name):
    rr = w.shape[0]

    def body(w_ref, m_ref, v_ref, g_ref, go_ref, d_ref, nm_ref, nv_ref):
        g = g_ref[0]
        for kk in range(1, N_DEV):
            g = g + g_ref[kk]
        d, nm, nv = _adam_math(w_ref[...], g, m_ref[...], v_ref[...])
        go_ref[...] = g
        d_ref[...] = d
        nm_ref[...] = nm
        nv_ref[...] = nv

    sh = jax.ShapeDtypeStruct((rr, LANES), F32)
    return pl.pallas_call(body, out_shape=[sh] * 4, name=name, compiler_params=_cp())(w, m, v, gall)


SHARDED = ("ev_w_in", "ev_w_out", "od_w_in", "od_conv_w", "od_w_out", "ffn_w_up", "ffn_conv_w", "ffn_w_down")
SMALL = ("od_a_log", "od_dt_bias", "od_norm_w", "ffn_conv_b", "ln1_g", "ln1_b", "ln2_g", "ln2_b")
ALL_W = ("ev_w_in", "ev_w_out", "od_w_in", "od_conv_w", "od_a_log", "od_dt_bias", "od_norm_w", "od_w_out",
         "ffn_w_up", "ffn_conv_w", "ffn_conv_b", "ffn_w_down", "ln1_g", "ln1_b", "ln2_g", "ln2_b")


def _layer_items(layer):
    j = layer // 2
    if layer % 2 == 0:
        mixer = [("in_t", "ev_w_in", j, "colT"), ("out", "ev_w_out", j, "row")]
    else:
        mixer = [("in_t", "od_w_in", j, "colT"), ("conv", "od_conv_w", j, "colsmall"), ("out", "od_w_out", j, "row")]
    return mixer + [("up_t", "ffn_w_up", layer, "colT"), ("fconv", "ffn_conv_w", layer, "colsmall"),
                    ("down", "ffn_w_down", layer, "row")]


OD_SHARD = OD_IN // N_DEV
OD_SHARD_PAD = OD_IN_PAD // N_DEV


def _od_pack(g, *, name):
    d = g.shape[-1]

    def body(g_ref, o_ref):
        for n in range(N_DEV):
            o_ref[OD_SHARD * n:OD_SHARD * (n + 1), :] = g_ref[n, 0:OD_SHARD, :]
        o_ref[OD_IN:OD_IN_PAD, :] = jnp.zeros((OD_IN_PAD - OD_IN, d), g.dtype)

    return pl.pallas_call(body, out_shape=jax.ShapeDtypeStruct((OD_IN_PAD, d), g.dtype), name=name,
                          compiler_params=_cp())(g)


def _od_unpack(main, tail, *, name):
    d = main.shape[-1]
    split = main.shape[0]

    def body(m_ref, t_ref, o_ref):
        for n in range(N_DEV):
            lo, hi = OD_SHARD * n, OD_SHARD * (n + 1)
            from_main = min(hi, split) - lo
            o_ref[n, 0:from_main, :] = m_ref[lo:lo + from_main, :]
            if hi > split:
                o_ref[n, from_main:OD_SHARD, :] = t_ref[0:hi - split, :]
            o_ref[n, OD_SHARD:OD_SHARD_PAD, :] = jnp.zeros((OD_SHARD_PAD - OD_SHARD, d), main.dtype)

    return pl.pallas_call(body, out_shape=jax.ShapeDtypeStruct((N_DEV, OD_SHARD_PAD, d), main.dtype), name=name,
                          compiler_params=_cp())(main, tail)


def _to_send(kind, name, w, j):
    if kind == "colT":
        s = w[j].T.astype(BF16)
        return jnp.pad(s, ((0, OD_SHARD_PAD - OD_SHARD), (0, 0))) if name == "od_w_in" else s
    return w[j].astype(BF16) if kind == "row" else w[j]


def _from_gather(kind, name, g, tag):
    if kind == "colsmall":
        return jnp.transpose(g, (1, 0, 2)).reshape(g.shape[1], -1)
    if name == "od_w_in":
        return _od_pack(g, name=tag + "_pack")
    return g.reshape(-1, g.shape[-1])


def _by_owner(kind, name, gfull, tag):
    if kind == "colsmall":
        kk, c8 = gfull.shape
        return jnp.transpose(gfull.reshape(kk, N_DEV, c8 // N_DEV), (1, 0, 2))
    if name == "od_w_in":
        return _od_unpack(*gfull, name=tag + "_unpack")
    return gfull.reshape(N_DEV, gfull.shape[0] // N_DEV, gfull.shape[1])


def _pack_small(d):
    flat = jnp.concatenate([d[n].reshape(-1) for n in SMALL])
    pad = (-flat.shape[0]) % (8 * LANES)
    return jnp.pad(flat, (0, pad)).reshape(-1, LANES)


def _unpack_small(packed, like):
    flat = packed.reshape(-1)
    out, off = {}, 0
    for n in SMALL:
        sz = int(np.prod(like[n].shape))
        out[n] = flat[off:off + sz].reshape(like[n].shape)
        off += sz
    return out


def kernel(x, positions, ev_w_in, ev_w_out, od_w_in, od_conv_w, od_a_log, od_dt_bias, od_norm_w, od_w_out, ffn_w_up, ffn_conv_w, ffn_conv_b, ffn_w_down, ln1_g, ln1_b, ln2_g, ln2_b, loss_target, m_ev_w_in, m_ev_w_out, m_od_w_in, m_od_conv_w, m_od_a_log, m_od_dt_bias, m_od_norm_w, m_od_w_out, m_ffn_w_up, m_ffn_conv_w, m_ffn_conv_b, m_ffn_w_down, m_ln1_g, m_ln1_b, m_ln2_g, m_ln2_b, v_ev_w_in, v_ev_w_out, v_od_w_in, v_od_conv_w, v_od_a_log, v_od_dt_bias, v_od_norm_w, v_od_w_out, v_ffn_w_up, v_ffn_conv_w, v_ffn_conv_b, v_ffn_w_down, v_ln1_g, v_ln1_b, v_ln2_g, v_ln2_b):
    w = dict(ev_w_in=ev_w_in, ev_w_out=ev_w_out, od_w_in=od_w_in, od_conv_w=od_conv_w, od_a_log=od_a_log,
             od_dt_bias=od_dt_bias, od_norm_w=od_norm_w, od_w_out=od_w_out, ffn_w_up=ffn_w_up, ffn_conv_w=ffn_conv_w,
             ffn_conv_b=ffn_conv_b, ffn_w_down=ffn_w_down, ln1_g=ln1_g, ln1_b=ln1_b, ln2_g=ln2_g, ln2_b=ln2_b)
    mom = dict(ev_w_in=m_ev_w_in, ev_w_out=m_ev_w_out, od_w_in=m_od_w_in, od_conv_w=m_od_conv_w, od_a_log=m_od_a_log,
               od_dt_bias=m_od_dt_bias, od_norm_w=m_od_norm_w, od_w_out=m_od_w_out, ffn_w_up=m_ffn_w_up,
               ffn_conv_w=m_ffn_conv_w, ffn_conv_b=m_ffn_conv_b, ffn_w_down=m_ffn_w_down, ln1_g=m_ln1_g,
               ln1_b=m_ln1_b, ln2_g=m_ln2_g, ln2_b=m_ln2_b)
    var = dict(ev_w_in=v_ev_w_in, ev_w_out=v_ev_w_out, od_w_in=v_od_w_in, od_conv_w=v_od_conv_w, od_a_log=v_od_a_log,
               od_dt_bias=v_od_dt_bias, od_norm_w=v_od_norm_w, od_w_out=v_od_w_out, ffn_w_up=v_ffn_w_up,
               ffn_conv_w=v_ffn_conv_w, ffn_conv_b=v_ffn_conv_b, ffn_w_down=v_ffn_w_down, ln1_g=v_ln1_g,
               ln1_b=v_ln1_b, ln2_g=v_ln2_g, ln2_b=v_ln2_b)

    myid = _my_index()
    small = {n: w[n] for n in SMALL}
    groups = [(layer, part) for layer in range(DEPTH) for part in ("mixer", "ffn")]

    def group_items(gi):
        layer, part = groups[gi]
        its = _layer_items(layer)
        return its[:-3] if part == "mixer" else its[-3:]

    level1, level2 = {}, {}

    def spread_job(gi):
        return ("spread", [_to_send(kind, n, w[n], j) for (_, n, j, kind) in group_items(gi)], None)

    def relay(gi, after, name):
        (srcs, lands), = _exchange_wait([level1.pop(gi)], after, name=name + "_wait")
        more = [spread_job(gi + 1)] if gi + 1 < len(groups) else []
        started, token = _exchange_start([("relay", [], lands)] + more, None, name=name + "_start")
        level2[gi] = (started[0], srcs)
        if more:
            level1[gi + 1] = started[1]
        return token

    def get_w(layer, part, after):
        gi = groups.index((layer, part))
        started, srcs = level2.pop(gi)
        (_, lands), = _exchange_wait([started], after, name=f"gather{gi}_wait")
        lands = [lax.dynamic_update_index_in_dim(l, s, myid, 0) for l, s in zip(lands, srcs)]
        return {key: _from_gather(kind, n, l, f"L{layer}_{key}")
                for (key, n, _, kind), l in zip(group_items(gi), lands)}, None

    def mid(layer, part, after):
        gi = groups.index((layer, part)) + 1
        return relay(gi, after, f"gather{gi}_relay") if gi < len(groups) else None

    landed = {}
    pending = []

    def scatter_finish(after):
        started, gi = pending.pop()
        (srcs, lands), = _exchange_wait([started], after, name=f"scatter{gi}_wait")
        for (key, _, _, _), l, s in zip(group_items(gi), lands, srcs):
            own = lax.dynamic_index_in_dim(s, myid, 0, keepdims=False)
            landed[(groups[gi][0], key)] = lax.dynamic_update_index_in_dim(l, own, myid, 0)

    def put_g(layer, part, g):
        gi = groups.index((layer, part))
        srcs = [_by_owner(kind, n, g[key], f"L{layer}_{key}") for (key, n, _, kind) in group_items(gi)]
        (started,), token = _exchange_start([("scatter", srcs, None)], None, name=f"scatter{gi}_start")
        if pending:
            scatter_finish(token)
        pending.append((started, gi))
        return token

    (level1[0],), token = _exchange_start([spread_job(0)], None, name="gather0_spread_start")
    tables = _tables(positions[0] + token[0, 0].astype(jnp.int32))
    relay(0, tables[-1], "gather0_relay")
    loss, grad_x, gS = _local_step(x[0], tables, loss_target[0], get_w, mid, put_g, small)
    loss = lax.psum(loss, ("x", "y", "c"))

    outs_g, outs_d, outs_m, outs_v = {}, {}, {}, {}
    where = {n: [None] * w[n].shape[0] for n in SHARDED}
    for layer in range(DEPTH):
        for (key, n, j, kind) in _layer_items(layer):
            where[n][j] = (layer, key, kind)

    def update(n):
        g = None
        for j, (layer, key, _) in enumerate(where[n]):
            g = _sum8(landed[(layer, key)], g, j, len(where[n]), name=f"L{layer}_{key}_sum")
        if n == "od_w_in":
            g = g[:, :OD_SHARD]
        if where[n][0][2] == "colT":
            tr = lambda a: jnp.swapaxes(a, 1, 2)
            d, nm, nv = _adamw_sharded(tr(w[n]), tr(mom[n]), tr(var[n]), g, name=f"adamw_{n}")
            outs_g[n], outs_d[n], outs_m[n], outs_v[n] = tr(g), tr(d), tr(nm), tr(nv)
        else:
            outs_g[n] = g
            outs_d[n], outs_m[n], outs_v[n] = _adamw_sharded(w[n], mom[n], var[n], g, name=f"adamw_{n}")

    (small_job,), _ = _exchange_start([("all", [_pack_small(gS)], None)], None, name="small_grads_start")
    last = {n for (_, n, _, _) in group_items(pending[0][1])}
    for n in SHARDED:
        if n not in last:
            update(n)
    scatter_finish(outs_d[[n for n in SHARDED if n not in last][-1]])
    for n in SHARDED:
        if n in last:
            update(n)
    ((mine,), (gall,)), = _exchange_wait([small_job], outs_d[[n for n in SHARDED if n in last][-1]], name="small_grads_wait")
    gall = lax.dynamic_update_index_in_dim(gall, mine, myid, 0)
    g, d, nm, nv = _adamw_small(_pack_small({n: w[n] for n in SMALL}), _pack_small({n: mom[n] for n in SMALL}),
                                _pack_small({n: var[n] for n in SMALL}), gall, name="adamw_small")
    for dst, packed in ((outs_g, g), (outs_d, d), (outs_m, nm), (outs_v, nv)):
        dst.update(_unpack_small(packed, {n: w[n] for n in SMALL}))

    return (loss, grad_x[None], *[outs_g[n] for n in ALL_W], *[outs_d[n] for n in ALL_W],
            *[outs_m[n] for n in ALL_W], *[outs_v[n] for n in ALL_W])
```

```python
import functools
import math

import numpy as np
import jax
import jax.numpy as jnp
from jax import lax
from jax.experimental import pallas as pl
from jax.experimental.pallas import tpu as pltpu

F32 = jnp.float32
BF16 = jnp.bfloat16
MESH = pl.DeviceIdType.MESH

D_MODEL = 1024
SEQ = 2048
DEPTH = 4
N_DEV = 8
RET_HEADS, RET_DK, RET_DV = 4, 128, 256
RET_THETA = 10000.0
DIL_HEADS, DIL_HD = 8, 64
DIL_PAIRS = ((128, 1), (512, 4), (2048, 16))
ROPE_THETA = 500000.0
ROPE_DIMS = DIL_HD // 4
GDN_HEADS, GDN_DK, GDN_DV, GDN_CHUNK, GDN_CONV = 8, 128, 128, 64, 4
D_FF = 2816
FFN_CONV = 3
ALPHA = (2.0 * DEPTH) ** 0.25
EPS = 1e-5
RET_QK_W = RET_HEADS * RET_DK
RET_V_W = RET_HEADS * RET_DV
DIL_W = DIL_HEADS * DIL_HD
EV_IN = 2 * RET_QK_W + 2 * RET_V_W + 3 * DIL_W
EV_MIX = RET_V_W + DIL_W
GDN_W = GDN_HEADS * GDN_DK
OD_IN = 4 * GDN_W + 2 * GDN_HEADS
OD_IN_PAD = 4 * GDN_W + 128
ADAM_LR, ADAM_B1, ADAM_B2, ADAM_EPS, ADAM_WD, ADAM_STEP = 0.001, 0.9, 0.999, 1e-08, 0.01, 10

LANES = 128
VMEM_LIMIT = 56 * 1024 * 1024
ATT_BLK = 256
NEG = -1e30


def _cp(**kw):
    return pltpu.CompilerParams(vmem_limit_bytes=VMEM_LIMIT, **kw)


def _tile(n, cap):
    if n <= cap:
        return n
    best = None
    for t in range(LANES, cap + 1, LANES):
        if n % t == 0:
            best = t
    assert best is not None, (n, cap)
    return best


def _mm(a, b, *, ta=False, tb=False, name, out_dtype=F32, dep=None, tm=None, tn=None):
    m = a.shape[1] if ta else a.shape[0]
    k = a.shape[0] if ta else a.shape[1]
    n = b.shape[0] if tb else b.shape[1]
    assert (b.shape[1] if tb else b.shape[0]) == k
    assert a.dtype == BF16 and b.dtype == BF16
    if tn is None:
        tn = n if n <= 1024 else _tile(n, 512)
    if tm is None:
        tm = m if (tn < n and k <= 1024 and m <= 2048) else _tile(m, 1408 if m > 2048 else 512)
    dims = (((0 if ta else 1,), (1 if tb else 0,)), ((), ()))

    def body(a_ref, b_ref, *rest):
        o_ref = rest[-1]
        o_ref[...] = lax.dot_general(a_ref[...], b_ref[...], dims,
                                     preferred_element_type=F32).astype(o_ref.dtype)

    a_spec = pl.BlockSpec((k, tm), lambda i, j: (0, i)) if ta else pl.BlockSpec((tm, k), lambda i, j: (i, 0))
    b_spec = pl.BlockSpec((tn, k), lambda i, j: (j, 0)) if tb else pl.BlockSpec((k, tn), lambda i, j: (0, j))
    extra = [] if dep is None else [dep]
    return pl.pallas_call(
        body, grid=(m // tm, n // tn), in_specs=[a_spec, b_spec] + [pl.BlockSpec(memory_space=pl.ANY)] * len(extra),
        out_specs=pl.BlockSpec((tm, tn), lambda i, j: (i, j)),
        out_shape=jax.ShapeDtypeStruct((m, n), out_dtype), name=name, compiler_params=_cp())(a, b, *extra)


LN_ROWS = 256


def _ln_bwd(z, g, dya, dyb, *, name):
    t, d = z.shape
    two = dyb is not None

    def body(*refs):
        if two:
            z_ref, g_ref, dya_ref, dyb_ref, dz_ref, dzb_ref, dg_ref, db_ref = refs
            dy = dya_ref[...] + ALPHA * dyb_ref[...]
        else:
            z_ref, g_ref, dya_ref, dz_ref, dzb_ref, dg_ref, db_ref = refs
            dy = dya_ref[...]
        zz = z_ref[...]
        mu = jnp.mean(zz, -1, keepdims=True)
        zc = zz - mu
        var = jnp.mean(zc * zc, -1, keepdims=True)
        r = lax.rsqrt(var + EPS)
        xh = zc * r
        dxh = dy * g_ref[...]
        dz = r * (dxh - jnp.mean(dxh, -1, keepdims=True) - xh * jnp.mean(dxh * xh, -1, keepdims=True))
        dz_ref[...] = dz
        dzb_ref[...] = dz.astype(BF16)

        @pl.when(pl.program_id(0) == 0)
        def _():
            dg_ref[...] = jnp.zeros_like(dg_ref)
            db_ref[...] = jnp.zeros_like(db_ref)

        dg_ref[...] += jnp.sum(dy * xh, 0, keepdims=True)
        db_ref[...] += jnp.sum(dy, 0, keepdims=True)

    row = pl.BlockSpec((LN_ROWS, d), lambda i: (i, 0))
    vec = pl.BlockSpec((1, d), lambda i: (0, 0))
    ins = [z, g, dya] + ([dyb] if two else [])
    return pl.pallas_call(
        body, grid=(t // LN_ROWS,), in_specs=[row, vec, row] + ([row] if two else []),
        out_specs=[row, row, vec, vec],
        out_shape=[jax.ShapeDtypeStruct((t, d), F32), jax.ShapeDtypeStruct((t, d), BF16),
                   jax.ShapeDtypeStruct((1, d), F32), jax.ShapeDtypeStruct((1, d), F32)],
        name=name, compiler_params=_cp())(*ins)


def _ln_rows(k):
    return 256 if k > 4096 else 512


def _mm_ln_fwd(parts, w, x, g, b, *, name, dep=None):
    t = parts[0].shape[0]
    offs, k = _part_offsets(parts)
    d = w.shape[1]
    tm = _ln_rows(k)
    npart = len(parts)

    def body(*refs):
        a_refs, w_refs = refs[:npart], refs[npart:2 * npart]
        x_ref, g_ref, b_ref = refs[2 * npart:2 * npart + 3]
        z_ref, y_ref, yb_ref = refs[-3:]
        z = ALPHA * x_ref[...]
        for a_ref, w_ref in zip(a_refs, w_refs):
            z = z + _nn(a_ref[...], w_ref[...])
        mu = jnp.mean(z, -1, keepdims=True)
        zc = z - mu
        var = jnp.mean(zc * zc, -1, keepdims=True)
        y = zc * lax.rsqrt(var + EPS) * g_ref[...] + b_ref[...]
        z_ref[...] = z
        y_ref[...] = y
        yb_ref[...] = y.astype(BF16)

    row = pl.BlockSpec((tm, d), lambda i: (i, 0))
    vec = pl.BlockSpec((1, d), lambda i: (0, 0))
    extra = [] if dep is None else [dep]
    a_specs = [pl.BlockSpec((tm, p.shape[1]), lambda i: (i, 0)) for p in parts]
    w_specs = [pl.BlockSpec((p.shape[1], d), functools.partial(lambda i, blk: (blk, 0), blk=o // p.shape[1]))
               for p, o in zip(parts, offs)]
    return pl.pallas_call(
        body, grid=(t // tm,),
        in_specs=a_specs + w_specs + [row, vec, vec] + [pl.BlockSpec(memory_space=pl.ANY)] * len(extra),
        out_specs=[row, row, row],
        out_shape=[jax.ShapeDtypeStruct((t, d), F32), jax.ShapeDtypeStruct((t, d), F32), jax.ShapeDtypeStruct((t, d), BF16)],
        name=name, compiler_params=_cp())(*parts, *([w] * npart), x, g, b, *extra)


def _part_offsets(parts):
    offs, o = [], 0
    for p in parts:
        assert o % p.shape[1] == 0
        offs.append(o)
        o += p.shape[1]
    return offs, o


def _mm_ln_bwd(parts, w, z, g, dyb, *, name, dep=None):
    t = parts[0].shape[0]
    offs, k = _part_offsets(parts)
    d = w.shape[1]
    tm = _ln_rows(k)
    npart = len(parts)

    def body(*refs):
        a_refs, w_refs = refs[:npart], refs[npart:2 * npart]
        z_ref, g_ref, dyb_ref = refs[2 * npart:2 * npart + 3]
        dz_ref, dzb_ref, dg_ref, db_ref = refs[-4:]
        dy = ALPHA * dyb_ref[...]
        for a_ref, w_ref in zip(a_refs, w_refs):
            dy = dy + _nn(a_ref[...], w_ref[...])
        zz = z_ref[...]
        mu = jnp.mean(zz, -1, keepdims=True)
        zc = zz - mu
        var = jnp.mean(zc * zc, -1, keepdims=True)
        r = lax.rsqrt(var + EPS)
        xh = zc * r
        dxh = dy * g_ref[...]
        dz = r * (dxh - jnp.mean(dxh, -1, keepdims=True) - xh * jnp.mean(dxh * xh, -1, keepdims=True))
        dz_ref[...] = dz
        dzb_ref[...] = dz.astype(BF16)

        @pl.when(pl.program_id(0) == 0)
        def _():
            dg_ref[...] = jnp.zeros_like(dg_ref)
            db_ref[...] = jnp.zeros_like(db_ref)

        dg_ref[...] += jnp.sum(dy * xh, 0, keepdims=True)
        db_ref[...] += jnp.sum(dy, 0, keepdims=True)

    row = pl.BlockSpec((tm, d), lambda i: (i, 0))
    vec = pl.BlockSpec((1, d), lambda i: (0, 0))
    extra = [] if dep is None else [dep]
    a_specs = [pl.BlockSpec((tm, p.shape[1]), lambda i: (i, 0)) for p in parts]
    w_specs = [pl.BlockSpec((p.shape[1], d), functools.partial(lambda i, blk: (blk, 0), blk=o // p.shape[1]))
               for p, o in zip(parts, offs)]
    return pl.pallas_call(
        body, grid=(t // tm,),
        in_specs=a_specs + w_specs + [row, vec, row] + [pl.BlockSpec(memory_space=pl.ANY)] * len(extra),
        out_specs=[row, row, vec, vec],
        out_shape=[jax.ShapeDtypeStruct((t, d), F32), jax.ShapeDtypeStruct((t, d), BF16),
                   jax.ShapeDtypeStruct((1, d), F32), jax.ShapeDtypeStruct((1, d), F32)],
        name=name, compiler_params=_cp())(*parts, *([w] * npart), z, g, dyb, *extra)


def _mm_tn_parts(parts, b, *, name):
    t, n = b.shape
    offs, m = _part_offsets(parts)
    tm = min(_tile(p.shape[1], 1408 if p.shape[1] > 2048 else 512) for p in parts)
    assert all(p.shape[1] % tm == 0 for p in parts)
    first = [o // tm for o in offs]
    count = [p.shape[1] // tm for p in parts]
    npart = len(parts)

    def body(*refs):
        a_refs, b_ref, o_ref = refs[:npart], refs[npart], refs[npart + 1]
        i = pl.program_id(0)
        for a_ref, f, c in zip(a_refs, first, count):
            @pl.when((i >= f) & (i < f + c))
            def _(a_ref=a_ref):
                o_ref[...] = _tn(a_ref[...], b_ref[...]).astype(BF16)

    a_specs = [pl.BlockSpec((t, tm), functools.partial(lambda i, f, c: (0, jnp.clip(i - f, 0, c - 1)), f=f, c=c))
               for f, c in zip(first, count)]
    return pl.pallas_call(
        body, grid=(m // tm,), in_specs=a_specs + [pl.BlockSpec((t, n), lambda i: (0, 0))],
        out_specs=pl.BlockSpec((tm, n), lambda i: (i, 0)), out_shape=jax.ShapeDtypeStruct((m, n), BF16),
        name=name, compiler_params=_cp())(*parts, b)


def _axpy(a, b, *, name):
    t, d = a.shape

    def body(a_ref, b_ref, o_ref):
        o_ref[...] = a_ref[...] + ALPHA * b_ref[...]

    row = pl.BlockSpec((LN_ROWS, d), lambda i: (i, 0))
    return pl.pallas_call(body, grid=(t // LN_ROWS,), in_specs=[row, row], out_specs=row,
                          out_shape=jax.ShapeDtypeStruct((t, d), F32), name=name, compiler_params=_cp())(a, b)


def _loss_head(y, target, *, name):
    t, d = y.shape

    def body(y_ref, t_ref, dy_ref, l_ref):
        e = y_ref[...] - t_ref[...]
        dy_ref[...] = e * (1.0 / d)

        @pl.when(pl.program_id(0) == 0)
        def _():
            l_ref[...] = jnp.zeros_like(l_ref)

        l_ref[...] += jnp.zeros_like(l_ref) + 0.5 * jnp.sum(jnp.mean(e * e, -1, keepdims=True), 0, keepdims=True)

    row = pl.BlockSpec((LN_ROWS, d), lambda i: (i, 0))
    return pl.pallas_call(
        body, grid=(t // LN_ROWS,), in_specs=[row, row],
        out_specs=[row, pl.BlockSpec((1, LANES), lambda i: (0, 0))],
        out_shape=[jax.ShapeDtypeStruct((t, d), F32), jax.ShapeDtypeStruct((1, LANES), F32)],
        name=name, compiler_params=_cp())(y, target)


def _sig(x):
    return 1.0 / (1.0 + jnp.exp(-x))


def _silu(x):
    return x * _sig(x)


def _dsilu(x):
    s = _sig(x)
    return s * (1.0 + x * (1.0 - s))


def _shift_down(u, k, row):
    if k == 0:
        return u
    return jnp.where(row >= k, pltpu.roll(u, k, 0), 0.0)


def _shift_up(u, k, row):
    if k == 0:
        return u
    t = u.shape[0]
    return jnp.where(row < t - k, pltpu.roll(u, t - k, 0), 0.0)


def _dwconv(u, w_ref, row):
    kk = w_ref.shape[0]
    acc = None
    for j in range(kk):
        term = w_ref[j:j + 1, :] * _shift_down(u, kk - 1 - j, row)
        acc = term if acc is None else acc + term
    return acc


def _dwconv_bwd(u, w_ref, dc, row, dw_ref):
    kk = w_ref.shape[0]
    du = None
    for j in range(kk):
        term = w_ref[j:j + 1, :] * _shift_up(dc, kk - 1 - j, row)
        du = term if du is None else du + term
        dw_ref[j:j + 1, :] = jnp.sum(dc * _shift_down(u, kk - 1 - j, row), 0, keepdims=True)
    return du


CONV_ROWS = 1024


def _rows(b):
    return pl.ds(pl.multiple_of(b * CONV_ROWS, CONV_ROWS), CONV_ROWS)


def _shifted_down(ref, b, k, row):
    cur = ref[_rows(b), :]
    if k == 0:
        return cur
    prev = jnp.where(b > 0, ref[_rows(jnp.maximum(b - 1, 0)), :], 0.0)
    return jnp.where(row >= k, pltpu.roll(cur, k, 0), pltpu.roll(prev, k, 0))


def _shifted_up(ref, b, k, row, nblk):
    cur = ref[_rows(b), :]
    if k == 0:
        return cur
    nxt = jnp.where(b < nblk - 1, ref[_rows(jnp.minimum(b + 1, nblk - 1)), :], 0.0)
    return jnp.where(row < CONV_ROWS - k, pltpu.roll(cur, CONV_ROWS - k, 0), pltpu.roll(nxt, CONV_ROWS - k, 0))


def _dwconv_blk(u_ref, w_ref, b, row):
    kk = w_ref.shape[0]
    views = [_shifted_down(u_ref, b, kk - 1 - j, row) for j in range(kk)]
    acc = None
    for j in range(kk):
        term = w_ref[j:j + 1, :] * views[j]
        acc = term if acc is None else acc + term
    return acc, views


def _dwconv_du_blk(dc_ref, w_ref, b, row, nblk):
    kk = w_ref.shape[0]
    du = None
    for j in range(kk):
        term = w_ref[j:j + 1, :] * _shifted_up(dc_ref, b, kk - 1 - j, row, nblk)
        du = term if du is None else du + term
    return du


FFN_TC = 256


def _ffn_up_mid(x, up_t, cw, cb, *, name, dep=None):
    t, d = x.shape
    nb = D_FF // FFN_TC

    def body(x_ref, ugt_ref, uvt_ref, wg_ref, wv_ref, bg_ref, bv_ref, *rest):
        ug_ref, uv_ref, a_ref = rest[-3:]
        xx = x_ref[...]
        row = lax.broadcasted_iota(jnp.int32, (t, FFN_TC), 0)
        ug = _nt(xx, ugt_ref[...])
        ug_ref[...] = ug
        uv = _nt(xx, uvt_ref[...])
        uv_ref[...] = uv
        cg = _dwconv(ug, wg_ref, row) + bg_ref[...]
        cv = _dwconv(uv, wv_ref, row) + bv_ref[...]
        a_ref[...] = (_silu(cg) * cv).astype(BF16)

    col = pl.BlockSpec((t, FFN_TC), lambda j: (0, j))
    wt = lambda off: pl.BlockSpec((FFN_TC, d), lambda j: (j + off, 0))
    wsp = lambda off: pl.BlockSpec((FFN_CONV, FFN_TC), lambda j: (0, j + off))
    bsp = lambda off: pl.BlockSpec((1, FFN_TC), lambda j: (0, j + off))
    extra = [] if dep is None else [dep]
    return pl.pallas_call(
        body, grid=(nb,),
        in_specs=[pl.BlockSpec((t, d), lambda j: (0, 0)), wt(0), wt(nb), wsp(0), wsp(nb), bsp(0), bsp(nb)]
        + [pl.BlockSpec(memory_space=pl.ANY)] * len(extra),
        out_specs=[col, col, col],
        out_shape=[jax.ShapeDtypeStruct((t, D_FF), F32), jax.ShapeDtypeStruct((t, D_FF), F32),
                   jax.ShapeDtypeStruct((t, D_FF), BF16)],
        name=name, compiler_params=_cp())(x, up_t, up_t, cw, cw, cb, cb, *extra)


def _ffn_mid_bwd(ug, uv, cw, cb, dz, down, *, name):
    t, d = dz.shape
    nb = D_FF // FFN_TC

    nblk = t // CONV_ROWS

    def body(ug_ref, uv_ref, wg_ref, wv_ref, bg_ref, bv_ref, dz_ref, dn_ref,
             dug_ref, duv_ref, dwg_ref, dwv_ref, dbg_ref, dbv_ref, da_ref, dcg_s, dcv_s):
        da_ref[...] = _nt(dz_ref[...], dn_ref[...])
        row = lax.broadcasted_iota(jnp.int32, (CONV_ROWS, FFN_TC), 0)
        zero = jnp.zeros((1, FFN_TC), F32)

        def first(b, acc):
            cg, ugs = _dwconv_blk(ug_ref, wg_ref, b, row)
            cv, uvs = _dwconv_blk(uv_ref, wv_ref, b, row)
            cg = cg + bg_ref[...]
            cv = cv + bv_ref[...]
            da_ = da_ref[_rows(b), :]
            dcv = da_ * _silu(cg)
            dcg = da_ * cv * _dsilu(cg)
            dcg_s[_rows(b), :] = dcg
            dcv_s[_rows(b), :] = dcv
            red = [jnp.sum(dcg * s, 0, keepdims=True) for s in ugs] + [jnp.sum(dcg, 0, keepdims=True)]
            red += [jnp.sum(dcv * s, 0, keepdims=True) for s in uvs] + [jnp.sum(dcv, 0, keepdims=True)]
            return tuple(a + r for a, r in zip(acc, red))

        acc = lax.fori_loop(0, nblk, first, (zero,) * (2 * FFN_CONV + 2))
        for j in range(FFN_CONV):
            dwg_ref[j:j + 1, :] = acc[j]
            dwv_ref[j:j + 1, :] = acc[FFN_CONV + 1 + j]
        dbg_ref[...] = acc[FFN_CONV]
        dbv_ref[...] = acc[2 * FFN_CONV + 1]

        def second(b, carry):
            dug_ref[_rows(b), :] = _dwconv_du_blk(dcg_s, wg_ref, b, row, nblk).astype(BF16)
            duv_ref[_rows(b), :] = _dwconv_du_blk(dcv_s, wv_ref, b, row, nblk).astype(BF16)
            return carry

        lax.fori_loop(0, nblk, second, 0)

    col = pl.BlockSpec((t, FFN_TC), lambda j: (0, j))
    wsp = lambda off: pl.BlockSpec((FFN_CONV, FFN_TC), lambda j: (0, j + off))
    bsp = lambda off: pl.BlockSpec((1, FFN_TC), lambda j: (0, j + off))
    outs = pl.pallas_call(
        body, grid=(nb,),
        in_specs=[col, col, wsp(0), wsp(nb), bsp(0), bsp(nb), pl.BlockSpec((t, d), lambda j: (0, 0)),
                  pl.BlockSpec((FFN_TC, d), lambda j: (j, 0))],
        out_specs=[col, col, wsp(0), wsp(0), bsp(0), bsp(0)],
        out_shape=[jax.ShapeDtypeStruct((t, D_FF), BF16), jax.ShapeDtypeStruct((t, D_FF), BF16),
                   jax.ShapeDtypeStruct((FFN_CONV, D_FF), F32), jax.ShapeDtypeStruct((FFN_CONV, D_FF), F32),
                   jax.ShapeDtypeStruct((1, D_FF), F32), jax.ShapeDtypeStruct((1, D_FF), F32)],
        scratch_shapes=[pltpu.VMEM((t, FFN_TC), F32), pltpu.VMEM((t, FFN_TC), F32), pltpu.VMEM((t, FFN_TC), F32)],
        name=name, compiler_params=_cp())(ug, uv, cw, cw, cb, cb, dz, down)
    dug, duv, dwg, dwv, dbg, dbv = outs
    return [dug, duv], jnp.concatenate([dwg, dwv], 1), jnp.concatenate([dbg, dbv], 1)


def _rot_a(x, c2, s2):
    return x * c2 + pltpu.roll(x, RET_DK // 2, 1) * s2


def _rot_a_t(dy, c2, s2):
    return dy * c2 + pltpu.roll(dy * s2, RET_DK // 2, 1)


RET_BWD_BLK = 512


def _decay_tile(lg, blk_diff, blk=ATT_BLK):
    r = lax.broadcasted_iota(jnp.int32, (blk, blk), 0)
    c = lax.broadcasted_iota(jnp.int32, (blk, blk), 1)
    rel = r - c + blk_diff * blk
    return jnp.where(rel >= 0, jnp.exp(jnp.maximum(rel, 0).astype(F32) * lg), 0.0)


def _nt(a, b):
    return lax.dot_general(a, b, (((1,), (1,)), ((), ())), preferred_element_type=F32)


def _nn(a, b):
    return lax.dot_general(a, b, (((1,), (0,)), ((), ())), preferred_element_type=F32)


def _tn(a, b):
    return lax.dot_general(a, b, (((0,), (0,)), ((), ())), preferred_element_type=F32)


def _ret_specs(t):
    q = pl.BlockSpec((t, RET_DK), lambda h: (0, h))
    k = pl.BlockSpec((t, RET_DK), lambda h: (0, RET_HEADS + h))
    v = pl.BlockSpec((t, RET_DV), lambda h: (0, RET_HEADS + h))
    g = pl.BlockSpec((t, RET_DV), lambda h: (0, 2 * RET_HEADS + h))
    tab = pl.BlockSpec((t, RET_DK), lambda h: (0, 0))
    lg = pl.BlockSpec((1, 1, LANES), lambda h: (h, 0, 0))
    return q, k, v, g, tab, lg


def _ret_fwd(h, c2, s2, lgt, *, name):
    t = h.shape[0]
    nblk = t // ATT_BLK
    scale = RET_DK ** -0.5

    def body(q_ref, k_ref, v_ref, g_ref, c_ref, s_ref, lg_ref, o_ref, ya_ref, qs, ks, vs):
        c2_, s2_ = c_ref[...], s_ref[...]
        qs[...] = _rot_a(q_ref[...], c2_, s2_).astype(BF16)
        ks[...] = (_rot_a(k_ref[...], c2_, s2_) * scale).astype(BF16)
        vs[...] = v_ref[...].astype(BF16)
        lg = lg_ref[0, :, 0:1]
        for i in range(nblk):
            qi = qs[pl.ds(i * ATT_BLK, ATT_BLK), :]
            acc = jnp.zeros((ATT_BLK, RET_DV), F32)
            for j in range(i + 1):
                sl = pl.ds(j * ATT_BLK, ATT_BLK)
                s = _nt(qi, ks[sl, :]) * _decay_tile(lg, i - j)
                acc = acc + _nn(s.astype(BF16), vs[sl, :])
            rows = pl.ds(i * ATT_BLK, ATT_BLK)
            o_ref[rows, :] = acc
            r = lax.rsqrt(jnp.mean(acc * acc, -1, keepdims=True) + EPS)
            ya_ref[rows, :] = (acc * r * _silu(g_ref[rows, :])).astype(BF16)

    q, k, v, g, tab, lg = _ret_specs(t)
    out = pl.BlockSpec((t, RET_DV), lambda hh: (0, hh))
    return pl.pallas_call(
        body, grid=(RET_HEADS,), in_specs=[q, k, v, g, tab, tab, lg], out_specs=[out, out],
        out_shape=[jax.ShapeDtypeStruct((t, RET_V_W), F32), jax.ShapeDtypeStruct((t, RET_V_W), BF16)],
        scratch_shapes=[pltpu.VMEM((t, RET_DK), BF16), pltpu.VMEM((t, RET_DK), BF16), pltpu.VMEM((t, RET_DV), BF16)],
        name=name, compiler_params=_cp())(h, h, h, h, c2, s2, lgt)


def _ret_bwd(h, c2, s2, lgt, o, dy, *, name):
    t = h.shape[0]
    blk = RET_BWD_BLK
    nblk = t // blk
    scale = RET_DK ** -0.5

    def body(q_ref, k_ref, v_ref, g_ref, c_ref, s_ref, lg_ref, o_ref, dy_ref,
             dq_ref, dk_ref, dv_ref, dg_ref, qs, ks, vs, dos, dka, dva):
        c2_, s2_ = c_ref[...], s_ref[...]
        qs[...] = _rot_a(q_ref[...], c2_, s2_).astype(BF16)
        ks[...] = (_rot_a(k_ref[...], c2_, s2_) * scale).astype(BF16)
        vs[...] = v_ref[...].astype(BF16)
        lg = lg_ref[0, :, 0:1]
        oo = o_ref[...]
        gg = g_ref[...]
        dya = dy_ref[...]
        r = lax.rsqrt(jnp.mean(oo * oo, -1, keepdims=True) + EPS)
        rn = oo * r
        dg_ref[...] = (dya * rn * _dsilu(gg)).astype(BF16)
        drn = dya * _silu(gg)
        dos[...] = (r * (drn - rn * jnp.mean(drn * rn, -1, keepdims=True))).astype(BF16)
        dka[...] = jnp.zeros_like(dka)
        dva[...] = jnp.zeros_like(dva)
        for i in range(nblk):
            rows = pl.ds(i * blk, blk)
            qi = qs[rows, :]
            doi = dos[rows, :]
            dqa = jnp.zeros((blk, RET_DK), F32)
            for j in range(i + 1):
                sl = pl.ds(j * blk, blk)
                dt_ = _decay_tile(lg, i - j, blk)
                kj = ks[sl, :]
                s = (_nt(qi, kj) * dt_).astype(BF16)
                ds = (_nt(doi, vs[sl, :]) * dt_).astype(BF16)
                dqa = dqa + _nn(ds, kj)
                dka[sl, :] += _tn(ds, qi)
                dva[sl, :] += _tn(s, doi)
            dq_ref[rows, :] = _rot_a_t(dqa, c_ref[rows, :], s_ref[rows, :]).astype(BF16)
        dk_ref[...] = (_rot_a_t(dka[...], c2_, s2_) * scale).astype(BF16)
        dv_ref[...] = dva[...].astype(BF16)

    q, k, v, g, tab, lg = _ret_specs(t)
    blk_v = pl.BlockSpec((t, RET_DV), lambda hh: (0, hh))
    blk_k = pl.BlockSpec((t, RET_DK), lambda hh: (0, hh))
    return pl.pallas_call(
        body, grid=(RET_HEADS,), in_specs=[q, k, v, g, tab, tab, lg, blk_v, blk_v],
        out_specs=[blk_k, blk_k, blk_v, blk_v],
        out_shape=[jax.ShapeDtypeStruct((t, RET_QK_W), BF16), jax.ShapeDtypeStruct((t, RET_QK_W), BF16),
                   jax.ShapeDtypeStruct((t, RET_V_W), BF16), jax.ShapeDtypeStruct((t, RET_V_W), BF16)],
        scratch_shapes=[pltpu.VMEM((t, RET_DK), BF16), pltpu.VMEM((t, RET_DK), BF16), pltpu.VMEM((t, RET_DV), BF16),
                        pltpu.VMEM((t, RET_DV), BF16), pltpu.VMEM((t, RET_DK), F32), pltpu.VMEM((t, RET_DV), F32)],
        name=name, compiler_params=_cp())(h, h, h, h, c2, s2, lgt, o, dy)


def _rot_b(x, cb, shi, slo):
    return x * cb + pltpu.roll(x, ROPE_DIMS // 2, 1) * shi + pltpu.roll(x, LANES - ROPE_DIMS // 2, 1) * slo


def _rot_b_t(dy, cb, shi, slo):
    return dy * cb + pltpu.roll(dy * shi, LANES - ROPE_DIMS // 2, 1) + pltpu.roll(dy * slo, ROPE_DIMS // 2, 1)


def _dil_specs(t):
    base = (2 * RET_QK_W + 2 * RET_V_W) // LANES
    npair = DIL_W // LANES
    q = pl.BlockSpec((t, LANES), lambda p: (0, base + p))
    k = pl.BlockSpec((t, LANES), lambda p: (0, base + npair + p))
    v = pl.BlockSpec((t, LANES), lambda p: (0, base + 2 * npair + p))
    tab = pl.BlockSpec((t, LANES), lambda p: (0, 0))
    strip = pl.BlockSpec((ATT_BLK, t), lambda p: (0, 0))
    pair = pl.BlockSpec((t, LANES), lambda p: (0, p))
    return q, k, v, tab, strip, pair


def _dil_fwd(h, cb, shi, slo, strip, *, name):
    t = h.shape[0]
    nblk = t // ATT_BLK
    scale = DIL_HD ** -0.5

    def body(q_ref, k_ref, v_ref, cb_ref, shi_ref, slo_ref, st_ref, o_ref, yb_ref, lse_ref, qs, ks, vs):
        cb_, shi_, slo_ = cb_ref[...], shi_ref[...], slo_ref[...]
        lane = lax.broadcasted_iota(jnp.int32, (t, LANES), 1)
        qr = _rot_b(q_ref[...], cb_, shi_, slo_) * scale
        qs[0] = jnp.where(lane < DIL_HD, qr, 0.0).astype(BF16)
        qs[1] = jnp.where(lane >= DIL_HD, qr, 0.0).astype(BF16)
        ks[...] = _rot_b(k_ref[...], cb_, shi_, slo_).astype(BF16)
        vs[...] = v_ref[...].astype(BF16)
        lane_b = lax.broadcasted_iota(jnp.int32, (ATT_BLK, LANES), 1)
        for i in range(nblk):
            w = (i + 1) * ATT_BLK
            rows = pl.ds(i * ATT_BLK, ATT_BLK)
            logc = st_ref[:, t - w:t]
            outs, lses = [], []
            for hd in range(2):
                s = _nt(qs[hd, rows, :], ks[0:w, :]) + logc
                m = jnp.max(s, -1, keepdims=True)
                p = jnp.exp(s - m)
                l = jnp.sum(p, -1, keepdims=True)
                outs.append(_nn(p.astype(BF16), vs[0:w, :]) / l)
                lses.append(m + jnp.log(l))
            o = jnp.where(lane_b < DIL_HD, outs[0], outs[1])
            o_ref[rows, :] = o
            yb_ref[rows, :] = o.astype(BF16)
            lse_ref[rows, :] = jnp.where(lane_b < DIL_HD, lses[0], lses[1])

    q, k, v, tab, strip_spec, pair = _dil_specs(t)
    return pl.pallas_call(
        body, grid=(DIL_W // LANES,), in_specs=[q, k, v, tab, tab, tab, strip_spec], out_specs=[pair, pair, pair],
        out_shape=[jax.ShapeDtypeStruct((t, DIL_W), F32), jax.ShapeDtypeStruct((t, DIL_W), BF16),
                   jax.ShapeDtypeStruct((t, DIL_W), F32)],
        scratch_shapes=[pltpu.VMEM((2, t, LANES), BF16), pltpu.VMEM((t, LANES), BF16), pltpu.VMEM((t, LANES), BF16)],
        name=name, compiler_params=_cp())(h, h, h, cb, shi, slo, strip)


def _dil_bwd(h, cb, shi, slo, strip, o, lse, dy, *, name):
    t = h.shape[0]
    nblk = t // ATT_BLK
    scale = DIL_HD ** -0.5

    def body(q_ref, k_ref, v_ref, cb_ref, shi_ref, slo_ref, st_ref, o_ref, lse_ref, dy_ref,
             dq_ref, dk_ref, dv_ref, qs, ks, vs, dos, dls, dka, dva):
        cb_, shi_, slo_ = cb_ref[...], shi_ref[...], slo_ref[...]
        lane = lax.broadcasted_iota(jnp.int32, (t, LANES), 1)
        qr = _rot_b(q_ref[...], cb_, shi_, slo_) * scale
        qs[0] = jnp.where(lane < DIL_HD, qr, 0.0).astype(BF16)
        qs[1] = jnp.where(lane >= DIL_HD, qr, 0.0).astype(BF16)
        ks[...] = _rot_b(k_ref[...], cb_, shi_, slo_).astype(BF16)
        vs[...] = v_ref[...].astype(BF16)
        do = dy_ref[...]
        prod = do * o_ref[...]
        d0 = jnp.sum(jnp.where(lane < DIL_HD, prod, 0.0), -1, keepdims=True)
        d1 = jnp.sum(jnp.where(lane >= DIL_HD, prod, 0.0), -1, keepdims=True)
        dls[...] = jnp.where(lane < DIL_HD, d0, d1)
        dos[0] = jnp.where(lane < DIL_HD, do, 0.0).astype(BF16)
        dos[1] = jnp.where(lane >= DIL_HD, do, 0.0).astype(BF16)
        dka[...] = jnp.zeros_like(dka)
        dva[...] = jnp.zeros_like(dva)
        lane_b = lax.broadcasted_iota(jnp.int32, (ATT_BLK, LANES), 1)
        for i in range(nblk):
            w = (i + 1) * ATT_BLK
            rows = pl.ds(i * ATT_BLK, ATT_BLK)
            logc = st_ref[:, t - w:t]
            dqs = []
            for hd in range(2):
                col = hd * DIL_HD
                qh = qs[hd, rows, :]
                doh = dos[hd, rows, :]
                lse_h = lse_ref[rows, col:col + 1]
                dl_h = dls[rows, col:col + 1]
                p = jnp.exp(_nt(qh, ks[0:w, :]) + logc - lse_h)
                dp = _nt(doh, vs[0:w, :])
                ds = (p * (dp - dl_h)).astype(BF16)
                dqs.append(_nn(ds, ks[0:w, :]))
                dka[0:w, :] += _tn(ds, qh)
                dva[0:w, :] += _tn(p.astype(BF16), doh)
            dq = jnp.where(lane_b < DIL_HD, dqs[0], dqs[1]) * scale
            dq_ref[rows, :] = _rot_b_t(dq, cb_ref[rows, :], shi_ref[rows, :], slo_ref[rows, :]).astype(BF16)
        dk_ref[...] = _rot_b_t(dka[...], cb_, shi_, slo_).astype(BF16)
        dv_ref[...] = dva[...].astype(BF16)

    q, k, v, tab, strip_spec, pair = _dil_specs(t)
    dy_spec = pl.BlockSpec((t, LANES), lambda p: (0, RET_V_W // LANES + p))
    return pl.pallas_call(
        body, grid=(DIL_W // LANES,), in_specs=[q, k, v, tab, tab, tab, strip_spec, pair, pair, dy_spec],
        out_specs=[pair, pair, pair],
        out_shape=[jax.ShapeDtypeStruct((t, DIL_W), BF16)] * 3,
        scratch_shapes=[pltpu.VMEM((2, t, LANES), BF16), pltpu.VMEM((t, LANES), BF16), pltpu.VMEM((t, LANES), BF16),
                        pltpu.VMEM((2, t, LANES), BF16), pltpu.VMEM((t, LANES), F32),
                        pltpu.VMEM((t, LANES), F32), pltpu.VMEM((t, LANES), F32)],
        name=name, compiler_params=_cp())(h, h, h, cb, shi, slo, strip, o, lse, dy)


def _gdn_prep_fwd(h, cw, *, name):
    t = h.shape[0]
    qscale = GDN_DK ** -0.5

    def body(hq_ref, hk_ref, hv_ref, wq_ref, wk_ref, wv_ref, q_ref, k_ref, v_ref):
        row = lax.broadcasted_iota(jnp.int32, (t, GDN_DK), 0)
        sq = _silu(_dwconv(hq_ref[...], wq_ref, row))
        sk = _silu(_dwconv(hk_ref[...], wk_ref, row))
        q_ref[0] = sq * lax.rsqrt(jnp.sum(sq * sq, -1, keepdims=True) + 1e-6) * qscale
        k_ref[0] = sk * lax.rsqrt(jnp.sum(sk * sk, -1, keepdims=True) + 1e-6)
        v_ref[0] = _silu(_dwconv(hv_ref[...], wv_ref, row))

    hs = lambda off: pl.BlockSpec((t, GDN_DK), lambda i: (0, i + off))
    ws = lambda off: pl.BlockSpec((GDN_CONV, GDN_DK), lambda i: (0, i + off))
    out = pl.BlockSpec((1, t, GDN_DK), lambda i: (i, 0, 0))
    return pl.pallas_call(
        body, grid=(GDN_HEADS,), in_specs=[hs(0), hs(8), hs(16), ws(0), ws(8), ws(16)], out_specs=[out, out, out],
        out_shape=[jax.ShapeDtypeStruct((GDN_HEADS, t, GDN_DK), F32)] * 3,
        name=name, compiler_params=_cp())(h, h, h, cw, cw, cw)


def _gdn_prep_bwd(h, cw, dq, dk, dv, *, name):
    t = h.shape[0]
    qscale = GDN_DK ** -0.5

    def body(hq_ref, hk_ref, hv_ref, wq_ref, wk_ref, wv_ref, dq_ref, dk_ref, dv_ref,
             dhq_ref, dhk_ref, dhv_ref, dwq_ref, dwk_ref, dwv_ref):
        row = lax.broadcasted_iota(jnp.int32, (t, GDN_DK), 0)

        def one(h_ref, w_ref, d_ref, dh_ref, dw_ref, norm, sc):
            u = h_ref[...]
            c = _dwconv(u, w_ref, row)
            d = d_ref[0]
            if norm:
                s = _silu(c)
                r = lax.rsqrt(jnp.sum(s * s, -1, keepdims=True) + 1e-6)
                n = s * r
                d = d * sc
                d = r * (d - n * jnp.sum(d * n, -1, keepdims=True))
            dc = d * _dsilu(c)
            dh_ref[...] = _dwconv_bwd(u, w_ref, dc, row, dw_ref).astype(BF16)

        one(hq_ref, wq_ref, dq_ref, dhq_ref, dwq_ref, True, qscale)
        one(hk_ref, wk_ref, dk_ref, dhk_ref, dwk_ref, True, 1.0)
        one(hv_ref, wv_ref, dv_ref, dhv_ref, dwv_ref, False, 1.0)

    hs = lambda off: pl.BlockSpec((t, GDN_DK), lambda i: (0, i + off))
    ws = lambda off: pl.BlockSpec((GDN_CONV, GDN_DK), lambda i: (0, i + off))
    hd = pl.BlockSpec((1, t, GDN_DK), lambda i: (i, 0, 0))
    return pl.pallas_call(
        body, grid=(GDN_HEADS,), in_specs=[hs(0), hs(8), hs(16), ws(0), ws(8), ws(16), hd, hd, hd],
        out_specs=[hs(0), hs(0), hs(0), ws(0), ws(0), ws(0)],
        out_shape=[jax.ShapeDtypeStruct((t, GDN_W), BF16)] * 3 + [jax.ShapeDtypeStruct((GDN_CONV, GDN_W), F32)] * 3,
        name=name, compiler_params=_cp())(h, h, h, cw, cw, cw, dq, dk, dv)


def _make_mm2(wide):
    def raw(a, b, dims):
        if wide:
            return lax.dot_general(a, b, (dims, ((), ())), precision=lax.Precision.HIGHEST, preferred_element_type=F32)
        return lax.dot_general(a.astype(BF16), b.astype(BF16), (dims, ((), ())), preferred_element_type=F32)

    @jax.custom_vjp
    def nn(a, b):
        return raw(a, b, ((1,), (0,)))

    @jax.custom_vjp
    def nt(a, b):
        return raw(a, b, ((1,), (1,)))

    @jax.custom_vjp
    def tn(a, b):
        return raw(a, b, ((0,), (0,)))

    nn.defvjp(lambda a, b: (nn(a, b), (a, b)), lambda r, g: (nt(g, r[1]), tn(r[0], g)))
    nt.defvjp(lambda a, b: (nt(a, b), (a, b)), lambda r, g: (nn(g, r[1]), tn(g, r[0])))
    tn.defvjp(lambda a, b: (tn(a, b), (a, b)), lambda r, g: (nt(r[1], g), nn(r[0], g)))
    return nn, nt, tn


_NN, _NT, _TN = _make_mm2(False)
_NNW, _NTW, _TNW = _make_mm2(True)


def _square_masks(c):
    ri = lax.broadcasted_iota(jnp.int32, (c, c), 0)
    ci = lax.broadcasted_iota(jnp.int32, (c, c), 1)
    return ri >= ci, ri > ci, ri == ci


def _cumsum_rows(m):
    tri, _, _ = _square_masks(m.shape[0])
    return _NNW(tri.astype(F32), m)


def _transpose_sq(m):
    _, _, eye = _square_masks(m.shape[0])
    return _NTW(eye.astype(F32), m)


@jax.custom_vjp
def _inv_unit_lower(l):
    c = l.shape[0]
    _, _, eye = _square_masks(c)
    p = -l
    t = eye.astype(F32) + p
    for _ in range(int(math.log2(c)) - 1):
        p = _NNW(p, p)
        t = t + _NNW(t, p)
    return t


def _inv_fwd(l):
    t = _inv_unit_lower(l)
    return t, t


def _inv_bwd(t, dt):
    return (-_NTW(_TNW(t, dt), t),)


_inv_unit_lower.defvjp(_inv_fwd, _inv_bwd)


@jax.custom_vjp
def _inv_known(l, t):
    return t


_inv_known.defvjp(lambda l, t: (t, t), lambda t, dt: (_inv_bwd(t, dt)[0], jnp.zeros_like(t)))


def _softplus(x):
    return jnp.maximum(x, 0.0) + jnp.log1p(jnp.exp(-jnp.abs(x)))


def _gdn_chunk(q, k, v, braw, araw, alog, dtb, state, inv=None):
    c = q.shape[0]
    dv = v.shape[1]
    tri, strict, _ = _square_masks(c)
    beta = _sig(braw)
    g = -jnp.exp(alog) * _softplus(araw + dtb)
    gcm = _cumsum_rows(g * jnp.ones((c, c), F32))
    gct = _transpose_sq(gcm)
    decay = jnp.where(tri, jnp.exp(jnp.where(tri, gcm - gct, 0.0)), 0.0)
    gc = jnp.sum(gcm, 1, keepdims=True) * (1.0 / c)
    glast = jnp.sum(g, 0, keepdims=True)
    egc = jnp.exp(gc)
    kb = k * beta
    low = jnp.where(strict, _NT(kb, k) * decay, 0.0)
    tm = _inv_unit_lower(low) if inv is None else _inv_known(low, inv)
    sol = _NNW(tm, jnp.concatenate([v * beta, kb * egc], 1))
    u, w = sol[:, :dv], sol[:, dv:]
    attn = jnp.where(tri, _NT(q, k) * decay, 0.0)
    k_dec = k * jnp.exp(glast - gc)
    q_dec = q * egc
    v_new = u - _NN(w, state)
    o = _NN(q_dec, state) + _NN(attn, v_new)
    new_state = state * jnp.exp(glast) + _TN(k_dec, v_new)
    return o, new_state, tm


def _gdn_specs(t, rev):
    nch = t // GDN_CHUNK
    cm = (lambda n: nch - 1 - n) if rev else (lambda n: n)
    tok = pl.BlockSpec((GDN_HEADS, GDN_CHUNK, GDN_DK), lambda n: (0, cm(n), 0))
    par = pl.BlockSpec((GDN_HEADS, 1, LANES), lambda n: (0, 0, 0))
    st = pl.BlockSpec((GDN_HEADS, 1, GDN_DK, GDN_DV), lambda n: (0, cm(n), 0, 0))
    inv = pl.BlockSpec((GDN_HEADS, GDN_CHUNK, GDN_CHUNK), lambda n: (0, cm(n), 0))
    sc = pl.BlockSpec((GDN_CHUNK, LANES), lambda n: (cm(n), 4 * GDN_W // LANES))
    return tok, par, st, inv, sc


def _head_columns(sc_ref, first):
    return jnp.stack([sc_ref[:, first + hh:first + hh + 1] for hh in range(GDN_HEADS)])


def _gdn_core_fwd(q, k, v, h, alog, dtb, *, name):
    t = q.shape[1]
    nch = t // GDN_CHUNK

    def body(q_ref, k_ref, v_ref, sc_ref, al_ref, dt_ref, o_ref, st_ref, inv_ref, state):
        @pl.when(pl.program_id(0) == 0)
        def _():
            state[...] = jnp.zeros_like(state)

        s0 = state[...]
        st_ref[:, 0] = s0
        o, s1, tm = jax.vmap(_gdn_chunk)(q_ref[...], k_ref[...], v_ref[...], _head_columns(sc_ref, 0),
                                         _head_columns(sc_ref, GDN_HEADS), al_ref[:, :, 0:1], dt_ref[:, :, 0:1], s0)
        o_ref[...] = o
        inv_ref[...] = tm
        state[...] = s1

    tok, par, st, inv, sc = _gdn_specs(t, False)
    return pl.pallas_call(
        body, grid=(nch,), in_specs=[tok, tok, tok, sc, par, par], out_specs=[tok, st, inv],
        out_shape=[jax.ShapeDtypeStruct((GDN_HEADS, t, GDN_DV), F32),
                   jax.ShapeDtypeStruct((GDN_HEADS, nch, GDN_DK, GDN_DV), F32),
                   jax.ShapeDtypeStruct((GDN_HEADS, t, GDN_CHUNK), F32)],
        scratch_shapes=[pltpu.VMEM((GDN_HEADS, GDN_DK, GDN_DV), F32)],
        name=name, compiler_params=_cp())(q, k, v, h, alog, dtb)


def _gdn_core_bwd(q, k, v, h, alog, dtb, states, invs, do, *, name):
    t = q.shape[1]
    nch = t // GDN_CHUNK

    def body(q_ref, k_ref, v_ref, sc_ref, al_ref, dt_ref, st_ref, inv_ref, do_ref,
             dq_ref, dk_ref, dv_ref, dsc_ref, dal_ref, ddt_ref, dstate):
        @pl.when(pl.program_id(0) == 0)
        def _():
            dstate[...] = jnp.zeros_like(dstate)
            dal_ref[...] = jnp.zeros_like(dal_ref)
            ddt_ref[...] = jnp.zeros_like(ddt_ref)

        args = (q_ref[...], k_ref[...], v_ref[...], _head_columns(sc_ref, 0), _head_columns(sc_ref, GDN_HEADS),
                al_ref[:, :, 0:1], dt_ref[:, :, 0:1], st_ref[:, 0])
        tm = inv_ref[...]

        def chunk(*a):
            return jax.vmap(_gdn_chunk)(*a, tm)[:2]

        _, pull = jax.vjp(chunk, *args)
        dq, dk, dv, dbr, dar, dal, ddt, ds = pull((do_ref[...], dstate[...]))
        dq_ref[...] = dq
        dk_ref[...] = dk
        dv_ref[...] = dv
        lane = lax.broadcasted_iota(jnp.int32, (GDN_CHUNK, LANES), 1)
        dsc = jnp.zeros((GDN_CHUNK, LANES), F32)
        for hh in range(GDN_HEADS):
            dsc = jnp.where(lane == hh, dbr[hh], dsc)
            dsc = jnp.where(lane == GDN_HEADS + hh, dar[hh], dsc)
        dsc_ref[...] = dsc
        dal_ref[...] += dal + jnp.zeros((GDN_HEADS, 1, LANES), F32)
        ddt_ref[...] += ddt + jnp.zeros((GDN_HEADS, 1, LANES), F32)
        dstate[...] = ds

    tok, par, st, inv, sc = _gdn_specs(t, True)
    tokshape = jax.ShapeDtypeStruct((GDN_HEADS, t, GDN_DK), F32)
    parshape = jax.ShapeDtypeStruct((GDN_HEADS, 1, LANES), F32)
    nch_map = pl.BlockSpec((GDN_CHUNK, LANES), lambda n: (nch - 1 - n, 0))
    return pl.pallas_call(
        body, grid=(nch,), in_specs=[tok, tok, tok, sc, par, par, st, inv, tok],
        out_specs=[tok, tok, tok, nch_map, par, par],
        out_shape=[tokshape] * 3 + [jax.ShapeDtypeStruct((t, LANES), F32)] + [parshape] * 2,
        scratch_shapes=[pltpu.VMEM((GDN_HEADS, GDN_DK, GDN_DV), F32)],
        name=name, compiler_params=_cp())(q, k, v, h, alog, dtb, states, invs, do)


GDN_ROWS = 2048


def _gdn_post_fwd(o, h, nw, *, name):
    t = o.shape[1]

    def body(o_ref, g_ref, nw_ref, y_ref):
        oo = o_ref[0]
        r = lax.rsqrt(jnp.mean(oo * oo, -1, keepdims=True) + EPS)
        y_ref[...] = (oo * r * nw_ref[...] * _silu(g_ref[...])).astype(BF16)

    return pl.pallas_call(
        body, grid=(GDN_HEADS, t // GDN_ROWS),
        in_specs=[pl.BlockSpec((1, GDN_ROWS, GDN_DV), lambda hh, i: (hh, i, 0)),
                  pl.BlockSpec((GDN_ROWS, GDN_DV), lambda hh, i: (i, 3 * GDN_HEADS + hh)),
                  pl.BlockSpec((1, GDN_DV), lambda hh, i: (0, 0))],
        out_specs=pl.BlockSpec((GDN_ROWS, GDN_DV), lambda hh, i: (i, hh)),
        out_shape=jax.ShapeDtypeStruct((t, GDN_W), BF16), name=name, compiler_params=_cp())(o, h, nw)


def _gdn_post_bwd(o, h, nw, dy, *, name):
    t = o.shape[1]

    def body(o_ref, g_ref, nw_ref, dy_ref, do_ref, dg_ref, dnw_ref):
        oo, gg, nw_, dy_ = o_ref[0], g_ref[...], nw_ref[...], dy_ref[...]
        r = lax.rsqrt(jnp.mean(oo * oo, -1, keepdims=True) + EPS)
        n = oo * r
        sg = _silu(gg)
        dg_ref[...] = (dy_ * n * nw_ * _dsilu(gg)).astype(BF16)
        dn = dy_ * sg * nw_
        do_ref[0] = r * (dn - n * jnp.mean(dn * n, -1, keepdims=True))

        @pl.when((pl.program_id(0) == 0) & (pl.program_id(1) == 0))
        def _():
            dnw_ref[...] = jnp.zeros_like(dnw_ref)

        dnw_ref[...] += jnp.sum(dy_ * sg * n, 0, keepdims=True)

    return pl.pallas_call(
        body, grid=(GDN_HEADS, t // GDN_ROWS),
        in_specs=[pl.BlockSpec((1, GDN_ROWS, GDN_DV), lambda hh, i: (hh, i, 0)),
                  pl.BlockSpec((GDN_ROWS, GDN_DV), lambda hh, i: (i, 3 * GDN_HEADS + hh)),
                  pl.BlockSpec((1, GDN_DV), lambda hh, i: (0, 0)),
                  pl.BlockSpec((GDN_ROWS, GDN_DV), lambda hh, i: (i, hh))],
        out_specs=[pl.BlockSpec((1, GDN_ROWS, GDN_DV), lambda hh, i: (hh, i, 0)),
                   pl.BlockSpec((GDN_ROWS, GDN_DV), lambda hh, i: (i, hh)),
                   pl.BlockSpec((1, GDN_DV), lambda hh, i: (0, 0))],
        out_shape=[jax.ShapeDtypeStruct((GDN_HEADS, t, GDN_DV), F32), jax.ShapeDtypeStruct((t, GDN_W), BF16),
                   jax.ShapeDtypeStruct((1, GDN_DV), F32)],
        name=name, compiler_params=_cp())(o, h, nw, dy)


def _tables(positions):
    pos = positions.astype(F32)[:, None]
    half = RET_DK // 2
    inv = jnp.power(RET_THETA, -jnp.arange(half, dtype=F32) * 2.0 / RET_DK)
    ang = pos * inv
    cos, sin = jnp.cos(ang), jnp.sin(ang)
    c2a = jnp.concatenate([cos, cos], 1)
    s2a = jnp.concatenate([-sin, sin], 1)
    hb = ROPE_DIMS // 2
    invb = jnp.power(ROPE_THETA, -jnp.arange(hb, dtype=F32) * 2.0 / ROPE_DIMS)
    angb = pos * invb
    cosb, sinb = jnp.cos(angb), jnp.sin(angb)
    t = pos.shape[0]
    ones = jnp.ones((t, DIL_HD - ROPE_DIMS), F32)
    zeros = jnp.zeros((t, DIL_HD - ROPE_DIMS), F32)
    z8 = jnp.zeros((t, hb), F32)
    cb = jnp.concatenate([cosb, cosb, ones] * 2, 1)
    shi = jnp.concatenate([z8, sinb, zeros] * 2, 1)
    slo = jnp.concatenate([-sinb, z8, zeros] * 2, 1)
    lg = jnp.log1p(-jnp.power(2.0, -5.0 - jnp.arange(RET_HEADS, dtype=F32)))
    lgt = jnp.broadcast_to(lg[:, None, None], (RET_HEADS, 1, LANES))
    delta = jnp.arange(ATT_BLK, dtype=jnp.int32)[:, None] + (SEQ - ATT_BLK) - jnp.arange(SEQ, dtype=jnp.int32)[None, :]
    cnt = jnp.zeros(delta.shape, F32)
    for (w, d) in DIL_PAIRS:
        cnt = cnt + ((delta >= 0) & (delta <= w) & (delta % d == 0)).astype(F32)
    strip = jnp.where(cnt > 0, jnp.log(jnp.maximum(cnt, 1.0)), NEG)
    return c2a, s2a, cb, shi, slo, lgt, strip


def _local_step(x, tables, target, get_w, mid, put_g, small):
    c2a, s2a, cb, shi, slo, lgt, strip = tables
    t = x.shape[0]
    saved = []
    xf = x
    xb = x.astype(BF16)
    for layer in range(DEPTH):
        j = layer // 2
        L = f"L{layer}_"
        W, dep = get_w(layer, "mixer", xb)
        rec = {"x": xf, "xb": xb}
        if layer % 2 == 0:
            h = _mm(xb, W["in_t"], tb=True, name=L + "ev_in", dep=dep)
            ro, ya = _ret_fwd(h, c2a, s2a, lgt, name=L + "ret_fwd")
            do_, yb, lse = _dil_fwd(h, cb, shi, slo, strip, name=L + "dil_fwd")
            y = [ya, yb]
            rec.update(h=h, ro=ro, dil_o=do_, lse=lse, y=y)
        else:
            h = _mm(xb, W["in_t"], tb=True, name=L + "od_in", dep=dep)
            cw = W["conv"]
            q, k, v = _gdn_prep_fwd(h, cw, name=L + "gdn_prep")
            alog = jnp.broadcast_to(small["od_a_log"][j][:, None, None], (GDN_HEADS, 1, LANES))
            dtb = jnp.broadcast_to(small["od_dt_bias"][j][:, None, None], (GDN_HEADS, 1, LANES))
            o, states, invs = _gdn_core_fwd(q, k, v, h, alog, dtb, name=L + "gdn_fwd")
            nw = small["od_norm_w"][j][None, :]
            y = [_gdn_post_fwd(o, h, nw, name=L + "gdn_post")]
            rec.update(h=h, q=q, k=k, v=v, alog=alog, dtb=dtb, states=states, invs=invs, o=o, y=y, nw=nw, cw=cw)
        z1, x1, x1b = _mm_ln_fwd(y, W["out"], xf, small["ln1_g"][layer][None], small["ln1_b"][layer][None],
                                 name=L + "out_ln1", dep=mid(layer, "mixer", y[0]))
        rec["Wm"] = W
        W, dep = get_w(layer, "ffn", x1b)
        rec["Wf"] = W
        fcw = W["fconv"]
        fcb = small["ffn_conv_b"][layer][None]
        ug, uv, a = _ffn_up_mid(x1b, W["up_t"], fcw, fcb, name=L + "ffn_up_mid", dep=dep)
        z2, x2, x2b = _mm_ln_fwd([a], W["down"], x1, small["ln2_g"][layer][None], small["ln2_b"][layer][None],
                                 name=L + "down_ln2", dep=mid(layer, "ffn", a))
        rec.update(z1=z1, x1b=x1b, ug=ug, uv=uv, a=a, z2=z2, fcw=fcw, fcb=fcb)
        saved.append(rec)
        xf, xb = x2, x2b

    dy, lossv = _loss_head(xf, target, name="loss_head")
    loss = lossv[0, 0]

    gS = {n: [None] * small[n].shape[0] for n in small}
    below = None
    for layer in reversed(range(DEPTH)):
        j = layer // 2
        L = f"L{layer}_"
        rec = saved[layer]
        Wm, Wf = rec["Wm"], rec["Wf"]
        g = {}
        if below is None:
            dz2, dz2b, dg2, db2 = _ln_bwd(rec["z2"], small["ln2_g"][layer][None], dy, None, name=L + "ln2_bwd")
        else:
            dz2, dz2b, dg2, db2 = _mm_ln_bwd(below[0], below[1], rec["z2"], small["ln2_g"][layer][None], below[2],
                                             name=L + "ln2_bwd", dep=below[3])
        gS["ln2_g"][layer], gS["ln2_b"][layer] = dg2[0], db2[0]
        g["down"] = _mm(rec["a"], dz2b, ta=True, name=L + "ffn_down_dw", out_dtype=BF16)
        du, dcw, dcb = _ffn_mid_bwd(rec["ug"], rec["uv"], rec["fcw"], rec["fcb"], dz2b, Wf["down"], name=L + "ffn_mid_bwd")
        g["fconv"] = dcw.astype(BF16)
        gS["ffn_conv_b"][layer] = dcb[0]
        g["up_t"] = _mm_tn_parts(du, rec["x1b"], name=L + "ffn_up_dw")
        dep = put_g(layer, "ffn", g)
        dz1, dz1b, dg1, db1 = _mm_ln_bwd(du, Wf["up_t"], rec["z1"], small["ln1_g"][layer][None], dz2,
                                         name=L + "ln1_bwd", dep=dep)
        gS["ln1_g"][layer], gS["ln1_b"][layer] = dg1[0], db1[0]
        g = {}
        if layer % 2 == 0:
            g["out"] = _mm_tn_parts(rec["y"], dz1b, name=L + "ev_out_dw")
            dyy = _mm(dz1b, Wm["out"], tb=True, name=L + "ev_out_dx")
            dqa, dka, dva, dga = _ret_bwd(rec["h"], c2a, s2a, lgt, rec["ro"], dyy, name=L + "ret_bwd")
            dqb, dkb, dvb = _dil_bwd(rec["h"], cb, shi, slo, strip, rec["dil_o"], rec["lse"], dyy, name=L + "dil_bwd")
            dh = [dqa, dka, dva, dga, dqb, dkb, dvb]
            g["in_t"] = _mm_tn_parts(dh, rec["xb"], name=L + "ev_in_dw")
            dep = put_g(layer, "mixer", g)
        else:
            g["out"] = _mm(rec["y"][0], dz1b, ta=True, name=L + "od_out_dw", out_dtype=BF16)
            dyy = _mm(dz1b, Wm["out"], tb=True, name=L + "od_out_dx")
            do, dgate, dnw = _gdn_post_bwd(rec["o"], rec["h"], rec["nw"], dyy, name=L + "gdn_post_bwd")
            gS["od_norm_w"][j] = dnw[0]
            dq, dk, dv, dsc, dal, ddt = _gdn_core_bwd(
                rec["q"], rec["k"], rec["v"], rec["h"], rec["alog"], rec["dtb"], rec["states"], rec["invs"], do,
                name=L + "gdn_bwd")
            gS["od_a_log"][j] = dal[:, 0, 0]
            gS["od_dt_bias"][j] = ddt[:, 0, 0]
            dhq, dhk, dhv, dwq, dwk, dwv = _gdn_prep_bwd(rec["h"], rec["cw"], dq, dk, dv, name=L + "gdn_prep_bwd")
            g["conv"] = jnp.concatenate([dwq, dwk, dwv], 1).astype(BF16)
            dh = [dhq, dhk, dhv, dgate, dsc.astype(BF16)]
            g["in_t"] = (_mm_tn_parts(dh[:4], rec["xb"], name=L + "od_in_dw"),
                         _mm(dh[4], rec["xb"], ta=True, name=L + "od_in_dw_logits", out_dtype=BF16))
            dep = put_g(layer, "mixer", g)
        below = (dh, Wm["in_t"], dz1, dep)
    grad_x = _axpy(_mm(jnp.concatenate(below[0], 1), below[1], name="L0_in_dx", dep=below[3]), below[2], name="grad_x")
    gS = {n: jnp.stack(v) for n, v in gS.items()}
    return loss, grad_x, gS


HBM = pl.BlockSpec(memory_space=pltpu.HBM)


def _me():
    return lax.axis_index("x"), lax.axis_index("y"), lax.axis_index("c")


def _my_index():
    x, y, c = _me()
    return 4 * x + 2 * y + c


SEM = pl.BlockSpec(memory_space=pltpu.SEMAPHORE)
ANY = pl.BlockSpec(memory_space=pl.ANY)
PLANS = {"scatter": (1, 2, 3, 4, 5, 6, 7), "spread": (1, 2, 4, 6), "relay": (2, 4, 6), "all": (1, 2, 3, 4, 5, 6, 7)}
ONE_SOURCE = ("spread", "all")
SIBLING = 1


def _peer(kk):
    x, y, c = _me()
    return x ^ (kk >> 2), y ^ ((kk >> 1) & 1), c ^ (kk & 1)


def _peer_index(kk):
    px, py, pc = _peer(kk)
    return 4 * px + 2 * py + pc


def _job_copies(mode, srcs, lands, send_sems, recv_sems, incoming):
    myid = _my_index()
    plan = PLANS[mode]
    out = []
    for a in range(len(lands)):
        for idx, kk in enumerate(plan):
            if mode == "relay":
                to, src = _peer(SIBLING), lands[a].at[_peer_index(kk)]
                slot_there, slot_here = _peer_index(kk), _peer_index(kk ^ SIBLING)
            else:
                to, src = _peer(kk), (srcs[a] if mode in ONE_SOURCE else srcs[a].at[_peer_index(kk)])
                slot_there, slot_here = myid, _peer_index(kk)
            sem = a * len(plan) + idx
            out.append(pltpu.make_async_remote_copy(
                src_ref=src, dst_ref=lands[a].at[slot_here if incoming else slot_there],
                send_sem=send_sems.at[sem], recv_sem=recv_sems.at[sem], device_id=to, device_id_type=MESH))
    return out


def _split_jobs(jobs, arrays):
    out, o = [], 0
    for (_, srcs, lands) in jobs:
        out.append((arrays[o:o + len(srcs)], arrays[o + len(srcs):o + len(srcs) + len(lands)]))
        o += len(srcs) + len(lands)
    return out


def _exchange_start(jobs, after, *, name):
    jobs = [(mode, list(srcs), [lax.empty((N_DEV, *s.shape) if mode in ONE_SOURCE else s.shape, s.dtype) for s in srcs]
             if lands is None else list(lands)) for (mode, srcs, lands) in jobs]
    flat = [a for (_, srcs, lands) in jobs for a in (*srcs, *lands)]
    n, nj = len(flat), len(jobs)
    nsem = [len(PLANS[mode]) * len(lands) for (mode, _, lands) in jobs]

    def body(*refs):
        o = n + (0 if after is None else 1)
        sems, token = refs[o:o + 2 * nj], refs[o + 2 * nj + n]
        for ji, ((mode, _, _), (src, land)) in enumerate(zip(jobs, _split_jobs(jobs, refs[:n]))):
            for cp in _job_copies(mode, src, land, sems[2 * ji], sems[2 * ji + 1], False):
                cp.start()
        token[...] = jnp.zeros_like(token)

    outs = pl.pallas_call(
        body, name=name,
        out_shape=(*[pltpu.SemaphoreType.DMA((ns,)) for ns in nsem for _ in range(2)],
                   *[pltpu.HBM(a.shape, a.dtype) for a in flat], jax.ShapeDtypeStruct((8, LANES), F32)),
        in_specs=[HBM] * n + ([] if after is None else [ANY]),
        out_specs=(*[SEM] * (2 * nj), *[HBM] * n, pl.BlockSpec(memory_space=pltpu.VMEM)),
        input_output_aliases={i: 2 * nj + i for i in range(n)},
        compiler_params=pltpu.CompilerParams(has_side_effects=pltpu.SideEffectType.DATAFLOW_SIDE_EFFECTING),
    )(*[pltpu.with_memory_space_constraint(a, pltpu.HBM) for a in flat], *([] if after is None else [after]))
    thru = _split_jobs(jobs, list(outs[2 * nj:2 * nj + n]))
    started = [(mode, outs[2 * ji], outs[2 * ji + 1], src, land) for ji, ((mode, _, _), (src, land)) in enumerate(zip(jobs, thru))]
    return started, outs[2 * nj + n]


def _exchange_wait(started, after, *, name):
    jobs = [(mode, srcs, lands) for (mode, _, _, srcs, lands) in started]
    flat = [a for (_, srcs, lands) in jobs for a in (*srcs, *lands)]
    n, nj = len(flat), len(jobs)

    def body(*refs):
        sems = refs[n:n + 2 * nj]
        for ji, ((mode, _, _), (src, land)) in enumerate(zip(jobs, _split_jobs(jobs, refs[:n]))):
            for cp in _job_copies(mode, src, land, sems[2 * ji], sems[2 * ji + 1], True):
                cp.wait_send()
                cp.wait_recv()

    outs = pl.pallas_call(
        body, name=name, out_shape=tuple(pltpu.HBM(a.shape, a.dtype) for a in flat),
        in_specs=[HBM] * n + [SEM] * (2 * nj) + [ANY], out_specs=tuple([HBM] * n),
        input_output_aliases={i: i for i in range(n)},
        compiler_params=pltpu.CompilerParams(has_side_effects=pltpu.SideEffectType.DATAFLOW_SIDE_EFFECTING),
    )(*flat, *[s for (_, ss, rs, _, _) in started for s in (ss, rs)], after)
    return _split_jobs(jobs, list(outs))


def _sum8(land, stack, j, depth, *, name):
    _, rr, cc = land.shape
    tr = _row_tile(rr)

    def body(l_ref, *rest):
        o_ref = rest[-1]
        acc = l_ref[0].astype(F32)
        for d in range(1, N_DEV):
            acc = acc + l_ref[d].astype(F32)
        o_ref[0] = acc

    prev = [] if stack is None else [stack]
    return pl.pallas_call(
        body, grid=(rr // tr,),
        in_specs=[pl.BlockSpec((N_DEV, tr, cc), lambda i: (0, i, 0))] + [pl.BlockSpec(memory_space=pl.ANY)] * len(prev),
        out_specs=pl.BlockSpec((1, tr, cc), lambda i: (j, i, 0)), out_shape=jax.ShapeDtypeStruct((depth, rr, cc), F32),
        input_output_aliases={1: 0} if prev else {}, name=name, compiler_params=_cp())(land, *prev)


def _row_tile(rr):
    for cand in (512, 384, 256, 192, 176, 128, 64, 32, 16, 8):
        if rr % cand == 0:
            return cand
    return rr


def _adam_math(w, g, m, v):
    m = ADAM_B1 * m + (1.0 - ADAM_B1) * g
    v = ADAM_B2 * v + (1.0 - ADAM_B2) * (g * g)
    m_hat = m / (1.0 - ADAM_B1 ** ADAM_STEP)
    v_hat = v / (1.0 - ADAM_B2 ** ADAM_STEP)
    delta = -ADAM_LR * (m_hat / (jnp.sqrt(v_hat) + ADAM_EPS) + ADAM_WD * w)
    return delta, m, v


def _adamw_sharded(w, m, v, g, *, name):
    ll, rr, cc = w.shape
    tr = _row_tile(rr)

    def body(w_ref, m_ref, v_ref, g_ref, d_ref, nm_ref, nv_ref):
        d, nm, nv = _adam_math(w_ref[...], g_ref[...], m_ref[...], v_ref[...])
        d_ref[...] = d
        nm_ref[...] = nm
        nv_ref[...] = nv

    blk = pl.BlockSpec((1, tr, cc), lambda l, i: (l, i, 0))
    sh = jax.ShapeDtypeStruct((ll, rr, cc), F32)
    return pl.pallas_call(
        body, grid=(ll, rr // tr), in_specs=[blk] * 4, out_specs=[blk] * 3, out_shape=[sh] * 3,
        name=name, compiler_params=_cp())(w, m, v, g)


def _adamw_small(w, m, v, gall, *, name):
    rr = w.shape[0]

    def body(w_ref, m_ref, v_ref, g_ref, go_ref, d_ref, nm_ref, nv_ref):
        g = g_ref[0]
        for kk in range(1, N_DEV):
            g = g + g_ref[kk]
        d, nm, nv = _adam_math(w_ref[...], g, m_ref[...], v_ref[...])
        go_ref[...] = g
        d_ref[...] = d
        nm_ref[...] = nm
        nv_ref[...] = nv

    sh = jax.ShapeDtypeStruct((rr, LANES), F32)
    return pl.pallas_call(body, out_shape=[sh] * 4, name=name, compiler_params=_cp())(w, m, v, gall)


SHARDED = ("ev_w_in", "ev_w_out", "od_w_in", "od_conv_w", "od_w_out", "ffn_w_up", "ffn_conv_w", "ffn_w_down")
SMALL = ("od_a_log", "od_dt_bias", "od_norm_w", "ffn_conv_b", "ln1_g", "ln1_b", "ln2_g", "ln2_b")
ALL_W = ("ev_w_in", "ev_w_out", "od_w_in", "od_conv_w", "od_a_log", "od_dt_bias", "od_norm_w", "od_w_out",
         "ffn_w_up", "ffn_conv_w", "ffn_conv_b", "ffn_w_down", "ln1_g", "ln1_b", "ln2_g", "ln2_b")


def _layer_items(layer):
    j = layer // 2
    if layer % 2 == 0:
        mixer = [("in_t", "ev_w_in", j, "colT"), ("out", "ev_w_out", j, "row")]
    else:
        mixer = [("in_t", "od_w_in", j, "colT"), ("conv", "od_conv_w", j, "colsmall"), ("out", "od_w_out", j, "row")]
    return mixer + [("up_t", "ffn_w_up", layer, "colT"), ("fconv", "ffn_conv_w", layer, "colsmall"),
                    ("down", "ffn_w_down", layer, "row")]


OD_SHARD = OD_IN // N_DEV
OD_SHARD_PAD = OD_IN_PAD // N_DEV


def _od_pack(g, *, name):
    d = g.shape[-1]

    def body(g_ref, o_ref):
        for n in range(N_DEV):
            o_ref[OD_SHARD * n:OD_SHARD * (n + 1), :] = g_ref[n, 0:OD_SHARD, :]
        o_ref[OD_IN:OD_IN_PAD, :] = jnp.zeros((OD_IN_PAD - OD_IN, d), g.dtype)

    return pl.pallas_call(body, out_shape=jax.ShapeDtypeStruct((OD_IN_PAD, d), g.dtype), name=name,
                          compiler_params=_cp())(g)


def _od_unpack(main, tail, *, name):
    d = main.shape[-1]
    split = main.shape[0]

    def body(m_ref, t_ref, o_ref):
        for n in range(N_DEV):
            lo, hi = OD_SHARD * n, OD_SHARD * (n + 1)
            from_main = min(hi, split) - lo
            o_ref[n, 0:from_main, :] = m_ref[lo:lo + from_main, :]
            if hi > split:
                o_ref[n, from_main:OD_SHARD, :] = t_ref[0:hi - split, :]
            o_ref[n, OD_SHARD:OD_SHARD_PAD, :] = jnp.zeros((OD_SHARD_PAD - OD_SHARD, d), main.dtype)

    return pl.pallas_call(body, out_shape=jax.ShapeDtypeStruct((N_DEV, OD_SHARD_PAD, d), main.dtype), name=name,
                          compiler_params=_cp())(main, tail)


def _to_send(kind, name, w, j):
    if kind == "colT":
        s = w[j].T.astype(BF16)
        return jnp.pad(s, ((0, OD_SHARD_PAD - OD_SHARD), (0, 0))) if name == "od_w_in" else s
    return w[j].astype(BF16) if kind == "row" else w[j]


def _from_gather(kind, name, g, tag):
    if kind == "colsmall":
        return jnp.transpose(g, (1, 0, 2)).reshape(g.shape[1], -1)
    if name == "od_w_in":
        return _od_pack(g, name=tag + "_pack")
    return g.reshape(-1, g.shape[-1])


def _by_owner(kind, name, gfull, tag):
    if kind == "colsmall":
        kk, c8 = gfull.shape
        return jnp.transpose(gfull.reshape(kk, N_DEV, c8 // N_DEV), (1, 0, 2))
    if name == "od_w_in":
        return _od_unpack(*gfull, name=tag + "_unpack")
    return gfull.reshape(N_DEV, gfull.shape[0] // N_DEV, gfull.shape[1])


def _pack_small(d):
    flat = jnp.concatenate([d[n].reshape(-1) for n in SMALL])
    pad = (-flat.shape[0]) % (8 * LANES)
    return jnp.pad(flat, (0, pad)).reshape(-1, LANES)


def _unpack_small(packed, like):
    flat = packed.reshape(-1)
    out, off = {}, 0
    for n in SMALL:
        sz = int(np.prod(like[n].shape))
        out[n] = flat[off:off + sz].reshape(like[n].shape)
        off += sz
    return out


def kernel(x, positions, ev_w_in, ev_w_out, od_w_in, od_conv_w, od_a_log, od_dt_bias, od_norm_w, od_w_out, ffn_w_up, ffn_conv_w, ffn_conv_b, ffn_w_down, ln1_g, ln1_b, ln2_g, ln2_b, loss_target, m_ev_w_in, m_ev_w_out, m_od_w_in, m_od_conv_w, m_od_a_log, m_od_dt_bias, m_od_norm_w, m_od_w_out, m_ffn_w_up, m_ffn_conv_w, m_ffn_conv_b, m_ffn_w_down, m_ln1_g, m_ln1_b, m_ln2_g, m_ln2_b, v_ev_w_in, v_ev_w_out, v_od_w_in, v_od_conv_w, v_od_a_log, v_od_dt_bias, v_od_norm_w, v_od_w_out, v_ffn_w_up, v_ffn_conv_w, v_ffn_conv_b, v_ffn_w_down, v_ln1_g, v_ln1_b, v_ln2_g, v_ln2_b):
    w = dict(ev_w_in=ev_w_in, ev_w_out=ev_w_out, od_w_in=od_w_in, od_conv_w=od_conv_w, od_a_log=od_a_log,
             od_dt_bias=od_dt_bias, od_norm_w=od_norm_w, od_w_out=od_w_out, ffn_w_up=ffn_w_up, ffn_conv_w=ffn_conv_w,
             ffn_conv_b=ffn_conv_b, ffn_w_down=ffn_w_down, ln1_g=ln1_g, ln1_b=ln1_b, ln2_g=ln2_g, ln2_b=ln2_b)
    mom = dict(ev_w_in=m_ev_w_in, ev_w_out=m_ev_w_out, od_w_in=m_od_w_in, od_conv_w=m_od_conv_w, od_a_log=m_od_a_log,
               od_dt_bias=m_od_dt_bias, od_norm_w=m_od_norm_w, od_w_out=m_od_w_out, ffn_w_up=m_ffn_w_up,
               ffn_conv_w=m_ffn_conv_w, ffn_conv_b=m_ffn_conv_b, ffn_w_down=m_ffn_w_down, ln1_g=m_ln1_g,
               ln1_b=m_ln1_b, ln2_g=m_ln2_g, ln2_b=m_ln2_b)
    var = dict(ev_w_in=v_ev_w_in, ev_w_out=v_ev_w_out, od_w_in=v_od_w_in, od_conv_w=v_od_conv_w, od_a_log=v_od_a_log,
               od_dt_bias=v_od_dt_bias, od_norm_w=v_od_norm_w, od_w_out=v_od_w_out, ffn_w_up=v_ffn_w_up,
               ffn_conv_w=v_ffn_conv_w, ffn_conv_b=v_ffn_conv_b, ffn_w_down=v_ffn_w_down, ln1_g=v_ln1_g,
               ln1_b=v_ln1_b, ln2_g=v_ln2_g, ln2_b=v_ln2_b)

    myid = _my_index()
    small = {n: w[n] for n in SMALL}
    groups = [(layer, part) for layer in range(DEPTH) for part in ("mixer", "ffn")]

    def group_items(gi):
        layer, part = groups[gi]
        its = _layer_items(layer)
        return its[:-3] if part == "mixer" else its[-3:]

    level1, level2 = {}, {}

    def spread_job(gi):
        return ("spread", [_to_send(kind, n, w[n], j) for (_, n, j, kind) in group_items(gi)], None)

    def relay(gi, after, name):
        (srcs, lands), = _exchange_wait([level1.pop(gi)], after, name=name + "_wait")
        more = [spread_job(gi + 1)] if gi + 1 < len(groups) else []
        started, token = _exchange_start([("relay", [], lands)] + more, None, name=name + "_start")
        level2[gi] = (started[0], srcs)
        if more:
            level1[gi + 1] = started[1]
        return token

    def get_w(layer, part, after):
        gi = groups.index((layer, part))
        started, srcs = level2.pop(gi)
        (_, lands), = _exchange_wait([started], after, name=f"gather{gi}_wait")
        lands = [lax.dynamic_update_index_in_dim(l, s, myid, 0) for l, s in zip(lands, srcs)]
        return {key: _from_gather(kind, n, l, f"L{layer}_{key}")
                for (key, n, _, kind), l in zip(group_items(gi), lands)}, None

    def mid(layer, part, after):
        gi = groups.index((layer, part)) + 1
        return relay(gi, after, f"gather{gi}_relay") if gi < len(groups) else None

    landed = {}
    pending = []

    def scatter_finish(after):
        started, gi = pending.pop()
        (srcs, lands), = _exchange_wait([started], after, name=f"scatter{gi}_wait")
        for (key, _, _, _), l, s in zip(group_items(gi), lands, srcs):
            own = lax.dynamic_index_in_dim(s, myid, 0, keepdims=False)
            landed[(groups[gi][0], key)] = lax.dynamic_update_index_in_dim(l, own, myid, 0)

    def put_g(layer, part, g):
        gi = groups.index((layer, part))
        srcs = [_by_owner(kind, n, g[key], f"L{layer}_{key}") for (key, n, _, kind) in group_items(gi)]
        (started,), token = _exchange_start([("scatter", srcs, None)], None, name=f"scatter{gi}_start")
        if pending:
            scatter_finish(token)
        pending.append((started, gi))
        return token

    (level1[0],), token = _exchange_start([spread_job(0)], None, name="gather0_spread_start")
    tables = _tables(positions[0] + token[0, 0].astype(jnp.int32))
    relay(0, tables[-1], "gather0_relay")
    loss, grad_x, gS = _local_step(x[0], tables, loss_target[0], get_w, mid, put_g, small)
    loss = lax.psum(loss, ("x", "y", "c"))

    outs_g, outs_d, outs_m, outs_v = {}, {}, {}, {}
    where = {n: [None] * w[n].shape[0] for n in SHARDED}
    for layer in range(DEPTH):
        for (key, n, j, kind) in _layer_items(layer):
            where[n][j] = (layer, key, kind)

    def update(n):
        g = None
        for j, (layer, key, _) in enumerate(where[n]):
            g = _sum8(landed[(layer, key)], g, j, len(where[n]), name=f"L{layer}_{key}_sum")
        if n == "od_w_in":
            g = g[:, :OD_SHARD]
        if where[n][0][2] == "colT":
            tr = lambda a: jnp.swapaxes(a, 1, 2)
            d, nm, nv = _adamw_sharded(tr(w[n]), tr(mom[n]), tr(var[n]), g, name=f"adamw_{n}")
            outs_g[n], outs_d[n], outs_m[n], outs_v[n] = tr(g), tr(d), tr(nm), tr(nv)
        else:
            outs_g[n] = g
            outs_d[n], outs_m[n], outs_v[n] = _adamw_sharded(w[n], mom[n], var[n], g, name=f"adamw_{n}")

    (small_job,), _ = _exchange_start([("all", [_pack_small(gS)], None)], None, name="small_grads_start")
    last = {n for (_, n, _, _) in group_items(pending[0][1])}
    for n in SHARDED:
        if n not in last:
            update(n)
    scatter_finish(outs_d[[n for n in SHARDED if n not in last][-1]])
    for n in SHARDED:
        if n in last:
            update(n)
    ((mine,), (gall,)), = _exchange_wait([small_job], outs_d[[n for n in SHARDED if n in last][-1]], name="small_grads_wait")
    gall = lax.dynamic_update_index_in_dim(gall, mine, myid, 0)
    g, d, nm, nv = _adamw_small(_pack_small({n: w[n] for n in SMALL}), _pack_small({n: mom[n] for n in SMALL}),
                                _pack_small({n: var[n] for n in SMALL}), gall, name="adamw_small")
    for dst, packed in ((outs_g, g), (outs_d, d), (outs_m, nm), (outs_v, nv)):
        dst.update(_unpack_small(packed, {n: w[n] for n in SMALL}))

    return (loss, grad_x[None], *[outs_g[n] for n in ALL_W], *[outs_d[n] for n in ALL_W],
            *[outs_m[n] for n in ALL_W], *[outs_v[n] for n in ALL_W])
```

```python
import functools
import math

import numpy as np
import jax
import jax.numpy as jnp
from jax import lax
from jax.experimental import pallas as pl
from jax.experimental.pallas import tpu as pltpu

F32 = jnp.float32
BF16 = jnp.bfloat16
MESH = pl.DeviceIdType.MESH

D_MODEL = 1024
SEQ = 2048
DEPTH = 4
N_DEV = 8
RET_HEADS, RET_DK, RET_DV = 4, 128, 256
RET_THETA = 10000.0
DIL_HEADS, DIL_HD = 8, 64
DIL_PAIRS = ((128, 1), (512, 4), (2048, 16))
ROPE_THETA = 500000.0
ROPE_DIMS = DIL_HD // 4
GDN_HEADS, GDN_DK, GDN_DV, GDN_CHUNK, GDN_CONV = 8, 128, 128, 64, 4
D_FF = 2816
FFN_CONV = 3
ALPHA = (2.0 * DEPTH) ** 0.25
EPS = 1e-5
RET_QK_W = RET_HEADS * RET_DK
RET_V_W = RET_HEADS * RET_DV
DIL_W = DIL_HEADS * DIL_HD
EV_IN = 2 * RET_QK_W + 2 * RET_V_W + 3 * DIL_W
EV_MIX = RET_V_W + DIL_W
GDN_W = GDN_HEADS * GDN_DK
OD_IN = 4 * GDN_W + 2 * GDN_HEADS
OD_IN_PAD = 4 * GDN_W + 128
ADAM_LR, ADAM_B1, ADAM_B2, ADAM_EPS, ADAM_WD, ADAM_STEP = 0.001, 0.9, 0.999, 1e-08, 0.01, 10

LANES = 128
VMEM_LIMIT = 56 * 1024 * 1024
ATT_BLK = 256
NEG = -1e30


def _cp(**kw):
    return pltpu.CompilerParams(vmem_limit_bytes=VMEM_LIMIT, **kw)


def _tile(n, cap):
    if n <= cap:
        return n
    best = None
    for t in range(LANES, cap + 1, LANES):
        if n % t == 0:
            best = t
    assert best is not None, (n, cap)
    return best


def _mm(a, b, *, ta=False, tb=False, name, out_dtype=F32, dep=None, tm=None, tn=None):
    m = a.shape[1] if ta else a.shape[0]
    k = a.shape[0] if ta else a.shape[1]
    n = b.shape[0] if tb else b.shape[1]
    assert (b.shape[1] if tb else b.shape[0]) == k
    assert a.dtype == BF16 and b.dtype == BF16
    if tn is None:
        tn = n if n <= 1024 else _tile(n, 512)
    if tm is None:
        tm = m if (tn < n and k <= 1024 and m <= 2048) else _tile(m, 512)
    dims = (((0 if ta else 1,), (1 if tb else 0,)), ((), ()))

    def body(a_ref, b_ref, *rest):
        o_ref = rest[-1]
        o_ref[...] = lax.dot_general(a_ref[...], b_ref[...], dims,
                                     preferred_element_type=F32).astype(o_ref.dtype)

    a_spec = pl.BlockSpec((k, tm), lambda i, j: (0, i)) if ta else pl.BlockSpec((tm, k), lambda i, j: (i, 0))
    b_spec = pl.BlockSpec((tn, k), lambda i, j: (j, 0)) if tb else pl.BlockSpec((k, tn), lambda i, j: (0, j))
    extra = [] if dep is None else [dep]
    return pl.pallas_call(
        body, grid=(m // tm, n // tn), in_specs=[a_spec, b_spec] + [pl.BlockSpec(memory_space=pl.ANY)] * len(extra),
        out_specs=pl.BlockSpec((tm, tn), lambda i, j: (i, j)),
        out_shape=jax.ShapeDtypeStruct((m, n), out_dtype), name=name, compiler_params=_cp())(a, b, *extra)


LN_ROWS = 256


def _ln_bwd(z, g, dya, dyb, *, name):
    t, d = z.shape
    two = dyb is not None

    def body(*refs):
        if two:
            z_ref, g_ref, dya_ref, dyb_ref, dz_ref, dzb_ref, dg_ref, db_ref = refs
            dy = dya_ref[...] + ALPHA * dyb_ref[...]
        else:
            z_ref, g_ref, dya_ref, dz_ref, dzb_ref, dg_ref, db_ref = refs
            dy = dya_ref[...]
        zz = z_ref[...]
        mu = jnp.mean(zz, -1, keepdims=True)
        zc = zz - mu
        var = jnp.mean(zc * zc, -1, keepdims=True)
        r = lax.rsqrt(var + EPS)
        xh = zc * r
        dxh = dy * g_ref[...]
        dz = r * (dxh - jnp.mean(dxh, -1, keepdims=True) - xh * jnp.mean(dxh * xh, -1, keepdims=True))
        dz_ref[...] = dz
        dzb_ref[...] = dz.astype(BF16)

        @pl.when(pl.program_id(0) == 0)
        def _():
            dg_ref[...] = jnp.zeros_like(dg_ref)
            db_ref[...] = jnp.zeros_like(db_ref)

        dg_ref[...] += jnp.sum(dy * xh, 0, keepdims=True)
        db_ref[...] += jnp.sum(dy, 0, keepdims=True)

    row = pl.BlockSpec((LN_ROWS, d), lambda i: (i, 0))
    vec = pl.BlockSpec((1, d), lambda i: (0, 0))
    ins = [z, g, dya] + ([dyb] if two else [])
    return pl.pallas_call(
        body, grid=(t // LN_ROWS,), in_specs=[row, vec, row] + ([row] if two else []),
        out_specs=[row, row, vec, vec],
        out_shape=[jax.ShapeDtypeStruct((t, d), F32), jax.ShapeDtypeStruct((t, d), BF16),
                   jax.ShapeDtypeStruct((1, d), F32), jax.ShapeDtypeStruct((1, d), F32)],
        name=name, compiler_params=_cp())(*ins)


def _ln_rows(k):
    return 256 if k > 4096 else 512


def _mm_ln_fwd(parts, w, x, g, b, *, name, dep=None):
    t = parts[0].shape[0]
    offs, k = _part_offsets(parts)
    d = w.shape[1]
    tm = _ln_rows(k)
    npart = len(parts)

    def body(*refs):
        a_refs, w_refs = refs[:npart], refs[npart:2 * npart]
        x_ref, g_ref, b_ref = refs[2 * npart:2 * npart + 3]
        z_ref, y_ref, yb_ref = refs[-3:]
        z = ALPHA * x_ref[...]
        for a_ref, w_ref in zip(a_refs, w_refs):
            z = z + _nn(a_ref[...], w_ref[...])
        mu = jnp.mean(z, -1, keepdims=True)
        zc = z - mu
        var = jnp.mean(zc * zc, -1, keepdims=True)
        y = zc * lax.rsqrt(var + EPS) * g_ref[...] + b_ref[...]
        z_ref[...] = z
        y_ref[...] = y
        yb_ref[...] = y.astype(BF16)

    row = pl.BlockSpec((tm, d), lambda i: (i, 0))
    vec = pl.BlockSpec((1, d), lambda i: (0, 0))
    extra = [] if dep is None else [dep]
    a_specs = [pl.BlockSpec((tm, p.shape[1]), lambda i: (i, 0)) for p in parts]
    w_specs = [pl.BlockSpec((p.shape[1], d), functools.partial(lambda i, blk: (blk, 0), blk=o // p.shape[1]))
               for p, o in zip(parts, offs)]
    return pl.pallas_call(
        body, grid=(t // tm,),
        in_specs=a_specs + w_specs + [row, vec, vec] + [pl.BlockSpec(memory_space=pl.ANY)] * len(extra),
        out_specs=[row, row, row],
        out_shape=[jax.ShapeDtypeStruct((t, d), F32), jax.ShapeDtypeStruct((t, d), F32), jax.ShapeDtypeStruct((t, d), BF16)],
        name=name, compiler_params=_cp())(*parts, *([w] * npart), x, g, b, *extra)


def _part_offsets(parts):
    offs, o = [], 0
    for p in parts:
        assert o % p.shape[1] == 0
        offs.append(o)
        o += p.shape[1]
    return offs, o


def _mm_ln_bwd(parts, w, z, g, dyb, *, name, dep=None):
    t = parts[0].shape[0]
    offs, k = _part_offsets(parts)
    d = w.shape[1]
    tm = _ln_rows(k)
    npart = len(parts)

    def body(*refs):
        a_refs, w_refs = refs[:npart], refs[npart:2 * npart]
        z_ref, g_ref, dyb_ref = refs[2 * npart:2 * npart + 3]
        dz_ref, dzb_ref, dg_ref, db_ref = refs[-4:]
        dy = ALPHA * dyb_ref[...]
        for a_ref, w_ref in zip(a_refs, w_refs):
            dy = dy + _nn(a_ref[...], w_ref[...])
        zz = z_ref[...]
        mu = jnp.mean(zz, -1, keepdims=True)
        zc = zz - mu
        var = jnp.mean(zc * zc, -1, keepdims=True)
        r = lax.rsqrt(var + EPS)
        xh = zc * r
        dxh = dy * g_ref[...]
        dz = r * (dxh - jnp.mean(dxh, -1, keepdims=True) - xh * jnp.mean(dxh * xh, -1, keepdims=True))
        dz_ref[...] = dz
        dzb_ref[...] = dz.astype(BF16)

        @pl.when(pl.program_id(0) == 0)
        def _():
            dg_ref[...] = jnp.zeros_like(dg_ref)
            db_ref[...] = jnp.zeros_like(db_ref)

        dg_ref[...] += jnp.sum(dy * xh, 0, keepdims=True)
        db_ref[...] += jnp.sum(dy, 0, keepdims=True)

    row = pl.BlockSpec((tm, d), lambda i: (i, 0))
    vec = pl.BlockSpec((1, d), lambda i: (0, 0))
    extra = [] if dep is None else [dep]
    a_specs = [pl.BlockSpec((tm, p.shape[1]), lambda i: (i, 0)) for p in parts]
    w_specs = [pl.BlockSpec((p.shape[1], d), functools.partial(lambda i, blk: (blk, 0), blk=o // p.shape[1]))
               for p, o in zip(parts, offs)]
    return pl.pallas_call(
        body, grid=(t // tm,),
        in_specs=a_specs + w_specs + [row, vec, row] + [pl.BlockSpec(memory_space=pl.ANY)] * len(extra),
        out_specs=[row, row, vec, vec],
        out_shape=[jax.ShapeDtypeStruct((t, d), F32), jax.ShapeDtypeStruct((t, d), BF16),
                   jax.ShapeDtypeStruct((1, d), F32), jax.ShapeDtypeStruct((1, d), F32)],
        name=name, compiler_params=_cp())(*parts, *([w] * npart), z, g, dyb, *extra)


def _mm_tn_parts(parts, b, *, name):
    t, n = b.shape
    offs, m = _part_offsets(parts)
    tm = min(_tile(p.shape[1], 1408 if p.shape[1] > 2048 else 512) for p in parts)
    assert all(p.shape[1] % tm == 0 for p in parts)
    first = [o // tm for o in offs]
    count = [p.shape[1] // tm for p in parts]
    npart = len(parts)

    def body(*refs):
        a_refs, b_ref, o_ref = refs[:npart], refs[npart], refs[npart + 1]
        i = pl.program_id(0)
        for a_ref, f, c in zip(a_refs, first, count):
            @pl.when((i >= f) & (i < f + c))
            def _(a_ref=a_ref):
                o_ref[...] = _tn(a_ref[...], b_ref[...]).astype(BF16)

    a_specs = [pl.BlockSpec((t, tm), functools.partial(lambda i, f, c: (0, jnp.clip(i - f, 0, c - 1)), f=f, c=c))
               for f, c in zip(first, count)]
    return pl.pallas_call(
        body, grid=(m // tm,), in_specs=a_specs + [pl.BlockSpec((t, n), lambda i: (0, 0))],
        out_specs=pl.BlockSpec((tm, n), lambda i: (i, 0)), out_shape=jax.ShapeDtypeStruct((m, n), BF16),
        name=name, compiler_params=_cp())(*parts, b)


TN_BUFS = 3
TN_PIPED_TILE = 256


def _mm_tn_parts_piped(parts, b, *, name):
    t, n = b.shape
    offs, m = _part_offsets(parts)
    tm = TN_PIPED_TILE
    assert all(p.shape[1] % tm == 0 for p in parts)
    first = [o // tm for o in offs]
    count = [p.shape[1] // tm for p in parts]
    ntile = m // tm
    npart = len(parts)

    def body(*refs):
        a_refs, b_ref, o_ref, buf, sems = refs[:npart], refs[npart], refs[npart + 1], refs[npart + 2], refs[npart + 3]
        i = pl.program_id(0)

        def tile_copy(tile, start):
            slot = tile % TN_BUFS
            for a_ref, f, c in zip(a_refs, first, count):
                @pl.when((tile >= f) & (tile < f + c))
                def _(a_ref=a_ref, f=f):
                    col = pl.multiple_of((tile - f) * tm, tm)
                    cp = pltpu.make_async_copy(a_ref.at[:, pl.ds(col, tm)], buf.at[slot], sems.at[slot])
                    if start:
                        cp.start()
                    else:
                        cp.wait()

        @pl.when(i == 0)
        def _():
            for ahead in range(TN_BUFS - 1):
                tile_copy(i + ahead, True)

        @pl.when(i + TN_BUFS - 1 < ntile)
        def _():
            tile_copy(i + TN_BUFS - 1, True)

        tile_copy(i, False)
        o_ref[...] = _tn(buf[i % TN_BUFS], b_ref[...]).astype(BF16)

    return pl.pallas_call(
        body, grid=(ntile,),
        in_specs=[pl.BlockSpec(memory_space=pl.ANY)] * npart + [pl.BlockSpec((t, n), lambda i: (0, 0))],
        out_specs=pl.BlockSpec((tm, n), lambda i: (i, 0)), out_shape=jax.ShapeDtypeStruct((m, n), BF16),
        scratch_shapes=[pltpu.VMEM((TN_BUFS, t, tm), BF16), pltpu.SemaphoreType.DMA((TN_BUFS,))],
        name=name, compiler_params=_cp())(*parts, b)


def _axpy(a, b, *, name):
    t, d = a.shape

    def body(a_ref, b_ref, o_ref):
        o_ref[...] = a_ref[...] + ALPHA * b_ref[...]

    row = pl.BlockSpec((LN_ROWS, d), lambda i: (i, 0))
    return pl.pallas_call(body, grid=(t // LN_ROWS,), in_specs=[row, row], out_specs=row,
                          out_shape=jax.ShapeDtypeStruct((t, d), F32), name=name, compiler_params=_cp())(a, b)


def _loss_head(y, target, *, name):
    t, d = y.shape

    def body(y_ref, t_ref, dy_ref, l_ref):
        e = y_ref[...] - t_ref[...]
        dy_ref[...] = e * (1.0 / d)

        @pl.when(pl.program_id(0) == 0)
        def _():
            l_ref[...] = jnp.zeros_like(l_ref)

        l_ref[...] += jnp.zeros_like(l_ref) + 0.5 * jnp.sum(jnp.mean(e * e, -1, keepdims=True), 0, keepdims=True)

    row = pl.BlockSpec((LN_ROWS, d), lambda i: (i, 0))
    return pl.pallas_call(
        body, grid=(t // LN_ROWS,), in_specs=[row, row],
        out_specs=[row, pl.BlockSpec((1, LANES), lambda i: (0, 0))],
        out_shape=[jax.ShapeDtypeStruct((t, d), F32), jax.ShapeDtypeStruct((1, LANES), F32)],
        name=name, compiler_params=_cp())(y, target)


def _sig(x):
    return 1.0 / (1.0 + jnp.exp(-x))


def _silu(x):
    return x * _sig(x)


def _dsilu(x):
    s = _sig(x)
    return s * (1.0 + x * (1.0 - s))


def _shift_down(u, k, row):
    if k == 0:
        return u
    return jnp.where(row >= k, pltpu.roll(u, k, 0), 0.0)


def _shift_up(u, k, row):
    if k == 0:
        return u
    t = u.shape[0]
    return jnp.where(row < t - k, pltpu.roll(u, t - k, 0), 0.0)


def _dwconv(u, w_ref, row):
    kk = w_ref.shape[0]
    acc = None
    for j in range(kk):
        term = w_ref[j:j + 1, :] * _shift_down(u, kk - 1 - j, row)
        acc = term if acc is None else acc + term
    return acc


def _dwconv_bwd(u, w_ref, dc, row, dw_ref):
    kk = w_ref.shape[0]
    du = None
    for j in range(kk):
        term = w_ref[j:j + 1, :] * _shift_up(dc, kk - 1 - j, row)
        du = term if du is None else du + term
        dw_ref[j:j + 1, :] = jnp.sum(dc * _shift_down(u, kk - 1 - j, row), 0, keepdims=True)
    return du


CONV_ROWS = 1024


def _rows(b):
    return pl.ds(pl.multiple_of(b * CONV_ROWS, CONV_ROWS), CONV_ROWS)


def _shifted_down(ref, b, k, row):
    cur = ref[_rows(b), :]
    if k == 0:
        return cur
    prev = jnp.where(b > 0, ref[_rows(jnp.maximum(b - 1, 0)), :], 0.0)
    return jnp.where(row >= k, pltpu.roll(cur, k, 0), pltpu.roll(prev, k, 0))


def _shifted_up(ref, b, k, row, nblk):
    cur = ref[_rows(b), :]
    if k == 0:
        return cur
    nxt = jnp.where(b < nblk - 1, ref[_rows(jnp.minimum(b + 1, nblk - 1)), :], 0.0)
    return jnp.where(row < CONV_ROWS - k, pltpu.roll(cur, CONV_ROWS - k, 0), pltpu.roll(nxt, CONV_ROWS - k, 0))


def _dwconv_blk(u_ref, w_ref, b, row):
    kk = w_ref.shape[0]
    views = [_shifted_down(u_ref, b, kk - 1 - j, row) for j in range(kk)]
    acc = None
    for j in range(kk):
        term = w_ref[j:j + 1, :] * views[j]
        acc = term if acc is None else acc + term
    return acc, views


def _dwconv_du_blk(dc_ref, w_ref, b, row, nblk):
    kk = w_ref.shape[0]
    du = None
    for j in range(kk):
        term = w_ref[j:j + 1, :] * _shifted_up(dc_ref, b, kk - 1 - j, row, nblk)
        du = term if du is None else du + term
    return du


FFN_TC = 256


def _ffn_up_mid(x, up_t, cw, cb, *, name, dep=None):
    t, d = x.shape
    nb = D_FF // FFN_TC

    def body(x_ref, ugt_ref, uvt_ref, wg_ref, wv_ref, bg_ref, bv_ref, *rest):
        ug_ref, uv_ref, a_ref = rest[-3:]
        xx = x_ref[...]
        row = lax.broadcasted_iota(jnp.int32, (t, FFN_TC), 0)
        ug = _nt(xx, ugt_ref[...])
        ug_ref[...] = ug
        uv = _nt(xx, uvt_ref[...])
        uv_ref[...] = uv
        cg = _dwconv(ug, wg_ref, row) + bg_ref[...]
        cv = _dwconv(uv, wv_ref, row) + bv_ref[...]
        a_ref[...] = (_silu(cg) * cv).astype(BF16)

    col = pl.BlockSpec((t, FFN_TC), lambda j: (0, j))
    wt = lambda off: pl.BlockSpec((FFN_TC, d), lambda j: (j + off, 0))
    wsp = lambda off: pl.BlockSpec((FFN_CONV, FFN_TC), lambda j: (0, j + off))
    bsp = lambda off: pl.BlockSpec((1, FFN_TC), lambda j: (0, j + off))
    extra = [] if dep is None else [dep]
    return pl.pallas_call(
        body, grid=(nb,),
        in_specs=[pl.BlockSpec((t, d), lambda j: (0, 0)), wt(0), wt(nb), wsp(0), wsp(nb), bsp(0), bsp(nb)]
        + [pl.BlockSpec(memory_space=pl.ANY)] * len(extra),
        out_specs=[col, col, col],
        out_shape=[jax.ShapeDtypeStruct((t, D_FF), F32), jax.ShapeDtypeStruct((t, D_FF), F32),
                   jax.ShapeDtypeStruct((t, D_FF), BF16)],
        name=name, compiler_params=_cp())(x, up_t, up_t, cw, cw, cb, cb, *extra)


def _ffn_mid_bwd(ug, uv, cw, cb, dz, down, *, name):
    t, d = dz.shape
    nb = D_FF // FFN_TC

    nblk = t // CONV_ROWS

    def body(ug_ref, uv_ref, wg_ref, wv_ref, bg_ref, bv_ref, dz_ref, dn_ref,
             dug_ref, duv_ref, dwg_ref, dwv_ref, dbg_ref, dbv_ref, da_ref, dcg_s, dcv_s):
        da_ref[...] = _nt(dz_ref[...], dn_ref[...])
        row = lax.broadcasted_iota(jnp.int32, (CONV_ROWS, FFN_TC), 0)
        zero = jnp.zeros((1, FFN_TC), F32)

        def first(b, acc):
            cg, ugs = _dwconv_blk(ug_ref, wg_ref, b, row)
            cv, uvs = _dwconv_blk(uv_ref, wv_ref, b, row)
            cg = cg + bg_ref[...]
            cv = cv + bv_ref[...]
            da_ = da_ref[_rows(b), :]
            dcv = da_ * _silu(cg)
            dcg = da_ * cv * _dsilu(cg)
            dcg_s[_rows(b), :] = dcg
            dcv_s[_rows(b), :] = dcv
            red = [jnp.sum(dcg * s, 0, keepdims=True) for s in ugs] + [jnp.sum(dcg, 0, keepdims=True)]
            red += [jnp.sum(dcv * s, 0, keepdims=True) for s in uvs] + [jnp.sum(dcv, 0, keepdims=True)]
            return tuple(a + r for a, r in zip(acc, red))

        acc = lax.fori_loop(0, nblk, first, (zero,) * (2 * FFN_CONV + 2))
        for j in range(FFN_CONV):
            dwg_ref[j:j + 1, :] = acc[j]
            dwv_ref[j:j + 1, :] = acc[FFN_CONV + 1 + j]
        dbg_ref[...] = acc[FFN_CONV]
        dbv_ref[...] = acc[2 * FFN_CONV + 1]

        def second(b, carry):
            dug_ref[_rows(b), :] = _dwconv_du_blk(dcg_s, wg_ref, b, row, nblk).astype(BF16)
            duv_ref[_rows(b), :] = _dwconv_du_blk(dcv_s, wv_ref, b, row, nblk).astype(BF16)
            return carry

        lax.fori_loop(0, nblk, second, 0)

    col = pl.BlockSpec((t, FFN_TC), lambda j: (0, j))
    wsp = lambda off: pl.BlockSpec((FFN_CONV, FFN_TC), lambda j: (0, j + off))
    bsp = lambda off: pl.BlockSpec((1, FFN_TC), lambda j: (0, j + off))
    outs = pl.pallas_call(
        body, grid=(nb,),
        in_specs=[col, col, wsp(0), wsp(nb), bsp(0), bsp(nb), pl.BlockSpec((t, d), lambda j: (0, 0)),
                  pl.BlockSpec((FFN_TC, d), lambda j: (j, 0))],
        out_specs=[col, col, wsp(0), wsp(0), bsp(0), bsp(0)],
        out_shape=[jax.ShapeDtypeStruct((t, D_FF), BF16), jax.ShapeDtypeStruct((t, D_FF), BF16),
                   jax.ShapeDtypeStruct((FFN_CONV, D_FF), F32), jax.ShapeDtypeStruct((FFN_CONV, D_FF), F32),
                   jax.ShapeDtypeStruct((1, D_FF), F32), jax.ShapeDtypeStruct((1, D_FF), F32)],
        scratch_shapes=[pltpu.VMEM((t, FFN_TC), F32), pltpu.VMEM((t, FFN_TC), F32), pltpu.VMEM((t, FFN_TC), F32)],
        name=name, compiler_params=_cp())(ug, uv, cw, cw, cb, cb, dz, down)
    dug, duv, dwg, dwv, dbg, dbv = outs
    return [dug, duv], jnp.concatenate([dwg, dwv], 1), jnp.concatenate([dbg, dbv], 1)


def _rot_a(x, c2, s2):
    return x * c2 + pltpu.roll(x, RET_DK // 2, 1) * s2


def _rot_a_t(dy, c2, s2):
    return dy * c2 + pltpu.roll(dy * s2, RET_DK // 2, 1)


RET_BWD_BLK = 512


def _decay_tile(lg, blk_diff, blk=ATT_BLK):
    r = lax.broadcasted_iota(jnp.int32, (blk, blk), 0)
    c = lax.broadcasted_iota(jnp.int32, (blk, blk), 1)
    rel = r - c + blk_diff * blk
    return jnp.where(rel >= 0, jnp.exp(jnp.maximum(rel, 0).astype(F32) * lg), 0.0)


def _nt(a, b):
    return lax.dot_general(a, b, (((1,), (1,)), ((), ())), preferred_element_type=F32)


def _nn(a, b):
    return lax.dot_general(a, b, (((1,), (0,)), ((), ())), preferred_element_type=F32)


def _tn(a, b):
    return lax.dot_general(a, b, (((0,), (0,)), ((), ())), preferred_element_type=F32)


def _ret_specs(t):
    q = pl.BlockSpec((t, RET_DK), lambda h: (0, h))
    k = pl.BlockSpec((t, RET_DK), lambda h: (0, RET_HEADS + h))
    v = pl.BlockSpec((t, RET_DV), lambda h: (0, RET_HEADS + h))
    g = pl.BlockSpec((t, RET_DV), lambda h: (0, 2 * RET_HEADS + h))
    tab = pl.BlockSpec((t, RET_DK), lambda h: (0, 0))
    lg = pl.BlockSpec((1, 1, LANES), lambda h: (h, 0, 0))
    return q, k, v, g, tab, lg


def _ret_fwd(h, c2, s2, lgt, *, name):
    t = h.shape[0]
    nblk = t // ATT_BLK
    scale = RET_DK ** -0.5

    def body(q_ref, k_ref, v_ref, g_ref, c_ref, s_ref, lg_ref, o_ref, ya_ref, qs, ks, vs):
        c2_, s2_ = c_ref[...], s_ref[...]
        qs[...] = _rot_a(q_ref[...], c2_, s2_).astype(BF16)
        ks[...] = (_rot_a(k_ref[...], c2_, s2_) * scale).astype(BF16)
        vs[...] = v_ref[...].astype(BF16)
        lg = lg_ref[0, :, 0:1]
        for i in range(nblk):
            qi = qs[pl.ds(i * ATT_BLK, ATT_BLK), :]
            acc = jnp.zeros((ATT_BLK, RET_DV), F32)
            for j in range(i + 1):
                sl = pl.ds(j * ATT_BLK, ATT_BLK)
                s = _nt(qi, ks[sl, :]) * _decay_tile(lg, i - j)
                acc = acc + _nn(s.astype(BF16), vs[sl, :])
            rows = pl.ds(i * ATT_BLK, ATT_BLK)
            o_ref[rows, :] = acc
            r = lax.rsqrt(jnp.mean(acc * acc, -1, keepdims=True) + EPS)
            ya_ref[rows, :] = (acc * r * _silu(g_ref[rows, :])).astype(BF16)

    q, k, v, g, tab, lg = _ret_specs(t)
    out = pl.BlockSpec((t, RET_DV), lambda hh: (0, hh))
    return pl.pallas_call(
        body, grid=(RET_HEADS,), in_specs=[q, k, v, g, tab, tab, lg], out_specs=[out, out],
        out_shape=[jax.ShapeDtypeStruct((t, RET_V_W), F32), jax.ShapeDtypeStruct((t, RET_V_W), BF16)],
        scratch_shapes=[pltpu.VMEM((t, RET_DK), BF16), pltpu.VMEM((t, RET_DK), BF16), pltpu.VMEM((t, RET_DV), BF16)],
        name=name, compiler_params=_cp())(h, h, h, h, c2, s2, lgt)


def _ret_bwd(h, c2, s2, lgt, o, dy, *, name):
    t = h.shape[0]
    blk = RET_BWD_BLK
    nblk = t // blk
    scale = RET_DK ** -0.5

    def body(q_ref, k_ref, v_ref, g_ref, c_ref, s_ref, lg_ref, o_ref, dy_ref,
             dq_ref, dk_ref, dv_ref, dg_ref, qs, ks, vs, dos, dka, dva):
        c2_, s2_ = c_ref[...], s_ref[...]
        qs[...] = _rot_a(q_ref[...], c2_, s2_).astype(BF16)
        ks[...] = (_rot_a(k_ref[...], c2_, s2_) * scale).astype(BF16)
        vs[...] = v_ref[...].astype(BF16)
        lg = lg_ref[0, :, 0:1]
        oo = o_ref[...]
        gg = g_ref[...]
        dya = dy_ref[...]
        r = lax.rsqrt(jnp.mean(oo * oo, -1, keepdims=True) + EPS)
        rn = oo * r
        dg_ref[...] = (dya * rn * _dsilu(gg)).astype(BF16)
        drn = dya * _silu(gg)
        dos[...] = (r * (drn - rn * jnp.mean(drn * rn, -1, keepdims=True))).astype(BF16)
        dka[...] = jnp.zeros_like(dka)
        dva[...] = jnp.zeros_like(dva)
        for i in range(nblk):
            rows = pl.ds(i * blk, blk)
            qi = qs[rows, :]
            doi = dos[rows, :]
            dqa = jnp.zeros((blk, RET_DK), F32)
            for j in range(i + 1):
                sl = pl.ds(j * blk, blk)
                dt_ = _decay_tile(lg, i - j, blk)
                kj = ks[sl, :]
                s = (_nt(qi, kj) * dt_).astype(BF16)
                ds = (_nt(doi, vs[sl, :]) * dt_).astype(BF16)
                dqa = dqa + _nn(ds, kj)
                dka[sl, :] += _tn(ds, qi)
                dva[sl, :] += _tn(s, doi)
            dq_ref[rows, :] = _rot_a_t(dqa, c_ref[rows, :], s_ref[rows, :]).astype(BF16)
        dk_ref[...] = (_rot_a_t(dka[...], c2_, s2_) * scale).astype(BF16)
        dv_ref[...] = dva[...].astype(BF16)

    q, k, v, g, tab, lg = _ret_specs(t)
    blk_v = pl.BlockSpec((t, RET_DV), lambda hh: (0, hh))
    blk_k = pl.BlockSpec((t, RET_DK), lambda hh: (0, hh))
    return pl.pallas_call(
        body, grid=(RET_HEADS,), in_specs=[q, k, v, g, tab, tab, lg, blk_v, blk_v],
        out_specs=[blk_k, blk_k, blk_v, blk_v],
        out_shape=[jax.ShapeDtypeStruct((t, RET_QK_W), BF16), jax.ShapeDtypeStruct((t, RET_QK_W), BF16),
                   jax.ShapeDtypeStruct((t, RET_V_W), BF16), jax.ShapeDtypeStruct((t, RET_V_W), BF16)],
        scratch_shapes=[pltpu.VMEM((t, RET_DK), BF16), pltpu.VMEM((t, RET_DK), BF16), pltpu.VMEM((t, RET_DV), BF16),
                        pltpu.VMEM((t, RET_DV), BF16), pltpu.VMEM((t, RET_DK), F32), pltpu.VMEM((t, RET_DV), F32)],
        name=name, compiler_params=_cp())(h, h, h, h, c2, s2, lgt, o, dy)


def _rot_b(x, cb, shi, slo):
    return x * cb + pltpu.roll(x, ROPE_DIMS // 2, 1) * shi + pltpu.roll(x, LANES - ROPE_DIMS // 2, 1) * slo


def _rot_b_t(dy, cb, shi, slo):
    return dy * cb + pltpu.roll(dy * shi, LANES - ROPE_DIMS // 2, 1) + pltpu.roll(dy * slo, ROPE_DIMS // 2, 1)


def _dil_specs(t):
    base = (2 * RET_QK_W + 2 * RET_V_W) // LANES
    npair = DIL_W // LANES
    q = pl.BlockSpec((t, LANES), lambda p: (0, base + p))
    k = pl.BlockSpec((t, LANES), lambda p: (0, base + npair + p))
    v = pl.BlockSpec((t, LANES), lambda p: (0, base + 2 * npair + p))
    tab = pl.BlockSpec((t, LANES), lambda p: (0, 0))
    strip = pl.BlockSpec((ATT_BLK, t), lambda p: (0, 0))
    pair = pl.BlockSpec((t, LANES), lambda p: (0, p))
    return q, k, v, tab, strip, pair


def _dil_fwd(h, cb, shi, slo, strip, *, name):
    t = h.shape[0]
    nblk = t // ATT_BLK
    scale = DIL_HD ** -0.5

    def body(q_ref, k_ref, v_ref, cb_ref, shi_ref, slo_ref, st_ref, o_ref, yb_ref, lse_ref, qs, ks, vs):
        cb_, shi_, slo_ = cb_ref[...], shi_ref[...], slo_ref[...]
        lane = lax.broadcasted_iota(jnp.int32, (t, LANES), 1)
        qr = _rot_b(q_ref[...], cb_, shi_, slo_) * scale
        qs[0] = jnp.where(lane < DIL_HD, qr, 0.0).astype(BF16)
        qs[1] = jnp.where(lane >= DIL_HD, qr, 0.0).astype(BF16)
        ks[...] = _rot_b(k_ref[...], cb_, shi_, slo_).astype(BF16)
        vs[...] = v_ref[...].astype(BF16)
        lane_b = lax.broadcasted_iota(jnp.int32, (ATT_BLK, LANES), 1)
        for i in range(nblk):
            w = (i + 1) * ATT_BLK
            rows = pl.ds(i * ATT_BLK, ATT_BLK)
            logc = st_ref[:, t - w:t]
            outs, lses = [], []
            for hd in range(2):
                s = _nt(qs[hd, rows, :], ks[0:w, :]) + logc
                m = jnp.max(s, -1, keepdims=True)
                p = jnp.exp(s - m)
                l = jnp.sum(p, -1, keepdims=True)
                outs.append(_nn(p.astype(BF16), vs[0:w, :]) / l)
                lses.append(m + jnp.log(l))
            o = jnp.where(lane_b < DIL_HD, outs[0], outs[1])
            o_ref[rows, :] = o
            yb_ref[rows, :] = o.astype(BF16)
            lse_ref[rows, :] = jnp.where(lane_b < DIL_HD, lses[0], lses[1])

    q, k, v, tab, strip_spec, pair = _dil_specs(t)
    return pl.pallas_call(
        body, grid=(DIL_W // LANES,), in_specs=[q, k, v, tab, tab, tab, strip_spec], out_specs=[pair, pair, pair],
        out_shape=[jax.ShapeDtypeStruct((t, DIL_W), F32), jax.ShapeDtypeStruct((t, DIL_W), BF16),
                   jax.ShapeDtypeStruct((t, DIL_W), F32)],
        scratch_shapes=[pltpu.VMEM((2, t, LANES), BF16), pltpu.VMEM((t, LANES), BF16), pltpu.VMEM((t, LANES), BF16)],
        name=name, compiler_params=_cp())(h, h, h, cb, shi, slo, strip)


def _dil_bwd(h, cb, shi, slo, strip, o, lse, dy, *, name):
    t = h.shape[0]
    nblk = t // ATT_BLK
    scale = DIL_HD ** -0.5

    def body(q_ref, k_ref, v_ref, cb_ref, shi_ref, slo_ref, st_ref, o_ref, lse_ref, dy_ref,
             dq_ref, dk_ref, dv_ref, qs, ks, vs, dos, dls, dka, dva):
        cb_, shi_, slo_ = cb_ref[...], shi_ref[...], slo_ref[...]
        lane = lax.broadcasted_iota(jnp.int32, (t, LANES), 1)
        qr = _rot_b(q_ref[...], cb_, shi_, slo_) * scale
        qs[0] = jnp.where(lane < DIL_HD, qr, 0.0).astype(BF16)
        qs[1] = jnp.where(lane >= DIL_HD, qr, 0.0).astype(BF16)
        ks[...] = _rot_b(k_ref[...], cb_, shi_, slo_).astype(BF16)
        vs[...] = v_ref[...].astype(BF16)
        do = dy_ref[...]
        prod = do * o_ref[...]
        d0 = jnp.sum(jnp.where(lane < DIL_HD, prod, 0.0), -1, keepdims=True)
        d1 = jnp.sum(jnp.where(lane >= DIL_HD, prod, 0.0), -1, keepdims=True)
        dls[...] = jnp.where(lane < DIL_HD, d0, d1)
        dos[0] = jnp.where(lane < DIL_HD, do, 0.0).astype(BF16)
        dos[1] = jnp.where(lane >= DIL_HD, do, 0.0).astype(BF16)
        dka[...] = jnp.zeros_like(dka)
        dva[...] = jnp.zeros_like(dva)
        lane_b = lax.broadcasted_iota(jnp.int32, (ATT_BLK, LANES), 1)
        for i in range(nblk):
            w = (i + 1) * ATT_BLK
            rows = pl.ds(i * ATT_BLK, ATT_BLK)
            logc = st_ref[:, t - w:t]
            dqs = []
            for hd in range(2):
                col = hd * DIL_HD
                qh = qs[hd, rows, :]
                doh = dos[hd, rows, :]
                lse_h = lse_ref[rows, col:col + 1]
                dl_h = dls[rows, col:col + 1]
                p = jnp.exp(_nt(qh, ks[0:w, :]) + logc - lse_h)
                dp = _nt(doh, vs[0:w, :])
                ds = (p * (dp - dl_h)).astype(BF16)
                dqs.append(_nn(ds, ks[0:w, :]))
                dka[0:w, :] += _tn(ds, qh)
                dva[0:w, :] += _tn(p.astype(BF16), doh)
            dq = jnp.where(lane_b < DIL_HD, dqs[0], dqs[1]) * scale
            dq_ref[rows, :] = _rot_b_t(dq, cb_ref[rows, :], shi_ref[rows, :], slo_ref[rows, :]).astype(BF16)
        dk_ref[...] = _rot_b_t(dka[...], cb_, shi_, slo_).astype(BF16)
        dv_ref[...] = dva[...].astype(BF16)

    q, k, v, tab, strip_spec, pair = _dil_specs(t)
    dy_spec = pl.BlockSpec((t, LANES), lambda p: (0, RET_V_W // LANES + p))
    return pl.pallas_call(
        body, grid=(DIL_W // LANES,), in_specs=[q, k, v, tab, tab, tab, strip_spec, pair, pair, dy_spec],
        out_specs=[pair, pair, pair],
        out_shape=[jax.ShapeDtypeStruct((t, DIL_W), BF16)] * 3,
        scratch_shapes=[pltpu.VMEM((2, t, LANES), BF16), pltpu.VMEM((t, LANES), BF16), pltpu.VMEM((t, LANES), BF16),
                        pltpu.VMEM((2, t, LANES), BF16), pltpu.VMEM((t, LANES), F32),
                        pltpu.VMEM((t, LANES), F32), pltpu.VMEM((t, LANES), F32)],
        name=name, compiler_params=_cp())(h, h, h, cb, shi, slo, strip, o, lse, dy)


def _gdn_prep_fwd(h, cw, *, name):
    t = h.shape[0]
    qscale = GDN_DK ** -0.5

    def body(hq_ref, hk_ref, hv_ref, wq_ref, wk_ref, wv_ref, q_ref, k_ref, v_ref):
        row = lax.broadcasted_iota(jnp.int32, (t, GDN_DK), 0)
        sq = _silu(_dwconv(hq_ref[...], wq_ref, row))
        sk = _silu(_dwconv(hk_ref[...], wk_ref, row))
        q_ref[0] = sq * lax.rsqrt(jnp.sum(sq * sq, -1, keepdims=True) + 1e-6) * qscale
        k_ref[0] = sk * lax.rsqrt(jnp.sum(sk * sk, -1, keepdims=True) + 1e-6)
        v_ref[0] = _silu(_dwconv(hv_ref[...], wv_ref, row))

    hs = lambda off: pl.BlockSpec((t, GDN_DK), lambda i: (0, i + off))
    ws = lambda off: pl.BlockSpec((GDN_CONV, GDN_DK), lambda i: (0, i + off))
    out = pl.BlockSpec((1, t, GDN_DK), lambda i: (i, 0, 0))
    return pl.pallas_call(
        body, grid=(GDN_HEADS,), in_specs=[hs(0), hs(8), hs(16), ws(0), ws(8), ws(16)], out_specs=[out, out, out],
        out_shape=[jax.ShapeDtypeStruct((GDN_HEADS, t, GDN_DK), F32)] * 3,
        name=name, compiler_params=_cp())(h, h, h, cw, cw, cw)


def _gdn_prep_bwd(h, cw, dq, dk, dv, *, name):
    t = h.shape[0]
    qscale = GDN_DK ** -0.5

    def body(hq_ref, hk_ref, hv_ref, wq_ref, wk_ref, wv_ref, dq_ref, dk_ref, dv_ref,
             dhq_ref, dhk_ref, dhv_ref, dwq_ref, dwk_ref, dwv_ref):
        row = lax.broadcasted_iota(jnp.int32, (t, GDN_DK), 0)

        def one(h_ref, w_ref, d_ref, dh_ref, dw_ref, norm, sc):
            u = h_ref[...]
            c = _dwconv(u, w_ref, row)
            d = d_ref[0]
            if norm:
                s = _silu(c)
                r = lax.rsqrt(jnp.sum(s * s, -1, keepdims=True) + 1e-6)
                n = s * r
                d = d * sc
                d = r * (d - n * jnp.sum(d * n, -1, keepdims=True))
            dc = d * _dsilu(c)
            dh_ref[...] = _dwconv_bwd(u, w_ref, dc, row, dw_ref).astype(BF16)

        one(hq_ref, wq_ref, dq_ref, dhq_ref, dwq_ref, True, qscale)
        one(hk_ref, wk_ref, dk_ref, dhk_ref, dwk_ref, True, 1.0)
        one(hv_ref, wv_ref, dv_ref, dhv_ref, dwv_ref, False, 1.0)

    hs = lambda off: pl.BlockSpec((t, GDN_DK), lambda i: (0, i + off))
    ws = lambda off: pl.BlockSpec((GDN_CONV, GDN_DK), lambda i: (0, i + off))
    hd = pl.BlockSpec((1, t, GDN_DK), lambda i: (i, 0, 0))
    return pl.pallas_call(
        body, grid=(GDN_HEADS,), in_specs=[hs(0), hs(8), hs(16), ws(0), ws(8), ws(16), hd, hd, hd],
        out_specs=[hs(0), hs(0), hs(0), ws(0), ws(0), ws(0)],
        out_shape=[jax.ShapeDtypeStruct((t, GDN_W), BF16)] * 3 + [jax.ShapeDtypeStruct((GDN_CONV, GDN_W), F32)] * 3,
        name=name, compiler_params=_cp())(h, h, h, cw, cw, cw, dq, dk, dv)


def _make_mm2(wide):
    def raw(a, b, dims):
        if wide:
            return lax.dot_general(a, b, (dims, ((), ())), precision=lax.Precision.HIGHEST, preferred_element_type=F32)
        return lax.dot_general(a.astype(BF16), b.astype(BF16), (dims, ((), ())), preferred_element_type=F32)

    @jax.custom_vjp
    def nn(a, b):
        return raw(a, b, ((1,), (0,)))

    @jax.custom_vjp
    def nt(a, b):
        return raw(a, b, ((1,), (1,)))

    @jax.custom_vjp
    def tn(a, b):
        return raw(a, b, ((0,), (0,)))

    nn.defvjp(lambda a, b: (nn(a, b), (a, b)), lambda r, g: (nt(g, r[1]), tn(r[0], g)))
    nt.defvjp(lambda a, b: (nt(a, b), (a, b)), lambda r, g: (nn(g, r[1]), tn(g, r[0])))
    tn.defvjp(lambda a, b: (tn(a, b), (a, b)), lambda r, g: (nt(r[1], g), nn(r[0], g)))
    return nn, nt, tn


_NN, _NT, _TN = _make_mm2(False)
_NNW, _NTW, _TNW = _make_mm2(True)


def _square_masks(c):
    ri = lax.broadcasted_iota(jnp.int32, (c, c), 0)
    ci = lax.broadcasted_iota(jnp.int32, (c, c), 1)
    return ri >= ci, ri > ci, ri == ci


def _cumsum_rows(m):
    tri, _, _ = _square_masks(m.shape[0])
    return _NNW(tri.astype(F32), m)


def _transpose_sq(m):
    _, _, eye = _square_masks(m.shape[0])
    return _NTW(eye.astype(F32), m)


@jax.custom_vjp
def _inv_unit_lower(l):
    c = l.shape[0]
    _, _, eye = _square_masks(c)
    p = -l
    t = eye.astype(F32) + p
    for _ in range(int(math.log2(c)) - 1):
        p = _NNW(p, p)
        t = t + _NNW(t, p)
    return t


def _inv_fwd(l):
    t = _inv_unit_lower(l)
    return t, t


def _inv_bwd(t, dt):
    return (-_NTW(_TNW(t, dt), t),)


_inv_unit_lower.defvjp(_inv_fwd, _inv_bwd)


@jax.custom_vjp
def _inv_known(l, t):
    return t


_inv_known.defvjp(lambda l, t: (t, t), lambda t, dt: (_inv_bwd(t, dt)[0], jnp.zeros_like(t)))


def _softplus(x):
    return jnp.maximum(x, 0.0) + jnp.log1p(jnp.exp(-jnp.abs(x)))


def _gdn_chunk(q, k, v, braw, araw, alog, dtb, state, inv=None):
    c = q.shape[0]
    dv = v.shape[1]
    tri, strict, _ = _square_masks(c)
    beta = _sig(braw)
    g = -jnp.exp(alog) * _softplus(araw + dtb)
    gcm = _cumsum_rows(g * jnp.ones((c, c), F32))
    gct = _transpose_sq(gcm)
    decay = jnp.where(tri, jnp.exp(jnp.where(tri, gcm - gct, 0.0)), 0.0)
    gc = jnp.sum(gcm, 1, keepdims=True) * (1.0 / c)
    glast = jnp.sum(g, 0, keepdims=True)
    egc = jnp.exp(gc)
    kb = k * beta
    low = jnp.where(strict, _NT(kb, k) * decay, 0.0)
    tm = _inv_unit_lower(low) if inv is None else _inv_known(low, inv)
    sol = _NNW(tm, jnp.concatenate([v * beta, kb * egc], 1))
    u, w = sol[:, :dv], sol[:, dv:]
    attn = jnp.where(tri, _NT(q, k) * decay, 0.0)
    k_dec = k * jnp.exp(glast - gc)
    q_dec = q * egc
    v_new = u - _NN(w, state)
    o = _NN(q_dec, state) + _NN(attn, v_new)
    new_state = state * jnp.exp(glast) + _TN(k_dec, v_new)
    return o, new_state, tm


def _gdn_specs(t, rev):
    nch = t // GDN_CHUNK
    cm = (lambda n: nch - 1 - n) if rev else (lambda n: n)
    tok = pl.BlockSpec((GDN_HEADS, GDN_CHUNK, GDN_DK), lambda n: (0, cm(n), 0))
    par = pl.BlockSpec((GDN_HEADS, 1, LANES), lambda n: (0, 0, 0))
    st = pl.BlockSpec((GDN_HEADS, 1, GDN_DK, GDN_DV), lambda n: (0, cm(n), 0, 0))
    inv = pl.BlockSpec((GDN_HEADS, GDN_CHUNK, GDN_CHUNK), lambda n: (0, cm(n), 0))
    sc = pl.BlockSpec((GDN_CHUNK, LANES), lambda n: (cm(n), 4 * GDN_W // LANES))
    return tok, par, st, inv, sc


def _head_columns(sc_ref, first):
    return jnp.stack([sc_ref[:, first + hh:first + hh + 1] for hh in range(GDN_HEADS)])


def _gdn_core_fwd(q, k, v, h, alog, dtb, *, name):
    t = q.shape[1]
    nch = t // GDN_CHUNK

    def body(q_ref, k_ref, v_ref, sc_ref, al_ref, dt_ref, o_ref, st_ref, inv_ref, state):
        @pl.when(pl.program_id(0) == 0)
        def _():
            state[...] = jnp.zeros_like(state)

        s0 = state[...]
        st_ref[:, 0] = s0
        o, s1, tm = jax.vmap(_gdn_chunk)(q_ref[...], k_ref[...], v_ref[...], _head_columns(sc_ref, 0),
                                         _head_columns(sc_ref, GDN_HEADS), al_ref[:, :, 0:1], dt_ref[:, :, 0:1], s0)
        o_ref[...] = o
        inv_ref[...] = tm
        state[...] = s1

    tok, par, st, inv, sc = _gdn_specs(t, False)
    return pl.pallas_call(
        body, grid=(nch,), in_specs=[tok, tok, tok, sc, par, par], out_specs=[tok, st, inv],
        out_shape=[jax.ShapeDtypeStruct((GDN_HEADS, t, GDN_DV), F32),
                   jax.ShapeDtypeStruct((GDN_HEADS, nch, GDN_DK, GDN_DV), F32),
                   jax.ShapeDtypeStruct((GDN_HEADS, t, GDN_CHUNK), F32)],
        scratch_shapes=[pltpu.VMEM((GDN_HEADS, GDN_DK, GDN_DV), F32)],
        name=name, compiler_params=_cp())(q, k, v, h, alog, dtb)


def _gdn_core_bwd(q, k, v, h, alog, dtb, states, invs, do, *, name):
    t = q.shape[1]
    nch = t // GDN_CHUNK

    def body(q_ref, k_ref, v_ref, sc_ref, al_ref, dt_ref, st_ref, inv_ref, do_ref,
             dq_ref, dk_ref, dv_ref, dsc_ref, dal_ref, ddt_ref, dstate):
        @pl.when(pl.program_id(0) == 0)
        def _():
            dstate[...] = jnp.zeros_like(dstate)
            dal_ref[...] = jnp.zeros_like(dal_ref)
            ddt_ref[...] = jnp.zeros_like(ddt_ref)

        args = (q_ref[...], k_ref[...], v_ref[...], _head_columns(sc_ref, 0), _head_columns(sc_ref, GDN_HEADS),
                al_ref[:, :, 0:1], dt_ref[:, :, 0:1], st_ref[:, 0])
        tm = inv_ref[...]

        def chunk(*a):
            return jax.vmap(_gdn_chunk)(*a, tm)[:2]

        _, pull = jax.vjp(chunk, *args)
        dq, dk, dv, dbr, dar, dal, ddt, ds = pull((do_ref[...], dstate[...]))
        dq_ref[...] = dq
        dk_ref[...] = dk
        dv_ref[...] = dv
        lane = lax.broadcasted_iota(jnp.int32, (GDN_CHUNK, LANES), 1)
        dsc = jnp.zeros((GDN_CHUNK, LANES), F32)
        for hh in range(GDN_HEADS):
            dsc = jnp.where(lane == hh, dbr[hh], dsc)
            dsc = jnp.where(lane == GDN_HEADS + hh, dar[hh], dsc)
        dsc_ref[...] = dsc
        dal_ref[...] += dal + jnp.zeros((GDN_HEADS, 1, LANES), F32)
        ddt_ref[...] += ddt + jnp.zeros((GDN_HEADS, 1, LANES), F32)
        dstate[...] = ds

    tok, par, st, inv, sc = _gdn_specs(t, True)
    tokshape = jax.ShapeDtypeStruct((GDN_HEADS, t, GDN_DK), F32)
    parshape = jax.ShapeDtypeStruct((GDN_HEADS, 1, LANES), F32)
    nch_map = pl.BlockSpec((GDN_CHUNK, LANES), lambda n: (nch - 1 - n, 0))
    return pl.pallas_call(
        body, grid=(nch,), in_specs=[tok, tok, tok, sc, par, par, st, inv, tok],
        out_specs=[tok, tok, tok, nch_map, par, par],
        out_shape=[tokshape] * 3 + [jax.ShapeDtypeStruct((t, LANES), F32)] + [parshape] * 2,
        scratch_shapes=[pltpu.VMEM((GDN_HEADS, GDN_DK, GDN_DV), F32)],
        name=name, compiler_params=_cp())(q, k, v, h, alog, dtb, states, invs, do)


GDN_ROWS = 2048


def _gdn_post_fwd(o, h, nw, *, name):
    t = o.shape[1]

    def body(o_ref, g_ref, nw_ref, y_ref):
        oo = o_ref[0]
        r = lax.rsqrt(jnp.mean(oo * oo, -1, keepdims=True) + EPS)
        y_ref[...] = (oo * r * nw_ref[...] * _silu(g_ref[...])).astype(BF16)

    return pl.pallas_call(
        body, grid=(GDN_HEADS, t // GDN_ROWS),
        in_specs=[pl.BlockSpec((1, GDN_ROWS, GDN_DV), lambda hh, i: (hh, i, 0)),
                  pl.BlockSpec((GDN_ROWS, GDN_DV), lambda hh, i: (i, 3 * GDN_HEADS + hh)),
                  pl.BlockSpec((1, GDN_DV), lambda hh, i: (0, 0))],
        out_specs=pl.BlockSpec((GDN_ROWS, GDN_DV), lambda hh, i: (i, hh)),
        out_shape=jax.ShapeDtypeStruct((t, GDN_W), BF16), name=name, compiler_params=_cp())(o, h, nw)


def _gdn_post_bwd(o, h, nw, dy, *, name):
    t = o.shape[1]

    def body(o_ref, g_ref, nw_ref, dy_ref, do_ref, dg_ref, dnw_ref):
        oo, gg, nw_, dy_ = o_ref[0], g_ref[...], nw_ref[...], dy_ref[...]
        r = lax.rsqrt(jnp.mean(oo * oo, -1, keepdims=True) + EPS)
        n = oo * r
        sg = _silu(gg)
        dg_ref[...] = (dy_ * n * nw_ * _dsilu(gg)).astype(BF16)
        dn = dy_ * sg * nw_
        do_ref[0] = r * (dn - n * jnp.mean(dn * n, -1, keepdims=True))

        @pl.when((pl.program_id(0) == 0) & (pl.program_id(1) == 0))
        def _():
            dnw_ref[...] = jnp.zeros_like(dnw_ref)

        dnw_ref[...] += jnp.sum(dy_ * sg * n, 0, keepdims=True)

    return pl.pallas_call(
        body, grid=(GDN_HEADS, t // GDN_ROWS),
        in_specs=[pl.BlockSpec((1, GDN_ROWS, GDN_DV), lambda hh, i: (hh, i, 0)),
                  pl.BlockSpec((GDN_ROWS, GDN_DV), lambda hh, i: (i, 3 * GDN_HEADS + hh)),
                  pl.BlockSpec((1, GDN_DV), lambda hh, i: (0, 0)),
                  pl.BlockSpec((GDN_ROWS, GDN_DV), lambda hh, i: (i, hh))],
        out_specs=[pl.BlockSpec((1, GDN_ROWS, GDN_DV), lambda hh, i: (hh, i, 0)),
                   pl.BlockSpec((GDN_ROWS, GDN_DV), lambda hh, i: (i, hh)),
                   pl.BlockSpec((1, GDN_DV), lambda hh, i: (0, 0))],
        out_shape=[jax.ShapeDtypeStruct((GDN_HEADS, t, GDN_DV), F32), jax.ShapeDtypeStruct((t, GDN_W), BF16),
                   jax.ShapeDtypeStruct((1, GDN_DV), F32)],
        name=name, compiler_params=_cp())(o, h, nw, dy)


def _tables(positions):
    pos = positions.astype(F32)[:, None]
    half = RET_DK // 2
    inv = jnp.power(RET_THETA, -jnp.arange(half, dtype=F32) * 2.0 / RET_DK)
    ang = pos * inv
    cos, sin = jnp.cos(ang), jnp.sin(ang)
    c2a = jnp.concatenate([cos, cos], 1)
    s2a = jnp.concatenate([-sin, sin], 1)
    hb = ROPE_DIMS // 2
    invb = jnp.power(ROPE_THETA, -jnp.arange(hb, dtype=F32) * 2.0 / ROPE_DIMS)
    angb = pos * invb
    cosb, sinb = jnp.cos(angb), jnp.sin(angb)
    t = pos.shape[0]
    ones = jnp.ones((t, DIL_HD - ROPE_DIMS), F32)
    zeros = jnp.zeros((t, DIL_HD - ROPE_DIMS), F32)
    z8 = jnp.zeros((t, hb), F32)
    cb = jnp.concatenate([cosb, cosb, ones] * 2, 1)
    shi = jnp.concatenate([z8, sinb, zeros] * 2, 1)
    slo = jnp.concatenate([-sinb, z8, zeros] * 2, 1)
    lg = jnp.log1p(-jnp.power(2.0, -5.0 - jnp.arange(RET_HEADS, dtype=F32)))
    lgt = jnp.broadcast_to(lg[:, None, None], (RET_HEADS, 1, LANES))
    delta = jnp.arange(ATT_BLK, dtype=jnp.int32)[:, None] + (SEQ - ATT_BLK) - jnp.arange(SEQ, dtype=jnp.int32)[None, :]
    cnt = jnp.zeros(delta.shape, F32)
    for (w, d) in DIL_PAIRS:
        cnt = cnt + ((delta >= 0) & (delta <= w) & (delta % d == 0)).astype(F32)
    strip = jnp.where(cnt > 0, jnp.log(jnp.maximum(cnt, 1.0)), NEG)
    return c2a, s2a, cb, shi, slo, lgt, strip


def _local_step(x, tables, target, get_w, mid, put_g, small):
    c2a, s2a, cb, shi, slo, lgt, strip = tables
    t = x.shape[0]
    saved = []
    xf = x
    xb = x.astype(BF16)
    for layer in range(DEPTH):
        j = layer // 2
        L = f"L{layer}_"
        W, dep = get_w(layer, "mixer", xb)
        rec = {"x": xf, "xb": xb}
        if layer % 2 == 0:
            h = _mm(xb, W["in_t"], tb=True, name=L + "ev_in", dep=dep)
            ro, ya = _ret_fwd(h, c2a, s2a, lgt, name=L + "ret_fwd")
            do_, yb, lse = _dil_fwd(h, cb, shi, slo, strip, name=L + "dil_fwd")
            y = [ya, yb]
            rec.update(h=h, ro=ro, dil_o=do_, lse=lse, y=y)
        else:
            h = _mm(xb, W["in_t"], tb=True, name=L + "od_in", dep=dep)
            cw = W["conv"]
            q, k, v = _gdn_prep_fwd(h, cw, name=L + "gdn_prep")
            alog = jnp.broadcast_to(small["od_a_log"][j][:, None, None], (GDN_HEADS, 1, LANES))
            dtb = jnp.broadcast_to(small["od_dt_bias"][j][:, None, None], (GDN_HEADS, 1, LANES))
            o, states, invs = _gdn_core_fwd(q, k, v, h, alog, dtb, name=L + "gdn_fwd")
            nw = small["od_norm_w"][j][None, :]
            y = [_gdn_post_fwd(o, h, nw, name=L + "gdn_post")]
            rec.update(h=h, q=q, k=k, v=v, alog=alog, dtb=dtb, states=states, invs=invs, o=o, y=y, nw=nw, cw=cw)
        z1, x1, x1b = _mm_ln_fwd(y, W["out"], xf, small["ln1_g"][layer][None], small["ln1_b"][layer][None],
                                 name=L + "out_ln1", dep=mid(layer, "mixer", y[0]))
        rec["Wm"] = W
        W, dep = get_w(layer, "ffn", x1b)
        rec["Wf"] = W
        fcw = W["fconv"]
        fcb = small["ffn_conv_b"][layer][None]
        ug, uv, a = _ffn_up_mid(x1b, W["up_t"], fcw, fcb, name=L + "ffn_up_mid", dep=dep)
        z2, x2, x2b = _mm_ln_fwd([a], W["down"], x1, small["ln2_g"][layer][None], small["ln2_b"][layer][None],
                                 name=L + "down_ln2", dep=mid(layer, "ffn", a))
        rec.update(z1=z1, x1b=x1b, ug=ug, uv=uv, a=a, z2=z2, fcw=fcw, fcb=fcb)
        saved.append(rec)
        xf, xb = x2, x2b

    dy, lossv = _loss_head(xf, target, name="loss_head")
    loss = lossv[0, 0]

    gS = {n: [None] * small[n].shape[0] for n in small}
    below = None
    for layer in reversed(range(DEPTH)):
        j = layer // 2
        L = f"L{layer}_"
        rec = saved[layer]
        Wm, Wf = rec["Wm"], rec["Wf"]
        g = {}
        if below is None:
            dz2, dz2b, dg2, db2 = _ln_bwd(rec["z2"], small["ln2_g"][layer][None], dy, None, name=L + "ln2_bwd")
        else:
            dz2, dz2b, dg2, db2 = _mm_ln_bwd(below[0], below[1], rec["z2"], small["ln2_g"][layer][None], below[2],
                                             name=L + "ln2_bwd", dep=below[3])
        gS["ln2_g"][layer], gS["ln2_b"][layer] = dg2[0], db2[0]
        g["down"] = _mm(rec["a"], dz2b, ta=True, name=L + "ffn_down_dw", out_dtype=BF16)
        du, dcw, dcb = _ffn_mid_bwd(rec["ug"], rec["uv"], rec["fcw"], rec["fcb"], dz2b, Wf["down"], name=L + "ffn_mid_bwd")
        g["fconv"] = dcw.astype(BF16)
        gS["ffn_conv_b"][layer] = dcb[0]
        g["up_t"] = _mm_tn_parts_piped(du, rec["x1b"], name=L + "ffn_up_dw")
        dep = put_g(layer, "ffn", g)
        dz1, dz1b, dg1, db1 = _mm_ln_bwd(du, Wf["up_t"], rec["z1"], small["ln1_g"][layer][None], dz2,
                                         name=L + "ln1_bwd", dep=dep)
        gS["ln1_g"][layer], gS["ln1_b"][layer] = dg1[0], db1[0]
        g = {}
        if layer % 2 == 0:
            g["out"] = _mm_tn_parts(rec["y"], dz1b, name=L + "ev_out_dw")
            dyy = _mm(dz1b, Wm["out"], tb=True, name=L + "ev_out_dx")
            dqa, dka, dva, dga = _ret_bwd(rec["h"], c2a, s2a, lgt, rec["ro"], dyy, name=L + "ret_bwd")
            dqb, dkb, dvb = _dil_bwd(rec["h"], cb, shi, slo, strip, rec["dil_o"], rec["lse"], dyy, name=L + "dil_bwd")
            dh = [dqa, dka, dva, dga, dqb, dkb, dvb]
            g["in_t"] = _mm_tn_parts(dh, rec["xb"], name=L + "ev_in_dw")
            dep = put_g(layer, "mixer", g)
        else:
            g["out"] = _mm(rec["y"][0], dz1b, ta=True, name=L + "od_out_dw", out_dtype=BF16)
            dyy = _mm(dz1b, Wm["out"], tb=True, name=L + "od_out_dx")
            do, dgate, dnw = _gdn_post_bwd(rec["o"], rec["h"], rec["nw"], dyy, name=L + "gdn_post_bwd")
            gS["od_norm_w"][j] = dnw[0]
            dq, dk, dv, dsc, dal, ddt = _gdn_core_bwd(
                rec["q"], rec["k"], rec["v"], rec["h"], rec["alog"], rec["dtb"], rec["states"], rec["invs"], do,
                name=L + "gdn_bwd")
            gS["od_a_log"][j] = dal[:, 0, 0]
            gS["od_dt_bias"][j] = ddt[:, 0, 0]
            dhq, dhk, dhv, dwq, dwk, dwv = _gdn_prep_bwd(rec["h"], rec["cw"], dq, dk, dv, name=L + "gdn_prep_bwd")
            g["conv"] = jnp.concatenate([dwq, dwk, dwv], 1).astype(BF16)
            dh = [dhq, dhk, dhv, dgate, dsc.astype(BF16)]
            g["in_t"] = (_mm_tn_parts(dh[:4], rec["xb"], name=L + "od_in_dw"),
                         _mm(dh[4], rec["xb"], ta=True, name=L + "od_in_dw_logits", out_dtype=BF16))
            dep = put_g(layer, "mixer", g)
        below = (dh, Wm["in_t"], dz1, dep)
    grad_x = _axpy(_mm(jnp.concatenate(below[0], 1), below[1], name="L0_in_dx", dep=below[3]), below[2], name="grad_x")
    gS = {n: jnp.stack(v) for n, v in gS.items()}
    return loss, grad_x, gS


HBM = pl.BlockSpec(memory_space=pltpu.HBM)


def _me():
    return lax.axis_index("x"), lax.axis_index("y"), lax.axis_index("c")


def _my_index():
    x, y, c = _me()
    return 4 * x + 2 * y + c


SEM = pl.BlockSpec(memory_space=pltpu.SEMAPHORE)
ANY = pl.BlockSpec(memory_space=pl.ANY)
PLANS = {"scatter": (1, 2, 3, 4, 5, 6, 7), "spread": (1, 2, 4, 6), "relay": (2, 4, 6), "all": (1, 2, 3, 4, 5, 6, 7)}
ONE_SOURCE = ("spread", "all")
SIBLING = 1


def _peer(kk):
    x, y, c = _me()
    return x ^ (kk >> 2), y ^ ((kk >> 1) & 1), c ^ (kk & 1)


def _peer_index(kk):
    px, py, pc = _peer(kk)
    return 4 * px + 2 * py + pc


def _job_copies(mode, srcs, lands, send_sems, recv_sems, incoming):
    myid = _my_index()
    plan = PLANS[mode]
    out = []
    for a in range(len(lands)):
        for idx, kk in enumerate(plan):
            if mode == "relay":
                to, src = _peer(SIBLING), lands[a].at[_peer_index(kk)]
                slot_there, slot_here = _peer_index(kk), _peer_index(kk ^ SIBLING)
            else:
                to, src = _peer(kk), (srcs[a] if mode in ONE_SOURCE else srcs[a].at[_peer_index(kk)])
                slot_there, slot_here = myid, _peer_index(kk)
            sem = a * len(plan) + idx
            out.append(pltpu.make_async_remote_copy(
                src_ref=src, dst_ref=lands[a].at[slot_here if incoming else slot_there],
                send_sem=send_sems.at[sem], recv_sem=recv_sems.at[sem], device_id=to, device_id_type=MESH))
    return out


def _split_jobs(jobs, arrays):
    out, o = [], 0
    for (_, srcs, lands) in jobs:
        out.append((arrays[o:o + len(srcs)], arrays[o + len(srcs):o + len(srcs) + len(lands)]))
        o += len(srcs) + len(lands)
    return out


def _exchange_start(jobs, after, *, name):
    jobs = [(mode, list(srcs), [lax.empty((N_DEV, *s.shape) if mode in ONE_SOURCE else s.shape, s.dtype) for s in srcs]
             if lands is None else list(lands)) for (mode, srcs, lands) in jobs]
    flat = [a for (_, srcs, lands) in jobs for a in (*srcs, *lands)]
    n, nj = len(flat), len(jobs)
    nsem = [len(PLANS[mode]) * len(lands) for (mode, _, lands) in jobs]

    def body(*refs):
        o = n + (0 if after is None else 1)
        sems, token = refs[o:o + 2 * nj], refs[o + 2 * nj + n]
        for ji, ((mode, _, _), (src, land)) in enumerate(zip(jobs, _split_jobs(jobs, refs[:n]))):
            for cp in _job_copies(mode, src, land, sems[2 * ji], sems[2 * ji + 1], False):
                cp.start()
        token[...] = jnp.zeros_like(token)

    outs = pl.pallas_call(
        body, name=name,
        out_shape=(*[pltpu.SemaphoreType.DMA((ns,)) for ns in nsem for _ in range(2)],
                   *[pltpu.HBM(a.shape, a.dtype) for a in flat], jax.ShapeDtypeStruct((8, LANES), F32)),
        in_specs=[HBM] * n + ([] if after is None else [ANY]),
        out_specs=(*[SEM] * (2 * nj), *[HBM] * n, pl.BlockSpec(memory_space=pltpu.VMEM)),
        input_output_aliases={i: 2 * nj + i for i in range(n)},
        compiler_params=pltpu.CompilerParams(has_side_effects=pltpu.SideEffectType.DATAFLOW_SIDE_EFFECTING),
    )(*[pltpu.with_memory_space_constraint(a, pltpu.HBM) for a in flat], *([] if after is None else [after]))
    thru = _split_jobs(jobs, list(outs[2 * nj:2 * nj + n]))
    started = [(mode, outs[2 * ji], outs[2 * ji + 1], src, land) for ji, ((mode, _, _), (src, land)) in enumerate(zip(jobs, thru))]
    return started, outs[2 * nj + n]


def _exchange_wait(started, after, *, name):
    jobs = [(mode, srcs, lands) for (mode, _, _, srcs, lands) in started]
    flat = [a for (_, srcs, lands) in jobs for a in (*srcs, *lands)]
    n, nj = len(flat), len(jobs)

    def body(*refs):
        sems = refs[n:n + 2 * nj]
        for ji, ((mode, _, _), (src, land)) in enumerate(zip(jobs, _split_jobs(jobs, refs[:n]))):
            for cp in _job_copies(mode, src, land, sems[2 * ji], sems[2 * ji + 1], True):
                cp.wait_send()
                cp.wait_recv()

    outs = pl.pallas_call(
        body, name=name, out_shape=tuple(pltpu.HBM(a.shape, a.dtype) for a in flat),
        in_specs=[HBM] * n + [SEM] * (2 * nj) + [ANY], out_specs=tuple([HBM] * n),
        input_output_aliases={i: i for i in range(n)},
        compiler_params=pltpu.CompilerParams(has_side_effects=pltpu.SideEffectType.DATAFLOW_SIDE_EFFECTING),
    )(*flat, *[s for (_, ss, rs, _, _) in started for s in (ss, rs)], after)
    return _split_jobs(jobs, list(outs))


def _sum8(land, stack, j, depth, *, name):
    _, rr, cc = land.shape
    tr = _row_tile(rr)

    def body(l_ref, *rest):
        o_ref = rest[-1]
        acc = l_ref[0].astype(F32)
        for d in range(1, N_DEV):
            acc = acc + l_ref[d].astype(F32)
        o_ref[0] = acc

    prev = [] if stack is None else [stack]
    return pl.pallas_call(
        body, grid=(rr // tr,),
        in_specs=[pl.BlockSpec((N_DEV, tr, cc), lambda i: (0, i, 0))] + [pl.BlockSpec(memory_space=pl.ANY)] * len(prev),
        out_specs=pl.BlockSpec((1, tr, cc), lambda i: (j, i, 0)), out_shape=jax.ShapeDtypeStruct((depth, rr, cc), F32),
        input_output_aliases={1: 0} if prev else {}, name=name, compiler_params=_cp())(land, *prev)


def _row_tile(rr):
    for cand in (512, 384, 256, 192, 176, 128, 64, 32, 16, 8):
        if rr % cand == 0:
            return cand
    return rr


def _adam_math(w, g, m, v):
    m = ADAM_B1 * m + (1.0 - ADAM_B1) * g
    v = ADAM_B2 * v + (1.0 - ADAM_B2) * (g * g)
    m_hat = m / (1.0 - ADAM_B1 ** ADAM_STEP)
    v_hat = v / (1.0 - ADAM_B2 ** ADAM_STEP)
    delta = -ADAM_LR * (m_hat / (jnp.sqrt(v_hat) + ADAM_EPS) + ADAM_WD * w)
    return delta, m, v


def _adamw_sharded(w, m, v, g, *, name):
    ll, rr, cc = w.shape
    tr = _row_tile(rr)

    def body(w_ref, m_ref, v_ref, g_ref, d_ref, nm_ref, nv_ref):
        d, nm, nv = _adam_math(w_ref[...], g_ref[...], m_ref[...], v_ref[...])
        d_ref[...] = d
        nm_ref[...] = nm
        nv_ref[...] = nv

    blk = pl.BlockSpec((1, tr, cc), lambda l, i: (l, i, 0))
    sh = jax.ShapeDtypeStruct((ll, rr, cc), F32)
    return pl.pallas_call(
        body, grid=(ll, rr // tr), in_specs=[blk] * 4, out_specs=[blk] * 3, out_shape=[sh] * 3,
        name=name, compiler_params=_cp())(w, m, v, g)


def _adamw_small(w, m, v, gall, *, name):
    rr = w.shape[0]

    def body(w_ref, m_ref, v_ref, g_ref, go_ref, d_ref, nm_ref, nv_ref):
        g = g_ref[0]
        for kk in range(1, N_DEV):
            g = g + g_ref[kk]
        d, nm, nv = _adam_math(w_ref[...], g, m_ref[...], v_ref[...])
        go_ref[...] = g
        d_ref[...] = d
        nm_ref[...] = nm
        nv_ref[...] = nv

    sh = jax.ShapeDtypeStruct((rr, LANES), F32)
    return pl.pallas_call(body, out_shape=[sh] * 4, name=name, compiler_params=_cp())(w, m, v, gall)


SHARDED = ("ev_w_in", "ev_w_out", "od_w_in", "od_conv_w", "od_w_out", "ffn_w_up", "ffn_conv_w", "ffn_w_down")
SMALL = ("od_a_log", "od_dt_bias", "od_norm_w", "ffn_conv_b", "ln1_g", "ln1_b", "ln2_g", "ln2_b")
ALL_W = ("ev_w_in", "ev_w_out", "od_w_in", "od_conv_w", "od_a_log", "od_dt_bias", "od_norm_w", "od_w_out",
         "ffn_w_up", "ffn_conv_w", "ffn_conv_b", "ffn_w_down", "ln1_g", "ln1_b", "ln2_g", "ln2_b")


def _layer_items(layer):
    j = layer // 2
    if layer % 2 == 0:
        mixer = [("in_t", "ev_w_in", j, "colT"), ("out", "ev_w_out", j, "row")]
    else:
        mixer = [("in_t", "od_w_in", j, "colT"), ("conv", "od_conv_w", j, "colsmall"), ("out", "od_w_out", j, "row")]
    return mixer + [("up_t", "ffn_w_up", layer, "colT"), ("fconv", "ffn_conv_w", layer, "colsmall"),
                    ("down", "ffn_w_down", layer, "row")]


OD_SHARD = OD_IN // N_DEV
OD_SHARD_PAD = OD_IN_PAD // N_DEV


def _od_pack(g, *, name):
    d = g.shape[-1]

    def body(g_ref, o_ref):
        for n in range(N_DEV):
            o_ref[OD_SHARD * n:OD_SHARD * (n + 1), :] = g_ref[n, 0:OD_SHARD, :]
        o_ref[OD_IN:OD_IN_PAD, :] = jnp.zeros((OD_IN_PAD - OD_IN, d), g.dtype)

    return pl.pallas_call(body, out_shape=jax.ShapeDtypeStruct((OD_IN_PAD, d), g.dtype), name=name,
                          compiler_params=_cp())(g)


def _od_unpack(main, tail, *, name):
    d = main.shape[-1]
    split = main.shape[0]

    def body(m_ref, t_ref, o_ref):
        for n in range(N_DEV):
            lo, hi = OD_SHARD * n, OD_SHARD * (n + 1)
            from_main = min(hi, split) - lo
            o_ref[n, 0:from_main, :] = m_ref[lo:lo + from_main, :]
            if hi > split:
                o_ref[n, from_main:OD_SHARD, :] = t_ref[0:hi - split, :]
            o_ref[n, OD_SHARD:OD_SHARD_PAD, :] = jnp.zeros((OD_SHARD_PAD - OD_SHARD, d), main.dtype)

    return pl.pallas_call(body, out_shape=jax.ShapeDtypeStruct((N_DEV, OD_SHARD_PAD, d), main.dtype), name=name,
                          compiler_params=_cp())(main, tail)


def _to_send(kind, name, w, j):
    if kind == "colT":
        s = w[j].T.astype(BF16)
        return jnp.pad(s, ((0, OD_SHARD_PAD - OD_SHARD), (0, 0))) if name == "od_w_in" else s
    return w[j].astype(BF16) if kind == "row" else w[j]


def _from_gather(kind, name, g, tag):
    if kind == "colsmall":
        return jnp.transpose(g, (1, 0, 2)).reshape(g.shape[1], -1)
    if name == "od_w_in":
        return _od_pack(g, name=tag + "_pack")
    return g.reshape(-1, g.shape[-1])


def _by_owner(kind, name, gfull, tag):
    if kind == "colsmall":
        kk, c8 = gfull.shape
        return jnp.transpose(gfull.reshape(kk, N_DEV, c8 // N_DEV), (1, 0, 2))
    if name == "od_w_in":
        return _od_unpack(*gfull, name=tag + "_unpack")
    return gfull.reshape(N_DEV, gfull.shape[0] // N_DEV, gfull.shape[1])


def _pack_small(d):
    flat = jnp.concatenate([d[n].reshape(-1) for n in SMALL])
    pad = (-flat.shape[0]) % (8 * LANES)
    return jnp.pad(flat, (0, pad)).reshape(-1, LANES)


def _unpack_small(packed, like):
    flat = packed.reshape(-1)
    out, off = {}, 0
    for n in SMALL:
        sz = int(np.prod(like[n].shape))
        out[n] = flat[off:off + sz].reshape(like[n].shape)
        off += sz
    return out


def kernel(x, positions, ev_w_in, ev_w_out, od_w_in, od_conv_w, od_a_log, od_dt_bias, od_norm_w, od_w_out, ffn_w_up, ffn_conv_w, ffn_conv_b, ffn_w_down, ln1_g, ln1_b, ln2_g, ln2_b, loss_target, m_ev_w_in, m_ev_w_out, m_od_w_in, m_od_conv_w, m_od_a_log, m_od_dt_bias, m_od_norm_w, m_od_w_out, m_ffn_w_up, m_ffn_conv_w, m_ffn_conv_b, m_ffn_w_down, m_ln1_g, m_ln1_b, m_ln2_g, m_ln2_b, v_ev_w_in, v_ev_w_out, v_od_w_in, v_od_conv_w, v_od_a_log, v_od_dt_bias, v_od_norm_w, v_od_w_out, v_ffn_w_up, v_ffn_conv_w, v_ffn_conv_b, v_ffn_w_down, v_ln1_g, v_ln1_b, v_ln2_g, v_ln2_b):
    w = dict(ev_w_in=ev_w_in, ev_w_out=ev_w_out, od_w_in=od_w_in, od_conv_w=od_conv_w, od_a_log=od_a_log,
             od_dt_bias=od_dt_bias, od_norm_w=od_norm_w, od_w_out=od_w_out, ffn_w_up=ffn_w_up, ffn_conv_w=ffn_conv_w,
             ffn_conv_b=ffn_conv_b, ffn_w_down=ffn_w_down, ln1_g=ln1_g, ln1_b=ln1_b, ln2_g=ln2_g, ln2_b=ln2_b)
    mom = dict(ev_w_in=m_ev_w_in, ev_w_out=m_ev_w_out, od_w_in=m_od_w_in, od_conv_w=m_od_conv_w, od_a_log=m_od_a_log,
               od_dt_bias=m_od_dt_bias, od_norm_w=m_od_norm_w, od_w_out=m_od_w_out, ffn_w_up=m_ffn_w_up,
               ffn_conv_w=m_ffn_conv_w, ffn_conv_b=m_ffn_conv_b, ffn_w_down=m_ffn_w_down, ln1_g=m_ln1_g,
               ln1_b=m_ln1_b, ln2_g=m_ln2_g, ln2_b=m_ln2_b)
    var = dict(ev_w_in=v_ev_w_in, ev_w_out=v_ev_w_out, od_w_in=v_od_w_in, od_conv_w=v_od_conv_w, od_a_log=v_od_a_log,
               od_dt_bias=v_od_dt_bias, od_norm_w=v_od_norm_w, od_w_out=v_od_w_out, ffn_w_up=v_ffn_w_up,
               ffn_conv_w=v_ffn_conv_w, ffn_conv_b=v_ffn_conv_b, ffn_w_down=v_ffn_w_down, ln1_g=v_ln1_g,
               ln1_b=v_ln1_b, ln2_g=v_ln2_g, ln2_b=v_ln2_b)

    myid = _my_index()
    small = {n: w[n] for n in SMALL}
    groups = [(layer, part) for layer in range(DEPTH) for part in ("mixer", "ffn")]

    def group_items(gi):
        layer, part = groups[gi]
        its = _layer_items(layer)
        return its[:-3] if part == "mixer" else its[-3:]

    level1, level2 = {}, {}

    def spread_job(gi):
        return ("spread", [_to_send(kind, n, w[n], j) for (_, n, j, kind) in group_items(gi)], None)

    def relay(gi, after, name):
        (srcs, lands), = _exchange_wait([level1.pop(gi)], after, name=name + "_wait")
        more = [spread_job(gi + 1)] if gi + 1 < len(groups) else []
        started, token = _exchange_start([("relay", [], lands)] + more, None, name=name + "_start")
        level2[gi] = (started[0], srcs)
        if more:
            level1[gi + 1] = started[1]
        return token

    def get_w(layer, part, after):
        gi = groups.index((layer, part))
        started, srcs = level2.pop(gi)
        (_, lands), = _exchange_wait([started], after, name=f"gather{gi}_wait")
        lands = [lax.dynamic_update_index_in_dim(l, s, myid, 0) for l, s in zip(lands, srcs)]
        return {key: _from_gather(kind, n, l, f"L{layer}_{key}")
                for (key, n, _, kind), l in zip(group_items(gi), lands)}, None

    def mid(layer, part, after):
        gi = groups.index((layer, part)) + 1
        return relay(gi, after, f"gather{gi}_relay") if gi < len(groups) else None

    landed = {}
    pending = []

    def scatter_finish(after):
        started, gi = pending.pop()
        (srcs, lands), = _exchange_wait([started], after, name=f"scatter{gi}_wait")
        for (key, _, _, _), l, s in zip(group_items(gi), lands, srcs):
            own = lax.dynamic_index_in_dim(s, myid, 0, keepdims=False)
            landed[(groups[gi][0], key)] = lax.dynamic_update_index_in_dim(l, own, myid, 0)

    def put_g(layer, part, g):
        gi = groups.index((layer, part))
        srcs = [_by_owner(kind, n, g[key], f"L{layer}_{key}") for (key, n, _, kind) in group_items(gi)]
        (started,), token = _exchange_start([("scatter", srcs, None)], None, name=f"scatter{gi}_start")
        if pending:
            scatter_finish(token)
        pending.append((started, gi))
        return token

    (level1[0],), token = _exchange_start([spread_job(0)], None, name="gather0_spread_start")
    tables = _tables(positions[0] + token[0, 0].astype(jnp.int32))
    relay(0, tables[-1], "gather0_relay")
    loss, grad_x, gS = _local_step(x[0], tables, loss_target[0], get_w, mid, put_g, small)
    loss = lax.psum(loss, ("x", "y", "c"))

    outs_g, outs_d, outs_m, outs_v = {}, {}, {}, {}
    where = {n: [None] * w[n].shape[0] for n in SHARDED}
    for layer in range(DEPTH):
        for (key, n, j, kind) in _layer_items(layer):
            where[n][j] = (layer, key, kind)

    def update(n):
        g = None
        for j, (layer, key, _) in enumerate(where[n]):
            g = _sum8(landed[(layer, key)], g, j, len(where[n]), name=f"L{layer}_{key}_sum")
        if n == "od_w_in":
            g = g[:, :OD_SHARD]
        if where[n][0][2] == "colT":
            tr = lambda a: jnp.swapaxes(a, 1, 2)
            d, nm, nv = _adamw_sharded(tr(w[n]), tr(mom[n]), tr(var[n]), g, name=f"adamw_{n}")
            outs_g[n], outs_d[n], outs_m[n], outs_v[n] = tr(g), tr(d), tr(nm), tr(nv)
        else:
            outs_g[n] = g
            outs_d[n], outs_m[n], outs_v[n] = _adamw_sharded(w[n], mom[n], var[n], g, name=f"adamw_{n}")

    (small_job,), _ = _exchange_start([("all", [_pack_small(gS)], None)], None, name="small_grads_start")
    last = {n for (_, n, _, _) in group_items(pending[0][1])}
    for n in SHARDED:
        if n not in last:
            update(n)
    scatter_finish(outs_d[[n for n in SHARDED if n not in last][-1]])
    for n in SHARDED:
        if n in last:
            update(n)
    ((mine,), (gall,)), = _exchange_wait([small_job], outs_d[[n for n in SHARDED if n in last][-1]], name="small_grads_wait")
    gall = lax.dynamic_update_index_in_dim(gall, mine, myid, 0)
    g, d, nm, nv = _adamw_small(_pack_small({n: w[n] for n in SMALL}), _pack_small({n: mom[n] for n in SMALL}),
                                _pack_small({n: var[n] for n in SMALL}), gall, name="adamw_small")
    for dst, packed in ((outs_g, g), (outs_d, d), (outs_m, nm), (outs_v, nv)):
        dst.update(_unpack_small(packed, {n: w[n] for n in SMALL}))

    return (loss, grad_x[None], *[outs_g[n] for n in ALL_W], *[outs_d[n] for n in ALL_W],
            *[outs_m[n] for n in ALL_W], *[outs_v[n] for n in ALL_W])
```
